```python
import jax
import jax.numpy as jnp
from jax import lax
import numpy as np

D_MODEL = 1024
BATCH = 8
SEQ = 16384
DEPTH = 4

N_MIXERS = 3
ROPE_THETA = 500000.0
EPS = 1e-6
BLOCK = 128

A_HEADS = 16
A_KV_HEADS = 4
A_HEAD_DIM = D_MODEL // A_HEADS
A_ROT_DIM = A_HEAD_DIM // 4
A_WINDOW = 128

B_CONV_WIDTH = 3

C_HEADS = 16
C_NOPE_DIM = 64
C_ROPE_DIM = 32
C_V_DIM = 64
C_Q_RANK = 384
C_KV_RANK = 256

D_FF = ((8 * D_MODEL + 767) // 768) * 256

N_LAYERS_A = (DEPTH + N_MIXERS - 1) // N_MIXERS
N_LAYERS_B = (DEPTH + N_MIXERS - 2) // N_MIXERS
N_LAYERS_C = DEPTH // N_MIXERS

kernel_name = 'hybrid_swa_sink_shortconv_mla_swiglu'


def rms_norm(x, g):
    xf = x.astype(jnp.float32)
    y = xf * lax.rsqrt(jnp.mean(xf * xf, axis=-1, keepdims=True) + EPS)
    return (y * g.astype(jnp.float32)).astype(x.dtype)


def rotate(x, pos):
    r = x.shape[-1]
    inv_freq = ROPE_THETA ** (-jnp.arange(0, r, 2, dtype=jnp.float32) / r)
    ang = pos.astype(jnp.float32)[:, :, None] * inv_freq
    cos = jnp.cos(ang)[:, :, None, :]
    sin = jnp.sin(ang)[:, :, None, :]
    x1, x2 = jnp.split(x.astype(jnp.float32), 2, axis=-1)
    out = jnp.concatenate([x1 * cos - x2 * sin, x2 * cos + x1 * sin], axis=-1)
    return out.astype(x.dtype)


def swa_sink_attention(h, pos, w_qkv, q_norm, k_norm, sinks, w_o):
    b, s, _ = h.shape
    hq, hkv, hd = A_HEADS, A_KV_HEADS, A_HEAD_DIM
    grp = hq // hkv
    nblk = s // BLOCK
    q, k, v = jnp.split(h @ w_qkv, [hq * hd, (hq + hkv) * hd], axis=-1)
    q = rms_norm(q.reshape(b, s, hq, hd), q_norm)
    k = rms_norm(k.reshape(b, s, hkv, hd), k_norm)
    v = v.reshape(b, s, hkv, hd)
    q = jnp.concatenate([rotate(q[..., :A_ROT_DIM], pos), q[..., A_ROT_DIM:]], axis=-1)
    k = jnp.concatenate([rotate(k[..., :A_ROT_DIM], pos), k[..., A_ROT_DIM:]], axis=-1)
    qb = q.reshape(b, nblk, BLOCK, hkv, grp, hd)
    kb = k.reshape(b, nblk, BLOCK, hkv, hd)
    vb = v.reshape(b, nblk, BLOCK, hkv, hd)
    pad = ((0, 0), (1, 0), (0, 0), (0, 0), (0, 0))
    kw = jnp.concatenate([jnp.pad(kb, pad)[:, :-1], kb], axis=2)
    vw = jnp.concatenate([jnp.pad(vb, pad)[:, :-1], vb], axis=2)
    scores = jnp.einsum('bnqkgd,bnjkd->bnkgqj', qb, kw).astype(jnp.float32) * (hd ** -0.5)
    qi = jnp.arange(BLOCK)[:, None]
    kj = jnp.arange(2 * BLOCK)[None, :]
    delta = qi + BLOCK - kj
    band = (delta >= 0) & (delta < A_WINDOW)
    has_prev = (jnp.arange(nblk) > 0)[:, None, None] | (kj >= BLOCK)[None]
    valid = band[None] & has_prev
    scores = jnp.where(valid[None, :, None, None], scores, -jnp.inf)
    sink = sinks.astype(jnp.float32).reshape(hkv, grp)[None, None, :, :, None, None]
    m = jnp.maximum(jnp.max(scores, axis=-1, keepdims=True), sink)
    p = jnp.exp(scores - m)
    p = p / (jnp.sum(p, axis=-1, keepdims=True) + jnp.exp(sink - m))
    o = jnp.einsum('bnkgqj,bnjkd->bnqkgd', p.astype(vw.dtype), vw)
    return o.reshape(b, s, hq * hd) @ w_o


def short_conv_mixer(h, w_in, conv_w, w_out):
    s = h.shape[1]
    b_gate, c_gate, u = jnp.split(h @ w_in, 3, axis=-1)
    z = c_gate * u
    zp = jnp.pad(z, ((0, 0), (B_CONV_WIDTH - 1, 0), (0, 0)))
    y = sum(conv_w[i] * zp[:, i:i + s] for i in range(B_CONV_WIDTH))
    return (b_gate * y) @ w_out


def mla_attention(h, pos, w_down, q_a_norm, kv_a_norm, w_q_up, w_kv_up, q_norm, k_norm, w_o):
    b, s, _ = h.shape
    nh = C_HEADS
    dqk = C_NOPE_DIM + C_ROPE_DIM
    nblk = s // BLOCK
    cq, ckv, k_rope = jnp.split(h @ w_down, [C_Q_RANK, C_Q_RANK + C_KV_RANK], axis=-1)
    cq = rms_norm(cq, q_a_norm)
    ckv = rms_norm(ckv, kv_a_norm)
    q = (cq @ w_q_up).reshape(b, s, nh, dqk)
    k_nope, v = jnp.split((ckv @ w_kv_up).reshape(b, s, nh, C_NOPE_DIM + C_V_DIM), [C_NOPE_DIM], axis=-1)
    k = jnp.concatenate([k_nope, jnp.broadcast_to(k_rope[:, :, None, :], (b, s, nh, C_ROPE_DIM))], axis=-1)
    q = rms_norm(q, q_norm)
    k = rms_norm(k, k_norm)
    q = jnp.concatenate([q[..., :C_NOPE_DIM], rotate(q[..., C_NOPE_DIM:], pos)], axis=-1)
    k = jnp.concatenate([k[..., :C_NOPE_DIM], rotate(k[..., C_NOPE_DIM:], pos)], axis=-1)
    scale = dqk ** -0.5
    kpos = jnp.arange(s)
    qblocks = jnp.moveaxis(q.reshape(b, nblk, BLOCK, nh, dqk), 1, 0)

    def block_attn(args):
        qblk, j = args
        sc = jnp.einsum('bqhd,bkhd->bhqk', qblk, k).astype(jnp.float32) * scale
        qpos = j * BLOCK + jnp.arange(BLOCK)
        sc = jnp.where(kpos[None, :] <= qpos[:, None], sc, -jnp.inf)
        p = jax.nn.softmax(sc, axis=-1)
        return jnp.einsum('bhqk,bkhd->bqhd', p.astype(v.dtype), v)

    o = lax.map(block_attn, (qblocks, jnp.arange(nblk)))
    o = jnp.moveaxis(o, 0, 1).reshape(b, s, nh * C_V_DIM)
    return o @ w_o


def swiglu_ffn(h, w_gate_up, w_down):
    gate, up = jnp.split(h @ w_gate_up, 2, axis=-1)
    return (jax.nn.silu(gate) * up) @ w_down


def _normal(key, shape, scale):
    return jax.random.normal(key, shape, jnp.float32) * scale


def _fwd_setup_inputs(seed: int = 0) -> dict:
    key = jax.random.key(seed)
    ks = jax.random.split(key, 24)
    d = D_MODEL
    out_scale = (2 * DEPTH) ** -0.5
    qkv_width = (A_HEADS + 2 * A_KV_HEADS) * A_HEAD_DIM
    dqk = C_NOPE_DIM + C_ROPE_DIM
    return {
        'x': _normal(ks[0], (BATCH, SEQ, d), 1.0),
        'positions': jax.random.randint(ks[1], (BATCH, 1), 0, 4096, dtype=jnp.int32) + jnp.arange(SEQ, dtype=jnp.int32)[None, :],
        'mix_norm': 1.0 + _normal(ks[2], (DEPTH, d), 0.02),
        'ffn_norm': 1.0 + _normal(ks[3], (DEPTH, d), 0.02),
        'a_w_qkv': _normal(ks[4], (N_LAYERS_A, d, qkv_width), d ** -0.5),
        'a_q_norm': 1.0 + _normal(ks[5], (N_LAYERS_A, A_HEAD_DIM), 0.02),
        'a_k_norm': 1.0 + _normal(ks[6], (N_LAYERS_A, A_HEAD_DIM), 0.02),
        'a_sinks': _normal(ks[7], (N_LAYERS_A, A_HEADS), 0.5),
        'a_w_o': _normal(ks[8], (N_LAYERS_A, A_HEADS * A_HEAD_DIM, d), (A_HEADS * A_HEAD_DIM) ** -0.5 * out_scale),
        'b_w_in': _normal(ks[9], (N_LAYERS_B, d, 3 * d), d ** -0.5),
        'b_conv_w': _normal(ks[10], (N_LAYERS_B, B_CONV_WIDTH, d), B_CONV_WIDTH ** -0.5),
        'b_w_out': _normal(ks[11], (N_LAYERS_B, d, d), d ** -0.5 * out_scale),
        'c_w_down': _normal(ks[12], (N_LAYERS_C, d, C_Q_RANK + C_KV_RANK + C_ROPE_DIM), d ** -0.5),
        'c_q_a_norm': 1.0 + _normal(ks[13], (N_LAYERS_C, C_Q_RANK), 0.02),
        'c_kv_a_norm': 1.0 + _normal(ks[14], (N_LAYERS_C, C_KV_RANK), 0.02),
        'c_w_q_up': _normal(ks[15], (N_LAYERS_C, C_Q_RANK, C_HEADS * dqk), C_Q_RANK ** -0.5),
        'c_w_kv_up': _normal(ks[16], (N_LAYERS_C, C_KV_RANK, C_HEADS * (C_NOPE_DIM + C_V_DIM)), C_KV_RANK ** -0.5),
        'c_q_norm': 1.0 + _normal(ks[17], (N_LAYERS_C, dqk), 0.02),
        'c_k_norm': 1.0 + _normal(ks[18], (N_LAYERS_C, dqk), 0.02),
        'c_w_o': _normal(ks[19], (N_LAYERS_C, C_HEADS * C_V_DIM, d), (C_HEADS * C_V_DIM) ** -0.5 * out_scale),
        'f_w_gate_up': _normal(ks[20], (DEPTH, d, 2 * D_FF), d ** -0.5),
        'f_w_down': _normal(ks[21], (DEPTH, D_FF, d), D_FF ** -0.5 * out_scale),
    }


def _fwd_reference(x, positions, mix_norm, ffn_norm, a_w_qkv, a_q_norm, a_k_norm, a_sinks, a_w_o, b_w_in, b_conv_w, b_w_out, c_w_down, c_q_a_norm, c_kv_a_norm, c_w_q_up, c_w_kv_up, c_q_norm, c_k_norm, c_w_o, f_w_gate_up, f_w_down):
    for i in range(DEPTH):
        kind = i % N_MIXERS
        j = i // N_MIXERS
        h = rms_norm(x, mix_norm[i])
        if kind == 0:
            y = swa_sink_attention(h, positions, a_w_qkv[j], a_q_norm[j], a_k_norm[j], a_sinks[j], a_w_o[j])
        elif kind == 1:
            y = short_conv_mixer(h, b_w_in[j], b_conv_w[j], b_w_out[j])
        else:
            y = mla_attention(h, positions, c_w_down[j], c_q_a_norm[j], c_kv_a_norm[j], c_w_q_up[j], c_w_kv_up[j], c_q_norm[j], c_k_norm[j], c_w_o[j])
        x = x + y
        h = rms_norm(x, ffn_norm[i])
        x = x + swiglu_ffn(h, f_w_gate_up[i], f_w_down[i])
    return x


import jax as _jax
import jax.numpy as _jnp

TWIN_FORMAT = 'train_step'
FWD_PARAMS = ['x', 'positions', 'mix_norm', 'ffn_norm', 'a_w_qkv', 'a_q_norm', 'a_k_norm', 'a_sinks', 'a_w_o', 'b_w_in', 'b_conv_w', 'b_w_out', 'c_w_down', 'c_q_a_norm', 'c_kv_a_norm', 'c_w_q_up', 'c_w_kv_up', 'c_q_norm', 'c_k_norm', 'c_w_o', 'f_w_gate_up', 'f_w_down']
TWIN_WEIGHTS = ['mix_norm', 'ffn_norm', 'a_w_qkv', 'a_q_norm', 'a_k_norm', 'a_sinks', 'a_w_o', 'b_w_in', 'b_conv_w', 'b_w_out', 'c_w_down', 'c_q_a_norm', 'c_kv_a_norm', 'c_w_q_up', 'c_w_kv_up', 'c_q_norm', 'c_k_norm', 'c_w_o', 'f_w_gate_up', 'f_w_down']
TWIN_DIFF_INPUT = 'x'
TWIN_INPUTS = ['x', 'positions', 'mix_norm', 'ffn_norm', 'a_w_qkv', 'a_q_norm', 'a_k_norm', 'a_sinks', 'a_w_o', 'b_w_in', 'b_conv_w', 'b_w_out', 'c_w_down', 'c_q_a_norm', 'c_kv_a_norm', 'c_w_q_up', 'c_w_kv_up', 'c_q_norm', 'c_k_norm', 'c_w_o', 'f_w_gate_up', 'f_w_down', 'loss_target', 'm_mix_norm', 'm_ffn_norm', 'm_a_w_qkv', 'm_a_q_norm', 'm_a_k_norm', 'm_a_sinks', 'm_a_w_o', 'm_b_w_in', 'm_b_conv_w', 'm_b_w_out', 'm_c_w_down', 'm_c_q_a_norm', 'm_c_kv_a_norm', 'm_c_w_q_up', 'm_c_w_kv_up', 'm_c_q_norm', 'm_c_k_norm', 'm_c_w_o', 'm_f_w_gate_up', 'm_f_w_down', 'v_mix_norm', 'v_ffn_norm', 'v_a_w_qkv', 'v_a_q_norm', 'v_a_k_norm', 'v_a_sinks', 'v_a_w_o', 'v_b_w_in', 'v_b_conv_w', 'v_b_w_out', 'v_c_w_down', 'v_c_q_a_norm', 'v_c_kv_a_norm', 'v_c_w_q_up', 'v_c_w_kv_up', 'v_c_q_norm', 'v_c_k_norm', 'v_c_w_o', 'v_f_w_gate_up', 'v_f_w_down']
TWIN_OUTPUTS = ['loss', 'grad_x', 'grad_mix_norm', 'grad_ffn_norm', 'grad_a_w_qkv', 'grad_a_q_norm', 'grad_a_k_norm', 'grad_a_sinks', 'grad_a_w_o', 'grad_b_w_in', 'grad_b_conv_w', 'grad_b_w_out', 'grad_c_w_down', 'grad_c_q_a_norm', 'grad_c_kv_a_norm', 'grad_c_w_q_up', 'grad_c_w_kv_up', 'grad_c_q_norm', 'grad_c_k_norm', 'grad_c_w_o', 'grad_f_w_gate_up', 'grad_f_w_down', 'delta_mix_norm', 'delta_ffn_norm', 'delta_a_w_qkv', 'delta_a_q_norm', 'delta_a_k_norm', 'delta_a_sinks', 'delta_a_w_o', 'delta_b_w_in', 'delta_b_conv_w', 'delta_b_w_out', 'delta_c_w_down', 'delta_c_q_a_norm', 'delta_c_kv_a_norm', 'delta_c_w_q_up', 'delta_c_w_kv_up', 'delta_c_q_norm', 'delta_c_k_norm', 'delta_c_w_o', 'delta_f_w_gate_up', 'delta_f_w_down', 'new_m_mix_norm', 'new_m_ffn_norm', 'new_m_a_w_qkv', 'new_m_a_q_norm', 'new_m_a_k_norm', 'new_m_a_sinks', 'new_m_a_w_o', 'new_m_b_w_in', 'new_m_b_conv_w', 'new_m_b_w_out', 'new_m_c_w_down', 'new_m_c_q_a_norm', 'new_m_c_kv_a_norm', 'new_m_c_w_q_up', 'new_m_c_w_kv_up', 'new_m_c_q_norm', 'new_m_c_k_norm', 'new_m_c_w_o', 'new_m_f_w_gate_up', 'new_m_f_w_down', 'new_v_mix_norm', 'new_v_ffn_norm', 'new_v_a_w_qkv', 'new_v_a_q_norm', 'new_v_a_k_norm', 'new_v_a_sinks', 'new_v_a_w_o', 'new_v_b_w_in', 'new_v_b_conv_w', 'new_v_b_w_out', 'new_v_c_w_down', 'new_v_c_q_a_norm', 'new_v_c_kv_a_norm', 'new_v_c_w_q_up', 'new_v_c_w_kv_up', 'new_v_c_q_norm', 'new_v_c_k_norm', 'new_v_c_w_o', 'new_v_f_w_gate_up', 'new_v_f_w_down']
TWIN_LEAF_KINDS = {'loss': 'loss', 'grad_x': 'grad_x', 'grad_mix_norm': 'grad_w', 'grad_ffn_norm': 'grad_w', 'grad_a_w_qkv': 'grad_w', 'grad_a_q_norm': 'grad_w', 'grad_a_k_norm': 'grad_w', 'grad_a_sinks': 'grad_w', 'grad_a_w_o': 'grad_w', 'grad_b_w_in': 'grad_w', 'grad_b_conv_w': 'grad_w', 'grad_b_w_out': 'grad_w', 'grad_c_w_down': 'grad_w', 'grad_c_q_a_norm': 'grad_w', 'grad_c_kv_a_norm': 'grad_w', 'grad_c_w_q_up': 'grad_w', 'grad_c_w_kv_up': 'grad_w', 'grad_c_q_norm': 'grad_w', 'grad_c_k_norm': 'grad_w', 'grad_c_w_o': 'grad_w', 'grad_f_w_gate_up': 'grad_w', 'grad_f_w_down': 'grad_w', 'delta_mix_norm': 'delta_w', 'delta_ffn_norm': 'delta_w', 'delta_a_w_qkv': 'delta_w', 'delta_a_q_norm': 'delta_w', 'delta_a_k_norm': 'delta_w', 'delta_a_sinks': 'delta_w', 'delta_a_w_o': 'delta_w', 'delta_b_w_in': 'delta_w', 'delta_b_conv_w': 'delta_w', 'delta_b_w_out': 'delta_w', 'delta_c_w_down': 'delta_w', 'delta_c_q_a_norm': 'delta_w', 'delta_c_kv_a_norm': 'delta_w', 'delta_c_w_q_up': 'delta_w', 'delta_c_w_kv_up': 'delta_w', 'delta_c_q_norm': 'delta_w', 'delta_c_k_norm': 'delta_w', 'delta_c_w_o': 'delta_w', 'delta_f_w_gate_up': 'delta_w', 'delta_f_w_down': 'delta_w', 'new_m_mix_norm': 'new_m', 'new_m_ffn_norm': 'new_m', 'new_m_a_w_qkv': 'new_m', 'new_m_a_q_norm': 'new_m', 'new_m_a_k_norm': 'new_m', 'new_m_a_sinks': 'new_m', 'new_m_a_w_o': 'new_m', 'new_m_b_w_in': 'new_m', 'new_m_b_conv_w': 'new_m', 'new_m_b_w_out': 'new_m', 'new_m_c_w_down': 'new_m', 'new_m_c_q_a_norm': 'new_m', 'new_m_c_kv_a_norm': 'new_m', 'new_m_c_w_q_up': 'new_m', 'new_m_c_w_kv_up': 'new_m', 'new_m_c_q_norm': 'new_m', 'new_m_c_k_norm': 'new_m', 'new_m_c_w_o': 'new_m', 'new_m_f_w_gate_up': 'new_m', 'new_m_f_w_down': 'new_m', 'new_v_mix_norm': 'new_v', 'new_v_ffn_norm': 'new_v', 'new_v_a_w_qkv': 'new_v', 'new_v_a_q_norm': 'new_v', 'new_v_a_k_norm': 'new_v', 'new_v_a_sinks': 'new_v', 'new_v_a_w_o': 'new_v', 'new_v_b_w_in': 'new_v', 'new_v_b_conv_w': 'new_v', 'new_v_b_w_out': 'new_v', 'new_v_c_w_down': 'new_v', 'new_v_c_q_a_norm': 'new_v', 'new_v_c_kv_a_norm': 'new_v', 'new_v_c_w_q_up': 'new_v', 'new_v_c_w_kv_up': 'new_v', 'new_v_c_q_norm': 'new_v', 'new_v_c_k_norm': 'new_v', 'new_v_c_w_o': 'new_v', 'new_v_f_w_gate_up': 'new_v', 'new_v_f_w_down': 'new_v'}


def _forward(args):
    return _fwd_reference(*[args[k] for k in FWD_PARAMS])


def _output_shape():
    def fwd():
        inp = _fwd_setup_inputs(0)
        return _fwd_reference(*[inp[k] for k in FWD_PARAMS])
    out = _jax.eval_shape(fwd)
    return out.shape, out.dtype

N_MICROBATCH = 1
ADAM_LR = 0.001
ADAM_B1 = 0.9
ADAM_B2 = 0.999
ADAM_EPS = 1e-08
ADAM_WD = 0.01
ADAM_STEP = 10
PER_EXAMPLE_BATCH_AXIS = {'x': 0, 'positions': 0, 'loss_target': 0}
SHARED_INPUTS = []
_WEIGHT_DTYPES = {'mix_norm': _jnp.float32, 'ffn_norm': _jnp.float32, 'a_w_qkv': _jnp.float32, 'a_q_norm': _jnp.float32, 'a_k_norm': _jnp.float32, 'a_sinks': _jnp.float32, 'a_w_o': _jnp.float32, 'b_w_in': _jnp.float32, 'b_conv_w': _jnp.float32, 'b_w_out': _jnp.float32, 'c_w_down': _jnp.float32, 'c_q_a_norm': _jnp.float32, 'c_kv_a_norm': _jnp.float32, 'c_w_q_up': _jnp.float32, 'c_w_kv_up': _jnp.float32, 'c_q_norm': _jnp.float32, 'c_k_norm': _jnp.float32, 'c_w_o': _jnp.float32, 'f_w_gate_up': _jnp.float32, 'f_w_down': _jnp.float32}
MOMENT_SCALE = {'mix_norm': 2.358513e+01, 'ffn_norm': 1.226504e+01, 'a_w_qkv': 7.129027e-02, 'a_q_norm': 2.438737e+00, 'a_k_norm': 2.434180e+00, 'a_sinks': 2.985096e-01, 'a_w_o': 1.610341e-01, 'b_w_in': 3.965380e-01, 'b_conv_w': 9.124284e+00, 'b_w_out': 1.058413e+00, 'c_w_down': 9.811731e-02, 'c_q_a_norm': 6.481212e-02, 'c_kv_a_norm': 4.545815e-01, 'c_w_q_up': 3.266042e-02, 'c_w_kv_up': 4.732441e-02, 'c_q_norm': 5.967225e-01, 'c_k_norm': 5.953182e-01, 'c_w_o': 1.577787e-01, 'f_w_gate_up': 1.045385e-01, 'f_w_down': 5.667247e-01}


def _to_microbatches(a, axis):
    t = _jnp.moveaxis(a, axis, 0)
    t = t.reshape((N_MICROBATCH, t.shape[0] // N_MICROBATCH) + t.shape[1:])
    return _jnp.moveaxis(t, 1, axis + 1)


def setup_inputs(seed: int = 0) -> dict:
    inp = _fwd_setup_inputs(seed)
    key = _jax.random.fold_in(_jax.random.key(seed), 7919)
    shape, _ = _output_shape()
    out = dict(inp)
    out["loss_target"] = _jax.random.normal(_jax.random.fold_in(key, 0), shape, _jnp.float32)
    for i, name in enumerate(TWIN_WEIGHTS):
        w = inp[name].astype(_jnp.float32)
        if MOMENT_SCALE is None:
            s = _jnp.sqrt(_jnp.mean(_jnp.square(w)) + 1e-30)
        else:
            s = MOMENT_SCALE[name]
        km, kv = _jax.random.split(_jax.random.fold_in(key, i + 1))
        out[name] = w
        out["m_" + name] = s * _jax.random.normal(km, w.shape, _jnp.float32)
        out["v_" + name] = (s * s) * _jax.random.uniform(kv, w.shape, _jnp.float32, 0.5, 1.5)
    if N_MICROBATCH > 1:
        for name, axis in PER_EXAMPLE_BATCH_AXIS.items():
            out[name] = _to_microbatches(out[name], axis)
    return {'x': out['x'], 'positions': out['positions'], 'mix_norm': out['mix_norm'], 'ffn_norm': out['ffn_norm'], 'a_w_qkv': out['a_w_qkv'], 'a_q_norm': out['a_q_norm'], 'a_k_norm': out['a_k_norm'], 'a_sinks': out['a_sinks'], 'a_w_o': out['a_w_o'], 'b_w_in': out['b_w_in'], 'b_conv_w': out['b_conv_w'], 'b_w_out': out['b_w_out'], 'c_w_down': out['c_w_down'], 'c_q_a_norm': out['c_q_a_norm'], 'c_kv_a_norm': out['c_kv_a_norm'], 'c_w_q_up': out['c_w_q_up'], 'c_w_kv_up': out['c_w_kv_up'], 'c_q_norm': out['c_q_norm'], 'c_k_norm': out['c_k_norm'], 'c_w_o': out['c_w_o'], 'f_w_gate_up': out['f_w_gate_up'], 'f_w_down': out['f_w_down'], 'loss_target': out['loss_target'], 'm_mix_norm': out['m_mix_norm'], 'm_ffn_norm': out['m_ffn_norm'], 'm_a_w_qkv': out['m_a_w_qkv'], 'm_a_q_norm': out['m_a_q_norm'], 'm_a_k_norm': out['m_a_k_norm'], 'm_a_sinks': out['m_a_sinks'], 'm_a_w_o': out['m_a_w_o'], 'm_b_w_in': out['m_b_w_in'], 'm_b_conv_w': out['m_b_conv_w'], 'm_b_w_out': out['m_b_w_out'], 'm_c_w_down': out['m_c_w_down'], 'm_c_q_a_norm': out['m_c_q_a_norm'], 'm_c_kv_a_norm': out['m_c_kv_a_norm'], 'm_c_w_q_up': out['m_c_w_q_up'], 'm_c_w_kv_up': out['m_c_w_kv_up'], 'm_c_q_norm': out['m_c_q_norm'], 'm_c_k_norm': out['m_c_k_norm'], 'm_c_w_o': out['m_c_w_o'], 'm_f_w_gate_up': out['m_f_w_gate_up'], 'm_f_w_down': out['m_f_w_down'], 'v_mix_norm': out['v_mix_norm'], 'v_ffn_norm': out['v_ffn_norm'], 'v_a_w_qkv': out['v_a_w_qkv'], 'v_a_q_norm': out['v_a_q_norm'], 'v_a_k_norm': out['v_a_k_norm'], 'v_a_sinks': out['v_a_sinks'], 'v_a_w_o': out['v_a_w_o'], 'v_b_w_in': out['v_b_w_in'], 'v_b_conv_w': out['v_b_conv_w'], 'v_b_w_out': out['v_b_w_out'], 'v_c_w_down': out['v_c_w_down'], 'v_c_q_a_norm': out['v_c_q_a_norm'], 'v_c_kv_a_norm': out['v_c_kv_a_norm'], 'v_c_w_q_up': out['v_c_w_q_up'], 'v_c_w_kv_up': out['v_c_w_kv_up'], 'v_c_q_norm': out['v_c_q_norm'], 'v_c_k_norm': out['v_c_k_norm'], 'v_c_w_o': out['v_c_w_o'], 'v_f_w_gate_up': out['v_f_w_gate_up'], 'v_f_w_down': out['v_f_w_down']}


def _loss(weights, diff, rest, loss_target):
    with _jax.named_scope("forward"):
        args = {**rest, TWIN_DIFF_INPUT: diff, **{k: w.astype(_WEIGHT_DTYPES[k]) for k, w in weights.items()}}
        y = _forward(args)
    with _jax.named_scope("loss_head"):
        err = _jnp.square(y.astype(_jnp.float32) - loss_target)
        return 0.5 * _jnp.sum(_jnp.mean(err, axis=-1)) if err.ndim else 0.5 * err


def _adamw(w, g, m, v):
    m = ADAM_B1 * m + (1.0 - ADAM_B1) * g
    v = ADAM_B2 * v + (1.0 - ADAM_B2) * _jnp.square(g)
    m_hat = m / (1.0 - ADAM_B1 ** ADAM_STEP)
    v_hat = v / (1.0 - ADAM_B2 ** ADAM_STEP)
    delta = -ADAM_LR * (m_hat / (_jnp.sqrt(v_hat) + ADAM_EPS) + ADAM_WD * w)
    return delta, m, v


def reference(x, positions, mix_norm, ffn_norm, a_w_qkv, a_q_norm, a_k_norm, a_sinks, a_w_o, b_w_in, b_conv_w, b_w_out, c_w_down, c_q_a_norm, c_kv_a_norm, c_w_q_up, c_w_kv_up, c_q_norm, c_k_norm, c_w_o, f_w_gate_up, f_w_down, loss_target, m_mix_norm, m_ffn_norm, m_a_w_qkv, m_a_q_norm, m_a_k_norm, m_a_sinks, m_a_w_o, m_b_w_in, m_b_conv_w, m_b_w_out, m_c_w_down, m_c_q_a_norm, m_c_kv_a_norm, m_c_w_q_up, m_c_w_kv_up, m_c_q_norm, m_c_k_norm, m_c_w_o, m_f_w_gate_up, m_f_w_down, v_mix_norm, v_ffn_norm, v_a_w_qkv, v_a_q_norm, v_a_k_norm, v_a_sinks, v_a_w_o, v_b_w_in, v_b_conv_w, v_b_w_out, v_c_w_down, v_c_q_a_norm, v_c_kv_a_norm, v_c_w_q_up, v_c_w_kv_up, v_c_q_norm, v_c_k_norm, v_c_w_o, v_f_w_gate_up, v_f_w_down):
    given = dict(x=x, positions=positions, mix_norm=mix_norm, ffn_norm=ffn_norm, a_w_qkv=a_w_qkv, a_q_norm=a_q_norm, a_k_norm=a_k_norm, a_sinks=a_sinks, a_w_o=a_w_o, b_w_in=b_w_in, b_conv_w=b_conv_w, b_w_out=b_w_out, c_w_down=c_w_down, c_q_a_norm=c_q_a_norm, c_kv_a_norm=c_kv_a_norm, c_w_q_up=c_w_q_up, c_w_kv_up=c_w_kv_up, c_q_norm=c_q_norm, c_k_norm=c_k_norm, c_w_o=c_w_o, f_w_gate_up=f_w_gate_up, f_w_down=f_w_down, loss_target=loss_target, m_mix_norm=m_mix_norm, m_ffn_norm=m_ffn_norm, m_a_w_qkv=m_a_w_qkv, m_a_q_norm=m_a_q_norm, m_a_k_norm=m_a_k_norm, m_a_sinks=m_a_sinks, m_a_w_o=m_a_w_o, m_b_w_in=m_b_w_in, m_b_conv_w=m_b_conv_w, m_b_w_out=m_b_w_out, m_c_w_down=m_c_w_down, m_c_q_a_norm=m_c_q_a_norm, m_c_kv_a_norm=m_c_kv_a_norm, m_c_w_q_up=m_c_w_q_up, m_c_w_kv_up=m_c_w_kv_up, m_c_q_norm=m_c_q_norm, m_c_k_norm=m_c_k_norm, m_c_w_o=m_c_w_o, m_f_w_gate_up=m_f_w_gate_up, m_f_w_down=m_f_w_down, v_mix_norm=v_mix_norm, v_ffn_norm=v_ffn_norm, v_a_w_qkv=v_a_w_qkv, v_a_q_norm=v_a_q_norm, v_a_k_norm=v_a_k_norm, v_a_sinks=v_a_sinks, v_a_w_o=v_a_w_o, v_b_w_in=v_b_w_in, v_b_conv_w=v_b_conv_w, v_b_w_out=v_b_w_out, v_c_w_down=v_c_w_down, v_c_q_a_norm=v_c_q_a_norm, v_c_kv_a_norm=v_c_kv_a_norm, v_c_w_q_up=v_c_w_q_up, v_c_w_kv_up=v_c_w_kv_up, v_c_q_norm=v_c_q_norm, v_c_k_norm=v_c_k_norm, v_c_w_o=v_c_w_o, v_f_w_gate_up=v_f_w_gate_up, v_f_w_down=v_f_w_down)
    weights = {n: given[n] for n in TWIN_WEIGHTS}
    shared = {n: given[n] for n in SHARED_INPUTS}
    per_example = {n: given[n] for n in ['x', 'positions']}
    grad_fn = _jax.value_and_grad(_loss, argnums=(0, 1))

    def one_microbatch(ex, loss_target):
        ex = dict(ex)
        diff = ex.pop(TWIN_DIFF_INPUT)
        return grad_fn(weights, diff, {**shared, **ex}, loss_target)

    if N_MICROBATCH == 1:
        loss, (grad_w, grad_x) = one_microbatch(per_example, given["loss_target"])
    else:
        def body(carry, xs):
            loss_sum, grad_sum = carry
            l_k, (gw_k, gx_k) = one_microbatch(xs[0], xs[1])
            with _jax.named_scope("update"):
                return (loss_sum + l_k, _jax.tree.map(_jnp.add, grad_sum, gw_k)), gx_k

        init = (_jnp.zeros((), _jnp.float32), _jax.tree.map(_jnp.zeros_like, weights))
        (loss, grad_w), grad_x = _jax.lax.scan(body, init, (per_example, given["loss_target"]))
    with _jax.named_scope("update"):
        delta_w, new_m, new_v = {}, {}, {}
        for n in TWIN_WEIGHTS:
            delta_w[n], new_m[n], new_v[n] = _adamw(weights[n], grad_w[n], given["m_" + n], given["v_" + n])
    return (loss, grad_x, *[grad_w[n] for n in TWIN_WEIGHTS], *[delta_w[n] for n in TWIN_WEIGHTS],
            *[new_m[n] for n in TWIN_WEIGHTS], *[new_v[n] for n in TWIN_WEIGHTS])
```

```python
import functools

import jax
import jax.numpy as jnp
from jax import lax
from jax.experimental import pallas as pl
from jax.experimental.pallas import tpu as pltpu

F32 = jnp.float32
BF = jnp.bfloat16

D_MODEL = 1024
DEPTH = 4
N_MIXERS = 3
ROPE_THETA = 500000.0
EPS = 1e-6
BLOCK = 128
LANES = 128

A_HEADS, A_KV_HEADS, A_HEAD_DIM = 16, 4, 64
A_ROT_DIM = A_HEAD_DIM // 4
A_GROUP = A_HEADS // A_KV_HEADS
C_HEADS, C_NOPE, C_ROPE, C_V, C_Q_RANK, C_KV_RANK = 16, 64, 32, 64, 384, 256
C_QK = C_NOPE + C_ROPE
D_FF = 2816

ADAM_LR, ADAM_B1, ADAM_B2, ADAM_EPS, ADAM_WD, ADAM_STEP = 0.001, 0.9, 0.999, 1e-08, 0.01, 10

N_CHIPS = 4
N_DEV = 8
MESH = pl.DeviceIdType.MESH

VMEM_LIMIT = 56 * 1024 * 1024
ROW_TILE = 512
PREP_TILE = 256
ATTN_TILE = 512
NEG = -1e30

WEIGHTS = ['mix_norm', 'ffn_norm', 'a_w_qkv', 'a_q_norm', 'a_k_norm', 'a_sinks', 'a_w_o', 'b_w_in', 'b_conv_w',
           'b_w_out', 'c_w_down', 'c_q_a_norm', 'c_kv_a_norm', 'c_w_q_up', 'c_w_kv_up', 'c_q_norm', 'c_k_norm',
           'c_w_o', 'f_w_gate_up', 'f_w_down']
BIG = {'a_w_qkv': 2, 'a_w_o': 1, 'b_w_in': 2, 'b_w_out': 1, 'c_w_down': 1, 'c_w_q_up': 2, 'c_w_kv_up': 2,
       'c_w_o': 1, 'f_w_gate_up': 2, 'f_w_down': 1}
SMALL_SHARDED = {'b_conv_w': 2, 'c_q_a_norm': 1, 'c_kv_a_norm': 1}
PACK_COLS = 1024
PACK_ROW_MULT = 256


def _cp(*sem):
    return pltpu.CompilerParams(dimension_semantics=sem, vmem_limit_bytes=VMEM_LIMIT)


def _tile(n, target, mult):
    if n <= target:
        return n
    t = (target // mult) * mult
    while t >= mult:
        if n % t == 0:
            return t
        t -= mult
    raise ValueError(f"no tile for {n}")


def _dot(a, b):
    return lax.dot_general(a, b, (((1,), (0,)), ((), ())), preferred_element_type=F32)


def _dot_nt(a, b):
    return lax.dot_general(a, b, (((1,), (1,)), ((), ())), preferred_element_type=F32)


def _dot_tn(a, b):
    return lax.dot_general(a, b, (((0,), (0,)), ((), ())), preferred_element_type=F32)


def _mm(pairs, *, out_dtype, name, res=None, trans_b=False):
    a0, b0, _ = pairs[0]
    m = a0.shape[0]
    n = b0.shape[0] if trans_b else b0.shape[1]
    tm = _tile(m, ROW_TILE, 8)
    tn = _tile(n, 1024, 128)
    n_pairs = len(pairs)
    has_res = res is not None

    def body(*refs):
        o_ref = refs[-1]
        acc = None
        for p in range(n_pairs):
            a = refs[2 * p][...].astype(BF)
            b = refs[2 * p + 1][...].astype(BF)
            d = _dot_nt(a, b) if trans_b else _dot(a, b)
            acc = d if acc is None else acc + d
        if has_res:
            acc = acc + refs[2 * n_pairs][...]
        o_ref[...] = acc.astype(out_dtype)

    in_specs, args = [], []
    for a, b, kblk in pairs:
        k = a.shape[1]
        in_specs.append(pl.BlockSpec((tm, k), lambda j, i: (i, 0)))
        if trans_b:
            in_specs.append(pl.BlockSpec((tn, k), functools.partial(lambda j, i, kb: (j, kb), kb=kblk)))
        else:
            in_specs.append(pl.BlockSpec((k, tn), lambda j, i: (0, j)))
        args += [a, b]
    if has_res:
        in_specs.append(pl.BlockSpec((tm, tn), lambda j, i: (i, j)))
        args.append(res)
    return pl.pallas_call(
        body, grid=(n // tn, m // tm), in_specs=in_specs,
        out_specs=pl.BlockSpec((tm, tn), lambda j, i: (i, j)),
        out_shape=jax.ShapeDtypeStruct((m, n), out_dtype),
        compiler_params=_cp("parallel", "parallel"), name=name)(*args)


def _mm_tn(a, b, *, name):
    m, k = a.shape
    n = b.shape[1]
    tm = _tile(m, ROW_TILE, 8)
    tk = _tile(k, 1408, 128)
    tn = _tile(n, 1408, 128)

    def body(a_ref, b_ref, o_ref):
        @pl.when(pl.program_id(2) == 0)
        def _():
            o_ref[...] = jnp.zeros_like(o_ref)

        o_ref[...] += _dot_tn(a_ref[...].astype(BF), b_ref[...].astype(BF))

    return pl.pallas_call(
        body, grid=(k // tk, n // tn, m // tm),
        in_specs=[pl.BlockSpec((tm, tk), lambda kk, j, i: (i, kk)),
                  pl.BlockSpec((tm, tn), lambda kk, j, i: (i, j))],
        out_specs=pl.BlockSpec((tk, tn), lambda kk, j, i: (kk, j)),
        out_shape=jax.ShapeDtypeStruct((k, n), F32),
        compiler_params=_cp("parallel", "parallel", "arbitrary"), name=name)(a, b)


def _rmsnorm_fwd(x, g, *, name):
    s, d = x.shape
    tm = _tile(s, ROW_TILE, 8)

    def body(x_ref, g_ref, h_ref):
        xv = x_ref[...]
        y = xv * lax.rsqrt(jnp.mean(xv * xv, axis=-1, keepdims=True) + EPS)
        h_ref[...] = (y * g_ref[...]).astype(BF)

    return pl.pallas_call(
        body, grid=(s // tm,),
        in_specs=[pl.BlockSpec((tm, d), lambda i: (i, 0)), pl.BlockSpec((1, d), lambda i: (0, 0))],
        out_specs=pl.BlockSpec((tm, d), lambda i: (i, 0)),
        out_shape=jax.ShapeDtypeStruct((s, d), BF),
        compiler_params=_cp("parallel"), name=name)(x, g.reshape(1, d))


def _rms_bwd_math(xv, g, dh, n):
    rstd = lax.rsqrt(jnp.sum(xv * xv, axis=-1, keepdims=True) * (1.0 / n) + EPS)
    xhat = xv * rstd
    dxh = dh * g
    dx = rstd * (dxh - xhat * (jnp.sum(dxh * xhat, axis=-1, keepdims=True) * (1.0 / n)))
    return dx, xhat


def _rmsnorm_bwd(x, g, dh, dres, *, name):
    s, d = x.shape
    tm = _tile(s, ROW_TILE, 8)

    def body(x_ref, g_ref, dh_ref, dres_ref, dx_ref, dg_ref):
        @pl.when(pl.program_id(0) == 0)
        def _():
            dg_ref[...] = jnp.zeros_like(dg_ref)

        dhv = dh_ref[...]
        dx, xhat = _rms_bwd_math(x_ref[...], g_ref[...], dhv, d)
        dx_ref[...] = dres_ref[...] + dx
        dg_ref[0:1, :] += jnp.sum(dhv * xhat, axis=0, keepdims=True)

    row = pl.BlockSpec((tm, d), lambda i: (i, 0))
    dx, dg = pl.pallas_call(
        body, grid=(s // tm,),
        in_specs=[row, pl.BlockSpec((1, d), lambda i: (0, 0)), row, row],
        out_specs=[row, pl.BlockSpec((8, d), lambda i: (0, 0))],
        out_shape=[jax.ShapeDtypeStruct((s, d), F32), jax.ShapeDtypeStruct((8, d), F32)],
        compiler_params=_cp("arbitrary"), name=name)(x, g.reshape(1, d), dh, dres)
    return dx, dg[0:1]


def _loss_fwd_bwd(y, target, *, name):
    s, d = y.shape
    tm = _tile(s, ROW_TILE, 8)

    def body(y_ref, t_ref, loss_ref, dy_ref):
        @pl.when(pl.program_id(0) == 0)
        def _():
            loss_ref[...] = jnp.zeros_like(loss_ref)

        err = y_ref[...] - t_ref[...]
        dy_ref[...] = err * (1.0 / d)
        loss_ref[...] += 0.5 * jnp.sum(jnp.sum(err * err, axis=-1, keepdims=True) * (1.0 / d))

    row = pl.BlockSpec((tm, d), lambda i: (i, 0))
    return pl.pallas_call(
        body, grid=(s // tm,), in_specs=[row, row],
        out_specs=[pl.BlockSpec((8, LANES), lambda i: (0, 0)), row],
        out_shape=[jax.ShapeDtypeStruct((8, LANES), F32), jax.ShapeDtypeStruct((s, d), F32)],
        compiler_params=_cp("arbitrary"), name=name)(y, target)


def _rope_rows():
    lane = jnp.arange(LANES)
    fa = ROPE_THETA ** (-jnp.arange(0, A_ROT_DIM, 2, dtype=F32) / A_ROT_DIM)
    fc = ROPE_THETA ** (-jnp.arange(0, C_ROPE, 2, dtype=F32) / C_ROPE)
    ha, hc = A_ROT_DIM // 2, C_ROPE // 2
    in_a = lane < A_ROT_DIM
    in_c = (lane >= C_NOPE) & (lane < C_QK)
    freq_a = jnp.where(in_a, fa[lane % ha], 0.0)
    freq_c = jnp.where(in_c, fc[(lane - C_NOPE) % hc], 0.0)
    sign_a = jnp.where(in_a, jnp.where(lane < ha, -1.0, 1.0), 0.0)
    sign_c = jnp.where(in_c, jnp.where(lane < C_NOPE + hc, -1.0, 1.0), 0.0)
    return jnp.stack([freq_a, sign_a, freq_c, sign_c] + [jnp.zeros(LANES)] * 4).astype(F32)


def _rope_tables(pos_col, *, name):
    s = pos_col.shape[0]
    tm = _tile(s, ROW_TILE, 8)

    def body(pos_ref, rows_ref, ca_ref, sa_ref, cc_ref, sc_ref):
        p = pos_ref[...].astype(F32)
        ang_a = p * rows_ref[0:1, :]
        ang_c = p * rows_ref[2:3, :]
        ca_ref[...] = jnp.cos(ang_a)
        sa_ref[...] = jnp.sin(ang_a) * rows_ref[1:2, :]
        cc_ref[...] = jnp.cos(ang_c)
        sc_ref[...] = jnp.sin(ang_c) * rows_ref[3:4, :]

    tab = pl.BlockSpec((tm, LANES), lambda i: (i, 0))
    return pl.pallas_call(
        body, grid=(s // tm,),
        in_specs=[pl.BlockSpec((tm, 1), lambda i: (i, 0)), pl.BlockSpec((8, LANES), lambda i: (0, 0))],
        out_specs=[tab] * 4, out_shape=[jax.ShapeDtypeStruct((s, LANES), F32)] * 4,
        compiler_params=_cp("parallel"), name=name)(pos_col, _rope_rows())


def _rope(xv, cos, sin, lo, half):
    lane = lax.broadcasted_iota(jnp.int32, xv.shape, 1)
    partner = jnp.where(lane < lo + half, pltpu.roll(xv, LANES - half, 1), pltpu.roll(xv, half, 1))
    return xv * cos + partner * sin


def _slot(ref, t):
    return ref[:, t * LANES:(t + 1) * LANES]


def _head_norm_rope(xv, g, cos, sin, n, lo, half):
    y = xv * lax.rsqrt(jnp.sum(xv * xv, axis=-1, keepdims=True) * (1.0 / n) + EPS) * g
    return _rope(y, cos, sin, lo, half)


def _head_norm_rope_bwd(xv, g, dy, cos, sin, n, lo, half):
    dyn = _rope(dy, cos, -sin, lo, half)
    dx, xhat = _rms_bwd_math(xv, g, dyn, n)
    return dx, jnp.sum(dyn * xhat, axis=0, keepdims=True)


def _prep_a_fwd(qkv, cos, sin, gq, gk, *, name):
    s = qkv.shape[0]
    tm = _tile(s, PREP_TILE, 8)
    nq, nkv = A_HEADS, A_KV_HEADS

    def body(x_ref, c_ref, s_ref, gq_ref, gk_ref, q_ref, k_ref, v_ref):
        cv, sv = c_ref[...], s_ref[...]
        for t in range(nq):
            q_ref[:, t * LANES:(t + 1) * LANES] = _head_norm_rope(
                _slot(x_ref, t), gq_ref[...], cv, sv, A_HEAD_DIM, 0, A_ROT_DIM // 2).astype(BF)
        for t in range(nkv):
            k_ref[:, t * LANES:(t + 1) * LANES] = _head_norm_rope(
                _slot(x_ref, nq + t), gk_ref[...], cv, sv, A_HEAD_DIM, 0, A_ROT_DIM // 2).astype(BF)
        v_ref[...] = x_ref[:, (nq + nkv) * LANES:].astype(BF)

    def rows(w):
        return pl.BlockSpec((tm, w), lambda i: (i, 0))

    vec = pl.BlockSpec((1, LANES), lambda i: (0, 0))
    return pl.pallas_call(
        body, grid=(s // tm,),
        in_specs=[rows(qkv.shape[1]), rows(LANES), rows(LANES), vec, vec],
        out_specs=[rows(nq * LANES), rows(nkv * LANES), rows(nkv * LANES)],
        out_shape=[jax.ShapeDtypeStruct((s, nq * LANES), BF), jax.ShapeDtypeStruct((s, nkv * LANES), BF),
                   jax.ShapeDtypeStruct((s, nkv * LANES), BF)],
        compiler_params=_cp("parallel"), name=name)(qkv, cos, sin, gq, gk)


def _prep_a_bwd(qkv, dq, dk, dv, cos, sin, gq, gk, *, name):
    s = qkv.shape[0]
    tm = _tile(s, PREP_TILE, 8)
    nq, nkv = A_HEADS, A_KV_HEADS

    def body(x_ref, dq_ref, dk_ref, dv_ref, c_ref, s_ref, gq_ref, gk_ref, dx_ref, dg_ref):
        @pl.when(pl.program_id(0) == 0)
        def _():
            dg_ref[...] = jnp.zeros_like(dg_ref)

        cv, sv = c_ref[...], s_ref[...]
        dgq = jnp.zeros((1, LANES), F32)
        dgk = jnp.zeros((1, LANES), F32)
        for t in range(nq):
            dx, dg = _head_norm_rope_bwd(_slot(x_ref, t), gq_ref[...], _slot(dq_ref, t), cv, sv,
                                         A_HEAD_DIM, 0, A_ROT_DIM // 2)
            dx_ref[:, t * LANES:(t + 1) * LANES] = dx.astype(BF)
            dgq = dgq + dg
        for t in range(nkv):
            dx, dg = _head_norm_rope_bwd(_slot(x_ref, nq + t), gk_ref[...], _slot(dk_ref, t), cv, sv,
                                         A_HEAD_DIM, 0, A_ROT_DIM // 2)
            dx_ref[:, (nq + t) * LANES:(nq + t + 1) * LANES] = dx.astype(BF)
            dgk = dgk + dg
        dx_ref[:, (nq + nkv) * LANES:] = dv_ref[...].astype(BF)
        dg_ref[0:1, :] += dgq
        dg_ref[1:2, :] += dgk

    def rows(w):
        return pl.BlockSpec((tm, w), lambda i: (i, 0))

    vec = pl.BlockSpec((1, LANES), lambda i: (0, 0))
    dx, dg = pl.pallas_call(
        body, grid=(s // tm,),
        in_specs=[rows(qkv.shape[1]), rows(nq * LANES), rows(nkv * LANES), rows(nkv * LANES), rows(LANES),
                  rows(LANES), vec, vec],
        out_specs=[rows(qkv.shape[1]), pl.BlockSpec((8, LANES), lambda i: (0, 0))],
        out_shape=[jax.ShapeDtypeStruct(qkv.shape, BF), jax.ShapeDtypeStruct((8, LANES), F32)],
        compiler_params=_cp("arbitrary"), name=name)(qkv, dq, dk, dv, cos, sin, gq, gk)
    return dx, dg[0:1], dg[1:2]


def _prep_c1_fwd(dn, gq, gkv, *, name):
    s = dn.shape[0]
    tm = _tile(s, ROW_TILE, 8)

    def body(x_ref, gq_ref, gkv_ref, cq_ref, ckv_ref):
        for lo, n, g_ref, o_ref in ((0, C_Q_RANK, gq_ref, cq_ref), (C_Q_RANK, C_KV_RANK, gkv_ref, ckv_ref)):
            xv = x_ref[:, lo:lo + n]
            y = xv * lax.rsqrt(jnp.mean(xv * xv, axis=-1, keepdims=True) + EPS)
            o_ref[...] = (y * g_ref[...]).astype(BF)

    def rows(w):
        return pl.BlockSpec((tm, w), lambda i: (i, 0))

    return pl.pallas_call(
        body, grid=(s // tm,),
        in_specs=[rows(dn.shape[1]), pl.BlockSpec((1, C_Q_RANK), lambda i: (0, 0)),
                  pl.BlockSpec((1, C_KV_RANK), lambda i: (0, 0))],
        out_specs=[rows(C_Q_RANK), rows(C_KV_RANK)],
        out_shape=[jax.ShapeDtypeStruct((s, C_Q_RANK), BF), jax.ShapeDtypeStruct((s, C_KV_RANK), BF)],
        compiler_params=_cp("parallel"), name=name)(dn, gq, gkv)


def _prep_c1_bwd(dn, dcq, dckv, dkr, gq, gkv, *, name):
    s = dn.shape[0]
    tm = _tile(s, ROW_TILE, 8)

    def body(x_ref, dcq_ref, dckv_ref, dkr_ref, gq_ref, gkv_ref, dx_ref, dgq_ref, dgkv_ref):
        @pl.when(pl.program_id(0) == 0)
        def _():
            dgq_ref[...] = jnp.zeros_like(dgq_ref)
            dgkv_ref[...] = jnp.zeros_like(dgkv_ref)

        for lo, n, g_ref, d_ref, dg_ref in ((0, C_Q_RANK, gq_ref, dcq_ref, dgq_ref),
                                            (C_Q_RANK, C_KV_RANK, gkv_ref, dckv_ref, dgkv_ref)):
            dv = d_ref[...]
            dx, xhat = _rms_bwd_math(x_ref[:, lo:lo + n], g_ref[...], dv, n)
            dx_ref[:, lo:lo + n] = dx.astype(BF)
            dg_ref[0:1, :] += jnp.sum(dv * xhat, axis=0, keepdims=True)
        dx_ref[:, C_Q_RANK + C_KV_RANK:] = dkr_ref[...].astype(BF)

    def rows(w):
        return pl.BlockSpec((tm, w), lambda i: (i, 0))

    def vec(w, r=1):
        return pl.BlockSpec((r, w), lambda i: (0, 0))

    dx, dgq, dgkv = pl.pallas_call(
        body, grid=(s // tm,),
        in_specs=[rows(dn.shape[1]), rows(C_Q_RANK), rows(C_KV_RANK), rows(LANES), vec(C_Q_RANK), vec(C_KV_RANK)],
        out_specs=[rows(dn.shape[1]), vec(C_Q_RANK, 8), vec(C_KV_RANK, 8)],
        out_shape=[jax.ShapeDtypeStruct(dn.shape, BF), jax.ShapeDtypeStruct((8, C_Q_RANK), F32),
                   jax.ShapeDtypeStruct((8, C_KV_RANK), F32)],
        compiler_params=_cp("arbitrary"), name=name)(dn, dcq, dckv, dkr, gq, gkv)
    return dx, dgq[0:1], dgkv[0:1]


def _prep_c2_fwd(qraw, knope, dn, cos, sin, gq, gk, *, name):
    s = qraw.shape[0]
    tm = _tile(s, PREP_TILE, 8)
    kr_blk = dn.shape[1] // LANES - 1

    def body(q_ref, kn_ref, kr_ref, c_ref, s_ref, gq_ref, gk_ref, qo_ref, ko_ref):
        cv, sv, kr = c_ref[...], s_ref[...], kr_ref[...]
        for t in range(C_HEADS):
            qo_ref[:, t * LANES:(t + 1) * LANES] = _head_norm_rope(
                _slot(q_ref, t), gq_ref[...], cv, sv, C_QK, C_NOPE, C_ROPE // 2).astype(BF)
            ko_ref[:, t * LANES:(t + 1) * LANES] = _head_norm_rope(
                _slot(kn_ref, t) + kr, gk_ref[...], cv, sv, C_QK, C_NOPE, C_ROPE // 2).astype(BF)

    def rows(w, blk=0):
        return pl.BlockSpec((tm, w), lambda i: (i, blk))

    vec = pl.BlockSpec((1, LANES), lambda i: (0, 0))
    w = C_HEADS * LANES
    return pl.pallas_call(
        body, grid=(s // tm,),
        in_specs=[rows(w), rows(w), rows(LANES, kr_blk), rows(LANES), rows(LANES), vec, vec],
        out_specs=[rows(w), rows(w)],
        out_shape=[jax.ShapeDtypeStruct((s, w), BF)] * 2,
        compiler_params=_cp("parallel"), name=name)(qraw, knope, dn, cos, sin, gq, gk)


def _prep_c2_bwd(qraw, knope, dn, dq, dk, cos, sin, gq, gk, *, name):
    s = qraw.shape[0]
    tm = _tile(s, PREP_TILE, 8)
    kr_blk = dn.shape[1] // LANES - 1

    def body(q_ref, kn_ref, kr_ref, dq_ref, dk_ref, c_ref, s_ref, gq_ref, gk_ref,
             dqo_ref, dkno_ref, dkr_ref, dg_ref):
        @pl.when(pl.program_id(0) == 0)
        def _():
            dg_ref[...] = jnp.zeros_like(dg_ref)

        cv, sv, kr = c_ref[...], s_ref[...], kr_ref[...]
        dgq = jnp.zeros((1, LANES), F32)
        dgk = jnp.zeros((1, LANES), F32)
        dkr = jnp.zeros((tm, LANES), F32)
        for t in range(C_HEADS):
            dx, dg = _head_norm_rope_bwd(_slot(q_ref, t), gq_ref[...], _slot(dq_ref, t), cv, sv,
                                         C_QK, C_NOPE, C_ROPE // 2)
            dqo_ref[:, t * LANES:(t + 1) * LANES] = dx.astype(BF)
            dgq = dgq + dg
            dx, dg = _head_norm_rope_bwd(_slot(kn_ref, t) + kr, gk_ref[...], _slot(dk_ref, t), cv, sv,
                                         C_QK, C_NOPE, C_ROPE // 2)
            dkno_ref[:, t * LANES:(t + 1) * LANES] = dx.astype(BF)
            dkr = dkr + dx
            dgk = dgk + dg
        lane = lax.broadcasted_iota(jnp.int32, dkr.shape, 1)
        dkr_ref[...] = jnp.where((lane >= C_NOPE) & (lane < C_QK), dkr, 0.0)
        dg_ref[0:1, :] += dgq
        dg_ref[1:2, :] += dgk

    def rows(w, blk=0):
        return pl.BlockSpec((tm, w), lambda i: (i, blk))

    vec = pl.BlockSpec((1, LANES), lambda i: (0, 0))
    w = C_HEADS * LANES
    dqo, dkno, dkr, dg = pl.pallas_call(
        body, grid=(s // tm,),
        in_specs=[rows(w), rows(w), rows(LANES, kr_blk), rows(w), rows(w), rows(LANES), rows(LANES), vec, vec],
        out_specs=[rows(w), rows(w), rows(LANES), pl.BlockSpec((8, LANES), lambda i: (0, 0))],
        out_shape=[jax.ShapeDtypeStruct((s, w), BF), jax.ShapeDtypeStruct((s, w), BF),
                   jax.ShapeDtypeStruct((s, LANES), F32), jax.ShapeDtypeStruct((8, LANES), F32)],
        compiler_params=_cp("arbitrary"), name=name)(qraw, knope, dn, dq, dk, cos, sin, gq, gk)
    return dqo, dkno, dkr, dg[0:1], dg[1:2]


def _lane_pick(mat, g):
    lane = lax.broadcasted_iota(jnp.int32, mat.shape, 1)
    return jnp.sum(jnp.where(lane == g, mat, 0.0), axis=-1, keepdims=True)


def _swa_fwd(q, k, v, sinks, *, name):
    s = q.shape[0]
    nblk = s // BLOCK
    scale = A_HEAD_DIM ** -0.5
    gw = A_GROUP * LANES

    def body(sink_ref, q_ref, kc_ref, kp_ref, vc_ref, vp_ref, o_ref, lse_ref):
        kvh, n = pl.program_id(0), pl.program_id(1)
        r = lax.broadcasted_iota(jnp.int32, (BLOCK, BLOCK), 0)
        c = lax.broadcasted_iota(jnp.int32, (BLOCK, BLOCK), 1)
        cur_ok = c <= r
        prev_ok = (c > r) & (n > 0)
        lane = lax.broadcasted_iota(jnp.int32, (BLOCK, LANES), 1)
        lse_mat = jnp.zeros((BLOCK, LANES), F32)
        for g in range(A_GROUP):
            qg = _slot(q_ref, g)
            sc = jnp.where(cur_ok, _dot_nt(qg, kc_ref[...]) * scale, NEG)
            sp = jnp.where(prev_ok, _dot_nt(qg, kp_ref[...]) * scale, NEG)
            sink = sink_ref[kvh * A_GROUP + g]
            m = jnp.maximum(jnp.maximum(jnp.max(sc, axis=-1, keepdims=True),
                                        jnp.max(sp, axis=-1, keepdims=True)), sink)
            pc = jnp.exp(sc - m)
            pp = jnp.exp(sp - m)
            l = jnp.sum(pc, axis=-1, keepdims=True) + jnp.sum(pp, axis=-1, keepdims=True) + jnp.exp(sink - m)
            inv = 1.0 / l
            o = _dot((pc * inv).astype(BF), vc_ref[...]) + _dot((pp * inv).astype(BF), vp_ref[...])
            o_ref[:, g * LANES:(g + 1) * LANES] = o.astype(BF)
            lse_mat = jnp.where(lane == g, m + jnp.log(l), lse_mat)
        lse_ref[0] = lse_mat

    cur = pl.BlockSpec((BLOCK, LANES), lambda h, n: (n, h))
    prev = pl.BlockSpec((BLOCK, LANES), lambda h, n: (jnp.maximum(n - 1, 0), h))
    return pl.pallas_call(
        body, grid=(A_KV_HEADS, nblk),
        in_specs=[pl.BlockSpec(memory_space=pltpu.SMEM), pl.BlockSpec((BLOCK, gw), lambda h, n: (n, h)),
                  cur, prev, cur, prev],
        out_specs=[pl.BlockSpec((BLOCK, gw), lambda h, n: (n, h)),
                   pl.BlockSpec((1, BLOCK, LANES), lambda h, n: (h, n, 0))],
        out_shape=[jax.ShapeDtypeStruct(q.shape, BF), jax.ShapeDtypeStruct((A_KV_HEADS, s, LANES), F32)],
        compiler_params=_cp("parallel", "parallel"), name=name)(sinks, q, k, k, v, v)


def _swa_bwd(q, k, v, o, do, lse, sinks, *, name):
    s = q.shape[0]
    nblk = s // BLOCK
    scale = A_HEAD_DIM ** -0.5
    gw = A_GROUP * LANES

    def body(sink_ref, qc_ref, qn_ref, k_ref, v_ref, oc_ref, on_ref, doc_ref, don_ref, lc_ref, ln_ref,
             dq_ref, dk_ref, dv_ref, dsink_ref, carry, dk_acc, dv_acc):
        kvh, j = pl.program_id(0), pl.program_id(1)

        @pl.when(j == 0)
        def _():
            carry[...] = jnp.zeros_like(carry)
            dsink_ref[...] = jnp.zeros_like(dsink_ref)

        dk_acc[...] = jnp.zeros_like(dk_acc)
        dv_acc[...] = jnp.zeros_like(dv_acc)
        r = lax.broadcasted_iota(jnp.int32, (BLOCK, BLOCK), 0)
        c = lax.broadcasted_iota(jnp.int32, (BLOCK, BLOCK), 1)
        lane8 = lax.broadcasted_iota(jnp.int32, (8, LANES), 1)
        kv, vv = k_ref[...], v_ref[...]

        def pair(g, q_ref, o_ref, do_ref, l_ref, ok):
            qg, og, dog = _slot(q_ref, g), _slot(o_ref, g), _slot(do_ref, g)
            lse = _lane_pick(l_ref[0], g)
            delta = jnp.sum(dog.astype(F32) * og.astype(F32), axis=-1, keepdims=True)
            sc = jnp.where(ok, _dot_nt(qg, kv) * scale, NEG)
            p = jnp.exp(sc - lse)
            dv_acc[...] += _dot_tn(p.astype(BF), dog)
            dp = _dot_nt(dog, vv)
            ds = (p * (dp - delta) * scale).astype(BF)
            dk_acc[...] += _dot_tn(ds, qg)
            return _dot(ds, kv), delta, lse

        dsink = jnp.zeros((8, LANES), F32)
        for g in range(A_GROUP):
            dq_part, delta, lse = pair(g, qc_ref, oc_ref, doc_ref, lc_ref, c <= r)
            dq_ref[:, g * LANES:(g + 1) * LANES] = carry[:, g * LANES:(g + 1) * LANES] + dq_part
            p_sink = jnp.exp(sink_ref[kvh * A_GROUP + g] - lse)
            dsink = dsink + jnp.where(lane8 == g, -jnp.sum(p_sink * delta), 0.0)
        dsink_ref[...] += dsink

        @pl.when(j < nblk - 1)
        def _():
            for g in range(A_GROUP):
                dq_part, _, _ = pair(g, qn_ref, on_ref, don_ref, ln_ref, c > r)
                carry[:, g * LANES:(g + 1) * LANES] = dq_part

        dk_ref[...] = dk_acc[...]
        dv_ref[...] = dv_acc[...]

    def nxt(n):
        return jnp.minimum(n + 1, nblk - 1)

    grp_c = pl.BlockSpec((BLOCK, gw), lambda h, n: (n, h))
    grp_n = pl.BlockSpec((BLOCK, gw), lambda h, n: (nxt(n), h))
    kvb = pl.BlockSpec((BLOCK, LANES), lambda h, n: (n, h))
    lse_c = pl.BlockSpec((1, BLOCK, LANES), lambda h, n: (h, n, 0))
    lse_n = pl.BlockSpec((1, BLOCK, LANES), lambda h, n: (h, nxt(n), 0))
    dq, dk, dv, dsink = pl.pallas_call(
        body, grid=(A_KV_HEADS, nblk),
        in_specs=[pl.BlockSpec(memory_space=pltpu.SMEM), grp_c, grp_n, kvb, kvb, grp_c, grp_n, grp_c, grp_n,
                  lse_c, lse_n],
        out_specs=[grp_c, kvb, kvb, pl.BlockSpec((8, LANES), lambda h, n: (h, 0))],
        out_shape=[jax.ShapeDtypeStruct(q.shape, F32), jax.ShapeDtypeStruct(k.shape, F32),
                   jax.ShapeDtypeStruct(v.shape, F32), jax.ShapeDtypeStruct((A_KV_HEADS * 8, LANES), F32)],
        scratch_shapes=[pltpu.VMEM((BLOCK, gw), F32), pltpu.VMEM((BLOCK, LANES), F32),
                        pltpu.VMEM((BLOCK, LANES), F32)],
        compiler_params=_cp("parallel", "arbitrary"), name=name)(sinks, q, q, k, v, o, o, do, do, lse, lse)
    dsinks = dsink.reshape(A_KV_HEADS, 8, LANES)[:, 0, :A_GROUP].reshape(A_HEADS)
    return dq, dk, dv, dsinks


def _mla_fwd(q, k, v, *, name):
    s = q.shape[0]
    t = _tile(s, ATTN_TILE, LANES)
    nt = s // t
    scale = C_QK ** -0.5
    rep = t // LANES

    def body(q_ref, k_ref, v_ref, o_ref, lse_ref, m_sc, l_sc, acc_sc):
        qi, ki = pl.program_id(1), pl.program_id(2)

        @pl.when(ki == 0)
        def _():
            m_sc[...] = jnp.full_like(m_sc, NEG)
            l_sc[...] = jnp.zeros_like(l_sc)
            acc_sc[...] = jnp.zeros_like(acc_sc)

        def step(diag):
            sc = _dot_nt(q_ref[...], k_ref[...]) * scale
            if diag:
                r = lax.broadcasted_iota(jnp.int32, (t, t), 0)
                c = lax.broadcasted_iota(jnp.int32, (t, t), 1)
                sc = jnp.where(c <= r, sc, NEG)
            m_prev = m_sc[...]
            m_new = jnp.maximum(m_prev, jnp.max(sc, axis=-1, keepdims=True))
            alpha = jnp.exp(m_prev - m_new)
            p = jnp.exp(sc - jnp.tile(m_new, (1, rep)))
            l_sc[...] = alpha * l_sc[...] + jnp.sum(p, axis=-1, keepdims=True)
            acc_sc[...] = alpha * acc_sc[...] + _dot(p.astype(BF), v_ref[...])
            m_sc[...] = m_new

        @pl.when(ki < qi)
        def _():
            step(False)

        @pl.when(ki == qi)
        def _():
            step(True)
            o_ref[...] = (acc_sc[...] / l_sc[...]).astype(BF)
            lse_ref[0] = m_sc[...] + jnp.log(l_sc[...])

    qspec = pl.BlockSpec((t, LANES), lambda h, i, j: (i, h))
    kspec = pl.BlockSpec((t, LANES), lambda h, i, j: (jnp.minimum(j, i), h))
    return pl.pallas_call(
        body, grid=(C_HEADS, nt, nt), in_specs=[qspec, kspec, kspec],
        out_specs=[qspec, pl.BlockSpec((1, t, LANES), lambda h, i, j: (h, i, 0))],
        out_shape=[jax.ShapeDtypeStruct(q.shape, BF), jax.ShapeDtypeStruct((C_HEADS, s, LANES), F32)],
        scratch_shapes=[pltpu.VMEM((t, LANES), F32)] * 3,
        compiler_params=_cp("parallel", "parallel", "arbitrary"), name=name)(q, k, v)


def _mla_bwd(q, k, v, o, do, lse, *, name):
    s = q.shape[0]
    t = _tile(s, ATTN_TILE, LANES)
    nt = s // t
    scale = C_QK ** -0.5
    rep = t // LANES

    def body(q_ref, k_ref, v_ref, o_ref, do_ref, lse_ref, dq_ref, dk_ref, dv_ref, dk_acc, dv_acc):
        kj, qi = pl.program_id(1), pl.program_id(2)

        @pl.when((kj == 0) & (qi == 0))
        def _():
            dq_ref[...] = jnp.zeros_like(dq_ref)

        @pl.when(qi == 0)
        def _():
            dk_acc[...] = jnp.zeros_like(dk_acc)
            dv_acc[...] = jnp.zeros_like(dv_acc)

        def step(diag):
            qv, kv, dov = q_ref[...], k_ref[...], do_ref[...]
            sc = _dot_nt(qv, kv) * scale
            if diag:
                r = lax.broadcasted_iota(jnp.int32, (t, t), 0)
                c = lax.broadcasted_iota(jnp.int32, (t, t), 1)
                sc = jnp.where(c <= r, sc, NEG)
            p = jnp.exp(sc - jnp.tile(lse_ref[0], (1, rep)))
            dv_acc[...] += _dot_tn(p.astype(BF), dov)
            dp = _dot_nt(dov, v_ref[...])
            delta = jnp.sum(dov.astype(F32) * o_ref[...].astype(F32), axis=-1, keepdims=True)
            ds = (p * (dp - delta) * scale).astype(BF)
            dk_acc[...] += _dot_tn(ds, qv)
            rows = pl.ds(pl.multiple_of(qi * t, t), t)
            dq_ref[rows, :] += _dot(ds, kv)

        @pl.when(qi > kj)
        def _():
            step(False)

        @pl.when(qi == kj)
        def _():
            step(True)

        @pl.when(qi == nt - 1)
        def _():
            dk_ref[...] = dk_acc[...]
            dv_ref[...] = dv_acc[...]

    qspec = pl.BlockSpec((t, LANES), lambda h, j, i: (jnp.maximum(i, j), h))
    kspec = pl.BlockSpec((t, LANES), lambda h, j, i: (j, h))
    lspec = pl.BlockSpec((1, t, LANES), lambda h, j, i: (h, jnp.maximum(i, j), 0))
    return pl.pallas_call(
        body, grid=(C_HEADS, nt, nt), in_specs=[qspec, kspec, kspec, qspec, qspec, lspec],
        out_specs=[pl.BlockSpec((s, LANES), lambda h, j, i: (0, h)), kspec, kspec],
        out_shape=[jax.ShapeDtypeStruct(q.shape, F32)] * 3,
        scratch_shapes=[pltpu.VMEM((t, LANES), F32)] * 2,
        compiler_params=_cp("parallel", "arbitrary", "arbitrary"), name=name)(q, k, v, o, do, lse)


def _conv_shifted(z, zprev, tm):
    row = lax.broadcasted_iota(jnp.int32, z.shape, 0)
    z1 = jnp.where(row == 0, zprev[7:8], pltpu.roll(z, 1, 0))
    z2 = jnp.where(row == 0, zprev[6:7], jnp.where(row == 1, zprev[7:8], pltpu.roll(z, 2, 0)))
    return z1, z2


def _conv_fwd(proj, w, *, name):
    s = proj.shape[0]
    d = D_MODEL
    tm = _tile(s, PREP_TILE, 8)

    def body(x_ref, xp_ref, w_ref, y_ref):
        i = pl.program_id(0)
        z = x_ref[:, d:2 * d] * x_ref[:, 2 * d:]
        zprev = jnp.where(i > 0, xp_ref[:, d:2 * d] * xp_ref[:, 2 * d:], 0.0)
        z1, z2 = _conv_shifted(z, zprev, tm)
        y = w_ref[0:1, :] * z2 + w_ref[1:2, :] * z1 + w_ref[2:3, :] * z
        y_ref[...] = (x_ref[:, :d] * y).astype(BF)

    per8 = tm // 8
    return pl.pallas_call(
        body, grid=(s // tm,),
        in_specs=[pl.BlockSpec((tm, 3 * d), lambda i: (i, 0)),
                  pl.BlockSpec((8, 3 * d), lambda i: (jnp.maximum(i * per8 - 1, 0), 0)),
                  pl.BlockSpec((8, d), lambda i: (0, 0))],
        out_specs=pl.BlockSpec((tm, d), lambda i: (i, 0)),
        out_shape=jax.ShapeDtypeStruct((s, d), BF),
        compiler_params=_cp("parallel"), name=name)(proj, proj, w)


def _conv_bwd(proj, dyb, w, *, name):
    s = proj.shape[0]
    d = D_MODEL
    tm = _tile(s, PREP_TILE, 8)
    nt = s // tm

    def body(x_ref, xp_ref, xn_ref, dy_ref, dyn_ref, w_ref, dx_ref, dw_ref):
        i = pl.program_id(0)

        @pl.when(i == 0)
        def _():
            dw_ref[...] = jnp.zeros_like(dw_ref)

        b, c, u = x_ref[:, :d], x_ref[:, d:2 * d], x_ref[:, 2 * d:]
        z = c * u
        zprev = jnp.where(i > 0, xp_ref[:, d:2 * d] * xp_ref[:, 2 * d:], 0.0)
        z1, z2 = _conv_shifted(z, zprev, tm)
        w0, w1, w2 = w_ref[0:1, :], w_ref[1:2, :], w_ref[2:3, :]
        y = w0 * z2 + w1 * z1 + w2 * z
        dyb_v = dy_ref[...]
        dyc = dyb_v * b
        dyn = jnp.where(i < nt - 1, dyn_ref[...] * xn_ref[:, :d], 0.0)
        row = lax.broadcasted_iota(jnp.int32, dyc.shape, 0)
        d1 = jnp.where(row == tm - 1, dyn[0:1], pltpu.roll(dyc, tm - 1, 0))
        d2 = jnp.where(row == tm - 1, dyn[1:2], jnp.where(row == tm - 2, dyn[0:1], pltpu.roll(dyc, tm - 2, 0)))
        dz = w2 * dyc + w1 * d1 + w0 * d2
        dx_ref[:, :d] = (dyb_v * y).astype(BF)
        dx_ref[:, d:2 * d] = (dz * u).astype(BF)
        dx_ref[:, 2 * d:] = (dz * c).astype(BF)
        dw_ref[0:1, :] += jnp.sum(dyc * z2, axis=0, keepdims=True)
        dw_ref[1:2, :] += jnp.sum(dyc * z1, axis=0, keepdims=True)
        dw_ref[2:3, :] += jnp.sum(dyc * z, axis=0, keepdims=True)

    per8 = tm // 8
    last8 = s // 8 - 1
    dx, dw = pl.pallas_call(
        body, grid=(nt,),
        in_specs=[pl.BlockSpec((tm, 3 * d), lambda i: (i, 0)),
                  pl.BlockSpec((8, 3 * d), lambda i: (jnp.maximum(i * per8 - 1, 0), 0)),
                  pl.BlockSpec((8, 3 * d), lambda i: (jnp.minimum((i + 1) * per8, last8), 0)),
                  pl.BlockSpec((tm, d), lambda i: (i, 0)),
                  pl.BlockSpec((8, d), lambda i: (jnp.minimum((i + 1) * per8, last8), 0)),
                  pl.BlockSpec((8, d), lambda i: (0, 0))],
        out_specs=[pl.BlockSpec((tm, 3 * d), lambda i: (i, 0)), pl.BlockSpec((8, d), lambda i: (0, 0))],
        out_shape=[jax.ShapeDtypeStruct((s, 3 * d), BF), jax.ShapeDtypeStruct((8, d), F32)],
        compiler_params=_cp("arbitrary"), name=name)(proj, proj, proj, dyb, dyb, w)
    return dx, dw[0:3]


def _ffn_up(h, wgu, *, name):
    s, d = h.shape
    f = wgu.shape[1] // 2
    tm = _tile(s, ROW_TILE, 8)
    tn = _tile(f, 1408, 128)
    nb = f // tn

    def body(h_ref, wg_ref, wu_ref, g_ref, u_ref, a_ref):
        hv = h_ref[...]
        g = _dot(hv, wg_ref[...])
        u = _dot(hv, wu_ref[...])
        g_ref[...] = g.astype(BF)
        u_ref[...] = u.astype(BF)
        a_ref[...] = (g * jax.nn.sigmoid(g) * u).astype(BF)

    out = pl.BlockSpec((tm, tn), lambda j, i: (i, j))
    return pl.pallas_call(
        body, grid=(nb, s // tm),
        in_specs=[pl.BlockSpec((tm, d), lambda j, i: (i, 0)),
                  pl.BlockSpec((d, tn), lambda j, i: (0, j)),
                  pl.BlockSpec((d, tn), lambda j, i: (0, j + nb))],
        out_specs=[out] * 3, out_shape=[jax.ShapeDtypeStruct((s, f), BF)] * 3,
        compiler_params=_cp("parallel", "parallel"), name=name)(h, wgu, wgu)


def _ffn_bwd_mid(dy, wd, gate, up, *, name):
    s, d = dy.shape
    f = wd.shape[0]
    tm = _tile(s, ROW_TILE, 8)
    tn = _tile(f, 1408, 128)

    def body(dy_ref, wd_ref, g_ref, u_ref, dg_ref, du_ref):
        dact = _dot_nt(dy_ref[...].astype(BF), wd_ref[...])
        g = g_ref[...].astype(F32)
        u = u_ref[...].astype(F32)
        sig = jax.nn.sigmoid(g)
        dg_ref[...] = (dact * u * (sig * (1.0 + g * (1.0 - sig)))).astype(BF)
        du_ref[...] = (dact * (g * sig)).astype(BF)

    blk = pl.BlockSpec((tm, tn), lambda j, i: (i, j))
    return pl.pallas_call(
        body, grid=(f // tn, s // tm),
        in_specs=[pl.BlockSpec((tm, d), lambda j, i: (i, 0)), pl.BlockSpec((tn, d), lambda j, i: (j, 0)), blk, blk],
        out_specs=[blk, blk], out_shape=[jax.ShapeDtypeStruct((s, f), BF)] * 2,
        compiler_params=_cp("parallel", "parallel"), name=name)(dy, wd, gate, up)


def _pad_cols(w, heads, dim):
    k = w.shape[0]
    return jnp.pad(w.reshape(k, heads, dim), ((0, 0), (0, 0), (0, LANES - dim))).reshape(k, heads * LANES)


def _unpad_cols(w, heads, dim):
    k = w.shape[0]
    return w.reshape(k, heads, LANES)[:, :, :dim].reshape(k, heads * dim)


def _pad_rows(w, heads, dim):
    n = w.shape[1]
    return jnp.pad(w.reshape(heads, dim, n), ((0, 0), (0, LANES - dim), (0, 0))).reshape(heads * LANES, n)


def _unpad_rows(w, heads, dim):
    n = w.shape[1]
    return w.reshape(heads, LANES, n)[:, :dim, :].reshape(heads * dim, n)


def _pad_vec(g):
    return jnp.pad(g, (0, LANES - g.shape[0])).reshape(1, LANES)


def _local_step(x, positions, w, target):
    s = x.shape[0]
    cos_a, sin_a, cos_c, sin_c = _rope_tables(positions.reshape(s, 1), name="rope_tables")
    grads = {n: [None] * w[n].shape[0] for n in WEIGHTS}
    saved = []

    for i in range(DEPTH):
        kind, j = i % N_MIXERS, i // N_MIXERS
        tag = f"l{i}"
        h = _rmsnorm_fwd(x, w['mix_norm'][i], name=f"{tag}_mix_norm")
        if kind == 0:
            wqkv = _pad_cols(w['a_w_qkv'][j], A_HEADS + 2 * A_KV_HEADS, A_HEAD_DIM)
            wo = _pad_rows(w['a_w_o'][j], A_HEADS, A_HEAD_DIM)
            gq, gk = _pad_vec(w['a_q_norm'][j]), _pad_vec(w['a_k_norm'][j])
            qkv = _mm([(h, wqkv, 0)], out_dtype=F32, name=f"{tag}_qkv")
            qa, ka, va = _prep_a_fwd(qkv, cos_a, sin_a, gq, gk, name=f"{tag}_prep")
            o, lse = _swa_fwd(qa, ka, va, w['a_sinks'][j], name=f"{tag}_attn")
            x1 = _mm([(o, wo, 0)], out_dtype=F32, res=x, name=f"{tag}_wo")
            mix = dict(wqkv=wqkv, wo=wo, gq=gq, gk=gk, qkv=qkv, qa=qa, ka=ka, va=va, o=o, lse=lse)
        elif kind == 1:
            cw = jnp.pad(w['b_conv_w'][j], ((0, 5), (0, 0)))
            proj = _mm([(h, w['b_w_in'][j], 0)], out_dtype=F32, name=f"{tag}_win")
            yb = _conv_fwd(proj, cw, name=f"{tag}_conv")
            x1 = _mm([(yb, w['b_w_out'][j], 0)], out_dtype=F32, res=x, name=f"{tag}_wout")
            mix = dict(cw=cw, proj=proj, yb=yb)
        else:
            wdn = w['c_w_down'][j]
            nqk = C_Q_RANK + C_KV_RANK
            zeros = functools.partial(jnp.zeros, dtype=wdn.dtype)
            wdn = jnp.concatenate([wdn[:, :nqk], zeros((D_MODEL, C_NOPE)), wdn[:, nqk:],
                                   zeros((D_MODEL, LANES - C_QK))], axis=1)
            wq = _pad_cols(w['c_w_q_up'][j], C_HEADS, C_QK)
            wkv = w['c_w_kv_up'][j].reshape(C_KV_RANK, C_HEADS, C_NOPE + C_V)
            wkn = _pad_cols(wkv[:, :, :C_NOPE].reshape(C_KV_RANK, -1), C_HEADS, C_NOPE)
            wv = _pad_cols(wkv[:, :, C_NOPE:].reshape(C_KV_RANK, -1), C_HEADS, C_V)
            wo = _pad_rows(w['c_w_o'][j], C_HEADS, C_V)
            gq, gk = _pad_vec(w['c_q_norm'][j]), _pad_vec(w['c_k_norm'][j])
            gqa, gkva = w['c_q_a_norm'][j].reshape(1, -1), w['c_kv_a_norm'][j].reshape(1, -1)
            dn = _mm([(h, wdn, 0)], out_dtype=F32, name=f"{tag}_wdown")
            cqn, ckvn = _prep_c1_fwd(dn, gqa, gkva, name=f"{tag}_prep1")
            qraw = _mm([(cqn, wq, 0)], out_dtype=F32, name=f"{tag}_wq")
            knope = _mm([(ckvn, wkn, 0)], out_dtype=F32, name=f"{tag}_wkn")
            vc = _mm([(ckvn, wv, 0)], out_dtype=BF, name=f"{tag}_wv")
            qc, kc = _prep_c2_fwd(qraw, knope, dn, cos_c, sin_c, gq, gk, name=f"{tag}_prep2")
            o, lse = _mla_fwd(qc, kc, vc, name=f"{tag}_attn")
            x1 = _mm([(o, wo, 0)], out_dtype=F32, res=x, name=f"{tag}_wo")
            mix = dict(wdn=wdn, wq=wq, wkn=wkn, wv=wv, wo=wo, gq=gq, gk=gk, gqa=gqa, gkva=gkva, dn=dn, cqn=cqn,
                       ckvn=ckvn, qraw=qraw, knope=knope, vc=vc, qc=qc, kc=kc, o=o, lse=lse)
        h2 = _rmsnorm_fwd(x1, w['ffn_norm'][i], name=f"{tag}_ffn_norm")
        gate, up, act = _ffn_up(h2, w['f_w_gate_up'][i], name=f"{tag}_ffn_up")
        x2 = _mm([(act, w['f_w_down'][i], 0)], out_dtype=F32, res=x1, name=f"{tag}_ffn_down")
        saved.append(dict(x=x, h=h, x1=x1, h2=h2, gate=gate, up=up, act=act, mix=mix))
        x = x2

    loss_blk, dx = _loss_fwd_bwd(x, target, name="loss")

    for i in reversed(range(DEPTH)):
        kind, j = i % N_MIXERS, i // N_MIXERS
        tag = f"l{i}b"
        sv = saved[i]
        mix = sv['mix']
        wgu, wd = w['f_w_gate_up'][i], w['f_w_down'][i]
        grads['f_w_down'][i] = _mm_tn(sv['act'], dx, name=f"{tag}_dwd")
        dgate, dup = _ffn_bwd_mid(dx, wd, sv['gate'], sv['up'], name=f"{tag}_ffn_mid")
        grads['f_w_gate_up'][i] = jnp.concatenate(
            [_mm_tn(sv['h2'], dgate, name=f"{tag}_dwg"), _mm_tn(sv['h2'], dup, name=f"{tag}_dwu")], axis=1)
        dh2 = _mm([(dgate, wgu, 0), (dup, wgu, 1)], out_dtype=F32, trans_b=True, name=f"{tag}_dh2")
        dx, dg = _rmsnorm_bwd(sv['x1'], w['ffn_norm'][i], dh2, dx, name=f"{tag}_ffn_norm")
        grads['ffn_norm'][i] = dg[0]
        if kind == 0:
            grads['a_w_o'][j] = _unpad_rows(_mm_tn(mix['o'], dx, name=f"{tag}_dwo"), A_HEADS, A_HEAD_DIM)
            do = _mm([(dx, mix['wo'], 0)], out_dtype=BF, trans_b=True, name=f"{tag}_do")
            dqa, dka, dva, dsinks = _swa_bwd(mix['qa'], mix['ka'], mix['va'], mix['o'], do, mix['lse'],
                                             w['a_sinks'][j], name=f"{tag}_attn")
            dqkv, dgq, dgk = _prep_a_bwd(mix['qkv'], dqa, dka, dva, cos_a, sin_a, mix['gq'], mix['gk'],
                                         name=f"{tag}_prep")
            grads['a_sinks'][j] = dsinks
            grads['a_q_norm'][j] = dgq[0, :A_HEAD_DIM]
            grads['a_k_norm'][j] = dgk[0, :A_HEAD_DIM]
            grads['a_w_qkv'][j] = _unpad_cols(_mm_tn(sv['h'], dqkv, name=f"{tag}_dwqkv"),
                                              A_HEADS + 2 * A_KV_HEADS, A_HEAD_DIM)
            dh = _mm([(dqkv, mix['wqkv'], 0)], out_dtype=F32, trans_b=True, name=f"{tag}_dh")
        elif kind == 1:
            grads['b_w_out'][j] = _mm_tn(mix['yb'], dx, name=f"{tag}_dwout")
            dyb = _mm([(dx, w['b_w_out'][j], 0)], out_dtype=F32, trans_b=True, name=f"{tag}_dyb")
            dproj, dcw = _conv_bwd(mix['proj'], dyb, mix['cw'], name=f"{tag}_conv")
            grads['b_conv_w'][j] = dcw
            grads['b_w_in'][j] = _mm_tn(sv['h'], dproj, name=f"{tag}_dwin")
            dh = _mm([(dproj, w['b_w_in'][j], 0)], out_dtype=F32, trans_b=True, name=f"{tag}_dh")
        else:
            grads['c_w_o'][j] = _unpad_rows(_mm_tn(mix['o'], dx, name=f"{tag}_dwo"), C_HEADS, C_V)
            do = _mm([(dx, mix['wo'], 0)], out_dtype=BF, trans_b=True, name=f"{tag}_do")
            dqc, dkc, dvc = _mla_bwd(mix['qc'], mix['kc'], mix['vc'], mix['o'], do, mix['lse'], name=f"{tag}_attn")
            dqraw, dknope, dkr, dgq, dgk = _prep_c2_bwd(mix['qraw'], mix['knope'], mix['dn'], dqc, dkc, cos_c, sin_c,
                                                        mix['gq'], mix['gk'], name=f"{tag}_prep2")
            grads['c_q_norm'][j] = dgq[0, :C_QK]
            grads['c_k_norm'][j] = dgk[0, :C_QK]
            grads['c_w_q_up'][j] = _unpad_cols(_mm_tn(mix['cqn'], dqraw, name=f"{tag}_dwq"), C_HEADS, C_QK)
            dwkn = _mm_tn(mix['ckvn'], dknope, name=f"{tag}_dwkn").reshape(C_KV_RANK, C_HEADS, LANES)
            dwv = _mm_tn(mix['ckvn'], dvc, name=f"{tag}_dwv").reshape(C_KV_RANK, C_HEADS, LANES)
            grads['c_w_kv_up'][j] = jnp.concatenate([dwkn[:, :, :C_NOPE], dwv[:, :, :C_V]], axis=2).reshape(
                C_KV_RANK, -1)
            dcq = _mm([(dqraw, mix['wq'], 0)], out_dtype=F32, trans_b=True, name=f"{tag}_dcq")
            dckv = _mm([(dknope, mix['wkn'], 0), (dvc, mix['wv'], 0)], out_dtype=F32, trans_b=True,
                       name=f"{tag}_dckv")
            ddn, dgqa, dgkva = _prep_c1_bwd(mix['dn'], dcq, dckv, dkr, mix['gqa'], mix['gkva'], name=f"{tag}_prep1")
            grads['c_q_a_norm'][j] = dgqa[0]
            grads['c_kv_a_norm'][j] = dgkva[0]
            dwdn = _mm_tn(sv['h'], ddn, name=f"{tag}_dwdown")
            nqk = C_Q_RANK + C_KV_RANK
            grads['c_w_down'][j] = jnp.concatenate([dwdn[:, :nqk], dwdn[:, nqk + C_NOPE:nqk + C_QK]], axis=1)
            dh = _mm([(ddn, mix['wdn'], 0)], out_dtype=F32, trans_b=True, name=f"{tag}_dh")
        dx, dg = _rmsnorm_bwd(sv['x'], w['mix_norm'][i], dh, dx, name=f"{tag}_mix_norm")
        grads['mix_norm'][i] = dg[0]

    return loss_blk, dx, {n: jnp.stack(g) for n, g in grads.items()}


def _my_place():
    return lax.axis_index("x"), lax.axis_index("y"), lax.axis_index("c")


def _flips(x, y):
    return [(1 - x, y), (x, 1 - y), (1 - x, 1 - y)]


def _gather_weights(big, small):
    def body(big_ref, small_ref, bout_ref, sout_ref, send_b, recv_b, send_s, recv_s, loc):
        x, y, c = _my_place()
        me = 2 * x + y

        def copies(k, src_chip, to):
            return (pltpu.make_async_remote_copy(src_ref=big_ref, dst_ref=bout_ref.at[src_chip],
                                                 send_sem=send_b.at[k], recv_sem=recv_b.at[k],
                                                 device_id=to, device_id_type=MESH),
                    pltpu.make_async_remote_copy(src_ref=small_ref, dst_ref=sout_ref.at[src_chip],
                                                 send_sem=send_s.at[k], recv_sem=recv_s.at[k],
                                                 device_id=to, device_id_type=MESH))

        own = (pltpu.make_async_copy(big_ref, bout_ref.at[me], loc.at[0]),
               pltpu.make_async_copy(small_ref, sout_ref.at[me], loc.at[1]))
        for cp in own:
            cp.start()
        sends = [copies(k, me, (px, py, c)) for k, (px, py) in enumerate(_flips(x, y))]
        for pair in sends:
            for cp in pair:
                cp.start()
        for k, (px, py) in enumerate(_flips(x, y)):
            for cp in copies(k, 2 * px + py, (x, y, c)):
                cp.wait_recv()
        for pair in sends:
            for cp in pair:
                cp.wait_send()
        for cp in own:
            cp.wait()

    hbm = pl.BlockSpec(memory_space=pltpu.HBM)
    return pl.pallas_call(
        body, in_specs=[hbm, hbm], out_specs=[hbm, hbm],
        out_shape=[jax.ShapeDtypeStruct((N_CHIPS,) + big.shape, big.dtype),
                   jax.ShapeDtypeStruct((N_CHIPS,) + small.shape, small.dtype)],
        scratch_shapes=[pltpu.SemaphoreType.DMA((3,)), pltpu.SemaphoreType.DMA((3,)),
                        pltpu.SemaphoreType.DMA((3,)), pltpu.SemaphoreType.DMA((3,)),
                        pltpu.SemaphoreType.DMA((2,))],
        name="gather_weights")(big, small)


def _exchange_grads(gbig, gsmall):
    def body(gbig_ref, gsmall_ref, got_ref, ssum_ref, sbuf, send_b, recv_b, send_s, recv_s):
        x, y, c = _my_place()
        me = 4 * x + 2 * y + c

        def big_copy(k, src_chip, to):
            return pltpu.make_async_remote_copy(src_ref=gbig_ref.at[src_chip], dst_ref=got_ref.at[k],
                                                send_sem=send_b.at[k], recv_sem=recv_b.at[k],
                                                device_id=to, device_id_type=MESH)

        def small_copy(k, slot, to):
            return pltpu.make_async_remote_copy(src_ref=gsmall_ref, dst_ref=sbuf.at[slot],
                                                send_sem=send_s.at[k], recv_sem=recv_s.at[k],
                                                device_id=to, device_id_type=MESH)

        peers = [(x ^ (k >> 2), y ^ ((k >> 1) & 1), c ^ (k & 1)) for k in range(1, N_DEV)]
        bigs = [big_copy(k, 2 * px + py, (px, py, c)) for k, (px, py) in enumerate(_flips(x, y))]
        smalls = [small_copy(k, me, p) for k, p in enumerate(peers)]
        for cp in bigs + smalls:
            cp.start()
        sbuf[me] = gsmall_ref[...]
        for k, (px, py, pc) in enumerate(peers):
            small_copy(k, 4 * px + 2 * py + pc, (x, y, c)).wait_recv()
        acc = sbuf[0]
        for d in range(1, N_DEV):
            acc = acc + sbuf[d]
        ssum_ref[...] = acc
        for k in range(3):
            big_copy(k, 0, (x, y, c)).wait_recv()
        for cp in bigs + smalls:
            cp.wait_send()

    hbm = pl.BlockSpec(memory_space=pltpu.HBM)
    vmem = pl.BlockSpec(memory_space=pltpu.VMEM)
    return pl.pallas_call(
        body, in_specs=[hbm, vmem], out_specs=[hbm, vmem],
        out_shape=[jax.ShapeDtypeStruct((3,) + gbig.shape[1:], gbig.dtype),
                   jax.ShapeDtypeStruct(gsmall.shape, gsmall.dtype)],
        scratch_shapes=[pltpu.VMEM((N_DEV,) + gsmall.shape, gsmall.dtype),
                        pltpu.SemaphoreType.DMA((3,)), pltpu.SemaphoreType.DMA((3,)),
                        pltpu.SemaphoreType.DMA((N_DEV - 1,)), pltpu.SemaphoreType.DMA((N_DEV - 1,))],
        name="exchange_grads")(gbig, gsmall)


def _sum_shards(chip, gbig, got):
    r, cols = gbig.shape[1:]
    tr = _tile(r, PACK_ROW_MULT, 8)

    def body(chip_ref, mine_ref, got_ref, o_ref):
        o_ref[...] = ((mine_ref[0] + got_ref[0]) + got_ref[1]) + got_ref[2]

    grid_spec = pltpu.PrefetchScalarGridSpec(
        num_scalar_prefetch=1, grid=(r // tr,),
        in_specs=[pl.BlockSpec((1, tr, cols), lambda i, chip_ref: (chip_ref[0], i, 0)),
                  pl.BlockSpec((3, tr, cols), lambda i, chip_ref: (0, i, 0))],
        out_specs=pl.BlockSpec((tr, cols), lambda i, chip_ref: (i, 0)))
    return pl.pallas_call(
        body, grid_spec=grid_spec, out_shape=jax.ShapeDtypeStruct((r, cols), F32),
        compiler_params=_cp("parallel"), name="sum_shards")(chip, gbig, got)


def _swap_sibling(t):
    def body(t_ref, got_ref, send_sem, recv_sem):
        x, y, c = _my_place()
        cp = pltpu.make_async_remote_copy(src_ref=t_ref, dst_ref=got_ref, send_sem=send_sem, recv_sem=recv_sem,
                                          device_id=(x, y, 1 - c), device_id_type=MESH)
        cp.start()
        cp.wait()

    hbm = pl.BlockSpec(memory_space=pltpu.HBM)
    return pl.pallas_call(
        body, in_specs=[hbm], out_specs=hbm, out_shape=jax.ShapeDtypeStruct(t.shape, t.dtype),
        scratch_shapes=[pltpu.SemaphoreType.DMA, pltpu.SemaphoreType.DMA], name="swap_sibling")(t)


def _adamw(wt, m, v, ga, gb, *, name):
    r, cols = wt.shape
    tr = _tile(r, 256, 8)

    def body(w_ref, m_ref, v_ref, ga_ref, gb_ref, g_ref, d_ref, nm_ref, nv_ref):
        g = ga_ref[...] + gb_ref[...]
        nm = ADAM_B1 * m_ref[...] + (1.0 - ADAM_B1) * g
        nv = ADAM_B2 * v_ref[...] + (1.0 - ADAM_B2) * (g * g)
        m_hat = nm / (1.0 - ADAM_B1 ** ADAM_STEP)
        v_hat = nv / (1.0 - ADAM_B2 ** ADAM_STEP)
        g_ref[...] = g
        d_ref[...] = -ADAM_LR * (m_hat / (jnp.sqrt(v_hat) + ADAM_EPS) + ADAM_WD * w_ref[...])
        nm_ref[...] = nm
        nv_ref[...] = nv

    blk = pl.BlockSpec((tr, cols), lambda i: (i, 0))
    return pl.pallas_call(
        body, grid=(r // tr,), in_specs=[blk] * 5, out_specs=[blk] * 4,
        out_shape=[jax.ShapeDtypeStruct((r, cols), F32)] * 4,
        compiler_params=_cp("parallel"), name=name)(wt, m, v, ga, gb)


def _shard_shape(full, axis):
    return tuple(d // N_CHIPS if a == axis else d for a, d in enumerate(full))


def _pack_rows(n):
    rows = -(-n // PACK_COLS)
    return -(-rows // PACK_ROW_MULT) * PACK_ROW_MULT


def _pack(parts, rows, dtype):
    flat = jnp.concatenate([p.reshape(-1).astype(dtype) for p in parts])
    return jnp.pad(flat, (0, rows * PACK_COLS - flat.shape[0])).reshape(rows, PACK_COLS)


def _join_shards(stacked, axis):
    moved = jnp.moveaxis(stacked, 0, axis)
    shp = moved.shape
    return moved.reshape(shp[:axis] + (shp[axis] * shp[axis + 1],) + shp[axis + 2:])


def _split_shards(full, axis):
    shp = full.shape
    split = full.reshape(shp[:axis] + (N_CHIPS, shp[axis] // N_CHIPS) + shp[axis + 1:])
    return jnp.moveaxis(split, axis, 0)


SMALL_ROWS = 128
SMALL_LAYOUT = [('mix_norm', DEPTH * D_MODEL), ('ffn_norm', DEPTH * D_MODEL), ('b_conv_w', 3 * D_MODEL),
                ('c_q_a_norm', C_Q_RANK), ('c_kv_a_norm', C_KV_RANK), ('a_q_norm', 2 * A_HEAD_DIM),
                ('a_k_norm', 2 * A_HEAD_DIM), ('a_sinks', 2 * A_HEADS), ('c_q_norm', C_QK), ('c_k_norm', C_QK),
                ('loss', 1)]


def _small_offsets():
    offs, row = {}, 0
    for name, n in SMALL_LAYOUT:
        offs[name] = (row * LANES, n)
        row += -(-n // LANES)
    assert row <= SMALL_ROWS
    return offs


def kernel(x, positions, mix_norm, ffn_norm, a_w_qkv, a_q_norm, a_k_norm, a_sinks, a_w_o, b_w_in, b_conv_w, b_w_out, c_w_down, c_q_a_norm, c_kv_a_norm, c_w_q_up, c_w_kv_up, c_q_norm, c_k_norm, c_w_o, f_w_gate_up, f_w_down, loss_target, m_mix_norm, m_ffn_norm, m_a_w_qkv, m_a_q_norm, m_a_k_norm, m_a_sinks, m_a_w_o, m_b_w_in, m_b_conv_w, m_b_w_out, m_c_w_down, m_c_q_a_norm, m_c_kv_a_norm, m_c_w_q_up, m_c_w_kv_up, m_c_q_norm, m_c_k_norm, m_c_w_o, m_f_w_gate_up, m_f_w_down, v_mix_norm, v_ffn_norm, v_a_w_qkv, v_a_q_norm, v_a_k_norm, v_a_sinks, v_a_w_o, v_b_w_in, v_b_conv_w, v_b_w_out, v_c_w_down, v_c_q_a_norm, v_c_kv_a_norm, v_c_w_q_up, v_c_w_kv_up, v_c_q_norm, v_c_k_norm, v_c_w_o, v_f_w_gate_up, v_f_w_down):
    args = dict(locals())
    wshard = {n: args[n] for n in WEIGHTS}
    sharded = {**BIG, **SMALL_SHARDED}
    chip = 2 * lax.axis_index("x") + lax.axis_index("y")

    n_big = sum(wshard[n].size for n in BIG)
    rows = _pack_rows(n_big)
    big = _pack([wshard[n] for n in BIG], rows, BF)
    small = jnp.concatenate([wshard[n].reshape(-1) for n in SMALL_SHARDED])
    small = jnp.pad(small, (0, 8 * LANES - small.shape[0])).reshape(8, LANES)
    big_all, small_all = _gather_weights(big, small)
    big_all = big_all.reshape(N_CHIPS, -1)
    small_all = small_all.reshape(N_CHIPS, -1)
    w = {}
    off = 0
    for n, axis in BIG.items():
        sz = wshard[n].size
        w[n] = _join_shards(big_all[:, off:off + sz].reshape((N_CHIPS,) + wshard[n].shape), axis)
        off += sz
    off = 0
    for n, axis in SMALL_SHARDED.items():
        sz = wshard[n].size
        w[n] = _join_shards(small_all[:, off:off + sz].reshape((N_CHIPS,) + wshard[n].shape), axis)
        off += sz
    for n in WEIGHTS:
        if n not in sharded:
            w[n] = wshard[n]

    loss_blk, grad_x, g = _local_step(x[0], positions[0], w, loss_target[0])

    gbig = jnp.concatenate([_split_shards(g[n], BIG[n]).reshape(N_CHIPS, -1) for n in BIG], axis=1)
    gbig = jnp.pad(gbig, ((0, 0), (0, rows * PACK_COLS - n_big))).reshape(N_CHIPS, rows, PACK_COLS)
    offs = _small_offsets()
    smalls = {**{n: g[n] for n, _ in SMALL_LAYOUT if n != 'loss'}, 'loss': loss_blk[0:1, 0:1]}
    gsmall = jnp.concatenate(
        [jnp.pad(smalls[n].reshape(-1), (0, -cnt % LANES)) for n, cnt in SMALL_LAYOUT])
    gsmall = jnp.pad(gsmall, (0, SMALL_ROWS * LANES - gsmall.shape[0])).reshape(SMALL_ROWS, LANES)
    got, ssum = _exchange_grads(gbig, gsmall)
    part = _sum_shards(chip.reshape(1).astype(jnp.int32), gbig, got)
    part_sib = _swap_sibling(part)

    ssum = ssum.reshape(-1)
    zero_small = jnp.zeros_like(ssum)
    part, part_sib = part.reshape(-1), part_sib.reshape(-1)
    outs = {}
    off = 0
    for n in WEIGHTS:
        wt = wshard[n]
        if n in BIG:
            ga, gb = part[off:off + wt.size], part_sib[off:off + wt.size]
            off += wt.size
        else:
            o0, cnt = offs[n]
            full = ssum[o0:o0 + cnt].reshape(g[n].shape)
            if n in SMALL_SHARDED:
                full = lax.dynamic_index_in_dim(_split_shards(full, SMALL_SHARDED[n]), chip, 0, keepdims=False)
            ga, gb = full.reshape(-1), zero_small[:wt.size]
        shape2 = (-1, wt.shape[-1])
        res = _adamw(wt.reshape(shape2), args['m_' + n].reshape(shape2), args['v_' + n].reshape(shape2),
                     ga.reshape(shape2), gb.reshape(shape2), name=f"adamw_{n}")
        outs[n] = [r.reshape(wt.shape) for r in res]
    loss = ssum[offs['loss'][0]]
    return (loss, grad_x[None], *[outs[n][0] for n in WEIGHTS], *[outs[n][1] for n in WEIGHTS],
            *[outs[n][2] for n in WEIGHTS], *[outs[n][3] for n in WEIGHTS])
```

```python
import functools

import jax
import jax.numpy as jnp
from jax import lax
from jax.experimental import pallas as pl
from jax.experimental.pallas import tpu as pltpu

F32 = jnp.float32
BF = jnp.bfloat16

D_MODEL = 1024
DEPTH = 4
N_MIXERS = 3
ROPE_THETA = 500000.0
EPS = 1e-6
BLOCK = 128
LANES = 128

A_HEADS, A_KV_HEADS, A_HEAD_DIM = 16, 4, 64
A_ROT_DIM = A_HEAD_DIM // 4
A_GROUP = A_HEADS // A_KV_HEADS
C_HEADS, C_NOPE, C_ROPE, C_V, C_Q_RANK, C_KV_RANK = 16, 64, 32, 64, 384, 256
C_QK = C_NOPE + C_ROPE
D_FF = 2816

ADAM_LR, ADAM_B1, ADAM_B2, ADAM_EPS, ADAM_WD, ADAM_STEP = 0.001, 0.9, 0.999, 1e-08, 0.01, 10

N_CHIPS = 4
N_DEV = 8
MESH = pl.DeviceIdType.MESH

VMEM_LIMIT = 56 * 1024 * 1024
ROW_TILE = 512
PREP_TILE = 256
ATTN_TILE = 512
NEG = -1e30

WEIGHTS = ['mix_norm', 'ffn_norm', 'a_w_qkv', 'a_q_norm', 'a_k_norm', 'a_sinks', 'a_w_o', 'b_w_in', 'b_conv_w',
           'b_w_out', 'c_w_down', 'c_q_a_norm', 'c_kv_a_norm', 'c_w_q_up', 'c_w_kv_up', 'c_q_norm', 'c_k_norm',
           'c_w_o', 'f_w_gate_up', 'f_w_down']
BIG = {'a_w_qkv': 2, 'a_w_o': 1, 'b_w_in': 2, 'b_w_out': 1, 'c_w_down': 1, 'c_w_q_up': 2, 'c_w_kv_up': 2,
       'c_w_o': 1, 'f_w_gate_up': 2, 'f_w_down': 1}
SMALL_SHARDED = {'b_conv_w': 2, 'c_q_a_norm': 1, 'c_kv_a_norm': 1}
PACK_COLS = 1024
PACK_ROW_MULT = 256


def _cp(*sem):
    return pltpu.CompilerParams(dimension_semantics=sem, vmem_limit_bytes=VMEM_LIMIT)


def _tile(n, target, mult):
    if n <= target:
        return n
    t = (target // mult) * mult
    while t >= mult:
        if n % t == 0:
            return t
        t -= mult
    raise ValueError(f"no tile for {n}")


def _dot(a, b):
    return lax.dot_general(a, b, (((1,), (0,)), ((), ())), preferred_element_type=F32)


def _dot_nt(a, b):
    return lax.dot_general(a, b, (((1,), (1,)), ((), ())), preferred_element_type=F32)


def _dot_tn(a, b):
    return lax.dot_general(a, b, (((0,), (0,)), ((), ())), preferred_element_type=F32)


def _mm(pairs, *, out_dtype, name, res=None, trans_b=False):
    a0, b0, _ = pairs[0]
    m = a0.shape[0]
    n = b0.shape[0] if trans_b else b0.shape[1]
    tm = _tile(m, ROW_TILE, 8)
    tn = _tile(n, 1024, 128)
    n_pairs = len(pairs)
    has_res = res is not None

    def body(*refs):
        o_ref = refs[-1]
        acc = None
        for p in range(n_pairs):
            a = refs[2 * p][...].astype(BF)
            b = refs[2 * p + 1][...].astype(BF)
            d = _dot_nt(a, b) if trans_b else _dot(a, b)
            acc = d if acc is None else acc + d
        if has_res:
            acc = acc + refs[2 * n_pairs][...]
        o_ref[...] = acc.astype(out_dtype)

    in_specs, args = [], []
    for a, b, kblk in pairs:
        k = a.shape[1]
        in_specs.append(pl.BlockSpec((tm, k), lambda j, i: (i, 0)))
        if trans_b:
            in_specs.append(pl.BlockSpec((tn, k), functools.partial(lambda j, i, kb: (j, kb), kb=kblk)))
        else:
            in_specs.append(pl.BlockSpec((k, tn), lambda j, i: (0, j)))
        args += [a, b]
    if has_res:
        in_specs.append(pl.BlockSpec((tm, tn), lambda j, i: (i, j)))
        args.append(res)
    return pl.pallas_call(
        body, grid=(n // tn, m // tm), in_specs=in_specs,
        out_specs=pl.BlockSpec((tm, tn), lambda j, i: (i, j)),
        out_shape=jax.ShapeDtypeStruct((m, n), out_dtype),
        compiler_params=_cp("parallel", "parallel"), name=name)(*args)


def _mm_tn(a, b, *, name):
    m, k = a.shape
    n = b.shape[1]
    tm = _tile(m, ROW_TILE, 8)
    tk = _tile(k, 1408, 128)
    tn = _tile(n, 1408, 128)

    def body(a_ref, b_ref, o_ref):
        @pl.when(pl.program_id(2) == 0)
        def _():
            o_ref[...] = jnp.zeros_like(o_ref)

        o_ref[...] += _dot_tn(a_ref[...].astype(BF), b_ref[...].astype(BF))

    return pl.pallas_call(
        body, grid=(k // tk, n // tn, m // tm),
        in_specs=[pl.BlockSpec((tm, tk), lambda kk, j, i: (i, kk)),
                  pl.BlockSpec((tm, tn), lambda kk, j, i: (i, j))],
        out_specs=pl.BlockSpec((tk, tn), lambda kk, j, i: (kk, j)),
        out_shape=jax.ShapeDtypeStruct((k, n), F32),
        compiler_params=_cp("parallel", "parallel", "arbitrary"), name=name)(a, b)


def _rmsnorm_fwd(x, g, *, name):
    s, d = x.shape
    tm = _tile(s, ROW_TILE, 8)

    def body(x_ref, g_ref, h_ref):
        xv = x_ref[...]
        y = xv * lax.rsqrt(jnp.mean(xv * xv, axis=-1, keepdims=True) + EPS)
        h_ref[...] = (y * g_ref[...]).astype(BF)

    return pl.pallas_call(
        body, grid=(s // tm,),
        in_specs=[pl.BlockSpec((tm, d), lambda i: (i, 0)), pl.BlockSpec((1, d), lambda i: (0, 0))],
        out_specs=pl.BlockSpec((tm, d), lambda i: (i, 0)),
        out_shape=jax.ShapeDtypeStruct((s, d), BF),
        compiler_params=_cp("parallel"), name=name)(x, g.reshape(1, d))


def _rms_bwd_math(xv, g, dh, n):
    rstd = lax.rsqrt(jnp.sum(xv * xv, axis=-1, keepdims=True) * (1.0 / n) + EPS)
    xhat = xv * rstd
    dxh = dh * g
    dx = rstd * (dxh - xhat * (jnp.sum(dxh * xhat, axis=-1, keepdims=True) * (1.0 / n)))
    return dx, xhat


def _rmsnorm_bwd(x, g, dh, dres, *, name):
    s, d = x.shape
    tm = _tile(s, ROW_TILE, 8)

    def body(x_ref, g_ref, dh_ref, dres_ref, dx_ref, dg_ref):
        @pl.when(pl.program_id(0) == 0)
        def _():
            dg_ref[...] = jnp.zeros_like(dg_ref)

        dhv = dh_ref[...]
        dx, xhat = _rms_bwd_math(x_ref[...], g_ref[...], dhv, d)
        dx_ref[...] = dres_ref[...] + dx
        dg_ref[0:1, :] += jnp.sum(dhv * xhat, axis=0, keepdims=True)

    row = pl.BlockSpec((tm, d), lambda i: (i, 0))
    dx, dg = pl.pallas_call(
        body, grid=(s // tm,),
        in_specs=[row, pl.BlockSpec((1, d), lambda i: (0, 0)), row, row],
        out_specs=[row, pl.BlockSpec((8, d), lambda i: (0, 0))],
        out_shape=[jax.ShapeDtypeStruct((s, d), F32), jax.ShapeDtypeStruct((8, d), F32)],
        compiler_params=_cp("arbitrary"), name=name)(x, g.reshape(1, d), dh, dres)
    return dx, dg[0:1]


def _loss_fwd_bwd(y, target, *, name):
    s, d = y.shape
    tm = _tile(s, ROW_TILE, 8)

    def body(y_ref, t_ref, loss_ref, dy_ref):
        @pl.when(pl.program_id(0) == 0)
        def _():
            loss_ref[...] = jnp.zeros_like(loss_ref)

        err = y_ref[...] - t_ref[...]
        dy_ref[...] = err * (1.0 / d)
        loss_ref[...] += 0.5 * jnp.sum(jnp.sum(err * err, axis=-1, keepdims=True) * (1.0 / d))

    row = pl.BlockSpec((tm, d), lambda i: (i, 0))
    return pl.pallas_call(
        body, grid=(s // tm,), in_specs=[row, row],
        out_specs=[pl.BlockSpec((8, LANES), lambda i: (0, 0)), row],
        out_shape=[jax.ShapeDtypeStruct((8, LANES), F32), jax.ShapeDtypeStruct((s, d), F32)],
        compiler_params=_cp("arbitrary"), name=name)(y, target)


def _rope_rows():
    lane = jnp.arange(LANES)
    fa = ROPE_THETA ** (-jnp.arange(0, A_ROT_DIM, 2, dtype=F32) / A_ROT_DIM)
    fc = ROPE_THETA ** (-jnp.arange(0, C_ROPE, 2, dtype=F32) / C_ROPE)
    ha, hc = A_ROT_DIM // 2, C_ROPE // 2
    in_a = lane < A_ROT_DIM
    in_c = (lane >= C_NOPE) & (lane < C_QK)
    freq_a = jnp.where(in_a, fa[lane % ha], 0.0)
    freq_c = jnp.where(in_c, fc[(lane - C_NOPE) % hc], 0.0)
    sign_a = jnp.where(in_a, jnp.where(lane < ha, -1.0, 1.0), 0.0)
    sign_c = jnp.where(in_c, jnp.where(lane < C_NOPE + hc, -1.0, 1.0), 0.0)
    return jnp.stack([freq_a, sign_a, freq_c, sign_c] + [jnp.zeros(LANES)] * 4).astype(F32)


def _rope_tables(pos_col, *, name):
    s = pos_col.shape[0]
    tm = _tile(s, ROW_TILE, 8)

    def body(pos_ref, rows_ref, ca_ref, sa_ref, cc_ref, sc_ref):
        p = pos_ref[...].astype(F32)
        ang_a = p * rows_ref[0:1, :]
        ang_c = p * rows_ref[2:3, :]
        ca_ref[...] = jnp.cos(ang_a)
        sa_ref[...] = jnp.sin(ang_a) * rows_ref[1:2, :]
        cc_ref[...] = jnp.cos(ang_c)
        sc_ref[...] = jnp.sin(ang_c) * rows_ref[3:4, :]

    tab = pl.BlockSpec((tm, LANES), lambda i: (i, 0))
    return pl.pallas_call(
        body, grid=(s // tm,),
        in_specs=[pl.BlockSpec((tm, 1), lambda i: (i, 0)), pl.BlockSpec((8, LANES), lambda i: (0, 0))],
        out_specs=[tab] * 4, out_shape=[jax.ShapeDtypeStruct((s, LANES), F32)] * 4,
        compiler_params=_cp("parallel"), name=name)(pos_col, _rope_rows())


def _rope(xv, cos, sin, lo, half):
    lane = lax.broadcasted_iota(jnp.int32, xv.shape, 1)
    partner = jnp.where(lane < lo + half, pltpu.roll(xv, LANES - half, 1), pltpu.roll(xv, half, 1))
    return xv * cos + partner * sin


def _slot(ref, t):
    return ref[:, t * LANES:(t + 1) * LANES]


def _head_norm_rope(xv, g, cos, sin, n, lo, half):
    y = xv * lax.rsqrt(jnp.sum(xv * xv, axis=-1, keepdims=True) * (1.0 / n) + EPS) * g
    return _rope(y, cos, sin, lo, half)


def _head_norm_rope_bwd(xv, g, dy, cos, sin, n, lo, half):
    dyn = _rope(dy, cos, -sin, lo, half)
    dx, xhat = _rms_bwd_math(xv, g, dyn, n)
    return dx, jnp.sum(dyn * xhat, axis=0, keepdims=True)


def _prep_a_fwd(qkv, cos, sin, gq, gk, *, name):
    s = qkv.shape[0]
    tm = _tile(s, PREP_TILE, 8)
    nq, nkv = A_HEADS, A_KV_HEADS

    def body(x_ref, c_ref, s_ref, gq_ref, gk_ref, q_ref, k_ref, v_ref):
        cv, sv = c_ref[...], s_ref[...]
        for t in range(nq):
            q_ref[:, t * LANES:(t + 1) * LANES] = _head_norm_rope(
                _slot(x_ref, t), gq_ref[...], cv, sv, A_HEAD_DIM, 0, A_ROT_DIM // 2).astype(BF)
        for t in range(nkv):
            k_ref[:, t * LANES:(t + 1) * LANES] = _head_norm_rope(
                _slot(x_ref, nq + t), gk_ref[...], cv, sv, A_HEAD_DIM, 0, A_ROT_DIM // 2).astype(BF)
        v_ref[...] = x_ref[:, (nq + nkv) * LANES:].astype(BF)

    def rows(w):
        return pl.BlockSpec((tm, w), lambda i: (i, 0))

    vec = pl.BlockSpec((1, LANES), lambda i: (0, 0))
    return pl.pallas_call(
        body, grid=(s // tm,),
        in_specs=[rows(qkv.shape[1]), rows(LANES), rows(LANES), vec, vec],
        out_specs=[rows(nq * LANES), rows(nkv * LANES), rows(nkv * LANES)],
        out_shape=[jax.ShapeDtypeStruct((s, nq * LANES), BF), jax.ShapeDtypeStruct((s, nkv * LANES), BF),
                   jax.ShapeDtypeStruct((s, nkv * LANES), BF)],
        compiler_params=_cp("parallel"), name=name)(qkv, cos, sin, gq, gk)


def _prep_a_bwd(qkv, dq, dk, dv, cos, sin, gq, gk, *, name):
    s = qkv.shape[0]
    tm = _tile(s, PREP_TILE, 8)
    nq, nkv = A_HEADS, A_KV_HEADS

    def body(x_ref, dq_ref, dk_ref, dv_ref, c_ref, s_ref, gq_ref, gk_ref, dx_ref, dg_ref):
        @pl.when(pl.program_id(0) == 0)
        def _():
            dg_ref[...] = jnp.zeros_like(dg_ref)

        cv, sv = c_ref[...], s_ref[...]
        dgq = jnp.zeros((1, LANES), F32)
        dgk = jnp.zeros((1, LANES), F32)
        for t in range(nq):
            dx, dg = _head_norm_rope_bwd(_slot(x_ref, t), gq_ref[...], _slot(dq_ref, t), cv, sv,
                                         A_HEAD_DIM, 0, A_ROT_DIM // 2)
            dx_ref[:, t * LANES:(t + 1) * LANES] = dx.astype(BF)
            dgq = dgq + dg
        for t in range(nkv):
            dx, dg = _head_norm_rope_bwd(_slot(x_ref, nq + t), gk_ref[...], _slot(dk_ref, t), cv, sv,
                                         A_HEAD_DIM, 0, A_ROT_DIM // 2)
            dx_ref[:, (nq + t) * LANES:(nq + t + 1) * LANES] = dx.astype(BF)
            dgk = dgk + dg
        dx_ref[:, (nq + nkv) * LANES:] = dv_ref[...].astype(BF)
        dg_ref[0:1, :] += dgq
        dg_ref[1:2, :] += dgk

    def rows(w):
        return pl.BlockSpec((tm, w), lambda i: (i, 0))

    vec = pl.BlockSpec((1, LANES), lambda i: (0, 0))
    dx, dg = pl.pallas_call(
        body, grid=(s // tm,),
        in_specs=[rows(qkv.shape[1]), rows(nq * LANES), rows(nkv * LANES), rows(nkv * LANES), rows(LANES),
                  rows(LANES), vec, vec],
        out_specs=[rows(qkv.shape[1]), pl.BlockSpec((8, LANES), lambda i: (0, 0))],
        out_shape=[jax.ShapeDtypeStruct(qkv.shape, BF), jax.ShapeDtypeStruct((8, LANES), F32)],
        compiler_params=_cp("arbitrary"), name=name)(qkv, dq, dk, dv, cos, sin, gq, gk)
    return dx, dg[0:1], dg[1:2]


def _prep_c1_fwd(dn, gq, gkv, *, name):
    s = dn.shape[0]
    tm = _tile(s, ROW_TILE, 8)

    def body(x_ref, gq_ref, gkv_ref, cq_ref, ckv_ref):
        for lo, n, g_ref, o_ref in ((0, C_Q_RANK, gq_ref, cq_ref), (C_Q_RANK, C_KV_RANK, gkv_ref, ckv_ref)):
            xv = x_ref[:, lo:lo + n]
            y = xv * lax.rsqrt(jnp.mean(xv * xv, axis=-1, keepdims=True) + EPS)
            o_ref[...] = (y * g_ref[...]).astype(BF)

    def rows(w):
        return pl.BlockSpec((tm, w), lambda i: (i, 0))

    return pl.pallas_call(
        body, grid=(s // tm,),
        in_specs=[rows(dn.shape[1]), pl.BlockSpec((1, C_Q_RANK), lambda i: (0, 0)),
                  pl.BlockSpec((1, C_KV_RANK), lambda i: (0, 0))],
        out_specs=[rows(C_Q_RANK), rows(C_KV_RANK)],
        out_shape=[jax.ShapeDtypeStruct((s, C_Q_RANK), BF), jax.ShapeDtypeStruct((s, C_KV_RANK), BF)],
        compiler_params=_cp("parallel"), name=name)(dn, gq, gkv)


def _prep_c1_bwd(dn, dcq, dckv, dkr, gq, gkv, *, name):
    s = dn.shape[0]
    tm = _tile(s, ROW_TILE, 8)

    def body(x_ref, dcq_ref, dckv_ref, dkr_ref, gq_ref, gkv_ref, dx_ref, dgq_ref, dgkv_ref):
        @pl.when(pl.program_id(0) == 0)
        def _():
            dgq_ref[...] = jnp.zeros_like(dgq_ref)
            dgkv_ref[...] = jnp.zeros_like(dgkv_ref)

        for lo, n, g_ref, d_ref, dg_ref in ((0, C_Q_RANK, gq_ref, dcq_ref, dgq_ref),
                                            (C_Q_RANK, C_KV_RANK, gkv_ref, dckv_ref, dgkv_ref)):
            dv = d_ref[...]
            dx, xhat = _rms_bwd_math(x_ref[:, lo:lo + n], g_ref[...], dv, n)
            dx_ref[:, lo:lo + n] = dx.astype(BF)
            dg_ref[0:1, :] += jnp.sum(dv * xhat, axis=0, keepdims=True)
        dx_ref[:, C_Q_RANK + C_KV_RANK:] = dkr_ref[...].astype(BF)

    def rows(w):
        return pl.BlockSpec((tm, w), lambda i: (i, 0))

    def vec(w, r=1):
        return pl.BlockSpec((r, w), lambda i: (0, 0))

    dx, dgq, dgkv = pl.pallas_call(
        body, grid=(s // tm,),
        in_specs=[rows(dn.shape[1]), rows(C_Q_RANK), rows(C_KV_RANK), rows(LANES), vec(C_Q_RANK), vec(C_KV_RANK)],
        out_specs=[rows(dn.shape[1]), vec(C_Q_RANK, 8), vec(C_KV_RANK, 8)],
        out_shape=[jax.ShapeDtypeStruct(dn.shape, BF), jax.ShapeDtypeStruct((8, C_Q_RANK), F32),
                   jax.ShapeDtypeStruct((8, C_KV_RANK), F32)],
        compiler_params=_cp("arbitrary"), name=name)(dn, dcq, dckv, dkr, gq, gkv)
    return dx, dgq[0:1], dgkv[0:1]


def _prep_c2_fwd(qraw, knope, dn, cos, sin, gq, gk, *, name):
    s = qraw.shape[0]
    tm = _tile(s, PREP_TILE, 8)
    kr_blk = dn.shape[1] // LANES - 1

    def body(q_ref, kn_ref, kr_ref, c_ref, s_ref, gq_ref, gk_ref, qo_ref, ko_ref):
        cv, sv, kr = c_ref[...], s_ref[...], kr_ref[...]
        for t in range(C_HEADS):
            qo_ref[:, t * LANES:(t + 1) * LANES] = _head_norm_rope(
                _slot(q_ref, t), gq_ref[...], cv, sv, C_QK, C_NOPE, C_ROPE // 2).astype(BF)
            ko_ref[:, t * LANES:(t + 1) * LANES] = _head_norm_rope(
                _slot(kn_ref, t) + kr, gk_ref[...], cv, sv, C_QK, C_NOPE, C_ROPE // 2).astype(BF)

    def rows(w, blk=0):
        return pl.BlockSpec((tm, w), lambda i: (i, blk))

    vec = pl.BlockSpec((1, LANES), lambda i: (0, 0))
    w = C_HEADS * LANES
    return pl.pallas_call(
        body, grid=(s // tm,),
        in_specs=[rows(w), rows(w), rows(LANES, kr_blk), rows(LANES), rows(LANES), vec, vec],
        out_specs=[rows(w), rows(w)],
        out_shape=[jax.ShapeDtypeStruct((s, w), BF)] * 2,
        compiler_params=_cp("parallel"), name=name)(qraw, knope, dn, cos, sin, gq, gk)


def _prep_c2_bwd(qraw, knope, dn, dq, dk, cos, sin, gq, gk, *, name):
    s = qraw.shape[0]
    tm = _tile(s, PREP_TILE, 8)
    kr_blk = dn.shape[1] // LANES - 1

    def body(q_ref, kn_ref, kr_ref, dq_ref, dk_ref, c_ref, s_ref, gq_ref, gk_ref,
             dqo_ref, dkno_ref, dkr_ref, dg_ref):
        @pl.when(pl.program_id(0) == 0)
        def _():
            dg_ref[...] = jnp.zeros_like(dg_ref)

        cv, sv, kr = c_ref[...], s_ref[...], kr_ref[...]
        dgq = jnp.zeros((1, LANES), F32)
        dgk = jnp.zeros((1, LANES), F32)
        dkr = jnp.zeros((tm, LANES), F32)
        for t in range(C_HEADS):
            dx, dg = _head_norm_rope_bwd(_slot(q_ref, t), gq_ref[...], _slot(dq_ref, t), cv, sv,
                                         C_QK, C_NOPE, C_ROPE // 2)
            dqo_ref[:, t * LANES:(t + 1) * LANES] = dx.astype(BF)
            dgq = dgq + dg
            dx, dg = _head_norm_rope_bwd(_slot(kn_ref, t) + kr, gk_ref[...], _slot(dk_ref, t), cv, sv,
                                         C_QK, C_NOPE, C_ROPE // 2)
            dkno_ref[:, t * LANES:(t + 1) * LANES] = dx.astype(BF)
            dkr = dkr + dx
            dgk = dgk + dg
        lane = lax.broadcasted_iota(jnp.int32, dkr.shape, 1)
        dkr_ref[...] = jnp.where((lane >= C_NOPE) & (lane < C_QK), dkr, 0.0)
        dg_ref[0:1, :] += dgq
        dg_ref[1:2, :] += dgk

    def rows(w, blk=0):
        return pl.BlockSpec((tm, w), lambda i: (i, blk))

    vec = pl.BlockSpec((1, LANES), lambda i: (0, 0))
    w = C_HEADS * LANES
    dqo, dkno, dkr, dg = pl.pallas_call(
        body, grid=(s // tm,),
        in_specs=[rows(w), rows(w), rows(LANES, kr_blk), rows(w), rows(w), rows(LANES), rows(LANES), vec, vec],
        out_specs=[rows(w), rows(w), rows(LANES), pl.BlockSpec((8, LANES), lambda i: (0, 0))],
        out_shape=[jax.ShapeDtypeStruct((s, w), BF), jax.ShapeDtypeStruct((s, w), BF),
                   jax.ShapeDtypeStruct((s, LANES), F32), jax.ShapeDtypeStruct((8, LANES), F32)],
        compiler_params=_cp("arbitrary"), name=name)(qraw, knope, dn, dq, dk, cos, sin, gq, gk)
    return dqo, dkno, dkr, dg[0:1], dg[1:2]


def _lane_pick(mat, g):
    lane = lax.broadcasted_iota(jnp.int32, mat.shape, 1)
    return jnp.sum(jnp.where(lane == g, mat, 0.0), axis=-1, keepdims=True)


def _swa_fwd(q, k, v, sinks, *, name):
    s = q.shape[0]
    nblk = s // BLOCK
    scale = A_HEAD_DIM ** -0.5
    gw = A_GROUP * LANES

    def body(sink_ref, q_ref, kc_ref, kp_ref, vc_ref, vp_ref, o_ref, lse_ref):
        kvh, n = pl.program_id(0), pl.program_id(1)
        rows = A_GROUP * BLOCK
        qs = jnp.concatenate([_slot(q_ref, g) for g in range(A_GROUP)], axis=0)
        k2 = jnp.concatenate([kp_ref[...], kc_ref[...]], axis=0)
        v2 = jnp.concatenate([vp_ref[...], vc_ref[...]], axis=0)
        r = lax.broadcasted_iota(jnp.int32, (rows, 2 * BLOCK), 0) & (BLOCK - 1)
        c = lax.broadcasted_iota(jnp.int32, (rows, 2 * BLOCK), 1)
        ok = ((c < BLOCK) & (c > r) & (n > 0)) | ((c >= BLOCK) & (c - BLOCK <= r))
        head = lax.broadcasted_iota(jnp.int32, (rows, 1), 0) >> 7
        sink = jnp.zeros((rows, 1), F32)
        for g in range(A_GROUP):
            sink = jnp.where(head == g, sink_ref[kvh * A_GROUP + g], sink)
        sc = jnp.where(ok, _dot_nt(qs, k2) * scale, NEG)
        m = jnp.maximum(jnp.max(sc, axis=-1, keepdims=True), sink)
        p = jnp.exp(sc - m)
        l = jnp.sum(p, axis=-1, keepdims=True) + jnp.exp(sink - m)
        o = _dot((p * (1.0 / l)).astype(BF), v2)
        lse = m + jnp.log(l)
        lane = lax.broadcasted_iota(jnp.int32, (BLOCK, LANES), 1)
        lse_mat = jnp.zeros((BLOCK, LANES), F32)
        for g in range(A_GROUP):
            o_ref[:, g * LANES:(g + 1) * LANES] = o[g * BLOCK:(g + 1) * BLOCK].astype(BF)
            lse_mat = jnp.where(lane == g, lse[g * BLOCK:(g + 1) * BLOCK], lse_mat)
        lse_ref[0] = lse_mat

    cur = pl.BlockSpec((BLOCK, LANES), lambda h, n: (n, h))
    prev = pl.BlockSpec((BLOCK, LANES), lambda h, n: (jnp.maximum(n - 1, 0), h))
    return pl.pallas_call(
        body, grid=(A_KV_HEADS, nblk),
        in_specs=[pl.BlockSpec(memory_space=pltpu.SMEM), pl.BlockSpec((BLOCK, gw), lambda h, n: (n, h)),
                  cur, prev, cur, prev],
        out_specs=[pl.BlockSpec((BLOCK, gw), lambda h, n: (n, h)),
                   pl.BlockSpec((1, BLOCK, LANES), lambda h, n: (h, n, 0))],
        out_shape=[jax.ShapeDtypeStruct(q.shape, BF), jax.ShapeDtypeStruct((A_KV_HEADS, s, LANES), F32)],
        compiler_params=_cp("parallel", "parallel"), name=name)(sinks, q, k, k, v, v)


def _swa_bwd(q, k, v, o, do, lse, sinks, *, name):
    s = q.shape[0]
    nblk = s // BLOCK
    scale = A_HEAD_DIM ** -0.5
    gw = A_GROUP * LANES

    def body(sink_ref, qc_ref, qn_ref, k_ref, v_ref, oc_ref, on_ref, doc_ref, don_ref, lc_ref, ln_ref,
             dq_ref, dk_ref, dv_ref, dsink_ref, carry):
        kvh, j = pl.program_id(0), pl.program_id(1)

        @pl.when(j == 0)
        def _():
            carry[...] = jnp.zeros_like(carry)
            dsink_ref[...] = jnp.zeros_like(dsink_ref)

        half = A_GROUP * BLOCK
        rows = 2 * half
        kv, vv = k_ref[...], v_ref[...]

        def stack(c_ref, n_ref):
            return jnp.concatenate([_slot(c_ref, g) for g in range(A_GROUP)]
                                   + [_slot(n_ref, g) for g in range(A_GROUP)], axis=0)

        qs, osk, dos = stack(qc_ref, qn_ref), stack(oc_ref, on_ref), stack(doc_ref, don_ref)
        lse = jnp.concatenate([_lane_pick(lc_ref[0], g) for g in range(A_GROUP)]
                              + [_lane_pick(ln_ref[0], g) for g in range(A_GROUP)], axis=0)
        delta = jnp.sum(dos.astype(F32) * osk.astype(F32), axis=-1, keepdims=True)
        row = lax.broadcasted_iota(jnp.int32, (rows, BLOCK), 0)
        r = row & (BLOCK - 1)
        c = lax.broadcasted_iota(jnp.int32, (rows, BLOCK), 1)
        ok = ((row < half) & (c <= r)) | ((row >= half) & (c > r) & (j < nblk - 1))
        sc = jnp.where(ok, _dot_nt(qs, kv) * scale, NEG)
        p = jnp.exp(sc - lse)
        dv_ref[...] = _dot_tn(p.astype(BF), dos)
        dp = _dot_nt(dos, vv)
        ds = (p * (dp - delta) * scale).astype(BF)
        dk_ref[...] = _dot_tn(ds, qs)
        dqs = _dot(ds, kv)
        lane8 = lax.broadcasted_iota(jnp.int32, (8, LANES), 1)
        dsink = jnp.zeros((8, LANES), F32)
        for g in range(A_GROUP):
            cur = slice(g * BLOCK, (g + 1) * BLOCK)
            dq_ref[:, g * LANES:(g + 1) * LANES] = carry[:, g * LANES:(g + 1) * LANES] + dqs[cur]
            carry[:, g * LANES:(g + 1) * LANES] = dqs[half + g * BLOCK:half + (g + 1) * BLOCK]
            p_sink = jnp.exp(sink_ref[kvh * A_GROUP + g] - lse[cur])
            dsink = dsink + jnp.where(lane8 == g, -jnp.sum(p_sink * delta[cur]), 0.0)
        dsink_ref[...] += dsink

    def nxt(n):
        return jnp.minimum(n + 1, nblk - 1)

    grp_c = pl.BlockSpec((BLOCK, gw), lambda h, n: (n, h))
    grp_n = pl.BlockSpec((BLOCK, gw), lambda h, n: (nxt(n), h))
    kvb = pl.BlockSpec((BLOCK, LANES), lambda h, n: (n, h))
    lse_c = pl.BlockSpec((1, BLOCK, LANES), lambda h, n: (h, n, 0))
    lse_n = pl.BlockSpec((1, BLOCK, LANES), lambda h, n: (h, nxt(n), 0))
    dq, dk, dv, dsink = pl.pallas_call(
        body, grid=(A_KV_HEADS, nblk),
        in_specs=[pl.BlockSpec(memory_space=pltpu.SMEM), grp_c, grp_n, kvb, kvb, grp_c, grp_n, grp_c, grp_n,
                  lse_c, lse_n],
        out_specs=[grp_c, kvb, kvb, pl.BlockSpec((8, LANES), lambda h, n: (h, 0))],
        out_shape=[jax.ShapeDtypeStruct(q.shape, F32), jax.ShapeDtypeStruct(k.shape, F32),
                   jax.ShapeDtypeStruct(v.shape, F32), jax.ShapeDtypeStruct((A_KV_HEADS * 8, LANES), F32)],
        scratch_shapes=[pltpu.VMEM((BLOCK, gw), F32)],
        compiler_params=_cp("parallel", "arbitrary"), name=name)(sinks, q, q, k, v, o, o, do, do, lse, lse)
    dsinks = dsink.reshape(A_KV_HEADS, 8, LANES)[:, 0, :A_GROUP].reshape(A_HEADS)
    return dq, dk, dv, dsinks


LOG2E = 1.4426950408889634


def _mla_fwd(q, k, v, *, name):
    s = q.shape[0]
    t = _tile(s, ATTN_TILE, LANES)
    nt = s // t
    scale = C_QK ** -0.5
    c2 = scale * LOG2E
    rep = t // LANES

    def body(q_ref, k_ref, v_ref, o_ref, lse_ref, m_sc, l_sc, acc_sc):
        qi = pl.program_id(1)
        qv = q_ref[...]
        m_sc[...] = jnp.full_like(m_sc, NEG)
        l_sc[...] = jnp.zeros_like(l_sc)
        acc_sc[...] = jnp.zeros_like(acc_sc)

        def step(j, diag):
            rows = pl.ds(pl.multiple_of(j * t, t), t)
            sc = _dot_nt(qv, k_ref[rows, :])
            if diag:
                r = lax.broadcasted_iota(jnp.int32, (t, t), 0)
                c = lax.broadcasted_iota(jnp.int32, (t, t), 1)
                sc = jnp.where(c <= r, sc, NEG)
            m_prev = m_sc[...]
            m_new = jnp.maximum(m_prev, jnp.max(sc, axis=-1, keepdims=True))
            alpha = jnp.exp2((m_prev - m_new) * c2)
            p = jnp.exp2((sc - jnp.tile(m_new, (1, rep))) * c2)
            l_sc[...] = alpha * l_sc[...] + jnp.sum(p, axis=-1, keepdims=True)
            acc_sc[...] = alpha * acc_sc[...] + _dot(p.astype(BF), v_ref[rows, :])
            m_sc[...] = m_new

        def loop_body(j, carry):
            step(j, False)
            return carry

        lax.fori_loop(0, qi, loop_body, 0)
        step(qi, True)
        o_ref[...] = (acc_sc[...] / l_sc[...]).astype(BF)
        lse2 = m_sc[...] * c2 + jnp.log(l_sc[...]) * LOG2E
        lse_ref[0, 0] = jnp.transpose(lse2)[0:8, :]

    qspec = pl.BlockSpec((t, LANES), lambda h, i: (i, h))
    kspec = pl.BlockSpec((s, LANES), lambda h, i: (0, h))
    return pl.pallas_call(
        body, grid=(C_HEADS, nt), in_specs=[qspec, kspec, kspec],
        out_specs=[qspec, pl.BlockSpec((1, 1, 8, t), lambda h, i: (h, i, 0, 0))],
        out_shape=[jax.ShapeDtypeStruct(q.shape, BF), jax.ShapeDtypeStruct((C_HEADS, nt, 8, t), F32)],
        scratch_shapes=[pltpu.VMEM((t, LANES), F32)] * 3,
        compiler_params=_cp("parallel", "parallel"), name=name)(q, k, v)


def _mla_delta(o, do, *, name):
    s = o.shape[0]
    t = _tile(s, ATTN_TILE, LANES)
    nt = s // t

    def body(o_ref, do_ref, d_ref):
        prod = jnp.transpose(o_ref[...].astype(F32) * do_ref[...].astype(F32))
        d_ref[0, 0] = jnp.broadcast_to(jnp.sum(prod, axis=0, keepdims=True), (8, t))

    blk = pl.BlockSpec((t, LANES), lambda h, i: (i, h))
    return pl.pallas_call(
        body, grid=(C_HEADS, nt), in_specs=[blk, blk],
        out_specs=pl.BlockSpec((1, 1, 8, t), lambda h, i: (h, i, 0, 0)),
        out_shape=jax.ShapeDtypeStruct((C_HEADS, nt, 8, t), F32),
        compiler_params=_cp("parallel", "parallel"), name=name)(o, do)


def _mla_bwd(q, k, v, do, lse2, delta, *, name):
    s = q.shape[0]
    t = _tile(s, ATTN_TILE, LANES)
    nt = s // t
    scale = C_QK ** -0.5
    c2 = scale * LOG2E

    def body(q_ref, do_ref, k_ref, v_ref, lse_ref, dl_ref, dq_ref, dk_ref, dv_ref):
        kj = pl.program_id(1)

        @pl.when(kj == 0)
        def _():
            dq_ref[...] = jnp.zeros_like(dq_ref)

        dk_ref[...] = jnp.zeros_like(dk_ref)
        dv_ref[...] = jnp.zeros_like(dv_ref)
        kv, vv = k_ref[...], v_ref[...]

        def step(i, diag):
            rows = pl.ds(pl.multiple_of(i * t, t), t)
            qv, dov = q_ref[rows, :], do_ref[rows, :]
            st = _dot_nt(kv, qv)
            if diag:
                r = lax.broadcasted_iota(jnp.int32, (t, t), 0)
                c = lax.broadcasted_iota(jnp.int32, (t, t), 1)
                st = jnp.where(r <= c, st, NEG)
            pt = jnp.exp2(st * c2 - lse_ref[0, i, 0:1, :])
            dv_ref[...] += _dot(pt.astype(BF), dov)
            dpt = _dot_nt(vv, dov)
            dst = (pt * (dpt - dl_ref[0, i, 0:1, :]) * scale).astype(BF)
            dk_ref[...] += _dot(dst, qv)
            dq_ref[rows, :] += _dot_tn(dst, kv)

        def loop_body(i, carry):
            step(i, False)
            return carry

        step(kj, True)
        lax.fori_loop(kj + 1, nt, loop_body, 0)

    res = pl.BlockSpec((s, LANES), lambda h, j: (0, h))
    kspec = pl.BlockSpec((t, LANES), lambda h, j: (j, h))
    stat = pl.BlockSpec((1, nt, 8, t), lambda h, j: (h, 0, 0, 0))
    return pl.pallas_call(
        body, grid=(C_HEADS, nt), in_specs=[res, res, kspec, kspec, stat, stat],
        out_specs=[res, kspec, kspec],
        out_shape=[jax.ShapeDtypeStruct(q.shape, F32)] * 3,
        compiler_params=_cp("parallel", "arbitrary"), name=name)(q, do, k, v, lse2, delta)


def _conv_shifted(z, zprev, tm):
    row = lax.broadcasted_iota(jnp.int32, z.shape, 0)
    z1 = jnp.where(row == 0, zprev[7:8], pltpu.roll(z, 1, 0))
    z2 = jnp.where(row == 0, zprev[6:7], jnp.where(row == 1, zprev[7:8], pltpu.roll(z, 2, 0)))
    return z1, z2


def _conv_fwd(proj, w, *, name):
    s = proj.shape[0]
    d = D_MODEL
    tm = _tile(s, PREP_TILE, 8)

    def body(x_ref, xp_ref, w_ref, y_ref):
        i = pl.program_id(0)
        z = x_ref[:, d:2 * d] * x_ref[:, 2 * d:]
        zprev = jnp.where(i > 0, xp_ref[:, d:2 * d] * xp_ref[:, 2 * d:], 0.0)
        z1, z2 = _conv_shifted(z, zprev, tm)
        y = w_ref[0:1, :] * z2 + w_ref[1:2, :] * z1 + w_ref[2:3, :] * z
        y_ref[...] = (x_ref[:, :d] * y).astype(BF)

    per8 = tm // 8
    return pl.pallas_call(
        body, grid=(s // tm,),
        in_specs=[pl.BlockSpec((tm, 3 * d), lambda i: (i, 0)),
                  pl.BlockSpec((8, 3 * d), lambda i: (jnp.maximum(i * per8 - 1, 0), 0)),
                  pl.BlockSpec((8, d), lambda i: (0, 0))],
        out_specs=pl.BlockSpec((tm, d), lambda i: (i, 0)),
        out_shape=jax.ShapeDtypeStruct((s, d), BF),
        compiler_params=_cp("parallel"), name=name)(proj, proj, w)


def _conv_bwd(proj, dyb, w, *, name):
    s = proj.shape[0]
    d = D_MODEL
    tm = _tile(s, PREP_TILE, 8)
    nt = s // tm

    def body(x_ref, xp_ref, xn_ref, dy_ref, dyn_ref, w_ref, dx_ref, dw_ref):
        i = pl.program_id(0)

        @pl.when(i == 0)
        def _():
            dw_ref[...] = jnp.zeros_like(dw_ref)

        b, c, u = x_ref[:, :d], x_ref[:, d:2 * d], x_ref[:, 2 * d:]
        z = c * u
        zprev = jnp.where(i > 0, xp_ref[:, d:2 * d] * xp_ref[:, 2 * d:], 0.0)
        z1, z2 = _conv_shifted(z, zprev, tm)
        w0, w1, w2 = w_ref[0:1, :], w_ref[1:2, :], w_ref[2:3, :]
        y = w0 * z2 + w1 * z1 + w2 * z
        dyb_v = dy_ref[...]
        dyc = dyb_v * b
        dyn = jnp.where(i < nt - 1, dyn_ref[...] * xn_ref[:, :d], 0.0)
        row = lax.broadcasted_iota(jnp.int32, dyc.shape, 0)
        d1 = jnp.where(row == tm - 1, dyn[0:1], pltpu.roll(dyc, tm - 1, 0))
        d2 = jnp.where(row == tm - 1, dyn[1:2], jnp.where(row == tm - 2, dyn[0:1], pltpu.roll(dyc, tm - 2, 0)))
        dz = w2 * dyc + w1 * d1 + w0 * d2
        dx_ref[:, :d] = (dyb_v * y).astype(BF)
        dx_ref[:, d:2 * d] = (dz * u).astype(BF)
        dx_ref[:, 2 * d:] = (dz * c).astype(BF)
        dw_ref[0:1, :] += jnp.sum(dyc * z2, axis=0, keepdims=True)
        dw_ref[1:2, :] += jnp.sum(dyc * z1, axis=0, keepdims=True)
        dw_ref[2:3, :] += jnp.sum(dyc * z, axis=0, keepdims=True)

    per8 = tm // 8
    last8 = s // 8 - 1
    dx, dw = pl.pallas_call(
        body, grid=(nt,),
        in_specs=[pl.BlockSpec((tm, 3 * d), lambda i: (i, 0)),
                  pl.BlockSpec((8, 3 * d), lambda i: (jnp.maximum(i * per8 - 1, 0), 0)),
                  pl.BlockSpec((8, 3 * d), lambda i: (jnp.minimum((i + 1) * per8, last8), 0)),
                  pl.BlockSpec((tm, d), lambda i: (i, 0)),
                  pl.BlockSpec((8, d), lambda i: (jnp.minimum((i + 1) * per8, last8), 0)),
                  pl.BlockSpec((8, d), lambda i: (0, 0))],
        out_specs=[pl.BlockSpec((tm, 3 * d), lambda i: (i, 0)), pl.BlockSpec((8, d), lambda i: (0, 0))],
        out_shape=[jax.ShapeDtypeStruct((s, 3 * d), BF), jax.ShapeDtypeStruct((8, d), F32)],
        compiler_params=_cp("arbitrary"), name=name)(proj, proj, proj, dyb, dyb, w)
    return dx, dw[0:3]


def _ffn_up(h, wgu, *, name):
    s, d = h.shape
    f = wgu.shape[1] // 2
    tm = _tile(s, ROW_TILE, 8)
    tn = _tile(f, 1408, 128)
    nb = f // tn

    def body(h_ref, wg_ref, wu_ref, g_ref, u_ref, a_ref):
        hv = h_ref[...]
        g = _dot(hv, wg_ref[...])
        u = _dot(hv, wu_ref[...])
        g_ref[...] = g.astype(BF)
        u_ref[...] = u.astype(BF)
        a_ref[...] = (g * jax.nn.sigmoid(g) * u).astype(BF)

    out = pl.BlockSpec((tm, tn), lambda j, i: (i, j))
    return pl.pallas_call(
        body, grid=(nb, s // tm),
        in_specs=[pl.BlockSpec((tm, d), lambda j, i: (i, 0)),
                  pl.BlockSpec((d, tn), lambda j, i: (0, j)),
                  pl.BlockSpec((d, tn), lambda j, i: (0, j + nb))],
        out_specs=[out] * 3, out_shape=[jax.ShapeDtypeStruct((s, f), BF)] * 3,
        compiler_params=_cp("parallel", "parallel"), name=name)(h, wgu, wgu)


def _ffn_bwd_mid(dy, wd, gate, up, *, name):
    s, d = dy.shape
    f = wd.shape[0]
    tm = _tile(s, ROW_TILE, 8)
    tn = _tile(f, 1408, 128)

    def body(dy_ref, wd_ref, g_ref, u_ref, dg_ref, du_ref):
        dact = _dot_nt(dy_ref[...].astype(BF), wd_ref[...])
        g = g_ref[...].astype(F32)
        u = u_ref[...].astype(F32)
        sig = jax.nn.sigmoid(g)
        dg_ref[...] = (dact * u * (sig * (1.0 + g * (1.0 - sig)))).astype(BF)
        du_ref[...] = (dact * (g * sig)).astype(BF)

    blk = pl.BlockSpec((tm, tn), lambda j, i: (i, j))
    return pl.pallas_call(
        body, grid=(f // tn, s // tm),
        in_specs=[pl.BlockSpec((tm, d), lambda j, i: (i, 0)), pl.BlockSpec((tn, d), lambda j, i: (j, 0)), blk, blk],
        out_specs=[blk, blk], out_shape=[jax.ShapeDtypeStruct((s, f), BF)] * 2,
        compiler_params=_cp("parallel", "parallel"), name=name)(dy, wd, gate, up)


def _pad_cols(w, heads, dim):
    k = w.shape[0]
    return jnp.pad(w.reshape(k, heads, dim), ((0, 0), (0, 0), (0, LANES - dim))).reshape(k, heads * LANES)


def _unpad_cols(w, heads, dim):
    k = w.shape[0]
    return w.reshape(k, heads, LANES)[:, :, :dim].reshape(k, heads * dim)


def _pad_rows(w, heads, dim):
    n = w.shape[1]
    return jnp.pad(w.reshape(heads, dim, n), ((0, 0), (0, LANES - dim), (0, 0))).reshape(heads * LANES, n)


def _unpad_rows(w, heads, dim):
    n = w.shape[1]
    return w.reshape(heads, LANES, n)[:, :dim, :].reshape(heads * dim, n)


def _pad_vec(g):
    return jnp.pad(g, (0, LANES - g.shape[0])).reshape(1, LANES)


def _local_step(x, positions, w, target):
    s = x.shape[0]
    cos_a, sin_a, cos_c, sin_c = _rope_tables(positions.reshape(s, 1), name="rope_tables")
    grads = {n: [None] * w[n].shape[0] for n in WEIGHTS}
    saved = []

    for i in range(DEPTH):
        kind, j = i % N_MIXERS, i // N_MIXERS
        tag = f"l{i}"
        h = _rmsnorm_fwd(x, w['mix_norm'][i], name=f"{tag}_mix_norm")
        if kind == 0:
            wqkv = _pad_cols(w['a_w_qkv'][j], A_HEADS + 2 * A_KV_HEADS, A_HEAD_DIM)
            wo = _pad_rows(w['a_w_o'][j], A_HEADS, A_HEAD_DIM)
            gq, gk = _pad_vec(w['a_q_norm'][j]), _pad_vec(w['a_k_norm'][j])
            qkv = _mm([(h, wqkv, 0)], out_dtype=F32, name=f"{tag}_qkv")
            qa, ka, va = _prep_a_fwd(qkv, cos_a, sin_a, gq, gk, name=f"{tag}_prep")
            o, lse = _swa_fwd(qa, ka, va, w['a_sinks'][j], name=f"{tag}_attn")
            x1 = _mm([(o, wo, 0)], out_dtype=F32, res=x, name=f"{tag}_wo")
            mix = dict(wqkv=wqkv, wo=wo, gq=gq, gk=gk, qkv=qkv, qa=qa, ka=ka, va=va, o=o, lse=lse)
        elif kind == 1:
            cw = jnp.pad(w['b_conv_w'][j], ((0, 5), (0, 0)))
            proj = _mm([(h, w['b_w_in'][j], 0)], out_dtype=F32, name=f"{tag}_win")
            yb = _conv_fwd(proj, cw, name=f"{tag}_conv")
            x1 = _mm([(yb, w['b_w_out'][j], 0)], out_dtype=F32, res=x, name=f"{tag}_wout")
            mix = dict(cw=cw, proj=proj, yb=yb)
        else:
            wdn = w['c_w_down'][j]
            nqk = C_Q_RANK + C_KV_RANK
            zeros = functools.partial(jnp.zeros, dtype=wdn.dtype)
            wdn = jnp.concatenate([wdn[:, :nqk], zeros((D_MODEL, C_NOPE)), wdn[:, nqk:],
                                   zeros((D_MODEL, LANES - C_QK))], axis=1)
            wq = _pad_cols(w['c_w_q_up'][j], C_HEADS, C_QK)
            wkv = w['c_w_kv_up'][j].reshape(C_KV_RANK, C_HEADS, C_NOPE + C_V)
            wkn = _pad_cols(wkv[:, :, :C_NOPE].reshape(C_KV_RANK, -1), C_HEADS, C_NOPE)
            wv = _pad_cols(wkv[:, :, C_NOPE:].reshape(C_KV_RANK, -1), C_HEADS, C_V)
            wo = _pad_rows(w['c_w_o'][j], C_HEADS, C_V)
            gq, gk = _pad_vec(w['c_q_norm'][j]), _pad_vec(w['c_k_norm'][j])
            gqa, gkva = w['c_q_a_norm'][j].reshape(1, -1), w['c_kv_a_norm'][j].reshape(1, -1)
            dn = _mm([(h, wdn, 0)], out_dtype=F32, name=f"{tag}_wdown")
            cqn, ckvn = _prep_c1_fwd(dn, gqa, gkva, name=f"{tag}_prep1")
            qraw = _mm([(cqn, wq, 0)], out_dtype=F32, name=f"{tag}_wq")
            knope = _mm([(ckvn, wkn, 0)], out_dtype=F32, name=f"{tag}_wkn")
            vc = _mm([(ckvn, wv, 0)], out_dtype=BF, name=f"{tag}_wv")
            qc, kc = _prep_c2_fwd(qraw, knope, dn, cos_c, sin_c, gq, gk, name=f"{tag}_prep2")
            o, lse = _mla_fwd(qc, kc, vc, name=f"{tag}_attn")
            x1 = _mm([(o, wo, 0)], out_dtype=F32, res=x, name=f"{tag}_wo")
            mix = dict(wdn=wdn, wq=wq, wkn=wkn, wv=wv, wo=wo, gq=gq, gk=gk, gqa=gqa, gkva=gkva, dn=dn, cqn=cqn,
                       ckvn=ckvn, qraw=qraw, knope=knope, vc=vc, qc=qc, kc=kc, o=o, lse=lse)
        h2 = _rmsnorm_fwd(x1, w['ffn_norm'][i], name=f"{tag}_ffn_norm")
        gate, up, act = _ffn_up(h2, w['f_w_gate_up'][i], name=f"{tag}_ffn_up")
        x2 = _mm([(act, w['f_w_down'][i], 0)], out_dtype=F32, res=x1, name=f"{tag}_ffn_down")
        saved.append(dict(x=x, h=h, x1=x1, h2=h2, gate=gate, up=up, act=act, mix=mix))
        x = x2

    loss_blk, dx = _loss_fwd_bwd(x, target, name="loss")

    for i in reversed(range(DEPTH)):
        kind, j = i % N_MIXERS, i // N_MIXERS
        tag = f"l{i}b"
        sv = saved[i]
        mix = sv['mix']
        wgu, wd = w['f_w_gate_up'][i], w['f_w_down'][i]
        grads['f_w_down'][i] = _mm_tn(sv['act'], dx, name=f"{tag}_dwd")
        dgate, dup = _ffn_bwd_mid(dx, wd, sv['gate'], sv['up'], name=f"{tag}_ffn_mid")
        grads['f_w_gate_up'][i] = jnp.concatenate(
            [_mm_tn(sv['h2'], dgate, name=f"{tag}_dwg"), _mm_tn(sv['h2'], dup, name=f"{tag}_dwu")], axis=1)
        dh2 = _mm([(dgate, wgu, 0), (dup, wgu, 1)], out_dtype=F32, trans_b=True, name=f"{tag}_dh2")
        dx, dg = _rmsnorm_bwd(sv['x1'], w['ffn_norm'][i], dh2, dx, name=f"{tag}_ffn_norm")
        grads['ffn_norm'][i] = dg[0]
        if kind == 0:
            grads['a_w_o'][j] = _unpad_rows(_mm_tn(mix['o'], dx, name=f"{tag}_dwo"), A_HEADS, A_HEAD_DIM)
            do = _mm([(dx, mix['wo'], 0)], out_dtype=BF, trans_b=True, name=f"{tag}_do")
            dqa, dka, dva, dsinks = _swa_bwd(mix['qa'], mix['ka'], mix['va'], mix['o'], do, mix['lse'],
                                             w['a_sinks'][j], name=f"{tag}_attn")
            dqkv, dgq, dgk = _prep_a_bwd(mix['qkv'], dqa, dka, dva, cos_a, sin_a, mix['gq'], mix['gk'],
                                         name=f"{tag}_prep")
            grads['a_sinks'][j] = dsinks
            grads['a_q_norm'][j] = dgq[0, :A_HEAD_DIM]
            grads['a_k_norm'][j] = dgk[0, :A_HEAD_DIM]
            grads['a_w_qkv'][j] = _unpad_cols(_mm_tn(sv['h'], dqkv, name=f"{tag}_dwqkv"),
                                              A_HEADS + 2 * A_KV_HEADS, A_HEAD_DIM)
            dh = _mm([(dqkv, mix['wqkv'], 0)], out_dtype=F32, trans_b=True, name=f"{tag}_dh")
        elif kind == 1:
            grads['b_w_out'][j] = _mm_tn(mix['yb'], dx, name=f"{tag}_dwout")
            dyb = _mm([(dx, w['b_w_out'][j], 0)], out_dtype=F32, trans_b=True, name=f"{tag}_dyb")
            dproj, dcw = _conv_bwd(mix['proj'], dyb, mix['cw'], name=f"{tag}_conv")
            grads['b_conv_w'][j] = dcw
            grads['b_w_in'][j] = _mm_tn(sv['h'], dproj, name=f"{tag}_dwin")
            dh = _mm([(dproj, w['b_w_in'][j], 0)], out_dtype=F32, trans_b=True, name=f"{tag}_dh")
        else:
            grads['c_w_o'][j] = _unpad_rows(_mm_tn(mix['o'], dx, name=f"{tag}_dwo"), C_HEADS, C_V)
            do = _mm([(dx, mix['wo'], 0)], out_dtype=BF, trans_b=True, name=f"{tag}_do")
            delta = _mla_delta(mix['o'], do, name=f"{tag}_delta")
            dqc, dkc, dvc = _mla_bwd(mix['qc'], mix['kc'], mix['vc'], do, mix['lse'], delta, name=f"{tag}_attn")
            dqraw, dknope, dkr, dgq, dgk = _prep_c2_bwd(mix['qraw'], mix['knope'], mix['dn'], dqc, dkc, cos_c, sin_c,
                                                        mix['gq'], mix['gk'], name=f"{tag}_prep2")
            grads['c_q_norm'][j] = dgq[0, :C_QK]
            grads['c_k_norm'][j] = dgk[0, :C_QK]
            grads['c_w_q_up'][j] = _unpad_cols(_mm_tn(mix['cqn'], dqraw, name=f"{tag}_dwq"), C_HEADS, C_QK)
            dwkn = _mm_tn(mix['ckvn'], dknope, name=f"{tag}_dwkn").reshape(C_KV_RANK, C_HEADS, LANES)
            dwv = _mm_tn(mix['ckvn'], dvc, name=f"{tag}_dwv").reshape(C_KV_RANK, C_HEADS, LANES)
            grads['c_w_kv_up'][j] = jnp.concatenate([dwkn[:, :, :C_NOPE], dwv[:, :, :C_V]], axis=2).reshape(
                C_KV_RANK, -1)
            dcq = _mm([(dqraw, mix['wq'], 0)], out_dtype=F32, trans_b=True, name=f"{tag}_dcq")
            dckv = _mm([(dknope, mix['wkn'], 0), (dvc, mix['wv'], 0)], out_dtype=F32, trans_b=True,
                       name=f"{tag}_dckv")
            ddn, dgqa, dgkva = _prep_c1_bwd(mix['dn'], dcq, dckv, dkr, mix['gqa'], mix['gkva'], name=f"{tag}_prep1")
            grads['c_q_a_norm'][j] = dgqa[0]
            grads['c_kv_a_norm'][j] = dgkva[0]
            dwdn = _mm_tn(sv['h'], ddn, name=f"{tag}_dwdown")
            nqk = C_Q_RANK + C_KV_RANK
            grads['c_w_down'][j] = jnp.concatenate([dwdn[:, :nqk], dwdn[:, nqk + C_NOPE:nqk + C_QK]], axis=1)
            dh = _mm([(ddn, mix['wdn'], 0)], out_dtype=F32, trans_b=True, name=f"{tag}_dh")
        dx, dg = _rmsnorm_bwd(sv['x'], w['mix_norm'][i], dh, dx, name=f"{tag}_mix_norm")
        grads['mix_norm'][i] = dg[0]

    return loss_blk, dx, {n: jnp.stack(g) for n, g in grads.items()}


def _my_place():
    return lax.axis_index("x"), lax.axis_index("y"), lax.axis_index("c")


def _flips(x, y):
    return [(1 - x, y), (x, 1 - y), (1 - x, 1 - y)]


def _gather_weights(big, small):
    def body(big_ref, small_ref, bout_ref, sout_ref, send_b, recv_b, send_s, recv_s, loc):
        x, y, c = _my_place()
        me = 2 * x + y

        def copies(k, src_chip, to):
            return (pltpu.make_async_remote_copy(src_ref=big_ref, dst_ref=bout_ref.at[src_chip],
                                                 send_sem=send_b.at[k], recv_sem=recv_b.at[k],
                                                 device_id=to, device_id_type=MESH),
                    pltpu.make_async_remote_copy(src_ref=small_ref, dst_ref=sout_ref.at[src_chip],
                                                 send_sem=send_s.at[k], recv_sem=recv_s.at[k],
                                                 device_id=to, device_id_type=MESH))

        own = (pltpu.make_async_copy(big_ref, bout_ref.at[me], loc.at[0]),
               pltpu.make_async_copy(small_ref, sout_ref.at[me], loc.at[1]))
        for cp in own:
            cp.start()
        sends = [copies(k, me, (px, py, c)) for k, (px, py) in enumerate(_flips(x, y))]
        for pair in sends:
            for cp in pair:
                cp.start()
        for k, (px, py) in enumerate(_flips(x, y)):
            for cp in copies(k, 2 * px + py, (x, y, c)):
                cp.wait_recv()
        for pair in sends:
            for cp in pair:
                cp.wait_send()
        for cp in own:
            cp.wait()

    hbm = pl.BlockSpec(memory_space=pltpu.HBM)
    return pl.pallas_call(
        body, in_specs=[hbm, hbm], out_specs=[hbm, hbm],
        out_shape=[jax.ShapeDtypeStruct((N_CHIPS,) + big.shape, big.dtype),
                   jax.ShapeDtypeStruct((N_CHIPS,) + small.shape, small.dtype)],
        scratch_shapes=[pltpu.SemaphoreType.DMA((3,)), pltpu.SemaphoreType.DMA((3,)),
                        pltpu.SemaphoreType.DMA((3,)), pltpu.SemaphoreType.DMA((3,)),
                        pltpu.SemaphoreType.DMA((2,))],
        name="gather_weights")(big, small)


def _exchange_grads(gbig, gsmall):
    def body(gbig_ref, gsmall_ref, got_ref, ssum_ref, sbuf, send_b, recv_b, send_s, recv_s):
        x, y, c = _my_place()
        me = 4 * x + 2 * y + c

        def big_copy(k, src_chip, to):
            return pltpu.make_async_remote_copy(src_ref=gbig_ref.at[src_chip], dst_ref=got_ref.at[k],
                                                send_sem=send_b.at[k], recv_sem=recv_b.at[k],
                                                device_id=to, device_id_type=MESH)

        def small_copy(k, slot, to):
            return pltpu.make_async_remote_copy(src_ref=gsmall_ref, dst_ref=sbuf.at[slot],
                                                send_sem=send_s.at[k], recv_sem=recv_s.at[k],
                                                device_id=to, device_id_type=MESH)

        peers = [(x ^ (k >> 2), y ^ ((k >> 1) & 1), c ^ (k & 1)) for k in range(1, N_DEV)]
        bigs = [big_copy(k, 2 * px + py, (px, py, c)) for k, (px, py) in enumerate(_flips(x, y))]
        smalls = [small_copy(k, me, p) for k, p in enumerate(peers)]
        for cp in bigs + smalls:
            cp.start()
        sbuf[me] = gsmall_ref[...]
        for k, (px, py, pc) in enumerate(peers):
            small_copy(k, 4 * px + 2 * py + pc, (x, y, c)).wait_recv()
        acc = sbuf[0]
        for d in range(1, N_DEV):
            acc = acc + sbuf[d]
        ssum_ref[...] = acc
        for k in range(3):
            big_copy(k, 0, (x, y, c)).wait_recv()
        for cp in bigs + smalls:
            cp.wait_send()

    hbm = pl.BlockSpec(memory_space=pltpu.HBM)
    vmem = pl.BlockSpec(memory_space=pltpu.VMEM)
    return pl.pallas_call(
        body, in_specs=[hbm, vmem], out_specs=[hbm, vmem],
        out_shape=[jax.ShapeDtypeStruct((3,) + gbig.shape[1:], gbig.dtype),
                   jax.ShapeDtypeStruct(gsmall.shape, gsmall.dtype)],
        scratch_shapes=[pltpu.VMEM((N_DEV,) + gsmall.shape, gsmall.dtype),
                        pltpu.SemaphoreType.DMA((3,)), pltpu.SemaphoreType.DMA((3,)),
                        pltpu.SemaphoreType.DMA((N_DEV - 1,)), pltpu.SemaphoreType.DMA((N_DEV - 1,))],
        name="exchange_grads")(gbig, gsmall)


def _sum_shards(chip, gbig, got):
    r, cols = gbig.shape[1:]
    tr = _tile(r, PACK_ROW_MULT, 8)

    def body(chip_ref, mine_ref, got_ref, o_ref):
        o_ref[...] = ((mine_ref[0] + got_ref[0]) + got_ref[1]) + got_ref[2]

    grid_spec = pltpu.PrefetchScalarGridSpec(
        num_scalar_prefetch=1, grid=(r // tr,),
        in_specs=[pl.BlockSpec((1, tr, cols), lambda i, chip_ref: (chip_ref[0], i, 0)),
                  pl.BlockSpec((3, tr, cols), lambda i, chip_ref: (0, i, 0))],
        out_specs=pl.BlockSpec((tr, cols), lambda i, chip_ref: (i, 0)))
    return pl.pallas_call(
        body, grid_spec=grid_spec, out_shape=jax.ShapeDtypeStruct((r, cols), F32),
        compiler_params=_cp("parallel"), name="sum_shards")(chip, gbig, got)


def _swap_sibling(t, *, name):
    def body(t_ref, got_ref, send_sem, recv_sem):
        x, y, c = _my_place()
        cp = pltpu.make_async_remote_copy(src_ref=t_ref, dst_ref=got_ref, send_sem=send_sem, recv_sem=recv_sem,
                                          device_id=(x, y, 1 - c), device_id_type=MESH)
        cp.start()
        cp.wait()

    hbm = pl.BlockSpec(memory_space=pltpu.HBM)
    return pl.pallas_call(
        body, in_specs=[hbm], out_specs=hbm, out_shape=jax.ShapeDtypeStruct(t.shape, t.dtype),
        scratch_shapes=[pltpu.SemaphoreType.DMA, pltpu.SemaphoreType.DMA], name=name)(t)


def _add2(a, b, *, name):
    n, r, cols = a.shape
    tr = _tile(r, PACK_ROW_MULT, 8)

    def body(a_ref, b_ref, o_ref):
        o_ref[...] = a_ref[...] + b_ref[...]

    blk = pl.BlockSpec((n, tr, cols), lambda i: (0, i, 0))
    return pl.pallas_call(
        body, grid=(r // tr,), in_specs=[blk, blk], out_specs=blk, out_shape=jax.ShapeDtypeStruct(a.shape, F32),
        compiler_params=_cp("parallel"), name=name)(a, b)


def _by_core(c, mine, sibling):
    return jnp.where(c == 0, mine, sibling), jnp.where(c == 0, sibling, mine)


def _adamw(wt, m, v, g, *, name):
    r, cols = wt.shape
    tr = _tile(r, 256, 8)

    def body(w_ref, m_ref, v_ref, g_ref, d_ref, nm_ref, nv_ref):
        gv = g_ref[...]
        nm = ADAM_B1 * m_ref[...] + (1.0 - ADAM_B1) * gv
        nv = ADAM_B2 * v_ref[...] + (1.0 - ADAM_B2) * (gv * gv)
        m_hat = nm / (1.0 - ADAM_B1 ** ADAM_STEP)
        v_hat = nv / (1.0 - ADAM_B2 ** ADAM_STEP)
        d_ref[...] = -ADAM_LR * (m_hat / (jnp.sqrt(v_hat) + ADAM_EPS) + ADAM_WD * w_ref[...])
        nm_ref[...] = nm
        nv_ref[...] = nv

    blk = pl.BlockSpec((tr, cols), lambda i: (i, 0))
    return pl.pallas_call(
        body, grid=(r // tr,), in_specs=[blk] * 4, out_specs=[blk] * 3,
        out_shape=[jax.ShapeDtypeStruct((r, cols), F32)] * 3,
        compiler_params=_cp("parallel"), name=name)(wt, m, v, g)


def _shard_shape(full, axis):
    return tuple(d // N_CHIPS if a == axis else d for a, d in enumerate(full))


def _pack_rows(n):
    rows = -(-n // PACK_COLS)
    return -(-rows // PACK_ROW_MULT) * PACK_ROW_MULT


def _pack(parts, rows, dtype):
    flat = jnp.concatenate([p.reshape(-1).astype(dtype) for p in parts])
    return jnp.pad(flat, (0, rows * PACK_COLS - flat.shape[0])).reshape(rows, PACK_COLS)


def _join_shards(stacked, axis):
    moved = jnp.moveaxis(stacked, 0, axis)
    shp = moved.shape
    return moved.reshape(shp[:axis] + (shp[axis] * shp[axis + 1],) + shp[axis + 2:])


def _split_shards(full, axis):
    shp = full.shape
    split = full.reshape(shp[:axis] + (N_CHIPS, shp[axis] // N_CHIPS) + shp[axis + 1:])
    return jnp.moveaxis(split, axis, 0)


SMALL_ROWS = 128
SMALL_LAYOUT = [('mix_norm', DEPTH * D_MODEL), ('ffn_norm', DEPTH * D_MODEL), ('b_conv_w', 3 * D_MODEL),
                ('c_q_a_norm', C_Q_RANK), ('c_kv_a_norm', C_KV_RANK), ('a_q_norm', 2 * A_HEAD_DIM),
                ('a_k_norm', 2 * A_HEAD_DIM), ('a_sinks', 2 * A_HEADS), ('c_q_norm', C_QK), ('c_k_norm', C_QK),
                ('loss', 1)]


def _small_offsets():
    offs, row = {}, 0
    for name, n in SMALL_LAYOUT:
        offs[name] = (row * LANES, n)
        row += -(-n // LANES)
    assert row <= SMALL_ROWS
    return offs


def kernel(x, positions, mix_norm, ffn_norm, a_w_qkv, a_q_norm, a_k_norm, a_sinks, a_w_o, b_w_in, b_conv_w, b_w_out, c_w_down, c_q_a_norm, c_kv_a_norm, c_w_q_up, c_w_kv_up, c_q_norm, c_k_norm, c_w_o, f_w_gate_up, f_w_down, loss_target, m_mix_norm, m_ffn_norm, m_a_w_qkv, m_a_q_norm, m_a_k_norm, m_a_sinks, m_a_w_o, m_b_w_in, m_b_conv_w, m_b_w_out, m_c_w_down, m_c_q_a_norm, m_c_kv_a_norm, m_c_w_q_up, m_c_w_kv_up, m_c_q_norm, m_c_k_norm, m_c_w_o, m_f_w_gate_up, m_f_w_down, v_mix_norm, v_ffn_norm, v_a_w_qkv, v_a_q_norm, v_a_k_norm, v_a_sinks, v_a_w_o, v_b_w_in, v_b_conv_w, v_b_w_out, v_c_w_down, v_c_q_a_norm, v_c_kv_a_norm, v_c_w_q_up, v_c_w_kv_up, v_c_q_norm, v_c_k_norm, v_c_w_o, v_f_w_gate_up, v_f_w_down):
    args = dict(locals())
    wshard = {n: args[n] for n in WEIGHTS}
    sharded = {**BIG, **SMALL_SHARDED}
    chip = 2 * lax.axis_index("x") + lax.axis_index("y")

    n_big = sum(wshard[n].size for n in BIG)
    rows = _pack_rows(n_big)
    big = _pack([wshard[n] for n in BIG], rows, BF)
    small = jnp.concatenate([wshard[n].reshape(-1) for n in SMALL_SHARDED])
    small = jnp.pad(small, (0, 8 * LANES - small.shape[0])).reshape(8, LANES)
    core = lax.axis_index("c")
    half = rows // 2
    mine_all, small_all = _gather_weights(lax.dynamic_slice_in_dim(big, core * half, half, axis=0), small)
    sib_all = _swap_sibling(mine_all, name="swap_weights")
    big_all = jnp.concatenate(_by_core(core, mine_all, sib_all), axis=1)
    big_all = big_all.reshape(N_CHIPS, -1)
    small_all = small_all.reshape(N_CHIPS, -1)
    w = {}
    off = 0
    for n, axis in BIG.items():
        sz = wshard[n].size
        w[n] = _join_shards(big_all[:, off:off + sz].reshape((N_CHIPS,) + wshard[n].shape), axis)
        off += sz
    off = 0
    for n, axis in SMALL_SHARDED.items():
        sz = wshard[n].size
        w[n] = _join_shards(small_all[:, off:off + sz].reshape((N_CHIPS,) + wshard[n].shape), axis)
        off += sz
    for n in WEIGHTS:
        if n not in sharded:
            w[n] = wshard[n]

    loss_blk, grad_x, g = _local_step(x[0], positions[0], w, loss_target[0])

    gbig = jnp.concatenate([_split_shards(g[n], BIG[n]).reshape(N_CHIPS, -1) for n in BIG], axis=1)
    gbig = jnp.pad(gbig, ((0, 0), (0, rows * PACK_COLS - n_big))).reshape(N_CHIPS, rows, PACK_COLS)
    offs = _small_offsets()
    smalls = {**{n: g[n] for n, _ in SMALL_LAYOUT if n != 'loss'}, 'loss': loss_blk[0:1, 0:1]}
    gsmall = jnp.concatenate(
        [jnp.pad(smalls[n].reshape(-1), (0, -cnt % LANES)) for n, cnt in SMALL_LAYOUT])
    gsmall = jnp.pad(gsmall, (0, SMALL_ROWS * LANES - gsmall.shape[0])).reshape(SMALL_ROWS, LANES)
    keep = lax.dynamic_slice_in_dim(gbig, core * half, half, axis=1)
    give = lax.dynamic_slice_in_dim(gbig, (1 - core) * half, half, axis=1)
    chip_sum = _add2(keep, _swap_sibling(give, name="swap_grads"), name="add_sibling")
    got, ssum = _exchange_grads(chip_sum, gsmall)
    part = _sum_shards(chip.reshape(1).astype(jnp.int32), chip_sum, got)
    part_sib = _swap_sibling(part, name="swap_sums")
    gflat = jnp.concatenate(_by_core(core, part, part_sib), axis=0).reshape(-1)

    ssum = ssum.reshape(-1)
    outs = {}
    off = 0
    for n in WEIGHTS:
        wt = wshard[n]
        if n in BIG:
            grad = gflat[off:off + wt.size]
            off += wt.size
        else:
            o0, cnt = offs[n]
            grad = ssum[o0:o0 + cnt].reshape(g[n].shape)
            if n in SMALL_SHARDED:
                grad = lax.dynamic_index_in_dim(_split_shards(grad, SMALL_SHARDED[n]), chip, 0, keepdims=False)
        shape2 = (-1, wt.shape[-1])
        grad = grad.reshape(shape2)
        res = _adamw(wt.reshape(shape2), args['m_' + n].reshape(shape2), args['v_' + n].reshape(shape2), grad,
                     name=f"adamw_{n}")
        outs[n] = [r.reshape(wt.shape) for r in [grad] + list(res)]
    loss = ssum[offs['loss'][0]]
    return (loss, grad_x[None], *[outs[n][0] for n in WEIGHTS], *[outs[n][1] for n in WEIGHTS],
            *[outs[n][2] for n in WEIGHTS], *[outs[n][3] for n in WEIGHTS])
```

```python
import functools

import jax
import jax.numpy as jnp
from jax import lax
from jax.experimental import pallas as pl
from jax.experimental.pallas import tpu as pltpu

F32 = jnp.float32
BF = jnp.bfloat16

D_MODEL = 1024
DEPTH = 4
N_MIXERS = 3
ROPE_THETA = 500000.0
EPS = 1e-6
BLOCK = 128
LANES = 128

A_HEADS, A_KV_HEADS, A_HEAD_DIM = 16, 4, 64
A_ROT_DIM = A_HEAD_DIM // 4
A_GROUP = A_HEADS // A_KV_HEADS
C_HEADS, C_NOPE, C_ROPE, C_V, C_Q_RANK, C_KV_RANK = 16, 64, 32, 64, 384, 256
C_QK = C_NOPE + C_ROPE
D_FF = 2816

ADAM_LR, ADAM_B1, ADAM_B2, ADAM_EPS, ADAM_WD, ADAM_STEP = 0.001, 0.9, 0.999, 1e-08, 0.01, 10

N_CHIPS = 4
N_DEV = 8
MESH = pl.DeviceIdType.MESH

VMEM_LIMIT = 56 * 1024 * 1024
ROW_TILE = 512
PREP_TILE = 256
ATTN_TILE = 512
FWD_WIDE = 4
SWA_BLOCKS = 4
NEG = -1e30

WEIGHTS = ['mix_norm', 'ffn_norm', 'a_w_qkv', 'a_q_norm', 'a_k_norm', 'a_sinks', 'a_w_o', 'b_w_in', 'b_conv_w',
           'b_w_out', 'c_w_down', 'c_q_a_norm', 'c_kv_a_norm', 'c_w_q_up', 'c_w_kv_up', 'c_q_norm', 'c_k_norm',
           'c_w_o', 'f_w_gate_up', 'f_w_down']
BIG = {'a_w_qkv': 2, 'a_w_o': 1, 'b_w_in': 2, 'b_w_out': 1, 'c_w_down': 1, 'c_w_q_up': 2, 'c_w_kv_up': 2,
       'c_w_o': 1, 'f_w_gate_up': 2, 'f_w_down': 1}
SMALL_SHARDED = {'b_conv_w': 2, 'c_q_a_norm': 1, 'c_kv_a_norm': 1}
PACK_COLS = 512
PACK_ROW_MULT = 16
PACK_TILE_ROWS = 512


def _cp(*sem):
    return pltpu.CompilerParams(dimension_semantics=sem, vmem_limit_bytes=VMEM_LIMIT)


def _tile(n, target, mult):
    if n <= target:
        return n
    t = (target // mult) * mult
    while t >= mult:
        if n % t == 0:
            return t
        t -= mult
    raise ValueError(f"no tile for {n}")


def _dot(a, b):
    return lax.dot_general(a, b, (((1,), (0,)), ((), ())), preferred_element_type=F32)


def _dot_nt(a, b):
    return lax.dot_general(a, b, (((1,), (1,)), ((), ())), preferred_element_type=F32)


def _dot_tn(a, b):
    return lax.dot_general(a, b, (((0,), (0,)), ((), ())), preferred_element_type=F32)


def _mm(pairs, *, out_dtype, name, res=None, trans_b=False):
    a0, b0, _ = pairs[0]
    m = a0.shape[0]
    n = b0.shape[0] if trans_b else b0.shape[1]
    tm = _tile(m, ROW_TILE, 8)
    tn = _tile(n, 1024, 128)
    n_pairs = len(pairs)
    has_res = res is not None

    def body(*refs):
        o_ref = refs[-1]
        acc = None
        for p in range(n_pairs):
            a = refs[2 * p][...].astype(BF)
            b = refs[2 * p + 1][...].astype(BF)
            d = _dot_nt(a, b) if trans_b else _dot(a, b)
            acc = d if acc is None else acc + d
        if has_res:
            acc = acc + refs[2 * n_pairs][...]
        o_ref[...] = acc.astype(out_dtype)

    in_specs, args = [], []
    for a, b, kblk in pairs:
        k = a.shape[1]
        in_specs.append(pl.BlockSpec((tm, k), lambda j, i: (i, 0)))
        if trans_b:
            in_specs.append(pl.BlockSpec((tn, k), functools.partial(lambda j, i, kb: (j, kb), kb=kblk)))
        else:
            in_specs.append(pl.BlockSpec((k, tn), lambda j, i: (0, j)))
        args += [a, b]
    if has_res:
        in_specs.append(pl.BlockSpec((tm, tn), lambda j, i: (i, j)))
        args.append(res)
    return pl.pallas_call(
        body, grid=(n // tn, m // tm), in_specs=in_specs,
        out_specs=pl.BlockSpec((tm, tn), lambda j, i: (i, j)),
        out_shape=jax.ShapeDtypeStruct((m, n), out_dtype),
        compiler_params=_cp("parallel", "parallel"), name=name)(*args)


def _mm_tn(a, b, *, name):
    m, k = a.shape
    n = b.shape[1]
    tm = _tile(m, ROW_TILE, 8)
    tk = _tile(k, 1408, 128)
    tn = _tile(n, 1408, 128)

    def body(a_ref, b_ref, o_ref):
        @pl.when(pl.program_id(2) == 0)
        def _():
            o_ref[...] = jnp.zeros_like(o_ref)

        o_ref[...] += _dot_tn(a_ref[...].astype(BF), b_ref[...].astype(BF))

    return pl.pallas_call(
        body, grid=(k // tk, n // tn, m // tm),
        in_specs=[pl.BlockSpec((tm, tk), lambda kk, j, i: (i, kk)),
                  pl.BlockSpec((tm, tn), lambda kk, j, i: (i, j))],
        out_specs=pl.BlockSpec((tk, tn), lambda kk, j, i: (kk, j)),
        out_shape=jax.ShapeDtypeStruct((k, n), F32),
        compiler_params=_cp("parallel", "parallel", "arbitrary"), name=name)(a, b)


def _rmsnorm_fwd(x, g, *, name):
    s, d = x.shape
    tm = _tile(s, ROW_TILE, 8)

    def body(x_ref, g_ref, h_ref):
        xv = x_ref[...]
        y = xv * lax.rsqrt(jnp.mean(xv * xv, axis=-1, keepdims=True) + EPS)
        h_ref[...] = (y * g_ref[...]).astype(BF)

    return pl.pallas_call(
        body, grid=(s // tm,),
        in_specs=[pl.BlockSpec((tm, d), lambda i: (i, 0)), pl.BlockSpec((1, d), lambda i: (0, 0))],
        out_specs=pl.BlockSpec((tm, d), lambda i: (i, 0)),
        out_shape=jax.ShapeDtypeStruct((s, d), BF),
        compiler_params=_cp("parallel"), name=name)(x, g.reshape(1, d))


def _rms_bwd_math(xv, g, dh, n):
    rstd = lax.rsqrt(jnp.sum(xv * xv, axis=-1, keepdims=True) * (1.0 / n) + EPS)
    xhat = xv * rstd
    dxh = dh * g
    dx = rstd * (dxh - xhat * (jnp.sum(dxh * xhat, axis=-1, keepdims=True) * (1.0 / n)))
    return dx, xhat


def _rmsnorm_bwd(x, g, dh, dres, *, name):
    s, d = x.shape
    tm = _tile(s, ROW_TILE, 8)

    def body(x_ref, g_ref, dh_ref, dres_ref, dx_ref, dg_ref):
        @pl.when(pl.program_id(0) == 0)
        def _():
            dg_ref[...] = jnp.zeros_like(dg_ref)

        dhv = dh_ref[...]
        dx, xhat = _rms_bwd_math(x_ref[...], g_ref[...], dhv, d)
        dx_ref[...] = dres_ref[...] + dx
        dg_ref[0:1, :] += jnp.sum(dhv * xhat, axis=0, keepdims=True)

    row = pl.BlockSpec((tm, d), lambda i: (i, 0))
    dx, dg = pl.pallas_call(
        body, grid=(s // tm,),
        in_specs=[row, pl.BlockSpec((1, d), lambda i: (0, 0)), row, row],
        out_specs=[row, pl.BlockSpec((8, d), lambda i: (0, 0))],
        out_shape=[jax.ShapeDtypeStruct((s, d), F32), jax.ShapeDtypeStruct((8, d), F32)],
        compiler_params=_cp("arbitrary"), name=name)(x, g.reshape(1, d), dh, dres)
    return dx, dg[0:1]


def _loss_fwd_bwd(y, target, *, name):
    s, d = y.shape
    tm = _tile(s, ROW_TILE, 8)

    def body(y_ref, t_ref, loss_ref, dy_ref):
        @pl.when(pl.program_id(0) == 0)
        def _():
            loss_ref[...] = jnp.zeros_like(loss_ref)

        err = y_ref[...] - t_ref[...]
        dy_ref[...] = err * (1.0 / d)
        loss_ref[...] += 0.5 * jnp.sum(jnp.sum(err * err, axis=-1, keepdims=True) * (1.0 / d))

    row = pl.BlockSpec((tm, d), lambda i: (i, 0))
    return pl.pallas_call(
        body, grid=(s // tm,), in_specs=[row, row],
        out_specs=[pl.BlockSpec((8, LANES), lambda i: (0, 0)), row],
        out_shape=[jax.ShapeDtypeStruct((8, LANES), F32), jax.ShapeDtypeStruct((s, d), F32)],
        compiler_params=_cp("arbitrary"), name=name)(y, target)


HALF = LANES // 2
A_HR, C_HR = A_ROT_DIM // 2, C_ROPE // 2
A_X1, C_X1 = 0, 32
LAYOUT_A = [(A_X1, 0, A_HR), (A_HR, A_ROT_DIM, A_HEAD_DIM - A_ROT_DIM), (HALF + A_X1, A_HR, A_HR)]
LAYOUT_C = [(0, 0, 32), (C_X1, C_NOPE, C_HR), (HALF, 32, 32), (HALF + C_X1, C_NOPE + C_HR, C_HR)]
LAYOUT_KN = [(0, 0, 32), (HALF, 32, 32)]
LAYOUT_KR = [(C_X1, 0, C_HR), (HALF + C_X1, C_HR, C_HR)]
LAYOUT_V = [(0, 0, C_V)]


def _rope_rows():
    lane = jnp.arange(LANES)
    fa = ROPE_THETA ** (-jnp.arange(0, A_ROT_DIM, 2, dtype=F32) / A_ROT_DIM)
    fc = ROPE_THETA ** (-jnp.arange(0, C_ROPE, 2, dtype=F32) / C_ROPE)

    def rows(f, x1, hr):
        first = (lane >= x1) & (lane < x1 + hr)
        second = (lane >= HALF + x1) & (lane < HALF + x1 + hr)
        freq = jnp.where(first | second, f[(lane - x1) % HALF % hr], 0.0)
        return freq, jnp.where(first, -1.0, jnp.where(second, 1.0, 0.0))

    freq_a, sign_a = rows(fa, A_X1, A_HR)
    freq_c, sign_c = rows(fc, C_X1, C_HR)
    return jnp.stack([freq_a, sign_a, freq_c, sign_c] + [jnp.zeros(LANES)] * 4).astype(F32)


def _rope_tables(pos_col, *, name):
    s = pos_col.shape[0]
    tm = _tile(s, ROW_TILE, 8)

    def body(pos_ref, rows_ref, ca_ref, sa_ref, cc_ref, sc_ref):
        p = pos_ref[...].astype(F32)
        ang_a = p * rows_ref[0:1, :]
        ang_c = p * rows_ref[2:3, :]
        ca_ref[...] = jnp.cos(ang_a)
        sa_ref[...] = jnp.sin(ang_a) * rows_ref[1:2, :]
        cc_ref[...] = jnp.cos(ang_c)
        sc_ref[...] = jnp.sin(ang_c) * rows_ref[3:4, :]

    tab = pl.BlockSpec((tm, LANES), lambda i: (i, 0))
    return pl.pallas_call(
        body, grid=(s // tm,),
        in_specs=[pl.BlockSpec((tm, 1), lambda i: (i, 0)), pl.BlockSpec((8, LANES), lambda i: (0, 0))],
        out_specs=[tab] * 4, out_shape=[jax.ShapeDtypeStruct((s, LANES), F32)] * 4,
        compiler_params=_cp("parallel"), name=name)(pos_col, _rope_rows())


def _rope(xv, cos, sin):
    return xv * cos + pltpu.roll(xv, HALF, 1) * sin


def _slot(ref, t):
    return ref[:, t * LANES:(t + 1) * LANES]


def _head_norm_rope(xv, g, cos, sin, n):
    y = xv * lax.rsqrt(jnp.sum(xv * xv, axis=-1, keepdims=True) * (1.0 / n) + EPS) * g
    return _rope(y, cos, sin)


def _head_norm_rope_bwd(xv, g, dy, cos, sin, n):
    dyn = _rope(dy, cos, -sin)
    dx, xhat = _rms_bwd_math(xv, g, dyn, n)
    return dx, jnp.sum(dyn * xhat, axis=0, keepdims=True)


def _prep_a_fwd(qkv, cos, sin, gq, gk, *, name):
    s = qkv.shape[0]
    tm = _tile(s, PREP_TILE, 8)
    nq, nkv = A_HEADS, A_KV_HEADS

    def body(x_ref, c_ref, s_ref, gq_ref, gk_ref, q_ref, k_ref, v_ref):
        cv, sv = c_ref[...], s_ref[...]
        for t in range(nq):
            q_ref[:, t * LANES:(t + 1) * LANES] = _head_norm_rope(
                _slot(x_ref, t), gq_ref[...], cv, sv, A_HEAD_DIM).astype(BF)
        for t in range(nkv):
            k_ref[:, t * LANES:(t + 1) * LANES] = _head_norm_rope(
                _slot(x_ref, nq + t), gk_ref[...], cv, sv, A_HEAD_DIM).astype(BF)
        v_ref[...] = x_ref[:, (nq + nkv) * LANES:].astype(BF)

    def rows(w):
        return pl.BlockSpec((tm, w), lambda i: (i, 0))

    vec = pl.BlockSpec((1, LANES), lambda i: (0, 0))
    return pl.pallas_call(
        body, grid=(s // tm,),
        in_specs=[rows(qkv.shape[1]), rows(LANES), rows(LANES), vec, vec],
        out_specs=[rows(nq * LANES), rows(nkv * LANES), rows(nkv * LANES)],
        out_shape=[jax.ShapeDtypeStruct((s, nq * LANES), BF), jax.ShapeDtypeStruct((s, nkv * LANES), BF),
                   jax.ShapeDtypeStruct((s, nkv * LANES), BF)],
        compiler_params=_cp("parallel"), name=name)(qkv, cos, sin, gq, gk)


def _prep_a_bwd(qkv, dq, dk, dv, cos, sin, gq, gk, *, name):
    s = qkv.shape[0]
    tm = _tile(s, PREP_TILE, 8)
    nq, nkv = A_HEADS, A_KV_HEADS

    def body(x_ref, dq_ref, dk_ref, dv_ref, c_ref, s_ref, gq_ref, gk_ref, dx_ref, dg_ref):
        @pl.when(pl.program_id(0) == 0)
        def _():
            dg_ref[...] = jnp.zeros_like(dg_ref)

        cv, sv = c_ref[...], s_ref[...]
        dgq = jnp.zeros((1, LANES), F32)
        dgk = jnp.zeros((1, LANES), F32)
        for t in range(nq):
            dx, dg = _head_norm_rope_bwd(_slot(x_ref, t), gq_ref[...], _slot(dq_ref, t), cv, sv,
                                         A_HEAD_DIM)
            dx_ref[:, t * LANES:(t + 1) * LANES] = dx.astype(BF)
            dgq = dgq + dg
        for t in range(nkv):
            dx, dg = _head_norm_rope_bwd(_slot(x_ref, nq + t), gk_ref[...], _slot(dk_ref, t), cv, sv,
                                         A_HEAD_DIM)
            dx_ref[:, (nq + t) * LANES:(nq + t + 1) * LANES] = dx.astype(BF)
            dgk = dgk + dg
        dx_ref[:, (nq + nkv) * LANES:] = dv_ref[...].astype(BF)
        dg_ref[0:1, :] += dgq
        dg_ref[1:2, :] += dgk

    def rows(w):
        return pl.BlockSpec((tm, w), lambda i: (i, 0))

    vec = pl.BlockSpec((1, LANES), lambda i: (0, 0))
    dx, dg = pl.pallas_call(
        body, grid=(s // tm,),
        in_specs=[rows(qkv.shape[1]), rows(nq * LANES), rows(nkv * LANES), rows(nkv * LANES), rows(LANES),
                  rows(LANES), vec, vec],
        out_specs=[rows(qkv.shape[1]), pl.BlockSpec((8, LANES), lambda i: (0, 0))],
        out_shape=[jax.ShapeDtypeStruct(qkv.shape, BF), jax.ShapeDtypeStruct((8, LANES), F32)],
        compiler_params=_cp("arbitrary"), name=name)(qkv, dq, dk, dv, cos, sin, gq, gk)
    return dx, dg[0:1], dg[1:2]


def _prep_c1_fwd(dn, gq, gkv, *, name):
    s = dn.shape[0]
    tm = _tile(s, ROW_TILE, 8)

    def body(x_ref, gq_ref, gkv_ref, cq_ref, ckv_ref):
        for lo, n, g_ref, o_ref in ((0, C_Q_RANK, gq_ref, cq_ref), (C_Q_RANK, C_KV_RANK, gkv_ref, ckv_ref)):
            xv = x_ref[:, lo:lo + n]
            y = xv * lax.rsqrt(jnp.mean(xv * xv, axis=-1, keepdims=True) + EPS)
            o_ref[...] = (y * g_ref[...]).astype(BF)

    def rows(w):
        return pl.BlockSpec((tm, w), lambda i: (i, 0))

    return pl.pallas_call(
        body, grid=(s // tm,),
        in_specs=[rows(dn.shape[1]), pl.BlockSpec((1, C_Q_RANK), lambda i: (0, 0)),
                  pl.BlockSpec((1, C_KV_RANK), lambda i: (0, 0))],
        out_specs=[rows(C_Q_RANK), rows(C_KV_RANK)],
        out_shape=[jax.ShapeDtypeStruct((s, C_Q_RANK), BF), jax.ShapeDtypeStruct((s, C_KV_RANK), BF)],
        compiler_params=_cp("parallel"), name=name)(dn, gq, gkv)


def _prep_c1_bwd(dn, dcq, dckv, dkr, gq, gkv, *, name):
    s = dn.shape[0]
    tm = _tile(s, ROW_TILE, 8)

    def body(x_ref, dcq_ref, dckv_ref, dkr_ref, gq_ref, gkv_ref, dx_ref, dgq_ref, dgkv_ref):
        @pl.when(pl.program_id(0) == 0)
        def _():
            dgq_ref[...] = jnp.zeros_like(dgq_ref)
            dgkv_ref[...] = jnp.zeros_like(dgkv_ref)

        for lo, n, g_ref, d_ref, dg_ref in ((0, C_Q_RANK, gq_ref, dcq_ref, dgq_ref),
                                            (C_Q_RANK, C_KV_RANK, gkv_ref, dckv_ref, dgkv_ref)):
            dv = d_ref[...]
            dx, xhat = _rms_bwd_math(x_ref[:, lo:lo + n], g_ref[...], dv, n)
            dx_ref[:, lo:lo + n] = dx.astype(BF)
            dg_ref[0:1, :] += jnp.sum(dv * xhat, axis=0, keepdims=True)
        dx_ref[:, C_Q_RANK + C_KV_RANK:] = dkr_ref[...].astype(BF)

    def rows(w):
        return pl.BlockSpec((tm, w), lambda i: (i, 0))

    def vec(w, r=1):
        return pl.BlockSpec((r, w), lambda i: (0, 0))

    dx, dgq, dgkv = pl.pallas_call(
        body, grid=(s // tm,),
        in_specs=[rows(dn.shape[1]), rows(C_Q_RANK), rows(C_KV_RANK), rows(LANES), vec(C_Q_RANK), vec(C_KV_RANK)],
        out_specs=[rows(dn.shape[1]), vec(C_Q_RANK, 8), vec(C_KV_RANK, 8)],
        out_shape=[jax.ShapeDtypeStruct(dn.shape, BF), jax.ShapeDtypeStruct((8, C_Q_RANK), F32),
                   jax.ShapeDtypeStruct((8, C_KV_RANK), F32)],
        compiler_params=_cp("arbitrary"), name=name)(dn, dcq, dckv, dkr, gq, gkv)
    return dx, dgq[0:1], dgkv[0:1]


def _prep_c2_fwd(qraw, knope, dn, cos, sin, gq, gk, *, name):
    s = qraw.shape[0]
    tm = _tile(s, PREP_TILE, 8)
    kr_blk = dn.shape[1] // LANES - 1

    def body(q_ref, kn_ref, kr_ref, c_ref, s_ref, gq_ref, gk_ref, qo_ref, ko_ref):
        cv, sv, kr = c_ref[...], s_ref[...], kr_ref[...]
        for t in range(C_HEADS):
            qo_ref[:, t * LANES:(t + 1) * LANES] = _head_norm_rope(
                _slot(q_ref, t), gq_ref[...], cv, sv, C_QK).astype(BF)
            ko_ref[:, t * LANES:(t + 1) * LANES] = _head_norm_rope(
                _slot(kn_ref, t) + kr, gk_ref[...], cv, sv, C_QK).astype(BF)

    def rows(w, blk=0):
        return pl.BlockSpec((tm, w), lambda i: (i, blk))

    vec = pl.BlockSpec((1, LANES), lambda i: (0, 0))
    w = C_HEADS * LANES
    return pl.pallas_call(
        body, grid=(s // tm,),
        in_specs=[rows(w), rows(w), rows(LANES, kr_blk), rows(LANES), rows(LANES), vec, vec],
        out_specs=[rows(w), rows(w)],
        out_shape=[jax.ShapeDtypeStruct((s, w), BF)] * 2,
        compiler_params=_cp("parallel"), name=name)(qraw, knope, dn, cos, sin, gq, gk)


def _prep_c2_bwd(qraw, knope, dn, dq, dk, cos, sin, gq, gk, *, name):
    s = qraw.shape[0]
    tm = _tile(s, PREP_TILE, 8)
    kr_blk = dn.shape[1] // LANES - 1

    def body(q_ref, kn_ref, kr_ref, dq_ref, dk_ref, c_ref, s_ref, gq_ref, gk_ref,
             dqo_ref, dkno_ref, dkr_ref, dg_ref):
        @pl.when(pl.program_id(0) == 0)
        def _():
            dg_ref[...] = jnp.zeros_like(dg_ref)

        cv, sv, kr = c_ref[...], s_ref[...], kr_ref[...]
        dgq = jnp.zeros((1, LANES), F32)
        dgk = jnp.zeros((1, LANES), F32)
        dkr = jnp.zeros((tm, LANES), F32)
        for t in range(C_HEADS):
            dx, dg = _head_norm_rope_bwd(_slot(q_ref, t), gq_ref[...], _slot(dq_ref, t), cv, sv,
                                         C_QK)
            dqo_ref[:, t * LANES:(t + 1) * LANES] = dx.astype(BF)
            dgq = dgq + dg
            dx, dg = _head_norm_rope_bwd(_slot(kn_ref, t) + kr, gk_ref[...], _slot(dk_ref, t), cv, sv,
                                         C_QK)
            dkno_ref[:, t * LANES:(t + 1) * LANES] = dx.astype(BF)
            dkr = dkr + dx
            dgk = dgk + dg
        dkr_ref[...] = dkr
        dg_ref[0:1, :] += dgq
        dg_ref[1:2, :] += dgk

    def rows(w, blk=0):
        return pl.BlockSpec((tm, w), lambda i: (i, blk))

    vec = pl.BlockSpec((1, LANES), lambda i: (0, 0))
    w = C_HEADS * LANES
    dqo, dkno, dkr, dg = pl.pallas_call(
        body, grid=(s // tm,),
        in_specs=[rows(w), rows(w), rows(LANES, kr_blk), rows(w), rows(w), rows(LANES), rows(LANES), vec, vec],
        out_specs=[rows(w), rows(w), rows(LANES), pl.BlockSpec((8, LANES), lambda i: (0, 0))],
        out_shape=[jax.ShapeDtypeStruct((s, w), BF), jax.ShapeDtypeStruct((s, w), BF),
                   jax.ShapeDtypeStruct((s, LANES), F32), jax.ShapeDtypeStruct((8, LANES), F32)],
        compiler_params=_cp("arbitrary"), name=name)(qraw, knope, dn, dq, dk, cos, sin, gq, gk)
    return dqo, dkno, dkr, dg[0:1], dg[1:2]


def _lane_pick(mat, g):
    lane = lax.broadcasted_iota(jnp.int32, mat.shape, 1)
    return jnp.sum(jnp.where(lane == g, mat, 0.0), axis=-1, keepdims=True)


def _swa_fwd(q, k, v, sinks, *, name):
    s = q.shape[0]
    nb = min(SWA_BLOCKS, s // BLOCK)
    qt = nb * BLOCK
    scale = A_HEAD_DIM ** -0.5
    gw = A_GROUP * LANES

    def body(sink_ref, q_ref, kc_ref, kp_ref, vc_ref, vp_ref, o_ref, lse_ref):
        kvh, n = pl.program_id(0), pl.program_id(1)
        rows = A_GROUP * BLOCK
        kall = jnp.concatenate([kp_ref[...], kc_ref[...]], axis=0)
        vall = jnp.concatenate([vp_ref[...], vc_ref[...]], axis=0)
        r = lax.broadcasted_iota(jnp.int32, (rows, 2 * BLOCK), 0) & (BLOCK - 1)
        c = lax.broadcasted_iota(jnp.int32, (rows, 2 * BLOCK), 1)
        cur_ok = (c >= BLOCK) & (c - BLOCK <= r)
        prev_ok = (c < BLOCK) & (c > r)
        head = lax.broadcasted_iota(jnp.int32, (rows, 1), 0) >> 7
        sink = jnp.zeros((rows, 1), F32)
        for g in range(A_GROUP):
            sink = jnp.where(head == g, sink_ref[kvh * A_GROUP + g], sink)
        lane = lax.broadcasted_iota(jnp.int32, (BLOCK, LANES), 1)
        for b in range(nb):
            blk = slice(b * BLOCK, (b + 1) * BLOCK)
            qs = jnp.concatenate([q_ref[blk, g * LANES:(g + 1) * LANES] for g in range(A_GROUP)], axis=0)
            k2 = kall[b * BLOCK:(b + 2) * BLOCK]
            v2 = vall[b * BLOCK:(b + 2) * BLOCK]
            ok = (cur_ok | prev_ok) if b > 0 else (cur_ok | (prev_ok & (n > 0)))
            sc = jnp.where(ok, _dot_nt(qs, k2) * scale, NEG)
            m = jnp.maximum(jnp.max(sc, axis=-1, keepdims=True), sink)
            p = jnp.exp(sc - m)
            l = jnp.sum(p, axis=-1, keepdims=True) + jnp.exp(sink - m)
            o = _dot((p * (1.0 / l)).astype(BF), v2)
            lse = m + jnp.log(l)
            lse_mat = jnp.zeros((BLOCK, LANES), F32)
            for g in range(A_GROUP):
                o_ref[blk, g * LANES:(g + 1) * LANES] = o[g * BLOCK:(g + 1) * BLOCK].astype(BF)
                lse_mat = jnp.where(lane == g, lse[g * BLOCK:(g + 1) * BLOCK], lse_mat)
            lse_ref[0, blk, :] = lse_mat

    cur = pl.BlockSpec((qt, LANES), lambda h, n: (n, h))
    prev = pl.BlockSpec((BLOCK, LANES), lambda h, n: (jnp.maximum(n * nb - 1, 0), h))
    return pl.pallas_call(
        body, grid=(A_KV_HEADS, s // qt),
        in_specs=[pl.BlockSpec(memory_space=pltpu.SMEM), pl.BlockSpec((qt, gw), lambda h, n: (n, h)),
                  cur, prev, cur, prev],
        out_specs=[pl.BlockSpec((qt, gw), lambda h, n: (n, h)),
                   pl.BlockSpec((1, qt, LANES), lambda h, n: (h, n, 0))],
        out_shape=[jax.ShapeDtypeStruct(q.shape, BF), jax.ShapeDtypeStruct((A_KV_HEADS, s, LANES), F32)],
        compiler_params=_cp("parallel", "parallel"), name=name)(sinks, q, k, k, v, v)


def _swa_bwd(q, k, v, o, do, lse, sinks, *, name):
    s = q.shape[0]
    nblk = s // BLOCK
    nb = min(SWA_BLOCKS, nblk)
    qt = nb * BLOCK
    nsteps = s // qt
    scale = A_HEAD_DIM ** -0.5
    gw = A_GROUP * LANES

    def body(sink_ref, qc_ref, qn_ref, k_ref, v_ref, oc_ref, on_ref, doc_ref, don_ref, lc_ref, ln_ref,
             dq_ref, dk_ref, dv_ref, dsink_ref, carry):
        kvh, n = pl.program_id(0), pl.program_id(1)

        @pl.when(n == 0)
        def _():
            carry[...] = jnp.zeros_like(carry)
            dsink_ref[...] = jnp.zeros_like(dsink_ref)

        half = A_GROUP * BLOCK
        rows = 2 * half
        qall = jnp.concatenate([qc_ref[...], qn_ref[...]], axis=0)
        oall = jnp.concatenate([oc_ref[...], on_ref[...]], axis=0)
        doall = jnp.concatenate([doc_ref[...], don_ref[...]], axis=0)
        lall = jnp.concatenate([lc_ref[0], ln_ref[0]], axis=0)
        row = lax.broadcasted_iota(jnp.int32, (rows, BLOCK), 0)
        r = row & (BLOCK - 1)
        c = lax.broadcasted_iota(jnp.int32, (rows, BLOCK), 1)
        diag_ok = (row < half) & (c <= r)
        next_ok = (row >= half) & (c > r)
        lane8 = lax.broadcasted_iota(jnp.int32, (8, LANES), 1)
        dsink = jnp.zeros((8, LANES), F32)
        off_prev = None
        for b in range(nb):
            blk = slice(b * BLOCK, (b + 1) * BLOCK)
            kv, vv = k_ref[blk, :], v_ref[blk, :]

            def stack(allv):
                return jnp.concatenate(
                    [allv[b * BLOCK:(b + 1) * BLOCK, g * LANES:(g + 1) * LANES] for g in range(A_GROUP)]
                    + [allv[(b + 1) * BLOCK:(b + 2) * BLOCK, g * LANES:(g + 1) * LANES] for g in range(A_GROUP)],
                    axis=0)

            qs, osk, dos = stack(qall), stack(oall), stack(doall)
            lse = jnp.concatenate([_lane_pick(lall[b * BLOCK:(b + 1) * BLOCK], g) for g in range(A_GROUP)]
                                  + [_lane_pick(lall[(b + 1) * BLOCK:(b + 2) * BLOCK], g) for g in range(A_GROUP)],
                                  axis=0)
            delta = jnp.sum(dos.astype(F32) * osk.astype(F32), axis=-1, keepdims=True)
            ok = (diag_ok | next_ok) if b < nb - 1 else (diag_ok | (next_ok & (n < nsteps - 1)))
            sc = jnp.where(ok, _dot_nt(qs, kv) * scale, NEG)
            p = jnp.exp(sc - lse)
            dv_ref[blk, :] = _dot_tn(p.astype(BF), dos)
            dp = _dot_nt(dos, vv)
            ds = (p * (dp - delta) * scale).astype(BF)
            dk_ref[blk, :] = _dot_tn(ds, qs)
            dqs = _dot(ds, kv)
            for g in range(A_GROUP):
                cur = slice(g * BLOCK, (g + 1) * BLOCK)
                before = carry[:, g * LANES:(g + 1) * LANES] if b == 0 else off_prev[cur]
                dq_ref[blk, g * LANES:(g + 1) * LANES] = before + dqs[cur]
                p_sink = jnp.exp(sink_ref[kvh * A_GROUP + g] - lse[cur])
                dsink = dsink + jnp.where(lane8 == g, -jnp.sum(p_sink * delta[cur]), 0.0)
            off_prev = dqs[half:]
        for g in range(A_GROUP):
            carry[:, g * LANES:(g + 1) * LANES] = off_prev[g * BLOCK:(g + 1) * BLOCK]
        dsink_ref[...] += dsink

    def nxt(n):
        return jnp.minimum((n + 1) * nb, nblk - 1)

    grp_c = pl.BlockSpec((qt, gw), lambda h, n: (n, h))
    grp_n = pl.BlockSpec((BLOCK, gw), lambda h, n: (nxt(n), h))
    kvb = pl.BlockSpec((qt, LANES), lambda h, n: (n, h))
    lse_c = pl.BlockSpec((1, qt, LANES), lambda h, n: (h, n, 0))
    lse_n = pl.BlockSpec((1, BLOCK, LANES), lambda h, n: (h, nxt(n), 0))
    dq, dk, dv, dsink = pl.pallas_call(
        body, grid=(A_KV_HEADS, nsteps),
        in_specs=[pl.BlockSpec(memory_space=pltpu.SMEM), grp_c, grp_n, kvb, kvb, grp_c, grp_n, grp_c, grp_n,
                  lse_c, lse_n],
        out_specs=[grp_c, kvb, kvb, pl.BlockSpec((8, LANES), lambda h, n: (h, 0))],
        out_shape=[jax.ShapeDtypeStruct(q.shape, F32), jax.ShapeDtypeStruct(k.shape, F32),
                   jax.ShapeDtypeStruct(v.shape, F32), jax.ShapeDtypeStruct((A_KV_HEADS * 8, LANES), F32)],
        scratch_shapes=[pltpu.VMEM((BLOCK, gw), F32)],
        compiler_params=_cp("parallel", "arbitrary"), name=name)(sinks, q, q, k, v, o, o, do, do, lse, lse)
    dsinks = dsink.reshape(A_KV_HEADS, 8, LANES)[:, 0, :A_GROUP].reshape(A_HEADS)
    return dq, dk, dv, dsinks


LOG2E = 1.4426950408889634


def _mla_fwd(q, k, v, *, name):
    s = q.shape[0]
    t = _tile(s, ATTN_TILE, LANES)
    nt = s // t
    scale = C_QK ** -0.5
    c2 = scale * LOG2E

    def body(q_ref, k_ref, v_ref, o_ref, lse_ref, m_sc, acc_sc):
        qi = pl.program_id(1)
        qv = q_ref[...]
        m_sc[...] = jnp.full_like(m_sc, NEG)
        acc_sc[...] = jnp.zeros_like(acc_sc)

        def step(start, width, diag):
            rows = pl.ds(pl.multiple_of(start, t), width)
            sc = _dot_nt(qv, k_ref[rows, :])
            if diag:
                r = lax.broadcasted_iota(jnp.int32, (t, width), 0)
                c = lax.broadcasted_iota(jnp.int32, (t, width), 1)
                sc = jnp.where(c <= r, sc, NEG)
            m_prev = m_sc[...]
            m_new = jnp.maximum(m_prev, jnp.max(sc, axis=-1, keepdims=True))
            alpha = jnp.exp2((m_prev - m_new) * c2)
            p = jnp.exp2((sc - jnp.tile(m_new, (1, width // LANES))) * c2).astype(BF)
            lane = lax.broadcasted_iota(jnp.int32, (width, LANES), 1)
            vv = jnp.where(lane == C_V, jnp.ones((), BF), v_ref[rows, :])
            acc_sc[...] = alpha * acc_sc[...] + _dot(p, vv)
            m_sc[...] = m_new

        def wide_body(wi, carry):
            step(wi * (FWD_WIDE * t), FWD_WIDE * t, False)
            return carry

        def single_body(j, carry):
            step(j * t, t, False)
            return carry

        n_wide = qi // FWD_WIDE
        lax.fori_loop(0, n_wide, wide_body, 0)
        lax.fori_loop(n_wide * FWD_WIDE, qi, single_body, 0)
        step(qi * t, t, True)
        acc = acc_sc[...]
        l = _lane_pick(acc, C_V)
        lane = lax.broadcasted_iota(jnp.int32, (t, LANES), 1)
        o_ref[...] = jnp.where(lane == C_V, 0.0, acc * (1.0 / l)).astype(BF)
        lse2 = m_sc[...] * c2 + jnp.log(l) * LOG2E
        lse_ref[0, 0] = jnp.transpose(lse2)[0:8, :]

    qspec = pl.BlockSpec((t, LANES), lambda h, i: (i, h))
    kspec = pl.BlockSpec((s, LANES), lambda h, i: (0, h))
    return pl.pallas_call(
        body, grid=(C_HEADS, nt), in_specs=[qspec, kspec, kspec],
        out_specs=[qspec, pl.BlockSpec((1, 1, 8, t), lambda h, i: (h, i, 0, 0))],
        out_shape=[jax.ShapeDtypeStruct(q.shape, BF), jax.ShapeDtypeStruct((C_HEADS, nt, 8, t), F32)],
        scratch_shapes=[pltpu.VMEM((t, LANES), F32)] * 2,
        compiler_params=_cp("parallel", "parallel"), name=name)(q, k, v)


def _mla_delta(o, do, *, name):
    s = o.shape[0]
    t = _tile(s, ATTN_TILE, LANES)
    nt = s // t

    def body(o_ref, do_ref, d_ref):
        for h in range(C_HEADS):
            prod = jnp.transpose(_slot(o_ref, h).astype(F32) * _slot(do_ref, h).astype(F32))
            d_ref[h, 0] = jnp.broadcast_to(jnp.sum(prod, axis=0, keepdims=True), (8, t))

    blk = pl.BlockSpec((t, C_HEADS * LANES), lambda i: (i, 0))
    return pl.pallas_call(
        body, grid=(nt,), in_specs=[blk, blk],
        out_specs=pl.BlockSpec((C_HEADS, 1, 8, t), lambda i: (0, i, 0, 0)),
        out_shape=jax.ShapeDtypeStruct((C_HEADS, nt, 8, t), F32),
        compiler_params=_cp("parallel"), name=name)(o, do)


def _mla_bwd(q, k, v, do, lse2, delta, *, name):
    s = q.shape[0]
    t = _tile(s, ATTN_TILE, LANES)
    nt = s // t
    scale = C_QK ** -0.5
    c2 = scale * LOG2E

    def body(q_ref, do_ref, k_ref, v_ref, lse_ref, dl_ref, dq_ref, dk_ref, dv_ref):
        kj = pl.program_id(1)

        @pl.when(kj == 0)
        def _():
            dq_ref[...] = jnp.zeros_like(dq_ref)

        dk_ref[...] = jnp.zeros_like(dk_ref)
        dv_ref[...] = jnp.zeros_like(dv_ref)
        kv, vv = k_ref[...], v_ref[...]

        def step(i, n, diag):
            rows = pl.ds(pl.multiple_of(i * t, t), n * t)
            qv, dov = q_ref[rows, :], do_ref[rows, :]
            st = _dot_nt(kv, qv)
            if diag:
                r = lax.broadcasted_iota(jnp.int32, (t, n * t), 0)
                c = lax.broadcasted_iota(jnp.int32, (t, n * t), 1)
                st = jnp.where(r <= c, st, NEG)
            lse = jnp.concatenate([lse_ref[0, i + u, 0:1, :] for u in range(n)], axis=1)
            dl = jnp.concatenate([dl_ref[0, i + u, 0:1, :] for u in range(n)], axis=1)
            pt = jnp.exp2(st * c2 - lse)
            dv_ref[...] += _dot(pt.astype(BF), dov)
            dpt = _dot_nt(vv, dov)
            dst = (pt * (dpt - dl) * scale).astype(BF)
            dk_ref[...] += _dot(dst, qv)
            dq_ref[rows, :] += _dot_tn(dst, kv)

        def pair_body(pi, carry):
            step(kj + 1 + 2 * pi, 2, False)
            return carry

        step(kj, 1, True)
        rest = nt - 1 - kj
        lax.fori_loop(0, rest // 2, pair_body, 0)

        @pl.when(rest % 2 == 1)
        def _():
            step(nt - 1, 1, False)

    res = pl.BlockSpec((s, LANES), lambda h, j: (0, h))
    kspec = pl.BlockSpec((t, LANES), lambda h, j: (j, h))
    stat = pl.BlockSpec((1, nt, 8, t), lambda h, j: (h, 0, 0, 0))
    return pl.pallas_call(
        body, grid=(C_HEADS, nt), in_specs=[res, res, kspec, kspec, stat, stat],
        out_specs=[res, kspec, kspec],
        out_shape=[jax.ShapeDtypeStruct(q.shape, F32)] * 3,
        compiler_params=_cp("parallel", "arbitrary"), name=name)(q, do, k, v, lse2, delta)


def _conv_shifted(z, zprev, tm):
    row = lax.broadcasted_iota(jnp.int32, z.shape, 0)
    z1 = jnp.where(row == 0, zprev[7:8], pltpu.roll(z, 1, 0))
    z2 = jnp.where(row == 0, zprev[6:7], jnp.where(row == 1, zprev[7:8], pltpu.roll(z, 2, 0)))
    return z1, z2


def _conv_fwd(proj, w, *, name):
    s = proj.shape[0]
    d = D_MODEL
    tm = _tile(s, PREP_TILE, 8)

    def body(x_ref, xp_ref, w_ref, y_ref):
        i = pl.program_id(0)
        z = x_ref[:, d:2 * d] * x_ref[:, 2 * d:]
        zprev = jnp.where(i > 0, xp_ref[:, d:2 * d] * xp_ref[:, 2 * d:], 0.0)
        z1, z2 = _conv_shifted(z, zprev, tm)
        y = w_ref[0:1, :] * z2 + w_ref[1:2, :] * z1 + w_ref[2:3, :] * z
        y_ref[...] = (x_ref[:, :d] * y).astype(BF)

    per8 = tm // 8
    return pl.pallas_call(
        body, grid=(s // tm,),
        in_specs=[pl.BlockSpec((tm, 3 * d), lambda i: (i, 0)),
                  pl.BlockSpec((8, 3 * d), lambda i: (jnp.maximum(i * per8 - 1, 0), 0)),
                  pl.BlockSpec((8, d), lambda i: (0, 0))],
        out_specs=pl.BlockSpec((tm, d), lambda i: (i, 0)),
        out_shape=jax.ShapeDtypeStruct((s, d), BF),
        compiler_params=_cp("parallel"), name=name)(proj, proj, w)


def _conv_bwd(proj, dyb, w, *, name):
    s = proj.shape[0]
    d = D_MODEL
    tm = _tile(s, PREP_TILE, 8)
    nt = s // tm

    def body(x_ref, xp_ref, xn_ref, dy_ref, dyn_ref, w_ref, dx_ref, dw_ref):
        i = pl.program_id(0)

        @pl.when(i == 0)
        def _():
            dw_ref[...] = jnp.zeros_like(dw_ref)

        b, c, u = x_ref[:, :d], x_ref[:, d:2 * d], x_ref[:, 2 * d:]
        z = c * u
        zprev = jnp.where(i > 0, xp_ref[:, d:2 * d] * xp_ref[:, 2 * d:], 0.0)
        z1, z2 = _conv_shifted(z, zprev, tm)
        w0, w1, w2 = w_ref[0:1, :], w_ref[1:2, :], w_ref[2:3, :]
        y = w0 * z2 + w1 * z1 + w2 * z
        dyb_v = dy_ref[...]
        dyc = dyb_v * b
        dyn = jnp.where(i < nt - 1, dyn_ref[...] * xn_ref[:, :d], 0.0)
        row = lax.broadcasted_iota(jnp.int32, dyc.shape, 0)
        d1 = jnp.where(row == tm - 1, dyn[0:1], pltpu.roll(dyc, tm - 1, 0))
        d2 = jnp.where(row == tm - 1, dyn[1:2], jnp.where(row == tm - 2, dyn[0:1], pltpu.roll(dyc, tm - 2, 0)))
        dz = w2 * dyc + w1 * d1 + w0 * d2
        dx_ref[:, :d] = (dyb_v * y).astype(BF)
        dx_ref[:, d:2 * d] = (dz * u).astype(BF)
        dx_ref[:, 2 * d:] = (dz * c).astype(BF)
        dw_ref[0:1, :] += jnp.sum(dyc * z2, axis=0, keepdims=True)
        dw_ref[1:2, :] += jnp.sum(dyc * z1, axis=0, keepdims=True)
        dw_ref[2:3, :] += jnp.sum(dyc * z, axis=0, keepdims=True)

    per8 = tm // 8
    last8 = s // 8 - 1
    dx, dw = pl.pallas_call(
        body, grid=(nt,),
        in_specs=[pl.BlockSpec((tm, 3 * d), lambda i: (i, 0)),
                  pl.BlockSpec((8, 3 * d), lambda i: (jnp.maximum(i * per8 - 1, 0), 0)),
                  pl.BlockSpec((8, 3 * d), lambda i: (jnp.minimum((i + 1) * per8, last8), 0)),
                  pl.BlockSpec((tm, d), lambda i: (i, 0)),
                  pl.BlockSpec((8, d), lambda i: (jnp.minimum((i + 1) * per8, last8), 0)),
                  pl.BlockSpec((8, d), lambda i: (0, 0))],
        out_specs=[pl.BlockSpec((tm, 3 * d), lambda i: (i, 0)), pl.BlockSpec((8, d), lambda i: (0, 0))],
        out_shape=[jax.ShapeDtypeStruct((s, 3 * d), BF), jax.ShapeDtypeStruct((8, d), F32)],
        compiler_params=_cp("arbitrary"), name=name)(proj, proj, proj, dyb, dyb, w)
    return dx, dw[0:3]


def _ffn_up(h, wgu, *, name):
    s, d = h.shape
    f = wgu.shape[1] // 2
    tm = _tile(s, ROW_TILE, 8)
    tn = _tile(f, 1408, 128)
    nb = f // tn

    def body(h_ref, wg_ref, wu_ref, g_ref, u_ref, a_ref):
        hv = h_ref[...]
        g = _dot(hv, wg_ref[...])
        u = _dot(hv, wu_ref[...])
        g_ref[...] = g.astype(BF)
        u_ref[...] = u.astype(BF)
        a_ref[...] = (g * jax.nn.sigmoid(g) * u).astype(BF)

    out = pl.BlockSpec((tm, tn), lambda j, i: (i, j))
    return pl.pallas_call(
        body, grid=(nb, s // tm),
        in_specs=[pl.BlockSpec((tm, d), lambda j, i: (i, 0)),
                  pl.BlockSpec((d, tn), lambda j, i: (0, j)),
                  pl.BlockSpec((d, tn), lambda j, i: (0, j + nb))],
        out_specs=[out] * 3, out_shape=[jax.ShapeDtypeStruct((s, f), BF)] * 3,
        compiler_params=_cp("parallel", "parallel"), name=name)(h, wgu, wgu)


def _ffn_bwd_mid(dy, wd, gate, up, *, name):
    s, d = dy.shape
    f = wd.shape[0]
    tm = _tile(s, ROW_TILE, 8)
    tn = _tile(f, 1408, 128)

    def body(dy_ref, wd_ref, g_ref, u_ref, dg_ref, du_ref):
        dact = _dot_nt(dy_ref[...].astype(BF), wd_ref[...])
        g = g_ref[...].astype(F32)
        u = u_ref[...].astype(F32)
        sig = jax.nn.sigmoid(g)
        dg_ref[...] = (dact * u * (sig * (1.0 + g * (1.0 - sig)))).astype(BF)
        du_ref[...] = (dact * (g * sig)).astype(BF)

    blk = pl.BlockSpec((tm, tn), lambda j, i: (i, j))
    return pl.pallas_call(
        body, grid=(f // tn, s // tm),
        in_specs=[pl.BlockSpec((tm, d), lambda j, i: (i, 0)), pl.BlockSpec((tn, d), lambda j, i: (j, 0)), blk, blk],
        out_specs=[blk, blk], out_shape=[jax.ShapeDtypeStruct((s, f), BF)] * 2,
        compiler_params=_cp("parallel", "parallel"), name=name)(dy, wd, gate, up)


def _pad_cols(w, heads, dim, layout):
    k = w.shape[0]
    w3 = w.reshape(k, heads, dim)
    pieces, lane = [], 0
    for lane0, dim0, cnt in sorted(layout):
        if lane0 > lane:
            pieces.append(jnp.zeros((k, heads, lane0 - lane), w.dtype))
        pieces.append(w3[:, :, dim0:dim0 + cnt])
        lane = lane0 + cnt
    if lane < LANES:
        pieces.append(jnp.zeros((k, heads, LANES - lane), w.dtype))
    return jnp.concatenate(pieces, axis=2).reshape(k, heads * LANES)


def _unpad_cols(w, heads, dim, layout):
    k = w.shape[0]
    w3 = w.reshape(k, heads, LANES)
    by_dim = sorted(layout, key=lambda seg: seg[1])
    assert sum(cnt for _, _, cnt in by_dim) == dim
    return jnp.concatenate([w3[:, :, lane0:lane0 + cnt] for lane0, _, cnt in by_dim], axis=2).reshape(k, heads * dim)


def _pad_rows(w, heads, dim):
    n = w.shape[1]
    return jnp.pad(w.reshape(heads, dim, n), ((0, 0), (0, LANES - dim), (0, 0))).reshape(heads * LANES, n)


def _unpad_rows(w, heads, dim):
    n = w.shape[1]
    return w.reshape(heads, LANES, n)[:, :dim, :].reshape(heads * dim, n)


def _pad_vec(g, layout):
    return _pad_cols(g.reshape(1, -1), 1, g.shape[0], layout)


def _unpad_vec(g, dim, layout):
    return _unpad_cols(g.reshape(1, LANES), 1, dim, layout)[0]


def _local_step(x, positions, w, target):
    s = x.shape[0]
    cos_a, sin_a, cos_c, sin_c = _rope_tables(positions.reshape(s, 1), name="rope_tables")
    grads = {n: [None] * w[n].shape[0] for n in WEIGHTS}
    saved = []

    for i in range(DEPTH):
        kind, j = i % N_MIXERS, i // N_MIXERS
        tag = f"l{i}"
        h = _rmsnorm_fwd(x, w['mix_norm'][i], name=f"{tag}_mix_norm")
        if kind == 0:
            nqk = (A_HEADS + A_KV_HEADS) * A_HEAD_DIM
            wqkv = jnp.concatenate(
                [_pad_cols(w['a_w_qkv'][j][:, :nqk], A_HEADS + A_KV_HEADS, A_HEAD_DIM, LAYOUT_A),
                 _pad_cols(w['a_w_qkv'][j][:, nqk:], A_KV_HEADS, A_HEAD_DIM, LAYOUT_V)], axis=1)
            wo = _pad_rows(w['a_w_o'][j], A_HEADS, A_HEAD_DIM)
            gq, gk = _pad_vec(w['a_q_norm'][j], LAYOUT_A), _pad_vec(w['a_k_norm'][j], LAYOUT_A)
            qkv = _mm([(h, wqkv, 0)], out_dtype=F32, name=f"{tag}_qkv")
            qa, ka, va = _prep_a_fwd(qkv, cos_a, sin_a, gq, gk, name=f"{tag}_prep")
            o, lse = _swa_fwd(qa, ka, va, w['a_sinks'][j], name=f"{tag}_attn")
            x1 = _mm([(o, wo, 0)], out_dtype=F32, res=x, name=f"{tag}_wo")
            mix = dict(wqkv=wqkv, wo=wo, gq=gq, gk=gk, qkv=qkv, qa=qa, ka=ka, va=va, o=o, lse=lse)
        elif kind == 1:
            cw = jnp.pad(w['b_conv_w'][j], ((0, 5), (0, 0)))
            proj = _mm([(h, w['b_w_in'][j], 0)], out_dtype=F32, name=f"{tag}_win")
            yb = _conv_fwd(proj, cw, name=f"{tag}_conv")
            x1 = _mm([(yb, w['b_w_out'][j], 0)], out_dtype=F32, res=x, name=f"{tag}_wout")
            mix = dict(cw=cw, proj=proj, yb=yb)
        else:
            wdn = w['c_w_down'][j]
            nqk = C_Q_RANK + C_KV_RANK
            wdn = jnp.concatenate([wdn[:, :nqk], _pad_cols(wdn[:, nqk:], 1, C_ROPE, LAYOUT_KR)], axis=1)
            wq = _pad_cols(w['c_w_q_up'][j], C_HEADS, C_QK, LAYOUT_C)
            wkv = w['c_w_kv_up'][j].reshape(C_KV_RANK, C_HEADS, C_NOPE + C_V)
            wkn = _pad_cols(wkv[:, :, :C_NOPE].reshape(C_KV_RANK, -1), C_HEADS, C_NOPE, LAYOUT_KN)
            wv = _pad_cols(wkv[:, :, C_NOPE:].reshape(C_KV_RANK, -1), C_HEADS, C_V, LAYOUT_V)
            wo = _pad_rows(w['c_w_o'][j], C_HEADS, C_V)
            gq, gk = _pad_vec(w['c_q_norm'][j], LAYOUT_C), _pad_vec(w['c_k_norm'][j], LAYOUT_C)
            gqa, gkva = w['c_q_a_norm'][j].reshape(1, -1), w['c_kv_a_norm'][j].reshape(1, -1)
            dn = _mm([(h, wdn, 0)], out_dtype=F32, name=f"{tag}_wdown")
            cqn, ckvn = _prep_c1_fwd(dn, gqa, gkva, name=f"{tag}_prep1")
            qraw = _mm([(cqn, wq, 0)], out_dtype=F32, name=f"{tag}_wq")
            knope = _mm([(ckvn, wkn, 0)], out_dtype=F32, name=f"{tag}_wkn")
            vc = _mm([(ckvn, wv, 0)], out_dtype=BF, name=f"{tag}_wv")
            qc, kc = _prep_c2_fwd(qraw, knope, dn, cos_c, sin_c, gq, gk, name=f"{tag}_prep2")
            o, lse = _mla_fwd(qc, kc, vc, name=f"{tag}_attn")
            x1 = _mm([(o, wo, 0)], out_dtype=F32, res=x, name=f"{tag}_wo")
            mix = dict(wdn=wdn, wq=wq, wkn=wkn, wv=wv, wo=wo, gq=gq, gk=gk, gqa=gqa, gkva=gkva, dn=dn, cqn=cqn,
                       ckvn=ckvn, qraw=qraw, knope=knope, vc=vc, qc=qc, kc=kc, o=o, lse=lse)
        h2 = _rmsnorm_fwd(x1, w['ffn_norm'][i], name=f"{tag}_ffn_norm")
        gate, up, act = _ffn_up(h2, w['f_w_gate_up'][i], name=f"{tag}_ffn_up")
        x2 = _mm([(act, w['f_w_down'][i], 0)], out_dtype=F32, res=x1, name=f"{tag}_ffn_down")
        saved.append(dict(x=x, h=h, x1=x1, h2=h2, gate=gate, up=up, act=act, mix=mix))
        x = x2

    loss_blk, dx = _loss_fwd_bwd(x, target, name="loss")

    for i in reversed(range(DEPTH)):
        kind, j = i % N_MIXERS, i // N_MIXERS
        tag = f"l{i}b"
        sv = saved[i]
        mix = sv['mix']
        wgu, wd = w['f_w_gate_up'][i], w['f_w_down'][i]
        grads['f_w_down'][i] = _mm_tn(sv['act'], dx, name=f"{tag}_dwd")
        dgate, dup = _ffn_bwd_mid(dx, wd, sv['gate'], sv['up'], name=f"{tag}_ffn_mid")
        grads['f_w_gate_up'][i] = jnp.concatenate(
            [_mm_tn(sv['h2'], dgate, name=f"{tag}_dwg"), _mm_tn(sv['h2'], dup, name=f"{tag}_dwu")], axis=1)
        dh2 = _mm([(dgate, wgu, 0), (dup, wgu, 1)], out_dtype=F32, trans_b=True, name=f"{tag}_dh2")
        dx, dg = _rmsnorm_bwd(sv['x1'], w['ffn_norm'][i], dh2, dx, name=f"{tag}_ffn_norm")
        grads['ffn_norm'][i] = dg[0]
        if kind == 0:
            grads['a_w_o'][j] = _unpad_rows(_mm_tn(mix['o'], dx, name=f"{tag}_dwo"), A_HEADS, A_HEAD_DIM)
            do = _mm([(dx, mix['wo'], 0)], out_dtype=BF, trans_b=True, name=f"{tag}_do")
            dqa, dka, dva, dsinks = _swa_bwd(mix['qa'], mix['ka'], mix['va'], mix['o'], do, mix['lse'],
                                             w['a_sinks'][j], name=f"{tag}_attn")
            dqkv, dgq, dgk = _prep_a_bwd(mix['qkv'], dqa, dka, dva, cos_a, sin_a, mix['gq'], mix['gk'],
                                         name=f"{tag}_prep")
            grads['a_sinks'][j] = dsinks
            grads['a_q_norm'][j] = _unpad_vec(dgq, A_HEAD_DIM, LAYOUT_A)
            grads['a_k_norm'][j] = _unpad_vec(dgk, A_HEAD_DIM, LAYOUT_A)
            dwqkv = _mm_tn(sv['h'], dqkv, name=f"{tag}_dwqkv")
            nqk = (A_HEADS + A_KV_HEADS) * LANES
            grads['a_w_qkv'][j] = jnp.concatenate(
                [_unpad_cols(dwqkv[:, :nqk], A_HEADS + A_KV_HEADS, A_HEAD_DIM, LAYOUT_A),
                 _unpad_cols(dwqkv[:, nqk:], A_KV_HEADS, A_HEAD_DIM, LAYOUT_V)], axis=1)
            dh = _mm([(dqkv, mix['wqkv'], 0)], out_dtype=F32, trans_b=True, name=f"{tag}_dh")
        elif kind == 1:
            grads['b_w_out'][j] = _mm_tn(mix['yb'], dx, name=f"{tag}_dwout")
            dyb = _mm([(dx, w['b_w_out'][j], 0)], out_dtype=F32, trans_b=True, name=f"{tag}_dyb")
            dproj, dcw = _conv_bwd(mix['proj'], dyb, mix['cw'], name=f"{tag}_conv")
            grads['b_conv_w'][j] = dcw
            grads['b_w_in'][j] = _mm_tn(sv['h'], dproj, name=f"{tag}_dwin")
            dh = _mm([(dproj, w['b_w_in'][j], 0)], out_dtype=F32, trans_b=True, name=f"{tag}_dh")
        else:
            grads['c_w_o'][j] = _unpad_rows(_mm_tn(mix['o'], dx, name=f"{tag}_dwo"), C_HEADS, C_V)
            do = _mm([(dx, mix['wo'], 0)], out_dtype=BF, trans_b=True, name=f"{tag}_do")
            delta = _mla_delta(mix['o'], do, name=f"{tag}_delta")
            dqc, dkc, dvc = _mla_bwd(mix['qc'], mix['kc'], mix['vc'], do, mix['lse'], delta, name=f"{tag}_attn")
            dqraw, dknope, dkr, dgq, dgk = _prep_c2_bwd(mix['qraw'], mix['knope'], mix['dn'], dqc, dkc, cos_c, sin_c,
                                                        mix['gq'], mix['gk'], name=f"{tag}_prep2")
            grads['c_q_norm'][j] = _unpad_vec(dgq, C_QK, LAYOUT_C)
            grads['c_k_norm'][j] = _unpad_vec(dgk, C_QK, LAYOUT_C)
            grads['c_w_q_up'][j] = _unpad_cols(_mm_tn(mix['cqn'], dqraw, name=f"{tag}_dwq"), C_HEADS, C_QK,
                                               LAYOUT_C)
            dwkn = _unpad_cols(_mm_tn(mix['ckvn'], dknope, name=f"{tag}_dwkn"), C_HEADS, C_NOPE, LAYOUT_KN)
            dwv = _unpad_cols(_mm_tn(mix['ckvn'], dvc, name=f"{tag}_dwv"), C_HEADS, C_V, LAYOUT_V)
            grads['c_w_kv_up'][j] = jnp.concatenate(
                [dwkn.reshape(C_KV_RANK, C_HEADS, C_NOPE), dwv.reshape(C_KV_RANK, C_HEADS, C_V)], axis=2).reshape(
                C_KV_RANK, -1)
            dcq = _mm([(dqraw, mix['wq'], 0)], out_dtype=F32, trans_b=True, name=f"{tag}_dcq")
            dckv = _mm([(dknope, mix['wkn'], 0), (dvc, mix['wv'], 0)], out_dtype=F32, trans_b=True,
                       name=f"{tag}_dckv")
            ddn, dgqa, dgkva = _prep_c1_bwd(mix['dn'], dcq, dckv, dkr, mix['gqa'], mix['gkva'], name=f"{tag}_prep1")
            grads['c_q_a_norm'][j] = dgqa[0]
            grads['c_kv_a_norm'][j] = dgkva[0]
            dwdn = _mm_tn(sv['h'], ddn, name=f"{tag}_dwdown")
            nqk = C_Q_RANK + C_KV_RANK
            grads['c_w_down'][j] = jnp.concatenate(
                [dwdn[:, :nqk], _unpad_cols(dwdn[:, nqk:], 1, C_ROPE, LAYOUT_KR)], axis=1)
            dh = _mm([(ddn, mix['wdn'], 0)], out_dtype=F32, trans_b=True, name=f"{tag}_dh")
        dx, dg = _rmsnorm_bwd(sv['x'], w['mix_norm'][i], dh, dx, name=f"{tag}_mix_norm")
        grads['mix_norm'][i] = dg[0]

    return loss_blk, dx, {n: jnp.stack(g) for n, g in grads.items()}


def _my_place():
    return lax.axis_index("x"), lax.axis_index("y"), lax.axis_index("c")


def _flips(x, y):
    return [(1 - x, y), (x, 1 - y), (1 - x, 1 - y)]


def _gather_weights(big, small):
    def body(big_ref, small_ref, bout_ref, sout_ref, send_b, recv_b, send_s, recv_s, loc):
        x, y, c = _my_place()
        me = 2 * x + y

        def copies(k, src_chip, to):
            return (pltpu.make_async_remote_copy(src_ref=big_ref, dst_ref=bout_ref.at[src_chip],
                                                 send_sem=send_b.at[k], recv_sem=recv_b.at[k],
                                                 device_id=to, device_id_type=MESH),
                    pltpu.make_async_remote_copy(src_ref=small_ref, dst_ref=sout_ref.at[src_chip],
                                                 send_sem=send_s.at[k], recv_sem=recv_s.at[k],
                                                 device_id=to, device_id_type=MESH))

        own = (pltpu.make_async_copy(big_ref, bout_ref.at[me], loc.at[0]),
               pltpu.make_async_copy(small_ref, sout_ref.at[me], loc.at[1]))
        for cp in own:
            cp.start()
        sends = [copies(k, me, (px, py, c)) for k, (px, py) in enumerate(_flips(x, y))]
        for pair in sends:
            for cp in pair:
                cp.start()
        for k, (px, py) in enumerate(_flips(x, y)):
            for cp in copies(k, 2 * px + py, (x, y, c)):
                cp.wait_recv()
        for pair in sends:
            for cp in pair:
                cp.wait_send()
        for cp in own:
            cp.wait()

    hbm = pl.BlockSpec(memory_space=pltpu.HBM)
    return pl.pallas_call(
        body, in_specs=[hbm, hbm], out_specs=[hbm, hbm],
        out_shape=[jax.ShapeDtypeStruct((N_CHIPS,) + big.shape, big.dtype),
                   jax.ShapeDtypeStruct((N_CHIPS,) + small.shape, small.dtype)],
        scratch_shapes=[pltpu.SemaphoreType.DMA((3,)), pltpu.SemaphoreType.DMA((3,)),
                        pltpu.SemaphoreType.DMA((3,)), pltpu.SemaphoreType.DMA((3,)),
                        pltpu.SemaphoreType.DMA((2,))],
        name="gather_weights")(big, small)


def _exchange_grads(gbig, gsmall):
    def body(gbig_ref, gsmall_ref, got_ref, ssum_ref, sbuf, send_b, recv_b, send_s, recv_s):
        x, y, c = _my_place()
        me = 4 * x + 2 * y + c

        def big_copy(k, src_chip, to):
            return pltpu.make_async_remote_copy(src_ref=gbig_ref.at[src_chip], dst_ref=got_ref.at[k],
                                                send_sem=send_b.at[k], recv_sem=recv_b.at[k],
                                                device_id=to, device_id_type=MESH)

        def small_copy(k, slot, to):
            return pltpu.make_async_remote_copy(src_ref=gsmall_ref, dst_ref=sbuf.at[slot],
                                                send_sem=send_s.at[k], recv_sem=recv_s.at[k],
                                                device_id=to, device_id_type=MESH)

        peers = [(x ^ (k >> 2), y ^ ((k >> 1) & 1), c ^ (k & 1)) for k in range(1, N_DEV)]
        bigs = [big_copy(k, 2 * px + py, (px, py, c)) for k, (px, py) in enumerate(_flips(x, y))]
        smalls = [small_copy(k, me, p) for k, p in enumerate(peers)]
        for cp in bigs + smalls:
            cp.start()
        sbuf[me] = gsmall_ref[...]
        for k, (px, py, pc) in enumerate(peers):
            small_copy(k, 4 * px + 2 * py + pc, (x, y, c)).wait_recv()
        acc = sbuf[0]
        for d in range(1, N_DEV):
            acc = acc + sbuf[d]
        ssum_ref[...] = acc
        for k in range(3):
            big_copy(k, 0, (x, y, c)).wait_recv()
        for cp in bigs + smalls:
            cp.wait_send()

    hbm = pl.BlockSpec(memory_space=pltpu.HBM)
    vmem = pl.BlockSpec(memory_space=pltpu.VMEM)
    return pl.pallas_call(
        body, in_specs=[hbm, vmem], out_specs=[hbm, vmem],
        out_shape=[jax.ShapeDtypeStruct((3,) + gbig.shape[1:], gbig.dtype),
                   jax.ShapeDtypeStruct(gsmall.shape, gsmall.dtype)],
        scratch_shapes=[pltpu.VMEM((N_DEV,) + gsmall.shape, gsmall.dtype),
                        pltpu.SemaphoreType.DMA((3,)), pltpu.SemaphoreType.DMA((3,)),
                        pltpu.SemaphoreType.DMA((N_DEV - 1,)), pltpu.SemaphoreType.DMA((N_DEV - 1,))],
        name="exchange_grads")(gbig, gsmall)


def _sum_shards(chip, gbig, got):
    r, cols = gbig.shape[1:]
    tr = _tile(r, PACK_TILE_ROWS, 8)

    def body(chip_ref, mine_ref, got_ref, o_ref):
        o_ref[...] = ((mine_ref[0] + got_ref[0]) + got_ref[1]) + got_ref[2]

    grid_spec = pltpu.PrefetchScalarGridSpec(
        num_scalar_prefetch=1, grid=(r // tr,),
        in_specs=[pl.BlockSpec((1, tr, cols), lambda i, chip_ref: (chip_ref[0], i, 0)),
                  pl.BlockSpec((3, tr, cols), lambda i, chip_ref: (0, i, 0))],
        out_specs=pl.BlockSpec((tr, cols), lambda i, chip_ref: (i, 0)))
    return pl.pallas_call(
        body, grid_spec=grid_spec, out_shape=jax.ShapeDtypeStruct((r, cols), F32),
        compiler_params=_cp("parallel"), name="sum_shards")(chip, gbig, got)


def _swap_sibling(t, *, name):
    def body(t_ref, got_ref, send_sem, recv_sem):
        x, y, c = _my_place()
        cp = pltpu.make_async_remote_copy(src_ref=t_ref, dst_ref=got_ref, send_sem=send_sem, recv_sem=recv_sem,
                                          device_id=(x, y, 1 - c), device_id_type=MESH)
        cp.start()
        cp.wait()

    hbm = pl.BlockSpec(memory_space=pltpu.HBM)
    return pl.pallas_call(
        body, in_specs=[hbm], out_specs=hbm, out_shape=jax.ShapeDtypeStruct(t.shape, t.dtype),
        scratch_shapes=[pltpu.SemaphoreType.DMA, pltpu.SemaphoreType.DMA], name=name)(t)


def _add2(a, b, *, name):
    n, r, cols = a.shape
    tr = _tile(r, PACK_TILE_ROWS, 8)

    def body(a_ref, b_ref, o_ref):
        o_ref[...] = a_ref[...] + b_ref[...]

    blk = pl.BlockSpec((n, tr, cols), lambda i: (0, i, 0))
    return pl.pallas_call(
        body, grid=(r // tr,), in_specs=[blk, blk], out_specs=blk, out_shape=jax.ShapeDtypeStruct(a.shape, F32),
        compiler_params=_cp("parallel"), name=name)(a, b)


def _by_core(c, mine, sibling):
    return jnp.where(c == 0, mine, sibling), jnp.where(c == 0, sibling, mine)


def _adamw(wt, m, v, g, *, name):
    r, cols = wt.shape
    tr = _tile(r, 256, 8)

    def body(w_ref, m_ref, v_ref, g_ref, d_ref, nm_ref, nv_ref):
        gv = g_ref[...]
        nm = ADAM_B1 * m_ref[...] + (1.0 - ADAM_B1) * gv
        nv = ADAM_B2 * v_ref[...] + (1.0 - ADAM_B2) * (gv * gv)
        m_hat = nm / (1.0 - ADAM_B1 ** ADAM_STEP)
        v_hat = nv / (1.0 - ADAM_B2 ** ADAM_STEP)
        d_ref[...] = -ADAM_LR * (m_hat / (jnp.sqrt(v_hat) + ADAM_EPS) + ADAM_WD * w_ref[...])
        nm_ref[...] = nm
        nv_ref[...] = nv

    blk = pl.BlockSpec((tr, cols), lambda i: (i, 0))
    return pl.pallas_call(
        body, grid=(r // tr,), in_specs=[blk] * 4, out_specs=[blk] * 3,
        out_shape=[jax.ShapeDtypeStruct((r, cols), F32)] * 3,
        compiler_params=_cp("parallel"), name=name)(wt, m, v, g)


def _shard_shape(full, axis):
    return tuple(d // N_CHIPS if a == axis else d for a, d in enumerate(full))


def _pack_rows(n):
    rows = -(-n // PACK_COLS)
    return -(-rows // PACK_ROW_MULT) * PACK_ROW_MULT


def _pack(parts, rows, dtype):
    flat = jnp.concatenate([p.reshape(-1).astype(dtype) for p in parts])
    return jnp.pad(flat, (0, rows * PACK_COLS - flat.shape[0])).reshape(rows, PACK_COLS)


def _join_shards(stacked, axis):
    moved = jnp.moveaxis(stacked, 0, axis)
    shp = moved.shape
    return moved.reshape(shp[:axis] + (shp[axis] * shp[axis + 1],) + shp[axis + 2:])


def _split_shards(full, axis):
    shp = full.shape
    split = full.reshape(shp[:axis] + (N_CHIPS, shp[axis] // N_CHIPS) + shp[axis + 1:])
    return jnp.moveaxis(split, axis, 0)


SMALL_ROWS = 128
SMALL_LAYOUT = [('mix_norm', DEPTH * D_MODEL), ('ffn_norm', DEPTH * D_MODEL), ('b_conv_w', 3 * D_MODEL),
                ('c_q_a_norm', C_Q_RANK), ('c_kv_a_norm', C_KV_RANK), ('a_q_norm', 2 * A_HEAD_DIM),
                ('a_k_norm', 2 * A_HEAD_DIM), ('a_sinks', 2 * A_HEADS), ('c_q_norm', C_QK), ('c_k_norm', C_QK),
                ('loss', 1)]


def _small_offsets():
    offs, row = {}, 0
    for name, n in SMALL_LAYOUT:
        offs[name] = (row * LANES, n)
        row += -(-n // LANES)
    assert row <= SMALL_ROWS
    return offs


def kernel(x, positions, mix_norm, ffn_norm, a_w_qkv, a_q_norm, a_k_norm, a_sinks, a_w_o, b_w_in, b_conv_w, b_w_out, c_w_down, c_q_a_norm, c_kv_a_norm, c_w_q_up, c_w_kv_up, c_q_norm, c_k_norm, c_w_o, f_w_gate_up, f_w_down, loss_target, m_mix_norm, m_ffn_norm, m_a_w_qkv, m_a_q_norm, m_a_k_norm, m_a_sinks, m_a_w_o, m_b_w_in, m_b_conv_w, m_b_w_out, m_c_w_down, m_c_q_a_norm, m_c_kv_a_norm, m_c_w_q_up, m_c_w_kv_up, m_c_q_norm, m_c_k_norm, m_c_w_o, m_f_w_gate_up, m_f_w_down, v_mix_norm, v_ffn_norm, v_a_w_qkv, v_a_q_norm, v_a_k_norm, v_a_sinks, v_a_w_o, v_b_w_in, v_b_conv_w, v_b_w_out, v_c_w_down, v_c_q_a_norm, v_c_kv_a_norm, v_c_w_q_up, v_c_w_kv_up, v_c_q_norm, v_c_k_norm, v_c_w_o, v_f_w_gate_up, v_f_w_down):
    args = dict(locals())
    wshard = {n: args[n] for n in WEIGHTS}
    sharded = {**BIG, **SMALL_SHARDED}
    chip = 2 * lax.axis_index("x") + lax.axis_index("y")

    n_big = sum(wshard[n].size for n in BIG)
    rows = _pack_rows(n_big)
    big = _pack([wshard[n] for n in BIG], rows, BF)
    small = jnp.concatenate([wshard[n].reshape(-1) for n in SMALL_SHARDED])
    small = jnp.pad(small, (0, 8 * LANES - small.shape[0])).reshape(8, LANES)
    core = lax.axis_index("c")
    half = rows // 2
    mine_all, small_all = _gather_weights(lax.dynamic_slice_in_dim(big, core * half, half, axis=0), small)
    sib_all = _swap_sibling(mine_all, name="swap_weights")
    big_all = jnp.concatenate(_by_core(core, mine_all, sib_all), axis=1)
    big_all = big_all.reshape(N_CHIPS, -1)
    small_all = small_all.reshape(N_CHIPS, -1)
    w = {}
    off = 0
    for n, axis in BIG.items():
        sz = wshard[n].size
        w[n] = _join_shards(big_all[:, off:off + sz].reshape((N_CHIPS,) + wshard[n].shape), axis)
        off += sz
    off = 0
    for n, axis in SMALL_SHARDED.items():
        sz = wshard[n].size
        w[n] = _join_shards(small_all[:, off:off + sz].reshape((N_CHIPS,) + wshard[n].shape), axis)
        off += sz
    for n in WEIGHTS:
        if n not in sharded:
            w[n] = wshard[n]

    loss_blk, grad_x, g = _local_step(x[0], positions[0], w, loss_target[0])

    gbig = jnp.concatenate([_split_shards(g[n], BIG[n]).reshape(N_CHIPS, -1) for n in BIG], axis=1)
    gbig = jnp.pad(gbig, ((0, 0), (0, rows * PACK_COLS - n_big))).reshape(N_CHIPS, rows, PACK_COLS)
    offs = _small_offsets()
    smalls = {**{n: g[n] for n, _ in SMALL_LAYOUT if n != 'loss'}, 'loss': loss_blk[0:1, 0:1]}
    gsmall = jnp.concatenate(
        [jnp.pad(smalls[n].reshape(-1), (0, -cnt % LANES)) for n, cnt in SMALL_LAYOUT])
    gsmall = jnp.pad(gsmall, (0, SMALL_ROWS * LANES - gsmall.shape[0])).reshape(SMALL_ROWS, LANES)
    keep = lax.dynamic_slice_in_dim(gbig, core * half, half, axis=1)
    give = lax.dynamic_slice_in_dim(gbig, (1 - core) * half, half, axis=1)
    chip_sum = _add2(keep, _swap_sibling(give, name="swap_grads"), name="add_sibling")
    got, ssum = _exchange_grads(chip_sum, gsmall)
    part = _sum_shards(chip.reshape(1).astype(jnp.int32), chip_sum, got)
    part_sib = _swap_sibling(part, name="swap_sums")
    gflat = jnp.concatenate(_by_core(core, part, part_sib), axis=0).reshape(-1)

    ssum = ssum.reshape(-1)
    outs = {}
    off = 0
    for n in WEIGHTS:
        wt = wshard[n]
        if n in BIG:
            grad = gflat[off:off + wt.size]
            off += wt.size
        else:
            o0, cnt = offs[n]
            grad = ssum[o0:o0 + cnt].reshape(g[n].shape)
            if n in SMALL_SHARDED:
                grad = lax.dynamic_index_in_dim(_split_shards(grad, SMALL_SHARDED[n]), chip, 0, keepdims=False)
        shape2 = (-1, wt.shape[-1])
        grad = grad.reshape(shape2)
        res = _adamw(wt.reshape(shape2), args['m_' + n].reshape(shape2), args['v_' + n].reshape(shape2), grad,
                     name=f"adamw_{n}")
        outs[n] = [r.reshape(wt.shape) for r in [grad] + list(res)]
    loss = ssum[offs['loss'][0]]
    return (loss, grad_x[None], *[outs[n][0] for n in WEIGHTS], *[outs[n][1] for n in WEIGHTS],
            *[outs[n][2] for n in WEIGHTS], *[outs[n][3] for n in WEIGHTS])
```

```python
import functools

import jax
import jax.numpy as jnp
from jax import lax
from jax.experimental import pallas as pl
from jax.experimental.pallas import tpu as pltpu

F32 = jnp.float32
BF = jnp.bfloat16

D_MODEL = 1024
DEPTH = 4
N_MIXERS = 3
ROPE_THETA = 500000.0
EPS = 1e-6
BLOCK = 128
LANES = 128

A_HEADS, A_KV_HEADS, A_HEAD_DIM = 16, 4, 64
A_ROT_DIM = A_HEAD_DIM // 4
A_GROUP = A_HEADS // A_KV_HEADS
C_HEADS, C_NOPE, C_ROPE, C_V, C_Q_RANK, C_KV_RANK = 16, 64, 32, 64, 384, 256
C_QK = C_NOPE + C_ROPE
D_FF = 2816

ADAM_LR, ADAM_B1, ADAM_B2, ADAM_EPS, ADAM_WD, ADAM_STEP = 0.001, 0.9, 0.999, 1e-08, 0.01, 10

N_CHIPS = 4
N_DEV = 8
MESH = pl.DeviceIdType.MESH

VMEM_LIMIT = 56 * 1024 * 1024
ROW_TILE = 512
PREP_TILE = 256
ATTN_TILE = 512
FWD_WIDE = 4
SWA_BLOCKS = 4
NEG = -1e30

WEIGHTS = ['mix_norm', 'ffn_norm', 'a_w_qkv', 'a_q_norm', 'a_k_norm', 'a_sinks', 'a_w_o', 'b_w_in', 'b_conv_w',
           'b_w_out', 'c_w_down', 'c_q_a_norm', 'c_kv_a_norm', 'c_w_q_up', 'c_w_kv_up', 'c_q_norm', 'c_k_norm',
           'c_w_o', 'f_w_gate_up', 'f_w_down']
BIG = {'a_w_qkv': 2, 'a_w_o': 1, 'b_w_in': 2, 'b_w_out': 1, 'c_w_down': 1, 'c_w_q_up': 2, 'c_w_kv_up': 2,
       'c_w_o': 1, 'f_w_gate_up': 2, 'f_w_down': 1}
SMALL_SHARDED = {'b_conv_w': 2, 'c_q_a_norm': 1, 'c_kv_a_norm': 1}
PACK_COLS = 256
PACK_ROW_MULT = 32
PACK_TILE_ROWS = 1024


def _cp(*sem):
    return pltpu.CompilerParams(dimension_semantics=sem, vmem_limit_bytes=VMEM_LIMIT)


def _tile(n, target, mult):
    if n <= target:
        return n
    t = (target // mult) * mult
    while t >= mult:
        if n % t == 0:
            return t
        t -= mult
    raise ValueError(f"no tile for {n}")


def _dot(a, b):
    return lax.dot_general(a, b, (((1,), (0,)), ((), ())), preferred_element_type=F32)


def _dot_nt(a, b):
    return lax.dot_general(a, b, (((1,), (1,)), ((), ())), preferred_element_type=F32)


def _dot_tn(a, b):
    return lax.dot_general(a, b, (((0,), (0,)), ((), ())), preferred_element_type=F32)


def _mm(pairs, *, out_dtype, name, res=None, trans_b=False):
    a0, b0, _ = pairs[0]
    m = a0.shape[0]
    n = b0.shape[0] if trans_b else b0.shape[1]
    tm = _tile(m, ROW_TILE, 8)
    tn = _tile(n, 1024, 128)
    n_pairs = len(pairs)
    has_res = res is not None

    def body(*refs):
        o_ref = refs[-1]
        acc = None
        for p in range(n_pairs):
            a = refs[2 * p][...].astype(BF)
            b = refs[2 * p + 1][...].astype(BF)
            d = _dot_nt(a, b) if trans_b else _dot(a, b)
            acc = d if acc is None else acc + d
        if has_res:
            acc = acc + refs[2 * n_pairs][...]
        o_ref[...] = acc.astype(out_dtype)

    in_specs, args = [], []
    for a, b, kblk in pairs:
        k = a.shape[1]
        in_specs.append(pl.BlockSpec((tm, k), lambda j, i: (i, 0)))
        if trans_b:
            in_specs.append(pl.BlockSpec((tn, k), functools.partial(lambda j, i, kb: (j, kb), kb=kblk)))
        else:
            in_specs.append(pl.BlockSpec((k, tn), lambda j, i: (0, j)))
        args += [a, b]
    if has_res:
        in_specs.append(pl.BlockSpec((tm, tn), lambda j, i: (i, j)))
        args.append(res)
    return pl.pallas_call(
        body, grid=(n // tn, m // tm), in_specs=in_specs,
        out_specs=pl.BlockSpec((tm, tn), lambda j, i: (i, j)),
        out_shape=jax.ShapeDtypeStruct((m, n), out_dtype),
        compiler_params=_cp("parallel", "parallel"), name=name)(*args)


def _mm_tn(a, b, *, name):
    m, k = a.shape
    n = b.shape[1]
    tm = _tile(m, ROW_TILE, 8)
    tk = _tile(k, 1408, 128)
    tn = _tile(n, 1408, 128)

    def body(a_ref, b_ref, o_ref):
        @pl.when(pl.program_id(2) == 0)
        def _():
            o_ref[...] = jnp.zeros_like(o_ref)

        o_ref[...] += _dot_tn(a_ref[...].astype(BF), b_ref[...].astype(BF))

    return pl.pallas_call(
        body, grid=(k // tk, n // tn, m // tm),
        in_specs=[pl.BlockSpec((tm, tk), lambda kk, j, i: (i, kk)),
                  pl.BlockSpec((tm, tn), lambda kk, j, i: (i, j))],
        out_specs=pl.BlockSpec((tk, tn), lambda kk, j, i: (kk, j)),
        out_shape=jax.ShapeDtypeStruct((k, n), F32),
        compiler_params=_cp("parallel", "parallel", "arbitrary"), name=name)(a, b)


def _rmsnorm_fwd(x, g, *, name):
    s, d = x.shape
    tm = _tile(s, ROW_TILE, 8)

    def body(x_ref, g_ref, h_ref):
        xv = x_ref[...]
        y = xv * lax.rsqrt(jnp.mean(xv * xv, axis=-1, keepdims=True) + EPS)
        h_ref[...] = (y * g_ref[...]).astype(BF)

    return pl.pallas_call(
        body, grid=(s // tm,),
        in_specs=[pl.BlockSpec((tm, d), lambda i: (i, 0)), pl.BlockSpec((1, d), lambda i: (0, 0))],
        out_specs=pl.BlockSpec((tm, d), lambda i: (i, 0)),
        out_shape=jax.ShapeDtypeStruct((s, d), BF),
        compiler_params=_cp("parallel"), name=name)(x, g.reshape(1, d))


def _rms_bwd_math(xv, g, dh, n):
    rstd = lax.rsqrt(jnp.sum(xv * xv, axis=-1, keepdims=True) * (1.0 / n) + EPS)
    xhat = xv * rstd
    dxh = dh * g
    dx = rstd * (dxh - xhat * (jnp.sum(dxh * xhat, axis=-1, keepdims=True) * (1.0 / n)))
    return dx, xhat


def _rmsnorm_bwd(x, g, dh, dres, *, name):
    s, d = x.shape
    tm = _tile(s, ROW_TILE, 8)

    def body(x_ref, g_ref, dh_ref, dres_ref, dx_ref, dg_ref):
        @pl.when(pl.program_id(0) == 0)
        def _():
            dg_ref[...] = jnp.zeros_like(dg_ref)

        dhv = dh_ref[...]
        dx, xhat = _rms_bwd_math(x_ref[...], g_ref[...], dhv, d)
        dx_ref[...] = dres_ref[...] + dx
        dg_ref[0:1, :] += jnp.sum(dhv * xhat, axis=0, keepdims=True)

    row = pl.BlockSpec((tm, d), lambda i: (i, 0))
    dx, dg = pl.pallas_call(
        body, grid=(s // tm,),
        in_specs=[row, pl.BlockSpec((1, d), lambda i: (0, 0)), row, row],
        out_specs=[row, pl.BlockSpec((8, d), lambda i: (0, 0))],
        out_shape=[jax.ShapeDtypeStruct((s, d), F32), jax.ShapeDtypeStruct((8, d), F32)],
        compiler_params=_cp("arbitrary"), name=name)(x, g.reshape(1, d), dh, dres)
    return dx, dg[0:1]


def _loss_fwd_bwd(y, target, *, name):
    s, d = y.shape
    tm = _tile(s, ROW_TILE, 8)

    def body(y_ref, t_ref, loss_ref, dy_ref):
        @pl.when(pl.program_id(0) == 0)
        def _():
            loss_ref[...] = jnp.zeros_like(loss_ref)

        err = y_ref[...] - t_ref[...]
        dy_ref[...] = err * (1.0 / d)
        loss_ref[...] += 0.5 * jnp.sum(jnp.sum(err * err, axis=-1, keepdims=True) * (1.0 / d))

    row = pl.BlockSpec((tm, d), lambda i: (i, 0))
    return pl.pallas_call(
        body, grid=(s // tm,), in_specs=[row, row],
        out_specs=[pl.BlockSpec((8, LANES), lambda i: (0, 0)), row],
        out_shape=[jax.ShapeDtypeStruct((8, LANES), F32), jax.ShapeDtypeStruct((s, d), F32)],
        compiler_params=_cp("arbitrary"), name=name)(y, target)


HALF = LANES // 2
A_HR, C_HR = A_ROT_DIM // 2, C_ROPE // 2
A_X1, C_X1 = 0, 32
LAYOUT_A = [(A_X1, 0, A_HR), (A_HR, A_ROT_DIM, A_HEAD_DIM - A_ROT_DIM), (HALF + A_X1, A_HR, A_HR)]
LAYOUT_C = [(0, 0, 32), (C_X1, C_NOPE, C_HR), (HALF, 32, 32), (HALF + C_X1, C_NOPE + C_HR, C_HR)]
LAYOUT_KN = [(0, 0, 32), (HALF, 32, 32)]
LAYOUT_KR = [(C_X1, 0, C_HR), (HALF + C_X1, C_HR, C_HR)]
LAYOUT_V = [(0, 0, C_V)]


def _rope_rows():
    lane = jnp.arange(LANES)
    fa = ROPE_THETA ** (-jnp.arange(0, A_ROT_DIM, 2, dtype=F32) / A_ROT_DIM)
    fc = ROPE_THETA ** (-jnp.arange(0, C_ROPE, 2, dtype=F32) / C_ROPE)

    def rows(f, x1, hr):
        first = (lane >= x1) & (lane < x1 + hr)
        second = (lane >= HALF + x1) & (lane < HALF + x1 + hr)
        freq = jnp.where(first | second, f[(lane - x1) % HALF % hr], 0.0)
        return freq, jnp.where(first, -1.0, jnp.where(second, 1.0, 0.0))

    freq_a, sign_a = rows(fa, A_X1, A_HR)
    freq_c, sign_c = rows(fc, C_X1, C_HR)
    return jnp.stack([freq_a, sign_a, freq_c, sign_c] + [jnp.zeros(LANES)] * 4).astype(F32)


def _rope_tables(pos_col, *, name):
    s = pos_col.shape[0]
    tm = _tile(s, ROW_TILE, 8)

    def body(pos_ref, rows_ref, ca_ref, sa_ref, cc_ref, sc_ref):
        p = pos_ref[...].astype(F32)
        ang_a = p * rows_ref[0:1, :]
        ang_c = p * rows_ref[2:3, :]
        ca_ref[...] = jnp.cos(ang_a)
        sa_ref[...] = jnp.sin(ang_a) * rows_ref[1:2, :]
        cc_ref[...] = jnp.cos(ang_c)
        sc_ref[...] = jnp.sin(ang_c) * rows_ref[3:4, :]

    tab = pl.BlockSpec((tm, LANES), lambda i: (i, 0))
    return pl.pallas_call(
        body, grid=(s // tm,),
        in_specs=[pl.BlockSpec((tm, 1), lambda i: (i, 0)), pl.BlockSpec((8, LANES), lambda i: (0, 0))],
        out_specs=[tab] * 4, out_shape=[jax.ShapeDtypeStruct((s, LANES), F32)] * 4,
        compiler_params=_cp("parallel"), name=name)(pos_col, _rope_rows())


def _rope(xv, cos, sin):
    return xv * cos + pltpu.roll(xv, HALF, 1) * sin


def _slot(ref, t):
    return ref[:, t * LANES:(t + 1) * LANES]


def _head_norm_rope(xv, g, cos, sin, n):
    y = xv * lax.rsqrt(jnp.sum(xv * xv, axis=-1, keepdims=True) * (1.0 / n) + EPS) * g
    return _rope(y, cos, sin)


def _head_norm_rope_bwd(xv, g, dy, cos, sin, n):
    dyn = _rope(dy, cos, -sin)
    dx, xhat = _rms_bwd_math(xv, g, dyn, n)
    return dx, jnp.sum(dyn * xhat, axis=0, keepdims=True)


def _prep_a_fwd(qkv, cos, sin, gq, gk, *, name):
    s = qkv.shape[0]
    tm = _tile(s, PREP_TILE, 8)
    nq, nkv = A_HEADS, A_KV_HEADS

    def body(x_ref, c_ref, s_ref, gq_ref, gk_ref, q_ref, k_ref, v_ref):
        cv, sv = c_ref[...], s_ref[...]
        for t in range(nq):
            q_ref[:, t * LANES:(t + 1) * LANES] = _head_norm_rope(
                _slot(x_ref, t), gq_ref[...], cv, sv, A_HEAD_DIM).astype(BF)
        for t in range(nkv):
            k_ref[:, t * LANES:(t + 1) * LANES] = _head_norm_rope(
                _slot(x_ref, nq + t), gk_ref[...], cv, sv, A_HEAD_DIM).astype(BF)
        v_ref[...] = x_ref[:, (nq + nkv) * LANES:].astype(BF)

    def rows(w):
        return pl.BlockSpec((tm, w), lambda i: (i, 0))

    vec = pl.BlockSpec((1, LANES), lambda i: (0, 0))
    return pl.pallas_call(
        body, grid=(s // tm,),
        in_specs=[rows(qkv.shape[1]), rows(LANES), rows(LANES), vec, vec],
        out_specs=[rows(nq * LANES), rows(nkv * LANES), rows(nkv * LANES)],
        out_shape=[jax.ShapeDtypeStruct((s, nq * LANES), BF), jax.ShapeDtypeStruct((s, nkv * LANES), BF),
                   jax.ShapeDtypeStruct((s, nkv * LANES), BF)],
        compiler_params=_cp("parallel"), name=name)(qkv, cos, sin, gq, gk)


def _prep_a_bwd(qkv, dq, dk, dv, cos, sin, gq, gk, *, name):
    s = qkv.shape[0]
    tm = _tile(s, PREP_TILE, 8)
    nq, nkv = A_HEADS, A_KV_HEADS

    def body(x_ref, dq_ref, dk_ref, dv_ref, c_ref, s_ref, gq_ref, gk_ref, dx_ref, dg_ref):
        @pl.when(pl.program_id(0) == 0)
        def _():
            dg_ref[...] = jnp.zeros_like(dg_ref)

        cv, sv = c_ref[...], s_ref[...]
        dgq = jnp.zeros((1, LANES), F32)
        dgk = jnp.zeros((1, LANES), F32)
        for t in range(nq):
            dx, dg = _head_norm_rope_bwd(_slot(x_ref, t), gq_ref[...], _slot(dq_ref, t), cv, sv,
                                         A_HEAD_DIM)
            dx_ref[:, t * LANES:(t + 1) * LANES] = dx.astype(BF)
            dgq = dgq + dg
        for t in range(nkv):
            dx, dg = _head_norm_rope_bwd(_slot(x_ref, nq + t), gk_ref[...], _slot(dk_ref, t), cv, sv,
                                         A_HEAD_DIM)
            dx_ref[:, (nq + t) * LANES:(nq + t + 1) * LANES] = dx.astype(BF)
            dgk = dgk + dg
        dx_ref[:, (nq + nkv) * LANES:] = dv_ref[...].astype(BF)
        dg_ref[0:1, :] += dgq
        dg_ref[1:2, :] += dgk

    def rows(w):
        return pl.BlockSpec((tm, w), lambda i: (i, 0))

    vec = pl.BlockSpec((1, LANES), lambda i: (0, 0))
    dx, dg = pl.pallas_call(
        body, grid=(s // tm,),
        in_specs=[rows(qkv.shape[1]), rows(nq * LANES), rows(nkv * LANES), rows(nkv * LANES), rows(LANES),
                  rows(LANES), vec, vec],
        out_specs=[rows(qkv.shape[1]), pl.BlockSpec((8, LANES), lambda i: (0, 0))],
        out_shape=[jax.ShapeDtypeStruct(qkv.shape, BF), jax.ShapeDtypeStruct((8, LANES), F32)],
        compiler_params=_cp("arbitrary"), name=name)(qkv, dq, dk, dv, cos, sin, gq, gk)
    return dx, dg[0:1], dg[1:2]


def _prep_c1_fwd(dn, gq, gkv, *, name):
    s = dn.shape[0]
    tm = _tile(s, ROW_TILE, 8)

    def body(x_ref, gq_ref, gkv_ref, cq_ref, ckv_ref):
        for lo, n, g_ref, o_ref in ((0, C_Q_RANK, gq_ref, cq_ref), (C_Q_RANK, C_KV_RANK, gkv_ref, ckv_ref)):
            xv = x_ref[:, lo:lo + n]
            y = xv * lax.rsqrt(jnp.mean(xv * xv, axis=-1, keepdims=True) + EPS)
            o_ref[...] = (y * g_ref[...]).astype(BF)

    def rows(w):
        return pl.BlockSpec((tm, w), lambda i: (i, 0))

    return pl.pallas_call(
        body, grid=(s // tm,),
        in_specs=[rows(dn.shape[1]), pl.BlockSpec((1, C_Q_RANK), lambda i: (0, 0)),
                  pl.BlockSpec((1, C_KV_RANK), lambda i: (0, 0))],
        out_specs=[rows(C_Q_RANK), rows(C_KV_RANK)],
        out_shape=[jax.ShapeDtypeStruct((s, C_Q_RANK), BF), jax.ShapeDtypeStruct((s, C_KV_RANK), BF)],
        compiler_params=_cp("parallel"), name=name)(dn, gq, gkv)


def _prep_c1_bwd(dn, dcq, dckv, dkr, gq, gkv, *, name):
    s = dn.shape[0]
    tm = _tile(s, ROW_TILE, 8)

    def body(x_ref, dcq_ref, dckv_ref, dkr_ref, gq_ref, gkv_ref, dx_ref, dgq_ref, dgkv_ref):
        @pl.when(pl.program_id(0) == 0)
        def _():
            dgq_ref[...] = jnp.zeros_like(dgq_ref)
            dgkv_ref[...] = jnp.zeros_like(dgkv_ref)

        for lo, n, g_ref, d_ref, dg_ref in ((0, C_Q_RANK, gq_ref, dcq_ref, dgq_ref),
                                            (C_Q_RANK, C_KV_RANK, gkv_ref, dckv_ref, dgkv_ref)):
            dv = d_ref[...]
            dx, xhat = _rms_bwd_math(x_ref[:, lo:lo + n], g_ref[...], dv, n)
            dx_ref[:, lo:lo + n] = dx.astype(BF)
            dg_ref[0:1, :] += jnp.sum(dv * xhat, axis=0, keepdims=True)
        dx_ref[:, C_Q_RANK + C_KV_RANK:] = dkr_ref[...].astype(BF)

    def rows(w):
        return pl.BlockSpec((tm, w), lambda i: (i, 0))

    def vec(w, r=1):
        return pl.BlockSpec((r, w), lambda i: (0, 0))

    dx, dgq, dgkv = pl.pallas_call(
        body, grid=(s // tm,),
        in_specs=[rows(dn.shape[1]), rows(C_Q_RANK), rows(C_KV_RANK), rows(LANES), vec(C_Q_RANK), vec(C_KV_RANK)],
        out_specs=[rows(dn.shape[1]), vec(C_Q_RANK, 8), vec(C_KV_RANK, 8)],
        out_shape=[jax.ShapeDtypeStruct(dn.shape, BF), jax.ShapeDtypeStruct((8, C_Q_RANK), F32),
                   jax.ShapeDtypeStruct((8, C_KV_RANK), F32)],
        compiler_params=_cp("arbitrary"), name=name)(dn, dcq, dckv, dkr, gq, gkv)
    return dx, dgq[0:1], dgkv[0:1]


def _prep_c2_fwd(qraw, knope, dn, cos, sin, gq, gk, *, name):
    s = qraw.shape[0]
    tm = _tile(s, PREP_TILE, 8)
    kr_blk = dn.shape[1] // LANES - 1

    def body(q_ref, kn_ref, kr_ref, c_ref, s_ref, gq_ref, gk_ref, qo_ref, ko_ref):
        cv, sv, kr = c_ref[...], s_ref[...], kr_ref[...]
        for t in range(C_HEADS):
            qo_ref[:, t * LANES:(t + 1) * LANES] = _head_norm_rope(
                _slot(q_ref, t), gq_ref[...], cv, sv, C_QK).astype(BF)
            ko_ref[:, t * LANES:(t + 1) * LANES] = _head_norm_rope(
                _slot(kn_ref, t) + kr, gk_ref[...], cv, sv, C_QK).astype(BF)

    def rows(w, blk=0):
        return pl.BlockSpec((tm, w), lambda i: (i, blk))

    vec = pl.BlockSpec((1, LANES), lambda i: (0, 0))
    w = C_HEADS * LANES
    return pl.pallas_call(
        body, grid=(s // tm,),
        in_specs=[rows(w), rows(w), rows(LANES, kr_blk), rows(LANES), rows(LANES), vec, vec],
        out_specs=[rows(w), rows(w)],
        out_shape=[jax.ShapeDtypeStruct((s, w), BF)] * 2,
        compiler_params=_cp("parallel"), name=name)(qraw, knope, dn, cos, sin, gq, gk)


def _prep_c2_bwd(qraw, knope, dn, dq, dk, cos, sin, gq, gk, *, name):
    s = qraw.shape[0]
    tm = _tile(s, PREP_TILE, 8)
    kr_blk = dn.shape[1] // LANES - 1

    def body(q_ref, kn_ref, kr_ref, dq_ref, dk_ref, c_ref, s_ref, gq_ref, gk_ref,
             dqo_ref, dkno_ref, dkr_ref, dg_ref):
        @pl.when(pl.program_id(0) == 0)
        def _():
            dg_ref[...] = jnp.zeros_like(dg_ref)

        cv, sv, kr = c_ref[...], s_ref[...], kr_ref[...]
        dgq = jnp.zeros((1, LANES), F32)
        dgk = jnp.zeros((1, LANES), F32)
        dkr = jnp.zeros((tm, LANES), F32)
        for t in range(C_HEADS):
            dx, dg = _head_norm_rope_bwd(_slot(q_ref, t), gq_ref[...], _slot(dq_ref, t), cv, sv,
                                         C_QK)
            dqo_ref[:, t * LANES:(t + 1) * LANES] = dx.astype(BF)
            dgq = dgq + dg
            dx, dg = _head_norm_rope_bwd(_slot(kn_ref, t) + kr, gk_ref[...], _slot(dk_ref, t), cv, sv,
                                         C_QK)
            dkno_ref[:, t * LANES:(t + 1) * LANES] = dx.astype(BF)
            dkr = dkr + dx
            dgk = dgk + dg
        dkr_ref[...] = dkr
        dg_ref[0:1, :] += dgq
        dg_ref[1:2, :] += dgk

    def rows(w, blk=0):
        return pl.BlockSpec((tm, w), lambda i: (i, blk))

    vec = pl.BlockSpec((1, LANES), lambda i: (0, 0))
    w = C_HEADS * LANES
    dqo, dkno, dkr, dg = pl.pallas_call(
        body, grid=(s // tm,),
        in_specs=[rows(w), rows(w), rows(LANES, kr_blk), rows(w), rows(w), rows(LANES), rows(LANES), vec, vec],
        out_specs=[rows(w), rows(w), rows(LANES), pl.BlockSpec((8, LANES), lambda i: (0, 0))],
        out_shape=[jax.ShapeDtypeStruct((s, w), BF), jax.ShapeDtypeStruct((s, w), BF),
                   jax.ShapeDtypeStruct((s, LANES), F32), jax.ShapeDtypeStruct((8, LANES), F32)],
        compiler_params=_cp("arbitrary"), name=name)(qraw, knope, dn, dq, dk, cos, sin, gq, gk)
    return dqo, dkno, dkr, dg[0:1], dg[1:2]


def _lane_pick(mat, g):
    lane = lax.broadcasted_iota(jnp.int32, mat.shape, 1)
    return jnp.sum(jnp.where(lane == g, mat, 0.0), axis=-1, keepdims=True)


def _swa_fwd(q, k, v, sinks, *, name):
    s = q.shape[0]
    nb = min(SWA_BLOCKS, s // BLOCK)
    qt = nb * BLOCK
    scale = A_HEAD_DIM ** -0.5
    gw = A_GROUP * LANES

    def body(sink_ref, q_ref, kc_ref, kp_ref, vc_ref, vp_ref, o_ref, lse_ref):
        kvh, n = pl.program_id(0), pl.program_id(1)
        rows = A_GROUP * BLOCK
        kall = jnp.concatenate([kp_ref[...], kc_ref[...]], axis=0)
        vall = jnp.concatenate([vp_ref[...], vc_ref[...]], axis=0)
        r = lax.broadcasted_iota(jnp.int32, (rows, 2 * BLOCK), 0) & (BLOCK - 1)
        c = lax.broadcasted_iota(jnp.int32, (rows, 2 * BLOCK), 1)
        cur_ok = (c >= BLOCK) & (c - BLOCK <= r)
        prev_ok = (c < BLOCK) & (c > r)
        head = lax.broadcasted_iota(jnp.int32, (rows, 1), 0) >> 7
        sink = jnp.zeros((rows, 1), F32)
        for g in range(A_GROUP):
            sink = jnp.where(head == g, sink_ref[kvh * A_GROUP + g], sink)
        lane = lax.broadcasted_iota(jnp.int32, (BLOCK, LANES), 1)
        for b in range(nb):
            blk = slice(b * BLOCK, (b + 1) * BLOCK)
            qs = jnp.concatenate([q_ref[blk, g * LANES:(g + 1) * LANES] for g in range(A_GROUP)], axis=0)
            k2 = kall[b * BLOCK:(b + 2) * BLOCK]
            v2 = vall[b * BLOCK:(b + 2) * BLOCK]
            ok = (cur_ok | prev_ok) if b > 0 else (cur_ok | (prev_ok & (n > 0)))
            sc = jnp.where(ok, _dot_nt(qs, k2) * scale, NEG)
            m = jnp.maximum(jnp.max(sc, axis=-1, keepdims=True), sink)
            p = jnp.exp(sc - m)
            l = jnp.sum(p, axis=-1, keepdims=True) + jnp.exp(sink - m)
            o = _dot((p * (1.0 / l)).astype(BF), v2)
            lse = m + jnp.log(l)
            lse_mat = jnp.zeros((BLOCK, LANES), F32)
            for g in range(A_GROUP):
                o_ref[blk, g * LANES:(g + 1) * LANES] = o[g * BLOCK:(g + 1) * BLOCK].astype(BF)
                lse_mat = jnp.where(lane == g, lse[g * BLOCK:(g + 1) * BLOCK], lse_mat)
            lse_ref[0, blk, :] = lse_mat

    cur = pl.BlockSpec((qt, LANES), lambda h, n: (n, h))
    prev = pl.BlockSpec((BLOCK, LANES), lambda h, n: (jnp.maximum(n * nb - 1, 0), h))
    return pl.pallas_call(
        body, grid=(A_KV_HEADS, s // qt),
        in_specs=[pl.BlockSpec(memory_space=pltpu.SMEM), pl.BlockSpec((qt, gw), lambda h, n: (n, h)),
                  cur, prev, cur, prev],
        out_specs=[pl.BlockSpec((qt, gw), lambda h, n: (n, h)),
                   pl.BlockSpec((1, qt, LANES), lambda h, n: (h, n, 0))],
        out_shape=[jax.ShapeDtypeStruct(q.shape, BF), jax.ShapeDtypeStruct((A_KV_HEADS, s, LANES), F32)],
        compiler_params=_cp("parallel", "parallel"), name=name)(sinks, q, k, k, v, v)


def _swa_bwd(q, k, v, o, do, lse, sinks, *, name):
    s = q.shape[0]
    nblk = s // BLOCK
    nb = min(SWA_BLOCKS, nblk)
    qt = nb * BLOCK
    nsteps = s // qt
    scale = A_HEAD_DIM ** -0.5
    gw = A_GROUP * LANES

    def body(sink_ref, qc_ref, qn_ref, k_ref, v_ref, oc_ref, on_ref, doc_ref, don_ref, lc_ref, ln_ref,
             dq_ref, dk_ref, dv_ref, dsink_ref, carry):
        kvh, n = pl.program_id(0), pl.program_id(1)

        @pl.when(n == 0)
        def _():
            carry[...] = jnp.zeros_like(carry)
            dsink_ref[...] = jnp.zeros_like(dsink_ref)

        half = A_GROUP * BLOCK
        rows = 2 * half
        qall = jnp.concatenate([qc_ref[...], qn_ref[...]], axis=0)
        oall = jnp.concatenate([oc_ref[...], on_ref[...]], axis=0)
        doall = jnp.concatenate([doc_ref[...], don_ref[...]], axis=0)
        lall = jnp.concatenate([lc_ref[0], ln_ref[0]], axis=0)
        row = lax.broadcasted_iota(jnp.int32, (rows, BLOCK), 0)
        r = row & (BLOCK - 1)
        c = lax.broadcasted_iota(jnp.int32, (rows, BLOCK), 1)
        diag_ok = (row < half) & (c <= r)
        next_ok = (row >= half) & (c > r)
        lane8 = lax.broadcasted_iota(jnp.int32, (8, LANES), 1)
        dsink = jnp.zeros((8, LANES), F32)
        off_prev = None
        for b in range(nb):
            blk = slice(b * BLOCK, (b + 1) * BLOCK)
            kv, vv = k_ref[blk, :], v_ref[blk, :]

            def stack(allv):
                return jnp.concatenate(
                    [allv[b * BLOCK:(b + 1) * BLOCK, g * LANES:(g + 1) * LANES] for g in range(A_GROUP)]
                    + [allv[(b + 1) * BLOCK:(b + 2) * BLOCK, g * LANES:(g + 1) * LANES] for g in range(A_GROUP)],
                    axis=0)

            qs, osk, dos = stack(qall), stack(oall), stack(doall)
            lse = jnp.concatenate([_lane_pick(lall[b * BLOCK:(b + 1) * BLOCK], g) for g in range(A_GROUP)]
                                  + [_lane_pick(lall[(b + 1) * BLOCK:(b + 2) * BLOCK], g) for g in range(A_GROUP)],
                                  axis=0)
            delta = jnp.sum(dos.astype(F32) * osk.astype(F32), axis=-1, keepdims=True)
            ok = (diag_ok | next_ok) if b < nb - 1 else (diag_ok | (next_ok & (n < nsteps - 1)))
            sc = jnp.where(ok, _dot_nt(qs, kv) * scale, NEG)
            p = jnp.exp(sc - lse)
            dv_ref[blk, :] = _dot_tn(p.astype(BF), dos)
            dp = _dot_nt(dos, vv)
            ds = (p * (dp - delta) * scale).astype(BF)
            dk_ref[blk, :] = _dot_tn(ds, qs)
            dqs = _dot(ds, kv)
            for g in range(A_GROUP):
                cur = slice(g * BLOCK, (g + 1) * BLOCK)
                before = carry[:, g * LANES:(g + 1) * LANES] if b == 0 else off_prev[cur]
                dq_ref[blk, g * LANES:(g + 1) * LANES] = before + dqs[cur]
                p_sink = jnp.exp(sink_ref[kvh * A_GROUP + g] - lse[cur])
                dsink = dsink + jnp.where(lane8 == g, -jnp.sum(p_sink * delta[cur]), 0.0)
            off_prev = dqs[half:]
        for g in range(A_GROUP):
            carry[:, g * LANES:(g + 1) * LANES] = off_prev[g * BLOCK:(g + 1) * BLOCK]
        dsink_ref[...] += dsink

    def nxt(n):
        return jnp.minimum((n + 1) * nb, nblk - 1)

    grp_c = pl.BlockSpec((qt, gw), lambda h, n: (n, h))
    grp_n = pl.BlockSpec((BLOCK, gw), lambda h, n: (nxt(n), h))
    kvb = pl.BlockSpec((qt, LANES), lambda h, n: (n, h))
    lse_c = pl.BlockSpec((1, qt, LANES), lambda h, n: (h, n, 0))
    lse_n = pl.BlockSpec((1, BLOCK, LANES), lambda h, n: (h, nxt(n), 0))
    dq, dk, dv, dsink = pl.pallas_call(
        body, grid=(A_KV_HEADS, nsteps),
        in_specs=[pl.BlockSpec(memory_space=pltpu.SMEM), grp_c, grp_n, kvb, kvb, grp_c, grp_n, grp_c, grp_n,
                  lse_c, lse_n],
        out_specs=[grp_c, kvb, kvb, pl.BlockSpec((8, LANES), lambda h, n: (h, 0))],
        out_shape=[jax.ShapeDtypeStruct(q.shape, F32), jax.ShapeDtypeStruct(k.shape, F32),
                   jax.ShapeDtypeStruct(v.shape, F32), jax.ShapeDtypeStruct((A_KV_HEADS * 8, LANES), F32)],
        scratch_shapes=[pltpu.VMEM((BLOCK, gw), F32)],
        compiler_params=_cp("parallel", "arbitrary"), name=name)(sinks, q, q, k, v, o, o, do, do, lse, lse)
    dsinks = dsink.reshape(A_KV_HEADS, 8, LANES)[:, 0, :A_GROUP].reshape(A_HEADS)
    return dq, dk, dv, dsinks


LOG2E = 1.4426950408889634


def _mla_fwd(q, k, v, *, name):
    s = q.shape[0]
    t = _tile(s, ATTN_TILE, LANES)
    nt = s // t
    scale = C_QK ** -0.5
    c2 = scale * LOG2E

    def body(q_ref, k_ref, v_ref, o_ref, lse_ref, m_sc, acc_sc):
        qi = pl.program_id(1)
        qv = q_ref[...]
        m_sc[...] = jnp.full_like(m_sc, NEG)
        acc_sc[...] = jnp.zeros_like(acc_sc)

        def step(start, width, diag):
            rows = pl.ds(pl.multiple_of(start, t), width)
            sc = _dot_nt(qv, k_ref[rows, :])
            if diag:
                r = lax.broadcasted_iota(jnp.int32, (t, t), 0)
                c = lax.broadcasted_iota(jnp.int32, (t, t), 1)
                own = jnp.where(c <= r, sc[:, width - t:], NEG)
                sc = own if width == t else jnp.concatenate([sc[:, :width - t], own], axis=1)
            m_prev = m_sc[...]
            m_new = jnp.maximum(m_prev, jnp.max(sc, axis=-1, keepdims=True))
            alpha = jnp.exp2((m_prev - m_new) * c2)
            p = jnp.exp2((sc - jnp.tile(m_new, (1, width // LANES))) * c2).astype(BF)
            lane = lax.broadcasted_iota(jnp.int32, (width, LANES), 1)
            vv = jnp.where(lane == C_V, jnp.ones((), BF), v_ref[rows, :])
            acc_sc[...] = alpha * acc_sc[...] + _dot(p, vv)
            m_sc[...] = m_new

        def wide_body(wi, carry):
            step(wi * (FWD_WIDE * t), FWD_WIDE * t, False)
            return carry

        def single_body(j, carry):
            step(j * t, t, False)
            return carry

        with_diag = qi >= FWD_WIDE - 1

        @pl.when(with_diag)
        def _():
            step((qi - (FWD_WIDE - 1)) * t, FWD_WIDE * t, True)

        @pl.when(jnp.logical_not(with_diag))
        def _():
            step(qi * t, t, True)

        rest = jnp.where(with_diag, qi - (FWD_WIDE - 1), qi)
        n_wide = rest // FWD_WIDE
        lax.fori_loop(0, n_wide, wide_body, 0)
        lax.fori_loop(n_wide * FWD_WIDE, rest, single_body, 0)
        acc = acc_sc[...]
        l = _lane_pick(acc, C_V)
        lane = lax.broadcasted_iota(jnp.int32, (t, LANES), 1)
        o_ref[...] = jnp.where(lane == C_V, 0.0, acc * (1.0 / l)).astype(BF)
        lse2 = m_sc[...] * c2 + jnp.log(l) * LOG2E
        lse_ref[0, 0] = jnp.transpose(lse2)[0:8, :]

    qspec = pl.BlockSpec((t, LANES), lambda h, i: (i, h))
    kspec = pl.BlockSpec((s, LANES), lambda h, i: (0, h))
    return pl.pallas_call(
        body, grid=(C_HEADS, nt), in_specs=[qspec, kspec, kspec],
        out_specs=[qspec, pl.BlockSpec((1, 1, 8, t), lambda h, i: (h, i, 0, 0))],
        out_shape=[jax.ShapeDtypeStruct(q.shape, BF), jax.ShapeDtypeStruct((C_HEADS, nt, 8, t), F32)],
        scratch_shapes=[pltpu.VMEM((t, LANES), F32)] * 2,
        compiler_params=_cp("parallel", "parallel"), name=name)(q, k, v)


def _mla_delta(o, do, *, name):
    s = o.shape[0]
    t = _tile(s, ATTN_TILE, LANES)
    nt = s // t

    def body(o_ref, do_ref, d_ref):
        for h in range(C_HEADS):
            prod = jnp.transpose(_slot(o_ref, h).astype(F32) * _slot(do_ref, h).astype(F32))
            d_ref[h, 0] = jnp.broadcast_to(jnp.sum(prod, axis=0, keepdims=True), (8, t))

    blk = pl.BlockSpec((t, C_HEADS * LANES), lambda i: (i, 0))
    return pl.pallas_call(
        body, grid=(nt,), in_specs=[blk, blk],
        out_specs=pl.BlockSpec((C_HEADS, 1, 8, t), lambda i: (0, i, 0, 0)),
        out_shape=jax.ShapeDtypeStruct((C_HEADS, nt, 8, t), F32),
        compiler_params=_cp("parallel"), name=name)(o, do)


def _mla_bwd(q, k, v, do, lse2, delta, *, name):
    s = q.shape[0]
    t = _tile(s, ATTN_TILE, LANES)
    nt = s // t
    scale = C_QK ** -0.5
    c2 = scale * LOG2E

    def body(q_ref, do_ref, k_ref, v_ref, lse_ref, dl_ref, dq_ref, dk_ref, dv_ref):
        kj = pl.program_id(1)

        @pl.when(kj == 0)
        def _():
            dq_ref[...] = jnp.zeros_like(dq_ref)

        dk_ref[...] = jnp.zeros_like(dk_ref)
        dv_ref[...] = jnp.zeros_like(dv_ref)
        kv, vv = k_ref[...], v_ref[...]

        def step(i, n, diag):
            rows = pl.ds(pl.multiple_of(i * t, t), n * t)
            qv, dov = q_ref[rows, :], do_ref[rows, :]
            st = _dot_nt(kv, qv)
            if diag:
                r = lax.broadcasted_iota(jnp.int32, (t, t), 0)
                c = lax.broadcasted_iota(jnp.int32, (t, t), 1)
                own = jnp.where(r <= c, st[:, :t], NEG)
                st = own if n == 1 else jnp.concatenate([own, st[:, t:]], axis=1)
            lse = jnp.concatenate([lse_ref[0, i + u, 0:1, :] for u in range(n)], axis=1)
            dl = jnp.concatenate([dl_ref[0, i + u, 0:1, :] for u in range(n)], axis=1)
            pt = jnp.exp2(st * c2 - lse)
            dv_ref[...] += _dot(pt.astype(BF), dov)
            dpt = _dot_nt(vv, dov)
            dst = (pt * (dpt - dl) * scale).astype(BF)
            dk_ref[...] += _dot(dst, qv)
            dq_ref[rows, :] += _dot_tn(dst, kv)

        odd = (nt - kj) % 2
        first = kj + 2 - odd

        @pl.when(odd == 0)
        def _():
            step(kj, 2, True)

        @pl.when(odd == 1)
        def _():
            step(kj, 1, True)

        def pair_body(pi, carry):
            step(first + 2 * pi, 2, False)
            return carry

        lax.fori_loop(0, (nt - first) // 2, pair_body, 0)

    res = pl.BlockSpec((s, LANES), lambda h, j: (0, h))
    kspec = pl.BlockSpec((t, LANES), lambda h, j: (j, h))
    stat = pl.BlockSpec((1, nt, 8, t), lambda h, j: (h, 0, 0, 0))
    return pl.pallas_call(
        body, grid=(C_HEADS, nt), in_specs=[res, res, kspec, kspec, stat, stat],
        out_specs=[res, kspec, kspec],
        out_shape=[jax.ShapeDtypeStruct(q.shape, F32)] * 3,
        compiler_params=_cp("parallel", "arbitrary"), name=name)(q, do, k, v, lse2, delta)


def _conv_shifted(z, zprev, tm):
    row = lax.broadcasted_iota(jnp.int32, z.shape, 0)
    z1 = jnp.where(row == 0, zprev[7:8], pltpu.roll(z, 1, 0))
    z2 = jnp.where(row == 0, zprev[6:7], jnp.where(row == 1, zprev[7:8], pltpu.roll(z, 2, 0)))
    return z1, z2


def _conv_fwd(proj, w, *, name):
    s = proj.shape[0]
    d = D_MODEL
    tm = _tile(s, PREP_TILE, 8)

    def body(x_ref, xp_ref, w_ref, y_ref):
        i = pl.program_id(0)
        z = x_ref[:, d:2 * d] * x_ref[:, 2 * d:]
        zprev = jnp.where(i > 0, xp_ref[:, d:2 * d] * xp_ref[:, 2 * d:], 0.0)
        z1, z2 = _conv_shifted(z, zprev, tm)
        y = w_ref[0:1, :] * z2 + w_ref[1:2, :] * z1 + w_ref[2:3, :] * z
        y_ref[...] = (x_ref[:, :d] * y).astype(BF)

    per8 = tm // 8
    return pl.pallas_call(
        body, grid=(s // tm,),
        in_specs=[pl.BlockSpec((tm, 3 * d), lambda i: (i, 0)),
                  pl.BlockSpec((8, 3 * d), lambda i: (jnp.maximum(i * per8 - 1, 0), 0)),
                  pl.BlockSpec((8, d), lambda i: (0, 0))],
        out_specs=pl.BlockSpec((tm, d), lambda i: (i, 0)),
        out_shape=jax.ShapeDtypeStruct((s, d), BF),
        compiler_params=_cp("parallel"), name=name)(proj, proj, w)


def _conv_bwd(proj, dyb, w, *, name):
    s = proj.shape[0]
    d = D_MODEL
    tm = _tile(s, PREP_TILE, 8)
    nt = s // tm

    def body(x_ref, xp_ref, xn_ref, dy_ref, dyn_ref, w_ref, dx_ref, dw_ref):
        i = pl.program_id(0)

        @pl.when(i == 0)
        def _():
            dw_ref[...] = jnp.zeros_like(dw_ref)

        b, c, u = x_ref[:, :d], x_ref[:, d:2 * d], x_ref[:, 2 * d:]
        z = c * u
        zprev = jnp.where(i > 0, xp_ref[:, d:2 * d] * xp_ref[:, 2 * d:], 0.0)
        z1, z2 = _conv_shifted(z, zprev, tm)
        w0, w1, w2 = w_ref[0:1, :], w_ref[1:2, :], w_ref[2:3, :]
        y = w0 * z2 + w1 * z1 + w2 * z
        dyb_v = dy_ref[...]
        dyc = dyb_v * b
        dyn = jnp.where(i < nt - 1, dyn_ref[...] * xn_ref[:, :d], 0.0)
        row = lax.broadcasted_iota(jnp.int32, dyc.shape, 0)
        d1 = jnp.where(row == tm - 1, dyn[0:1], pltpu.roll(dyc, tm - 1, 0))
        d2 = jnp.where(row == tm - 1, dyn[1:2], jnp.where(row == tm - 2, dyn[0:1], pltpu.roll(dyc, tm - 2, 0)))
        dz = w2 * dyc + w1 * d1 + w0 * d2
        dx_ref[:, :d] = (dyb_v * y).astype(BF)
        dx_ref[:, d:2 * d] = (dz * u).astype(BF)
        dx_ref[:, 2 * d:] = (dz * c).astype(BF)
        dw_ref[0:1, :] += jnp.sum(dyc * z2, axis=0, keepdims=True)
        dw_ref[1:2, :] += jnp.sum(dyc * z1, axis=0, keepdims=True)
        dw_ref[2:3, :] += jnp.sum(dyc * z, axis=0, keepdims=True)

    per8 = tm // 8
    last8 = s // 8 - 1
    dx, dw = pl.pallas_call(
        body, grid=(nt,),
        in_specs=[pl.BlockSpec((tm, 3 * d), lambda i: (i, 0)),
                  pl.BlockSpec((8, 3 * d), lambda i: (jnp.maximum(i * per8 - 1, 0), 0)),
                  pl.BlockSpec((8, 3 * d), lambda i: (jnp.minimum((i + 1) * per8, last8), 0)),
                  pl.BlockSpec((tm, d), lambda i: (i, 0)),
                  pl.BlockSpec((8, d), lambda i: (jnp.minimum((i + 1) * per8, last8), 0)),
                  pl.BlockSpec((8, d), lambda i: (0, 0))],
        out_specs=[pl.BlockSpec((tm, 3 * d), lambda i: (i, 0)), pl.BlockSpec((8, d), lambda i: (0, 0))],
        out_shape=[jax.ShapeDtypeStruct((s, 3 * d), BF), jax.ShapeDtypeStruct((8, d), F32)],
        compiler_params=_cp("arbitrary"), name=name)(proj, proj, proj, dyb, dyb, w)
    return dx, dw[0:3]


def _ffn_up(h, wgu, *, name):
    s, d = h.shape
    f = wgu.shape[1] // 2
    tm = _tile(s, ROW_TILE, 8)
    tn = _tile(f, 1408, 128)
    nb = f // tn

    def body(h_ref, wg_ref, wu_ref, g_ref, u_ref, a_ref):
        hv = h_ref[...]
        g = _dot(hv, wg_ref[...])
        u = _dot(hv, wu_ref[...])
        g_ref[...] = g.astype(BF)
        u_ref[...] = u.astype(BF)
        a_ref[...] = (g * jax.nn.sigmoid(g) * u).astype(BF)

    out = pl.BlockSpec((tm, tn), lambda j, i: (i, j))
    return pl.pallas_call(
        body, grid=(nb, s // tm),
        in_specs=[pl.BlockSpec((tm, d), lambda j, i: (i, 0)),
                  pl.BlockSpec((d, tn), lambda j, i: (0, j)),
                  pl.BlockSpec((d, tn), lambda j, i: (0, j + nb))],
        out_specs=[out] * 3, out_shape=[jax.ShapeDtypeStruct((s, f), BF)] * 3,
        compiler_params=_cp("parallel", "parallel"), name=name)(h, wgu, wgu)


def _ffn_bwd_mid(dy, wd, gate, up, *, name):
    s, d = dy.shape
    f = wd.shape[0]
    tm = _tile(s, ROW_TILE, 8)
    tn = _tile(f, 1408, 128)

    def body(dy_ref, wd_ref, g_ref, u_ref, dg_ref, du_ref):
        dact = _dot_nt(dy_ref[...].astype(BF), wd_ref[...])
        g = g_ref[...].astype(F32)
        u = u_ref[...].astype(F32)
        sig = jax.nn.sigmoid(g)
        dg_ref[...] = (dact * u * (sig * (1.0 + g * (1.0 - sig)))).astype(BF)
        du_ref[...] = (dact * (g * sig)).astype(BF)

    blk = pl.BlockSpec((tm, tn), lambda j, i: (i, j))
    return pl.pallas_call(
        body, grid=(f // tn, s // tm),
        in_specs=[pl.BlockSpec((tm, d), lambda j, i: (i, 0)), pl.BlockSpec((tn, d), lambda j, i: (j, 0)), blk, blk],
        out_specs=[blk, blk], out_shape=[jax.ShapeDtypeStruct((s, f), BF)] * 2,
        compiler_params=_cp("parallel", "parallel"), name=name)(dy, wd, gate, up)


def _pad_cols(w, heads, dim, layout):
    k = w.shape[0]
    w3 = w.reshape(k, heads, dim)
    pieces, lane = [], 0
    for lane0, dim0, cnt in sorted(layout):
        if lane0 > lane:
            pieces.append(jnp.zeros((k, heads, lane0 - lane), w.dtype))
        pieces.append(w3[:, :, dim0:dim0 + cnt])
        lane = lane0 + cnt
    if lane < LANES:
        pieces.append(jnp.zeros((k, heads, LANES - lane), w.dtype))
    return jnp.concatenate(pieces, axis=2).reshape(k, heads * LANES)


def _unpad_cols(w, heads, dim, layout):
    k = w.shape[0]
    w3 = w.reshape(k, heads, LANES)
    by_dim = sorted(layout, key=lambda seg: seg[1])
    assert sum(cnt for _, _, cnt in by_dim) == dim
    return jnp.concatenate([w3[:, :, lane0:lane0 + cnt] for lane0, _, cnt in by_dim], axis=2).reshape(k, heads * dim)


def _pad_rows(w, heads, dim):
    n = w.shape[1]
    return jnp.pad(w.reshape(heads, dim, n), ((0, 0), (0, LANES - dim), (0, 0))).reshape(heads * LANES, n)


def _unpad_rows(w, heads, dim):
    n = w.shape[1]
    return w.reshape(heads, LANES, n)[:, :dim, :].reshape(heads * dim, n)


def _pad_vec(g, layout):
    return _pad_cols(g.reshape(1, -1), 1, g.shape[0], layout)


def _unpad_vec(g, dim, layout):
    return _unpad_cols(g.reshape(1, LANES), 1, dim, layout)[0]


def _local_step(x, positions, w, target):
    s = x.shape[0]
    cos_a, sin_a, cos_c, sin_c = _rope_tables(positions.reshape(s, 1), name="rope_tables")
    grads = {n: [None] * w[n].shape[0] for n in WEIGHTS}
    saved = []

    for i in range(DEPTH):
        kind, j = i % N_MIXERS, i // N_MIXERS
        tag = f"l{i}"
        h = _rmsnorm_fwd(x, w['mix_norm'][i], name=f"{tag}_mix_norm")
        if kind == 0:
            nqk = (A_HEADS + A_KV_HEADS) * A_HEAD_DIM
            wqkv = jnp.concatenate(
                [_pad_cols(w['a_w_qkv'][j][:, :nqk], A_HEADS + A_KV_HEADS, A_HEAD_DIM, LAYOUT_A),
                 _pad_cols(w['a_w_qkv'][j][:, nqk:], A_KV_HEADS, A_HEAD_DIM, LAYOUT_V)], axis=1)
            wo = _pad_rows(w['a_w_o'][j], A_HEADS, A_HEAD_DIM)
            gq, gk = _pad_vec(w['a_q_norm'][j], LAYOUT_A), _pad_vec(w['a_k_norm'][j], LAYOUT_A)
            qkv = _mm([(h, wqkv, 0)], out_dtype=F32, name=f"{tag}_qkv")
            qa, ka, va = _prep_a_fwd(qkv, cos_a, sin_a, gq, gk, name=f"{tag}_prep")
            o, lse = _swa_fwd(qa, ka, va, w['a_sinks'][j], name=f"{tag}_attn")
            x1 = _mm([(o, wo, 0)], out_dtype=F32, res=x, name=f"{tag}_wo")
            mix = dict(wqkv=wqkv, wo=wo, gq=gq, gk=gk, qkv=qkv, qa=qa, ka=ka, va=va, o=o, lse=lse)
        elif kind == 1:
            cw = jnp.pad(w['b_conv_w'][j], ((0, 5), (0, 0)))
            proj = _mm([(h, w['b_w_in'][j], 0)], out_dtype=F32, name=f"{tag}_win")
            yb = _conv_fwd(proj, cw, name=f"{tag}_conv")
            x1 = _mm([(yb, w['b_w_out'][j], 0)], out_dtype=F32, res=x, name=f"{tag}_wout")
            mix = dict(cw=cw, proj=proj, yb=yb)
        else:
            wdn = w['c_w_down'][j]
            nqk = C_Q_RANK + C_KV_RANK
            wdn = jnp.concatenate([wdn[:, :nqk], _pad_cols(wdn[:, nqk:], 1, C_ROPE, LAYOUT_KR)], axis=1)
            wq = _pad_cols(w['c_w_q_up'][j], C_HEADS, C_QK, LAYOUT_C)
            wkv = w['c_w_kv_up'][j].reshape(C_KV_RANK, C_HEADS, C_NOPE + C_V)
            wkn = _pad_cols(wkv[:, :, :C_NOPE].reshape(C_KV_RANK, -1), C_HEADS, C_NOPE, LAYOUT_KN)
            wv = _pad_cols(wkv[:, :, C_NOPE:].reshape(C_KV_RANK, -1), C_HEADS, C_V, LAYOUT_V)
            wo = _pad_rows(w['c_w_o'][j], C_HEADS, C_V)
            gq, gk = _pad_vec(w['c_q_norm'][j], LAYOUT_C), _pad_vec(w['c_k_norm'][j], LAYOUT_C)
            gqa, gkva = w['c_q_a_norm'][j].reshape(1, -1), w['c_kv_a_norm'][j].reshape(1, -1)
            dn = _mm([(h, wdn, 0)], out_dtype=F32, name=f"{tag}_wdown")
            cqn, ckvn = _prep_c1_fwd(dn, gqa, gkva, name=f"{tag}_prep1")
            qraw = _mm([(cqn, wq, 0)], out_dtype=F32, name=f"{tag}_wq")
            knope = _mm([(ckvn, wkn, 0)], out_dtype=F32, name=f"{tag}_wkn")
            vc = _mm([(ckvn, wv, 0)], out_dtype=BF, name=f"{tag}_wv")
            qc, kc = _prep_c2_fwd(qraw, knope, dn, cos_c, sin_c, gq, gk, name=f"{tag}_prep2")
            o, lse = _mla_fwd(qc, kc, vc, name=f"{tag}_attn")
            x1 = _mm([(o, wo, 0)], out_dtype=F32, res=x, name=f"{tag}_wo")
            mix = dict(wdn=wdn, wq=wq, wkn=wkn, wv=wv, wo=wo, gq=gq, gk=gk, gqa=gqa, gkva=gkva, dn=dn, cqn=cqn,
                       ckvn=ckvn, qraw=qraw, knope=knope, vc=vc, qc=qc, kc=kc, o=o, lse=lse)
        h2 = _rmsnorm_fwd(x1, w['ffn_norm'][i], name=f"{tag}_ffn_norm")
        gate, up, act = _ffn_up(h2, w['f_w_gate_up'][i], name=f"{tag}_ffn_up")
        x2 = _mm([(act, w['f_w_down'][i], 0)], out_dtype=F32, res=x1, name=f"{tag}_ffn_down")
        saved.append(dict(x=x, h=h, x1=x1, h2=h2, gate=gate, up=up, act=act, mix=mix))
        x = x2

    loss_blk, dx = _loss_fwd_bwd(x, target, name="loss")

    for i in reversed(range(DEPTH)):
        kind, j = i % N_MIXERS, i // N_MIXERS
        tag = f"l{i}b"
        sv = saved[i]
        mix = sv['mix']
        wgu, wd = w['f_w_gate_up'][i], w['f_w_down'][i]
        grads['f_w_down'][i] = _mm_tn(sv['act'], dx, name=f"{tag}_dwd")
        dgate, dup = _ffn_bwd_mid(dx, wd, sv['gate'], sv['up'], name=f"{tag}_ffn_mid")
        grads['f_w_gate_up'][i] = (_mm_tn(sv['h2'], dgate, name=f"{tag}_dwg"),
                                   _mm_tn(sv['h2'], dup, name=f"{tag}_dwu"))
        dh2 = _mm([(dgate, wgu, 0), (dup, wgu, 1)], out_dtype=F32, trans_b=True, name=f"{tag}_dh2")
        dx, dg = _rmsnorm_bwd(sv['x1'], w['ffn_norm'][i], dh2, dx, name=f"{tag}_ffn_norm")
        grads['ffn_norm'][i] = dg[0]
        if kind == 0:
            grads['a_w_o'][j] = _unpad_rows(_mm_tn(mix['o'], dx, name=f"{tag}_dwo"), A_HEADS, A_HEAD_DIM)
            do = _mm([(dx, mix['wo'], 0)], out_dtype=BF, trans_b=True, name=f"{tag}_do")
            dqa, dka, dva, dsinks = _swa_bwd(mix['qa'], mix['ka'], mix['va'], mix['o'], do, mix['lse'],
                                             w['a_sinks'][j], name=f"{tag}_attn")
            dqkv, dgq, dgk = _prep_a_bwd(mix['qkv'], dqa, dka, dva, cos_a, sin_a, mix['gq'], mix['gk'],
                                         name=f"{tag}_prep")
            grads['a_sinks'][j] = dsinks
            grads['a_q_norm'][j] = _unpad_vec(dgq, A_HEAD_DIM, LAYOUT_A)
            grads['a_k_norm'][j] = _unpad_vec(dgk, A_HEAD_DIM, LAYOUT_A)
            dwqkv = _mm_tn(sv['h'], dqkv, name=f"{tag}_dwqkv")
            nqk = (A_HEADS + A_KV_HEADS) * LANES
            grads['a_w_qkv'][j] = jnp.concatenate(
                [_unpad_cols(dwqkv[:, :nqk], A_HEADS + A_KV_HEADS, A_HEAD_DIM, LAYOUT_A),
                 _unpad_cols(dwqkv[:, nqk:], A_KV_HEADS, A_HEAD_DIM, LAYOUT_V)], axis=1)
            dh = _mm([(dqkv, mix['wqkv'], 0)], out_dtype=F32, trans_b=True, name=f"{tag}_dh")
        elif kind == 1:
            grads['b_w_out'][j] = _mm_tn(mix['yb'], dx, name=f"{tag}_dwout")
            dyb = _mm([(dx, w['b_w_out'][j], 0)], out_dtype=F32, trans_b=True, name=f"{tag}_dyb")
            dproj, dcw = _conv_bwd(mix['proj'], dyb, mix['cw'], name=f"{tag}_conv")
            grads['b_conv_w'][j] = dcw
            grads['b_w_in'][j] = _mm_tn(sv['h'], dproj, name=f"{tag}_dwin")
            dh = _mm([(dproj, w['b_w_in'][j], 0)], out_dtype=F32, trans_b=True, name=f"{tag}_dh")
        else:
            grads['c_w_o'][j] = _unpad_rows(_mm_tn(mix['o'], dx, name=f"{tag}_dwo"), C_HEADS, C_V)
            do = _mm([(dx, mix['wo'], 0)], out_dtype=BF, trans_b=True, name=f"{tag}_do")
            delta = _mla_delta(mix['o'], do, name=f"{tag}_delta")
            dqc, dkc, dvc = _mla_bwd(mix['qc'], mix['kc'], mix['vc'], do, mix['lse'], delta, name=f"{tag}_attn")
            dqraw, dknope, dkr, dgq, dgk = _prep_c2_bwd(mix['qraw'], mix['knope'], mix['dn'], dqc, dkc, cos_c, sin_c,
                                                        mix['gq'], mix['gk'], name=f"{tag}_prep2")
            grads['c_q_norm'][j] = _unpad_vec(dgq, C_QK, LAYOUT_C)
            grads['c_k_norm'][j] = _unpad_vec(dgk, C_QK, LAYOUT_C)
            grads['c_w_q_up'][j] = _unpad_cols(_mm_tn(mix['cqn'], dqraw, name=f"{tag}_dwq"), C_HEADS, C_QK,
                                               LAYOUT_C)
            dwkn = _unpad_cols(_mm_tn(mix['ckvn'], dknope, name=f"{tag}_dwkn"), C_HEADS, C_NOPE, LAYOUT_KN)
            dwv = _unpad_cols(_mm_tn(mix['ckvn'], dvc, name=f"{tag}_dwv"), C_HEADS, C_V, LAYOUT_V)
            grads['c_w_kv_up'][j] = jnp.concatenate(
                [dwkn.reshape(C_KV_RANK, C_HEADS, C_NOPE), dwv.reshape(C_KV_RANK, C_HEADS, C_V)], axis=2).reshape(
                C_KV_RANK, -1)
            dcq = _mm([(dqraw, mix['wq'], 0)], out_dtype=F32, trans_b=True, name=f"{tag}_dcq")
            dckv = _mm([(dknope, mix['wkn'], 0), (dvc, mix['wv'], 0)], out_dtype=F32, trans_b=True,
                       name=f"{tag}_dckv")
            ddn, dgqa, dgkva = _prep_c1_bwd(mix['dn'], dcq, dckv, dkr, mix['gqa'], mix['gkva'], name=f"{tag}_prep1")
            grads['c_q_a_norm'][j] = dgqa[0]
            grads['c_kv_a_norm'][j] = dgkva[0]
            dwdn = _mm_tn(sv['h'], ddn, name=f"{tag}_dwdown")
            nqk = C_Q_RANK + C_KV_RANK
            grads['c_w_down'][j] = jnp.concatenate(
                [dwdn[:, :nqk], _unpad_cols(dwdn[:, nqk:], 1, C_ROPE, LAYOUT_KR)], axis=1)
            dh = _mm([(ddn, mix['wdn'], 0)], out_dtype=F32, trans_b=True, name=f"{tag}_dh")
        dx, dg = _rmsnorm_bwd(sv['x'], w['mix_norm'][i], dh, dx, name=f"{tag}_mix_norm")
        grads['mix_norm'][i] = dg[0]

    return loss_blk, dx, {n: (g if n in BIG else jnp.stack(g)) for n, g in grads.items()}


def _my_place():
    return lax.axis_index("x"), lax.axis_index("y"), lax.axis_index("c")


def _flips(x, y):
    return [(1 - x, y), (x, 1 - y), (1 - x, 1 - y)]


def _gather_weights(big, small):
    half = big.shape[0] // 2

    def body(big_ref, small_ref, bout_ref, sout_ref, send_b, recv_b, send_f, recv_f, send_s, recv_s, loc):
        x, y, c = _my_place()
        me = 2 * x + y
        mine = pl.ds(pl.multiple_of(c * half, 16), half)
        theirs = pl.ds(pl.multiple_of((1 - c) * half, 16), half)

        def ici(k, chip, to):
            return pltpu.make_async_remote_copy(src_ref=big_ref.at[mine, :], dst_ref=bout_ref.at[chip, mine, :],
                                                send_sem=send_b.at[k], recv_sem=recv_b.at[k],
                                                device_id=to, device_id_type=MESH)

        def forward(k, chip, rows):
            return pltpu.make_async_remote_copy(src_ref=bout_ref.at[chip, rows, :], dst_ref=bout_ref.at[chip, rows, :],
                                                send_sem=send_f.at[k], recv_sem=recv_f.at[k],
                                                device_id=(x, y, 1 - c), device_id_type=MESH)

        def small_copy(k, chip, to):
            return pltpu.make_async_remote_copy(src_ref=small_ref, dst_ref=sout_ref.at[chip],
                                                send_sem=send_s.at[k], recv_sem=recv_s.at[k],
                                                device_id=to, device_id_type=MESH)

        own = (pltpu.make_async_copy(big_ref, bout_ref.at[me], loc.at[0]),
               pltpu.make_async_copy(small_ref, sout_ref.at[me], loc.at[1]))
        for cp in own:
            cp.start()
        chips = _flips(x, y)
        sends = [ici(k, me, (px, py, c)) for k, (px, py) in enumerate(chips)]
        sends += [small_copy(k, me, (px, py, c)) for k, (px, py) in enumerate(chips)]
        for cp in sends:
            cp.start()
        passed = [forward(k, 2 * px + py, mine) for k, (px, py) in enumerate(chips)]
        for k, (px, py) in enumerate(chips):
            ici(k, 2 * px + py, (x, y, c)).wait_recv()
            passed[k].start()
        for k, (px, py) in enumerate(chips):
            forward(k, 2 * px + py, theirs).wait_recv()
            small_copy(k, 2 * px + py, (x, y, c)).wait_recv()
        for cp in sends + passed:
            cp.wait_send()
        for cp in own:
            cp.wait()

    hbm = pl.BlockSpec(memory_space=pltpu.HBM)
    sem3 = pltpu.SemaphoreType.DMA((3,))
    return pl.pallas_call(
        body, in_specs=[hbm, hbm], out_specs=[hbm, hbm],
        out_shape=[jax.ShapeDtypeStruct((N_CHIPS,) + big.shape, big.dtype),
                   jax.ShapeDtypeStruct((N_CHIPS,) + small.shape, small.dtype)],
        scratch_shapes=[sem3, sem3, sem3, sem3, sem3, sem3, pltpu.SemaphoreType.DMA((2,))],
        name="gather_weights")(big, small)


def _exchange_grads(gbig, gsmall):
    def body(gbig_ref, gsmall_ref, got_ref, ssum_ref, sbuf, send_b, recv_b, send_s, recv_s):
        x, y, c = _my_place()
        me = 4 * x + 2 * y + c

        def big_copy(k, src_chip, to):
            return pltpu.make_async_remote_copy(src_ref=gbig_ref.at[src_chip], dst_ref=got_ref.at[k],
                                                send_sem=send_b.at[k], recv_sem=recv_b.at[k],
                                                device_id=to, device_id_type=MESH)

        def small_copy(k, slot, to):
            return pltpu.make_async_remote_copy(src_ref=gsmall_ref, dst_ref=sbuf.at[slot],
                                                send_sem=send_s.at[k], recv_sem=recv_s.at[k],
                                                device_id=to, device_id_type=MESH)

        peers = [(x ^ (k >> 2), y ^ ((k >> 1) & 1), c ^ (k & 1)) for k in range(1, N_DEV)]
        bigs = [big_copy(k, 2 * px + py, (px, py, c)) for k, (px, py) in enumerate(_flips(x, y))]
        smalls = [small_copy(k, me, p) for k, p in enumerate(peers)]
        for cp in bigs + smalls:
            cp.start()
        sbuf[me] = gsmall_ref[...]
        for k, (px, py, pc) in enumerate(peers):
            small_copy(k, 4 * px + 2 * py + pc, (x, y, c)).wait_recv()
        acc = sbuf[0]
        for d in range(1, N_DEV):
            acc = acc + sbuf[d]
        ssum_ref[...] = acc
        for k in range(3):
            big_copy(k, 0, (x, y, c)).wait_recv()
        for cp in bigs + smalls:
            cp.wait_send()

    hbm = pl.BlockSpec(memory_space=pltpu.HBM)
    vmem = pl.BlockSpec(memory_space=pltpu.VMEM)
    return pl.pallas_call(
        body, in_specs=[hbm, vmem], out_specs=[hbm, vmem],
        out_shape=[jax.ShapeDtypeStruct((3,) + gbig.shape[1:], gbig.dtype),
                   jax.ShapeDtypeStruct(gsmall.shape, gsmall.dtype)],
        scratch_shapes=[pltpu.VMEM((N_DEV,) + gsmall.shape, gsmall.dtype),
                        pltpu.SemaphoreType.DMA((3,)), pltpu.SemaphoreType.DMA((3,)),
                        pltpu.SemaphoreType.DMA((N_DEV - 1,)), pltpu.SemaphoreType.DMA((N_DEV - 1,))],
        name="exchange_grads")(gbig, gsmall)


def _sum_shards(chip, mine, got):
    r, cols = mine.shape[1:]
    tr = _tile(r, PACK_TILE_ROWS, 16)

    def body(chip_ref, mine_ref, got_ref, o_ref):
        o_ref[...] = ((mine_ref[0] + got_ref[0].astype(F32)) + got_ref[1].astype(F32)) + got_ref[2].astype(F32)

    grid_spec = pltpu.PrefetchScalarGridSpec(
        num_scalar_prefetch=1, grid=(r // tr,),
        in_specs=[pl.BlockSpec((1, tr, cols), lambda i, chip_ref: (chip_ref[0], i, 0)),
                  pl.BlockSpec((3, tr, cols), lambda i, chip_ref: (0, i, 0))],
        out_specs=pl.BlockSpec((tr, cols), lambda i, chip_ref: (i, 0)))
    return pl.pallas_call(
        body, grid_spec=grid_spec, out_shape=jax.ShapeDtypeStruct((r, cols), F32),
        compiler_params=_cp("parallel"), name="sum_shards")(chip, mine, got)


def _swap_sibling(t, *, name, give_half=False):
    half = t.shape[1] // 2 if give_half else None
    out_shape = (t.shape[0], half) + t.shape[2:] if give_half else t.shape

    def body(t_ref, got_ref, send_sem, recv_sem):
        x, y, c = _my_place()
        src = t_ref.at[:, pl.ds(pl.multiple_of((1 - c) * half, 16), half), :] if give_half else t_ref
        cp = pltpu.make_async_remote_copy(src_ref=src, dst_ref=got_ref, send_sem=send_sem, recv_sem=recv_sem,
                                          device_id=(x, y, 1 - c), device_id_type=MESH)
        cp.start()
        cp.wait()

    hbm = pl.BlockSpec(memory_space=pltpu.HBM)
    return pl.pallas_call(
        body, in_specs=[hbm], out_specs=hbm, out_shape=jax.ShapeDtypeStruct(out_shape, t.dtype),
        scratch_shapes=[pltpu.SemaphoreType.DMA, pltpu.SemaphoreType.DMA], name=name)(t)


def _add_kept(core, gbig, got, *, name):
    n, r, cols = got.shape
    tr = _tile(r, PACK_TILE_ROWS, 16)
    nblk = r // tr

    def body(core_ref, a_ref, b_ref, o_ref, ob_ref):
        sm = a_ref[...] + b_ref[...]
        o_ref[...] = sm
        ob_ref[...] = sm.astype(BF)

    blk = pl.BlockSpec((n, tr, cols), lambda i, core_ref: (0, i, 0))
    grid_spec = pltpu.PrefetchScalarGridSpec(
        num_scalar_prefetch=1, grid=(nblk,),
        in_specs=[pl.BlockSpec((n, tr, cols), lambda i, core_ref: (0, core_ref[0] * nblk + i, 0)), blk],
        out_specs=[blk, blk])
    return pl.pallas_call(
        body, grid_spec=grid_spec,
        out_shape=[jax.ShapeDtypeStruct(got.shape, F32), jax.ShapeDtypeStruct(got.shape, BF)],
        compiler_params=_cp("parallel"), name=name)(core, gbig, got)


def _by_core(c, mine, sibling):
    return jnp.where(c == 0, mine, sibling), jnp.where(c == 0, sibling, mine)


def _adamw(wt, m, v, g, *, name):
    r, cols = wt.shape
    tr = _tile(r, 256, 8)

    def body(w_ref, m_ref, v_ref, g_ref, d_ref, nm_ref, nv_ref):
        gv = g_ref[...]
        nm = ADAM_B1 * m_ref[...] + (1.0 - ADAM_B1) * gv
        nv = ADAM_B2 * v_ref[...] + (1.0 - ADAM_B2) * (gv * gv)
        m_hat = nm / (1.0 - ADAM_B1 ** ADAM_STEP)
        v_hat = nv / (1.0 - ADAM_B2 ** ADAM_STEP)
        d_ref[...] = -ADAM_LR * (m_hat / (jnp.sqrt(v_hat) + ADAM_EPS) + ADAM_WD * w_ref[...])
        nm_ref[...] = nm
        nv_ref[...] = nv

    blk = pl.BlockSpec((tr, cols), lambda i: (i, 0))
    return pl.pallas_call(
        body, grid=(r // tr,), in_specs=[blk] * 4, out_specs=[blk] * 3,
        out_shape=[jax.ShapeDtypeStruct((r, cols), F32)] * 3,
        compiler_params=_cp("parallel"), name=name)(wt, m, v, g)


def _shard_shape(full, axis):
    return tuple(d // N_CHIPS if a == axis else d for a, d in enumerate(full))


def _pack_rows(n):
    rows = -(-n // PACK_COLS)
    return -(-rows // PACK_ROW_MULT) * PACK_ROW_MULT


def _pack(parts, rows, dtype):
    flat = jnp.concatenate([p.reshape(-1).astype(dtype) for p in parts])
    return jnp.pad(flat, (0, rows * PACK_COLS - flat.shape[0])).reshape(rows, PACK_COLS)


def _join_shards(stacked, axis):
    moved = jnp.moveaxis(stacked, 0, axis)
    shp = moved.shape
    return moved.reshape(shp[:axis] + (shp[axis] * shp[axis + 1],) + shp[axis + 2:])


def _split_shards(full, axis):
    shp = full.shape
    split = full.reshape(shp[:axis] + (N_CHIPS, shp[axis] // N_CHIPS) + shp[axis + 1:])
    return jnp.moveaxis(split, axis, 0)


SMALL_ROWS = 128
SMALL_LAYOUT = [('mix_norm', DEPTH * D_MODEL), ('ffn_norm', DEPTH * D_MODEL), ('b_conv_w', 3 * D_MODEL),
                ('c_q_a_norm', C_Q_RANK), ('c_kv_a_norm', C_KV_RANK), ('a_q_norm', 2 * A_HEAD_DIM),
                ('a_k_norm', 2 * A_HEAD_DIM), ('a_sinks', 2 * A_HEADS), ('c_q_norm', C_QK), ('c_k_norm', C_QK),
                ('loss', 1)]


def _small_offsets():
    offs, row = {}, 0
    for name, n in SMALL_LAYOUT:
        offs[name] = (row * LANES, n)
        row += -(-n // LANES)
    assert row <= SMALL_ROWS
    return offs


def kernel(x, positions, mix_norm, ffn_norm, a_w_qkv, a_q_norm, a_k_norm, a_sinks, a_w_o, b_w_in, b_conv_w, b_w_out, c_w_down, c_q_a_norm, c_kv_a_norm, c_w_q_up, c_w_kv_up, c_q_norm, c_k_norm, c_w_o, f_w_gate_up, f_w_down, loss_target, m_mix_norm, m_ffn_norm, m_a_w_qkv, m_a_q_norm, m_a_k_norm, m_a_sinks, m_a_w_o, m_b_w_in, m_b_conv_w, m_b_w_out, m_c_w_down, m_c_q_a_norm, m_c_kv_a_norm, m_c_w_q_up, m_c_w_kv_up, m_c_q_norm, m_c_k_norm, m_c_w_o, m_f_w_gate_up, m_f_w_down, v_mix_norm, v_ffn_norm, v_a_w_qkv, v_a_q_norm, v_a_k_norm, v_a_sinks, v_a_w_o, v_b_w_in, v_b_conv_w, v_b_w_out, v_c_w_down, v_c_q_a_norm, v_c_kv_a_norm, v_c_w_q_up, v_c_w_kv_up, v_c_q_norm, v_c_k_norm, v_c_w_o, v_f_w_gate_up, v_f_w_down):
    args = dict(locals())
    wshard = {n: args[n] for n in WEIGHTS}
    sharded = {**BIG, **SMALL_SHARDED}
    chip = 2 * lax.axis_index("x") + lax.axis_index("y")

    n_big = sum(wshard[n].size for n in BIG)
    rows = _pack_rows(n_big)
    big = _pack([wshard[n] for n in BIG], rows, BF)
    small = jnp.concatenate([wshard[n].reshape(-1) for n in SMALL_SHARDED])
    small = jnp.pad(small, (0, 8 * LANES - small.shape[0])).reshape(8, LANES)
    core = lax.axis_index("c")
    big_all, small_all = _gather_weights(big, small)
    big_all = big_all.reshape(N_CHIPS, -1)
    small_all = small_all.reshape(N_CHIPS, -1)
    w = {}
    off = 0
    for n, axis in BIG.items():
        sz = wshard[n].size
        w[n] = _join_shards(big_all[:, off:off + sz].reshape((N_CHIPS,) + wshard[n].shape), axis)
        off += sz
    off = 0
    for n, axis in SMALL_SHARDED.items():
        sz = wshard[n].size
        w[n] = _join_shards(small_all[:, off:off + sz].reshape((N_CHIPS,) + wshard[n].shape), axis)
        off += sz
    for n in WEIGHTS:
        if n not in sharded:
            w[n] = wshard[n]

    loss_blk, grad_x, g = _local_step(x[0], positions[0], w, loss_target[0])

    def shard_of(n, item, k):
        if isinstance(item, tuple):
            item = item[k // 2]
            k = k % 2
            width = item.shape[1] // 2
        else:
            width = item.shape[BIG[n] - 1] // N_CHIPS
        return item[:, k * width:(k + 1) * width] if BIG[n] == 2 else item[k * width:(k + 1) * width]

    gbig = jnp.stack([jnp.concatenate([shard_of(n, item, k).reshape(-1) for n in BIG for item in g[n]])
                      for k in range(N_CHIPS)])
    gbig = jnp.pad(gbig, ((0, 0), (0, rows * PACK_COLS - n_big))).reshape(N_CHIPS, rows, PACK_COLS)
    offs = _small_offsets()
    smalls = {**{n: g[n] for n, _ in SMALL_LAYOUT if n != 'loss'}, 'loss': loss_blk[0:1, 0:1]}
    gsmall = jnp.concatenate(
        [jnp.pad(smalls[n].reshape(-1), (0, -cnt % LANES)) for n, cnt in SMALL_LAYOUT])
    gsmall = jnp.pad(gsmall, (0, SMALL_ROWS * LANES - gsmall.shape[0])).reshape(SMALL_ROWS, LANES)
    from_sibling = _swap_sibling(gbig, name="swap_grads", give_half=True)
    chip_sum, chip_sum_bf = _add_kept(core.reshape(1).astype(jnp.int32), gbig, from_sibling, name="add_sibling")
    got, ssum = _exchange_grads(chip_sum_bf, gsmall)
    part = _sum_shards(chip.reshape(1).astype(jnp.int32), chip_sum, got)
    part_sib = _swap_sibling(part, name="swap_sums")
    gflat = jnp.concatenate(_by_core(core, part, part_sib), axis=0).reshape(-1)

    ssum = ssum.reshape(-1)
    outs = {}
    off = 0
    for n in WEIGHTS:
        wt = wshard[n]
        if n in BIG:
            grad = gflat[off:off + wt.size]
            off += wt.size
        else:
            o0, cnt = offs[n]
            grad = ssum[o0:o0 + cnt].reshape(g[n].shape)
            if n in SMALL_SHARDED:
                grad = lax.dynamic_index_in_dim(_split_shards(grad, SMALL_SHARDED[n]), chip, 0, keepdims=False)
        shape2 = (-1, wt.shape[-1])
        grad = grad.reshape(shape2)
        res = _adamw(wt.reshape(shape2), args['m_' + n].reshape(shape2), args['v_' + n].reshape(shape2), grad,
                     name=f"adamw_{n}")
        outs[n] = [r.reshape(wt.shape) for r in [grad] + list(res)]
    loss = ssum[offs['loss'][0]]
    return (loss, grad_x[None], *[outs[n][0] for n in WEIGHTS], *[outs[n][1] for n in WEIGHTS],
            *[outs[n][2] for n in WEIGHTS], *[outs[n][3] for n in WEIGHTS])
```

```python
import functools

import jax
import jax.numpy as jnp
from jax import lax
from jax.experimental import pallas as pl
from jax.experimental.pallas import tpu as pltpu

F32 = jnp.float32
BF = jnp.bfloat16

D_MODEL = 1024
DEPTH = 4
N_MIXERS = 3
ROPE_THETA = 500000.0
EPS = 1e-6
BLOCK = 128
LANES = 128

A_HEADS, A_KV_HEADS, A_HEAD_DIM = 16, 4, 64
A_ROT_DIM = A_HEAD_DIM // 4
A_GROUP = A_HEADS // A_KV_HEADS
C_HEADS, C_NOPE, C_ROPE, C_V, C_Q_RANK, C_KV_RANK = 16, 64, 32, 64, 384, 256
C_QK = C_NOPE + C_ROPE
D_FF = 2816

ADAM_LR, ADAM_B1, ADAM_B2, ADAM_EPS, ADAM_WD, ADAM_STEP = 0.001, 0.9, 0.999, 1e-08, 0.01, 10

N_CHIPS = 4
N_DEV = 8
MESH = pl.DeviceIdType.MESH

VMEM_LIMIT = 56 * 1024 * 1024
ROW_TILE = 512
PREP_TILE = 256
ATTN_TILE = 512
FWD_WIDE = 4
SWA_BLOCKS = 4
NEG = -1e30

WEIGHTS = ['mix_norm', 'ffn_norm', 'a_w_qkv', 'a_q_norm', 'a_k_norm', 'a_sinks', 'a_w_o', 'b_w_in', 'b_conv_w',
           'b_w_out', 'c_w_down', 'c_q_a_norm', 'c_kv_a_norm', 'c_w_q_up', 'c_w_kv_up', 'c_q_norm', 'c_k_norm',
           'c_w_o', 'f_w_gate_up', 'f_w_down']
BIG = {'a_w_qkv': 2, 'a_w_o': 1, 'b_w_in': 2, 'b_w_out': 1, 'c_w_down': 1, 'c_w_q_up': 2, 'c_w_kv_up': 2,
       'c_w_o': 1, 'f_w_gate_up': 2, 'f_w_down': 1}
SMALL_SHARDED = {'b_conv_w': 2, 'c_q_a_norm': 1, 'c_kv_a_norm': 1}
PACK_COLS = 256
PACK_ROW_MULT = 32
PACK_TILE_ROWS = 1024


def _cp(*sem):
    return pltpu.CompilerParams(dimension_semantics=sem, vmem_limit_bytes=VMEM_LIMIT)


def _tile(n, target, mult):
    if n <= target:
        return n
    t = (target // mult) * mult
    while t >= mult:
        if n % t == 0:
            return t
        t -= mult
    raise ValueError(f"no tile for {n}")


def _dot(a, b):
    return lax.dot_general(a, b, (((1,), (0,)), ((), ())), preferred_element_type=F32)


def _dot_nt(a, b):
    return lax.dot_general(a, b, (((1,), (1,)), ((), ())), preferred_element_type=F32)


def _dot_tn(a, b):
    return lax.dot_general(a, b, (((0,), (0,)), ((), ())), preferred_element_type=F32)


def _mm(pairs, *, out_dtype, name, res=None, trans_b=False, norm_g=None):
    a0, b0, _ = pairs[0]
    m = a0.shape[0]
    n = b0.shape[0] if trans_b else b0.shape[1]
    tm = _tile(m, ROW_TILE, 8)
    tn = _tile(n, 1024, 128)
    n_pairs = len(pairs)
    has_res = res is not None
    has_norm = norm_g is not None
    assert not has_norm or tn == n

    def body(*refs):
        acc = None
        for p in range(n_pairs):
            a = refs[2 * p][...].astype(BF)
            b = refs[2 * p + 1][...].astype(BF)
            d = _dot_nt(a, b) if trans_b else _dot(a, b)
            acc = d if acc is None else acc + d
        nxt = 2 * n_pairs
        if has_res:
            acc = acc + refs[nxt][...]
            nxt += 1
        if has_norm:
            y = acc * lax.rsqrt(jnp.mean(acc * acc, axis=-1, keepdims=True) + EPS)
            refs[-1][...] = (y * refs[nxt][...]).astype(BF)
            refs[-2][...] = acc.astype(out_dtype)
        else:
            refs[-1][...] = acc.astype(out_dtype)

    in_specs, args = [], []
    for a, b, kblk in pairs:
        k = a.shape[1]
        in_specs.append(pl.BlockSpec((tm, k), lambda j, i: (i, 0)))
        if trans_b:
            in_specs.append(pl.BlockSpec((tn, k), functools.partial(lambda j, i, kb: (j, kb), kb=kblk)))
        else:
            in_specs.append(pl.BlockSpec((k, tn), lambda j, i: (0, j)))
        args += [a, b]
    if has_res:
        in_specs.append(pl.BlockSpec((tm, tn), lambda j, i: (i, j)))
        args.append(res)
    out_spec = pl.BlockSpec((tm, tn), lambda j, i: (i, j))
    out_specs, out_shape = out_spec, jax.ShapeDtypeStruct((m, n), out_dtype)
    if has_norm:
        in_specs.append(pl.BlockSpec((1, n), lambda j, i: (0, 0)))
        args.append(norm_g.reshape(1, n))
        out_specs, out_shape = [out_spec, out_spec], [out_shape, jax.ShapeDtypeStruct((m, n), BF)]
    return pl.pallas_call(
        body, grid=(n // tn, m // tm), in_specs=in_specs, out_specs=out_specs, out_shape=out_shape,
        compiler_params=_cp("parallel", "parallel"), name=name)(*args)


def _mm_norm_bwd(pairs, x, g, dres, *, name):
    m, d = x.shape
    tm = _tile(m, ROW_TILE // 2, 8)
    n_pairs = len(pairs)

    def body(*refs):
        x_ref, g_ref, dres_ref, dx_ref, dg_ref = refs[2 * n_pairs:]

        @pl.when(pl.program_id(0) == 0)
        def _():
            dg_ref[...] = jnp.zeros_like(dg_ref)

        dh = None
        for p in range(n_pairs):
            dpart = _dot_nt(refs[2 * p][...].astype(BF), refs[2 * p + 1][...].astype(BF))
            dh = dpart if dh is None else dh + dpart
        dx, xhat = _rms_bwd_math(x_ref[...], g_ref[...], dh, d)
        dx_ref[...] = dres_ref[...] + dx
        dg_ref[0:1, :] += jnp.sum(dh * xhat, axis=0, keepdims=True)

    in_specs, args = [], []
    for a, b, kblk in pairs:
        k = a.shape[1]
        in_specs.append(pl.BlockSpec((tm, k), lambda i: (i, 0)))
        in_specs.append(pl.BlockSpec((d, k), functools.partial(lambda i, kb: (0, kb), kb=kblk)))
        args += [a, b]
    row = pl.BlockSpec((tm, d), lambda i: (i, 0))
    in_specs += [row, pl.BlockSpec((1, d), lambda i: (0, 0)), row]
    dx, dg = pl.pallas_call(
        body, grid=(m // tm,), in_specs=in_specs,
        out_specs=[row, pl.BlockSpec((8, d), lambda i: (0, 0))],
        out_shape=[jax.ShapeDtypeStruct((m, d), F32), jax.ShapeDtypeStruct((8, d), F32)],
        compiler_params=_cp("arbitrary"), name=name)(*args, x, g.reshape(1, d), dres)
    return dx, dg[0:1]


def _mm_tn(a, b, *, name):
    m, k = a.shape
    n = b.shape[1]
    tm = _tile(m, ROW_TILE, 8)
    tk = _tile(k, 1408, 128)
    tn = _tile(n, 1408, 128)

    def body(a_ref, b_ref, o_ref):
        @pl.when(pl.program_id(2) == 0)
        def _():
            o_ref[...] = jnp.zeros_like(o_ref)

        o_ref[...] += _dot_tn(a_ref[...].astype(BF), b_ref[...].astype(BF))

    return pl.pallas_call(
        body, grid=(k // tk, n // tn, m // tm),
        in_specs=[pl.BlockSpec((tm, tk), lambda kk, j, i: (i, kk)),
                  pl.BlockSpec((tm, tn), lambda kk, j, i: (i, j))],
        out_specs=pl.BlockSpec((tk, tn), lambda kk, j, i: (kk, j)),
        out_shape=jax.ShapeDtypeStruct((k, n), F32),
        compiler_params=_cp("parallel", "parallel", "arbitrary"), name=name)(a, b)


def _rmsnorm_fwd(x, g, *, name):
    s, d = x.shape
    tm = _tile(s, ROW_TILE, 8)

    def body(x_ref, g_ref, h_ref):
        xv = x_ref[...]
        y = xv * lax.rsqrt(jnp.mean(xv * xv, axis=-1, keepdims=True) + EPS)
        h_ref[...] = (y * g_ref[...]).astype(BF)

    return pl.pallas_call(
        body, grid=(s // tm,),
        in_specs=[pl.BlockSpec((tm, d), lambda i: (i, 0)), pl.BlockSpec((1, d), lambda i: (0, 0))],
        out_specs=pl.BlockSpec((tm, d), lambda i: (i, 0)),
        out_shape=jax.ShapeDtypeStruct((s, d), BF),
        compiler_params=_cp("parallel"), name=name)(x, g.reshape(1, d))


def _row_sum(v):
    return jnp.sum(v, axis=-1, keepdims=True)


def _slot_sum(v):
    ones = jnp.ones((LANES, LANES), BF)
    hi = v.astype(BF)
    lo = (v - hi.astype(F32)).astype(BF)
    return _dot(hi, ones) + _dot(lo, ones)


def _rms_bwd_math(xv, g, dh, n, row_sum=_row_sum):
    rstd = lax.rsqrt(row_sum(xv * xv) * (1.0 / n) + EPS)
    xhat = xv * rstd
    dxh = dh * g
    dx = rstd * (dxh - xhat * (row_sum(dxh * xhat) * (1.0 / n)))
    return dx, xhat


def _loss_fwd_bwd(y, target, *, name):
    s, d = y.shape
    tm = _tile(s, ROW_TILE, 8)

    def body(y_ref, t_ref, loss_ref, dy_ref):
        @pl.when(pl.program_id(0) == 0)
        def _():
            loss_ref[...] = jnp.zeros_like(loss_ref)

        err = y_ref[...] - t_ref[...]
        dy_ref[...] = err * (1.0 / d)
        loss_ref[...] += 0.5 * jnp.sum(jnp.sum(err * err, axis=-1, keepdims=True) * (1.0 / d))

    row = pl.BlockSpec((tm, d), lambda i: (i, 0))
    return pl.pallas_call(
        body, grid=(s // tm,), in_specs=[row, row],
        out_specs=[pl.BlockSpec((8, LANES), lambda i: (0, 0)), row],
        out_shape=[jax.ShapeDtypeStruct((8, LANES), F32), jax.ShapeDtypeStruct((s, d), F32)],
        compiler_params=_cp("arbitrary"), name=name)(y, target)


HALF = LANES // 2
A_HR, C_HR = A_ROT_DIM // 2, C_ROPE // 2
A_X1, C_X1 = 0, 32
LAYOUT_A = [(A_X1, 0, A_HR), (A_HR, A_ROT_DIM, A_HEAD_DIM - A_ROT_DIM), (HALF + A_X1, A_HR, A_HR)]
LAYOUT_C = [(0, 0, 32), (C_X1, C_NOPE, C_HR), (HALF, 32, 32), (HALF + C_X1, C_NOPE + C_HR, C_HR)]
LAYOUT_KN = [(0, 0, 32), (HALF, 32, 32)]
LAYOUT_KR = [(C_X1, 0, C_HR), (HALF + C_X1, C_HR, C_HR)]
LAYOUT_V = [(0, 0, C_V)]


def _rope_rows():
    lane = jnp.arange(LANES)
    fa = ROPE_THETA ** (-jnp.arange(0, A_ROT_DIM, 2, dtype=F32) / A_ROT_DIM)
    fc = ROPE_THETA ** (-jnp.arange(0, C_ROPE, 2, dtype=F32) / C_ROPE)

    def rows(f, x1, hr):
        first = (lane >= x1) & (lane < x1 + hr)
        second = (lane >= HALF + x1) & (lane < HALF + x1 + hr)
        freq = jnp.where(first | second, f[(lane - x1) % HALF % hr], 0.0)
        return freq, jnp.where(first, -1.0, jnp.where(second, 1.0, 0.0))

    freq_a, sign_a = rows(fa, A_X1, A_HR)
    freq_c, sign_c = rows(fc, C_X1, C_HR)
    return jnp.stack([freq_a, sign_a, freq_c, sign_c] + [jnp.zeros(LANES)] * 4).astype(F32)


def _rope_tables(pos_col, *, name):
    s = pos_col.shape[0]
    tm = _tile(s, ROW_TILE, 8)

    def body(pos_ref, rows_ref, ca_ref, sa_ref, cc_ref, sc_ref):
        p = pos_ref[...].astype(F32)
        ang_a = p * rows_ref[0:1, :]
        ang_c = p * rows_ref[2:3, :]
        ca_ref[...] = jnp.cos(ang_a)
        sa_ref[...] = jnp.sin(ang_a) * rows_ref[1:2, :]
        cc_ref[...] = jnp.cos(ang_c)
        sc_ref[...] = jnp.sin(ang_c) * rows_ref[3:4, :]

    tab = pl.BlockSpec((tm, LANES), lambda i: (i, 0))
    return pl.pallas_call(
        body, grid=(s // tm,),
        in_specs=[pl.BlockSpec((tm, 1), lambda i: (i, 0)), pl.BlockSpec((8, LANES), lambda i: (0, 0))],
        out_specs=[tab] * 4, out_shape=[jax.ShapeDtypeStruct((s, LANES), F32)] * 4,
        compiler_params=_cp("parallel"), name=name)(pos_col, _rope_rows())


def _rope(xv, cos, sin):
    return xv * cos + pltpu.roll(xv, HALF, 1) * sin


def _slot(ref, t):
    return ref[:, t * LANES:(t + 1) * LANES]


def _head_norm_rope(xv, g, cos, sin, n):
    y = xv * lax.rsqrt(_slot_sum(xv * xv) * (1.0 / n) + EPS) * g
    return _rope(y, cos, sin)


def _head_norm_rope_bwd(xv, g, dy, cos, sin, n):
    dyn = _rope(dy, cos, -sin)
    dx, xhat = _rms_bwd_math(xv, g, dyn, n, _slot_sum)
    return dx, jnp.sum(dyn * xhat, axis=0, keepdims=True)


def _prep_a_fwd(qkv, cos, sin, gq, gk, *, name):
    s = qkv.shape[0]
    tm = _tile(s, PREP_TILE, 8)
    nq, nkv = A_HEADS, A_KV_HEADS

    def body(x_ref, c_ref, s_ref, gq_ref, gk_ref, q_ref, k_ref, v_ref):
        cv, sv = c_ref[...], s_ref[...]
        for t in range(nq):
            q_ref[:, t * LANES:(t + 1) * LANES] = _head_norm_rope(
                _slot(x_ref, t), gq_ref[...], cv, sv, A_HEAD_DIM).astype(BF)
        for t in range(nkv):
            k_ref[:, t * LANES:(t + 1) * LANES] = _head_norm_rope(
                _slot(x_ref, nq + t), gk_ref[...], cv, sv, A_HEAD_DIM).astype(BF)
        v_ref[...] = x_ref[:, (nq + nkv) * LANES:].astype(BF)

    def rows(w):
        return pl.BlockSpec((tm, w), lambda i: (i, 0))

    vec = pl.BlockSpec((1, LANES), lambda i: (0, 0))
    return pl.pallas_call(
        body, grid=(s // tm,),
        in_specs=[rows(qkv.shape[1]), rows(LANES), rows(LANES), vec, vec],
        out_specs=[rows(nq * LANES), rows(nkv * LANES), rows(nkv * LANES)],
        out_shape=[jax.ShapeDtypeStruct((s, nq * LANES), BF), jax.ShapeDtypeStruct((s, nkv * LANES), BF),
                   jax.ShapeDtypeStruct((s, nkv * LANES), BF)],
        compiler_params=_cp("parallel"), name=name)(qkv, cos, sin, gq, gk)


def _prep_a_bwd(qkv, dq, dk, dv, cos, sin, gq, gk, *, name):
    s = qkv.shape[0]
    tm = _tile(s, PREP_TILE, 8)
    nq, nkv = A_HEADS, A_KV_HEADS

    def body(x_ref, dq_ref, dk_ref, dv_ref, c_ref, s_ref, gq_ref, gk_ref, dx_ref, dg_ref):
        @pl.when(pl.program_id(0) == 0)
        def _():
            dg_ref[...] = jnp.zeros_like(dg_ref)

        cv, sv = c_ref[...], s_ref[...]
        dgq = jnp.zeros((1, LANES), F32)
        dgk = jnp.zeros((1, LANES), F32)
        for t in range(nq):
            dx, dg = _head_norm_rope_bwd(_slot(x_ref, t), gq_ref[...], _slot(dq_ref, t), cv, sv,
                                         A_HEAD_DIM)
            dx_ref[:, t * LANES:(t + 1) * LANES] = dx.astype(BF)
            dgq = dgq + dg
        for t in range(nkv):
            dx, dg = _head_norm_rope_bwd(_slot(x_ref, nq + t), gk_ref[...], _slot(dk_ref, t), cv, sv,
                                         A_HEAD_DIM)
            dx_ref[:, (nq + t) * LANES:(nq + t + 1) * LANES] = dx.astype(BF)
            dgk = dgk + dg
        dx_ref[:, (nq + nkv) * LANES:] = dv_ref[...].astype(BF)
        dg_ref[0:1, :] += dgq
        dg_ref[1:2, :] += dgk

    def rows(w):
        return pl.BlockSpec((tm, w), lambda i: (i, 0))

    vec = pl.BlockSpec((1, LANES), lambda i: (0, 0))
    dx, dg = pl.pallas_call(
        body, grid=(s // tm,),
        in_specs=[rows(qkv.shape[1]), rows(nq * LANES), rows(nkv * LANES), rows(nkv * LANES), rows(LANES),
                  rows(LANES), vec, vec],
        out_specs=[rows(qkv.shape[1]), pl.BlockSpec((8, LANES), lambda i: (0, 0))],
        out_shape=[jax.ShapeDtypeStruct(qkv.shape, BF), jax.ShapeDtypeStruct((8, LANES), F32)],
        compiler_params=_cp("arbitrary"), name=name)(qkv, dq, dk, dv, cos, sin, gq, gk)
    return dx, dg[0:1], dg[1:2]


def _prep_c1_fwd(dn, gq, gkv, *, name):
    s = dn.shape[0]
    tm = _tile(s, ROW_TILE, 8)

    def body(x_ref, gq_ref, gkv_ref, cq_ref, ckv_ref):
        for lo, n, g_ref, o_ref in ((0, C_Q_RANK, gq_ref, cq_ref), (C_Q_RANK, C_KV_RANK, gkv_ref, ckv_ref)):
            xv = x_ref[:, lo:lo + n]
            y = xv * lax.rsqrt(jnp.mean(xv * xv, axis=-1, keepdims=True) + EPS)
            o_ref[...] = (y * g_ref[...]).astype(BF)

    def rows(w):
        return pl.BlockSpec((tm, w), lambda i: (i, 0))

    return pl.pallas_call(
        body, grid=(s // tm,),
        in_specs=[rows(dn.shape[1]), pl.BlockSpec((1, C_Q_RANK), lambda i: (0, 0)),
                  pl.BlockSpec((1, C_KV_RANK), lambda i: (0, 0))],
        out_specs=[rows(C_Q_RANK), rows(C_KV_RANK)],
        out_shape=[jax.ShapeDtypeStruct((s, C_Q_RANK), BF), jax.ShapeDtypeStruct((s, C_KV_RANK), BF)],
        compiler_params=_cp("parallel"), name=name)(dn, gq, gkv)


def _prep_c1_bwd(dn, dcq, dckv, dkr, gq, gkv, *, name):
    s = dn.shape[0]
    tm = _tile(s, ROW_TILE, 8)

    def body(x_ref, dcq_ref, dckv_ref, dkr_ref, gq_ref, gkv_ref, dx_ref, dgq_ref, dgkv_ref):
        @pl.when(pl.program_id(0) == 0)
        def _():
            dgq_ref[...] = jnp.zeros_like(dgq_ref)
            dgkv_ref[...] = jnp.zeros_like(dgkv_ref)

        for lo, n, g_ref, d_ref, dg_ref in ((0, C_Q_RANK, gq_ref, dcq_ref, dgq_ref),
                                            (C_Q_RANK, C_KV_RANK, gkv_ref, dckv_ref, dgkv_ref)):
            dv = d_ref[...]
            dx, xhat = _rms_bwd_math(x_ref[:, lo:lo + n], g_ref[...], dv, n)
            dx_ref[:, lo:lo + n] = dx.astype(BF)
            dg_ref[0:1, :] += jnp.sum(dv * xhat, axis=0, keepdims=True)
        dx_ref[:, C_Q_RANK + C_KV_RANK:] = dkr_ref[...].astype(BF)

    def rows(w):
        return pl.BlockSpec((tm, w), lambda i: (i, 0))

    def vec(w, r=1):
        return pl.BlockSpec((r, w), lambda i: (0, 0))

    dx, dgq, dgkv = pl.pallas_call(
        body, grid=(s // tm,),
        in_specs=[rows(dn.shape[1]), rows(C_Q_RANK), rows(C_KV_RANK), rows(LANES), vec(C_Q_RANK), vec(C_KV_RANK)],
        out_specs=[rows(dn.shape[1]), vec(C_Q_RANK, 8), vec(C_KV_RANK, 8)],
        out_shape=[jax.ShapeDtypeStruct(dn.shape, BF), jax.ShapeDtypeStruct((8, C_Q_RANK), F32),
                   jax.ShapeDtypeStruct((8, C_KV_RANK), F32)],
        compiler_params=_cp("arbitrary"), name=name)(dn, dcq, dckv, dkr, gq, gkv)
    return dx, dgq[0:1], dgkv[0:1]


def _prep_c2_fwd(qraw, knope, dn, cos, sin, gq, gk, *, name):
    s = qraw.shape[0]
    tm = _tile(s, PREP_TILE, 8)
    kr_blk = dn.shape[1] // LANES - 1

    def body(q_ref, kn_ref, kr_ref, c_ref, s_ref, gq_ref, gk_ref, qo_ref, ko_ref):
        cv, sv, kr = c_ref[...], s_ref[...], kr_ref[...]
        for t in range(C_HEADS):
            qo_ref[:, t * LANES:(t + 1) * LANES] = _head_norm_rope(
                _slot(q_ref, t), gq_ref[...], cv, sv, C_QK).astype(BF)
            ko_ref[:, t * LANES:(t + 1) * LANES] = _head_norm_rope(
                _slot(kn_ref, t) + kr, gk_ref[...], cv, sv, C_QK).astype(BF)

    def rows(w, blk=0):
        return pl.BlockSpec((tm, w), lambda i: (i, blk))

    vec = pl.BlockSpec((1, LANES), lambda i: (0, 0))
    w = C_HEADS * LANES
    return pl.pallas_call(
        body, grid=(s // tm,),
        in_specs=[rows(w), rows(w), rows(LANES, kr_blk), rows(LANES), rows(LANES), vec, vec],
        out_specs=[rows(w), rows(w)],
        out_shape=[jax.ShapeDtypeStruct((s, w), BF)] * 2,
        compiler_params=_cp("parallel"), name=name)(qraw, knope, dn, cos, sin, gq, gk)


def _prep_c2_bwd(qraw, knope, dn, dq, dk, cos, sin, gq, gk, *, name):
    s = qraw.shape[0]
    tm = _tile(s, PREP_TILE, 8)
    kr_blk = dn.shape[1] // LANES - 1

    def body(q_ref, kn_ref, kr_ref, dq_ref, dk_ref, c_ref, s_ref, gq_ref, gk_ref,
             dqo_ref, dkno_ref, dkr_ref, dg_ref):
        @pl.when(pl.program_id(0) == 0)
        def _():
            dg_ref[...] = jnp.zeros_like(dg_ref)

        cv, sv, kr = c_ref[...], s_ref[...], kr_ref[...]
        dgq = jnp.zeros((1, LANES), F32)
        dgk = jnp.zeros((1, LANES), F32)
        dkr = jnp.zeros((tm, LANES), F32)
        for t in range(C_HEADS):
            dx, dg = _head_norm_rope_bwd(_slot(q_ref, t), gq_ref[...], _slot(dq_ref, t), cv, sv,
                                         C_QK)
            dqo_ref[:, t * LANES:(t + 1) * LANES] = dx.astype(BF)
            dgq = dgq + dg
            dx, dg = _head_norm_rope_bwd(_slot(kn_ref, t) + kr, gk_ref[...], _slot(dk_ref, t), cv, sv,
                                         C_QK)
            dkno_ref[:, t * LANES:(t + 1) * LANES] = dx.astype(BF)
            dkr = dkr + dx
            dgk = dgk + dg
        dkr_ref[...] = dkr
        dg_ref[0:1, :] += dgq
        dg_ref[1:2, :] += dgk

    def rows(w, blk=0):
        return pl.BlockSpec((tm, w), lambda i: (i, blk))

    vec = pl.BlockSpec((1, LANES), lambda i: (0, 0))
    w = C_HEADS * LANES
    dqo, dkno, dkr, dg = pl.pallas_call(
        body, grid=(s // tm,),
        in_specs=[rows(w), rows(w), rows(LANES, kr_blk), rows(w), rows(w), rows(LANES), rows(LANES), vec, vec],
        out_specs=[rows(w), rows(w), rows(LANES), pl.BlockSpec((8, LANES), lambda i: (0, 0))],
        out_shape=[jax.ShapeDtypeStruct((s, w), BF), jax.ShapeDtypeStruct((s, w), BF),
                   jax.ShapeDtypeStruct((s, LANES), F32), jax.ShapeDtypeStruct((8, LANES), F32)],
        compiler_params=_cp("arbitrary"), name=name)(qraw, knope, dn, dq, dk, cos, sin, gq, gk)
    return dqo, dkno, dkr, dg[0:1], dg[1:2]


def _lane_pick(mat, g):
    lane = lax.broadcasted_iota(jnp.int32, mat.shape, 1)
    return jnp.sum(jnp.where(lane == g, mat, 0.0), axis=-1, keepdims=True)


def _swa_fwd(q, k, v, sinks, *, name):
    s = q.shape[0]
    nb = min(SWA_BLOCKS, s // BLOCK)
    qt = nb * BLOCK
    scale = A_HEAD_DIM ** -0.5
    gw = A_GROUP * LANES

    def body(sink_ref, q_ref, kc_ref, kp_ref, vc_ref, vp_ref, o_ref, lse_ref):
        kvh, n = pl.program_id(0), pl.program_id(1)
        rows = A_GROUP * BLOCK
        kall = jnp.concatenate([kp_ref[...], kc_ref[...]], axis=0)
        vall = jnp.concatenate([vp_ref[...], vc_ref[...]], axis=0)
        r = lax.broadcasted_iota(jnp.int32, (rows, 2 * BLOCK), 0) & (BLOCK - 1)
        c = lax.broadcasted_iota(jnp.int32, (rows, 2 * BLOCK), 1)
        cur_ok = (c >= BLOCK) & (c - BLOCK <= r)
        prev_ok = (c < BLOCK) & (c > r)
        head = lax.broadcasted_iota(jnp.int32, (rows, 1), 0) >> 7
        sink = jnp.zeros((rows, 1), F32)
        for g in range(A_GROUP):
            sink = jnp.where(head == g, sink_ref[kvh * A_GROUP + g], sink)
        lane = lax.broadcasted_iota(jnp.int32, (BLOCK, LANES), 1)
        for b in range(nb):
            blk = slice(b * BLOCK, (b + 1) * BLOCK)
            qs = jnp.concatenate([q_ref[blk, g * LANES:(g + 1) * LANES] for g in range(A_GROUP)], axis=0)
            k2 = kall[b * BLOCK:(b + 2) * BLOCK]
            v2 = vall[b * BLOCK:(b + 2) * BLOCK]
            ok = (cur_ok | prev_ok) if b > 0 else (cur_ok | (prev_ok & (n > 0)))
            sc = jnp.where(ok, _dot_nt(qs, k2) * scale, NEG)
            m = jnp.maximum(jnp.max(sc, axis=-1, keepdims=True), sink)
            p = jnp.exp(sc - m)
            l = jnp.sum(p, axis=-1, keepdims=True) + jnp.exp(sink - m)
            o = _dot((p * (1.0 / l)).astype(BF), v2)
            lse = m + jnp.log(l)
            lse_mat = jnp.zeros((BLOCK, LANES), F32)
            for g in range(A_GROUP):
                o_ref[blk, g * LANES:(g + 1) * LANES] = o[g * BLOCK:(g + 1) * BLOCK].astype(BF)
                lse_mat = jnp.where(lane == g, lse[g * BLOCK:(g + 1) * BLOCK], lse_mat)
            lse_ref[0, blk, :] = lse_mat

    cur = pl.BlockSpec((qt, LANES), lambda h, n: (n, h))
    prev = pl.BlockSpec((BLOCK, LANES), lambda h, n: (jnp.maximum(n * nb - 1, 0), h))
    return pl.pallas_call(
        body, grid=(A_KV_HEADS, s // qt),
        in_specs=[pl.BlockSpec(memory_space=pltpu.SMEM), pl.BlockSpec((qt, gw), lambda h, n: (n, h)),
                  cur, prev, cur, prev],
        out_specs=[pl.BlockSpec((qt, gw), lambda h, n: (n, h)),
                   pl.BlockSpec((1, qt, LANES), lambda h, n: (h, n, 0))],
        out_shape=[jax.ShapeDtypeStruct(q.shape, BF), jax.ShapeDtypeStruct((A_KV_HEADS, s, LANES), F32)],
        compiler_params=_cp("parallel", "parallel"), name=name)(sinks, q, k, k, v, v)


def _swa_bwd(q, k, v, o, do, lse, sinks, *, name):
    s = q.shape[0]
    nblk = s // BLOCK
    nb = min(SWA_BLOCKS, nblk)
    qt = nb * BLOCK
    nsteps = s // qt
    scale = A_HEAD_DIM ** -0.5
    gw = A_GROUP * LANES

    def body(sink_ref, qc_ref, qn_ref, k_ref, v_ref, oc_ref, on_ref, doc_ref, don_ref, lc_ref, ln_ref,
             dq_ref, dk_ref, dv_ref, dsink_ref, carry):
        kvh, n = pl.program_id(0), pl.program_id(1)

        @pl.when(n == 0)
        def _():
            carry[...] = jnp.zeros_like(carry)
            dsink_ref[...] = jnp.zeros_like(dsink_ref)

        half = A_GROUP * BLOCK
        rows = 2 * half
        qall = jnp.concatenate([qc_ref[...], qn_ref[...]], axis=0)
        oall = jnp.concatenate([oc_ref[...], on_ref[...]], axis=0)
        doall = jnp.concatenate([doc_ref[...], don_ref[...]], axis=0)
        lall = jnp.concatenate([lc_ref[0], ln_ref[0]], axis=0)
        row = lax.broadcasted_iota(jnp.int32, (rows, BLOCK), 0)
        r = row & (BLOCK - 1)
        c = lax.broadcasted_iota(jnp.int32, (rows, BLOCK), 1)
        diag_ok = (row < half) & (c <= r)
        next_ok = (row >= half) & (c > r)
        lane8 = lax.broadcasted_iota(jnp.int32, (8, LANES), 1)
        dsink = jnp.zeros((8, LANES), F32)
        off_prev = None
        for b in range(nb):
            blk = slice(b * BLOCK, (b + 1) * BLOCK)
            kv, vv = k_ref[blk, :], v_ref[blk, :]

            def stack(allv):
                return jnp.concatenate(
                    [allv[b * BLOCK:(b + 1) * BLOCK, g * LANES:(g + 1) * LANES] for g in range(A_GROUP)]
                    + [allv[(b + 1) * BLOCK:(b + 2) * BLOCK, g * LANES:(g + 1) * LANES] for g in range(A_GROUP)],
                    axis=0)

            qs, osk, dos = stack(qall), stack(oall), stack(doall)
            lse = jnp.concatenate([_lane_pick(lall[b * BLOCK:(b + 1) * BLOCK], g) for g in range(A_GROUP)]
                                  + [_lane_pick(lall[(b + 1) * BLOCK:(b + 2) * BLOCK], g) for g in range(A_GROUP)],
                                  axis=0)
            delta = jnp.sum(dos.astype(F32) * osk.astype(F32), axis=-1, keepdims=True)
            ok = (diag_ok | next_ok) if b < nb - 1 else (diag_ok | (next_ok & (n < nsteps - 1)))
            sc = jnp.where(ok, _dot_nt(qs, kv) * scale, NEG)
            p = jnp.exp(sc - lse)
            dv_ref[blk, :] = _dot_tn(p.astype(BF), dos)
            dp = _dot_nt(dos, vv)
            ds = (p * (dp - delta) * scale).astype(BF)
            dk_ref[blk, :] = _dot_tn(ds, qs)
            dqs = _dot(ds, kv)
            for g in range(A_GROUP):
                cur = slice(g * BLOCK, (g + 1) * BLOCK)
                before = carry[:, g * LANES:(g + 1) * LANES] if b == 0 else off_prev[cur]
                dq_ref[blk, g * LANES:(g + 1) * LANES] = before + dqs[cur]
                p_sink = jnp.exp(sink_ref[kvh * A_GROUP + g] - lse[cur])
                dsink = dsink + jnp.where(lane8 == g, -jnp.sum(p_sink * delta[cur]), 0.0)
            off_prev = dqs[half:]
        for g in range(A_GROUP):
            carry[:, g * LANES:(g + 1) * LANES] = off_prev[g * BLOCK:(g + 1) * BLOCK]
        dsink_ref[...] += dsink

    def nxt(n):
        return jnp.minimum((n + 1) * nb, nblk - 1)

    grp_c = pl.BlockSpec((qt, gw), lambda h, n: (n, h))
    grp_n = pl.BlockSpec((BLOCK, gw), lambda h, n: (nxt(n), h))
    kvb = pl.BlockSpec((qt, LANES), lambda h, n: (n, h))
    lse_c = pl.BlockSpec((1, qt, LANES), lambda h, n: (h, n, 0))
    lse_n = pl.BlockSpec((1, BLOCK, LANES), lambda h, n: (h, nxt(n), 0))
    dq, dk, dv, dsink = pl.pallas_call(
        body, grid=(A_KV_HEADS, nsteps),
        in_specs=[pl.BlockSpec(memory_space=pltpu.SMEM), grp_c, grp_n, kvb, kvb, grp_c, grp_n, grp_c, grp_n,
                  lse_c, lse_n],
        out_specs=[grp_c, kvb, kvb, pl.BlockSpec((8, LANES), lambda h, n: (h, 0))],
        out_shape=[jax.ShapeDtypeStruct(q.shape, F32), jax.ShapeDtypeStruct(k.shape, F32),
                   jax.ShapeDtypeStruct(v.shape, F32), jax.ShapeDtypeStruct((A_KV_HEADS * 8, LANES), F32)],
        scratch_shapes=[pltpu.VMEM((BLOCK, gw), F32)],
        compiler_params=_cp("parallel", "arbitrary"), name=name)(sinks, q, q, k, v, o, o, do, do, lse, lse)
    dsinks = dsink.reshape(A_KV_HEADS, 8, LANES)[:, 0, :A_GROUP].reshape(A_HEADS)
    return dq, dk, dv, dsinks


LOG2E = 1.4426950408889634


def _mla_fwd(q, k, v, *, name):
    s = q.shape[0]
    t = _tile(s, ATTN_TILE, LANES)
    nt = s // t
    scale = C_QK ** -0.5
    c2 = scale * LOG2E

    def body(q_ref, k_ref, v_ref, o_ref, lse_ref, m_sc, acc_sc):
        qi = pl.program_id(1)
        qv = q_ref[...]
        m_sc[...] = jnp.full_like(m_sc, NEG)
        acc_sc[...] = jnp.zeros_like(acc_sc)

        def step(start, width, diag):
            rows = pl.ds(pl.multiple_of(start, t), width)
            sc = _dot_nt(qv, k_ref[rows, :])
            if diag:
                r = lax.broadcasted_iota(jnp.int32, (t, t), 0)
                c = lax.broadcasted_iota(jnp.int32, (t, t), 1)
                own = jnp.where(c <= r, sc[:, width - t:], NEG)
                sc = own if width == t else jnp.concatenate([sc[:, :width - t], own], axis=1)
            m_prev = m_sc[...]
            m_new = jnp.maximum(m_prev, jnp.max(sc, axis=-1, keepdims=True))
            alpha = jnp.exp2((m_prev - m_new) * c2)
            p = jnp.exp2((sc - jnp.tile(m_new, (1, width // LANES))) * c2).astype(BF)
            lane = lax.broadcasted_iota(jnp.int32, (width, LANES), 1)
            vv = jnp.where(lane == C_V, jnp.ones((), BF), v_ref[rows, :])
            acc_sc[...] = alpha * acc_sc[...] + _dot(p, vv)
            m_sc[...] = m_new

        def wide_body(wi, carry):
            step(wi * (FWD_WIDE * t), FWD_WIDE * t, False)
            return carry

        def single_body(j, carry):
            step(j * t, t, False)
            return carry

        with_diag = qi >= FWD_WIDE - 1

        @pl.when(with_diag)
        def _():
            step((qi - (FWD_WIDE - 1)) * t, FWD_WIDE * t, True)

        @pl.when(jnp.logical_not(with_diag))
        def _():
            step(qi * t, t, True)

        rest = jnp.where(with_diag, qi - (FWD_WIDE - 1), qi)
        n_wide = rest // FWD_WIDE
        lax.fori_loop(0, n_wide, wide_body, 0)
        lax.fori_loop(n_wide * FWD_WIDE, rest, single_body, 0)
        acc = acc_sc[...]
        l = _lane_pick(acc, C_V)
        lane = lax.broadcasted_iota(jnp.int32, (t, LANES), 1)
        o_ref[...] = jnp.where(lane == C_V, 0.0, acc * (1.0 / l)).astype(BF)
        lse2 = m_sc[...] * c2 + jnp.log(l) * LOG2E
        lse_ref[0, 0] = jnp.transpose(lse2)[0:8, :]

    qspec = pl.BlockSpec((t, LANES), lambda h, i: (i, h))
    kspec = pl.BlockSpec((s, LANES), lambda h, i: (0, h))
    return pl.pallas_call(
        body, grid=(C_HEADS, nt), in_specs=[qspec, kspec, kspec],
        out_specs=[qspec, pl.BlockSpec((1, 1, 8, t), lambda h, i: (h, i, 0, 0))],
        out_shape=[jax.ShapeDtypeStruct(q.shape, BF), jax.ShapeDtypeStruct((C_HEADS, nt, 8, t), F32)],
        scratch_shapes=[pltpu.VMEM((t, LANES), F32)] * 2,
        compiler_params=_cp("parallel", "parallel"), name=name)(q, k, v)


def _mla_delta(o, do, *, name):
    s = o.shape[0]
    t = _tile(s, ATTN_TILE, LANES)
    nt = s // t

    def body(o_ref, do_ref, d_ref):
        for h in range(C_HEADS):
            prod = jnp.transpose(_slot(o_ref, h).astype(F32) * _slot(do_ref, h).astype(F32))
            d_ref[h, 0] = jnp.broadcast_to(jnp.sum(prod, axis=0, keepdims=True), (8, t))

    blk = pl.BlockSpec((t, C_HEADS * LANES), lambda i: (i, 0))
    return pl.pallas_call(
        body, grid=(nt,), in_specs=[blk, blk],
        out_specs=pl.BlockSpec((C_HEADS, 1, 8, t), lambda i: (0, i, 0, 0)),
        out_shape=jax.ShapeDtypeStruct((C_HEADS, nt, 8, t), F32),
        compiler_params=_cp("parallel"), name=name)(o, do)


def _mla_bwd(q, k, v, do, lse2, delta, *, name):
    s = q.shape[0]
    t = _tile(s, ATTN_TILE, LANES)
    nt = s // t
    scale = C_QK ** -0.5
    c2 = scale * LOG2E

    def body(q_ref, do_ref, k_ref, v_ref, lse_ref, dl_ref, dq_ref, dk_ref, dv_ref):
        kj = pl.program_id(1)

        @pl.when(kj == 0)
        def _():
            dq_ref[...] = jnp.zeros_like(dq_ref)

        dk_ref[...] = jnp.zeros_like(dk_ref)
        dv_ref[...] = jnp.zeros_like(dv_ref)
        kv, vv = k_ref[...], v_ref[...]

        def step(i, n, diag):
            rows = pl.ds(pl.multiple_of(i * t, t), n * t)
            qv, dov = q_ref[rows, :], do_ref[rows, :]
            st = _dot_nt(kv, qv)
            if diag:
                r = lax.broadcasted_iota(jnp.int32, (t, t), 0)
                c = lax.broadcasted_iota(jnp.int32, (t, t), 1)
                own = jnp.where(r <= c, st[:, :t], NEG)
                st = own if n == 1 else jnp.concatenate([own, st[:, t:]], axis=1)
            lse = jnp.concatenate([lse_ref[0, i + u, 0:1, :] for u in range(n)], axis=1)
            dl = jnp.concatenate([dl_ref[0, i + u, 0:1, :] for u in range(n)], axis=1)
            pt = jnp.exp2(st * c2 - lse)
            dv_ref[...] += _dot(pt.astype(BF), dov)
            dpt = _dot_nt(vv, dov)
            dst = (pt * (dpt - dl) * scale).astype(BF)
            dk_ref[...] += _dot(dst, qv)
            dq_ref[rows, :] += _dot_tn(dst, kv)

        odd = (nt - kj) % 2
        first = kj + 2 - odd

        @pl.when(odd == 0)
        def _():
            step(kj, 2, True)

        @pl.when(odd == 1)
        def _():
            step(kj, 1, True)

        def pair_body(pi, carry):
            step(first + 2 * pi, 2, False)
            return carry

        lax.fori_loop(0, (nt - first) // 2, pair_body, 0)

    res = pl.BlockSpec((s, LANES), lambda h, j: (0, h))
    kspec = pl.BlockSpec((t, LANES), lambda h, j: (j, h))
    stat = pl.BlockSpec((1, nt, 8, t), lambda h, j: (h, 0, 0, 0))
    return pl.pallas_call(
        body, grid=(C_HEADS, nt), in_specs=[res, res, kspec, kspec, stat, stat],
        out_specs=[res, kspec, kspec],
        out_shape=[jax.ShapeDtypeStruct(q.shape, F32)] * 3,
        compiler_params=_cp("parallel", "arbitrary"), name=name)(q, do, k, v, lse2, delta)


def _conv_shifted(z, zprev, tm):
    row = lax.broadcasted_iota(jnp.int32, z.shape, 0)
    z1 = jnp.where(row == 0, zprev[7:8], pltpu.roll(z, 1, 0))
    z2 = jnp.where(row == 0, zprev[6:7], jnp.where(row == 1, zprev[7:8], pltpu.roll(z, 2, 0)))
    return z1, z2


def _conv_fwd(proj, w, *, name):
    s = proj.shape[0]
    d = D_MODEL
    tm = _tile(s, PREP_TILE, 8)

    def body(x_ref, xp_ref, w_ref, y_ref):
        i = pl.program_id(0)
        z = x_ref[:, d:2 * d] * x_ref[:, 2 * d:]
        zprev = jnp.where(i > 0, xp_ref[:, d:2 * d] * xp_ref[:, 2 * d:], 0.0)
        z1, z2 = _conv_shifted(z, zprev, tm)
        y = w_ref[0:1, :] * z2 + w_ref[1:2, :] * z1 + w_ref[2:3, :] * z
        y_ref[...] = (x_ref[:, :d] * y).astype(BF)

    per8 = tm // 8
    return pl.pallas_call(
        body, grid=(s // tm,),
        in_specs=[pl.BlockSpec((tm, 3 * d), lambda i: (i, 0)),
                  pl.BlockSpec((8, 3 * d), lambda i: (jnp.maximum(i * per8 - 1, 0), 0)),
                  pl.BlockSpec((8, d), lambda i: (0, 0))],
        out_specs=pl.BlockSpec((tm, d), lambda i: (i, 0)),
        out_shape=jax.ShapeDtypeStruct((s, d), BF),
        compiler_params=_cp("parallel"), name=name)(proj, proj, w)


def _conv_bwd(proj, dyb, w, *, name):
    s = proj.shape[0]
    d = D_MODEL
    tm = _tile(s, PREP_TILE, 8)
    nt = s // tm

    def body(x_ref, xp_ref, xn_ref, dy_ref, dyn_ref, w_ref, dx_ref, dw_ref):
        i = pl.program_id(0)

        @pl.when(i == 0)
        def _():
            dw_ref[...] = jnp.zeros_like(dw_ref)

        b, c, u = x_ref[:, :d], x_ref[:, d:2 * d], x_ref[:, 2 * d:]
        z = c * u
        zprev = jnp.where(i > 0, xp_ref[:, d:2 * d] * xp_ref[:, 2 * d:], 0.0)
        z1, z2 = _conv_shifted(z, zprev, tm)
        w0, w1, w2 = w_ref[0:1, :], w_ref[1:2, :], w_ref[2:3, :]
        y = w0 * z2 + w1 * z1 + w2 * z
        dyb_v = dy_ref[...]
        dyc = dyb_v * b
        dyn = jnp.where(i < nt - 1, dyn_ref[...] * xn_ref[:, :d], 0.0)
        row = lax.broadcasted_iota(jnp.int32, dyc.shape, 0)
        d1 = jnp.where(row == tm - 1, dyn[0:1], pltpu.roll(dyc, tm - 1, 0))
        d2 = jnp.where(row == tm - 1, dyn[1:2], jnp.where(row == tm - 2, dyn[0:1], pltpu.roll(dyc, tm - 2, 0)))
        dz = w2 * dyc + w1 * d1 + w0 * d2
        dx_ref[:, :d] = (dyb_v * y).astype(BF)
        dx_ref[:, d:2 * d] = (dz * u).astype(BF)
        dx_ref[:, 2 * d:] = (dz * c).astype(BF)
        dw_ref[0:1, :] += jnp.sum(dyc * z2, axis=0, keepdims=True)
        dw_ref[1:2, :] += jnp.sum(dyc * z1, axis=0, keepdims=True)
        dw_ref[2:3, :] += jnp.sum(dyc * z, axis=0, keepdims=True)

    per8 = tm // 8
    last8 = s // 8 - 1
    dx, dw = pl.pallas_call(
        body, grid=(nt,),
        in_specs=[pl.BlockSpec((tm, 3 * d), lambda i: (i, 0)),
                  pl.BlockSpec((8, 3 * d), lambda i: (jnp.maximum(i * per8 - 1, 0), 0)),
                  pl.BlockSpec((8, 3 * d), lambda i: (jnp.minimum((i + 1) * per8, last8), 0)),
                  pl.BlockSpec((tm, d), lambda i: (i, 0)),
                  pl.BlockSpec((8, d), lambda i: (jnp.minimum((i + 1) * per8, last8), 0)),
                  pl.BlockSpec((8, d), lambda i: (0, 0))],
        out_specs=[pl.BlockSpec((tm, 3 * d), lambda i: (i, 0)), pl.BlockSpec((8, d), lambda i: (0, 0))],
        out_shape=[jax.ShapeDtypeStruct((s, 3 * d), BF), jax.ShapeDtypeStruct((8, d), F32)],
        compiler_params=_cp("arbitrary"), name=name)(proj, proj, proj, dyb, dyb, w)
    return dx, dw[0:3]


def _ffn_up(h, wgu, *, name):
    s, d = h.shape
    f = wgu.shape[1] // 2
    tm = _tile(s, ROW_TILE, 8)
    tn = _tile(f, 1408, 128)
    nb = f // tn

    def body(h_ref, wg_ref, wu_ref, g_ref, u_ref, a_ref):
        hv = h_ref[...]
        g = _dot(hv, wg_ref[...])
        u = _dot(hv, wu_ref[...])
        g_ref[...] = g.astype(BF)
        u_ref[...] = u.astype(BF)
        a_ref[...] = (g * jax.nn.sigmoid(g) * u).astype(BF)

    out = pl.BlockSpec((tm, tn), lambda j, i: (i, j))
    return pl.pallas_call(
        body, grid=(nb, s // tm),
        in_specs=[pl.BlockSpec((tm, d), lambda j, i: (i, 0)),
                  pl.BlockSpec((d, tn), lambda j, i: (0, j)),
                  pl.BlockSpec((d, tn), lambda j, i: (0, j + nb))],
        out_specs=[out] * 3, out_shape=[jax.ShapeDtypeStruct((s, f), BF)] * 3,
        compiler_params=_cp("parallel", "parallel"), name=name)(h, wgu, wgu)


def _ffn_bwd_mid(dy, wd, gate, up, *, name):
    s, d = dy.shape
    f = wd.shape[0]
    tm = _tile(s, ROW_TILE, 8)
    tn = _tile(f, 1408, 128)

    def body(dy_ref, wd_ref, g_ref, u_ref, dg_ref, du_ref):
        dact = _dot_nt(dy_ref[...].astype(BF), wd_ref[...])
        g = g_ref[...].astype(F32)
        u = u_ref[...].astype(F32)
        sig = jax.nn.sigmoid(g)
        dg_ref[...] = (dact * u * (sig * (1.0 + g * (1.0 - sig)))).astype(BF)
        du_ref[...] = (dact * (g * sig)).astype(BF)

    blk = pl.BlockSpec((tm, tn), lambda j, i: (i, j))
    return pl.pallas_call(
        body, grid=(f // tn, s // tm),
        in_specs=[pl.BlockSpec((tm, d), lambda j, i: (i, 0)), pl.BlockSpec((tn, d), lambda j, i: (j, 0)), blk, blk],
        out_specs=[blk, blk], out_shape=[jax.ShapeDtypeStruct((s, f), BF)] * 2,
        compiler_params=_cp("parallel", "parallel"), name=name)(dy, wd, gate, up)


def _pad_cols(w, heads, dim, layout):
    k = w.shape[0]
    w3 = w.reshape(k, heads, dim)
    pieces, lane = [], 0
    for lane0, dim0, cnt in sorted(layout):
        if lane0 > lane:
            pieces.append(jnp.zeros((k, heads, lane0 - lane), w.dtype))
        pieces.append(w3[:, :, dim0:dim0 + cnt])
        lane = lane0 + cnt
    if lane < LANES:
        pieces.append(jnp.zeros((k, heads, LANES - lane), w.dtype))
    return jnp.concatenate(pieces, axis=2).reshape(k, heads * LANES)


def _unpad_cols(w, heads, dim, layout):
    k = w.shape[0]
    w3 = w.reshape(k, heads, LANES)
    by_dim = sorted(layout, key=lambda seg: seg[1])
    assert sum(cnt for _, _, cnt in by_dim) == dim
    return jnp.concatenate([w3[:, :, lane0:lane0 + cnt] for lane0, _, cnt in by_dim], axis=2).reshape(k, heads * dim)


def _pad_rows(w, heads, dim):
    n = w.shape[1]
    return jnp.pad(w.reshape(heads, dim, n), ((0, 0), (0, LANES - dim), (0, 0))).reshape(heads * LANES, n)


def _unpad_rows(w, heads, dim):
    n = w.shape[1]
    return w.reshape(heads, LANES, n)[:, :dim, :].reshape(heads * dim, n)


def _pad_vec(g, layout):
    return _pad_cols(g.reshape(1, -1), 1, g.shape[0], layout)


def _unpad_vec(g, dim, layout):
    return _unpad_cols(g.reshape(1, LANES), 1, dim, layout)[0]


def _local_step(x, positions, w, target):
    s = x.shape[0]
    cos_a, sin_a, cos_c, sin_c = _rope_tables(positions.reshape(s, 1), name="rope_tables")
    grads = {n: [None] * len(w[n]) for n in WEIGHTS}
    saved = []

    for i in range(DEPTH):
        kind, j = i % N_MIXERS, i // N_MIXERS
        tag = f"l{i}"
        if i == 0:
            h = _rmsnorm_fwd(x, w['mix_norm'][i], name=f"{tag}_mix_norm")
        fnorm = w['ffn_norm'][i]
        if kind == 0:
            nqk = (A_HEADS + A_KV_HEADS) * A_HEAD_DIM
            wqkv = jnp.concatenate(
                [_pad_cols(w['a_w_qkv'][j][:, :nqk], A_HEADS + A_KV_HEADS, A_HEAD_DIM, LAYOUT_A),
                 _pad_cols(w['a_w_qkv'][j][:, nqk:], A_KV_HEADS, A_HEAD_DIM, LAYOUT_V)], axis=1)
            wo = _pad_rows(w['a_w_o'][j], A_HEADS, A_HEAD_DIM)
            gq, gk = _pad_vec(w['a_q_norm'][j], LAYOUT_A), _pad_vec(w['a_k_norm'][j], LAYOUT_A)
            qkv = _mm([(h, wqkv, 0)], out_dtype=F32, name=f"{tag}_qkv")
            qa, ka, va = _prep_a_fwd(qkv, cos_a, sin_a, gq, gk, name=f"{tag}_prep")
            o, lse = _swa_fwd(qa, ka, va, w['a_sinks'][j], name=f"{tag}_attn")
            x1, h2 = _mm([(o, wo, 0)], out_dtype=F32, res=x, norm_g=fnorm, name=f"{tag}_wo")
            mix = dict(wqkv=wqkv, wo=wo, gq=gq, gk=gk, qkv=qkv, qa=qa, ka=ka, va=va, o=o, lse=lse)
        elif kind == 1:
            cw = jnp.pad(w['b_conv_w'][j], ((0, 5), (0, 0)))
            proj = _mm([(h, w['b_w_in'][j], 0)], out_dtype=F32, name=f"{tag}_win")
            yb = _conv_fwd(proj, cw, name=f"{tag}_conv")
            x1, h2 = _mm([(yb, w['b_w_out'][j], 0)], out_dtype=F32, res=x, norm_g=fnorm, name=f"{tag}_wout")
            mix = dict(cw=cw, proj=proj, yb=yb)
        else:
            wdn = w['c_w_down'][j]
            nqk = C_Q_RANK + C_KV_RANK
            wdn = jnp.concatenate([wdn[:, :nqk], _pad_cols(wdn[:, nqk:], 1, C_ROPE, LAYOUT_KR)], axis=1)
            wq = _pad_cols(w['c_w_q_up'][j], C_HEADS, C_QK, LAYOUT_C)
            wkv = w['c_w_kv_up'][j].reshape(C_KV_RANK, C_HEADS, C_NOPE + C_V)
            wkn = _pad_cols(wkv[:, :, :C_NOPE].reshape(C_KV_RANK, -1), C_HEADS, C_NOPE, LAYOUT_KN)
            wv = _pad_cols(wkv[:, :, C_NOPE:].reshape(C_KV_RANK, -1), C_HEADS, C_V, LAYOUT_V)
            wo = _pad_rows(w['c_w_o'][j], C_HEADS, C_V)
            gq, gk = _pad_vec(w['c_q_norm'][j], LAYOUT_C), _pad_vec(w['c_k_norm'][j], LAYOUT_C)
            gqa, gkva = w['c_q_a_norm'][j].reshape(1, -1), w['c_kv_a_norm'][j].reshape(1, -1)
            dn = _mm([(h, wdn, 0)], out_dtype=F32, name=f"{tag}_wdown")
            cqn, ckvn = _prep_c1_fwd(dn, gqa, gkva, name=f"{tag}_prep1")
            qraw = _mm([(cqn, wq, 0)], out_dtype=F32, name=f"{tag}_wq")
            knope = _mm([(ckvn, wkn, 0)], out_dtype=F32, name=f"{tag}_wkn")
            vc = _mm([(ckvn, wv, 0)], out_dtype=BF, name=f"{tag}_wv")
            qc, kc = _prep_c2_fwd(qraw, knope, dn, cos_c, sin_c, gq, gk, name=f"{tag}_prep2")
            o, lse = _mla_fwd(qc, kc, vc, name=f"{tag}_attn")
            x1, h2 = _mm([(o, wo, 0)], out_dtype=F32, res=x, norm_g=fnorm, name=f"{tag}_wo")
            mix = dict(wdn=wdn, wq=wq, wkn=wkn, wv=wv, wo=wo, gq=gq, gk=gk, gqa=gqa, gkva=gkva, dn=dn, cqn=cqn,
                       ckvn=ckvn, qraw=qraw, knope=knope, vc=vc, qc=qc, kc=kc, o=o, lse=lse)
        gate, up, act = _ffn_up(h2, w['f_w_gate_up'][i], name=f"{tag}_ffn_up")
        saved.append(dict(x=x, h=h, x1=x1, h2=h2, gate=gate, up=up, act=act, mix=mix))
        if i + 1 < DEPTH:
            x, h = _mm([(act, w['f_w_down'][i], 0)], out_dtype=F32, res=x1, norm_g=w['mix_norm'][i + 1],
                       name=f"{tag}_ffn_down")
        else:
            x = _mm([(act, w['f_w_down'][i], 0)], out_dtype=F32, res=x1, name=f"{tag}_ffn_down")

    loss_blk, dx = _loss_fwd_bwd(x, target, name="loss")

    for i in reversed(range(DEPTH)):
        kind, j = i % N_MIXERS, i // N_MIXERS
        tag = f"l{i}b"
        sv = saved[i]
        mix = sv['mix']
        wgu, wd = w['f_w_gate_up'][i], w['f_w_down'][i]
        grads['f_w_down'][i] = _mm_tn(sv['act'], dx, name=f"{tag}_dwd")
        dgate, dup = _ffn_bwd_mid(dx, wd, sv['gate'], sv['up'], name=f"{tag}_ffn_mid")
        grads['f_w_gate_up'][i] = (_mm_tn(sv['h2'], dgate, name=f"{tag}_dwg"),
                                   _mm_tn(sv['h2'], dup, name=f"{tag}_dwu"))
        dx, dg = _mm_norm_bwd([(dgate, wgu, 0), (dup, wgu, 1)], sv['x1'], w['ffn_norm'][i], dx, name=f"{tag}_dh2")
        grads['ffn_norm'][i] = dg[0]
        if kind == 0:
            grads['a_w_o'][j] = _unpad_rows(_mm_tn(mix['o'], dx, name=f"{tag}_dwo"), A_HEADS, A_HEAD_DIM)
            do = _mm([(dx, mix['wo'], 0)], out_dtype=BF, trans_b=True, name=f"{tag}_do")
            dqa, dka, dva, dsinks = _swa_bwd(mix['qa'], mix['ka'], mix['va'], mix['o'], do, mix['lse'],
                                             w['a_sinks'][j], name=f"{tag}_attn")
            dqkv, dgq, dgk = _prep_a_bwd(mix['qkv'], dqa, dka, dva, cos_a, sin_a, mix['gq'], mix['gk'],
                                         name=f"{tag}_prep")
            grads['a_sinks'][j] = dsinks
            grads['a_q_norm'][j] = _unpad_vec(dgq, A_HEAD_DIM, LAYOUT_A)
            grads['a_k_norm'][j] = _unpad_vec(dgk, A_HEAD_DIM, LAYOUT_A)
            dwqkv = _mm_tn(sv['h'], dqkv, name=f"{tag}_dwqkv")
            nqk = (A_HEADS + A_KV_HEADS) * LANES
            grads['a_w_qkv'][j] = jnp.concatenate(
                [_unpad_cols(dwqkv[:, :nqk], A_HEADS + A_KV_HEADS, A_HEAD_DIM, LAYOUT_A),
                 _unpad_cols(dwqkv[:, nqk:], A_KV_HEADS, A_HEAD_DIM, LAYOUT_V)], axis=1)
            dh_pairs = [(dqkv, mix['wqkv'], 0)]
        elif kind == 1:
            grads['b_w_out'][j] = _mm_tn(mix['yb'], dx, name=f"{tag}_dwout")
            dyb = _mm([(dx, w['b_w_out'][j], 0)], out_dtype=F32, trans_b=True, name=f"{tag}_dyb")
            dproj, dcw = _conv_bwd(mix['proj'], dyb, mix['cw'], name=f"{tag}_conv")
            grads['b_conv_w'][j] = dcw
            grads['b_w_in'][j] = _mm_tn(sv['h'], dproj, name=f"{tag}_dwin")
            dh_pairs = [(dproj, w['b_w_in'][j], 0)]
        else:
            grads['c_w_o'][j] = _unpad_rows(_mm_tn(mix['o'], dx, name=f"{tag}_dwo"), C_HEADS, C_V)
            do = _mm([(dx, mix['wo'], 0)], out_dtype=BF, trans_b=True, name=f"{tag}_do")
            delta = _mla_delta(mix['o'], do, name=f"{tag}_delta")
            dqc, dkc, dvc = _mla_bwd(mix['qc'], mix['kc'], mix['vc'], do, mix['lse'], delta, name=f"{tag}_attn")
            dqraw, dknope, dkr, dgq, dgk = _prep_c2_bwd(mix['qraw'], mix['knope'], mix['dn'], dqc, dkc, cos_c, sin_c,
                                                        mix['gq'], mix['gk'], name=f"{tag}_prep2")
            grads['c_q_norm'][j] = _unpad_vec(dgq, C_QK, LAYOUT_C)
            grads['c_k_norm'][j] = _unpad_vec(dgk, C_QK, LAYOUT_C)
            grads['c_w_q_up'][j] = _unpad_cols(_mm_tn(mix['cqn'], dqraw, name=f"{tag}_dwq"), C_HEADS, C_QK,
                                               LAYOUT_C)
            dwkn = _unpad_cols(_mm_tn(mix['ckvn'], dknope, name=f"{tag}_dwkn"), C_HEADS, C_NOPE, LAYOUT_KN)
            dwv = _unpad_cols(_mm_tn(mix['ckvn'], dvc, name=f"{tag}_dwv"), C_HEADS, C_V, LAYOUT_V)
            grads['c_w_kv_up'][j] = jnp.concatenate(
                [dwkn.reshape(C_KV_RANK, C_HEADS, C_NOPE), dwv.reshape(C_KV_RANK, C_HEADS, C_V)], axis=2).reshape(
                C_KV_RANK, -1)
            dcq = _mm([(dqraw, mix['wq'], 0)], out_dtype=F32, trans_b=True, name=f"{tag}_dcq")
            dckv = _mm([(dknope, mix['wkn'], 0), (dvc, mix['wv'], 0)], out_dtype=F32, trans_b=True,
                       name=f"{tag}_dckv")
            ddn, dgqa, dgkva = _prep_c1_bwd(mix['dn'], dcq, dckv, dkr, mix['gqa'], mix['gkva'], name=f"{tag}_prep1")
            grads['c_q_a_norm'][j] = dgqa[0]
            grads['c_kv_a_norm'][j] = dgkva[0]
            dwdn = _mm_tn(sv['h'], ddn, name=f"{tag}_dwdown")
            nqk = C_Q_RANK + C_KV_RANK
            grads['c_w_down'][j] = jnp.concatenate(
                [dwdn[:, :nqk], _unpad_cols(dwdn[:, nqk:], 1, C_ROPE, LAYOUT_KR)], axis=1)
            dh_pairs = [(ddn, mix['wdn'], 0)]
        dx, dg = _mm_norm_bwd(dh_pairs, sv['x'], w['mix_norm'][i], dx, name=f"{tag}_dh")
        grads['mix_norm'][i] = dg[0]

    return loss_blk, dx, {n: (g if n in BIG else jnp.stack(g)) for n, g in grads.items()}


def _my_place():
    return lax.axis_index("x"), lax.axis_index("y"), lax.axis_index("c")


def _flips(x, y):
    return [(1 - x, y), (x, 1 - y), (1 - x, 1 - y)]


def _gather_weights(big, small):
    half = big.shape[0] // 2

    def body(big_ref, small_ref, bout_ref, sout_ref, send_b, recv_b, send_f, recv_f, send_s, recv_s, loc):
        x, y, c = _my_place()
        me = 2 * x + y
        mine = pl.ds(pl.multiple_of(c * half, 16), half)
        theirs = pl.ds(pl.multiple_of((1 - c) * half, 16), half)

        def ici(k, chip, to):
            return pltpu.make_async_remote_copy(src_ref=big_ref.at[mine, :], dst_ref=bout_ref.at[chip, mine, :],
                                                send_sem=send_b.at[k], recv_sem=recv_b.at[k],
                                                device_id=to, device_id_type=MESH)

        def forward(k, chip, rows):
            return pltpu.make_async_remote_copy(src_ref=bout_ref.at[chip, rows, :], dst_ref=bout_ref.at[chip, rows, :],
                                                send_sem=send_f.at[k], recv_sem=recv_f.at[k],
                                                device_id=(x, y, 1 - c), device_id_type=MESH)

        def small_copy(k, chip, to):
            return pltpu.make_async_remote_copy(src_ref=small_ref, dst_ref=sout_ref.at[chip],
                                                send_sem=send_s.at[k], recv_sem=recv_s.at[k],
                                                device_id=to, device_id_type=MESH)

        own = (pltpu.make_async_copy(big_ref, bout_ref.at[me], loc.at[0]),
               pltpu.make_async_copy(small_ref, sout_ref.at[me], loc.at[1]))
        for cp in own:
            cp.start()
        chips = _flips(x, y)
        sends = [ici(k, me, (px, py, c)) for k, (px, py) in enumerate(chips)]
        sends += [small_copy(k, me, (px, py, c)) for k, (px, py) in enumerate(chips)]
        for cp in sends:
            cp.start()
        passed = [forward(k, 2 * px + py, mine) for k, (px, py) in enumerate(chips)]
        for k, (px, py) in enumerate(chips):
            ici(k, 2 * px + py, (x, y, c)).wait_recv()
            passed[k].start()
        for k, (px, py) in enumerate(chips):
            forward(k, 2 * px + py, theirs).wait_recv()
            small_copy(k, 2 * px + py, (x, y, c)).wait_recv()
        for cp in sends + passed:
            cp.wait_send()
        for cp in own:
            cp.wait()

    hbm = pl.BlockSpec(memory_space=pltpu.HBM)
    sem3 = pltpu.SemaphoreType.DMA((3,))
    return pl.pallas_call(
        body, in_specs=[hbm, hbm], out_specs=[hbm, hbm],
        out_shape=[jax.ShapeDtypeStruct((N_CHIPS,) + big.shape, big.dtype),
                   jax.ShapeDtypeStruct((N_CHIPS,) + small.shape, small.dtype)],
        scratch_shapes=[sem3, sem3, sem3, sem3, sem3, sem3, pltpu.SemaphoreType.DMA((2,))],
        name="gather_weights")(big, small)


def _exchange_grads(gbig, gsmall):
    def body(gbig_ref, gsmall_ref, got_ref, ssum_ref, sbuf, send_b, recv_b, send_s, recv_s):
        x, y, c = _my_place()
        me = 4 * x + 2 * y + c

        def big_copy(k, src_chip, to):
            return pltpu.make_async_remote_copy(src_ref=gbig_ref.at[src_chip], dst_ref=got_ref.at[k],
                                                send_sem=send_b.at[k], recv_sem=recv_b.at[k],
                                                device_id=to, device_id_type=MESH)

        def small_copy(k, slot, to):
            return pltpu.make_async_remote_copy(src_ref=gsmall_ref, dst_ref=sbuf.at[slot],
                                                send_sem=send_s.at[k], recv_sem=recv_s.at[k],
                                                device_id=to, device_id_type=MESH)

        peers = [(x ^ (k >> 2), y ^ ((k >> 1) & 1), c ^ (k & 1)) for k in range(1, N_DEV)]
        bigs = [big_copy(k, 2 * px + py, (px, py, c)) for k, (px, py) in enumerate(_flips(x, y))]
        smalls = [small_copy(k, me, p) for k, p in enumerate(peers)]
        for cp in bigs + smalls:
            cp.start()
        sbuf[me] = gsmall_ref[...]
        for k, (px, py, pc) in enumerate(peers):
            small_copy(k, 4 * px + 2 * py + pc, (x, y, c)).wait_recv()
        acc = sbuf[0]
        for d in range(1, N_DEV):
            acc = acc + sbuf[d]
        ssum_ref[...] = acc
        for k in range(3):
            big_copy(k, 0, (x, y, c)).wait_recv()
        for cp in bigs + smalls:
            cp.wait_send()

    hbm = pl.BlockSpec(memory_space=pltpu.HBM)
    vmem = pl.BlockSpec(memory_space=pltpu.VMEM)
    return pl.pallas_call(
        body, in_specs=[hbm, vmem], out_specs=[hbm, vmem],
        out_shape=[jax.ShapeDtypeStruct((3,) + gbig.shape[1:], gbig.dtype),
                   jax.ShapeDtypeStruct(gsmall.shape, gsmall.dtype)],
        scratch_shapes=[pltpu.VMEM((N_DEV,) + gsmall.shape, gsmall.dtype),
                        pltpu.SemaphoreType.DMA((3,)), pltpu.SemaphoreType.DMA((3,)),
                        pltpu.SemaphoreType.DMA((N_DEV - 1,)), pltpu.SemaphoreType.DMA((N_DEV - 1,))],
        name="exchange_grads")(gbig, gsmall)


def _sum_shards(chip, mine, got):
    r, cols = mine.shape[1:]
    tr = _tile(r, PACK_TILE_ROWS, 16)

    def body(chip_ref, mine_ref, got_ref, o_ref):
        o_ref[...] = ((mine_ref[0] + got_ref[0].astype(F32)) + got_ref[1].astype(F32)) + got_ref[2].astype(F32)

    grid_spec = pltpu.PrefetchScalarGridSpec(
        num_scalar_prefetch=1, grid=(r // tr,),
        in_specs=[pl.BlockSpec((1, tr, cols), lambda i, chip_ref: (chip_ref[0], i, 0)),
                  pl.BlockSpec((3, tr, cols), lambda i, chip_ref: (0, i, 0))],
        out_specs=pl.BlockSpec((tr, cols), lambda i, chip_ref: (i, 0)))
    return pl.pallas_call(
        body, grid_spec=grid_spec, out_shape=jax.ShapeDtypeStruct((r, cols), F32),
        compiler_params=_cp("parallel"), name="sum_shards")(chip, mine, got)


def _swap_sibling(t, *, name):
    def body(t_ref, got_ref, send_sem, recv_sem):
        x, y, c = _my_place()
        cp = pltpu.make_async_remote_copy(src_ref=t_ref, dst_ref=got_ref, send_sem=send_sem, recv_sem=recv_sem,
                                          device_id=(x, y, 1 - c), device_id_type=MESH)
        cp.start()
        cp.wait()

    hbm = pl.BlockSpec(memory_space=pltpu.HBM)
    return pl.pallas_call(
        body, in_specs=[hbm], out_specs=hbm, out_shape=jax.ShapeDtypeStruct(t.shape, t.dtype),
        scratch_shapes=[pltpu.SemaphoreType.DMA, pltpu.SemaphoreType.DMA], name=name)(t)


def _give_halves(parts, *, name):
    n = len(parts)
    r, cols = parts[0].shape
    half = r // 2

    def body(*refs):
        got_ref, send_sems, recv_sems = refs[n:]
        x, y, c = _my_place()
        theirs = pl.ds(pl.multiple_of((1 - c) * half, 16), half)
        copies = [pltpu.make_async_remote_copy(src_ref=refs[k].at[theirs, :], dst_ref=got_ref.at[k],
                                               send_sem=send_sems.at[k], recv_sem=recv_sems.at[k],
                                               device_id=(x, y, 1 - c), device_id_type=MESH) for k in range(n)]
        for cp in copies:
            cp.start()
        for cp in copies:
            cp.wait()

    hbm = pl.BlockSpec(memory_space=pltpu.HBM)
    return pl.pallas_call(
        body, in_specs=[hbm] * n, out_specs=hbm, out_shape=jax.ShapeDtypeStruct((n, half, cols), parts[0].dtype),
        scratch_shapes=[pltpu.SemaphoreType.DMA((n,)), pltpu.SemaphoreType.DMA((n,))], name=name)(*parts)


def _add_kept(core, parts, got, *, name):
    n, r, cols = got.shape
    tr = _tile(r, PACK_TILE_ROWS, 16)
    nblk = r // tr

    def body(core_ref, *refs):
        b_ref, o_ref, ob_ref = refs[n:]
        for k in range(n):
            sm = refs[k][...] + b_ref[k]
            o_ref[k] = sm
            ob_ref[k] = sm.astype(BF)

    blk = pl.BlockSpec((n, tr, cols), lambda i, core_ref: (0, i, 0))
    kept = pl.BlockSpec((tr, cols), lambda i, core_ref: (core_ref[0] * nblk + i, 0))
    grid_spec = pltpu.PrefetchScalarGridSpec(
        num_scalar_prefetch=1, grid=(nblk,), in_specs=[kept] * n + [blk], out_specs=[blk, blk])
    return pl.pallas_call(
        body, grid_spec=grid_spec,
        out_shape=[jax.ShapeDtypeStruct(got.shape, F32), jax.ShapeDtypeStruct(got.shape, BF)],
        compiler_params=_cp("parallel"), name=name)(core, *parts, got)


def _by_core(c, mine, sibling):
    return jnp.where(c == 0, mine, sibling), jnp.where(c == 0, sibling, mine)


def _adamw(wt, m, v, g, *, name):
    r, cols = wt.shape
    tr = _tile(r, 256, 8)

    def body(w_ref, m_ref, v_ref, g_ref, d_ref, nm_ref, nv_ref):
        gv = g_ref[...]
        nm = ADAM_B1 * m_ref[...] + (1.0 - ADAM_B1) * gv
        nv = ADAM_B2 * v_ref[...] + (1.0 - ADAM_B2) * (gv * gv)
        m_hat = nm / (1.0 - ADAM_B1 ** ADAM_STEP)
        v_hat = nv / (1.0 - ADAM_B2 ** ADAM_STEP)
        d_ref[...] = -ADAM_LR * (m_hat / (jnp.sqrt(v_hat) + ADAM_EPS) + ADAM_WD * w_ref[...])
        nm_ref[...] = nm
        nv_ref[...] = nv

    blk = pl.BlockSpec((tr, cols), lambda i: (i, 0))
    return pl.pallas_call(
        body, grid=(r // tr,), in_specs=[blk] * 4, out_specs=[blk] * 3,
        out_shape=[jax.ShapeDtypeStruct((r, cols), F32)] * 3,
        compiler_params=_cp("parallel"), name=name)(wt, m, v, g)


def _shard_shape(full, axis):
    return tuple(d // N_CHIPS if a == axis else d for a, d in enumerate(full))


def _pack_rows(n):
    rows = -(-n // PACK_COLS)
    return -(-rows // PACK_ROW_MULT) * PACK_ROW_MULT


def _pack(parts, rows, dtype):
    flat = jnp.concatenate([p.reshape(-1).astype(dtype) for p in parts])
    return jnp.pad(flat, (0, rows * PACK_COLS - flat.shape[0])).reshape(rows, PACK_COLS)


def _join_shards(stacked, axis):
    moved = jnp.moveaxis(stacked, 0, axis)
    shp = moved.shape
    return moved.reshape(shp[:axis] + (shp[axis] * shp[axis + 1],) + shp[axis + 2:])


def _split_shards(full, axis):
    shp = full.shape
    split = full.reshape(shp[:axis] + (N_CHIPS, shp[axis] // N_CHIPS) + shp[axis + 1:])
    return jnp.moveaxis(split, axis, 0)


SMALL_ROWS = 128
SMALL_LAYOUT = [('mix_norm', DEPTH * D_MODEL), ('ffn_norm', DEPTH * D_MODEL), ('b_conv_w', 3 * D_MODEL),
                ('c_q_a_norm', C_Q_RANK), ('c_kv_a_norm', C_KV_RANK), ('a_q_norm', 2 * A_HEAD_DIM),
                ('a_k_norm', 2 * A_HEAD_DIM), ('a_sinks', 2 * A_HEADS), ('c_q_norm', C_QK), ('c_k_norm', C_QK),
                ('loss', 1)]


def _small_offsets():
    offs, row = {}, 0
    for name, n in SMALL_LAYOUT:
        offs[name] = (row * LANES, n)
        row += -(-n // LANES)
    assert row <= SMALL_ROWS
    return offs


def kernel(x, positions, mix_norm, ffn_norm, a_w_qkv, a_q_norm, a_k_norm, a_sinks, a_w_o, b_w_in, b_conv_w, b_w_out, c_w_down, c_q_a_norm, c_kv_a_norm, c_w_q_up, c_w_kv_up, c_q_norm, c_k_norm, c_w_o, f_w_gate_up, f_w_down, loss_target, m_mix_norm, m_ffn_norm, m_a_w_qkv, m_a_q_norm, m_a_k_norm, m_a_sinks, m_a_w_o, m_b_w_in, m_b_conv_w, m_b_w_out, m_c_w_down, m_c_q_a_norm, m_c_kv_a_norm, m_c_w_q_up, m_c_w_kv_up, m_c_q_norm, m_c_k_norm, m_c_w_o, m_f_w_gate_up, m_f_w_down, v_mix_norm, v_ffn_norm, v_a_w_qkv, v_a_q_norm, v_a_k_norm, v_a_sinks, v_a_w_o, v_b_w_in, v_b_conv_w, v_b_w_out, v_c_w_down, v_c_q_a_norm, v_c_kv_a_norm, v_c_w_q_up, v_c_w_kv_up, v_c_q_norm, v_c_k_norm, v_c_w_o, v_f_w_gate_up, v_f_w_down):
    args = dict(locals())
    wshard = {n: args[n] for n in WEIGHTS}
    sharded = {**BIG, **SMALL_SHARDED}
    chip = 2 * lax.axis_index("x") + lax.axis_index("y")

    n_big = sum(wshard[n].size for n in BIG)
    rows = _pack_rows(n_big)
    big = _pack([wshard[n] for n in BIG], rows, BF)
    small = jnp.concatenate([wshard[n].reshape(-1) for n in SMALL_SHARDED])
    small = jnp.pad(small, (0, 8 * LANES - small.shape[0])).reshape(8, LANES)
    core = lax.axis_index("c")
    big_all, small_all = _gather_weights(big, small)
    big_all = big_all.reshape(N_CHIPS, -1)
    small_all = small_all.reshape(N_CHIPS, -1)
    w = {}
    off = 0
    for n, axis in BIG.items():
        shp = wshard[n].shape
        per = shp[1] * shp[2]
        w[n] = [_join_shards(big_all[:, off + l * per:off + (l + 1) * per].reshape((N_CHIPS,) + shp[1:]), axis - 1)
                for l in range(shp[0])]
        off += wshard[n].size
    off = 0
    for n, axis in SMALL_SHARDED.items():
        sz = wshard[n].size
        w[n] = _join_shards(small_all[:, off:off + sz].reshape((N_CHIPS,) + wshard[n].shape), axis)
        off += sz
    for n in WEIGHTS:
        if n not in sharded:
            w[n] = wshard[n]

    loss_blk, grad_x, g = _local_step(x[0], positions[0], w, loss_target[0])

    def shard_of(n, item, k):
        if isinstance(item, tuple):
            item = item[k // 2]
            k = k % 2
            width = item.shape[1] // 2
        else:
            width = item.shape[BIG[n] - 1] // N_CHIPS
        return item[:, k * width:(k + 1) * width] if BIG[n] == 2 else item[k * width:(k + 1) * width]

    gparts = [_pack([shard_of(n, item, k) for n in BIG for item in g[n]], rows, F32) for k in range(N_CHIPS)]
    offs = _small_offsets()
    smalls = {**{n: g[n] for n, _ in SMALL_LAYOUT if n != 'loss'}, 'loss': loss_blk[0:1, 0:1]}
    gsmall = jnp.concatenate(
        [jnp.pad(smalls[n].reshape(-1), (0, -cnt % LANES)) for n, cnt in SMALL_LAYOUT])
    gsmall = jnp.pad(gsmall, (0, SMALL_ROWS * LANES - gsmall.shape[0])).reshape(SMALL_ROWS, LANES)
    from_sibling = _give_halves(gparts, name="swap_grads")
    chip_sum, chip_sum_bf = _add_kept(core.reshape(1).astype(jnp.int32), gparts, from_sibling, name="add_sibling")
    got, ssum = _exchange_grads(chip_sum_bf, gsmall)
    part = _sum_shards(chip.reshape(1).astype(jnp.int32), chip_sum, got)
    part_sib = _swap_sibling(part, name="swap_sums")
    gflat = jnp.concatenate(_by_core(core, part, part_sib), axis=0).reshape(-1)

    ssum = ssum.reshape(-1)
    outs = {}
    off = 0
    for n in WEIGHTS:
        wt = wshard[n]
        if n in BIG:
            grad = gflat[off:off + wt.size]
            off += wt.size
        else:
            o0, cnt = offs[n]
            grad = ssum[o0:o0 + cnt].reshape(g[n].shape)
            if n in SMALL_SHARDED:
                grad = lax.dynamic_index_in_dim(_split_shards(grad, SMALL_SHARDED[n]), chip, 0, keepdims=False)
        shape2 = (-1, wt.shape[-1])
        grad = grad.reshape(shape2)
        res = _adamw(wt.reshape(shape2), args['m_' + n].reshape(shape2), args['v_' + n].reshape(shape2), grad,
                     name=f"adamw_{n}")
        outs[n] = [r.reshape(wt.shape) for r in [grad] + list(res)]
    loss = ssum[offs['loss'][0]]
    return (loss, grad_x[None], *[outs[n][0] for n in WEIGHTS], *[outs[n][1] for n in WEIGHTS],
            *[outs[n][2] for n in WEIGHTS], *[outs[n][3] for n in WEIGHTS])
```

```python
import functools

import jax
import jax.numpy as jnp
from jax import lax
from jax.experimental import pallas as pl
from jax.experimental.pallas import tpu as pltpu

F32 = jnp.float32
BF = jnp.bfloat16

D_MODEL = 1024
DEPTH = 4
N_MIXERS = 3
ROPE_THETA = 500000.0
EPS = 1e-6
BLOCK = 128
LANES = 128

A_HEADS, A_KV_HEADS, A_HEAD_DIM = 16, 4, 64
A_ROT_DIM = A_HEAD_DIM // 4
A_GROUP = A_HEADS // A_KV_HEADS
C_HEADS, C_NOPE, C_ROPE, C_V, C_Q_RANK, C_KV_RANK = 16, 64, 32, 64, 384, 256
C_QK = C_NOPE + C_ROPE
D_FF = 2816

ADAM_LR, ADAM_B1, ADAM_B2, ADAM_EPS, ADAM_WD, ADAM_STEP = 0.001, 0.9, 0.999, 1e-08, 0.01, 10

N_CHIPS = 4
N_DEV = 8
MESH = pl.DeviceIdType.MESH

VMEM_LIMIT = 56 * 1024 * 1024
ROW_TILE = 512
PREP_TILE = 256
ATTN_TILE = 512
FWD_WIDE = 4
SWA_BLOCKS = 8
NEG = -1e30

WEIGHTS = ['mix_norm', 'ffn_norm', 'a_w_qkv', 'a_q_norm', 'a_k_norm', 'a_sinks', 'a_w_o', 'b_w_in', 'b_conv_w',
           'b_w_out', 'c_w_down', 'c_q_a_norm', 'c_kv_a_norm', 'c_w_q_up', 'c_w_kv_up', 'c_q_norm', 'c_k_norm',
           'c_w_o', 'f_w_gate_up', 'f_w_down']
BIG = {'a_w_qkv': 2, 'a_w_o': 1, 'b_w_in': 2, 'b_w_out': 1, 'c_w_down': 1, 'c_w_q_up': 2, 'c_w_kv_up': 2,
       'c_w_o': 1, 'f_w_gate_up': 2, 'f_w_down': 1}
SMALL_SHARDED = {'b_conv_w': 2, 'c_q_a_norm': 1, 'c_kv_a_norm': 1}
PACK_COLS = 256
PACK_ROW_MULT = 32
PACK_TILE_ROWS = 1024


def _cp(*sem):
    return pltpu.CompilerParams(dimension_semantics=sem, vmem_limit_bytes=VMEM_LIMIT)


def _tile(n, target, mult):
    if n <= target:
        return n
    t = (target // mult) * mult
    while t >= mult:
        if n % t == 0:
            return t
        t -= mult
    raise ValueError(f"no tile for {n}")


def _dot(a, b):
    return lax.dot_general(a, b, (((1,), (0,)), ((), ())), preferred_element_type=F32)


def _dot_nt(a, b):
    return lax.dot_general(a, b, (((1,), (1,)), ((), ())), preferred_element_type=F32)


def _dot_tn(a, b):
    return lax.dot_general(a, b, (((0,), (0,)), ((), ())), preferred_element_type=F32)


def _mm(pairs, *, out_dtype, name, res=None, trans_b=False, norm_g=None):
    a0, b0, _ = pairs[0]
    m = a0.shape[0]
    n = b0.shape[0] if trans_b else b0.shape[1]
    tm = _tile(m, ROW_TILE, 8)
    tn = _tile(n, 1024, 128)
    n_pairs = len(pairs)
    has_res = res is not None
    has_norm = norm_g is not None
    assert not has_norm or tn == n

    def body(*refs):
        acc = None
        for p in range(n_pairs):
            a = refs[2 * p][...].astype(BF)
            b = refs[2 * p + 1][...].astype(BF)
            d = _dot_nt(a, b) if trans_b else _dot(a, b)
            acc = d if acc is None else acc + d
        nxt = 2 * n_pairs
        if has_res:
            acc = acc + refs[nxt][...]
            nxt += 1
        if has_norm:
            y = acc * lax.rsqrt(jnp.mean(acc * acc, axis=-1, keepdims=True) + EPS)
            refs[-1][...] = (y * refs[nxt][...]).astype(BF)
            refs[-2][...] = acc.astype(out_dtype)
        else:
            refs[-1][...] = acc.astype(out_dtype)

    in_specs, args = [], []
    for a, b, kblk in pairs:
        k = a.shape[1]
        in_specs.append(pl.BlockSpec((tm, k), lambda j, i: (i, 0)))
        if trans_b:
            in_specs.append(pl.BlockSpec((tn, k), functools.partial(lambda j, i, kb: (j, kb), kb=kblk)))
        else:
            in_specs.append(pl.BlockSpec((k, tn), lambda j, i: (0, j)))
        args += [a, b]
    if has_res:
        in_specs.append(pl.BlockSpec((tm, tn), lambda j, i: (i, j)))
        args.append(res)
    out_spec = pl.BlockSpec((tm, tn), lambda j, i: (i, j))
    out_specs, out_shape = out_spec, jax.ShapeDtypeStruct((m, n), out_dtype)
    if has_norm:
        in_specs.append(pl.BlockSpec((1, n), lambda j, i: (0, 0)))
        args.append(norm_g.reshape(1, n))
        out_specs, out_shape = [out_spec, out_spec], [out_shape, jax.ShapeDtypeStruct((m, n), BF)]
    return pl.pallas_call(
        body, grid=(n // tn, m // tm), in_specs=in_specs, out_specs=out_specs, out_shape=out_shape,
        compiler_params=_cp("parallel", "parallel"), name=name)(*args)


def _mm_norm_bwd(pairs, x, g, dres, *, name):
    m, d = x.shape
    tm = _tile(m, ROW_TILE // 2, 8)
    n_pairs = len(pairs)

    def body(*refs):
        x_ref, g_ref, dres_ref, dx_ref, dg_ref = refs[2 * n_pairs:]

        @pl.when(pl.program_id(0) == 0)
        def _():
            dg_ref[...] = jnp.zeros_like(dg_ref)

        dh = None
        for p in range(n_pairs):
            dpart = _dot_nt(refs[2 * p][...].astype(BF), refs[2 * p + 1][...].astype(BF))
            dh = dpart if dh is None else dh + dpart
        dx, xhat = _rms_bwd_math(x_ref[...], g_ref[...], dh, d)
        dx_ref[...] = dres_ref[...] + dx
        dg_ref[0:1, :] += jnp.sum(dh * xhat, axis=0, keepdims=True)

    in_specs, args = [], []
    for a, b, kblk in pairs:
        k = a.shape[1]
        in_specs.append(pl.BlockSpec((tm, k), lambda i: (i, 0)))
        in_specs.append(pl.BlockSpec((d, k), functools.partial(lambda i, kb: (0, kb), kb=kblk)))
        args += [a, b]
    row = pl.BlockSpec((tm, d), lambda i: (i, 0))
    in_specs += [row, pl.BlockSpec((1, d), lambda i: (0, 0)), row]
    dx, dg = pl.pallas_call(
        body, grid=(m // tm,), in_specs=in_specs,
        out_specs=[row, pl.BlockSpec((8, d), lambda i: (0, 0))],
        out_shape=[jax.ShapeDtypeStruct((m, d), F32), jax.ShapeDtypeStruct((8, d), F32)],
        compiler_params=_cp("arbitrary"), name=name)(*args, x, g.reshape(1, d), dres)
    return dx, dg[0:1]


def _mm_tn(a, b, *, name):
    m, k = a.shape
    n = b.shape[1]
    tm = _tile(m, 2 * ROW_TILE, 8)
    tk = _tile(k, 1408, 128)
    tn = _tile(n, 1408, 128)

    def body(a_ref, b_ref, o_ref):
        @pl.when(pl.program_id(2) == 0)
        def _():
            o_ref[...] = jnp.zeros_like(o_ref)

        o_ref[...] += _dot_tn(a_ref[...].astype(BF), b_ref[...].astype(BF))

    return pl.pallas_call(
        body, grid=(k // tk, n // tn, m // tm),
        in_specs=[pl.BlockSpec((tm, tk), lambda kk, j, i: (i, kk)),
                  pl.BlockSpec((tm, tn), lambda kk, j, i: (i, j))],
        out_specs=pl.BlockSpec((tk, tn), lambda kk, j, i: (kk, j)),
        out_shape=jax.ShapeDtypeStruct((k, n), F32),
        compiler_params=_cp("parallel", "parallel", "arbitrary"), name=name)(a, b)


def _rmsnorm_fwd(x, g, *, name):
    s, d = x.shape
    tm = _tile(s, ROW_TILE, 8)

    def body(x_ref, g_ref, h_ref):
        xv = x_ref[...]
        y = xv * lax.rsqrt(jnp.mean(xv * xv, axis=-1, keepdims=True) + EPS)
        h_ref[...] = (y * g_ref[...]).astype(BF)

    return pl.pallas_call(
        body, grid=(s // tm,),
        in_specs=[pl.BlockSpec((tm, d), lambda i: (i, 0)), pl.BlockSpec((1, d), lambda i: (0, 0))],
        out_specs=pl.BlockSpec((tm, d), lambda i: (i, 0)),
        out_shape=jax.ShapeDtypeStruct((s, d), BF),
        compiler_params=_cp("parallel"), name=name)(x, g.reshape(1, d))


def _row_sum(v):
    return jnp.sum(v, axis=-1, keepdims=True)


def _slot_sum(v):
    ones = jnp.ones((LANES, LANES), BF)
    hi = v.astype(BF)
    lo = (v - hi.astype(F32)).astype(BF)
    return _dot(hi, ones) + _dot(lo, ones)


def _rms_bwd_math(xv, g, dh, n, row_sum=_row_sum):
    rstd = lax.rsqrt(row_sum(xv * xv) * (1.0 / n) + EPS)
    xhat = xv * rstd
    dxh = dh * g
    dx = rstd * (dxh - xhat * (row_sum(dxh * xhat) * (1.0 / n)))
    return dx, xhat


def _loss_fwd_bwd(y, target, *, name):
    s, d = y.shape
    tm = _tile(s, ROW_TILE, 8)

    def body(y_ref, t_ref, loss_ref, dy_ref):
        @pl.when(pl.program_id(0) == 0)
        def _():
            loss_ref[...] = jnp.zeros_like(loss_ref)

        err = y_ref[...] - t_ref[...]
        dy_ref[...] = err * (1.0 / d)
        loss_ref[...] += 0.5 * jnp.sum(jnp.sum(err * err, axis=-1, keepdims=True) * (1.0 / d))

    row = pl.BlockSpec((tm, d), lambda i: (i, 0))
    return pl.pallas_call(
        body, grid=(s // tm,), in_specs=[row, row],
        out_specs=[pl.BlockSpec((8, LANES), lambda i: (0, 0)), row],
        out_shape=[jax.ShapeDtypeStruct((8, LANES), F32), jax.ShapeDtypeStruct((s, d), F32)],
        compiler_params=_cp("arbitrary"), name=name)(y, target)


HALF = LANES // 2
A_HR, C_HR = A_ROT_DIM // 2, C_ROPE // 2
A_X1, C_X1 = 0, 32
LAYOUT_A = [(A_X1, 0, A_HR), (A_HR, A_ROT_DIM, A_HEAD_DIM - A_ROT_DIM), (HALF + A_X1, A_HR, A_HR)]
LAYOUT_C = [(0, 0, 32), (C_X1, C_NOPE, C_HR), (HALF, 32, 32), (HALF + C_X1, C_NOPE + C_HR, C_HR)]
LAYOUT_KN = [(0, 0, 32), (HALF, 32, 32)]
LAYOUT_KR = [(C_X1, 0, C_HR), (HALF + C_X1, C_HR, C_HR)]
LAYOUT_V = [(0, 0, C_V)]


def _rope_rows():
    lane = jnp.arange(LANES)
    fa = ROPE_THETA ** (-jnp.arange(0, A_ROT_DIM, 2, dtype=F32) / A_ROT_DIM)
    fc = ROPE_THETA ** (-jnp.arange(0, C_ROPE, 2, dtype=F32) / C_ROPE)

    def rows(f, x1, hr):
        first = (lane >= x1) & (lane < x1 + hr)
        second = (lane >= HALF + x1) & (lane < HALF + x1 + hr)
        freq = jnp.where(first | second, f[(lane - x1) % HALF % hr], 0.0)
        return freq, jnp.where(first, -1.0, jnp.where(second, 1.0, 0.0))

    freq_a, sign_a = rows(fa, A_X1, A_HR)
    freq_c, sign_c = rows(fc, C_X1, C_HR)
    return jnp.stack([freq_a, sign_a, freq_c, sign_c] + [jnp.zeros(LANES)] * 4).astype(F32)


def _rope_tables(pos_col, *, name):
    s = pos_col.shape[0]
    tm = _tile(s, ROW_TILE, 8)

    def body(pos_ref, rows_ref, ca_ref, sa_ref, cc_ref, sc_ref):
        p = pos_ref[...].astype(F32)
        ang_a = p * rows_ref[0:1, :]
        ang_c = p * rows_ref[2:3, :]
        ca_ref[...] = jnp.cos(ang_a)
        sa_ref[...] = jnp.sin(ang_a) * rows_ref[1:2, :]
        cc_ref[...] = jnp.cos(ang_c)
        sc_ref[...] = jnp.sin(ang_c) * rows_ref[3:4, :]

    tab = pl.BlockSpec((tm, LANES), lambda i: (i, 0))
    return pl.pallas_call(
        body, grid=(s // tm,),
        in_specs=[pl.BlockSpec((tm, 1), lambda i: (i, 0)), pl.BlockSpec((8, LANES), lambda i: (0, 0))],
        out_specs=[tab] * 4, out_shape=[jax.ShapeDtypeStruct((s, LANES), F32)] * 4,
        compiler_params=_cp("parallel"), name=name)(pos_col, _rope_rows())


def _rope(xv, cos, sin):
    return xv * cos + pltpu.roll(xv, HALF, 1) * sin


def _slot(ref, t):
    return ref[:, t * LANES:(t + 1) * LANES]


def _head_norm_rope(xv, g, cos, sin, n):
    y = xv * lax.rsqrt(_slot_sum(xv * xv) * (1.0 / n) + EPS) * g
    return _rope(y, cos, sin)


def _head_norm_rope_bwd(xv, g, dy, cos, sin, n):
    dyn = _rope(dy, cos, -sin)
    dx, xhat = _rms_bwd_math(xv, g, dyn, n, _slot_sum)
    return dx, jnp.sum(dyn * xhat, axis=0, keepdims=True)


def _prep_a_fwd(qkv, cos, sin, gq, gk, *, name):
    s = qkv.shape[0]
    tm = _tile(s, PREP_TILE, 8)
    nq, nkv = A_HEADS, A_KV_HEADS

    def body(x_ref, c_ref, s_ref, gq_ref, gk_ref, q_ref, k_ref, v_ref):
        cv, sv = c_ref[...], s_ref[...]
        for t in range(nq):
            q_ref[:, t * LANES:(t + 1) * LANES] = _head_norm_rope(
                _slot(x_ref, t), gq_ref[...], cv, sv, A_HEAD_DIM).astype(BF)
        for t in range(nkv):
            k_ref[:, t * LANES:(t + 1) * LANES] = _head_norm_rope(
                _slot(x_ref, nq + t), gk_ref[...], cv, sv, A_HEAD_DIM).astype(BF)
        v_ref[...] = x_ref[:, (nq + nkv) * LANES:].astype(BF)

    def rows(w):
        return pl.BlockSpec((tm, w), lambda i: (i, 0))

    vec = pl.BlockSpec((1, LANES), lambda i: (0, 0))
    return pl.pallas_call(
        body, grid=(s // tm,),
        in_specs=[rows(qkv.shape[1]), rows(LANES), rows(LANES), vec, vec],
        out_specs=[rows(nq * LANES), rows(nkv * LANES), rows(nkv * LANES)],
        out_shape=[jax.ShapeDtypeStruct((s, nq * LANES), BF), jax.ShapeDtypeStruct((s, nkv * LANES), BF),
                   jax.ShapeDtypeStruct((s, nkv * LANES), BF)],
        compiler_params=_cp("parallel"), name=name)(qkv, cos, sin, gq, gk)


def _prep_a_bwd(qkv, dq, dk, dv, cos, sin, gq, gk, *, name):
    s = qkv.shape[0]
    tm = _tile(s, PREP_TILE, 8)
    nq, nkv = A_HEADS, A_KV_HEADS

    def body(x_ref, dq_ref, dk_ref, dv_ref, c_ref, s_ref, gq_ref, gk_ref, dx_ref, dg_ref):
        @pl.when(pl.program_id(0) == 0)
        def _():
            dg_ref[...] = jnp.zeros_like(dg_ref)

        cv, sv = c_ref[...], s_ref[...]
        dgq = jnp.zeros((1, LANES), F32)
        dgk = jnp.zeros((1, LANES), F32)
        for t in range(nq):
            dx, dg = _head_norm_rope_bwd(_slot(x_ref, t), gq_ref[...], _slot(dq_ref, t), cv, sv,
                                         A_HEAD_DIM)
            dx_ref[:, t * LANES:(t + 1) * LANES] = dx.astype(BF)
            dgq = dgq + dg
        for t in range(nkv):
            dx, dg = _head_norm_rope_bwd(_slot(x_ref, nq + t), gk_ref[...], _slot(dk_ref, t), cv, sv,
                                         A_HEAD_DIM)
            dx_ref[:, (nq + t) * LANES:(nq + t + 1) * LANES] = dx.astype(BF)
            dgk = dgk + dg
        dx_ref[:, (nq + nkv) * LANES:] = dv_ref[...].astype(BF)
        dg_ref[0:1, :] += dgq
        dg_ref[1:2, :] += dgk

    def rows(w):
        return pl.BlockSpec((tm, w), lambda i: (i, 0))

    vec = pl.BlockSpec((1, LANES), lambda i: (0, 0))
    dx, dg = pl.pallas_call(
        body, grid=(s // tm,),
        in_specs=[rows(qkv.shape[1]), rows(nq * LANES), rows(nkv * LANES), rows(nkv * LANES), rows(LANES),
                  rows(LANES), vec, vec],
        out_specs=[rows(qkv.shape[1]), pl.BlockSpec((8, LANES), lambda i: (0, 0))],
        out_shape=[jax.ShapeDtypeStruct(qkv.shape, BF), jax.ShapeDtypeStruct((8, LANES), F32)],
        compiler_params=_cp("arbitrary"), name=name)(qkv, dq, dk, dv, cos, sin, gq, gk)
    return dx, dg[0:1], dg[1:2]


def _prep_c1_fwd(dn, gq, gkv, *, name):
    s = dn.shape[0]
    tm = _tile(s, ROW_TILE, 8)

    def body(x_ref, gq_ref, gkv_ref, cq_ref, ckv_ref):
        for lo, n, g_ref, o_ref in ((0, C_Q_RANK, gq_ref, cq_ref), (C_Q_RANK, C_KV_RANK, gkv_ref, ckv_ref)):
            xv = x_ref[:, lo:lo + n]
            y = xv * lax.rsqrt(jnp.mean(xv * xv, axis=-1, keepdims=True) + EPS)
            o_ref[...] = (y * g_ref[...]).astype(BF)

    def rows(w):
        return pl.BlockSpec((tm, w), lambda i: (i, 0))

    return pl.pallas_call(
        body, grid=(s // tm,),
        in_specs=[rows(dn.shape[1]), pl.BlockSpec((1, C_Q_RANK), lambda i: (0, 0)),
                  pl.BlockSpec((1, C_KV_RANK), lambda i: (0, 0))],
        out_specs=[rows(C_Q_RANK), rows(C_KV_RANK)],
        out_shape=[jax.ShapeDtypeStruct((s, C_Q_RANK), BF), jax.ShapeDtypeStruct((s, C_KV_RANK), BF)],
        compiler_params=_cp("parallel"), name=name)(dn, gq, gkv)


def _prep_c1_bwd(dn, dcq, dckv, dkr, gq, gkv, *, name):
    s = dn.shape[0]
    tm = _tile(s, ROW_TILE, 8)

    def body(x_ref, dcq_ref, dckv_ref, dkr_ref, gq_ref, gkv_ref, dx_ref, dgq_ref, dgkv_ref):
        @pl.when(pl.program_id(0) == 0)
        def _():
            dgq_ref[...] = jnp.zeros_like(dgq_ref)
            dgkv_ref[...] = jnp.zeros_like(dgkv_ref)

        for lo, n, g_ref, d_ref, dg_ref in ((0, C_Q_RANK, gq_ref, dcq_ref, dgq_ref),
                                            (C_Q_RANK, C_KV_RANK, gkv_ref, dckv_ref, dgkv_ref)):
            dv = d_ref[...]
            dx, xhat = _rms_bwd_math(x_ref[:, lo:lo + n], g_ref[...], dv, n)
            dx_ref[:, lo:lo + n] = dx.astype(BF)
            dg_ref[0:1, :] += jnp.sum(dv * xhat, axis=0, keepdims=True)
        dx_ref[:, C_Q_RANK + C_KV_RANK:] = dkr_ref[...].astype(BF)

    def rows(w):
        return pl.BlockSpec((tm, w), lambda i: (i, 0))

    def vec(w, r=1):
        return pl.BlockSpec((r, w), lambda i: (0, 0))

    dx, dgq, dgkv = pl.pallas_call(
        body, grid=(s // tm,),
        in_specs=[rows(dn.shape[1]), rows(C_Q_RANK), rows(C_KV_RANK), rows(LANES), vec(C_Q_RANK), vec(C_KV_RANK)],
        out_specs=[rows(dn.shape[1]), vec(C_Q_RANK, 8), vec(C_KV_RANK, 8)],
        out_shape=[jax.ShapeDtypeStruct(dn.shape, BF), jax.ShapeDtypeStruct((8, C_Q_RANK), F32),
                   jax.ShapeDtypeStruct((8, C_KV_RANK), F32)],
        compiler_params=_cp("arbitrary"), name=name)(dn, dcq, dckv, dkr, gq, gkv)
    return dx, dgq[0:1], dgkv[0:1]


def _prep_c2_fwd(qraw, knope, dn, cos, sin, gq, gk, *, name):
    s = qraw.shape[0]
    tm = _tile(s, PREP_TILE, 8)
    kr_blk = dn.shape[1] // LANES - 1

    def body(q_ref, kn_ref, kr_ref, c_ref, s_ref, gq_ref, gk_ref, qo_ref, ko_ref):
        cv, sv, kr = c_ref[...], s_ref[...], kr_ref[...]
        for t in range(C_HEADS):
            qo_ref[:, t * LANES:(t + 1) * LANES] = _head_norm_rope(
                _slot(q_ref, t), gq_ref[...], cv, sv, C_QK).astype(BF)
            ko_ref[:, t * LANES:(t + 1) * LANES] = _head_norm_rope(
                _slot(kn_ref, t) + kr, gk_ref[...], cv, sv, C_QK).astype(BF)

    def rows(w, blk=0):
        return pl.BlockSpec((tm, w), lambda i: (i, blk))

    vec = pl.BlockSpec((1, LANES), lambda i: (0, 0))
    w = C_HEADS * LANES
    return pl.pallas_call(
        body, grid=(s // tm,),
        in_specs=[rows(w), rows(w), rows(LANES, kr_blk), rows(LANES), rows(LANES), vec, vec],
        out_specs=[rows(w), rows(w)],
        out_shape=[jax.ShapeDtypeStruct((s, w), BF)] * 2,
        compiler_params=_cp("parallel"), name=name)(qraw, knope, dn, cos, sin, gq, gk)


def _prep_c2_bwd(qraw, knope, dn, dq, dk, cos, sin, gq, gk, *, name):
    s = qraw.shape[0]
    tm = _tile(s, PREP_TILE, 8)
    kr_blk = dn.shape[1] // LANES - 1

    def body(q_ref, kn_ref, kr_ref, dq_ref, dk_ref, c_ref, s_ref, gq_ref, gk_ref,
             dqo_ref, dkno_ref, dkr_ref, dg_ref):
        @pl.when(pl.program_id(0) == 0)
        def _():
            dg_ref[...] = jnp.zeros_like(dg_ref)

        cv, sv, kr = c_ref[...], s_ref[...], kr_ref[...]
        dgq = jnp.zeros((1, LANES), F32)
        dgk = jnp.zeros((1, LANES), F32)
        dkr = jnp.zeros((tm, LANES), F32)
        for t in range(C_HEADS):
            dx, dg = _head_norm_rope_bwd(_slot(q_ref, t), gq_ref[...], _slot(dq_ref, t), cv, sv,
                                         C_QK)
            dqo_ref[:, t * LANES:(t + 1) * LANES] = dx.astype(BF)
            dgq = dgq + dg
            dx, dg = _head_norm_rope_bwd(_slot(kn_ref, t) + kr, gk_ref[...], _slot(dk_ref, t), cv, sv,
                                         C_QK)
            dkno_ref[:, t * LANES:(t + 1) * LANES] = dx.astype(BF)
            dkr = dkr + dx
            dgk = dgk + dg
        dkr_ref[...] = dkr
        dg_ref[0:1, :] += dgq
        dg_ref[1:2, :] += dgk

    def rows(w, blk=0):
        return pl.BlockSpec((tm, w), lambda i: (i, blk))

    vec = pl.BlockSpec((1, LANES), lambda i: (0, 0))
    w = C_HEADS * LANES
    dqo, dkno, dkr, dg = pl.pallas_call(
        body, grid=(s // tm,),
        in_specs=[rows(w), rows(w), rows(LANES, kr_blk), rows(w), rows(w), rows(LANES), rows(LANES), vec, vec],
        out_specs=[rows(w), rows(w), rows(LANES), pl.BlockSpec((8, LANES), lambda i: (0, 0))],
        out_shape=[jax.ShapeDtypeStruct((s, w), BF), jax.ShapeDtypeStruct((s, w), BF),
                   jax.ShapeDtypeStruct((s, LANES), F32), jax.ShapeDtypeStruct((8, LANES), F32)],
        compiler_params=_cp("arbitrary"), name=name)(qraw, knope, dn, dq, dk, cos, sin, gq, gk)
    return dqo, dkno, dkr, dg[0:1], dg[1:2]


def _lane_pick(mat, g):
    lane = lax.broadcasted_iota(jnp.int32, mat.shape, 1)
    return jnp.sum(jnp.where(lane == g, mat, 0.0), axis=-1, keepdims=True)


def _swa_fwd(q, k, v, sinks, *, name):
    s = q.shape[0]
    nb = min(SWA_BLOCKS, s // BLOCK)
    qt = nb * BLOCK
    scale = A_HEAD_DIM ** -0.5
    gw = A_GROUP * LANES

    def body(sink_ref, q_ref, kc_ref, kp_ref, vc_ref, vp_ref, o_ref, lse_ref):
        kvh, n = pl.program_id(0), pl.program_id(1)
        rows = A_GROUP * BLOCK
        kall = jnp.concatenate([kp_ref[...], kc_ref[...]], axis=0)
        vall = jnp.concatenate([vp_ref[...], vc_ref[...]], axis=0)
        r = lax.broadcasted_iota(jnp.int32, (rows, 2 * BLOCK), 0) & (BLOCK - 1)
        c = lax.broadcasted_iota(jnp.int32, (rows, 2 * BLOCK), 1)
        cur_ok = (c >= BLOCK) & (c - BLOCK <= r)
        prev_ok = (c < BLOCK) & (c > r)
        head = lax.broadcasted_iota(jnp.int32, (rows, 1), 0) >> 7
        sink = jnp.zeros((rows, 1), F32)
        for g in range(A_GROUP):
            sink = jnp.where(head == g, sink_ref[kvh * A_GROUP + g], sink)
        lane = lax.broadcasted_iota(jnp.int32, (BLOCK, LANES), 1)
        for b in range(nb):
            blk = slice(b * BLOCK, (b + 1) * BLOCK)
            qs = jnp.concatenate([q_ref[blk, g * LANES:(g + 1) * LANES] for g in range(A_GROUP)], axis=0)
            k2 = kall[b * BLOCK:(b + 2) * BLOCK]
            v2 = vall[b * BLOCK:(b + 2) * BLOCK]
            ok = (cur_ok | prev_ok) if b > 0 else (cur_ok | (prev_ok & (n > 0)))
            sc = jnp.where(ok, _dot_nt(qs, k2) * scale, NEG)
            m = jnp.maximum(jnp.max(sc, axis=-1, keepdims=True), sink)
            p = jnp.exp(sc - m)
            l = jnp.sum(p, axis=-1, keepdims=True) + jnp.exp(sink - m)
            o = _dot((p * (1.0 / l)).astype(BF), v2)
            lse = m + jnp.log(l)
            lse_mat = jnp.zeros((BLOCK, LANES), F32)
            for g in range(A_GROUP):
                o_ref[blk, g * LANES:(g + 1) * LANES] = o[g * BLOCK:(g + 1) * BLOCK].astype(BF)
                lse_mat = jnp.where(lane == g, lse[g * BLOCK:(g + 1) * BLOCK], lse_mat)
            lse_ref[0, blk, :] = lse_mat

    cur = pl.BlockSpec((qt, LANES), lambda h, n: (n, h))
    prev = pl.BlockSpec((BLOCK, LANES), lambda h, n: (jnp.maximum(n * nb - 1, 0), h))
    return pl.pallas_call(
        body, grid=(A_KV_HEADS, s // qt),
        in_specs=[pl.BlockSpec(memory_space=pltpu.SMEM), pl.BlockSpec((qt, gw), lambda h, n: (n, h)),
                  cur, prev, cur, prev],
        out_specs=[pl.BlockSpec((qt, gw), lambda h, n: (n, h)),
                   pl.BlockSpec((1, qt, LANES), lambda h, n: (h, n, 0))],
        out_shape=[jax.ShapeDtypeStruct(q.shape, BF), jax.ShapeDtypeStruct((A_KV_HEADS, s, LANES), F32)],
        compiler_params=_cp("parallel", "parallel"), name=name)(sinks, q, k, k, v, v)


def _swa_bwd(q, k, v, o, do, lse, sinks, *, name):
    s = q.shape[0]
    nblk = s // BLOCK
    nb = min(SWA_BLOCKS, nblk)
    qt = nb * BLOCK
    nsteps = s // qt
    scale = A_HEAD_DIM ** -0.5
    gw = A_GROUP * LANES

    def body(sink_ref, qc_ref, qn_ref, k_ref, v_ref, oc_ref, on_ref, doc_ref, don_ref, lc_ref, ln_ref,
             dq_ref, dk_ref, dv_ref, dsink_ref, carry):
        kvh, n = pl.program_id(0), pl.program_id(1)

        @pl.when(n == 0)
        def _():
            carry[...] = jnp.zeros_like(carry)
            dsink_ref[...] = jnp.zeros_like(dsink_ref)

        half = A_GROUP * BLOCK
        rows = 2 * half
        qall = jnp.concatenate([qc_ref[...], qn_ref[...]], axis=0)
        oall = jnp.concatenate([oc_ref[...], on_ref[...]], axis=0)
        doall = jnp.concatenate([doc_ref[...], don_ref[...]], axis=0)
        lall = jnp.concatenate([lc_ref[0], ln_ref[0]], axis=0)
        row = lax.broadcasted_iota(jnp.int32, (rows, BLOCK), 0)
        r = row & (BLOCK - 1)
        c = lax.broadcasted_iota(jnp.int32, (rows, BLOCK), 1)
        diag_ok = (row < half) & (c <= r)
        next_ok = (row >= half) & (c > r)
        lane8 = lax.broadcasted_iota(jnp.int32, (8, LANES), 1)
        dsink = jnp.zeros((8, LANES), F32)
        off_prev = None
        for b in range(nb):
            blk = slice(b * BLOCK, (b + 1) * BLOCK)
            kv, vv = k_ref[blk, :], v_ref[blk, :]

            def stack(allv):
                return jnp.concatenate(
                    [allv[b * BLOCK:(b + 1) * BLOCK, g * LANES:(g + 1) * LANES] for g in range(A_GROUP)]
                    + [allv[(b + 1) * BLOCK:(b + 2) * BLOCK, g * LANES:(g + 1) * LANES] for g in range(A_GROUP)],
                    axis=0)

            qs, osk, dos = stack(qall), stack(oall), stack(doall)
            lse = jnp.concatenate([_lane_pick(lall[b * BLOCK:(b + 1) * BLOCK], g) for g in range(A_GROUP)]
                                  + [_lane_pick(lall[(b + 1) * BLOCK:(b + 2) * BLOCK], g) for g in range(A_GROUP)],
                                  axis=0)
            delta = jnp.sum(dos.astype(F32) * osk.astype(F32), axis=-1, keepdims=True)
            ok = (diag_ok | next_ok) if b < nb - 1 else (diag_ok | (next_ok & (n < nsteps - 1)))
            sc = jnp.where(ok, _dot_nt(qs, kv) * scale, NEG)
            p = jnp.exp(sc - lse)
            dv_ref[blk, :] = _dot_tn(p.astype(BF), dos)
            dp = _dot_nt(dos, vv)
            ds = (p * (dp - delta) * scale).astype(BF)
            dk_ref[blk, :] = _dot_tn(ds, qs)
            dqs = _dot(ds, kv)
            for g in range(A_GROUP):
                cur = slice(g * BLOCK, (g + 1) * BLOCK)
                before = carry[:, g * LANES:(g + 1) * LANES] if b == 0 else off_prev[cur]
                dq_ref[blk, g * LANES:(g + 1) * LANES] = before + dqs[cur]
                p_sink = jnp.exp(sink_ref[kvh * A_GROUP + g] - lse[cur])
                dsink = dsink + jnp.where(lane8 == g, -jnp.sum(p_sink * delta[cur]), 0.0)
            off_prev = dqs[half:]
        for g in range(A_GROUP):
            carry[:, g * LANES:(g + 1) * LANES] = off_prev[g * BLOCK:(g + 1) * BLOCK]
        dsink_ref[...] += dsink

    def nxt(n):
        return jnp.minimum((n + 1) * nb, nblk - 1)

    grp_c = pl.BlockSpec((qt, gw), lambda h, n: (n, h))
    grp_n = pl.BlockSpec((BLOCK, gw), lambda h, n: (nxt(n), h))
    kvb = pl.BlockSpec((qt, LANES), lambda h, n: (n, h))
    lse_c = pl.BlockSpec((1, qt, LANES), lambda h, n: (h, n, 0))
    lse_n = pl.BlockSpec((1, BLOCK, LANES), lambda h, n: (h, nxt(n), 0))
    dq, dk, dv, dsink = pl.pallas_call(
        body, grid=(A_KV_HEADS, nsteps),
        in_specs=[pl.BlockSpec(memory_space=pltpu.SMEM), grp_c, grp_n, kvb, kvb, grp_c, grp_n, grp_c, grp_n,
                  lse_c, lse_n],
        out_specs=[grp_c, kvb, kvb, pl.BlockSpec((8, LANES), lambda h, n: (h, 0))],
        out_shape=[jax.ShapeDtypeStruct(q.shape, F32), jax.ShapeDtypeStruct(k.shape, F32),
                   jax.ShapeDtypeStruct(v.shape, F32), jax.ShapeDtypeStruct((A_KV_HEADS * 8, LANES), F32)],
        scratch_shapes=[pltpu.VMEM((BLOCK, gw), F32)],
        compiler_params=_cp("parallel", "arbitrary"), name=name)(sinks, q, q, k, v, o, o, do, do, lse, lse)
    dsinks = dsink.reshape(A_KV_HEADS, 8, LANES)[:, 0, :A_GROUP].reshape(A_HEADS)
    return dq, dk, dv, dsinks


LOG2E = 1.4426950408889634


def _mla_fwd(q, k, v, *, name):
    s = q.shape[0]
    t = _tile(s, ATTN_TILE, LANES)
    nt = s // t
    scale = C_QK ** -0.5
    c2 = scale * LOG2E

    def body(q_ref, k_ref, v_ref, o_ref, lse_ref, m_sc, acc_sc):
        qi = pl.program_id(1)
        qv = q_ref[...]
        m_sc[...] = jnp.full_like(m_sc, NEG)
        acc_sc[...] = jnp.zeros_like(acc_sc)

        def step(start, width, diag):
            rows = pl.ds(pl.multiple_of(start, t), width)
            sc = _dot_nt(qv, k_ref[rows, :])
            if diag:
                r = lax.broadcasted_iota(jnp.int32, (t, t), 0)
                c = lax.broadcasted_iota(jnp.int32, (t, t), 1)
                own = jnp.where(c <= r, sc[:, width - t:], NEG)
                sc = own if width == t else jnp.concatenate([sc[:, :width - t], own], axis=1)
            m_prev = m_sc[...]
            m_new = jnp.maximum(m_prev, jnp.max(sc, axis=-1, keepdims=True))
            alpha = jnp.exp2((m_prev - m_new) * c2)
            p = jnp.exp2((sc - jnp.tile(m_new, (1, width // LANES))) * c2).astype(BF)
            lane = lax.broadcasted_iota(jnp.int32, (width, LANES), 1)
            vv = jnp.where(lane == C_V, jnp.ones((), BF), v_ref[rows, :])
            acc_sc[...] = alpha * acc_sc[...] + _dot(p, vv)
            m_sc[...] = m_new

        def wide_body(wi, carry):
            step(wi * (FWD_WIDE * t), FWD_WIDE * t, False)
            return carry

        def single_body(j, carry):
            step(j * t, t, False)
            return carry

        with_diag = qi >= FWD_WIDE - 1

        @pl.when(with_diag)
        def _():
            step((qi - (FWD_WIDE - 1)) * t, FWD_WIDE * t, True)

        @pl.when(jnp.logical_not(with_diag))
        def _():
            step(qi * t, t, True)

        rest = jnp.where(with_diag, qi - (FWD_WIDE - 1), qi)
        n_wide = rest // FWD_WIDE
        lax.fori_loop(0, n_wide, wide_body, 0)
        lax.fori_loop(n_wide * FWD_WIDE, rest, single_body, 0)
        acc = acc_sc[...]
        l = _lane_pick(acc, C_V)
        lane = lax.broadcasted_iota(jnp.int32, (t, LANES), 1)
        o_ref[...] = jnp.where(lane == C_V, 0.0, acc * (1.0 / l)).astype(BF)
        lse2 = m_sc[...] * c2 + jnp.log(l) * LOG2E
        lse_ref[0, 0] = jnp.transpose(lse2)[0:8, :]

    qspec = pl.BlockSpec((t, LANES), lambda h, i: (i, h))
    kspec = pl.BlockSpec((s, LANES), lambda h, i: (0, h))
    return pl.pallas_call(
        body, grid=(C_HEADS, nt), in_specs=[qspec, kspec, kspec],
        out_specs=[qspec, pl.BlockSpec((1, 1, 8, t), lambda h, i: (h, i, 0, 0))],
        out_shape=[jax.ShapeDtypeStruct(q.shape, BF), jax.ShapeDtypeStruct((C_HEADS, nt, 8, t), F32)],
        scratch_shapes=[pltpu.VMEM((t, LANES), F32)] * 2,
        compiler_params=_cp("parallel", "parallel"), name=name)(q, k, v)


def _mla_delta(o, do, *, name):
    s = o.shape[0]
    t = _tile(s, ATTN_TILE, LANES)
    nt = s // t

    def body(o_ref, do_ref, d_ref):
        for h in range(C_HEADS):
            prod = jnp.transpose(_slot(o_ref, h).astype(F32) * _slot(do_ref, h).astype(F32))
            d_ref[h, 0] = jnp.broadcast_to(jnp.sum(prod, axis=0, keepdims=True), (8, t))

    blk = pl.BlockSpec((t, C_HEADS * LANES), lambda i: (i, 0))
    return pl.pallas_call(
        body, grid=(nt,), in_specs=[blk, blk],
        out_specs=pl.BlockSpec((C_HEADS, 1, 8, t), lambda i: (0, i, 0, 0)),
        out_shape=jax.ShapeDtypeStruct((C_HEADS, nt, 8, t), F32),
        compiler_params=_cp("parallel"), name=name)(o, do)


def _mla_bwd(q, k, v, do, lse2, delta, *, name):
    s = q.shape[0]
    t = _tile(s, ATTN_TILE, LANES)
    nt = s // t
    scale = C_QK ** -0.5
    c2 = scale * LOG2E

    def body(q_ref, do_ref, k_ref, v_ref, lse_ref, dl_ref, dq_ref, dk_ref, dv_ref):
        kj = pl.program_id(1)

        @pl.when(kj == 0)
        def _():
            dq_ref[...] = jnp.zeros_like(dq_ref)

        dk_ref[...] = jnp.zeros_like(dk_ref)
        dv_ref[...] = jnp.zeros_like(dv_ref)
        kv, vv = k_ref[...], v_ref[...]

        def step(i, n, diag):
            rows = pl.ds(pl.multiple_of(i * t, t), n * t)
            qv, dov = q_ref[rows, :], do_ref[rows, :]
            st = _dot_nt(kv, qv)
            if diag:
                r = lax.broadcasted_iota(jnp.int32, (t, t), 0)
                c = lax.broadcasted_iota(jnp.int32, (t, t), 1)
                own = jnp.where(r <= c, st[:, :t], NEG)
                st = own if n == 1 else jnp.concatenate([own, st[:, t:]], axis=1)
            lse = jnp.concatenate([lse_ref[0, i + u, 0:1, :] for u in range(n)], axis=1)
            dl = jnp.concatenate([dl_ref[0, i + u, 0:1, :] for u in range(n)], axis=1)
            pt = jnp.exp2(st * c2 - lse)
            dv_ref[...] += _dot(pt.astype(BF), dov)
            dpt = _dot_nt(vv, dov)
            dst = (pt * (dpt - dl) * scale).astype(BF)
            dk_ref[...] += _dot(dst, qv)
            dq_ref[rows, :] += _dot_tn(dst, kv)

        odd = (nt - kj) % 2
        first = kj + 2 - odd

        @pl.when(odd == 0)
        def _():
            step(kj, 2, True)

        @pl.when(odd == 1)
        def _():
            step(kj, 1, True)

        def pair_body(pi, carry):
            step(first + 2 * pi, 2, False)
            return carry

        lax.fori_loop(0, (nt - first) // 2, pair_body, 0)

    res = pl.BlockSpec((s, LANES), lambda h, j: (0, h))
    kspec = pl.BlockSpec((t, LANES), lambda h, j: (j, h))
    stat = pl.BlockSpec((1, nt, 8, t), lambda h, j: (h, 0, 0, 0))
    return pl.pallas_call(
        body, grid=(C_HEADS, nt), in_specs=[res, res, kspec, kspec, stat, stat],
        out_specs=[res, kspec, kspec],
        out_shape=[jax.ShapeDtypeStruct(q.shape, F32)] * 3,
        compiler_params=_cp("parallel", "arbitrary"), name=name)(q, do, k, v, lse2, delta)


def _conv_shifted(z, zprev, tm):
    row = lax.broadcasted_iota(jnp.int32, z.shape, 0)
    z1 = jnp.where(row == 0, zprev[7:8], pltpu.roll(z, 1, 0))
    z2 = jnp.where(row == 0, zprev[6:7], jnp.where(row == 1, zprev[7:8], pltpu.roll(z, 2, 0)))
    return z1, z2


def _conv_fwd(proj, w, *, name):
    s = proj.shape[0]
    d = D_MODEL
    tm = _tile(s, PREP_TILE, 8)

    def body(x_ref, xp_ref, w_ref, y_ref):
        i = pl.program_id(0)
        z = x_ref[:, d:2 * d] * x_ref[:, 2 * d:]
        zprev = jnp.where(i > 0, xp_ref[:, d:2 * d] * xp_ref[:, 2 * d:], 0.0)
        z1, z2 = _conv_shifted(z, zprev, tm)
        y = w_ref[0:1, :] * z2 + w_ref[1:2, :] * z1 + w_ref[2:3, :] * z
        y_ref[...] = (x_ref[:, :d] * y).astype(BF)

    per8 = tm // 8
    return pl.pallas_call(
        body, grid=(s // tm,),
        in_specs=[pl.BlockSpec((tm, 3 * d), lambda i: (i, 0)),
                  pl.BlockSpec((8, 3 * d), lambda i: (jnp.maximum(i * per8 - 1, 0), 0)),
                  pl.BlockSpec((8, d), lambda i: (0, 0))],
        out_specs=pl.BlockSpec((tm, d), lambda i: (i, 0)),
        out_shape=jax.ShapeDtypeStruct((s, d), BF),
        compiler_params=_cp("parallel"), name=name)(proj, proj, w)


def _conv_bwd(proj, dyb, w, *, name):
    s = proj.shape[0]
    d = D_MODEL
    tm = _tile(s, PREP_TILE, 8)
    nt = s // tm

    def body(x_ref, xp_ref, xn_ref, dy_ref, dyn_ref, w_ref, dx_ref, dw_ref):
        i = pl.program_id(0)

        @pl.when(i == 0)
        def _():
            dw_ref[...] = jnp.zeros_like(dw_ref)

        b, c, u = x_ref[:, :d], x_ref[:, d:2 * d], x_ref[:, 2 * d:]
        z = c * u
        zprev = jnp.where(i > 0, xp_ref[:, d:2 * d] * xp_ref[:, 2 * d:], 0.0)
        z1, z2 = _conv_shifted(z, zprev, tm)
        w0, w1, w2 = w_ref[0:1, :], w_ref[1:2, :], w_ref[2:3, :]
        y = w0 * z2 + w1 * z1 + w2 * z
        dyb_v = dy_ref[...]
        dyc = dyb_v * b
        dyn = jnp.where(i < nt - 1, dyn_ref[...] * xn_ref[:, :d], 0.0)
        row = lax.broadcasted_iota(jnp.int32, dyc.shape, 0)
        d1 = jnp.where(row == tm - 1, dyn[0:1], pltpu.roll(dyc, tm - 1, 0))
        d2 = jnp.where(row == tm - 1, dyn[1:2], jnp.where(row == tm - 2, dyn[0:1], pltpu.roll(dyc, tm - 2, 0)))
        dz = w2 * dyc + w1 * d1 + w0 * d2
        dx_ref[:, :d] = (dyb_v * y).astype(BF)
        dx_ref[:, d:2 * d] = (dz * u).astype(BF)
        dx_ref[:, 2 * d:] = (dz * c).astype(BF)
        dw_ref[0:1, :] += jnp.sum(dyc * z2, axis=0, keepdims=True)
        dw_ref[1:2, :] += jnp.sum(dyc * z1, axis=0, keepdims=True)
        dw_ref[2:3, :] += jnp.sum(dyc * z, axis=0, keepdims=True)

    per8 = tm // 8
    last8 = s // 8 - 1
    dx, dw = pl.pallas_call(
        body, grid=(nt,),
        in_specs=[pl.BlockSpec((tm, 3 * d), lambda i: (i, 0)),
                  pl.BlockSpec((8, 3 * d), lambda i: (jnp.maximum(i * per8 - 1, 0), 0)),
                  pl.BlockSpec((8, 3 * d), lambda i: (jnp.minimum((i + 1) * per8, last8), 0)),
                  pl.BlockSpec((tm, d), lambda i: (i, 0)),
                  pl.BlockSpec((8, d), lambda i: (jnp.minimum((i + 1) * per8, last8), 0)),
                  pl.BlockSpec((8, d), lambda i: (0, 0))],
        out_specs=[pl.BlockSpec((tm, 3 * d), lambda i: (i, 0)), pl.BlockSpec((8, d), lambda i: (0, 0))],
        out_shape=[jax.ShapeDtypeStruct((s, 3 * d), BF), jax.ShapeDtypeStruct((8, d), F32)],
        compiler_params=_cp("arbitrary"), name=name)(proj, proj, proj, dyb, dyb, w)
    return dx, dw[0:3]


def _ffn_up(h, wgu, *, name):
    s, d = h.shape
    f = wgu.shape[1] // 2
    tm = _tile(s, ROW_TILE, 8)
    tn = _tile(f, 1408, 128)
    nb = f // tn

    def body(h_ref, wg_ref, wu_ref, g_ref, u_ref, a_ref):
        hv = h_ref[...]
        g = _dot(hv, wg_ref[...])
        u = _dot(hv, wu_ref[...])
        g_ref[...] = g.astype(BF)
        u_ref[...] = u.astype(BF)
        a_ref[...] = (g * jax.nn.sigmoid(g) * u).astype(BF)

    out = pl.BlockSpec((tm, tn), lambda j, i: (i, j))
    return pl.pallas_call(
        body, grid=(nb, s // tm),
        in_specs=[pl.BlockSpec((tm, d), lambda j, i: (i, 0)),
                  pl.BlockSpec((d, tn), lambda j, i: (0, j)),
                  pl.BlockSpec((d, tn), lambda j, i: (0, j + nb))],
        out_specs=[out] * 3, out_shape=[jax.ShapeDtypeStruct((s, f), BF)] * 3,
        compiler_params=_cp("parallel", "parallel"), name=name)(h, wgu, wgu)


def _ffn_bwd_mid(dy, wd, gate, up, *, name):
    s, d = dy.shape
    f = wd.shape[0]
    tm = _tile(s, ROW_TILE, 8)
    tn = _tile(f, 1408, 128)

    def body(dy_ref, wd_ref, g_ref, u_ref, dg_ref, du_ref):
        dyv = dy_ref[...].astype(BF)
        for c0 in range(0, tn, 2 * LANES):
            cols = slice(c0, min(c0 + 2 * LANES, tn))
            dact = _dot_nt(dyv, wd_ref[cols, :])
            g = g_ref[:, cols].astype(F32)
            u = u_ref[:, cols].astype(F32)
            sig = jax.nn.sigmoid(g)
            silu = g * sig
            dg_ref[:, cols] = (dact * u * (sig + silu * (1.0 - sig))).astype(BF)
            du_ref[:, cols] = (dact * silu).astype(BF)

    blk = pl.BlockSpec((tm, tn), lambda j, i: (i, j))
    return pl.pallas_call(
        body, grid=(f // tn, s // tm),
        in_specs=[pl.BlockSpec((tm, d), lambda j, i: (i, 0)), pl.BlockSpec((tn, d), lambda j, i: (j, 0)), blk, blk],
        out_specs=[blk, blk], out_shape=[jax.ShapeDtypeStruct((s, f), BF)] * 2,
        compiler_params=_cp("parallel", "parallel"), name=name)(dy, wd, gate, up)


def _pad_cols(w, heads, dim, layout):
    k = w.shape[0]
    w3 = w.reshape(k, heads, dim)
    pieces, lane = [], 0
    for lane0, dim0, cnt in sorted(layout):
        if lane0 > lane:
            pieces.append(jnp.zeros((k, heads, lane0 - lane), w.dtype))
        pieces.append(w3[:, :, dim0:dim0 + cnt])
        lane = lane0 + cnt
    if lane < LANES:
        pieces.append(jnp.zeros((k, heads, LANES - lane), w.dtype))
    return jnp.concatenate(pieces, axis=2).reshape(k, heads * LANES)


def _unpad_cols(w, heads, dim, layout):
    k = w.shape[0]
    w3 = w.reshape(k, heads, LANES)
    by_dim = sorted(layout, key=lambda seg: seg[1])
    assert sum(cnt for _, _, cnt in by_dim) == dim
    return jnp.concatenate([w3[:, :, lane0:lane0 + cnt] for lane0, _, cnt in by_dim], axis=2).reshape(k, heads * dim)


def _pad_rows(w, heads, dim):
    n = w.shape[1]
    return jnp.pad(w.reshape(heads, dim, n), ((0, 0), (0, LANES - dim), (0, 0))).reshape(heads * LANES, n)


def _unpad_rows(w, heads, dim):
    n = w.shape[1]
    return w.reshape(heads, LANES, n)[:, :dim, :].reshape(heads * dim, n)


def _pad_vec(g, layout):
    return _pad_cols(g.reshape(1, -1), 1, g.shape[0], layout)


def _unpad_vec(g, dim, layout):
    return _unpad_cols(g.reshape(1, LANES), 1, dim, layout)[0]


def _local_step(x, positions, w, target):
    s = x.shape[0]
    cos_a, sin_a, cos_c, sin_c = _rope_tables(positions.reshape(s, 1), name="rope_tables")
    grads = {n: [None] * len(w[n]) for n in WEIGHTS}
    saved = []

    for i in range(DEPTH):
        kind, j = i % N_MIXERS, i // N_MIXERS
        tag = f"l{i}"
        if i == 0:
            h = _rmsnorm_fwd(x, w['mix_norm'][i], name=f"{tag}_mix_norm")
        fnorm = w['ffn_norm'][i]
        if kind == 0:
            nqk = (A_HEADS + A_KV_HEADS) * A_HEAD_DIM
            wqkv = jnp.concatenate(
                [_pad_cols(w['a_w_qkv'][j][:, :nqk], A_HEADS + A_KV_HEADS, A_HEAD_DIM, LAYOUT_A),
                 _pad_cols(w['a_w_qkv'][j][:, nqk:], A_KV_HEADS, A_HEAD_DIM, LAYOUT_V)], axis=1)
            wo = _pad_rows(w['a_w_o'][j], A_HEADS, A_HEAD_DIM)
            gq, gk = _pad_vec(w['a_q_norm'][j], LAYOUT_A), _pad_vec(w['a_k_norm'][j], LAYOUT_A)
            qkv = _mm([(h, wqkv, 0)], out_dtype=F32, name=f"{tag}_qkv")
            qa, ka, va = _prep_a_fwd(qkv, cos_a, sin_a, gq, gk, name=f"{tag}_prep")
            o, lse = _swa_fwd(qa, ka, va, w['a_sinks'][j], name=f"{tag}_attn")
            x1, h2 = _mm([(o, wo, 0)], out_dtype=F32, res=x, norm_g=fnorm, name=f"{tag}_wo")
            mix = dict(wqkv=wqkv, wo=wo, gq=gq, gk=gk, qkv=qkv, qa=qa, ka=ka, va=va, o=o, lse=lse)
        elif kind == 1:
            cw = jnp.pad(w['b_conv_w'][j], ((0, 5), (0, 0)))
            proj = _mm([(h, w['b_w_in'][j], 0)], out_dtype=F32, name=f"{tag}_win")
            yb = _conv_fwd(proj, cw, name=f"{tag}_conv")
            x1, h2 = _mm([(yb, w['b_w_out'][j], 0)], out_dtype=F32, res=x, norm_g=fnorm, name=f"{tag}_wout")
            mix = dict(cw=cw, proj=proj, yb=yb)
        else:
            wdn = w['c_w_down'][j]
            nqk = C_Q_RANK + C_KV_RANK
            wdn = jnp.concatenate([wdn[:, :nqk], _pad_cols(wdn[:, nqk:], 1, C_ROPE, LAYOUT_KR)], axis=1)
            wq = _pad_cols(w['c_w_q_up'][j], C_HEADS, C_QK, LAYOUT_C)
            wkv = w['c_w_kv_up'][j].reshape(C_KV_RANK, C_HEADS, C_NOPE + C_V)
            wkn = _pad_cols(wkv[:, :, :C_NOPE].reshape(C_KV_RANK, -1), C_HEADS, C_NOPE, LAYOUT_KN)
            wv = _pad_cols(wkv[:, :, C_NOPE:].reshape(C_KV_RANK, -1), C_HEADS, C_V, LAYOUT_V)
            wo = _pad_rows(w['c_w_o'][j], C_HEADS, C_V)
            gq, gk = _pad_vec(w['c_q_norm'][j], LAYOUT_C), _pad_vec(w['c_k_norm'][j], LAYOUT_C)
            gqa, gkva = w['c_q_a_norm'][j].reshape(1, -1), w['c_kv_a_norm'][j].reshape(1, -1)
            dn = _mm([(h, wdn, 0)], out_dtype=F32, name=f"{tag}_wdown")
            cqn, ckvn = _prep_c1_fwd(dn, gqa, gkva, name=f"{tag}_prep1")
            qraw = _mm([(cqn, wq, 0)], out_dtype=F32, name=f"{tag}_wq")
            knope = _mm([(ckvn, wkn, 0)], out_dtype=F32, name=f"{tag}_wkn")
            vc = _mm([(ckvn, wv, 0)], out_dtype=BF, name=f"{tag}_wv")
            qc, kc = _prep_c2_fwd(qraw, knope, dn, cos_c, sin_c, gq, gk, name=f"{tag}_prep2")
            o, lse = _mla_fwd(qc, kc, vc, name=f"{tag}_attn")
            x1, h2 = _mm([(o, wo, 0)], out_dtype=F32, res=x, norm_g=fnorm, name=f"{tag}_wo")
            mix = dict(wdn=wdn, wq=wq, wkn=wkn, wv=wv, wo=wo, gq=gq, gk=gk, gqa=gqa, gkva=gkva, dn=dn, cqn=cqn,
                       ckvn=ckvn, qraw=qraw, knope=knope, vc=vc, qc=qc, kc=kc, o=o, lse=lse)
        gate, up, act = _ffn_up(h2, w['f_w_gate_up'][i], name=f"{tag}_ffn_up")
        saved.append(dict(x=x, h=h, x1=x1, h2=h2, gate=gate, up=up, act=act, mix=mix))
        if i + 1 < DEPTH:
            x, h = _mm([(act, w['f_w_down'][i], 0)], out_dtype=F32, res=x1, norm_g=w['mix_norm'][i + 1],
                       name=f"{tag}_ffn_down")
        else:
            x = _mm([(act, w['f_w_down'][i], 0)], out_dtype=F32, res=x1, name=f"{tag}_ffn_down")

    loss_blk, dx = _loss_fwd_bwd(x, target, name="loss")

    for i in reversed(range(DEPTH)):
        kind, j = i % N_MIXERS, i // N_MIXERS
        tag = f"l{i}b"
        sv = saved[i]
        mix = sv['mix']
        wgu, wd = w['f_w_gate_up'][i], w['f_w_down'][i]
        grads['f_w_down'][i] = _mm_tn(sv['act'], dx, name=f"{tag}_dwd")
        dgate, dup = _ffn_bwd_mid(dx, wd, sv['gate'], sv['up'], name=f"{tag}_ffn_mid")
        grads['f_w_gate_up'][i] = (_mm_tn(sv['h2'], dgate, name=f"{tag}_dwg"),
                                   _mm_tn(sv['h2'], dup, name=f"{tag}_dwu"))
        dx, dg = _mm_norm_bwd([(dgate, wgu, 0), (dup, wgu, 1)], sv['x1'], w['ffn_norm'][i], dx, name=f"{tag}_dh2")
        grads['ffn_norm'][i] = dg[0]
        if kind == 0:
            grads['a_w_o'][j] = _unpad_rows(_mm_tn(mix['o'], dx, name=f"{tag}_dwo"), A_HEADS, A_HEAD_DIM)
            do = _mm([(dx, mix['wo'], 0)], out_dtype=BF, trans_b=True, name=f"{tag}_do")
            dqa, dka, dva, dsinks = _swa_bwd(mix['qa'], mix['ka'], mix['va'], mix['o'], do, mix['lse'],
                                             w['a_sinks'][j], name=f"{tag}_attn")
            dqkv, dgq, dgk = _prep_a_bwd(mix['qkv'], dqa, dka, dva, cos_a, sin_a, mix['gq'], mix['gk'],
                                         name=f"{tag}_prep")
            grads['a_sinks'][j] = dsinks
            grads['a_q_norm'][j] = _unpad_vec(dgq, A_HEAD_DIM, LAYOUT_A)
            grads['a_k_norm'][j] = _unpad_vec(dgk, A_HEAD_DIM, LAYOUT_A)
            dwqkv = _mm_tn(sv['h'], dqkv, name=f"{tag}_dwqkv")
            nqk = (A_HEADS + A_KV_HEADS) * LANES
            grads['a_w_qkv'][j] = jnp.concatenate(
                [_unpad_cols(dwqkv[:, :nqk], A_HEADS + A_KV_HEADS, A_HEAD_DIM, LAYOUT_A),
                 _unpad_cols(dwqkv[:, nqk:], A_KV_HEADS, A_HEAD_DIM, LAYOUT_V)], axis=1)
            dh_pairs = [(dqkv, mix['wqkv'], 0)]
        elif kind == 1:
            grads['b_w_out'][j] = _mm_tn(mix['yb'], dx, name=f"{tag}_dwout")
            dyb = _mm([(dx, w['b_w_out'][j], 0)], out_dtype=F32, trans_b=True, name=f"{tag}_dyb")
            dproj, dcw = _conv_bwd(mix['proj'], dyb, mix['cw'], name=f"{tag}_conv")
            grads['b_conv_w'][j] = dcw
            grads['b_w_in'][j] = _mm_tn(sv['h'], dproj, name=f"{tag}_dwin")
            dh_pairs = [(dproj, w['b_w_in'][j], 0)]
        else:
            grads['c_w_o'][j] = _unpad_rows(_mm_tn(mix['o'], dx, name=f"{tag}_dwo"), C_HEADS, C_V)
            do = _mm([(dx, mix['wo'], 0)], out_dtype=BF, trans_b=True, name=f"{tag}_do")
            delta = _mla_delta(mix['o'], do, name=f"{tag}_delta")
            dqc, dkc, dvc = _mla_bwd(mix['qc'], mix['kc'], mix['vc'], do, mix['lse'], delta, name=f"{tag}_attn")
            dqraw, dknope, dkr, dgq, dgk = _prep_c2_bwd(mix['qraw'], mix['knope'], mix['dn'], dqc, dkc, cos_c, sin_c,
                                                        mix['gq'], mix['gk'], name=f"{tag}_prep2")
            grads['c_q_norm'][j] = _unpad_vec(dgq, C_QK, LAYOUT_C)
            grads['c_k_norm'][j] = _unpad_vec(dgk, C_QK, LAYOUT_C)
            grads['c_w_q_up'][j] = _unpad_cols(_mm_tn(mix['cqn'], dqraw, name=f"{tag}_dwq"), C_HEADS, C_QK,
                                               LAYOUT_C)
            dwkn = _unpad_cols(_mm_tn(mix['ckvn'], dknope, name=f"{tag}_dwkn"), C_HEADS, C_NOPE, LAYOUT_KN)
            dwv = _unpad_cols(_mm_tn(mix['ckvn'], dvc, name=f"{tag}_dwv"), C_HEADS, C_V, LAYOUT_V)
            grads['c_w_kv_up'][j] = jnp.concatenate(
                [dwkn.reshape(C_KV_RANK, C_HEADS, C_NOPE), dwv.reshape(C_KV_RANK, C_HEADS, C_V)], axis=2).reshape(
                C_KV_RANK, -1)
            dcq = _mm([(dqraw, mix['wq'], 0)], out_dtype=F32, trans_b=True, name=f"{tag}_dcq")
            dckv = _mm([(dknope, mix['wkn'], 0), (dvc, mix['wv'], 0)], out_dtype=F32, trans_b=True,
                       name=f"{tag}_dckv")
            ddn, dgqa, dgkva = _prep_c1_bwd(mix['dn'], dcq, dckv, dkr, mix['gqa'], mix['gkva'], name=f"{tag}_prep1")
            grads['c_q_a_norm'][j] = dgqa[0]
            grads['c_kv_a_norm'][j] = dgkva[0]
            dwdn = _mm_tn(sv['h'], ddn, name=f"{tag}_dwdown")
            nqk = C_Q_RANK + C_KV_RANK
            grads['c_w_down'][j] = jnp.concatenate(
                [dwdn[:, :nqk], _unpad_cols(dwdn[:, nqk:], 1, C_ROPE, LAYOUT_KR)], axis=1)
            dh_pairs = [(ddn, mix['wdn'], 0)]
        dx, dg = _mm_norm_bwd(dh_pairs, sv['x'], w['mix_norm'][i], dx, name=f"{tag}_dh")
        grads['mix_norm'][i] = dg[0]

    return loss_blk, dx, {n: (g if n in BIG else jnp.stack(g)) for n, g in grads.items()}


def _my_place():
    return lax.axis_index("x"), lax.axis_index("y"), lax.axis_index("c")


def _flips(x, y):
    return [(1 - x, y), (x, 1 - y), (1 - x, 1 - y)]


def _gather_weights(big, small):
    half = big.shape[0] // 2

    def body(big_ref, small_ref, bout_ref, sout_ref, send_b, recv_b, send_f, recv_f, send_s, recv_s, loc):
        x, y, c = _my_place()
        me = 2 * x + y
        mine = pl.ds(pl.multiple_of(c * half, 16), half)
        theirs = pl.ds(pl.multiple_of((1 - c) * half, 16), half)

        def ici(k, chip, to):
            return pltpu.make_async_remote_copy(src_ref=big_ref.at[mine, :], dst_ref=bout_ref.at[chip, mine, :],
                                                send_sem=send_b.at[k], recv_sem=recv_b.at[k],
                                                device_id=to, device_id_type=MESH)

        def forward(k, chip, rows):
            return pltpu.make_async_remote_copy(src_ref=bout_ref.at[chip, rows, :], dst_ref=bout_ref.at[chip, rows, :],
                                                send_sem=send_f.at[k], recv_sem=recv_f.at[k],
                                                device_id=(x, y, 1 - c), device_id_type=MESH)

        def small_copy(k, chip, to):
            return pltpu.make_async_remote_copy(src_ref=small_ref, dst_ref=sout_ref.at[chip],
                                                send_sem=send_s.at[k], recv_sem=recv_s.at[k],
                                                device_id=to, device_id_type=MESH)

        own = (pltpu.make_async_copy(big_ref, bout_ref.at[me], loc.at[0]),
               pltpu.make_async_copy(small_ref, sout_ref.at[me], loc.at[1]))
        for cp in own:
            cp.start()
        chips = _flips(x, y)
        sends = [ici(k, me, (px, py, c)) for k, (px, py) in enumerate(chips)]
        sends += [small_copy(k, me, (px, py, c)) for k, (px, py) in enumerate(chips)]
        for cp in sends:
            cp.start()
        passed = [forward(k, 2 * px + py, mine) for k, (px, py) in enumerate(chips)]
        for k, (px, py) in enumerate(chips):
            ici(k, 2 * px + py, (x, y, c)).wait_recv()
            passed[k].start()
        for k, (px, py) in enumerate(chips):
            forward(k, 2 * px + py, theirs).wait_recv()
            small_copy(k, 2 * px + py, (x, y, c)).wait_recv()
        for cp in sends + passed:
            cp.wait_send()
        for cp in own:
            cp.wait()

    hbm = pl.BlockSpec(memory_space=pltpu.HBM)
    sem3 = pltpu.SemaphoreType.DMA((3,))
    return pl.pallas_call(
        body, in_specs=[hbm, hbm], out_specs=[hbm, hbm],
        out_shape=[jax.ShapeDtypeStruct((N_CHIPS,) + big.shape, big.dtype),
                   jax.ShapeDtypeStruct((N_CHIPS,) + small.shape, small.dtype)],
        scratch_shapes=[sem3, sem3, sem3, sem3, sem3, sem3, pltpu.SemaphoreType.DMA((2,))],
        name="gather_weights")(big, small)


def _exchange_grads(gbig, gsmall):
    def body(gbig_ref, gsmall_ref, got_ref, ssum_ref, sbuf, send_b, recv_b, send_s, recv_s):
        x, y, c = _my_place()
        me = 4 * x + 2 * y + c

        def big_copy(k, src_chip, to):
            return pltpu.make_async_remote_copy(src_ref=gbig_ref.at[src_chip], dst_ref=got_ref.at[k],
                                                send_sem=send_b.at[k], recv_sem=recv_b.at[k],
                                                device_id=to, device_id_type=MESH)

        def small_copy(k, slot, to):
            return pltpu.make_async_remote_copy(src_ref=gsmall_ref, dst_ref=sbuf.at[slot],
                                                send_sem=send_s.at[k], recv_sem=recv_s.at[k],
                                                device_id=to, device_id_type=MESH)

        peers = [(x ^ (k >> 2), y ^ ((k >> 1) & 1), c ^ (k & 1)) for k in range(1, N_DEV)]
        bigs = [big_copy(k, 2 * px + py, (px, py, c)) for k, (px, py) in enumerate(_flips(x, y))]
        smalls = [small_copy(k, me, p) for k, p in enumerate(peers)]
        for cp in bigs + smalls:
            cp.start()
        sbuf[me] = gsmall_ref[...]
        for k, (px, py, pc) in enumerate(peers):
            small_copy(k, 4 * px + 2 * py + pc, (x, y, c)).wait_recv()
        acc = sbuf[0]
        for d in range(1, N_DEV):
            acc = acc + sbuf[d]
        ssum_ref[...] = acc
        for k in range(3):
            big_copy(k, 0, (x, y, c)).wait_recv()
        for cp in bigs + smalls:
            cp.wait_send()

    hbm = pl.BlockSpec(memory_space=pltpu.HBM)
    vmem = pl.BlockSpec(memory_space=pltpu.VMEM)
    return pl.pallas_call(
        body, in_specs=[hbm, vmem], out_specs=[hbm, vmem],
        out_shape=[jax.ShapeDtypeStruct((3,) + gbig.shape[1:], gbig.dtype),
                   jax.ShapeDtypeStruct(gsmall.shape, gsmall.dtype)],
        scratch_shapes=[pltpu.VMEM((N_DEV,) + gsmall.shape, gsmall.dtype),
                        pltpu.SemaphoreType.DMA((3,)), pltpu.SemaphoreType.DMA((3,)),
                        pltpu.SemaphoreType.DMA((N_DEV - 1,)), pltpu.SemaphoreType.DMA((N_DEV - 1,))],
        name="exchange_grads")(gbig, gsmall)


def _sum_shards(chip, mine, got):
    r, cols = mine.shape[1:]
    tr = _tile(r, PACK_TILE_ROWS, 16)

    def body(chip_ref, mine_ref, got_ref, o_ref):
        o_ref[...] = ((mine_ref[0] + got_ref[0].astype(F32)) + got_ref[1].astype(F32)) + got_ref[2].astype(F32)

    grid_spec = pltpu.PrefetchScalarGridSpec(
        num_scalar_prefetch=1, grid=(r // tr,),
        in_specs=[pl.BlockSpec((1, tr, cols), lambda i, chip_ref: (chip_ref[0], i, 0)),
                  pl.BlockSpec((3, tr, cols), lambda i, chip_ref: (0, i, 0))],
        out_specs=pl.BlockSpec((tr, cols), lambda i, chip_ref: (i, 0)))
    return pl.pallas_call(
        body, grid_spec=grid_spec, out_shape=jax.ShapeDtypeStruct((r, cols), F32),
        compiler_params=_cp("parallel"), name="sum_shards")(chip, mine, got)


def _swap_sibling(t, *, name):
    def body(t_ref, got_ref, send_sem, recv_sem):
        x, y, c = _my_place()
        cp = pltpu.make_async_remote_copy(src_ref=t_ref, dst_ref=got_ref, send_sem=send_sem, recv_sem=recv_sem,
                                          device_id=(x, y, 1 - c), device_id_type=MESH)
        cp.start()
        cp.wait()

    hbm = pl.BlockSpec(memory_space=pltpu.HBM)
    return pl.pallas_call(
        body, in_specs=[hbm], out_specs=hbm, out_shape=jax.ShapeDtypeStruct(t.shape, t.dtype),
        scratch_shapes=[pltpu.SemaphoreType.DMA, pltpu.SemaphoreType.DMA], name=name)(t)


def _give_halves(parts, *, name):
    n = len(parts)
    r, cols = parts[0].shape
    half = r // 2

    def body(*refs):
        got_ref, send_sems, recv_sems = refs[n:]
        x, y, c = _my_place()
        theirs = pl.ds(pl.multiple_of((1 - c) * half, 16), half)
        copies = [pltpu.make_async_remote_copy(src_ref=refs[k].at[theirs, :], dst_ref=got_ref.at[k],
                                               send_sem=send_sems.at[k], recv_sem=recv_sems.at[k],
                                               device_id=(x, y, 1 - c), device_id_type=MESH) for k in range(n)]
        for cp in copies:
            cp.start()
        for cp in copies:
            cp.wait()

    hbm = pl.BlockSpec(memory_space=pltpu.HBM)
    return pl.pallas_call(
        body, in_specs=[hbm] * n, out_specs=hbm, out_shape=jax.ShapeDtypeStruct((n, half, cols), parts[0].dtype),
        scratch_shapes=[pltpu.SemaphoreType.DMA((n,)), pltpu.SemaphoreType.DMA((n,))], name=name)(*parts)


def _add_kept(core, parts, got, *, name):
    n, r, cols = got.shape
    tr = _tile(r, PACK_TILE_ROWS, 16)
    nblk = r // tr

    def body(core_ref, *refs):
        b_ref, o_ref, ob_ref = refs[n:]
        for k in range(n):
            sm = refs[k][...] + b_ref[k]
            o_ref[k] = sm
            ob_ref[k] = sm.astype(BF)

    blk = pl.BlockSpec((n, tr, cols), lambda i, core_ref: (0, i, 0))
    kept = pl.BlockSpec((tr, cols), lambda i, core_ref: (core_ref[0] * nblk + i, 0))
    grid_spec = pltpu.PrefetchScalarGridSpec(
        num_scalar_prefetch=1, grid=(nblk,), in_specs=[kept] * n + [blk], out_specs=[blk, blk])
    return pl.pallas_call(
        body, grid_spec=grid_spec,
        out_shape=[jax.ShapeDtypeStruct(got.shape, F32), jax.ShapeDtypeStruct(got.shape, BF)],
        compiler_params=_cp("parallel"), name=name)(core, *parts, got)


def _by_core(c, mine, sibling):
    return jnp.where(c == 0, mine, sibling), jnp.where(c == 0, sibling, mine)


def _adamw(wt, m, v, g, *, name):
    r, cols = wt.shape
    tr = _tile(r, 256, 8)

    def body(w_ref, m_ref, v_ref, g_ref, d_ref, nm_ref, nv_ref):
        gv = g_ref[...]
        nm = ADAM_B1 * m_ref[...] + (1.0 - ADAM_B1) * gv
        nv = ADAM_B2 * v_ref[...] + (1.0 - ADAM_B2) * (gv * gv)
        m_hat = nm / (1.0 - ADAM_B1 ** ADAM_STEP)
        v_hat = nv / (1.0 - ADAM_B2 ** ADAM_STEP)
        d_ref[...] = -ADAM_LR * (m_hat / (jnp.sqrt(v_hat) + ADAM_EPS) + ADAM_WD * w_ref[...])
        nm_ref[...] = nm
        nv_ref[...] = nv

    blk = pl.BlockSpec((tr, cols), lambda i: (i, 0))
    return pl.pallas_call(
        body, grid=(r // tr,), in_specs=[blk] * 4, out_specs=[blk] * 3,
        out_shape=[jax.ShapeDtypeStruct((r, cols), F32)] * 3,
        compiler_params=_cp("parallel"), name=name)(wt, m, v, g)


def _shard_shape(full, axis):
    return tuple(d // N_CHIPS if a == axis else d for a, d in enumerate(full))


def _pack_rows(n):
    rows = -(-n // PACK_COLS)
    return -(-rows // PACK_ROW_MULT) * PACK_ROW_MULT


def _pack(parts, rows, dtype):
    mat = jnp.concatenate([p.astype(dtype).reshape(-1, PACK_COLS) for p in parts], axis=0)
    return jnp.pad(mat, ((0, rows - mat.shape[0]), (0, 0)))


def _join_shards(stacked, axis):
    moved = jnp.moveaxis(stacked, 0, axis)
    shp = moved.shape
    return moved.reshape(shp[:axis] + (shp[axis] * shp[axis + 1],) + shp[axis + 2:])


def _split_shards(full, axis):
    shp = full.shape
    split = full.reshape(shp[:axis] + (N_CHIPS, shp[axis] // N_CHIPS) + shp[axis + 1:])
    return jnp.moveaxis(split, axis, 0)


SMALL_ROWS = 128
SMALL_LAYOUT = [('mix_norm', DEPTH * D_MODEL), ('ffn_norm', DEPTH * D_MODEL), ('b_conv_w', 3 * D_MODEL),
                ('c_q_a_norm', C_Q_RANK), ('c_kv_a_norm', C_KV_RANK), ('a_q_norm', 2 * A_HEAD_DIM),
                ('a_k_norm', 2 * A_HEAD_DIM), ('a_sinks', 2 * A_HEADS), ('c_q_norm', C_QK), ('c_k_norm', C_QK),
                ('loss', 1)]


def _small_offsets():
    offs, row = {}, 0
    for name, n in SMALL_LAYOUT:
        offs[name] = (row * LANES, n)
        row += -(-n // LANES)
    assert row <= SMALL_ROWS
    return offs


def kernel(x, positions, mix_norm, ffn_norm, a_w_qkv, a_q_norm, a_k_norm, a_sinks, a_w_o, b_w_in, b_conv_w, b_w_out, c_w_down, c_q_a_norm, c_kv_a_norm, c_w_q_up, c_w_kv_up, c_q_norm, c_k_norm, c_w_o, f_w_gate_up, f_w_down, loss_target, m_mix_norm, m_ffn_norm, m_a_w_qkv, m_a_q_norm, m_a_k_norm, m_a_sinks, m_a_w_o, m_b_w_in, m_b_conv_w, m_b_w_out, m_c_w_down, m_c_q_a_norm, m_c_kv_a_norm, m_c_w_q_up, m_c_w_kv_up, m_c_q_norm, m_c_k_norm, m_c_w_o, m_f_w_gate_up, m_f_w_down, v_mix_norm, v_ffn_norm, v_a_w_qkv, v_a_q_norm, v_a_k_norm, v_a_sinks, v_a_w_o, v_b_w_in, v_b_conv_w, v_b_w_out, v_c_w_down, v_c_q_a_norm, v_c_kv_a_norm, v_c_w_q_up, v_c_w_kv_up, v_c_q_norm, v_c_k_norm, v_c_w_o, v_f_w_gate_up, v_f_w_down):
    args = dict(locals())
    wshard = {n: args[n] for n in WEIGHTS}
    sharded = {**BIG, **SMALL_SHARDED}
    chip = 2 * lax.axis_index("x") + lax.axis_index("y")

    n_big = sum(wshard[n].size for n in BIG)
    rows = _pack_rows(n_big)
    big = _pack([wshard[n] for n in BIG], rows, BF)
    small = jnp.concatenate([wshard[n].reshape(-1) for n in SMALL_SHARDED])
    small = jnp.pad(small, (0, 8 * LANES - small.shape[0])).reshape(8, LANES)
    core = lax.axis_index("c")
    big_all, small_all = _gather_weights(big, small)
    big_all = big_all.reshape(N_CHIPS, -1)
    small_all = small_all.reshape(N_CHIPS, -1)
    w = {}
    off = 0
    for n, axis in BIG.items():
        shp = wshard[n].shape
        per = shp[1] * shp[2]
        w[n] = [_join_shards(big_all[:, off + l * per:off + (l + 1) * per].reshape((N_CHIPS,) + shp[1:]), axis - 1)
                for l in range(shp[0])]
        off += wshard[n].size
    off = 0
    for n, axis in SMALL_SHARDED.items():
        sz = wshard[n].size
        w[n] = _join_shards(small_all[:, off:off + sz].reshape((N_CHIPS,) + wshard[n].shape), axis)
        off += sz
    for n in WEIGHTS:
        if n not in sharded:
            w[n] = wshard[n]

    loss_blk, grad_x, g = _local_step(x[0], positions[0], w, loss_target[0])

    def shard_of(n, item, k):
        if isinstance(item, tuple):
            item = item[k // 2]
            k = k % 2
            width = item.shape[1] // 2
        else:
            width = item.shape[BIG[n] - 1] // N_CHIPS
        return item[:, k * width:(k + 1) * width] if BIG[n] == 2 else item[k * width:(k + 1) * width]

    gparts = [_pack([shard_of(n, item, k) for n in BIG for item in g[n]], rows, F32) for k in range(N_CHIPS)]
    offs = _small_offsets()
    smalls = {**{n: g[n] for n, _ in SMALL_LAYOUT if n != 'loss'}, 'loss': loss_blk[0:1, 0:1]}
    gsmall = jnp.concatenate(
        [jnp.pad(smalls[n].reshape(-1), (0, -cnt % LANES)) for n, cnt in SMALL_LAYOUT])
    gsmall = jnp.pad(gsmall, (0, SMALL_ROWS * LANES - gsmall.shape[0])).reshape(SMALL_ROWS, LANES)
    from_sibling = _give_halves(gparts, name="swap_grads")
    chip_sum, chip_sum_bf = _add_kept(core.reshape(1).astype(jnp.int32), gparts, from_sibling, name="add_sibling")
    got, ssum = _exchange_grads(chip_sum_bf, gsmall)
    part = _sum_shards(chip.reshape(1).astype(jnp.int32), chip_sum, got)
    part_sib = _swap_sibling(part, name="swap_sums")
    gflat = jnp.concatenate(_by_core(core, part, part_sib), axis=0).reshape(-1)

    ssum = ssum.reshape(-1)
    outs = {}
    off = 0
    for n in WEIGHTS:
        wt = wshard[n]
        if n in BIG:
            grad = gflat[off:off + wt.size]
            off += wt.size
        else:
            o0, cnt = offs[n]
            grad = ssum[o0:o0 + cnt].reshape(g[n].shape)
            if n in SMALL_SHARDED:
                grad = lax.dynamic_index_in_dim(_split_shards(grad, SMALL_SHARDED[n]), chip, 0, keepdims=False)
        shape2 = (-1, wt.shape[-1])
        grad = grad.reshape(shape2)
        res = _adamw(wt.reshape(shape2), args['m_' + n].reshape(shape2), args['v_' + n].reshape(shape2), grad,
                     name=f"adamw_{n}")
        outs[n] = [r.reshape(wt.shape) for r in [grad] + list(res)]
    loss = ssum[offs['loss'][0]]
    return (loss, grad_x[None], *[outs[n][0] for n in WEIGHTS], *[outs[n][1] for n in WEIGHTS],
            *[outs[n][2] for n in WEIGHTS], *[outs[n][3] for n in WEIGHTS])
```

```python
import functools

import jax
import jax.numpy as jnp
from jax import lax
from jax.experimental import pallas as pl
from jax.experimental.pallas import tpu as pltpu

F32 = jnp.float32
BF = jnp.bfloat16

D_MODEL = 1024
DEPTH = 4
N_MIXERS = 3
ROPE_THETA = 500000.0
EPS = 1e-6
BLOCK = 128
LANES = 128

A_HEADS, A_KV_HEADS, A_HEAD_DIM = 16, 4, 64
A_ROT_DIM = A_HEAD_DIM // 4
A_GROUP = A_HEADS // A_KV_HEADS
C_HEADS, C_NOPE, C_ROPE, C_V, C_Q_RANK, C_KV_RANK = 16, 64, 32, 64, 384, 256
C_QK = C_NOPE + C_ROPE
D_FF = 2816

ADAM_LR, ADAM_B1, ADAM_B2, ADAM_EPS, ADAM_WD, ADAM_STEP = 0.001, 0.9, 0.999, 1e-08, 0.01, 10

N_CHIPS = 4
N_DEV = 8
MESH = pl.DeviceIdType.MESH

VMEM_LIMIT = 56 * 1024 * 1024
ROW_TILE = 512
PREP_TILE = 256
ATTN_TILE = 512
FWD_WIDE = 4
SWA_BLOCKS = 8
NEG = -1e30

WEIGHTS = ['mix_norm', 'ffn_norm', 'a_w_qkv', 'a_q_norm', 'a_k_norm', 'a_sinks', 'a_w_o', 'b_w_in', 'b_conv_w',
           'b_w_out', 'c_w_down', 'c_q_a_norm', 'c_kv_a_norm', 'c_w_q_up', 'c_w_kv_up', 'c_q_norm', 'c_k_norm',
           'c_w_o', 'f_w_gate_up', 'f_w_down']
BIG = {'a_w_qkv': 2, 'a_w_o': 1, 'b_w_in': 2, 'b_w_out': 1, 'c_w_down': 1, 'c_w_q_up': 2, 'c_w_kv_up': 2,
       'c_w_o': 1, 'f_w_gate_up': 2, 'f_w_down': 1}
SMALL_SHARDED = {'b_conv_w': 2, 'c_q_a_norm': 1, 'c_kv_a_norm': 1}
PACK_COLS = 256
PACK_ROW_MULT = 32
PACK_TILE_ROWS = 1024
GATHER_CHUNKS = 3


def _cp(*sem):
    return pltpu.CompilerParams(dimension_semantics=sem, vmem_limit_bytes=VMEM_LIMIT)


def _tile(n, target, mult):
    if n <= target:
        return n
    t = (target // mult) * mult
    while t >= mult:
        if n % t == 0:
            return t
        t -= mult
    raise ValueError(f"no tile for {n}")


def _dot(a, b):
    return lax.dot_general(a, b, (((1,), (0,)), ((), ())), preferred_element_type=F32)


def _dot_nt(a, b):
    return lax.dot_general(a, b, (((1,), (1,)), ((), ())), preferred_element_type=F32)


def _dot_tn(a, b):
    return lax.dot_general(a, b, (((0,), (0,)), ((), ())), preferred_element_type=F32)


def _mm(pairs, *, out_dtype, name, res=None, trans_b=False, norm_g=None):
    a0, b0, _ = pairs[0]
    m = a0.shape[0]
    n = b0.shape[0] if trans_b else b0.shape[1]
    tm = _tile(m, 2 * ROW_TILE if sum(a.shape[1] for a, _, _ in pairs) <= 1024 else ROW_TILE, 8)
    tn = _tile(n, 1024, 128)
    n_pairs = len(pairs)
    has_res = res is not None
    has_norm = norm_g is not None
    assert not has_norm or tn == n

    def body(*refs):
        acc = None
        for p in range(n_pairs):
            a = refs[2 * p][...].astype(BF)
            b = refs[2 * p + 1][...].astype(BF)
            d = _dot_nt(a, b) if trans_b else _dot(a, b)
            acc = d if acc is None else acc + d
        nxt = 2 * n_pairs
        if has_res:
            acc = acc + refs[nxt][...]
            nxt += 1
        if has_norm:
            y = acc * lax.rsqrt(jnp.mean(acc * acc, axis=-1, keepdims=True) + EPS)
            refs[-1][...] = (y * refs[nxt][...]).astype(BF)
            refs[-2][...] = acc.astype(out_dtype)
        else:
            refs[-1][...] = acc.astype(out_dtype)

    in_specs, args = [], []
    for a, b, kblk in pairs:
        k = a.shape[1]
        in_specs.append(pl.BlockSpec((tm, k), lambda j, i: (i, 0)))
        if trans_b:
            in_specs.append(pl.BlockSpec((tn, k), functools.partial(lambda j, i, kb: (j, kb), kb=kblk)))
        else:
            in_specs.append(pl.BlockSpec((k, tn), lambda j, i: (0, j)))
        args += [a, b]
    if has_res:
        in_specs.append(pl.BlockSpec((tm, tn), lambda j, i: (i, j)))
        args.append(res)
    out_spec = pl.BlockSpec((tm, tn), lambda j, i: (i, j))
    out_specs, out_shape = out_spec, jax.ShapeDtypeStruct((m, n), out_dtype)
    if has_norm:
        in_specs.append(pl.BlockSpec((1, n), lambda j, i: (0, 0)))
        args.append(norm_g.reshape(1, n))
        out_specs, out_shape = [out_spec, out_spec], [out_shape, jax.ShapeDtypeStruct((m, n), BF)]
    return pl.pallas_call(
        body, grid=(n // tn, m // tm), in_specs=in_specs, out_specs=out_specs, out_shape=out_shape,
        compiler_params=_cp("parallel", "parallel"), name=name)(*args)


def _mm_norm_bwd(pairs, x, g, dres, *, name):
    m, d = x.shape
    tm = _tile(m, ROW_TILE, 8)
    n_pairs = len(pairs)

    def body(*refs):
        x_ref, g_ref, dres_ref, dx_ref, dg_ref = refs[2 * n_pairs:]

        @pl.when(pl.program_id(0) == 0)
        def _():
            dg_ref[...] = jnp.zeros_like(dg_ref)

        dh = None
        for p in range(n_pairs):
            dpart = _dot_nt(refs[2 * p][...].astype(BF), refs[2 * p + 1][...].astype(BF))
            dh = dpart if dh is None else dh + dpart
        dx, xhat = _rms_bwd_math(x_ref[...], g_ref[...], dh, d)
        dx_ref[...] = dres_ref[...] + dx
        dg_ref[0:1, :] += jnp.sum(dh * xhat, axis=0, keepdims=True)

    in_specs, args = [], []
    for a, b, kblk in pairs:
        k = a.shape[1]
        in_specs.append(pl.BlockSpec((tm, k), lambda i: (i, 0)))
        in_specs.append(pl.BlockSpec((d, k), functools.partial(lambda i, kb: (0, kb), kb=kblk)))
        args += [a, b]
    row = pl.BlockSpec((tm, d), lambda i: (i, 0))
    in_specs += [row, pl.BlockSpec((1, d), lambda i: (0, 0)), row]
    dx, dg = pl.pallas_call(
        body, grid=(m // tm,), in_specs=in_specs,
        out_specs=[row, pl.BlockSpec((8, d), lambda i: (0, 0))],
        out_shape=[jax.ShapeDtypeStruct((m, d), F32), jax.ShapeDtypeStruct((8, d), F32)],
        compiler_params=_cp("arbitrary"), name=name)(*args, x, g.reshape(1, d), dres)
    return dx, dg[0:1]


def _mm_tn(a, b, *, name):
    m, k = a.shape
    n = b.shape[1]
    tm = _tile(m, 2 * ROW_TILE, 8)
    tk = _tile(k, 1408, 128)
    tn = _tile(n, 1408, 128)

    def body(a_ref, b_ref, o_ref):
        @pl.when(pl.program_id(2) == 0)
        def _():
            o_ref[...] = jnp.zeros_like(o_ref)

        o_ref[...] += _dot_tn(a_ref[...].astype(BF), b_ref[...].astype(BF))

    return pl.pallas_call(
        body, grid=(k // tk, n // tn, m // tm),
        in_specs=[pl.BlockSpec((tm, tk), lambda kk, j, i: (i, kk)),
                  pl.BlockSpec((tm, tn), lambda kk, j, i: (i, j))],
        out_specs=pl.BlockSpec((tk, tn), lambda kk, j, i: (kk, j)),
        out_shape=jax.ShapeDtypeStruct((k, n), F32),
        compiler_params=_cp("parallel", "parallel", "arbitrary"), name=name)(a, b)


def _rmsnorm_fwd(x, g, *, name):
    s, d = x.shape
    tm = _tile(s, ROW_TILE, 8)

    def body(x_ref, g_ref, h_ref):
        xv = x_ref[...]
        y = xv * lax.rsqrt(jnp.mean(xv * xv, axis=-1, keepdims=True) + EPS)
        h_ref[...] = (y * g_ref[...]).astype(BF)

    return pl.pallas_call(
        body, grid=(s // tm,),
        in_specs=[pl.BlockSpec((tm, d), lambda i: (i, 0)), pl.BlockSpec((1, d), lambda i: (0, 0))],
        out_specs=pl.BlockSpec((tm, d), lambda i: (i, 0)),
        out_shape=jax.ShapeDtypeStruct((s, d), BF),
        compiler_params=_cp("parallel"), name=name)(x, g.reshape(1, d))


def _row_sum(v):
    return jnp.sum(v, axis=-1, keepdims=True)


def _slot_sum(v):
    ones = jnp.ones((LANES, LANES), BF)
    hi = v.astype(BF)
    lo = (v - hi.astype(F32)).astype(BF)
    return _dot(hi, ones) + _dot(lo, ones)


def _rms_bwd_math(xv, g, dh, n, row_sum=_row_sum):
    rstd = lax.rsqrt(row_sum(xv * xv) * (1.0 / n) + EPS)
    xhat = xv * rstd
    dxh = dh * g
    dx = rstd * (dxh - xhat * (row_sum(dxh * xhat) * (1.0 / n)))
    return dx, xhat


def _loss_fwd_bwd(y, target, *, name):
    s, d = y.shape
    tm = _tile(s, ROW_TILE, 8)

    def body(y_ref, t_ref, loss_ref, dy_ref):
        @pl.when(pl.program_id(0) == 0)
        def _():
            loss_ref[...] = jnp.zeros_like(loss_ref)

        err = y_ref[...] - t_ref[...]
        dy_ref[...] = err * (1.0 / d)
        loss_ref[...] += 0.5 * jnp.sum(jnp.sum(err * err, axis=-1, keepdims=True) * (1.0 / d))

    row = pl.BlockSpec((tm, d), lambda i: (i, 0))
    return pl.pallas_call(
        body, grid=(s // tm,), in_specs=[row, row],
        out_specs=[pl.BlockSpec((8, LANES), lambda i: (0, 0)), row],
        out_shape=[jax.ShapeDtypeStruct((8, LANES), F32), jax.ShapeDtypeStruct((s, d), F32)],
        compiler_params=_cp("arbitrary"), name=name)(y, target)


HALF = LANES // 2
A_HR, C_HR = A_ROT_DIM // 2, C_ROPE // 2
A_X1, C_X1 = 0, 32
LAYOUT_A = [(A_X1, 0, A_HR), (A_HR, A_ROT_DIM, A_HEAD_DIM - A_ROT_DIM), (HALF + A_X1, A_HR, A_HR)]
LAYOUT_C = [(0, 0, 32), (C_X1, C_NOPE, C_HR), (HALF, 32, 32), (HALF + C_X1, C_NOPE + C_HR, C_HR)]
LAYOUT_KN = [(0, 0, 32), (HALF, 32, 32)]
LAYOUT_KR = [(C_X1, 0, C_HR), (HALF + C_X1, C_HR, C_HR)]
LAYOUT_V = [(0, 0, C_V)]


def _rope_rows():
    lane = jnp.arange(LANES)
    fa = ROPE_THETA ** (-jnp.arange(0, A_ROT_DIM, 2, dtype=F32) / A_ROT_DIM)
    fc = ROPE_THETA ** (-jnp.arange(0, C_ROPE, 2, dtype=F32) / C_ROPE)

    def rows(f, x1, hr):
        first = (lane >= x1) & (lane < x1 + hr)
        second = (lane >= HALF + x1) & (lane < HALF + x1 + hr)
        freq = jnp.where(first | second, f[(lane - x1) % HALF % hr], 0.0)
        return freq, jnp.where(first, -1.0, jnp.where(second, 1.0, 0.0))

    freq_a, sign_a = rows(fa, A_X1, A_HR)
    freq_c, sign_c = rows(fc, C_X1, C_HR)
    return jnp.stack([freq_a, sign_a, freq_c, sign_c] + [jnp.zeros(LANES)] * 4).astype(F32)


def _rope_tables(pos_col, *, name):
    s = pos_col.shape[0]
    tm = _tile(s, ROW_TILE, 8)

    def body(pos_ref, rows_ref, ca_ref, sa_ref, cc_ref, sc_ref):
        p = pos_ref[...].astype(F32)
        ang_a = p * rows_ref[0:1, :]
        ang_c = p * rows_ref[2:3, :]
        ca_ref[...] = jnp.cos(ang_a)
        sa_ref[...] = jnp.sin(ang_a) * rows_ref[1:2, :]
        cc_ref[...] = jnp.cos(ang_c)
        sc_ref[...] = jnp.sin(ang_c) * rows_ref[3:4, :]

    tab = pl.BlockSpec((tm, LANES), lambda i: (i, 0))
    return pl.pallas_call(
        body, grid=(s // tm,),
        in_specs=[pl.BlockSpec((tm, 1), lambda i: (i, 0)), pl.BlockSpec((8, LANES), lambda i: (0, 0))],
        out_specs=[tab] * 4, out_shape=[jax.ShapeDtypeStruct((s, LANES), F32)] * 4,
        compiler_params=_cp("parallel"), name=name)(pos_col, _rope_rows())


def _rope(xv, cos, sin):
    return xv * cos + pltpu.roll(xv, HALF, 1) * sin


def _slot(ref, t):
    return ref[:, t * LANES:(t + 1) * LANES]


def _head_norm_rope(xv, g, cos, sin, n):
    y = xv * lax.rsqrt(_slot_sum(xv * xv) * (1.0 / n) + EPS) * g
    return _rope(y, cos, sin)


def _head_norm_rope_bwd(xv, g, dy, cos, sin, n):
    dyn = _rope(dy, cos, -sin)
    dx, xhat = _rms_bwd_math(xv, g, dyn, n, _slot_sum)
    return dx, jnp.sum(dyn * xhat, axis=0, keepdims=True)


def _prep_a_fwd(qkv, cos, sin, gq, gk, *, name):
    s = qkv.shape[0]
    tm = _tile(s, PREP_TILE, 8)
    nq, nkv = A_HEADS, A_KV_HEADS

    def body(x_ref, c_ref, s_ref, gq_ref, gk_ref, q_ref, k_ref, v_ref):
        cv, sv = c_ref[...], s_ref[...]
        for t in range(nq):
            q_ref[:, t * LANES:(t + 1) * LANES] = _head_norm_rope(
                _slot(x_ref, t), gq_ref[...], cv, sv, A_HEAD_DIM).astype(BF)
        for t in range(nkv):
            k_ref[:, t * LANES:(t + 1) * LANES] = _head_norm_rope(
                _slot(x_ref, nq + t), gk_ref[...], cv, sv, A_HEAD_DIM).astype(BF)
        v_ref[...] = x_ref[:, (nq + nkv) * LANES:].astype(BF)

    def rows(w):
        return pl.BlockSpec((tm, w), lambda i: (i, 0))

    vec = pl.BlockSpec((1, LANES), lambda i: (0, 0))
    return pl.pallas_call(
        body, grid=(s // tm,),
        in_specs=[rows(qkv.shape[1]), rows(LANES), rows(LANES), vec, vec],
        out_specs=[rows(nq * LANES), rows(nkv * LANES), rows(nkv * LANES)],
        out_shape=[jax.ShapeDtypeStruct((s, nq * LANES), BF), jax.ShapeDtypeStruct((s, nkv * LANES), BF),
                   jax.ShapeDtypeStruct((s, nkv * LANES), BF)],
        compiler_params=_cp("parallel"), name=name)(qkv, cos, sin, gq, gk)


def _prep_a_bwd(qkv, dq, dk, dv, cos, sin, gq, gk, *, name):
    s = qkv.shape[0]
    tm = _tile(s, PREP_TILE, 8)
    nq, nkv = A_HEADS, A_KV_HEADS

    def body(x_ref, dq_ref, dk_ref, dv_ref, c_ref, s_ref, gq_ref, gk_ref, dx_ref, dg_ref):
        @pl.when(pl.program_id(0) == 0)
        def _():
            dg_ref[...] = jnp.zeros_like(dg_ref)

        cv, sv = c_ref[...], s_ref[...]
        dgq = jnp.zeros((1, LANES), F32)
        dgk = jnp.zeros((1, LANES), F32)
        for t in range(nq):
            dx, dg = _head_norm_rope_bwd(_slot(x_ref, t), gq_ref[...], _slot(dq_ref, t), cv, sv,
                                         A_HEAD_DIM)
            dx_ref[:, t * LANES:(t + 1) * LANES] = dx.astype(BF)
            dgq = dgq + dg
        for t in range(nkv):
            dx, dg = _head_norm_rope_bwd(_slot(x_ref, nq + t), gk_ref[...], _slot(dk_ref, t), cv, sv,
                                         A_HEAD_DIM)
            dx_ref[:, (nq + t) * LANES:(nq + t + 1) * LANES] = dx.astype(BF)
            dgk = dgk + dg
        dx_ref[:, (nq + nkv) * LANES:] = dv_ref[...].astype(BF)
        dg_ref[0:1, :] += dgq
        dg_ref[1:2, :] += dgk

    def rows(w):
        return pl.BlockSpec((tm, w), lambda i: (i, 0))

    vec = pl.BlockSpec((1, LANES), lambda i: (0, 0))
    dx, dg = pl.pallas_call(
        body, grid=(s // tm,),
        in_specs=[rows(qkv.shape[1]), rows(nq * LANES), rows(nkv * LANES), rows(nkv * LANES), rows(LANES),
                  rows(LANES), vec, vec],
        out_specs=[rows(qkv.shape[1]), pl.BlockSpec((8, LANES), lambda i: (0, 0))],
        out_shape=[jax.ShapeDtypeStruct(qkv.shape, BF), jax.ShapeDtypeStruct((8, LANES), F32)],
        compiler_params=_cp("arbitrary"), name=name)(qkv, dq, dk, dv, cos, sin, gq, gk)
    return dx, dg[0:1], dg[1:2]


def _prep_c1_fwd(dn, gq, gkv, *, name):
    s = dn.shape[0]
    tm = _tile(s, ROW_TILE, 8)

    def body(x_ref, gq_ref, gkv_ref, cq_ref, ckv_ref):
        for lo, n, g_ref, o_ref in ((0, C_Q_RANK, gq_ref, cq_ref), (C_Q_RANK, C_KV_RANK, gkv_ref, ckv_ref)):
            xv = x_ref[:, lo:lo + n]
            y = xv * lax.rsqrt(jnp.mean(xv * xv, axis=-1, keepdims=True) + EPS)
            o_ref[...] = (y * g_ref[...]).astype(BF)

    def rows(w):
        return pl.BlockSpec((tm, w), lambda i: (i, 0))

    return pl.pallas_call(
        body, grid=(s // tm,),
        in_specs=[rows(dn.shape[1]), pl.BlockSpec((1, C_Q_RANK), lambda i: (0, 0)),
                  pl.BlockSpec((1, C_KV_RANK), lambda i: (0, 0))],
        out_specs=[rows(C_Q_RANK), rows(C_KV_RANK)],
        out_shape=[jax.ShapeDtypeStruct((s, C_Q_RANK), BF), jax.ShapeDtypeStruct((s, C_KV_RANK), BF)],
        compiler_params=_cp("parallel"), name=name)(dn, gq, gkv)


def _prep_c1_bwd(dn, dcq, dckv, dkr, gq, gkv, *, name):
    s = dn.shape[0]
    tm = _tile(s, ROW_TILE, 8)

    def body(x_ref, dcq_ref, dckv_ref, dkr_ref, gq_ref, gkv_ref, dx_ref, dgq_ref, dgkv_ref):
        @pl.when(pl.program_id(0) == 0)
        def _():
            dgq_ref[...] = jnp.zeros_like(dgq_ref)
            dgkv_ref[...] = jnp.zeros_like(dgkv_ref)

        for lo, n, g_ref, d_ref, dg_ref in ((0, C_Q_RANK, gq_ref, dcq_ref, dgq_ref),
                                            (C_Q_RANK, C_KV_RANK, gkv_ref, dckv_ref, dgkv_ref)):
            dv = d_ref[...]
            dx, xhat = _rms_bwd_math(x_ref[:, lo:lo + n], g_ref[...], dv, n)
            dx_ref[:, lo:lo + n] = dx.astype(BF)
            dg_ref[0:1, :] += jnp.sum(dv * xhat, axis=0, keepdims=True)
        dx_ref[:, C_Q_RANK + C_KV_RANK:] = dkr_ref[...].astype(BF)

    def rows(w):
        return pl.BlockSpec((tm, w), lambda i: (i, 0))

    def vec(w, r=1):
        return pl.BlockSpec((r, w), lambda i: (0, 0))

    dx, dgq, dgkv = pl.pallas_call(
        body, grid=(s // tm,),
        in_specs=[rows(dn.shape[1]), rows(C_Q_RANK), rows(C_KV_RANK), rows(LANES), vec(C_Q_RANK), vec(C_KV_RANK)],
        out_specs=[rows(dn.shape[1]), vec(C_Q_RANK, 8), vec(C_KV_RANK, 8)],
        out_shape=[jax.ShapeDtypeStruct(dn.shape, BF), jax.ShapeDtypeStruct((8, C_Q_RANK), F32),
                   jax.ShapeDtypeStruct((8, C_KV_RANK), F32)],
        compiler_params=_cp("arbitrary"), name=name)(dn, dcq, dckv, dkr, gq, gkv)
    return dx, dgq[0:1], dgkv[0:1]


def _prep_c2_fwd(qraw, knope, dn, cos, sin, gq, gk, *, name):
    s = qraw.shape[0]
    tm = _tile(s, PREP_TILE, 8)
    kr_blk = dn.shape[1] // LANES - 1

    def body(q_ref, kn_ref, kr_ref, c_ref, s_ref, gq_ref, gk_ref, qo_ref, ko_ref):
        cv, sv, kr = c_ref[...], s_ref[...], kr_ref[...]
        for t in range(C_HEADS):
            qo_ref[:, t * LANES:(t + 1) * LANES] = _head_norm_rope(
                _slot(q_ref, t), gq_ref[...], cv, sv, C_QK).astype(BF)
            ko_ref[:, t * LANES:(t + 1) * LANES] = _head_norm_rope(
                _slot(kn_ref, t) + kr, gk_ref[...], cv, sv, C_QK).astype(BF)

    def rows(w, blk=0):
        return pl.BlockSpec((tm, w), lambda i: (i, blk))

    vec = pl.BlockSpec((1, LANES), lambda i: (0, 0))
    w = C_HEADS * LANES
    return pl.pallas_call(
        body, grid=(s // tm,),
        in_specs=[rows(w), rows(w), rows(LANES, kr_blk), rows(LANES), rows(LANES), vec, vec],
        out_specs=[rows(w), rows(w)],
        out_shape=[jax.ShapeDtypeStruct((s, w), BF)] * 2,
        compiler_params=_cp("parallel"), name=name)(qraw, knope, dn, cos, sin, gq, gk)


def _prep_c2_bwd(qraw, knope, dn, dq, dk, cos, sin, gq, gk, *, name):
    s = qraw.shape[0]
    tm = _tile(s, PREP_TILE, 8)
    kr_blk = dn.shape[1] // LANES - 1

    def body(q_ref, kn_ref, kr_ref, dq_ref, dk_ref, c_ref, s_ref, gq_ref, gk_ref,
             dqo_ref, dkno_ref, dkr_ref, dg_ref):
        @pl.when(pl.program_id(0) == 0)
        def _():
            dg_ref[...] = jnp.zeros_like(dg_ref)

        cv, sv, kr = c_ref[...], s_ref[...], kr_ref[...]
        dgq = jnp.zeros((1, LANES), F32)
        dgk = jnp.zeros((1, LANES), F32)
        dkr = jnp.zeros((tm, LANES), F32)
        for t in range(C_HEADS):
            dx, dg = _head_norm_rope_bwd(_slot(q_ref, t), gq_ref[...], _slot(dq_ref, t), cv, sv,
                                         C_QK)
            dqo_ref[:, t * LANES:(t + 1) * LANES] = dx.astype(BF)
            dgq = dgq + dg
            dx, dg = _head_norm_rope_bwd(_slot(kn_ref, t) + kr, gk_ref[...], _slot(dk_ref, t), cv, sv,
                                         C_QK)
            dkno_ref[:, t * LANES:(t + 1) * LANES] = dx.astype(BF)
            dkr = dkr + dx
            dgk = dgk + dg
        dkr_ref[...] = dkr
        dg_ref[0:1, :] += dgq
        dg_ref[1:2, :] += dgk

    def rows(w, blk=0):
        return pl.BlockSpec((tm, w), lambda i: (i, blk))

    vec = pl.BlockSpec((1, LANES), lambda i: (0, 0))
    w = C_HEADS * LANES
    dqo, dkno, dkr, dg = pl.pallas_call(
        body, grid=(s // tm,),
        in_specs=[rows(w), rows(w), rows(LANES, kr_blk), rows(w), rows(w), rows(LANES), rows(LANES), vec, vec],
        out_specs=[rows(w), rows(w), rows(LANES), pl.BlockSpec((8, LANES), lambda i: (0, 0))],
        out_shape=[jax.ShapeDtypeStruct((s, w), BF), jax.ShapeDtypeStruct((s, w), BF),
                   jax.ShapeDtypeStruct((s, LANES), F32), jax.ShapeDtypeStruct((8, LANES), F32)],
        compiler_params=_cp("arbitrary"), name=name)(qraw, knope, dn, dq, dk, cos, sin, gq, gk)
    return dqo, dkno, dkr, dg[0:1], dg[1:2]


def _lane_pick(mat, g):
    lane = lax.broadcasted_iota(jnp.int32, mat.shape, 1)
    return jnp.sum(jnp.where(lane == g, mat, 0.0), axis=-1, keepdims=True)


def _swa_fwd(q, k, v, sinks, *, name):
    s = q.shape[0]
    nb = min(SWA_BLOCKS, s // BLOCK)
    qt = nb * BLOCK
    scale = A_HEAD_DIM ** -0.5
    gw = A_GROUP * LANES

    def body(sink_ref, q_ref, kc_ref, kp_ref, vc_ref, vp_ref, o_ref, lse_ref):
        kvh, n = pl.program_id(0), pl.program_id(1)
        rows = A_GROUP * BLOCK
        kall = jnp.concatenate([kp_ref[...], kc_ref[...]], axis=0)
        vall = jnp.concatenate([vp_ref[...], vc_ref[...]], axis=0)
        r = lax.broadcasted_iota(jnp.int32, (rows, 2 * BLOCK), 0) & (BLOCK - 1)
        c = lax.broadcasted_iota(jnp.int32, (rows, 2 * BLOCK), 1)
        cur_ok = (c >= BLOCK) & (c - BLOCK <= r)
        prev_ok = (c < BLOCK) & (c > r)
        head = lax.broadcasted_iota(jnp.int32, (rows, 1), 0) >> 7
        sink = jnp.zeros((rows, 1), F32)
        for g in range(A_GROUP):
            sink = jnp.where(head == g, sink_ref[kvh * A_GROUP + g], sink)
        lane = lax.broadcasted_iota(jnp.int32, (BLOCK, LANES), 1)
        for b in range(nb):
            blk = slice(b * BLOCK, (b + 1) * BLOCK)
            qs = jnp.concatenate([q_ref[blk, g * LANES:(g + 1) * LANES] for g in range(A_GROUP)], axis=0)
            k2 = kall[b * BLOCK:(b + 2) * BLOCK]
            v2 = vall[b * BLOCK:(b + 2) * BLOCK]
            ok = (cur_ok | prev_ok) if b > 0 else (cur_ok | (prev_ok & (n > 0)))
            sc = jnp.where(ok, _dot_nt(qs, k2) * scale, NEG)
            m = jnp.maximum(jnp.max(sc, axis=-1, keepdims=True), sink)
            p = jnp.exp(sc - m)
            l = jnp.sum(p, axis=-1, keepdims=True) + jnp.exp(sink - m)
            o = _dot((p * (1.0 / l)).astype(BF), v2)
            lse = m + jnp.log(l)
            lse_mat = jnp.zeros((BLOCK, LANES), F32)
            for g in range(A_GROUP):
                o_ref[blk, g * LANES:(g + 1) * LANES] = o[g * BLOCK:(g + 1) * BLOCK].astype(BF)
                lse_mat = jnp.where(lane == g, lse[g * BLOCK:(g + 1) * BLOCK], lse_mat)
            lse_ref[0, blk, :] = lse_mat

    cur = pl.BlockSpec((qt, LANES), lambda h, n: (n, h))
    prev = pl.BlockSpec((BLOCK, LANES), lambda h, n: (jnp.maximum(n * nb - 1, 0), h))
    return pl.pallas_call(
        body, grid=(A_KV_HEADS, s // qt),
        in_specs=[pl.BlockSpec(memory_space=pltpu.SMEM), pl.BlockSpec((qt, gw), lambda h, n: (n, h)),
                  cur, prev, cur, prev],
        out_specs=[pl.BlockSpec((qt, gw), lambda h, n: (n, h)),
                   pl.BlockSpec((1, qt, LANES), lambda h, n: (h, n, 0))],
        out_shape=[jax.ShapeDtypeStruct(q.shape, BF), jax.ShapeDtypeStruct((A_KV_HEADS, s, LANES), F32)],
        compiler_params=_cp("parallel", "parallel"), name=name)(sinks, q, k, k, v, v)


def _swa_bwd(q, k, v, o, do, lse, sinks, *, name):
    s = q.shape[0]
    nblk = s // BLOCK
    nb = min(SWA_BLOCKS, nblk)
    qt = nb * BLOCK
    nsteps = s // qt
    scale = A_HEAD_DIM ** -0.5
    gw = A_GROUP * LANES

    def body(sink_ref, qc_ref, qn_ref, k_ref, v_ref, oc_ref, on_ref, doc_ref, don_ref, lc_ref, ln_ref,
             dq_ref, dk_ref, dv_ref, dsink_ref, carry):
        kvh, n = pl.program_id(0), pl.program_id(1)

        @pl.when(n == 0)
        def _():
            carry[...] = jnp.zeros_like(carry)
            dsink_ref[...] = jnp.zeros_like(dsink_ref)

        half = A_GROUP * BLOCK
        rows = 2 * half
        qall = jnp.concatenate([qc_ref[...], qn_ref[...]], axis=0)
        oall = jnp.concatenate([oc_ref[...], on_ref[...]], axis=0)
        doall = jnp.concatenate([doc_ref[...], don_ref[...]], axis=0)
        lall = jnp.concatenate([lc_ref[0], ln_ref[0]], axis=0)
        row = lax.broadcasted_iota(jnp.int32, (rows, BLOCK), 0)
        r = row & (BLOCK - 1)
        c = lax.broadcasted_iota(jnp.int32, (rows, BLOCK), 1)
        diag_ok = (row < half) & (c <= r)
        next_ok = (row >= half) & (c > r)
        lane8 = lax.broadcasted_iota(jnp.int32, (8, LANES), 1)
        dsink = jnp.zeros((8, LANES), F32)
        off_prev = None
        for b in range(nb):
            blk = slice(b * BLOCK, (b + 1) * BLOCK)
            kv, vv = k_ref[blk, :], v_ref[blk, :]

            def stack(allv):
                return jnp.concatenate(
                    [allv[b * BLOCK:(b + 1) * BLOCK, g * LANES:(g + 1) * LANES] for g in range(A_GROUP)]
                    + [allv[(b + 1) * BLOCK:(b + 2) * BLOCK, g * LANES:(g + 1) * LANES] for g in range(A_GROUP)],
                    axis=0)

            qs, osk, dos = stack(qall), stack(oall), stack(doall)
            lse = jnp.concatenate([_lane_pick(lall[b * BLOCK:(b + 1) * BLOCK], g) for g in range(A_GROUP)]
                                  + [_lane_pick(lall[(b + 1) * BLOCK:(b + 2) * BLOCK], g) for g in range(A_GROUP)],
                                  axis=0)
            delta = jnp.sum(dos.astype(F32) * osk.astype(F32), axis=-1, keepdims=True)
            ok = (diag_ok | next_ok) if b < nb - 1 else (diag_ok | (next_ok & (n < nsteps - 1)))
            sc = jnp.where(ok, _dot_nt(qs, kv) * scale, NEG)
            p = jnp.exp(sc - lse)
            dv_ref[blk, :] = _dot_tn(p.astype(BF), dos)
            dp = _dot_nt(dos, vv)
            ds = (p * (dp - delta) * scale).astype(BF)
            dk_ref[blk, :] = _dot_tn(ds, qs)
            dqs = _dot(ds, kv)
            for g in range(A_GROUP):
                cur = slice(g * BLOCK, (g + 1) * BLOCK)
                before = carry[:, g * LANES:(g + 1) * LANES] if b == 0 else off_prev[cur]
                dq_ref[blk, g * LANES:(g + 1) * LANES] = before + dqs[cur]
                p_sink = jnp.exp(sink_ref[kvh * A_GROUP + g] - lse[cur])
                dsink = dsink + jnp.where(lane8 == g, -jnp.sum(p_sink * delta[cur]), 0.0)
            off_prev = dqs[half:]
        for g in range(A_GROUP):
            carry[:, g * LANES:(g + 1) * LANES] = off_prev[g * BLOCK:(g + 1) * BLOCK]
        dsink_ref[...] += dsink

    def nxt(n):
        return jnp.minimum((n + 1) * nb, nblk - 1)

    grp_c = pl.BlockSpec((qt, gw), lambda h, n: (n, h))
    grp_n = pl.BlockSpec((BLOCK, gw), lambda h, n: (nxt(n), h))
    kvb = pl.BlockSpec((qt, LANES), lambda h, n: (n, h))
    lse_c = pl.BlockSpec((1, qt, LANES), lambda h, n: (h, n, 0))
    lse_n = pl.BlockSpec((1, BLOCK, LANES), lambda h, n: (h, nxt(n), 0))
    dq, dk, dv, dsink = pl.pallas_call(
        body, grid=(A_KV_HEADS, nsteps),
        in_specs=[pl.BlockSpec(memory_space=pltpu.SMEM), grp_c, grp_n, kvb, kvb, grp_c, grp_n, grp_c, grp_n,
                  lse_c, lse_n],
        out_specs=[grp_c, kvb, kvb, pl.BlockSpec((8, LANES), lambda h, n: (h, 0))],
        out_shape=[jax.ShapeDtypeStruct(q.shape, F32), jax.ShapeDtypeStruct(k.shape, F32),
                   jax.ShapeDtypeStruct(v.shape, F32), jax.ShapeDtypeStruct((A_KV_HEADS * 8, LANES), F32)],
        scratch_shapes=[pltpu.VMEM((BLOCK, gw), F32)],
        compiler_params=_cp("parallel", "arbitrary"), name=name)(sinks, q, q, k, v, o, o, do, do, lse, lse)
    dsinks = dsink.reshape(A_KV_HEADS, 8, LANES)[:, 0, :A_GROUP].reshape(A_HEADS)
    return dq, dk, dv, dsinks


LOG2E = 1.4426950408889634


def _mla_fwd(q, k, v, *, name):
    s = q.shape[0]
    t = _tile(s, ATTN_TILE, LANES)
    nt = s // t
    scale = C_QK ** -0.5
    c2 = scale * LOG2E

    def body(q_ref, k_ref, v_ref, o_ref, lse_ref, m_sc, acc_sc):
        qi = pl.program_id(1)
        qv = q_ref[...]
        m_sc[...] = jnp.full_like(m_sc, NEG)
        acc_sc[...] = jnp.zeros_like(acc_sc)

        def step(start, width, diag):
            rows = pl.ds(pl.multiple_of(start, t), width)
            sc = _dot_nt(qv, k_ref[rows, :])
            if diag:
                r = lax.broadcasted_iota(jnp.int32, (t, t), 0)
                c = lax.broadcasted_iota(jnp.int32, (t, t), 1)
                own = jnp.where(c <= r, sc[:, width - t:], NEG)
                sc = own if width == t else jnp.concatenate([sc[:, :width - t], own], axis=1)
            m_prev = m_sc[...]
            m_new = jnp.maximum(m_prev, jnp.max(sc, axis=-1, keepdims=True))
            alpha = jnp.exp2((m_prev - m_new) * c2)
            p = jnp.exp2((sc - jnp.tile(m_new, (1, width // LANES))) * c2).astype(BF)
            lane = lax.broadcasted_iota(jnp.int32, (width, LANES), 1)
            vv = jnp.where(lane == C_V, jnp.ones((), BF), v_ref[rows, :])
            acc_sc[...] = alpha * acc_sc[...] + _dot(p, vv)
            m_sc[...] = m_new

        def wide_body(wi, carry):
            step(wi * (FWD_WIDE * t), FWD_WIDE * t, False)
            return carry

        def single_body(j, carry):
            step(j * t, t, False)
            return carry

        with_diag = qi >= FWD_WIDE - 1

        @pl.when(with_diag)
        def _():
            step((qi - (FWD_WIDE - 1)) * t, FWD_WIDE * t, True)

        @pl.when(jnp.logical_not(with_diag))
        def _():
            step(qi * t, t, True)

        rest = jnp.where(with_diag, qi - (FWD_WIDE - 1), qi)
        n_wide = rest // FWD_WIDE
        lax.fori_loop(0, n_wide, wide_body, 0)
        lax.fori_loop(n_wide * FWD_WIDE, rest, single_body, 0)
        acc = acc_sc[...]
        l = _lane_pick(acc, C_V)
        lane = lax.broadcasted_iota(jnp.int32, (t, LANES), 1)
        o_ref[...] = jnp.where(lane == C_V, 0.0, acc * (1.0 / l)).astype(BF)
        lse2 = m_sc[...] * c2 + jnp.log(l) * LOG2E
        lse_ref[0, 0] = jnp.transpose(lse2)[0:8, :]

    qspec = pl.BlockSpec((t, LANES), lambda h, i: (i, h))
    kspec = pl.BlockSpec((s, LANES), lambda h, i: (0, h))
    return pl.pallas_call(
        body, grid=(C_HEADS, nt), in_specs=[qspec, kspec, kspec],
        out_specs=[qspec, pl.BlockSpec((1, 1, 8, t), lambda h, i: (h, i, 0, 0))],
        out_shape=[jax.ShapeDtypeStruct(q.shape, BF), jax.ShapeDtypeStruct((C_HEADS, nt, 8, t), F32)],
        scratch_shapes=[pltpu.VMEM((t, LANES), F32)] * 2,
        compiler_params=_cp("parallel", "parallel"), name=name)(q, k, v)


def _mla_delta(o, do, *, name):
    s = o.shape[0]
    t = _tile(s, ATTN_TILE, LANES)
    nt = s // t

    def body(o_ref, do_ref, d_ref):
        for h in range(C_HEADS):
            prod = jnp.transpose(_slot(o_ref, h).astype(F32) * _slot(do_ref, h).astype(F32))
            d_ref[h, 0] = jnp.broadcast_to(jnp.sum(prod, axis=0, keepdims=True), (8, t))

    blk = pl.BlockSpec((t, C_HEADS * LANES), lambda i: (i, 0))
    return pl.pallas_call(
        body, grid=(nt,), in_specs=[blk, blk],
        out_specs=pl.BlockSpec((C_HEADS, 1, 8, t), lambda i: (0, i, 0, 0)),
        out_shape=jax.ShapeDtypeStruct((C_HEADS, nt, 8, t), F32),
        compiler_params=_cp("parallel"), name=name)(o, do)


def _mla_bwd(q, k, v, do, lse2, delta, *, name):
    s = q.shape[0]
    t = _tile(s, ATTN_TILE, LANES)
    nt = s // t
    scale = C_QK ** -0.5
    c2 = scale * LOG2E

    def body(q_ref, do_ref, k_ref, v_ref, lse_ref, dl_ref, dq_ref, dk_ref, dv_ref):
        kj = pl.program_id(1)

        @pl.when(kj == 0)
        def _():
            dq_ref[...] = jnp.zeros_like(dq_ref)

        dk_ref[...] = jnp.zeros_like(dk_ref)
        dv_ref[...] = jnp.zeros_like(dv_ref)
        kv, vv = k_ref[...], v_ref[...]

        def step(i, n, diag):
            rows = pl.ds(pl.multiple_of(i * t, t), n * t)
            qv, dov = q_ref[rows, :], do_ref[rows, :]
            st = _dot_nt(kv, qv)
            if diag:
                r = lax.broadcasted_iota(jnp.int32, (t, t), 0)
                c = lax.broadcasted_iota(jnp.int32, (t, t), 1)
                own = jnp.where(r <= c, st[:, :t], NEG)
                st = own if n == 1 else jnp.concatenate([own, st[:, t:]], axis=1)
            lse = jnp.concatenate([lse_ref[0, i + u, 0:1, :] for u in range(n)], axis=1)
            dl = jnp.concatenate([dl_ref[0, i + u, 0:1, :] for u in range(n)], axis=1)
            pt = jnp.exp2(st * c2 - lse)
            dv_ref[...] += _dot(pt.astype(BF), dov)
            dpt = _dot_nt(vv, dov)
            dst = (pt * (dpt - dl) * scale).astype(BF)
            dk_ref[...] += _dot(dst, qv)
            dq_ref[rows, :] += _dot_tn(dst, kv)

        odd = (nt - kj) % 2
        first = kj + 2 - odd

        @pl.when(odd == 0)
        def _():
            step(kj, 2, True)

        @pl.when(odd == 1)
        def _():
            step(kj, 1, True)

        def pair_body(pi, carry):
            step(first + 2 * pi, 2, False)
            return carry

        lax.fori_loop(0, (nt - first) // 2, pair_body, 0)

    res = pl.BlockSpec((s, LANES), lambda h, j: (0, h))
    kspec = pl.BlockSpec((t, LANES), lambda h, j: (j, h))
    stat = pl.BlockSpec((1, nt, 8, t), lambda h, j: (h, 0, 0, 0))
    return pl.pallas_call(
        body, grid=(C_HEADS, nt), in_specs=[res, res, kspec, kspec, stat, stat],
        out_specs=[res, kspec, kspec],
        out_shape=[jax.ShapeDtypeStruct(q.shape, F32)] * 3,
        compiler_params=_cp("parallel", "arbitrary"), name=name)(q, do, k, v, lse2, delta)


def _conv_shifted(z, zprev, tm):
    row = lax.broadcasted_iota(jnp.int32, z.shape, 0)
    z1 = jnp.where(row == 0, zprev[7:8], pltpu.roll(z, 1, 0))
    z2 = jnp.where(row == 0, zprev[6:7], jnp.where(row == 1, zprev[7:8], pltpu.roll(z, 2, 0)))
    return z1, z2


def _conv_fwd(proj, w, *, name):
    s = proj.shape[0]
    d = D_MODEL
    tm = _tile(s, PREP_TILE, 8)

    def body(x_ref, xp_ref, w_ref, y_ref):
        i = pl.program_id(0)
        z = x_ref[:, d:2 * d] * x_ref[:, 2 * d:]
        zprev = jnp.where(i > 0, xp_ref[:, d:2 * d] * xp_ref[:, 2 * d:], 0.0)
        z1, z2 = _conv_shifted(z, zprev, tm)
        y = w_ref[0:1, :] * z2 + w_ref[1:2, :] * z1 + w_ref[2:3, :] * z
        y_ref[...] = (x_ref[:, :d] * y).astype(BF)

    per8 = tm // 8
    return pl.pallas_call(
        body, grid=(s // tm,),
        in_specs=[pl.BlockSpec((tm, 3 * d), lambda i: (i, 0)),
                  pl.BlockSpec((8, 3 * d), lambda i: (jnp.maximum(i * per8 - 1, 0), 0)),
                  pl.BlockSpec((8, d), lambda i: (0, 0))],
        out_specs=pl.BlockSpec((tm, d), lambda i: (i, 0)),
        out_shape=jax.ShapeDtypeStruct((s, d), BF),
        compiler_params=_cp("parallel"), name=name)(proj, proj, w)


def _conv_bwd(proj, dyb, w, *, name):
    s = proj.shape[0]
    d = D_MODEL
    tm = _tile(s, PREP_TILE, 8)
    nt = s // tm

    def body(x_ref, xp_ref, xn_ref, dy_ref, dyn_ref, w_ref, dx_ref, dw_ref):
        i = pl.program_id(0)

        @pl.when(i == 0)
        def _():
            dw_ref[...] = jnp.zeros_like(dw_ref)

        b, c, u = x_ref[:, :d], x_ref[:, d:2 * d], x_ref[:, 2 * d:]
        z = c * u
        zprev = jnp.where(i > 0, xp_ref[:, d:2 * d] * xp_ref[:, 2 * d:], 0.0)
        z1, z2 = _conv_shifted(z, zprev, tm)
        w0, w1, w2 = w_ref[0:1, :], w_ref[1:2, :], w_ref[2:3, :]
        y = w0 * z2 + w1 * z1 + w2 * z
        dyb_v = dy_ref[...]
        dyc = dyb_v * b
        dyn = jnp.where(i < nt - 1, dyn_ref[...] * xn_ref[:, :d], 0.0)
        row = lax.broadcasted_iota(jnp.int32, dyc.shape, 0)
        d1 = jnp.where(row == tm - 1, dyn[0:1], pltpu.roll(dyc, tm - 1, 0))
        d2 = jnp.where(row == tm - 1, dyn[1:2], jnp.where(row == tm - 2, dyn[0:1], pltpu.roll(dyc, tm - 2, 0)))
        dz = w2 * dyc + w1 * d1 + w0 * d2
        dx_ref[:, :d] = (dyb_v * y).astype(BF)
        dx_ref[:, d:2 * d] = (dz * u).astype(BF)
        dx_ref[:, 2 * d:] = (dz * c).astype(BF)
        dw_ref[0:1, :] += jnp.sum(dyc * z2, axis=0, keepdims=True)
        dw_ref[1:2, :] += jnp.sum(dyc * z1, axis=0, keepdims=True)
        dw_ref[2:3, :] += jnp.sum(dyc * z, axis=0, keepdims=True)

    per8 = tm // 8
    last8 = s // 8 - 1
    dx, dw = pl.pallas_call(
        body, grid=(nt,),
        in_specs=[pl.BlockSpec((tm, 3 * d), lambda i: (i, 0)),
                  pl.BlockSpec((8, 3 * d), lambda i: (jnp.maximum(i * per8 - 1, 0), 0)),
                  pl.BlockSpec((8, 3 * d), lambda i: (jnp.minimum((i + 1) * per8, last8), 0)),
                  pl.BlockSpec((tm, d), lambda i: (i, 0)),
                  pl.BlockSpec((8, d), lambda i: (jnp.minimum((i + 1) * per8, last8), 0)),
                  pl.BlockSpec((8, d), lambda i: (0, 0))],
        out_specs=[pl.BlockSpec((tm, 3 * d), lambda i: (i, 0)), pl.BlockSpec((8, d), lambda i: (0, 0))],
        out_shape=[jax.ShapeDtypeStruct((s, 3 * d), BF), jax.ShapeDtypeStruct((8, d), F32)],
        compiler_params=_cp("arbitrary"), name=name)(proj, proj, proj, dyb, dyb, w)
    return dx, dw[0:3]


def _ffn_up(h, wgu, *, name):
    s, d = h.shape
    f = wgu.shape[1] // 2
    tm = _tile(s, ROW_TILE, 8)
    tn = _tile(f, 1408, 128)
    nb = f // tn

    def body(h_ref, wg_ref, wu_ref, g_ref, u_ref, a_ref):
        hv = h_ref[...]
        g = _dot(hv, wg_ref[...])
        u = _dot(hv, wu_ref[...])
        g_ref[...] = g.astype(BF)
        u_ref[...] = u.astype(BF)
        a_ref[...] = (g * jax.nn.sigmoid(g) * u).astype(BF)

    out = pl.BlockSpec((tm, tn), lambda j, i: (i, j))
    return pl.pallas_call(
        body, grid=(nb, s // tm),
        in_specs=[pl.BlockSpec((tm, d), lambda j, i: (i, 0)),
                  pl.BlockSpec((d, tn), lambda j, i: (0, j)),
                  pl.BlockSpec((d, tn), lambda j, i: (0, j + nb))],
        out_specs=[out] * 3, out_shape=[jax.ShapeDtypeStruct((s, f), BF)] * 3,
        compiler_params=_cp("parallel", "parallel"), name=name)(h, wgu, wgu)


def _ffn_bwd_mid(dy, wd, gate, up, *, name):
    s, d = dy.shape
    f = wd.shape[0]
    tm = _tile(s, ROW_TILE, 8)
    tn = _tile(f, 1408, 128)

    def body(dy_ref, wd_ref, g_ref, u_ref, dg_ref, du_ref):
        dyv = dy_ref[...].astype(BF)
        for c0 in range(0, tn, 2 * LANES):
            cols = slice(c0, min(c0 + 2 * LANES, tn))
            dact = _dot_nt(dyv, wd_ref[cols, :])
            g = g_ref[:, cols].astype(F32)
            u = u_ref[:, cols].astype(F32)
            sig = jax.nn.sigmoid(g)
            silu = g * sig
            dg_ref[:, cols] = (dact * u * (sig + silu * (1.0 - sig))).astype(BF)
            du_ref[:, cols] = (dact * silu).astype(BF)

    blk = pl.BlockSpec((tm, tn), lambda j, i: (i, j))
    return pl.pallas_call(
        body, grid=(f // tn, s // tm),
        in_specs=[pl.BlockSpec((tm, d), lambda j, i: (i, 0)), pl.BlockSpec((tn, d), lambda j, i: (j, 0)), blk, blk],
        out_specs=[blk, blk], out_shape=[jax.ShapeDtypeStruct((s, f), BF)] * 2,
        compiler_params=_cp("parallel", "parallel"), name=name)(dy, wd, gate, up)


def _pad_cols(w, heads, dim, layout):
    k = w.shape[0]
    w3 = w.reshape(k, heads, dim)
    pieces, lane = [], 0
    for lane0, dim0, cnt in sorted(layout):
        if lane0 > lane:
            pieces.append(jnp.zeros((k, heads, lane0 - lane), w.dtype))
        pieces.append(w3[:, :, dim0:dim0 + cnt])
        lane = lane0 + cnt
    if lane < LANES:
        pieces.append(jnp.zeros((k, heads, LANES - lane), w.dtype))
    return jnp.concatenate(pieces, axis=2).reshape(k, heads * LANES)


def _unpad_cols(w, heads, dim, layout):
    k = w.shape[0]
    w3 = w.reshape(k, heads, LANES)
    by_dim = sorted(layout, key=lambda seg: seg[1])
    assert sum(cnt for _, _, cnt in by_dim) == dim
    return jnp.concatenate([w3[:, :, lane0:lane0 + cnt] for lane0, _, cnt in by_dim], axis=2).reshape(k, heads * dim)


def _pad_rows(w, heads, dim):
    n = w.shape[1]
    return jnp.pad(w.reshape(heads, dim, n), ((0, 0), (0, LANES - dim), (0, 0))).reshape(heads * LANES, n)


def _unpad_rows(w, heads, dim):
    n = w.shape[1]
    return w.reshape(heads, LANES, n)[:, :dim, :].reshape(heads * dim, n)


def _pad_vec(g, layout):
    return _pad_cols(g.reshape(1, -1), 1, g.shape[0], layout)


def _unpad_vec(g, dim, layout):
    return _unpad_cols(g.reshape(1, LANES), 1, dim, layout)[0]


def _local_step(x, positions, w, target):
    s = x.shape[0]
    cos_a, sin_a, cos_c, sin_c = _rope_tables(positions.reshape(s, 1), name="rope_tables")
    grads = {n: [None] * len(w[n]) for n in WEIGHTS}
    saved = []

    for i in range(DEPTH):
        kind, j = i % N_MIXERS, i // N_MIXERS
        tag = f"l{i}"
        if i == 0:
            h = _rmsnorm_fwd(x, w['mix_norm'][i], name=f"{tag}_mix_norm")
        fnorm = w['ffn_norm'][i]
        if kind == 0:
            nqk = (A_HEADS + A_KV_HEADS) * A_HEAD_DIM
            wqkv = jnp.concatenate(
                [_pad_cols(w['a_w_qkv'][j][:, :nqk], A_HEADS + A_KV_HEADS, A_HEAD_DIM, LAYOUT_A),
                 _pad_cols(w['a_w_qkv'][j][:, nqk:], A_KV_HEADS, A_HEAD_DIM, LAYOUT_V)], axis=1)
            wo = _pad_rows(w['a_w_o'][j], A_HEADS, A_HEAD_DIM)
            gq, gk = _pad_vec(w['a_q_norm'][j], LAYOUT_A), _pad_vec(w['a_k_norm'][j], LAYOUT_A)
            qkv = _mm([(h, wqkv, 0)], out_dtype=F32, name=f"{tag}_qkv")
            qa, ka, va = _prep_a_fwd(qkv, cos_a, sin_a, gq, gk, name=f"{tag}_prep")
            o, lse = _swa_fwd(qa, ka, va, w['a_sinks'][j], name=f"{tag}_attn")
            x1, h2 = _mm([(o, wo, 0)], out_dtype=F32, res=x, norm_g=fnorm, name=f"{tag}_wo")
            mix = dict(wqkv=wqkv, wo=wo, gq=gq, gk=gk, qkv=qkv, qa=qa, ka=ka, va=va, o=o, lse=lse)
        elif kind == 1:
            cw = jnp.pad(w['b_conv_w'][j], ((0, 5), (0, 0)))
            proj = _mm([(h, w['b_w_in'][j], 0)], out_dtype=F32, name=f"{tag}_win")
            yb = _conv_fwd(proj, cw, name=f"{tag}_conv")
            x1, h2 = _mm([(yb, w['b_w_out'][j], 0)], out_dtype=F32, res=x, norm_g=fnorm, name=f"{tag}_wout")
            mix = dict(cw=cw, proj=proj, yb=yb)
        else:
            wdn = w['c_w_down'][j]
            nqk = C_Q_RANK + C_KV_RANK
            wdn = jnp.concatenate([wdn[:, :nqk], _pad_cols(wdn[:, nqk:], 1, C_ROPE, LAYOUT_KR)], axis=1)
            wq = _pad_cols(w['c_w_q_up'][j], C_HEADS, C_QK, LAYOUT_C)
            wkv = w['c_w_kv_up'][j].reshape(C_KV_RANK, C_HEADS, C_NOPE + C_V)
            wkn = _pad_cols(wkv[:, :, :C_NOPE].reshape(C_KV_RANK, -1), C_HEADS, C_NOPE, LAYOUT_KN)
            wv = _pad_cols(wkv[:, :, C_NOPE:].reshape(C_KV_RANK, -1), C_HEADS, C_V, LAYOUT_V)
            wo = _pad_rows(w['c_w_o'][j], C_HEADS, C_V)
            gq, gk = _pad_vec(w['c_q_norm'][j], LAYOUT_C), _pad_vec(w['c_k_norm'][j], LAYOUT_C)
            gqa, gkva = w['c_q_a_norm'][j].reshape(1, -1), w['c_kv_a_norm'][j].reshape(1, -1)
            dn = _mm([(h, wdn, 0)], out_dtype=F32, name=f"{tag}_wdown")
            cqn, ckvn = _prep_c1_fwd(dn, gqa, gkva, name=f"{tag}_prep1")
            qraw = _mm([(cqn, wq, 0)], out_dtype=F32, name=f"{tag}_wq")
            knope = _mm([(ckvn, wkn, 0)], out_dtype=F32, name=f"{tag}_wkn")
            vc = _mm([(ckvn, wv, 0)], out_dtype=BF, name=f"{tag}_wv")
            qc, kc = _prep_c2_fwd(qraw, knope, dn, cos_c, sin_c, gq, gk, name=f"{tag}_prep2")
            o, lse = _mla_fwd(qc, kc, vc, name=f"{tag}_attn")
            x1, h2 = _mm([(o, wo, 0)], out_dtype=F32, res=x, norm_g=fnorm, name=f"{tag}_wo")
            mix = dict(wdn=wdn, wq=wq, wkn=wkn, wv=wv, wo=wo, gq=gq, gk=gk, gqa=gqa, gkva=gkva, dn=dn, cqn=cqn,
                       ckvn=ckvn, qraw=qraw, knope=knope, vc=vc, qc=qc, kc=kc, o=o, lse=lse)
        gate, up, act = _ffn_up(h2, w['f_w_gate_up'][i], name=f"{tag}_ffn_up")
        saved.append(dict(x=x, h=h, x1=x1, h2=h2, gate=gate, up=up, act=act, mix=mix))
        if i + 1 < DEPTH:
            x, h = _mm([(act, w['f_w_down'][i], 0)], out_dtype=F32, res=x1, norm_g=w['mix_norm'][i + 1],
                       name=f"{tag}_ffn_down")
        else:
            x = _mm([(act, w['f_w_down'][i], 0)], out_dtype=F32, res=x1, name=f"{tag}_ffn_down")

    loss_blk, dx = _loss_fwd_bwd(x, target, name="loss")

    for i in reversed(range(DEPTH)):
        kind, j = i % N_MIXERS, i // N_MIXERS
        tag = f"l{i}b"
        sv = saved[i]
        mix = sv['mix']
        wgu, wd = w['f_w_gate_up'][i], w['f_w_down'][i]
        grads['f_w_down'][i] = _mm_tn(sv['act'], dx, name=f"{tag}_dwd")
        dgate, dup = _ffn_bwd_mid(dx, wd, sv['gate'], sv['up'], name=f"{tag}_ffn_mid")
        grads['f_w_gate_up'][i] = (_mm_tn(sv['h2'], dgate, name=f"{tag}_dwg"),
                                   _mm_tn(sv['h2'], dup, name=f"{tag}_dwu"))
        dx, dg = _mm_norm_bwd([(dgate, wgu, 0), (dup, wgu, 1)], sv['x1'], w['ffn_norm'][i], dx, name=f"{tag}_dh2")
        grads['ffn_norm'][i] = dg[0]
        if kind == 0:
            grads['a_w_o'][j] = _unpad_rows(_mm_tn(mix['o'], dx, name=f"{tag}_dwo"), A_HEADS, A_HEAD_DIM)
            do = _mm([(dx, mix['wo'], 0)], out_dtype=BF, trans_b=True, name=f"{tag}_do")
            dqa, dka, dva, dsinks = _swa_bwd(mix['qa'], mix['ka'], mix['va'], mix['o'], do, mix['lse'],
                                             w['a_sinks'][j], name=f"{tag}_attn")
            dqkv, dgq, dgk = _prep_a_bwd(mix['qkv'], dqa, dka, dva, cos_a, sin_a, mix['gq'], mix['gk'],
                                         name=f"{tag}_prep")
            grads['a_sinks'][j] = dsinks
            grads['a_q_norm'][j] = _unpad_vec(dgq, A_HEAD_DIM, LAYOUT_A)
            grads['a_k_norm'][j] = _unpad_vec(dgk, A_HEAD_DIM, LAYOUT_A)
            dwqkv = _mm_tn(sv['h'], dqkv, name=f"{tag}_dwqkv")
            nqk = (A_HEADS + A_KV_HEADS) * LANES
            grads['a_w_qkv'][j] = jnp.concatenate(
                [_unpad_cols(dwqkv[:, :nqk], A_HEADS + A_KV_HEADS, A_HEAD_DIM, LAYOUT_A),
                 _unpad_cols(dwqkv[:, nqk:], A_KV_HEADS, A_HEAD_DIM, LAYOUT_V)], axis=1)
            dh_pairs = [(dqkv, mix['wqkv'], 0)]
        elif kind == 1:
            grads['b_w_out'][j] = _mm_tn(mix['yb'], dx, name=f"{tag}_dwout")
            dyb = _mm([(dx, w['b_w_out'][j], 0)], out_dtype=F32, trans_b=True, name=f"{tag}_dyb")
            dproj, dcw = _conv_bwd(mix['proj'], dyb, mix['cw'], name=f"{tag}_conv")
            grads['b_conv_w'][j] = dcw
            grads['b_w_in'][j] = _mm_tn(sv['h'], dproj, name=f"{tag}_dwin")
            dh_pairs = [(dproj, w['b_w_in'][j], 0)]
        else:
            grads['c_w_o'][j] = _unpad_rows(_mm_tn(mix['o'], dx, name=f"{tag}_dwo"), C_HEADS, C_V)
            do = _mm([(dx, mix['wo'], 0)], out_dtype=BF, trans_b=True, name=f"{tag}_do")
            delta = _mla_delta(mix['o'], do, name=f"{tag}_delta")
            dqc, dkc, dvc = _mla_bwd(mix['qc'], mix['kc'], mix['vc'], do, mix['lse'], delta, name=f"{tag}_attn")
            dqraw, dknope, dkr, dgq, dgk = _prep_c2_bwd(mix['qraw'], mix['knope'], mix['dn'], dqc, dkc, cos_c, sin_c,
                                                        mix['gq'], mix['gk'], name=f"{tag}_prep2")
            grads['c_q_norm'][j] = _unpad_vec(dgq, C_QK, LAYOUT_C)
            grads['c_k_norm'][j] = _unpad_vec(dgk, C_QK, LAYOUT_C)
            grads['c_w_q_up'][j] = _unpad_cols(_mm_tn(mix['cqn'], dqraw, name=f"{tag}_dwq"), C_HEADS, C_QK,
                                               LAYOUT_C)
            dwkn = _unpad_cols(_mm_tn(mix['ckvn'], dknope, name=f"{tag}_dwkn"), C_HEADS, C_NOPE, LAYOUT_KN)
            dwv = _unpad_cols(_mm_tn(mix['ckvn'], dvc, name=f"{tag}_dwv"), C_HEADS, C_V, LAYOUT_V)
            grads['c_w_kv_up'][j] = jnp.concatenate(
                [dwkn.reshape(C_KV_RANK, C_HEADS, C_NOPE), dwv.reshape(C_KV_RANK, C_HEADS, C_V)], axis=2).reshape(
                C_KV_RANK, -1)
            dcq = _mm([(dqraw, mix['wq'], 0)], out_dtype=F32, trans_b=True, name=f"{tag}_dcq")
            dckv = _mm([(dknope, mix['wkn'], 0), (dvc, mix['wv'], 0)], out_dtype=F32, trans_b=True,
                       name=f"{tag}_dckv")
            ddn, dgqa, dgkva = _prep_c1_bwd(mix['dn'], dcq, dckv, dkr, mix['gqa'], mix['gkva'], name=f"{tag}_prep1")
            grads['c_q_a_norm'][j] = dgqa[0]
            grads['c_kv_a_norm'][j] = dgkva[0]
            dwdn = _mm_tn(sv['h'], ddn, name=f"{tag}_dwdown")
            nqk = C_Q_RANK + C_KV_RANK
            grads['c_w_down'][j] = jnp.concatenate(
                [dwdn[:, :nqk], _unpad_cols(dwdn[:, nqk:], 1, C_ROPE, LAYOUT_KR)], axis=1)
            dh_pairs = [(ddn, mix['wdn'], 0)]
        dx, dg = _mm_norm_bwd(dh_pairs, sv['x'], w['mix_norm'][i], dx, name=f"{tag}_dh")
        grads['mix_norm'][i] = dg[0]

    return loss_blk, dx, {n: (g if n in BIG else jnp.stack(g)) for n, g in grads.items()}


def _my_place():
    return lax.axis_index("x"), lax.axis_index("y"), lax.axis_index("c")


def _flips(x, y):
    return [(1 - x, y), (x, 1 - y), (1 - x, 1 - y)]


def _gather_weights(big, small):
    half = big.shape[0] // 2
    chunk = half // GATHER_CHUNKS

    def body(big_ref, small_ref, bout_ref, sout_ref, send_b, recv_b, send_f, recv_f, send_s, recv_s, loc):
        x, y, c = _my_place()
        me = 2 * x + y

        def rows(core, ch):
            return pl.ds(pl.multiple_of(core * half + ch * chunk, 16), chunk)

        def ici(k, ch, chip, to):
            return pltpu.make_async_remote_copy(
                src_ref=big_ref.at[rows(c, ch), :], dst_ref=bout_ref.at[chip, rows(c, ch), :],
                send_sem=send_b.at[k * GATHER_CHUNKS + ch], recv_sem=recv_b.at[k * GATHER_CHUNKS + ch],
                device_id=to, device_id_type=MESH)

        def forward(k, ch, chip, core):
            return pltpu.make_async_remote_copy(
                src_ref=bout_ref.at[chip, rows(core, ch), :], dst_ref=bout_ref.at[chip, rows(core, ch), :],
                send_sem=send_f.at[k * GATHER_CHUNKS + ch], recv_sem=recv_f.at[k * GATHER_CHUNKS + ch],
                device_id=(x, y, 1 - c), device_id_type=MESH)

        def small_copy(k, chip, to):
            return pltpu.make_async_remote_copy(src_ref=small_ref, dst_ref=sout_ref.at[chip],
                                                send_sem=send_s.at[k], recv_sem=recv_s.at[k],
                                                device_id=to, device_id_type=MESH)

        own = (pltpu.make_async_copy(big_ref, bout_ref.at[me], loc.at[0]),
               pltpu.make_async_copy(small_ref, sout_ref.at[me], loc.at[1]))
        for cp in own:
            cp.start()
        chips = _flips(x, y)
        every = [(ch, k, px, py) for ch in range(GATHER_CHUNKS) for k, (px, py) in enumerate(chips)]
        sends = [ici(k, ch, me, (px, py, c)) for ch, k, px, py in every]
        sends += [small_copy(k, me, (px, py, c)) for k, (px, py) in enumerate(chips)]
        for cp in sends:
            cp.start()
        passed = []
        for ch, k, px, py in every:
            ici(k, ch, 2 * px + py, (x, y, c)).wait_recv()
            passed.append(forward(k, ch, 2 * px + py, c))
            passed[-1].start()
        for ch, k, px, py in every:
            forward(k, ch, 2 * px + py, 1 - c).wait_recv()
        for k, (px, py) in enumerate(chips):
            small_copy(k, 2 * px + py, (x, y, c)).wait_recv()
        for cp in sends + passed:
            cp.wait_send()
        for cp in own:
            cp.wait()

    hbm = pl.BlockSpec(memory_space=pltpu.HBM)
    sem3 = pltpu.SemaphoreType.DMA((3,))
    semc = pltpu.SemaphoreType.DMA((3 * GATHER_CHUNKS,))
    return pl.pallas_call(
        body, in_specs=[hbm, hbm], out_specs=[hbm, hbm],
        out_shape=[jax.ShapeDtypeStruct((N_CHIPS,) + big.shape, big.dtype),
                   jax.ShapeDtypeStruct((N_CHIPS,) + small.shape, small.dtype)],
        scratch_shapes=[semc, semc, semc, semc, sem3, sem3, pltpu.SemaphoreType.DMA((2,))],
        name="gather_weights")(big, small)


def _exchange_grads(gbig, gsmall):
    def body(gbig_ref, gsmall_ref, got_ref, ssum_ref, sbuf, send_b, recv_b, send_s, recv_s):
        x, y, c = _my_place()
        me = 4 * x + 2 * y + c

        def big_copy(k, src_chip, to):
            return pltpu.make_async_remote_copy(src_ref=gbig_ref.at[src_chip], dst_ref=got_ref.at[k],
                                                send_sem=send_b.at[k], recv_sem=recv_b.at[k],
                                                device_id=to, device_id_type=MESH)

        def small_copy(k, slot, to):
            return pltpu.make_async_remote_copy(src_ref=gsmall_ref, dst_ref=sbuf.at[slot],
                                                send_sem=send_s.at[k], recv_sem=recv_s.at[k],
                                                device_id=to, device_id_type=MESH)

        peers = [(x ^ (k >> 2), y ^ ((k >> 1) & 1), c ^ (k & 1)) for k in range(1, N_DEV)]
        bigs = [big_copy(k, 2 * px + py, (px, py, c)) for k, (px, py) in enumerate(_flips(x, y))]
        smalls = [small_copy(k, me, p) for k, p in enumerate(peers)]
        for cp in bigs + smalls:
            cp.start()
        sbuf[me] = gsmall_ref[...]
        for k, (px, py, pc) in enumerate(peers):
            small_copy(k, 4 * px + 2 * py + pc, (x, y, c)).wait_recv()
        acc = sbuf[0]
        for d in range(1, N_DEV):
            acc = acc + sbuf[d]
        ssum_ref[...] = acc
        for k in range(3):
            big_copy(k, 0, (x, y, c)).wait_recv()
        for cp in bigs + smalls:
            cp.wait_send()

    hbm = pl.BlockSpec(memory_space=pltpu.HBM)
    vmem = pl.BlockSpec(memory_space=pltpu.VMEM)
    return pl.pallas_call(
        body, in_specs=[hbm, vmem], out_specs=[hbm, vmem],
        out_shape=[jax.ShapeDtypeStruct((3,) + gbig.shape[1:], gbig.dtype),
                   jax.ShapeDtypeStruct(gsmall.shape, gsmall.dtype)],
        scratch_shapes=[pltpu.VMEM((N_DEV,) + gsmall.shape, gsmall.dtype),
                        pltpu.SemaphoreType.DMA((3,)), pltpu.SemaphoreType.DMA((3,)),
                        pltpu.SemaphoreType.DMA((N_DEV - 1,)), pltpu.SemaphoreType.DMA((N_DEV - 1,))],
        name="exchange_grads")(gbig, gsmall)


def _sum_shards(chip, mine, got):
    r, cols = mine.shape[1:]
    tr = _tile(r, PACK_TILE_ROWS, 16)

    def body(chip_ref, mine_ref, got_ref, o_ref):
        o_ref[...] = ((mine_ref[0] + got_ref[0].astype(F32)) + got_ref[1].astype(F32)) + got_ref[2].astype(F32)

    grid_spec = pltpu.PrefetchScalarGridSpec(
        num_scalar_prefetch=1, grid=(r // tr,),
        in_specs=[pl.BlockSpec((1, tr, cols), lambda i, chip_ref: (chip_ref[0], i, 0)),
                  pl.BlockSpec((3, tr, cols), lambda i, chip_ref: (0, i, 0))],
        out_specs=pl.BlockSpec((tr, cols), lambda i, chip_ref: (i, 0)))
    return pl.pallas_call(
        body, grid_spec=grid_spec, out_shape=jax.ShapeDtypeStruct((r, cols), F32),
        compiler_params=_cp("parallel"), name="sum_shards")(chip, mine, got)


def _swap_sibling(t, *, name):
    def body(t_ref, got_ref, send_sem, recv_sem):
        x, y, c = _my_place()
        cp = pltpu.make_async_remote_copy(src_ref=t_ref, dst_ref=got_ref, send_sem=send_sem, recv_sem=recv_sem,
                                          device_id=(x, y, 1 - c), device_id_type=MESH)
        cp.start()
        cp.wait()

    hbm = pl.BlockSpec(memory_space=pltpu.HBM)
    return pl.pallas_call(
        body, in_specs=[hbm], out_specs=hbm, out_shape=jax.ShapeDtypeStruct(t.shape, t.dtype),
        scratch_shapes=[pltpu.SemaphoreType.DMA, pltpu.SemaphoreType.DMA], name=name)(t)


def _give_halves(parts, *, name):
    n = len(parts)
    r, cols = parts[0].shape
    half = r // 2

    def body(*refs):
        got_ref, send_sems, recv_sems = refs[n:]
        x, y, c = _my_place()
        theirs = pl.ds(pl.multiple_of((1 - c) * half, 16), half)
        copies = [pltpu.make_async_remote_copy(src_ref=refs[k].at[theirs, :], dst_ref=got_ref.at[k],
                                               send_sem=send_sems.at[k], recv_sem=recv_sems.at[k],
                                               device_id=(x, y, 1 - c), device_id_type=MESH) for k in range(n)]
        for cp in copies:
            cp.start()
        for cp in copies:
            cp.wait()

    hbm = pl.BlockSpec(memory_space=pltpu.HBM)
    return pl.pallas_call(
        body, in_specs=[hbm] * n, out_specs=hbm, out_shape=jax.ShapeDtypeStruct((n, half, cols), parts[0].dtype),
        scratch_shapes=[pltpu.SemaphoreType.DMA((n,)), pltpu.SemaphoreType.DMA((n,))], name=name)(*parts)


def _add_kept(core, parts, got, *, name):
    n, r, cols = got.shape
    tr = _tile(r, PACK_TILE_ROWS, 16)
    nblk = r // tr

    def body(core_ref, *refs):
        b_ref, o_ref, ob_ref = refs[n:]
        for k in range(n):
            sm = refs[k][...] + b_ref[k]
            o_ref[k] = sm
            ob_ref[k] = sm.astype(BF)

    blk = pl.BlockSpec((n, tr, cols), lambda i, core_ref: (0, i, 0))
    kept = pl.BlockSpec((tr, cols), lambda i, core_ref: (core_ref[0] * nblk + i, 0))
    grid_spec = pltpu.PrefetchScalarGridSpec(
        num_scalar_prefetch=1, grid=(nblk,), in_specs=[kept] * n + [blk], out_specs=[blk, blk])
    return pl.pallas_call(
        body, grid_spec=grid_spec,
        out_shape=[jax.ShapeDtypeStruct(got.shape, F32), jax.ShapeDtypeStruct(got.shape, BF)],
        compiler_params=_cp("parallel"), name=name)(core, *parts, got)


def _by_core(c, mine, sibling):
    return jnp.where(c == 0, mine, sibling), jnp.where(c == 0, sibling, mine)


def _adamw(wt, m, v, g, *, name):
    r, cols = wt.shape
    tr = _tile(r, 256, 8)

    def body(w_ref, m_ref, v_ref, g_ref, d_ref, nm_ref, nv_ref):
        gv = g_ref[...]
        nm = ADAM_B1 * m_ref[...] + (1.0 - ADAM_B1) * gv
        nv = ADAM_B2 * v_ref[...] + (1.0 - ADAM_B2) * (gv * gv)
        m_hat = nm / (1.0 - ADAM_B1 ** ADAM_STEP)
        v_hat = nv / (1.0 - ADAM_B2 ** ADAM_STEP)
        d_ref[...] = -ADAM_LR * (m_hat / (jnp.sqrt(v_hat) + ADAM_EPS) + ADAM_WD * w_ref[...])
        nm_ref[...] = nm
        nv_ref[...] = nv

    blk = pl.BlockSpec((tr, cols), lambda i: (i, 0))
    return pl.pallas_call(
        body, grid=(r // tr,), in_specs=[blk] * 4, out_specs=[blk] * 3,
        out_shape=[jax.ShapeDtypeStruct((r, cols), F32)] * 3,
        compiler_params=_cp("parallel"), name=name)(wt, m, v, g)


def _shard_shape(full, axis):
    return tuple(d // N_CHIPS if a == axis else d for a, d in enumerate(full))


def _pack_rows(n):
    rows = -(-n // PACK_COLS)
    return -(-rows // PACK_ROW_MULT) * PACK_ROW_MULT


def _pack(parts, rows, dtype):
    mat = jnp.concatenate([p.astype(dtype).reshape(-1, PACK_COLS) for p in parts], axis=0)
    return jnp.pad(mat, ((0, rows - mat.shape[0]), (0, 0)))


def _join_shards(stacked, axis):
    moved = jnp.moveaxis(stacked, 0, axis)
    shp = moved.shape
    return moved.reshape(shp[:axis] + (shp[axis] * shp[axis + 1],) + shp[axis + 2:])


def _split_shards(full, axis):
    shp = full.shape
    split = full.reshape(shp[:axis] + (N_CHIPS, shp[axis] // N_CHIPS) + shp[axis + 1:])
    return jnp.moveaxis(split, axis, 0)


SMALL_ROWS = 128
SMALL_LAYOUT = [('mix_norm', DEPTH * D_MODEL), ('ffn_norm', DEPTH * D_MODEL), ('b_conv_w', 3 * D_MODEL),
                ('c_q_a_norm', C_Q_RANK), ('c_kv_a_norm', C_KV_RANK), ('a_q_norm', 2 * A_HEAD_DIM),
                ('a_k_norm', 2 * A_HEAD_DIM), ('a_sinks', 2 * A_HEADS), ('c_q_norm', C_QK), ('c_k_norm', C_QK),
                ('loss', 1)]


def _small_offsets():
    offs, row = {}, 0
    for name, n in SMALL_LAYOUT:
        offs[name] = (row * LANES, n)
        row += -(-n // LANES)
    assert row <= SMALL_ROWS
    return offs


def kernel(x, positions, mix_norm, ffn_norm, a_w_qkv, a_q_norm, a_k_norm, a_sinks, a_w_o, b_w_in, b_conv_w, b_w_out, c_w_down, c_q_a_norm, c_kv_a_norm, c_w_q_up, c_w_kv_up, c_q_norm, c_k_norm, c_w_o, f_w_gate_up, f_w_down, loss_target, m_mix_norm, m_ffn_norm, m_a_w_qkv, m_a_q_norm, m_a_k_norm, m_a_sinks, m_a_w_o, m_b_w_in, m_b_conv_w, m_b_w_out, m_c_w_down, m_c_q_a_norm, m_c_kv_a_norm, m_c_w_q_up, m_c_w_kv_up, m_c_q_norm, m_c_k_norm, m_c_w_o, m_f_w_gate_up, m_f_w_down, v_mix_norm, v_ffn_norm, v_a_w_qkv, v_a_q_norm, v_a_k_norm, v_a_sinks, v_a_w_o, v_b_w_in, v_b_conv_w, v_b_w_out, v_c_w_down, v_c_q_a_norm, v_c_kv_a_norm, v_c_w_q_up, v_c_w_kv_up, v_c_q_norm, v_c_k_norm, v_c_w_o, v_f_w_gate_up, v_f_w_down):
    args = dict(locals())
    wshard = {n: args[n] for n in WEIGHTS}
    sharded = {**BIG, **SMALL_SHARDED}
    chip = 2 * lax.axis_index("x") + lax.axis_index("y")

    n_big = sum(wshard[n].size for n in BIG)
    rows = _pack_rows(n_big)
    big = _pack([wshard[n] for n in BIG], rows, BF)
    small = jnp.concatenate([wshard[n].reshape(-1) for n in SMALL_SHARDED])
    small = jnp.pad(small, (0, 8 * LANES - small.shape[0])).reshape(8, LANES)
    core = lax.axis_index("c")
    big_all, small_all = _gather_weights(big, small)
    big_all = big_all.reshape(N_CHIPS, -1)
    small_all = small_all.reshape(N_CHIPS, -1)
    w = {}
    off = 0
    for n, axis in BIG.items():
        shp = wshard[n].shape
        per = shp[1] * shp[2]
        w[n] = [_join_shards(big_all[:, off + l * per:off + (l + 1) * per].reshape((N_CHIPS,) + shp[1:]), axis - 1)
                for l in range(shp[0])]
        off += wshard[n].size
    off = 0
    for n, axis in SMALL_SHARDED.items():
        sz = wshard[n].size
        w[n] = _join_shards(small_all[:, off:off + sz].reshape((N_CHIPS,) + wshard[n].shape), axis)
        off += sz
    for n in WEIGHTS:
        if n not in sharded:
            w[n] = wshard[n]

    loss_blk, grad_x, g = _local_step(x[0], positions[0], w, loss_target[0])

    def shard_of(n, item, k):
        if isinstance(item, tuple):
            item = item[k // 2]
            k = k % 2
            width = item.shape[1] // 2
        else:
            width = item.shape[BIG[n] - 1] // N_CHIPS
        return item[:, k * width:(k + 1) * width] if BIG[n] == 2 else item[k * width:(k + 1) * width]

    gparts = [_pack([shard_of(n, item, k) for n in BIG for item in g[n]], rows, F32) for k in range(N_CHIPS)]
    offs = _small_offsets()
    smalls = {**{n: g[n] for n, _ in SMALL_LAYOUT if n != 'loss'}, 'loss': loss_blk[0:1, 0:1]}
    gsmall = jnp.concatenate(
        [jnp.pad(smalls[n].reshape(-1), (0, -cnt % LANES)) for n, cnt in SMALL_LAYOUT])
    gsmall = jnp.pad(gsmall, (0, SMALL_ROWS * LANES - gsmall.shape[0])).reshape(SMALL_ROWS, LANES)
    from_sibling = _give_halves(gparts, name="swap_grads")
    chip_sum, chip_sum_bf = _add_kept(core.reshape(1).astype(jnp.int32), gparts, from_sibling, name="add_sibling")
    got, ssum = _exchange_grads(chip_sum_bf, gsmall)
    part = _sum_shards(chip.reshape(1).astype(jnp.int32), chip_sum, got)
    part_sib = _swap_sibling(part, name="swap_sums")
    gflat = jnp.concatenate(_by_core(core, part, part_sib), axis=0).reshape(-1)

    ssum = ssum.reshape(-1)
    outs = {}
    off = 0
    for n in WEIGHTS:
        wt = wshard[n]
        if n in BIG:
            grad = gflat[off:off + wt.size]
            off += wt.size
        else:
            o0, cnt = offs[n]
            grad = ssum[o0:o0 + cnt].reshape(g[n].shape)
            if n in SMALL_SHARDED:
                grad = lax.dynamic_index_in_dim(_split_shards(grad, SMALL_SHARDED[n]), chip, 0, keepdims=False)
        shape2 = (-1, wt.shape[-1])
        grad = grad.reshape(shape2)
        res = _adamw(wt.reshape(shape2), args['m_' + n].reshape(shape2), args['v_' + n].reshape(shape2), grad,
                     name=f"adamw_{n}")
        outs[n] = [r.reshape(wt.shape) for r in [grad] + list(res)]
    loss = ssum[offs['loss'][0]]
    return (loss, grad_x[None], *[outs[n][0] for n in WEIGHTS], *[outs[n][1] for n in WEIGHTS],
            *[outs[n][2] for n in WEIGHTS], *[outs[n][3] for n in WEIGHTS])
```

```python
import functools

import jax
import jax.numpy as jnp
from jax import lax
from jax.experimental import pallas as pl
from jax.experimental.pallas import tpu as pltpu

F32 = jnp.float32
BF = jnp.bfloat16

D_MODEL = 1024
DEPTH = 4
N_MIXERS = 3
ROPE_THETA = 500000.0
EPS = 1e-6
BLOCK = 128
LANES = 128

A_HEADS, A_KV_HEADS, A_HEAD_DIM = 16, 4, 64
A_ROT_DIM = A_HEAD_DIM // 4
A_GROUP = A_HEADS // A_KV_HEADS
C_HEADS, C_NOPE, C_ROPE, C_V, C_Q_RANK, C_KV_RANK = 16, 64, 32, 64, 384, 256
C_QK = C_NOPE + C_ROPE
D_FF = 2816

ADAM_LR, ADAM_B1, ADAM_B2, ADAM_EPS, ADAM_WD, ADAM_STEP = 0.001, 0.9, 0.999, 1e-08, 0.01, 10

N_CHIPS = 4
N_DEV = 8
MESH = pl.DeviceIdType.MESH

VMEM_LIMIT = 56 * 1024 * 1024
ROW_TILE = 512
PREP_TILE = 256
ATTN_TILE = 512
FWD_WIDE = 4
SWA_BLOCKS = 8
NEG = -1e30

WEIGHTS = ['mix_norm', 'ffn_norm', 'a_w_qkv', 'a_q_norm', 'a_k_norm', 'a_sinks', 'a_w_o', 'b_w_in', 'b_conv_w',
           'b_w_out', 'c_w_down', 'c_q_a_norm', 'c_kv_a_norm', 'c_w_q_up', 'c_w_kv_up', 'c_q_norm', 'c_k_norm',
           'c_w_o', 'f_w_gate_up', 'f_w_down']
BIG = {'a_w_qkv': 2, 'a_w_o': 1, 'b_w_in': 2, 'b_w_out': 1, 'c_w_down': 1, 'c_w_q_up': 2, 'c_w_kv_up': 2,
       'c_w_o': 1, 'f_w_gate_up': 2, 'f_w_down': 1}
SMALL_SHARDED = {'b_conv_w': 2, 'c_q_a_norm': 1, 'c_kv_a_norm': 1}
PACK_COLS = 256
PACK_ROW_MULT = 32
PACK_TILE_ROWS = 1024
GATHER_CHUNKS = 3


def _cp(*sem):
    return pltpu.CompilerParams(dimension_semantics=sem, vmem_limit_bytes=VMEM_LIMIT)


def _tile(n, target, mult):
    if n <= target:
        return n
    t = (target // mult) * mult
    while t >= mult:
        if n % t == 0:
            return t
        t -= mult
    raise ValueError(f"no tile for {n}")


def _dot(a, b):
    return lax.dot_general(a, b, (((1,), (0,)), ((), ())), preferred_element_type=F32)


def _dot_nt(a, b):
    return lax.dot_general(a, b, (((1,), (1,)), ((), ())), preferred_element_type=F32)


def _dot_tn(a, b):
    return lax.dot_general(a, b, (((0,), (0,)), ((), ())), preferred_element_type=F32)


def _mm(pairs, *, out_dtype, name, res=None, trans_b=False, norm_g=None):
    a0, b0, _ = pairs[0]
    m = a0.shape[0]
    n = b0.shape[0] if trans_b else b0.shape[1]
    tm = _tile(m, 2 * ROW_TILE if sum(a.shape[1] for a, _, _ in pairs) <= 1024 else ROW_TILE, 8)
    tn = _tile(n, 1024, 128)
    n_pairs = len(pairs)
    has_res = res is not None
    has_norm = norm_g is not None
    assert not has_norm or tn == n

    def body(*refs):
        acc = None
        for p in range(n_pairs):
            a = refs[2 * p][...].astype(BF)
            b = refs[2 * p + 1][...].astype(BF)
            d = _dot_nt(a, b) if trans_b else _dot(a, b)
            acc = d if acc is None else acc + d
        nxt = 2 * n_pairs
        if has_res:
            acc = acc + refs[nxt][...]
            nxt += 1
        if has_norm:
            y = acc * lax.rsqrt(jnp.mean(acc * acc, axis=-1, keepdims=True) + EPS)
            refs[-1][...] = (y * refs[nxt][...]).astype(BF)
            refs[-2][...] = acc.astype(out_dtype)
        else:
            refs[-1][...] = acc.astype(out_dtype)

    in_specs, args = [], []
    for a, b, kblk in pairs:
        k = a.shape[1]
        in_specs.append(pl.BlockSpec((tm, k), lambda j, i: (i, 0)))
        if trans_b:
            in_specs.append(pl.BlockSpec((tn, k), functools.partial(lambda j, i, kb: (j, kb), kb=kblk)))
        else:
            in_specs.append(pl.BlockSpec((k, tn), lambda j, i: (0, j)))
        args += [a, b]
    if has_res:
        in_specs.append(pl.BlockSpec((tm, tn), lambda j, i: (i, j)))
        args.append(res)
    out_spec = pl.BlockSpec((tm, tn), lambda j, i: (i, j))
    out_specs, out_shape = out_spec, jax.ShapeDtypeStruct((m, n), out_dtype)
    if has_norm:
        in_specs.append(pl.BlockSpec((1, n), lambda j, i: (0, 0)))
        args.append(norm_g.reshape(1, n))
        out_specs, out_shape = [out_spec, out_spec], [out_shape, jax.ShapeDtypeStruct((m, n), BF)]
    return pl.pallas_call(
        body, grid=(n // tn, m // tm), in_specs=in_specs, out_specs=out_specs, out_shape=out_shape,
        compiler_params=_cp("parallel", "parallel"), name=name)(*args)


def _mm_norm_bwd(pairs, x, g, dres, *, name):
    m, d = x.shape
    tm = _tile(m, ROW_TILE, 8)
    n_pairs = len(pairs)

    def body(*refs):
        x_ref, g_ref, dres_ref, dx_ref, dg_ref = refs[2 * n_pairs:]

        @pl.when(pl.program_id(0) == 0)
        def _():
            dg_ref[...] = jnp.zeros_like(dg_ref)

        dh = None
        for p in range(n_pairs):
            dpart = _dot_nt(refs[2 * p][...].astype(BF), refs[2 * p + 1][...].astype(BF))
            dh = dpart if dh is None else dh + dpart
        dx, xhat = _rms_bwd_math(x_ref[...], g_ref[...], dh, d)
        dx_ref[...] = dres_ref[...] + dx
        dg_ref[0:1, :] += jnp.sum(dh * xhat, axis=0, keepdims=True)

    in_specs, args = [], []
    for a, b, kblk in pairs:
        k = a.shape[1]
        in_specs.append(pl.BlockSpec((tm, k), lambda i: (i, 0)))
        in_specs.append(pl.BlockSpec((d, k), functools.partial(lambda i, kb: (0, kb), kb=kblk)))
        args += [a, b]
    row = pl.BlockSpec((tm, d), lambda i: (i, 0))
    in_specs += [row, pl.BlockSpec((1, d), lambda i: (0, 0)), row]
    dx, dg = pl.pallas_call(
        body, grid=(m // tm,), in_specs=in_specs,
        out_specs=[row, pl.BlockSpec((8, d), lambda i: (0, 0))],
        out_shape=[jax.ShapeDtypeStruct((m, d), F32), jax.ShapeDtypeStruct((8, d), F32)],
        compiler_params=_cp("arbitrary"), name=name)(*args, x, g.reshape(1, d), dres)
    return dx, dg[0:1]


def _mm_tn(a, b, *, name):
    m, k = a.shape
    n = b.shape[1]
    tm = _tile(m, 2 * ROW_TILE, 8)
    tk = _tile(k, 1408, 128)
    tn = _tile(n, 1408, 128)

    def body(a_ref, b_ref, o_ref):
        @pl.when(pl.program_id(2) == 0)
        def _():
            o_ref[...] = jnp.zeros_like(o_ref)

        o_ref[...] += _dot_tn(a_ref[...].astype(BF), b_ref[...].astype(BF))

    return pl.pallas_call(
        body, grid=(k // tk, n // tn, m // tm),
        in_specs=[pl.BlockSpec((tm, tk), lambda kk, j, i: (i, kk)),
                  pl.BlockSpec((tm, tn), lambda kk, j, i: (i, j))],
        out_specs=pl.BlockSpec((tk, tn), lambda kk, j, i: (kk, j)),
        out_shape=jax.ShapeDtypeStruct((k, n), F32),
        compiler_params=_cp("parallel", "parallel", "arbitrary"), name=name)(a, b)


def _rmsnorm_fwd(x, g, *, name):
    s, d = x.shape
    tm = _tile(s, ROW_TILE, 8)

    def body(x_ref, g_ref, h_ref):
        xv = x_ref[...]
        y = xv * lax.rsqrt(jnp.mean(xv * xv, axis=-1, keepdims=True) + EPS)
        h_ref[...] = (y * g_ref[...]).astype(BF)

    return pl.pallas_call(
        body, grid=(s // tm,),
        in_specs=[pl.BlockSpec((tm, d), lambda i: (i, 0)), pl.BlockSpec((1, d), lambda i: (0, 0))],
        out_specs=pl.BlockSpec((tm, d), lambda i: (i, 0)),
        out_shape=jax.ShapeDtypeStruct((s, d), BF),
        compiler_params=_cp("parallel"), name=name)(x, g.reshape(1, d))


def _row_sum(v):
    return jnp.sum(v, axis=-1, keepdims=True)


def _slot_sum(v):
    ones = jnp.ones((LANES, LANES), BF)
    hi = v.astype(BF)
    lo = (v - hi.astype(F32)).astype(BF)
    return _dot(hi, ones) + _dot(lo, ones)


def _rms_bwd_math(xv, g, dh, n, row_sum=_row_sum):
    rstd = lax.rsqrt(row_sum(xv * xv) * (1.0 / n) + EPS)
    xhat = xv * rstd
    dxh = dh * g
    dx = rstd * (dxh - xhat * (row_sum(dxh * xhat) * (1.0 / n)))
    return dx, xhat


def _loss_fwd_bwd(y, target, *, name):
    s, d = y.shape
    tm = _tile(s, ROW_TILE, 8)

    def body(y_ref, t_ref, loss_ref, dy_ref):
        @pl.when(pl.program_id(0) == 0)
        def _():
            loss_ref[...] = jnp.zeros_like(loss_ref)

        err = y_ref[...] - t_ref[...]
        dy_ref[...] = err * (1.0 / d)
        loss_ref[...] += 0.5 * jnp.sum(jnp.sum(err * err, axis=-1, keepdims=True) * (1.0 / d))

    row = pl.BlockSpec((tm, d), lambda i: (i, 0))
    return pl.pallas_call(
        body, grid=(s // tm,), in_specs=[row, row],
        out_specs=[pl.BlockSpec((8, LANES), lambda i: (0, 0)), row],
        out_shape=[jax.ShapeDtypeStruct((8, LANES), F32), jax.ShapeDtypeStruct((s, d), F32)],
        compiler_params=_cp("arbitrary"), name=name)(y, target)


HALF = LANES // 2
A_HR, C_HR = A_ROT_DIM // 2, C_ROPE // 2
A_X1, C_X1 = 0, 32
LAYOUT_A = [(A_X1, 0, A_HR), (A_HR, A_ROT_DIM, A_HEAD_DIM - A_ROT_DIM), (HALF + A_X1, A_HR, A_HR)]
LAYOUT_C = [(0, 0, 32), (C_X1, C_NOPE, C_HR), (HALF, 32, 32), (HALF + C_X1, C_NOPE + C_HR, C_HR)]
LAYOUT_KN = [(0, 0, 32), (HALF, 32, 32)]
LAYOUT_KR = [(C_X1, 0, C_HR), (HALF + C_X1, C_HR, C_HR)]
LAYOUT_V = [(0, 0, C_V)]


def _rope_rows():
    lane = jnp.arange(LANES)
    fa = ROPE_THETA ** (-jnp.arange(0, A_ROT_DIM, 2, dtype=F32) / A_ROT_DIM)
    fc = ROPE_THETA ** (-jnp.arange(0, C_ROPE, 2, dtype=F32) / C_ROPE)

    def rows(f, x1, hr):
        first = (lane >= x1) & (lane < x1 + hr)
        second = (lane >= HALF + x1) & (lane < HALF + x1 + hr)
        freq = jnp.where(first | second, f[(lane - x1) % HALF % hr], 0.0)
        return freq, jnp.where(first, -1.0, jnp.where(second, 1.0, 0.0))

    freq_a, sign_a = rows(fa, A_X1, A_HR)
    freq_c, sign_c = rows(fc, C_X1, C_HR)
    return jnp.stack([freq_a, sign_a, freq_c, sign_c] + [jnp.zeros(LANES)] * 4).astype(F32)


def _rope_tables(pos_col, *, name):
    s = pos_col.shape[0]
    tm = _tile(s, ROW_TILE, 8)

    def body(pos_ref, rows_ref, ca_ref, sa_ref, cc_ref, sc_ref):
        p = pos_ref[...].astype(F32)
        ang_a = p * rows_ref[0:1, :]
        ang_c = p * rows_ref[2:3, :]
        ca_ref[...] = jnp.cos(ang_a)
        sa_ref[...] = jnp.sin(ang_a) * rows_ref[1:2, :]
        cc_ref[...] = jnp.cos(ang_c)
        sc_ref[...] = jnp.sin(ang_c) * rows_ref[3:4, :]

    tab = pl.BlockSpec((tm, LANES), lambda i: (i, 0))
    return pl.pallas_call(
        body, grid=(s // tm,),
        in_specs=[pl.BlockSpec((tm, 1), lambda i: (i, 0)), pl.BlockSpec((8, LANES), lambda i: (0, 0))],
        out_specs=[tab] * 4, out_shape=[jax.ShapeDtypeStruct((s, LANES), F32)] * 4,
        compiler_params=_cp("parallel"), name=name)(pos_col, _rope_rows())


def _rope(xv, cos, sin):
    return xv * cos + pltpu.roll(xv, HALF, 1) * sin


def _slot(ref, t):
    return ref[:, t * LANES:(t + 1) * LANES]


def _head_norm_rope(xv, g, cos, sin, n):
    y = xv * lax.rsqrt(_slot_sum(xv * xv) * (1.0 / n) + EPS) * g
    return _rope(y, cos, sin)


def _head_norm_rope_bwd(xv, g, dy, cos, sin, n):
    dyn = _rope(dy, cos, -sin)
    dx, xhat = _rms_bwd_math(xv, g, dyn, n, _slot_sum)
    return dx, jnp.sum(dyn * xhat, axis=0, keepdims=True)


def _prep_a_fwd(qkv, cos, sin, gq, gk, *, name):
    s = qkv.shape[0]
    tm = _tile(s, PREP_TILE, 8)
    nq, nkv = A_HEADS, A_KV_HEADS

    def body(x_ref, c_ref, s_ref, gq_ref, gk_ref, q_ref, k_ref, v_ref):
        cv, sv = c_ref[...], s_ref[...]
        for t in range(nq):
            q_ref[:, t * LANES:(t + 1) * LANES] = _head_norm_rope(
                _slot(x_ref, t), gq_ref[...], cv, sv, A_HEAD_DIM).astype(BF)
        for t in range(nkv):
            k_ref[:, t * LANES:(t + 1) * LANES] = _head_norm_rope(
                _slot(x_ref, nq + t), gk_ref[...], cv, sv, A_HEAD_DIM).astype(BF)
        v_ref[...] = x_ref[:, (nq + nkv) * LANES:].astype(BF)

    def rows(w):
        return pl.BlockSpec((tm, w), lambda i: (i, 0))

    vec = pl.BlockSpec((1, LANES), lambda i: (0, 0))
    return pl.pallas_call(
        body, grid=(s // tm,),
        in_specs=[rows(qkv.shape[1]), rows(LANES), rows(LANES), vec, vec],
        out_specs=[rows(nq * LANES), rows(nkv * LANES), rows(nkv * LANES)],
        out_shape=[jax.ShapeDtypeStruct((s, nq * LANES), BF), jax.ShapeDtypeStruct((s, nkv * LANES), BF),
                   jax.ShapeDtypeStruct((s, nkv * LANES), BF)],
        compiler_params=_cp("parallel"), name=name)(qkv, cos, sin, gq, gk)


def _prep_a_bwd(qkv, dq, dk, dv, cos, sin, gq, gk, *, name):
    s = qkv.shape[0]
    tm = _tile(s, PREP_TILE, 8)
    nq, nkv = A_HEADS, A_KV_HEADS

    def body(x_ref, dq_ref, dk_ref, dv_ref, c_ref, s_ref, gq_ref, gk_ref, dx_ref, dg_ref):
        @pl.when(pl.program_id(0) == 0)
        def _():
            dg_ref[...] = jnp.zeros_like(dg_ref)

        cv, sv = c_ref[...], s_ref[...]
        dgq = jnp.zeros((1, LANES), F32)
        dgk = jnp.zeros((1, LANES), F32)
        for t in range(nq):
            dx, dg = _head_norm_rope_bwd(_slot(x_ref, t), gq_ref[...], _slot(dq_ref, t), cv, sv,
                                         A_HEAD_DIM)
            dx_ref[:, t * LANES:(t + 1) * LANES] = dx.astype(BF)
            dgq = dgq + dg
        for t in range(nkv):
            dx, dg = _head_norm_rope_bwd(_slot(x_ref, nq + t), gk_ref[...], _slot(dk_ref, t), cv, sv,
                                         A_HEAD_DIM)
            dx_ref[:, (nq + t) * LANES:(nq + t + 1) * LANES] = dx.astype(BF)
            dgk = dgk + dg
        dx_ref[:, (nq + nkv) * LANES:] = dv_ref[...].astype(BF)
        dg_ref[0:1, :] += dgq
        dg_ref[1:2, :] += dgk

    def rows(w):
        return pl.BlockSpec((tm, w), lambda i: (i, 0))

    vec = pl.BlockSpec((1, LANES), lambda i: (0, 0))
    dx, dg = pl.pallas_call(
        body, grid=(s // tm,),
        in_specs=[rows(qkv.shape[1]), rows(nq * LANES), rows(nkv * LANES), rows(nkv * LANES), rows(LANES),
                  rows(LANES), vec, vec],
        out_specs=[rows(qkv.shape[1]), pl.BlockSpec((8, LANES), lambda i: (0, 0))],
        out_shape=[jax.ShapeDtypeStruct(qkv.shape, BF), jax.ShapeDtypeStruct((8, LANES), F32)],
        compiler_params=_cp("arbitrary"), name=name)(qkv, dq, dk, dv, cos, sin, gq, gk)
    return dx, dg[0:1], dg[1:2]


def _prep_c1_fwd(dn, gq, gkv, *, name):
    s = dn.shape[0]
    tm = _tile(s, ROW_TILE, 8)

    def body(x_ref, gq_ref, gkv_ref, cq_ref, ckv_ref):
        for lo, n, g_ref, o_ref in ((0, C_Q_RANK, gq_ref, cq_ref), (C_Q_RANK, C_KV_RANK, gkv_ref, ckv_ref)):
            xv = x_ref[:, lo:lo + n]
            y = xv * lax.rsqrt(jnp.mean(xv * xv, axis=-1, keepdims=True) + EPS)
            o_ref[...] = (y * g_ref[...]).astype(BF)

    def rows(w):
        return pl.BlockSpec((tm, w), lambda i: (i, 0))

    return pl.pallas_call(
        body, grid=(s // tm,),
        in_specs=[rows(dn.shape[1]), pl.BlockSpec((1, C_Q_RANK), lambda i: (0, 0)),
                  pl.BlockSpec((1, C_KV_RANK), lambda i: (0, 0))],
        out_specs=[rows(C_Q_RANK), rows(C_KV_RANK)],
        out_shape=[jax.ShapeDtypeStruct((s, C_Q_RANK), BF), jax.ShapeDtypeStruct((s, C_KV_RANK), BF)],
        compiler_params=_cp("parallel"), name=name)(dn, gq, gkv)


def _prep_c1_bwd(dn, dcq, dckv, dkr, gq, gkv, *, name):
    s = dn.shape[0]
    tm = _tile(s, ROW_TILE, 8)

    def body(x_ref, dcq_ref, dckv_ref, dkr_ref, gq_ref, gkv_ref, dx_ref, dgq_ref, dgkv_ref):
        @pl.when(pl.program_id(0) == 0)
        def _():
            dgq_ref[...] = jnp.zeros_like(dgq_ref)
            dgkv_ref[...] = jnp.zeros_like(dgkv_ref)

        for lo, n, g_ref, d_ref, dg_ref in ((0, C_Q_RANK, gq_ref, dcq_ref, dgq_ref),
                                            (C_Q_RANK, C_KV_RANK, gkv_ref, dckv_ref, dgkv_ref)):
            dv = d_ref[...]
            dx, xhat = _rms_bwd_math(x_ref[:, lo:lo + n], g_ref[...], dv, n)
            dx_ref[:, lo:lo + n] = dx.astype(BF)
            dg_ref[0:1, :] += jnp.sum(dv * xhat, axis=0, keepdims=True)
        dx_ref[:, C_Q_RANK + C_KV_RANK:] = dkr_ref[...].astype(BF)

    def rows(w):
        return pl.BlockSpec((tm, w), lambda i: (i, 0))

    def vec(w, r=1):
        return pl.BlockSpec((r, w), lambda i: (0, 0))

    dx, dgq, dgkv = pl.pallas_call(
        body, grid=(s // tm,),
        in_specs=[rows(dn.shape[1]), rows(C_Q_RANK), rows(C_KV_RANK), rows(LANES), vec(C_Q_RANK), vec(C_KV_RANK)],
        out_specs=[rows(dn.shape[1]), vec(C_Q_RANK, 8), vec(C_KV_RANK, 8)],
        out_shape=[jax.ShapeDtypeStruct(dn.shape, BF), jax.ShapeDtypeStruct((8, C_Q_RANK), F32),
                   jax.ShapeDtypeStruct((8, C_KV_RANK), F32)],
        compiler_params=_cp("arbitrary"), name=name)(dn, dcq, dckv, dkr, gq, gkv)
    return dx, dgq[0:1], dgkv[0:1]


def _prep_c2_fwd(qraw, knope, dn, cos, sin, gq, gk, *, name):
    s = qraw.shape[0]
    tm = _tile(s, PREP_TILE, 8)
    kr_blk = dn.shape[1] // LANES - 1

    def body(q_ref, kn_ref, kr_ref, c_ref, s_ref, gq_ref, gk_ref, qo_ref, ko_ref):
        cv, sv, kr = c_ref[...], s_ref[...], kr_ref[...]
        for t in range(C_HEADS):
            qo_ref[:, t * LANES:(t + 1) * LANES] = _head_norm_rope(
                _slot(q_ref, t), gq_ref[...], cv, sv, C_QK).astype(BF)
            ko_ref[:, t * LANES:(t + 1) * LANES] = _head_norm_rope(
                _slot(kn_ref, t) + kr, gk_ref[...], cv, sv, C_QK).astype(BF)

    def rows(w, blk=0):
        return pl.BlockSpec((tm, w), lambda i: (i, blk))

    vec = pl.BlockSpec((1, LANES), lambda i: (0, 0))
    w = C_HEADS * LANES
    return pl.pallas_call(
        body, grid=(s // tm,),
        in_specs=[rows(w), rows(w), rows(LANES, kr_blk), rows(LANES), rows(LANES), vec, vec],
        out_specs=[rows(w), rows(w)],
        out_shape=[jax.ShapeDtypeStruct((s, w), BF)] * 2,
        compiler_params=_cp("parallel"), name=name)(qraw, knope, dn, cos, sin, gq, gk)


def _prep_c2_bwd(qraw, knope, dn, dq, dk, cos, sin, gq, gk, *, name):
    s = qraw.shape[0]
    tm = _tile(s, PREP_TILE, 8)
    kr_blk = dn.shape[1] // LANES - 1

    def body(q_ref, kn_ref, kr_ref, dq_ref, dk_ref, c_ref, s_ref, gq_ref, gk_ref,
             dqo_ref, dkno_ref, dkr_ref, dg_ref):
        @pl.when(pl.program_id(0) == 0)
        def _():
            dg_ref[...] = jnp.zeros_like(dg_ref)

        cv, sv, kr = c_ref[...], s_ref[...], kr_ref[...]
        dgq = jnp.zeros((1, LANES), F32)
        dgk = jnp.zeros((1, LANES), F32)
        dkr = jnp.zeros((tm, LANES), F32)
        for t in range(C_HEADS):
            dx, dg = _head_norm_rope_bwd(_slot(q_ref, t), gq_ref[...], _slot(dq_ref, t), cv, sv,
                                         C_QK)
            dqo_ref[:, t * LANES:(t + 1) * LANES] = dx.astype(BF)
            dgq = dgq + dg
            dx, dg = _head_norm_rope_bwd(_slot(kn_ref, t) + kr, gk_ref[...], _slot(dk_ref, t), cv, sv,
                                         C_QK)
            dkno_ref[:, t * LANES:(t + 1) * LANES] = dx.astype(BF)
            dkr = dkr + dx
            dgk = dgk + dg
        dkr_ref[...] = dkr
        dg_ref[0:1, :] += dgq
        dg_ref[1:2, :] += dgk

    def rows(w, blk=0):
        return pl.BlockSpec((tm, w), lambda i: (i, blk))

    vec = pl.BlockSpec((1, LANES), lambda i: (0, 0))
    w = C_HEADS * LANES
    dqo, dkno, dkr, dg = pl.pallas_call(
        body, grid=(s // tm,),
        in_specs=[rows(w), rows(w), rows(LANES, kr_blk), rows(w), rows(w), rows(LANES), rows(LANES), vec, vec],
        out_specs=[rows(w), rows(w), rows(LANES), pl.BlockSpec((8, LANES), lambda i: (0, 0))],
        out_shape=[jax.ShapeDtypeStruct((s, w), BF), jax.ShapeDtypeStruct((s, w), BF),
                   jax.ShapeDtypeStruct((s, LANES), F32), jax.ShapeDtypeStruct((8, LANES), F32)],
        compiler_params=_cp("arbitrary"), name=name)(qraw, knope, dn, dq, dk, cos, sin, gq, gk)
    return dqo, dkno, dkr, dg[0:1], dg[1:2]


def _lane_pick(mat, g):
    lane = lax.broadcasted_iota(jnp.int32, mat.shape, 1)
    return jnp.sum(jnp.where(lane == g, mat, 0.0), axis=-1, keepdims=True)


def _swa_fwd(q, k, v, sinks, *, name):
    s = q.shape[0]
    nb = min(SWA_BLOCKS, s // BLOCK)
    qt = nb * BLOCK
    scale = A_HEAD_DIM ** -0.5
    gw = A_GROUP * LANES

    def body(sink_ref, q_ref, kc_ref, kp_ref, vc_ref, vp_ref, o_ref, lse_ref):
        kvh, n = pl.program_id(0), pl.program_id(1)
        rows = A_GROUP * BLOCK
        kall = jnp.concatenate([kp_ref[...], kc_ref[...]], axis=0)
        vall = jnp.concatenate([vp_ref[...], vc_ref[...]], axis=0)
        r = lax.broadcasted_iota(jnp.int32, (rows, 2 * BLOCK), 0) & (BLOCK - 1)
        c = lax.broadcasted_iota(jnp.int32, (rows, 2 * BLOCK), 1)
        cur_ok = (c >= BLOCK) & (c - BLOCK <= r)
        prev_ok = (c < BLOCK) & (c > r)
        head = lax.broadcasted_iota(jnp.int32, (rows, 1), 0) >> 7
        sink = jnp.zeros((rows, 1), F32)
        for g in range(A_GROUP):
            sink = jnp.where(head == g, sink_ref[kvh * A_GROUP + g], sink)
        lane = lax.broadcasted_iota(jnp.int32, (BLOCK, LANES), 1)
        for b in range(nb):
            blk = slice(b * BLOCK, (b + 1) * BLOCK)
            qs = jnp.concatenate([q_ref[blk, g * LANES:(g + 1) * LANES] for g in range(A_GROUP)], axis=0)
            k2 = kall[b * BLOCK:(b + 2) * BLOCK]
            v2 = vall[b * BLOCK:(b + 2) * BLOCK]
            ok = (cur_ok | prev_ok) if b > 0 else (cur_ok | (prev_ok & (n > 0)))
            sc = jnp.where(ok, _dot_nt(qs, k2) * scale, NEG)
            m = jnp.maximum(jnp.max(sc, axis=-1, keepdims=True), sink)
            p = jnp.exp(sc - m)
            l = jnp.sum(p, axis=-1, keepdims=True) + jnp.exp(sink - m)
            o = _dot((p * (1.0 / l)).astype(BF), v2)
            lse = m + jnp.log(l)
            lse_mat = jnp.zeros((BLOCK, LANES), F32)
            for g in range(A_GROUP):
                o_ref[blk, g * LANES:(g + 1) * LANES] = o[g * BLOCK:(g + 1) * BLOCK].astype(BF)
                lse_mat = jnp.where(lane == g, lse[g * BLOCK:(g + 1) * BLOCK], lse_mat)
            lse_ref[0, blk, :] = lse_mat

    cur = pl.BlockSpec((qt, LANES), lambda h, n: (n, h))
    prev = pl.BlockSpec((BLOCK, LANES), lambda h, n: (jnp.maximum(n * nb - 1, 0), h))
    return pl.pallas_call(
        body, grid=(A_KV_HEADS, s // qt),
        in_specs=[pl.BlockSpec(memory_space=pltpu.SMEM), pl.BlockSpec((qt, gw), lambda h, n: (n, h)),
                  cur, prev, cur, prev],
        out_specs=[pl.BlockSpec((qt, gw), lambda h, n: (n, h)),
                   pl.BlockSpec((1, qt, LANES), lambda h, n: (h, n, 0))],
        out_shape=[jax.ShapeDtypeStruct(q.shape, BF), jax.ShapeDtypeStruct((A_KV_HEADS, s, LANES), F32)],
        compiler_params=_cp("parallel", "parallel"), name=name)(sinks, q, k, k, v, v)


def _swa_bwd(q, k, v, o, do, lse, sinks, *, name):
    s = q.shape[0]
    nblk = s // BLOCK
    nb = min(SWA_BLOCKS, nblk)
    qt = nb * BLOCK
    nsteps = s // qt
    scale = A_HEAD_DIM ** -0.5
    gw = A_GROUP * LANES

    def body(sink_ref, qc_ref, qn_ref, k_ref, v_ref, oc_ref, on_ref, doc_ref, don_ref, lc_ref, ln_ref,
             dq_ref, dk_ref, dv_ref, dsink_ref, carry):
        kvh, n = pl.program_id(0), pl.program_id(1)

        @pl.when(n == 0)
        def _():
            carry[...] = jnp.zeros_like(carry)
            dsink_ref[...] = jnp.zeros_like(dsink_ref)

        half = A_GROUP * BLOCK
        rows = 2 * half
        qall = jnp.concatenate([qc_ref[...], qn_ref[...]], axis=0)
        oall = jnp.concatenate([oc_ref[...], on_ref[...]], axis=0)
        doall = jnp.concatenate([doc_ref[...], don_ref[...]], axis=0)
        lall = jnp.concatenate([lc_ref[0], ln_ref[0]], axis=0)
        row = lax.broadcasted_iota(jnp.int32, (rows, BLOCK), 0)
        r = row & (BLOCK - 1)
        c = lax.broadcasted_iota(jnp.int32, (rows, BLOCK), 1)
        diag_ok = (row < half) & (c <= r)
        next_ok = (row >= half) & (c > r)
        lane8 = lax.broadcasted_iota(jnp.int32, (8, LANES), 1)
        dsink = jnp.zeros((8, LANES), F32)
        off_prev = None
        for b in range(nb):
            blk = slice(b * BLOCK, (b + 1) * BLOCK)
            kv, vv = k_ref[blk, :], v_ref[blk, :]

            def stack(allv):
                return jnp.concatenate(
                    [allv[b * BLOCK:(b + 1) * BLOCK, g * LANES:(g + 1) * LANES] for g in range(A_GROUP)]
                    + [allv[(b + 1) * BLOCK:(b + 2) * BLOCK, g * LANES:(g + 1) * LANES] for g in range(A_GROUP)],
                    axis=0)

            qs, osk, dos = stack(qall), stack(oall), stack(doall)
            lse = jnp.concatenate([_lane_pick(lall[b * BLOCK:(b + 1) * BLOCK], g) for g in range(A_GROUP)]
                                  + [_lane_pick(lall[(b + 1) * BLOCK:(b + 2) * BLOCK], g) for g in range(A_GROUP)],
                                  axis=0)
            delta = jnp.sum(dos.astype(F32) * osk.astype(F32), axis=-1, keepdims=True)
            ok = (diag_ok | next_ok) if b < nb - 1 else (diag_ok | (next_ok & (n < nsteps - 1)))
            sc = jnp.where(ok, _dot_nt(qs, kv) * scale, NEG)
            p = jnp.exp(sc - lse)
            dv_ref[blk, :] = _dot_tn(p.astype(BF), dos)
            dp = _dot_nt(dos, vv)
            ds = (p * (dp - delta) * scale).astype(BF)
            dk_ref[blk, :] = _dot_tn(ds, qs)
            dqs = _dot(ds, kv)
            for g in range(A_GROUP):
                cur = slice(g * BLOCK, (g + 1) * BLOCK)
                before = carry[:, g * LANES:(g + 1) * LANES] if b == 0 else off_prev[cur]
                dq_ref[blk, g * LANES:(g + 1) * LANES] = before + dqs[cur]
                p_sink = jnp.exp(sink_ref[kvh * A_GROUP + g] - lse[cur])
                dsink = dsink + jnp.where(lane8 == g, -jnp.sum(p_sink * delta[cur]), 0.0)
            off_prev = dqs[half:]
        for g in range(A_GROUP):
            carry[:, g * LANES:(g + 1) * LANES] = off_prev[g * BLOCK:(g + 1) * BLOCK]
        dsink_ref[...] += dsink

    def nxt(n):
        return jnp.minimum((n + 1) * nb, nblk - 1)

    grp_c = pl.BlockSpec((qt, gw), lambda h, n: (n, h))
    grp_n = pl.BlockSpec((BLOCK, gw), lambda h, n: (nxt(n), h))
    kvb = pl.BlockSpec((qt, LANES), lambda h, n: (n, h))
    lse_c = pl.BlockSpec((1, qt, LANES), lambda h, n: (h, n, 0))
    lse_n = pl.BlockSpec((1, BLOCK, LANES), lambda h, n: (h, nxt(n), 0))
    dq, dk, dv, dsink = pl.pallas_call(
        body, grid=(A_KV_HEADS, nsteps),
        in_specs=[pl.BlockSpec(memory_space=pltpu.SMEM), grp_c, grp_n, kvb, kvb, grp_c, grp_n, grp_c, grp_n,
                  lse_c, lse_n],
        out_specs=[grp_c, kvb, kvb, pl.BlockSpec((8, LANES), lambda h, n: (h, 0))],
        out_shape=[jax.ShapeDtypeStruct(q.shape, F32), jax.ShapeDtypeStruct(k.shape, F32),
                   jax.ShapeDtypeStruct(v.shape, F32), jax.ShapeDtypeStruct((A_KV_HEADS * 8, LANES), F32)],
        scratch_shapes=[pltpu.VMEM((BLOCK, gw), F32)],
        compiler_params=_cp("parallel", "arbitrary"), name=name)(sinks, q, q, k, v, o, o, do, do, lse, lse)
    dsinks = dsink.reshape(A_KV_HEADS, 8, LANES)[:, 0, :A_GROUP].reshape(A_HEADS)
    return dq, dk, dv, dsinks


LOG2E = 1.4426950408889634


def _mla_fwd(q, k, v, *, name):
    s = q.shape[0]
    t = _tile(s, ATTN_TILE, LANES)
    nt = s // t
    scale = C_QK ** -0.5
    c2 = scale * LOG2E

    def body(q_ref, k_ref, v_ref, o_ref, lse_ref, m_sc, acc_sc):
        qi = pl.program_id(1)
        qv = q_ref[...]
        m_sc[...] = jnp.full_like(m_sc, NEG)
        acc_sc[...] = jnp.zeros_like(acc_sc)

        def step(start, width, diag):
            rows = pl.ds(pl.multiple_of(start, t), width)
            sc = _dot_nt(qv, k_ref[rows, :])
            if diag:
                r = lax.broadcasted_iota(jnp.int32, (t, t), 0)
                c = lax.broadcasted_iota(jnp.int32, (t, t), 1)
                own = jnp.where(c <= r, sc[:, width - t:], NEG)
                sc = own if width == t else jnp.concatenate([sc[:, :width - t], own], axis=1)
            m_prev = m_sc[...]
            m_new = jnp.maximum(m_prev, jnp.max(sc, axis=-1, keepdims=True))
            alpha = jnp.exp2((m_prev - m_new) * c2)
            p = jnp.exp2((sc - jnp.tile(m_new, (1, width // LANES))) * c2).astype(BF)
            lane = lax.broadcasted_iota(jnp.int32, (width, LANES), 1)
            vv = jnp.where(lane == C_V, jnp.ones((), BF), v_ref[rows, :])
            acc_sc[...] = alpha * acc_sc[...] + _dot(p, vv)
            m_sc[...] = m_new

        def wide_body(wi, carry):
            step(wi * (FWD_WIDE * t), FWD_WIDE * t, False)
            return carry

        def single_body(j, carry):
            step(j * t, t, False)
            return carry

        with_diag = qi >= FWD_WIDE - 1

        @pl.when(with_diag)
        def _():
            step((qi - (FWD_WIDE - 1)) * t, FWD_WIDE * t, True)

        @pl.when(jnp.logical_not(with_diag))
        def _():
            step(qi * t, t, True)

        rest = jnp.where(with_diag, qi - (FWD_WIDE - 1), qi)
        n_wide = rest // FWD_WIDE
        lax.fori_loop(0, n_wide, wide_body, 0)
        lax.fori_loop(n_wide * FWD_WIDE, rest, single_body, 0)
        acc = acc_sc[...]
        l = _lane_pick(acc, C_V)
        lane = lax.broadcasted_iota(jnp.int32, (t, LANES), 1)
        o_ref[...] = jnp.where(lane == C_V, 0.0, acc * (1.0 / l)).astype(BF)
        lse2 = m_sc[...] * c2 + jnp.log(l) * LOG2E
        lse_ref[0, 0] = jnp.transpose(lse2)[0:8, :]

    qspec = pl.BlockSpec((t, LANES), lambda h, i: (i, h))
    kspec = pl.BlockSpec((s, LANES), lambda h, i: (0, h))
    return pl.pallas_call(
        body, grid=(C_HEADS, nt), in_specs=[qspec, kspec, kspec],
        out_specs=[qspec, pl.BlockSpec((1, 1, 8, t), lambda h, i: (h, i, 0, 0))],
        out_shape=[jax.ShapeDtypeStruct(q.shape, BF), jax.ShapeDtypeStruct((C_HEADS, nt, 8, t), F32)],
        scratch_shapes=[pltpu.VMEM((t, LANES), F32)] * 2,
        compiler_params=_cp("parallel", "parallel"), name=name)(q, k, v)


def _mla_delta(o, do, *, name):
    s = o.shape[0]
    t = _tile(s, ATTN_TILE, LANES)
    nt = s // t

    def body(o_ref, do_ref, d_ref):
        for h in range(C_HEADS):
            prod = jnp.transpose(_slot(o_ref, h).astype(F32) * _slot(do_ref, h).astype(F32))
            d_ref[h, 0] = jnp.broadcast_to(jnp.sum(prod, axis=0, keepdims=True), (8, t))

    blk = pl.BlockSpec((t, C_HEADS * LANES), lambda i: (i, 0))
    return pl.pallas_call(
        body, grid=(nt,), in_specs=[blk, blk],
        out_specs=pl.BlockSpec((C_HEADS, 1, 8, t), lambda i: (0, i, 0, 0)),
        out_shape=jax.ShapeDtypeStruct((C_HEADS, nt, 8, t), F32),
        compiler_params=_cp("parallel"), name=name)(o, do)


def _mla_bwd(q, k, v, do, lse2, delta, *, name):
    s = q.shape[0]
    t = _tile(s, ATTN_TILE, LANES)
    nt = s // t
    scale = C_QK ** -0.5
    c2 = scale * LOG2E

    def body(q_ref, do_ref, k_ref, v_ref, lse_ref, dl_ref, dq_ref, dk_ref, dv_ref):
        kj = pl.program_id(1)

        @pl.when(kj == 0)
        def _():
            dq_ref[...] = jnp.zeros_like(dq_ref)

        dk_ref[...] = jnp.zeros_like(dk_ref)
        dv_ref[...] = jnp.zeros_like(dv_ref)
        kv, vv = k_ref[...], v_ref[...]

        def step(i, n, diag):
            rows = pl.ds(pl.multiple_of(i * t, t), n * t)
            qv, dov = q_ref[rows, :], do_ref[rows, :]
            st = _dot_nt(kv, qv)
            if diag:
                r = lax.broadcasted_iota(jnp.int32, (t, t), 0)
                c = lax.broadcasted_iota(jnp.int32, (t, t), 1)
                own = jnp.where(r <= c, st[:, :t], NEG)
                st = own if n == 1 else jnp.concatenate([own, st[:, t:]], axis=1)
            lse = jnp.concatenate([lse_ref[0, i + u, 0:1, :] for u in range(n)], axis=1)
            dl = jnp.concatenate([dl_ref[0, i + u, 0:1, :] for u in range(n)], axis=1)
            pt = jnp.exp2(st * c2 - lse)
            dv_ref[...] += _dot(pt.astype(BF), dov)
            dpt = _dot_nt(vv, dov)
            dst = (pt * (dpt - dl) * scale).astype(BF)
            dk_ref[...] += _dot(dst, qv)
            dq_ref[rows, :] += _dot_tn(dst, kv)

        odd = (nt - kj) % 2
        first = kj + 2 - odd

        @pl.when(odd == 0)
        def _():
            step(kj, 2, True)

        @pl.when(odd == 1)
        def _():
            step(kj, 1, True)

        def pair_body(pi, carry):
            step(first + 2 * pi, 2, False)
            return carry

        lax.fori_loop(0, (nt - first) // 2, pair_body, 0)

    res = pl.BlockSpec((s, LANES), lambda h, j: (0, h))
    kspec = pl.BlockSpec((t, LANES), lambda h, j: (j, h))
    stat = pl.BlockSpec((1, nt, 8, t), lambda h, j: (h, 0, 0, 0))
    return pl.pallas_call(
        body, grid=(C_HEADS, nt), in_specs=[res, res, kspec, kspec, stat, stat],
        out_specs=[res, kspec, kspec],
        out_shape=[jax.ShapeDtypeStruct(q.shape, F32)] * 3,
        compiler_params=_cp("parallel", "arbitrary"), name=name)(q, do, k, v, lse2, delta)


def _conv_shifted(z, zprev, tm):
    row = lax.broadcasted_iota(jnp.int32, z.shape, 0)
    z1 = jnp.where(row == 0, zprev[7:8], pltpu.roll(z, 1, 0))
    z2 = jnp.where(row == 0, zprev[6:7], jnp.where(row == 1, zprev[7:8], pltpu.roll(z, 2, 0)))
    return z1, z2


def _conv_fwd(proj, w, *, name):
    s = proj.shape[0]
    d = D_MODEL
    tm = _tile(s, PREP_TILE, 8)

    def body(x_ref, xp_ref, w_ref, y_ref):
        i = pl.program_id(0)
        z = x_ref[:, d:2 * d] * x_ref[:, 2 * d:]
        zprev = jnp.where(i > 0, xp_ref[:, d:2 * d] * xp_ref[:, 2 * d:], 0.0)
        z1, z2 = _conv_shifted(z, zprev, tm)
        y = w_ref[0:1, :] * z2 + w_ref[1:2, :] * z1 + w_ref[2:3, :] * z
        y_ref[...] = (x_ref[:, :d] * y).astype(BF)

    per8 = tm // 8
    return pl.pallas_call(
        body, grid=(s // tm,),
        in_specs=[pl.BlockSpec((tm, 3 * d), lambda i: (i, 0)),
                  pl.BlockSpec((8, 3 * d), lambda i: (jnp.maximum(i * per8 - 1, 0), 0)),
                  pl.BlockSpec((8, d), lambda i: (0, 0))],
        out_specs=pl.BlockSpec((tm, d), lambda i: (i, 0)),
        out_shape=jax.ShapeDtypeStruct((s, d), BF),
        compiler_params=_cp("parallel"), name=name)(proj, proj, w)


def _conv_bwd(proj, dyb, w, *, name):
    s = proj.shape[0]
    d = D_MODEL
    tm = _tile(s, PREP_TILE, 8)
    nt = s // tm

    def body(x_ref, xp_ref, xn_ref, dy_ref, dyn_ref, w_ref, dx_ref, dw_ref):
        i = pl.program_id(0)

        @pl.when(i == 0)
        def _():
            dw_ref[...] = jnp.zeros_like(dw_ref)

        b, c, u = x_ref[:, :d], x_ref[:, d:2 * d], x_ref[:, 2 * d:]
        z = c * u
        zprev = jnp.where(i > 0, xp_ref[:, d:2 * d] * xp_ref[:, 2 * d:], 0.0)
        z1, z2 = _conv_shifted(z, zprev, tm)
        w0, w1, w2 = w_ref[0:1, :], w_ref[1:2, :], w_ref[2:3, :]
        y = w0 * z2 + w1 * z1 + w2 * z
        dyb_v = dy_ref[...]
        dyc = dyb_v * b
        dyn = jnp.where(i < nt - 1, dyn_ref[...] * xn_ref[:, :d], 0.0)
        row = lax.broadcasted_iota(jnp.int32, dyc.shape, 0)
        d1 = jnp.where(row == tm - 1, dyn[0:1], pltpu.roll(dyc, tm - 1, 0))
        d2 = jnp.where(row == tm - 1, dyn[1:2], jnp.where(row == tm - 2, dyn[0:1], pltpu.roll(dyc, tm - 2, 0)))
        dz = w2 * dyc + w1 * d1 + w0 * d2
        dx_ref[:, :d] = (dyb_v * y).astype(BF)
        dx_ref[:, d:2 * d] = (dz * u).astype(BF)
        dx_ref[:, 2 * d:] = (dz * c).astype(BF)
        dw_ref[0:1, :] += jnp.sum(dyc * z2, axis=0, keepdims=True)
        dw_ref[1:2, :] += jnp.sum(dyc * z1, axis=0, keepdims=True)
        dw_ref[2:3, :] += jnp.sum(dyc * z, axis=0, keepdims=True)

    per8 = tm // 8
    last8 = s // 8 - 1
    dx, dw = pl.pallas_call(
        body, grid=(nt,),
        in_specs=[pl.BlockSpec((tm, 3 * d), lambda i: (i, 0)),
                  pl.BlockSpec((8, 3 * d), lambda i: (jnp.maximum(i * per8 - 1, 0), 0)),
                  pl.BlockSpec((8, 3 * d), lambda i: (jnp.minimum((i + 1) * per8, last8), 0)),
                  pl.BlockSpec((tm, d), lambda i: (i, 0)),
                  pl.BlockSpec((8, d), lambda i: (jnp.minimum((i + 1) * per8, last8), 0)),
                  pl.BlockSpec((8, d), lambda i: (0, 0))],
        out_specs=[pl.BlockSpec((tm, 3 * d), lambda i: (i, 0)), pl.BlockSpec((8, d), lambda i: (0, 0))],
        out_shape=[jax.ShapeDtypeStruct((s, 3 * d), BF), jax.ShapeDtypeStruct((8, d), F32)],
        compiler_params=_cp("arbitrary"), name=name)(proj, proj, proj, dyb, dyb, w)
    return dx, dw[0:3]


def _ffn_up(h, wgu, *, name):
    s, d = h.shape
    f = wgu.shape[1] // 2
    tm = _tile(s, ROW_TILE, 8)
    tn = _tile(f, 1408, 128)
    nb = f // tn

    def body(h_ref, wg_ref, wu_ref, dg_ref, du_ref, a_ref):
        hv = h_ref[...]
        g = _dot(hv, wg_ref[...])
        u = _dot(hv, wu_ref[...])
        sig = jax.nn.sigmoid(g)
        silu = g * sig
        dg_ref[...] = (u * (sig + silu * (1.0 - sig))).astype(BF)
        du_ref[...] = silu.astype(BF)
        a_ref[...] = (silu * u).astype(BF)

    out = pl.BlockSpec((tm, tn), lambda j, i: (i, j))
    return pl.pallas_call(
        body, grid=(nb, s // tm),
        in_specs=[pl.BlockSpec((tm, d), lambda j, i: (i, 0)),
                  pl.BlockSpec((d, tn), lambda j, i: (0, j)),
                  pl.BlockSpec((d, tn), lambda j, i: (0, j + nb))],
        out_specs=[out] * 3, out_shape=[jax.ShapeDtypeStruct((s, f), BF)] * 3,
        compiler_params=_cp("parallel", "parallel"), name=name)(h, wgu, wgu)


def _ffn_bwd_mid(dy, wd, gate_factor, up_factor, *, name):
    s, d = dy.shape
    f = wd.shape[0]
    tm = _tile(s, ROW_TILE, 8)
    tn = _tile(f, 1408, 128)

    def body(dy_ref, wd_ref, g_ref, u_ref, dg_ref, du_ref):
        dyv = dy_ref[...].astype(BF)
        for c0 in range(0, tn, 2 * LANES):
            cols = slice(c0, min(c0 + 2 * LANES, tn))
            dact = _dot_nt(dyv, wd_ref[cols, :])
            dg_ref[:, cols] = (dact * g_ref[:, cols].astype(F32)).astype(BF)
            du_ref[:, cols] = (dact * u_ref[:, cols].astype(F32)).astype(BF)

    blk = pl.BlockSpec((tm, tn), lambda j, i: (i, j))
    return pl.pallas_call(
        body, grid=(f // tn, s // tm),
        in_specs=[pl.BlockSpec((tm, d), lambda j, i: (i, 0)), pl.BlockSpec((tn, d), lambda j, i: (j, 0)), blk, blk],
        out_specs=[blk, blk], out_shape=[jax.ShapeDtypeStruct((s, f), BF)] * 2,
        compiler_params=_cp("parallel", "parallel"), name=name)(dy, wd, gate_factor, up_factor)


def _pad_cols(w, heads, dim, layout):
    k = w.shape[0]
    w3 = w.reshape(k, heads, dim)
    pieces, lane = [], 0
    for lane0, dim0, cnt in sorted(layout):
        if lane0 > lane:
            pieces.append(jnp.zeros((k, heads, lane0 - lane), w.dtype))
        pieces.append(w3[:, :, dim0:dim0 + cnt])
        lane = lane0 + cnt
    if lane < LANES:
        pieces.append(jnp.zeros((k, heads, LANES - lane), w.dtype))
    return jnp.concatenate(pieces, axis=2).reshape(k, heads * LANES)


def _unpad_cols(w, heads, dim, layout):
    k = w.shape[0]
    w3 = w.reshape(k, heads, LANES)
    by_dim = sorted(layout, key=lambda seg: seg[1])
    assert sum(cnt for _, _, cnt in by_dim) == dim
    return jnp.concatenate([w3[:, :, lane0:lane0 + cnt] for lane0, _, cnt in by_dim], axis=2).reshape(k, heads * dim)


def _pad_rows(w, heads, dim):
    n = w.shape[1]
    return jnp.pad(w.reshape(heads, dim, n), ((0, 0), (0, LANES - dim), (0, 0))).reshape(heads * LANES, n)


def _unpad_rows(w, heads, dim):
    n = w.shape[1]
    return w.reshape(heads, LANES, n)[:, :dim, :].reshape(heads * dim, n)


def _pad_vec(g, layout):
    return _pad_cols(g.reshape(1, -1), 1, g.shape[0], layout)


def _unpad_vec(g, dim, layout):
    return _unpad_cols(g.reshape(1, LANES), 1, dim, layout)[0]


def _local_step(x, positions, w, target):
    s = x.shape[0]
    cos_a, sin_a, cos_c, sin_c = _rope_tables(positions.reshape(s, 1), name="rope_tables")
    grads = {n: [None] * len(w[n]) for n in WEIGHTS}
    saved = []

    for i in range(DEPTH):
        kind, j = i % N_MIXERS, i // N_MIXERS
        tag = f"l{i}"
        if i == 0:
            h = _rmsnorm_fwd(x, w['mix_norm'][i], name=f"{tag}_mix_norm")
        fnorm = w['ffn_norm'][i]
        if kind == 0:
            nqk = (A_HEADS + A_KV_HEADS) * A_HEAD_DIM
            wqkv = jnp.concatenate(
                [_pad_cols(w['a_w_qkv'][j][:, :nqk], A_HEADS + A_KV_HEADS, A_HEAD_DIM, LAYOUT_A),
                 _pad_cols(w['a_w_qkv'][j][:, nqk:], A_KV_HEADS, A_HEAD_DIM, LAYOUT_V)], axis=1)
            wo = _pad_rows(w['a_w_o'][j], A_HEADS, A_HEAD_DIM)
            gq, gk = _pad_vec(w['a_q_norm'][j], LAYOUT_A), _pad_vec(w['a_k_norm'][j], LAYOUT_A)
            qkv = _mm([(h, wqkv, 0)], out_dtype=F32, name=f"{tag}_qkv")
            qa, ka, va = _prep_a_fwd(qkv, cos_a, sin_a, gq, gk, name=f"{tag}_prep")
            o, lse = _swa_fwd(qa, ka, va, w['a_sinks'][j], name=f"{tag}_attn")
            x1, h2 = _mm([(o, wo, 0)], out_dtype=F32, res=x, norm_g=fnorm, name=f"{tag}_wo")
            mix = dict(wqkv=wqkv, wo=wo, gq=gq, gk=gk, qkv=qkv, qa=qa, ka=ka, va=va, o=o, lse=lse)
        elif kind == 1:
            cw = jnp.pad(w['b_conv_w'][j], ((0, 5), (0, 0)))
            proj = _mm([(h, w['b_w_in'][j], 0)], out_dtype=F32, name=f"{tag}_win")
            yb = _conv_fwd(proj, cw, name=f"{tag}_conv")
            x1, h2 = _mm([(yb, w['b_w_out'][j], 0)], out_dtype=F32, res=x, norm_g=fnorm, name=f"{tag}_wout")
            mix = dict(cw=cw, proj=proj, yb=yb)
        else:
            wdn = w['c_w_down'][j]
            nqk = C_Q_RANK + C_KV_RANK
            wdn = jnp.concatenate([wdn[:, :nqk], _pad_cols(wdn[:, nqk:], 1, C_ROPE, LAYOUT_KR)], axis=1)
            wq = _pad_cols(w['c_w_q_up'][j], C_HEADS, C_QK, LAYOUT_C)
            wkv = w['c_w_kv_up'][j].reshape(C_KV_RANK, C_HEADS, C_NOPE + C_V)
            wkn = _pad_cols(wkv[:, :, :C_NOPE].reshape(C_KV_RANK, -1), C_HEADS, C_NOPE, LAYOUT_KN)
            wv = _pad_cols(wkv[:, :, C_NOPE:].reshape(C_KV_RANK, -1), C_HEADS, C_V, LAYOUT_V)
            wo = _pad_rows(w['c_w_o'][j], C_HEADS, C_V)
            gq, gk = _pad_vec(w['c_q_norm'][j], LAYOUT_C), _pad_vec(w['c_k_norm'][j], LAYOUT_C)
            gqa, gkva = w['c_q_a_norm'][j].reshape(1, -1), w['c_kv_a_norm'][j].reshape(1, -1)
            dn = _mm([(h, wdn, 0)], out_dtype=F32, name=f"{tag}_wdown")
            cqn, ckvn = _prep_c1_fwd(dn, gqa, gkva, name=f"{tag}_prep1")
            qraw = _mm([(cqn, wq, 0)], out_dtype=F32, name=f"{tag}_wq")
            knope = _mm([(ckvn, wkn, 0)], out_dtype=F32, name=f"{tag}_wkn")
            vc = _mm([(ckvn, wv, 0)], out_dtype=BF, name=f"{tag}_wv")
            qc, kc = _prep_c2_fwd(qraw, knope, dn, cos_c, sin_c, gq, gk, name=f"{tag}_prep2")
            o, lse = _mla_fwd(qc, kc, vc, name=f"{tag}_attn")
            x1, h2 = _mm([(o, wo, 0)], out_dtype=F32, res=x, norm_g=fnorm, name=f"{tag}_wo")
            mix = dict(wdn=wdn, wq=wq, wkn=wkn, wv=wv, wo=wo, gq=gq, gk=gk, gqa=gqa, gkva=gkva, dn=dn, cqn=cqn,
                       ckvn=ckvn, qraw=qraw, knope=knope, vc=vc, qc=qc, kc=kc, o=o, lse=lse)
        gate_factor, up_factor, act = _ffn_up(h2, w['f_w_gate_up'][i], name=f"{tag}_ffn_up")
        saved.append(dict(x=x, h=h, x1=x1, h2=h2, gate_factor=gate_factor, up_factor=up_factor, act=act, mix=mix))
        if i + 1 < DEPTH:
            x, h = _mm([(act, w['f_w_down'][i], 0)], out_dtype=F32, res=x1, norm_g=w['mix_norm'][i + 1],
                       name=f"{tag}_ffn_down")
        else:
            x = _mm([(act, w['f_w_down'][i], 0)], out_dtype=F32, res=x1, name=f"{tag}_ffn_down")

    loss_blk, dx = _loss_fwd_bwd(x, target, name="loss")

    for i in reversed(range(DEPTH)):
        kind, j = i % N_MIXERS, i // N_MIXERS
        tag = f"l{i}b"
        sv = saved[i]
        mix = sv['mix']
        wgu, wd = w['f_w_gate_up'][i], w['f_w_down'][i]
        grads['f_w_down'][i] = _mm_tn(sv['act'], dx, name=f"{tag}_dwd")
        dgate, dup = _ffn_bwd_mid(dx, wd, sv['gate_factor'], sv['up_factor'], name=f"{tag}_ffn_mid")
        grads['f_w_gate_up'][i] = (_mm_tn(sv['h2'], dgate, name=f"{tag}_dwg"),
                                   _mm_tn(sv['h2'], dup, name=f"{tag}_dwu"))
        dx, dg = _mm_norm_bwd([(dgate, wgu, 0), (dup, wgu, 1)], sv['x1'], w['ffn_norm'][i], dx, name=f"{tag}_dh2")
        grads['ffn_norm'][i] = dg[0]
        if kind == 0:
            grads['a_w_o'][j] = _unpad_rows(_mm_tn(mix['o'], dx, name=f"{tag}_dwo"), A_HEADS, A_HEAD_DIM)
            do = _mm([(dx, mix['wo'], 0)], out_dtype=BF, trans_b=True, name=f"{tag}_do")
            dqa, dka, dva, dsinks = _swa_bwd(mix['qa'], mix['ka'], mix['va'], mix['o'], do, mix['lse'],
                                             w['a_sinks'][j], name=f"{tag}_attn")
            dqkv, dgq, dgk = _prep_a_bwd(mix['qkv'], dqa, dka, dva, cos_a, sin_a, mix['gq'], mix['gk'],
                                         name=f"{tag}_prep")
            grads['a_sinks'][j] = dsinks
            grads['a_q_norm'][j] = _unpad_vec(dgq, A_HEAD_DIM, LAYOUT_A)
            grads['a_k_norm'][j] = _unpad_vec(dgk, A_HEAD_DIM, LAYOUT_A)
            dwqkv = _mm_tn(sv['h'], dqkv, name=f"{tag}_dwqkv")
            nqk = (A_HEADS + A_KV_HEADS) * LANES
            grads['a_w_qkv'][j] = jnp.concatenate(
                [_unpad_cols(dwqkv[:, :nqk], A_HEADS + A_KV_HEADS, A_HEAD_DIM, LAYOUT_A),
                 _unpad_cols(dwqkv[:, nqk:], A_KV_HEADS, A_HEAD_DIM, LAYOUT_V)], axis=1)
            dh_pairs = [(dqkv, mix['wqkv'], 0)]
        elif kind == 1:
            grads['b_w_out'][j] = _mm_tn(mix['yb'], dx, name=f"{tag}_dwout")
            dyb = _mm([(dx, w['b_w_out'][j], 0)], out_dtype=F32, trans_b=True, name=f"{tag}_dyb")
            dproj, dcw = _conv_bwd(mix['proj'], dyb, mix['cw'], name=f"{tag}_conv")
            grads['b_conv_w'][j] = dcw
            grads['b_w_in'][j] = _mm_tn(sv['h'], dproj, name=f"{tag}_dwin")
            dh_pairs = [(dproj, w['b_w_in'][j], 0)]
        else:
            grads['c_w_o'][j] = _unpad_rows(_mm_tn(mix['o'], dx, name=f"{tag}_dwo"), C_HEADS, C_V)
            do = _mm([(dx, mix['wo'], 0)], out_dtype=BF, trans_b=True, name=f"{tag}_do")
            delta = _mla_delta(mix['o'], do, name=f"{tag}_delta")
            dqc, dkc, dvc = _mla_bwd(mix['qc'], mix['kc'], mix['vc'], do, mix['lse'], delta, name=f"{tag}_attn")
            dqraw, dknope, dkr, dgq, dgk = _prep_c2_bwd(mix['qraw'], mix['knope'], mix['dn'], dqc, dkc, cos_c, sin_c,
                                                        mix['gq'], mix['gk'], name=f"{tag}_prep2")
            grads['c_q_norm'][j] = _unpad_vec(dgq, C_QK, LAYOUT_C)
            grads['c_k_norm'][j] = _unpad_vec(dgk, C_QK, LAYOUT_C)
            grads['c_w_q_up'][j] = _unpad_cols(_mm_tn(mix['cqn'], dqraw, name=f"{tag}_dwq"), C_HEADS, C_QK,
                                               LAYOUT_C)
            dwkn = _unpad_cols(_mm_tn(mix['ckvn'], dknope, name=f"{tag}_dwkn"), C_HEADS, C_NOPE, LAYOUT_KN)
            dwv = _unpad_cols(_mm_tn(mix['ckvn'], dvc, name=f"{tag}_dwv"), C_HEADS, C_V, LAYOUT_V)
            grads['c_w_kv_up'][j] = jnp.concatenate(
                [dwkn.reshape(C_KV_RANK, C_HEADS, C_NOPE), dwv.reshape(C_KV_RANK, C_HEADS, C_V)], axis=2).reshape(
                C_KV_RANK, -1)
            dcq = _mm([(dqraw, mix['wq'], 0)], out_dtype=F32, trans_b=True, name=f"{tag}_dcq")
            dckv = _mm([(dknope, mix['wkn'], 0), (dvc, mix['wv'], 0)], out_dtype=F32, trans_b=True,
                       name=f"{tag}_dckv")
            ddn, dgqa, dgkva = _prep_c1_bwd(mix['dn'], dcq, dckv, dkr, mix['gqa'], mix['gkva'], name=f"{tag}_prep1")
            grads['c_q_a_norm'][j] = dgqa[0]
            grads['c_kv_a_norm'][j] = dgkva[0]
            dwdn = _mm_tn(sv['h'], ddn, name=f"{tag}_dwdown")
            nqk = C_Q_RANK + C_KV_RANK
            grads['c_w_down'][j] = jnp.concatenate(
                [dwdn[:, :nqk], _unpad_cols(dwdn[:, nqk:], 1, C_ROPE, LAYOUT_KR)], axis=1)
            dh_pairs = [(ddn, mix['wdn'], 0)]
        dx, dg = _mm_norm_bwd(dh_pairs, sv['x'], w['mix_norm'][i], dx, name=f"{tag}_dh")
        grads['mix_norm'][i] = dg[0]

    return loss_blk, dx, {n: (g if n in BIG else jnp.stack(g)) for n, g in grads.items()}


def _my_place():
    return lax.axis_index("x"), lax.axis_index("y"), lax.axis_index("c")


def _flips(x, y):
    return [(1 - x, y), (x, 1 - y), (1 - x, 1 - y)]


def _gather_weights(big, small):
    half = big.shape[0] // 2
    chunk = half // GATHER_CHUNKS

    def body(big_ref, small_ref, bout_ref, sout_ref, send_b, recv_b, send_f, recv_f, send_s, recv_s, loc):
        x, y, c = _my_place()
        me = 2 * x + y

        def rows(core, ch):
            return pl.ds(pl.multiple_of(core * half + ch * chunk, 16), chunk)

        def ici(k, ch, chip, to):
            return pltpu.make_async_remote_copy(
                src_ref=big_ref.at[rows(c, ch), :], dst_ref=bout_ref.at[chip, rows(c, ch), :],
                send_sem=send_b.at[k * GATHER_CHUNKS + ch], recv_sem=recv_b.at[k * GATHER_CHUNKS + ch],
                device_id=to, device_id_type=MESH)

        def forward(k, ch, chip, core):
            return pltpu.make_async_remote_copy(
                src_ref=bout_ref.at[chip, rows(core, ch), :], dst_ref=bout_ref.at[chip, rows(core, ch), :],
                send_sem=send_f.at[k * GATHER_CHUNKS + ch], recv_sem=recv_f.at[k * GATHER_CHUNKS + ch],
                device_id=(x, y, 1 - c), device_id_type=MESH)

        def small_copy(k, chip, to):
            return pltpu.make_async_remote_copy(src_ref=small_ref, dst_ref=sout_ref.at[chip],
                                                send_sem=send_s.at[k], recv_sem=recv_s.at[k],
                                                device_id=to, device_id_type=MESH)

        chips = _flips(x, y)
        every = [(ch, k, px, py) for ch in range(GATHER_CHUNKS) for k, (px, py) in enumerate(chips)]
        sends = [ici(k, ch, me, (px, py, c)) for ch, k, px, py in every]
        sends += [small_copy(k, me, (px, py, c)) for k, (px, py) in enumerate(chips)]
        for cp in sends:
            cp.start()
        own = (pltpu.make_async_copy(big_ref, bout_ref.at[me], loc.at[0]),
               pltpu.make_async_copy(small_ref, sout_ref.at[me], loc.at[1]))
        for cp in own:
            cp.start()
        passed = []
        for ch, k, px, py in every:
            ici(k, ch, 2 * px + py, (x, y, c)).wait_recv()
            passed.append(forward(k, ch, 2 * px + py, c))
            passed[-1].start()
        for ch, k, px, py in every:
            forward(k, ch, 2 * px + py, 1 - c).wait_recv()
        for k, (px, py) in enumerate(chips):
            small_copy(k, 2 * px + py, (x, y, c)).wait_recv()
        for cp in sends + passed:
            cp.wait_send()
        for cp in own:
            cp.wait()

    hbm = pl.BlockSpec(memory_space=pltpu.HBM)
    sem3 = pltpu.SemaphoreType.DMA((3,))
    semc = pltpu.SemaphoreType.DMA((3 * GATHER_CHUNKS,))
    return pl.pallas_call(
        body, in_specs=[hbm, hbm], out_specs=[hbm, hbm],
        out_shape=[jax.ShapeDtypeStruct((N_CHIPS,) + big.shape, big.dtype),
                   jax.ShapeDtypeStruct((N_CHIPS,) + small.shape, small.dtype)],
        scratch_shapes=[semc, semc, semc, semc, sem3, sem3, pltpu.SemaphoreType.DMA((2,))],
        name="gather_weights")(big, small)


def _exchange_grads(gbig, gsmall):
    def body(gbig_ref, gsmall_ref, got_ref, ssum_ref, sbuf, send_b, recv_b, send_s, recv_s):
        x, y, c = _my_place()
        me = 4 * x + 2 * y + c

        def big_copy(k, src_chip, to):
            return pltpu.make_async_remote_copy(src_ref=gbig_ref.at[src_chip], dst_ref=got_ref.at[k],
                                                send_sem=send_b.at[k], recv_sem=recv_b.at[k],
                                                device_id=to, device_id_type=MESH)

        def small_copy(k, slot, to):
            return pltpu.make_async_remote_copy(src_ref=gsmall_ref, dst_ref=sbuf.at[slot],
                                                send_sem=send_s.at[k], recv_sem=recv_s.at[k],
                                                device_id=to, device_id_type=MESH)

        peers = [(x ^ (k >> 2), y ^ ((k >> 1) & 1), c ^ (k & 1)) for k in range(1, N_DEV)]
        bigs = [big_copy(k, 2 * px + py, (px, py, c)) for k, (px, py) in enumerate(_flips(x, y))]
        smalls = [small_copy(k, me, p) for k, p in enumerate(peers)]
        for cp in bigs + smalls:
            cp.start()
        sbuf[me] = gsmall_ref[...]
        for k, (px, py, pc) in enumerate(peers):
            small_copy(k, 4 * px + 2 * py + pc, (x, y, c)).wait_recv()
        acc = sbuf[0]
        for d in range(1, N_DEV):
            acc = acc + sbuf[d]
        ssum_ref[...] = acc
        for k in range(3):
            big_copy(k, 0, (x, y, c)).wait_recv()
        for cp in bigs + smalls:
            cp.wait_send()

    hbm = pl.BlockSpec(memory_space=pltpu.HBM)
    vmem = pl.BlockSpec(memory_space=pltpu.VMEM)
    return pl.pallas_call(
        body, in_specs=[hbm, vmem], out_specs=[hbm, vmem],
        out_shape=[jax.ShapeDtypeStruct((3,) + gbig.shape[1:], gbig.dtype),
                   jax.ShapeDtypeStruct(gsmall.shape, gsmall.dtype)],
        scratch_shapes=[pltpu.VMEM((N_DEV,) + gsmall.shape, gsmall.dtype),
                        pltpu.SemaphoreType.DMA((3,)), pltpu.SemaphoreType.DMA((3,)),
                        pltpu.SemaphoreType.DMA((N_DEV - 1,)), pltpu.SemaphoreType.DMA((N_DEV - 1,))],
        name="exchange_grads")(gbig, gsmall)


def _sum_shards(chip, mine, got):
    r, cols = mine.shape[1:]
    tr = _tile(r, PACK_TILE_ROWS, 16)

    def body(chip_ref, mine_ref, got_ref, o_ref):
        o_ref[...] = ((mine_ref[0] + got_ref[0].astype(F32)) + got_ref[1].astype(F32)) + got_ref[2].astype(F32)

    grid_spec = pltpu.PrefetchScalarGridSpec(
        num_scalar_prefetch=1, grid=(r // tr,),
        in_specs=[pl.BlockSpec((1, tr, cols), lambda i, chip_ref: (chip_ref[0], i, 0)),
                  pl.BlockSpec((3, tr, cols), lambda i, chip_ref: (0, i, 0))],
        out_specs=pl.BlockSpec((tr, cols), lambda i, chip_ref: (i, 0)))
    return pl.pallas_call(
        body, grid_spec=grid_spec, out_shape=jax.ShapeDtypeStruct((r, cols), F32),
        compiler_params=_cp("parallel"), name="sum_shards")(chip, mine, got)


def _swap_sibling(t, *, name):
    def body(t_ref, got_ref, send_sem, recv_sem):
        x, y, c = _my_place()
        cp = pltpu.make_async_remote_copy(src_ref=t_ref, dst_ref=got_ref, send_sem=send_sem, recv_sem=recv_sem,
                                          device_id=(x, y, 1 - c), device_id_type=MESH)
        cp.start()
        cp.wait()

    hbm = pl.BlockSpec(memory_space=pltpu.HBM)
    return pl.pallas_call(
        body, in_specs=[hbm], out_specs=hbm, out_shape=jax.ShapeDtypeStruct(t.shape, t.dtype),
        scratch_shapes=[pltpu.SemaphoreType.DMA, pltpu.SemaphoreType.DMA], name=name)(t)


def _give_halves(parts, *, name):
    n = len(parts)
    r, cols = parts[0].shape
    half = r // 2

    def body(*refs):
        got_ref, send_sems, recv_sems = refs[n:]
        x, y, c = _my_place()
        theirs = pl.ds(pl.multiple_of((1 - c) * half, 16), half)
        copies = [pltpu.make_async_remote_copy(src_ref=refs[k].at[theirs, :], dst_ref=got_ref.at[k],
                                               send_sem=send_sems.at[k], recv_sem=recv_sems.at[k],
                                               device_id=(x, y, 1 - c), device_id_type=MESH) for k in range(n)]
        for cp in copies:
            cp.start()
        for cp in copies:
            cp.wait()

    hbm = pl.BlockSpec(memory_space=pltpu.HBM)
    return pl.pallas_call(
        body, in_specs=[hbm] * n, out_specs=hbm, out_shape=jax.ShapeDtypeStruct((n, half, cols), parts[0].dtype),
        scratch_shapes=[pltpu.SemaphoreType.DMA((n,)), pltpu.SemaphoreType.DMA((n,))], name=name)(*parts)


def _add_kept(core, parts, got, *, name):
    n, r, cols = got.shape
    tr = _tile(r, PACK_TILE_ROWS, 16)
    nblk = r // tr

    def body(core_ref, *refs):
        b_ref, o_ref, ob_ref = refs[n:]
        for k in range(n):
            sm = refs[k][...] + b_ref[k]
            o_ref[k] = sm
            ob_ref[k] = sm.astype(BF)

    blk = pl.BlockSpec((n, tr, cols), lambda i, core_ref: (0, i, 0))
    kept = pl.BlockSpec((tr, cols), lambda i, core_ref: (core_ref[0] * nblk + i, 0))
    grid_spec = pltpu.PrefetchScalarGridSpec(
        num_scalar_prefetch=1, grid=(nblk,), in_specs=[kept] * n + [blk], out_specs=[blk, blk])
    return pl.pallas_call(
        body, grid_spec=grid_spec,
        out_shape=[jax.ShapeDtypeStruct(got.shape, F32), jax.ShapeDtypeStruct(got.shape, BF)],
        compiler_params=_cp("parallel"), name=name)(core, *parts, got)


def _by_core(c, mine, sibling):
    return jnp.where(c == 0, mine, sibling), jnp.where(c == 0, sibling, mine)


def _adamw(wt, m, v, g, *, name):
    r, cols = wt.shape
    tr = _tile(r, 256, 8)

    def body(w_ref, m_ref, v_ref, g_ref, d_ref, nm_ref, nv_ref):
        gv = g_ref[...]
        nm = ADAM_B1 * m_ref[...] + (1.0 - ADAM_B1) * gv
        nv = ADAM_B2 * v_ref[...] + (1.0 - ADAM_B2) * (gv * gv)
        m_hat = nm / (1.0 - ADAM_B1 ** ADAM_STEP)
        v_hat = nv / (1.0 - ADAM_B2 ** ADAM_STEP)
        d_ref[...] = -ADAM_LR * (m_hat / (jnp.sqrt(v_hat) + ADAM_EPS) + ADAM_WD * w_ref[...])
        nm_ref[...] = nm
        nv_ref[...] = nv

    blk = pl.BlockSpec((tr, cols), lambda i: (i, 0))
    return pl.pallas_call(
        body, grid=(r // tr,), in_specs=[blk] * 4, out_specs=[blk] * 3,
        out_shape=[jax.ShapeDtypeStruct((r, cols), F32)] * 3,
        compiler_params=_cp("parallel"), name=name)(wt, m, v, g)


def _shard_shape(full, axis):
    return tuple(d // N_CHIPS if a == axis else d for a, d in enumerate(full))


def _pack_rows(n):
    rows = -(-n // PACK_COLS)
    return -(-rows // PACK_ROW_MULT) * PACK_ROW_MULT


def _pack(parts, rows, dtype):
    mat = jnp.concatenate([p.astype(dtype).reshape(-1, PACK_COLS) for p in parts], axis=0)
    return jnp.pad(mat, ((0, rows - mat.shape[0]), (0, 0)))


def _join_shards(stacked, axis):
    moved = jnp.moveaxis(stacked, 0, axis)
    shp = moved.shape
    return moved.reshape(shp[:axis] + (shp[axis] * shp[axis + 1],) + shp[axis + 2:])


def _split_shards(full, axis):
    shp = full.shape
    split = full.reshape(shp[:axis] + (N_CHIPS, shp[axis] // N_CHIPS) + shp[axis + 1:])
    return jnp.moveaxis(split, axis, 0)


SMALL_ROWS = 128
SMALL_LAYOUT = [('mix_norm', DEPTH * D_MODEL), ('ffn_norm', DEPTH * D_MODEL), ('b_conv_w', 3 * D_MODEL),
                ('c_q_a_norm', C_Q_RANK), ('c_kv_a_norm', C_KV_RANK), ('a_q_norm', 2 * A_HEAD_DIM),
                ('a_k_norm', 2 * A_HEAD_DIM), ('a_sinks', 2 * A_HEADS), ('c_q_norm', C_QK), ('c_k_norm', C_QK),
                ('loss', 1)]


def _small_offsets():
    offs, row = {}, 0
    for name, n in SMALL_LAYOUT:
        offs[name] = (row * LANES, n)
        row += -(-n // LANES)
    assert row <= SMALL_ROWS
    return offs


def kernel(x, positions, mix_norm, ffn_norm, a_w_qkv, a_q_norm, a_k_norm, a_sinks, a_w_o, b_w_in, b_conv_w, b_w_out, c_w_down, c_q_a_norm, c_kv_a_norm, c_w_q_up, c_w_kv_up, c_q_norm, c_k_norm, c_w_o, f_w_gate_up, f_w_down, loss_target, m_mix_norm, m_ffn_norm, m_a_w_qkv, m_a_q_norm, m_a_k_norm, m_a_sinks, m_a_w_o, m_b_w_in, m_b_conv_w, m_b_w_out, m_c_w_down, m_c_q_a_norm, m_c_kv_a_norm, m_c_w_q_up, m_c_w_kv_up, m_c_q_norm, m_c_k_norm, m_c_w_o, m_f_w_gate_up, m_f_w_down, v_mix_norm, v_ffn_norm, v_a_w_qkv, v_a_q_norm, v_a_k_norm, v_a_sinks, v_a_w_o, v_b_w_in, v_b_conv_w, v_b_w_out, v_c_w_down, v_c_q_a_norm, v_c_kv_a_norm, v_c_w_q_up, v_c_w_kv_up, v_c_q_norm, v_c_k_norm, v_c_w_o, v_f_w_gate_up, v_f_w_down):
    args = dict(locals())
    wshard = {n: args[n] for n in WEIGHTS}
    sharded = {**BIG, **SMALL_SHARDED}
    chip = 2 * lax.axis_index("x") + lax.axis_index("y")

    n_big = sum(wshard[n].size for n in BIG)
    rows = _pack_rows(n_big)
    big = _pack([wshard[n] for n in BIG], rows, BF)
    small = jnp.concatenate([wshard[n].reshape(-1) for n in SMALL_SHARDED])
    small = jnp.pad(small, (0, 8 * LANES - small.shape[0])).reshape(8, LANES)
    core = lax.axis_index("c")
    big_all, small_all = _gather_weights(big, small)
    big_all = big_all.reshape(N_CHIPS, -1)
    small_all = small_all.reshape(N_CHIPS, -1)
    w = {}
    off = 0
    for n, axis in BIG.items():
        shp = wshard[n].shape
        per = shp[1] * shp[2]
        w[n] = [_join_shards(big_all[:, off + l * per:off + (l + 1) * per].reshape((N_CHIPS,) + shp[1:]), axis - 1)
                for l in range(shp[0])]
        off += wshard[n].size
    off = 0
    for n, axis in SMALL_SHARDED.items():
        sz = wshard[n].size
        w[n] = _join_shards(small_all[:, off:off + sz].reshape((N_CHIPS,) + wshard[n].shape), axis)
        off += sz
    for n in WEIGHTS:
        if n not in sharded:
            w[n] = wshard[n]

    loss_blk, grad_x, g = _local_step(x[0], positions[0], w, loss_target[0])

    def shard_of(n, item, k):
        if isinstance(item, tuple):
            item = item[k // 2]
            k = k % 2
            width = item.shape[1] // 2
        else:
            width = item.shape[BIG[n] - 1] // N_CHIPS
        return item[:, k * width:(k + 1) * width] if BIG[n] == 2 else item[k * width:(k + 1) * width]

    gparts = [_pack([shard_of(n, item, k) for n in BIG for item in g[n]], rows, F32) for k in range(N_CHIPS)]
    offs = _small_offsets()
    smalls = {**{n: g[n] for n, _ in SMALL_LAYOUT if n != 'loss'}, 'loss': loss_blk[0:1, 0:1]}
    gsmall = jnp.concatenate(
        [jnp.pad(smalls[n].reshape(-1), (0, -cnt % LANES)) for n, cnt in SMALL_LAYOUT])
    gsmall = jnp.pad(gsmall, (0, SMALL_ROWS * LANES - gsmall.shape[0])).reshape(SMALL_ROWS, LANES)
    from_sibling = _give_halves(gparts, name="swap_grads")
    chip_sum, chip_sum_bf = _add_kept(core.reshape(1).astype(jnp.int32), gparts, from_sibling, name="add_sibling")
    got, ssum = _exchange_grads(chip_sum_bf, gsmall)
    part = _sum_shards(chip.reshape(1).astype(jnp.int32), chip_sum, got)
    part_sib = _swap_sibling(part, name="swap_sums")
    gflat = jnp.concatenate(_by_core(core, part, part_sib), axis=0).reshape(-1)

    ssum = ssum.reshape(-1)
    outs = {}
    off = 0
    for n in WEIGHTS:
        wt = wshard[n]
        if n in BIG:
            grad = gflat[off:off + wt.size]
            off += wt.size
        else:
            o0, cnt = offs[n]
            grad = ssum[o0:o0 + cnt].reshape(g[n].shape)
            if n in SMALL_SHARDED:
                grad = lax.dynamic_index_in_dim(_split_shards(grad, SMALL_SHARDED[n]), chip, 0, keepdims=False)
        shape2 = (-1, wt.shape[-1])
        grad = grad.reshape(shape2)
        res = _adamw(wt.reshape(shape2), args['m_' + n].reshape(shape2), args['v_' + n].reshape(shape2), grad,
                     name=f"adamw_{n}")
        outs[n] = [r.reshape(wt.shape) for r in [grad] + list(res)]
    loss = ssum[offs['loss'][0]]
    return (loss, grad_x[None], *[outs[n][0] for n in WEIGHTS], *[outs[n][1] for n in WEIGHTS],
            *[outs[n][2] for n in WEIGHTS], *[outs[n][3] for n in WEIGHTS])
```

```python
import functools

import jax
import jax.numpy as jnp
from jax import lax
from jax.experimental import pallas as pl
from jax.experimental.pallas import tpu as pltpu

F32 = jnp.float32
BF = jnp.bfloat16

D_MODEL = 1024
DEPTH = 4
N_MIXERS = 3
ROPE_THETA = 500000.0
EPS = 1e-6
BLOCK = 128
LANES = 128

A_HEADS, A_KV_HEADS, A_HEAD_DIM = 16, 4, 64
A_ROT_DIM = A_HEAD_DIM // 4
A_GROUP = A_HEADS // A_KV_HEADS
C_HEADS, C_NOPE, C_ROPE, C_V, C_Q_RANK, C_KV_RANK = 16, 64, 32, 64, 384, 256
C_QK = C_NOPE + C_ROPE
D_FF = 2816

ADAM_LR, ADAM_B1, ADAM_B2, ADAM_EPS, ADAM_WD, ADAM_STEP = 0.001, 0.9, 0.999, 1e-08, 0.01, 10

N_CHIPS = 4
N_DEV = 8
MESH = pl.DeviceIdType.MESH

VMEM_LIMIT = 56 * 1024 * 1024
ROW_TILE = 512
PREP_TILE = 256
ATTN_TILE = 512
FWD_WIDE = 4
SWA_BLOCKS = 16
NEG = -1e30

WEIGHTS = ['mix_norm', 'ffn_norm', 'a_w_qkv', 'a_q_norm', 'a_k_norm', 'a_sinks', 'a_w_o', 'b_w_in', 'b_conv_w',
           'b_w_out', 'c_w_down', 'c_q_a_norm', 'c_kv_a_norm', 'c_w_q_up', 'c_w_kv_up', 'c_q_norm', 'c_k_norm',
           'c_w_o', 'f_w_gate_up', 'f_w_down']
BIG = {'a_w_qkv': 2, 'a_w_o': 1, 'b_w_in': 2, 'b_w_out': 1, 'c_w_down': 1, 'c_w_q_up': 2, 'c_w_kv_up': 2,
       'c_w_o': 1, 'f_w_gate_up': 2, 'f_w_down': 1}
SMALL_SHARDED = {'b_conv_w': 2, 'c_q_a_norm': 1, 'c_kv_a_norm': 1}
PACK_COLS = 256
PACK_ROW_MULT = 32
PACK_TILE_ROWS = 1024
GATHER_CHUNKS = 3


def _cp(*sem):
    return pltpu.CompilerParams(dimension_semantics=sem, vmem_limit_bytes=VMEM_LIMIT)


def _tile(n, target, mult):
    if n <= target:
        return n
    t = (target // mult) * mult
    while t >= mult:
        if n % t == 0:
            return t
        t -= mult
    raise ValueError(f"no tile for {n}")


def _dot(a, b):
    return lax.dot_general(a, b, (((1,), (0,)), ((), ())), preferred_element_type=F32)


def _dot_nt(a, b):
    return lax.dot_general(a, b, (((1,), (1,)), ((), ())), preferred_element_type=F32)


def _dot_tn(a, b):
    return lax.dot_general(a, b, (((0,), (0,)), ((), ())), preferred_element_type=F32)


def _mm(pairs, *, out_dtype, name, res=None, trans_b=False, norm_g=None):
    a0, b0, _ = pairs[0]
    m = a0.shape[0]
    n = b0.shape[0] if trans_b else b0.shape[1]
    tm = _tile(m, 2 * ROW_TILE if sum(a.shape[1] for a, _, _ in pairs) <= 1024 else ROW_TILE, 8)
    tn = _tile(n, 1024, 128)
    n_pairs = len(pairs)
    has_res = res is not None
    has_norm = norm_g is not None
    assert not has_norm or tn == n

    def body(*refs):
        acc = None
        for p in range(n_pairs):
            a = refs[2 * p][...].astype(BF)
            b = refs[2 * p + 1][...].astype(BF)
            d = _dot_nt(a, b) if trans_b else _dot(a, b)
            acc = d if acc is None else acc + d
        nxt = 2 * n_pairs
        if has_res:
            acc = acc + refs[nxt][...]
            nxt += 1
        if has_norm:
            y = acc * lax.rsqrt(jnp.mean(acc * acc, axis=-1, keepdims=True) + EPS)
            refs[-1][...] = (y * refs[nxt][...]).astype(BF)
            refs[-2][...] = acc.astype(out_dtype)
        else:
            refs[-1][...] = acc.astype(out_dtype)

    in_specs, args = [], []
    for a, b, kblk in pairs:
        k = a.shape[1]
        in_specs.append(pl.BlockSpec((tm, k), lambda j, i: (i, 0)))
        if trans_b:
            in_specs.append(pl.BlockSpec((tn, k), functools.partial(lambda j, i, kb: (j, kb), kb=kblk)))
        else:
            in_specs.append(pl.BlockSpec((k, tn), lambda j, i: (0, j)))
        args += [a, b]
    if has_res:
        in_specs.append(pl.BlockSpec((tm, tn), lambda j, i: (i, j)))
        args.append(res)
    out_spec = pl.BlockSpec((tm, tn), lambda j, i: (i, j))
    out_specs, out_shape = out_spec, jax.ShapeDtypeStruct((m, n), out_dtype)
    if has_norm:
        in_specs.append(pl.BlockSpec((1, n), lambda j, i: (0, 0)))
        args.append(norm_g.reshape(1, n))
        out_specs, out_shape = [out_spec, out_spec], [out_shape, jax.ShapeDtypeStruct((m, n), BF)]
    return pl.pallas_call(
        body, grid=(n // tn, m // tm), in_specs=in_specs, out_specs=out_specs, out_shape=out_shape,
        compiler_params=_cp("parallel", "parallel"), name=name)(*args)


def _mm_norm_bwd(pairs, x, g, dres, *, name):
    m, d = x.shape
    tm = _tile(m, ROW_TILE, 8)
    n_pairs = len(pairs)

    def body(*refs):
        x_ref, g_ref, dres_ref, dx_ref, dg_ref = refs[2 * n_pairs:]

        @pl.when(pl.program_id(0) == 0)
        def _():
            dg_ref[...] = jnp.zeros_like(dg_ref)

        dh = None
        for p in range(n_pairs):
            dpart = _dot_nt(refs[2 * p][...].astype(BF), refs[2 * p + 1][...].astype(BF))
            dh = dpart if dh is None else dh + dpart
        dx, xhat = _rms_bwd_math(x_ref[...], g_ref[...], dh, d)
        dx_ref[...] = dres_ref[...] + dx
        dg_ref[0:1, :] += jnp.sum(dh * xhat, axis=0, keepdims=True)

    in_specs, args = [], []
    for a, b, kblk in pairs:
        k = a.shape[1]
        in_specs.append(pl.BlockSpec((tm, k), lambda i: (i, 0)))
        in_specs.append(pl.BlockSpec((d, k), functools.partial(lambda i, kb: (0, kb), kb=kblk)))
        args += [a, b]
    row = pl.BlockSpec((tm, d), lambda i: (i, 0))
    in_specs += [row, pl.BlockSpec((1, d), lambda i: (0, 0)), row]
    dx, dg = pl.pallas_call(
        body, grid=(m // tm,), in_specs=in_specs,
        out_specs=[row, pl.BlockSpec((8, d), lambda i: (0, 0))],
        out_shape=[jax.ShapeDtypeStruct((m, d), F32), jax.ShapeDtypeStruct((8, d), F32)],
        compiler_params=_cp("arbitrary"), name=name)(*args, x, g.reshape(1, d), dres)
    return dx, dg[0:1]


def _mm_tn(a, b, *, name):
    m, k = a.shape
    n = b.shape[1]
    tm = _tile(m, 2 * ROW_TILE, 8)
    tk = _tile(k, 1408, 128)
    tn = _tile(n, 1408, 128)

    def body(a_ref, b_ref, o_ref):
        @pl.when(pl.program_id(2) == 0)
        def _():
            o_ref[...] = jnp.zeros_like(o_ref)

        o_ref[...] += _dot_tn(a_ref[...].astype(BF), b_ref[...].astype(BF))

    return pl.pallas_call(
        body, grid=(k // tk, n // tn, m // tm),
        in_specs=[pl.BlockSpec((tm, tk), lambda kk, j, i: (i, kk)),
                  pl.BlockSpec((tm, tn), lambda kk, j, i: (i, j))],
        out_specs=pl.BlockSpec((tk, tn), lambda kk, j, i: (kk, j)),
        out_shape=jax.ShapeDtypeStruct((k, n), F32),
        compiler_params=_cp("parallel", "parallel", "arbitrary"), name=name)(a, b)


def _rmsnorm_fwd(x, g, *, name):
    s, d = x.shape
    tm = _tile(s, ROW_TILE, 8)

    def body(x_ref, g_ref, h_ref):
        xv = x_ref[...]
        y = xv * lax.rsqrt(jnp.mean(xv * xv, axis=-1, keepdims=True) + EPS)
        h_ref[...] = (y * g_ref[...]).astype(BF)

    return pl.pallas_call(
        body, grid=(s // tm,),
        in_specs=[pl.BlockSpec((tm, d), lambda i: (i, 0)), pl.BlockSpec((1, d), lambda i: (0, 0))],
        out_specs=pl.BlockSpec((tm, d), lambda i: (i, 0)),
        out_shape=jax.ShapeDtypeStruct((s, d), BF),
        compiler_params=_cp("parallel"), name=name)(x, g.reshape(1, d))


def _row_sum(v):
    return jnp.sum(v, axis=-1, keepdims=True)


def _slot_sum(v):
    ones = jnp.ones((LANES, LANES), BF)
    hi = v.astype(BF)
    lo = (v - hi.astype(F32)).astype(BF)
    return _dot(hi, ones) + _dot(lo, ones)


def _rms_bwd_math(xv, g, dh, n, row_sum=_row_sum):
    rstd = lax.rsqrt(row_sum(xv * xv) * (1.0 / n) + EPS)
    xhat = xv * rstd
    dxh = dh * g
    dx = rstd * (dxh - xhat * (row_sum(dxh * xhat) * (1.0 / n)))
    return dx, xhat


def _loss_fwd_bwd(y, target, *, name):
    s, d = y.shape
    tm = _tile(s, ROW_TILE, 8)

    def body(y_ref, t_ref, loss_ref, dy_ref):
        @pl.when(pl.program_id(0) == 0)
        def _():
            loss_ref[...] = jnp.zeros_like(loss_ref)

        err = y_ref[...] - t_ref[...]
        dy_ref[...] = err * (1.0 / d)
        loss_ref[...] += 0.5 * jnp.sum(jnp.sum(err * err, axis=-1, keepdims=True) * (1.0 / d))

    row = pl.BlockSpec((tm, d), lambda i: (i, 0))
    return pl.pallas_call(
        body, grid=(s // tm,), in_specs=[row, row],
        out_specs=[pl.BlockSpec((8, LANES), lambda i: (0, 0)), row],
        out_shape=[jax.ShapeDtypeStruct((8, LANES), F32), jax.ShapeDtypeStruct((s, d), F32)],
        compiler_params=_cp("arbitrary"), name=name)(y, target)


HALF = LANES // 2
A_HR, C_HR = A_ROT_DIM // 2, C_ROPE // 2
A_X1, C_X1 = 0, 32
LAYOUT_A = [(A_X1, 0, A_HR), (A_HR, A_ROT_DIM, A_HEAD_DIM - A_ROT_DIM), (HALF + A_X1, A_HR, A_HR)]
LAYOUT_C = [(0, 0, 32), (C_X1, C_NOPE, C_HR), (HALF, 32, 32), (HALF + C_X1, C_NOPE + C_HR, C_HR)]
LAYOUT_KN = [(0, 0, 32), (HALF, 32, 32)]
LAYOUT_KR = [(C_X1, 0, C_HR), (HALF + C_X1, C_HR, C_HR)]
LAYOUT_V = [(0, 0, C_V)]


def _rope_rows():
    lane = jnp.arange(LANES)
    fa = ROPE_THETA ** (-jnp.arange(0, A_ROT_DIM, 2, dtype=F32) / A_ROT_DIM)
    fc = ROPE_THETA ** (-jnp.arange(0, C_ROPE, 2, dtype=F32) / C_ROPE)

    def rows(f, x1, hr):
        first = (lane >= x1) & (lane < x1 + hr)
        second = (lane >= HALF + x1) & (lane < HALF + x1 + hr)
        freq = jnp.where(first | second, f[(lane - x1) % HALF % hr], 0.0)
        return freq, jnp.where(first, -1.0, jnp.where(second, 1.0, 0.0))

    freq_a, sign_a = rows(fa, A_X1, A_HR)
    freq_c, sign_c = rows(fc, C_X1, C_HR)
    return jnp.stack([freq_a, sign_a, freq_c, sign_c] + [jnp.zeros(LANES)] * 4).astype(F32)


def _rope_tables(pos_col, *, name):
    s = pos_col.shape[0]
    tm = _tile(s, ROW_TILE, 8)

    def body(pos_ref, rows_ref, ca_ref, sa_ref, cc_ref, sc_ref):
        p = pos_ref[...].astype(F32)
        ang_a = p * rows_ref[0:1, :]
        ang_c = p * rows_ref[2:3, :]
        ca_ref[...] = jnp.cos(ang_a)
        sa_ref[...] = jnp.sin(ang_a) * rows_ref[1:2, :]
        cc_ref[...] = jnp.cos(ang_c)
        sc_ref[...] = jnp.sin(ang_c) * rows_ref[3:4, :]

    tab = pl.BlockSpec((tm, LANES), lambda i: (i, 0))
    return pl.pallas_call(
        body, grid=(s // tm,),
        in_specs=[pl.BlockSpec((tm, 1), lambda i: (i, 0)), pl.BlockSpec((8, LANES), lambda i: (0, 0))],
        out_specs=[tab] * 4, out_shape=[jax.ShapeDtypeStruct((s, LANES), F32)] * 4,
        compiler_params=_cp("parallel"), name=name)(pos_col, _rope_rows())


def _rope(xv, cos, sin):
    return xv * cos + pltpu.roll(xv, HALF, 1) * sin


def _slot(ref, t):
    return ref[:, t * LANES:(t + 1) * LANES]


def _head_norm_rope(xv, g, cos, sin, n):
    y = xv * lax.rsqrt(_slot_sum(xv * xv) * (1.0 / n) + EPS) * g
    return _rope(y, cos, sin)


def _head_norm_rope_bwd(xv, g, dy, cos, sin, n):
    dyn = _rope(dy, cos, -sin)
    dx, xhat = _rms_bwd_math(xv, g, dyn, n, _slot_sum)
    return dx, jnp.sum(dyn * xhat, axis=0, keepdims=True)


def _prep_a_fwd(qkv, cos, sin, gq, gk, *, name):
    s = qkv.shape[0]
    tm = _tile(s, PREP_TILE, 8)
    nq, nkv = A_HEADS, A_KV_HEADS

    def body(x_ref, c_ref, s_ref, gq_ref, gk_ref, q_ref, k_ref, v_ref):
        cv, sv = c_ref[...], s_ref[...]
        for t in range(nq):
            q_ref[:, t * LANES:(t + 1) * LANES] = _head_norm_rope(
                _slot(x_ref, t), gq_ref[...], cv, sv, A_HEAD_DIM).astype(BF)
        for t in range(nkv):
            k_ref[:, t * LANES:(t + 1) * LANES] = _head_norm_rope(
                _slot(x_ref, nq + t), gk_ref[...], cv, sv, A_HEAD_DIM).astype(BF)
        v_ref[...] = x_ref[:, (nq + nkv) * LANES:].astype(BF)

    def rows(w):
        return pl.BlockSpec((tm, w), lambda i: (i, 0))

    vec = pl.BlockSpec((1, LANES), lambda i: (0, 0))
    return pl.pallas_call(
        body, grid=(s // tm,),
        in_specs=[rows(qkv.shape[1]), rows(LANES), rows(LANES), vec, vec],
        out_specs=[rows(nq * LANES), rows(nkv * LANES), rows(nkv * LANES)],
        out_shape=[jax.ShapeDtypeStruct((s, nq * LANES), BF), jax.ShapeDtypeStruct((s, nkv * LANES), BF),
                   jax.ShapeDtypeStruct((s, nkv * LANES), BF)],
        compiler_params=_cp("parallel"), name=name)(qkv, cos, sin, gq, gk)


def _prep_a_bwd(qkv, dq, dk, dv, cos, sin, gq, gk, *, name):
    s = qkv.shape[0]
    tm = _tile(s, PREP_TILE, 8)
    nq, nkv = A_HEADS, A_KV_HEADS

    def body(x_ref, dq_ref, dk_ref, dv_ref, c_ref, s_ref, gq_ref, gk_ref, dx_ref, dg_ref):
        @pl.when(pl.program_id(0) == 0)
        def _():
            dg_ref[...] = jnp.zeros_like(dg_ref)

        cv, sv = c_ref[...], s_ref[...]
        dgq = jnp.zeros((1, LANES), F32)
        dgk = jnp.zeros((1, LANES), F32)
        for t in range(nq):
            dx, dg = _head_norm_rope_bwd(_slot(x_ref, t), gq_ref[...], _slot(dq_ref, t), cv, sv,
                                         A_HEAD_DIM)
            dx_ref[:, t * LANES:(t + 1) * LANES] = dx.astype(BF)
            dgq = dgq + dg
        for t in range(nkv):
            dx, dg = _head_norm_rope_bwd(_slot(x_ref, nq + t), gk_ref[...], _slot(dk_ref, t), cv, sv,
                                         A_HEAD_DIM)
            dx_ref[:, (nq + t) * LANES:(nq + t + 1) * LANES] = dx.astype(BF)
            dgk = dgk + dg
        dx_ref[:, (nq + nkv) * LANES:] = dv_ref[...].astype(BF)
        dg_ref[0:1, :] += dgq
        dg_ref[1:2, :] += dgk

    def rows(w):
        return pl.BlockSpec((tm, w), lambda i: (i, 0))

    vec = pl.BlockSpec((1, LANES), lambda i: (0, 0))
    dx, dg = pl.pallas_call(
        body, grid=(s // tm,),
        in_specs=[rows(qkv.shape[1]), rows(nq * LANES), rows(nkv * LANES), rows(nkv * LANES), rows(LANES),
                  rows(LANES), vec, vec],
        out_specs=[rows(qkv.shape[1]), pl.BlockSpec((8, LANES), lambda i: (0, 0))],
        out_shape=[jax.ShapeDtypeStruct(qkv.shape, BF), jax.ShapeDtypeStruct((8, LANES), F32)],
        compiler_params=_cp("arbitrary"), name=name)(qkv, dq, dk, dv, cos, sin, gq, gk)
    return dx, dg[0:1], dg[1:2]


def _prep_c1_fwd(dn, gq, gkv, *, name):
    s = dn.shape[0]
    tm = _tile(s, ROW_TILE, 8)

    def body(x_ref, gq_ref, gkv_ref, cq_ref, ckv_ref):
        for lo, n, g_ref, o_ref in ((0, C_Q_RANK, gq_ref, cq_ref), (C_Q_RANK, C_KV_RANK, gkv_ref, ckv_ref)):
            xv = x_ref[:, lo:lo + n]
            y = xv * lax.rsqrt(jnp.mean(xv * xv, axis=-1, keepdims=True) + EPS)
            o_ref[...] = (y * g_ref[...]).astype(BF)

    def rows(w):
        return pl.BlockSpec((tm, w), lambda i: (i, 0))

    return pl.pallas_call(
        body, grid=(s // tm,),
        in_specs=[rows(dn.shape[1]), pl.BlockSpec((1, C_Q_RANK), lambda i: (0, 0)),
                  pl.BlockSpec((1, C_KV_RANK), lambda i: (0, 0))],
        out_specs=[rows(C_Q_RANK), rows(C_KV_RANK)],
        out_shape=[jax.ShapeDtypeStruct((s, C_Q_RANK), BF), jax.ShapeDtypeStruct((s, C_KV_RANK), BF)],
        compiler_params=_cp("parallel"), name=name)(dn, gq, gkv)


def _prep_c1_bwd(dn, dcq, dckv, dkr, gq, gkv, *, name):
    s = dn.shape[0]
    tm = _tile(s, ROW_TILE, 8)

    def body(x_ref, dcq_ref, dckv_ref, dkr_ref, gq_ref, gkv_ref, dx_ref, dgq_ref, dgkv_ref):
        @pl.when(pl.program_id(0) == 0)
        def _():
            dgq_ref[...] = jnp.zeros_like(dgq_ref)
            dgkv_ref[...] = jnp.zeros_like(dgkv_ref)

        for lo, n, g_ref, d_ref, dg_ref in ((0, C_Q_RANK, gq_ref, dcq_ref, dgq_ref),
                                            (C_Q_RANK, C_KV_RANK, gkv_ref, dckv_ref, dgkv_ref)):
            dv = d_ref[...]
            dx, xhat = _rms_bwd_math(x_ref[:, lo:lo + n], g_ref[...], dv, n)
            dx_ref[:, lo:lo + n] = dx.astype(BF)
            dg_ref[0:1, :] += jnp.sum(dv * xhat, axis=0, keepdims=True)
        dx_ref[:, C_Q_RANK + C_KV_RANK:] = dkr_ref[...].astype(BF)

    def rows(w):
        return pl.BlockSpec((tm, w), lambda i: (i, 0))

    def vec(w, r=1):
        return pl.BlockSpec((r, w), lambda i: (0, 0))

    dx, dgq, dgkv = pl.pallas_call(
        body, grid=(s // tm,),
        in_specs=[rows(dn.shape[1]), rows(C_Q_RANK), rows(C_KV_RANK), rows(LANES), vec(C_Q_RANK), vec(C_KV_RANK)],
        out_specs=[rows(dn.shape[1]), vec(C_Q_RANK, 8), vec(C_KV_RANK, 8)],
        out_shape=[jax.ShapeDtypeStruct(dn.shape, BF), jax.ShapeDtypeStruct((8, C_Q_RANK), F32),
                   jax.ShapeDtypeStruct((8, C_KV_RANK), F32)],
        compiler_params=_cp("arbitrary"), name=name)(dn, dcq, dckv, dkr, gq, gkv)
    return dx, dgq[0:1], dgkv[0:1]


def _prep_c2_fwd(qraw, knope, dn, cos, sin, gq, gk, *, name):
    s = qraw.shape[0]
    tm = _tile(s, PREP_TILE, 8)
    kr_blk = dn.shape[1] // LANES - 1

    def body(q_ref, kn_ref, kr_ref, c_ref, s_ref, gq_ref, gk_ref, qo_ref, ko_ref):
        cv, sv, kr = c_ref[...], s_ref[...], kr_ref[...]
        for t in range(C_HEADS):
            qo_ref[:, t * LANES:(t + 1) * LANES] = _head_norm_rope(
                _slot(q_ref, t), gq_ref[...], cv, sv, C_QK).astype(BF)
            ko_ref[:, t * LANES:(t + 1) * LANES] = _head_norm_rope(
                _slot(kn_ref, t) + kr, gk_ref[...], cv, sv, C_QK).astype(BF)

    def rows(w, blk=0):
        return pl.BlockSpec((tm, w), lambda i: (i, blk))

    vec = pl.BlockSpec((1, LANES), lambda i: (0, 0))
    w = C_HEADS * LANES
    return pl.pallas_call(
        body, grid=(s // tm,),
        in_specs=[rows(w), rows(w), rows(LANES, kr_blk), rows(LANES), rows(LANES), vec, vec],
        out_specs=[rows(w), rows(w)],
        out_shape=[jax.ShapeDtypeStruct((s, w), BF)] * 2,
        compiler_params=_cp("parallel"), name=name)(qraw, knope, dn, cos, sin, gq, gk)


def _prep_c2_bwd(qraw, knope, dn, dq, dk, cos, sin, gq, gk, *, name):
    s = qraw.shape[0]
    tm = _tile(s, PREP_TILE, 8)
    kr_blk = dn.shape[1] // LANES - 1

    def body(q_ref, kn_ref, kr_ref, dq_ref, dk_ref, c_ref, s_ref, gq_ref, gk_ref,
             dqo_ref, dkno_ref, dkr_ref, dg_ref):
        @pl.when(pl.program_id(0) == 0)
        def _():
            dg_ref[...] = jnp.zeros_like(dg_ref)

        cv, sv, kr = c_ref[...], s_ref[...], kr_ref[...]
        dgq = jnp.zeros((1, LANES), F32)
        dgk = jnp.zeros((1, LANES), F32)
        dkr = jnp.zeros((tm, LANES), F32)
        for t in range(C_HEADS):
            dx, dg = _head_norm_rope_bwd(_slot(q_ref, t), gq_ref[...], _slot(dq_ref, t), cv, sv,
                                         C_QK)
            dqo_ref[:, t * LANES:(t + 1) * LANES] = dx.astype(BF)
            dgq = dgq + dg
            dx, dg = _head_norm_rope_bwd(_slot(kn_ref, t) + kr, gk_ref[...], _slot(dk_ref, t), cv, sv,
                                         C_QK)
            dkno_ref[:, t * LANES:(t + 1) * LANES] = dx.astype(BF)
            dkr = dkr + dx
            dgk = dgk + dg
        dkr_ref[...] = dkr
        dg_ref[0:1, :] += dgq
        dg_ref[1:2, :] += dgk

    def rows(w, blk=0):
        return pl.BlockSpec((tm, w), lambda i: (i, blk))

    vec = pl.BlockSpec((1, LANES), lambda i: (0, 0))
    w = C_HEADS * LANES
    dqo, dkno, dkr, dg = pl.pallas_call(
        body, grid=(s // tm,),
        in_specs=[rows(w), rows(w), rows(LANES, kr_blk), rows(w), rows(w), rows(LANES), rows(LANES), vec, vec],
        out_specs=[rows(w), rows(w), rows(LANES), pl.BlockSpec((8, LANES), lambda i: (0, 0))],
        out_shape=[jax.ShapeDtypeStruct((s, w), BF), jax.ShapeDtypeStruct((s, w), BF),
                   jax.ShapeDtypeStruct((s, LANES), F32), jax.ShapeDtypeStruct((8, LANES), F32)],
        compiler_params=_cp("arbitrary"), name=name)(qraw, knope, dn, dq, dk, cos, sin, gq, gk)
    return dqo, dkno, dkr, dg[0:1], dg[1:2]


def _lane_pick(mat, g):
    lane = lax.broadcasted_iota(jnp.int32, mat.shape, 1)
    return jnp.sum(jnp.where(lane == g, mat, 0.0), axis=-1, keepdims=True)


def _swa_fwd(q, k, v, sinks, *, name):
    s = q.shape[0]
    nb = min(SWA_BLOCKS, s // BLOCK)
    qt = nb * BLOCK
    scale = A_HEAD_DIM ** -0.5
    gw = A_GROUP * LANES

    def body(sink_ref, q_ref, kc_ref, kp_ref, vc_ref, vp_ref, o_ref, lse_ref):
        kvh, n = pl.program_id(0), pl.program_id(1)
        rows = A_GROUP * BLOCK
        kall = jnp.concatenate([kp_ref[...], kc_ref[...]], axis=0)
        vall = jnp.concatenate([vp_ref[...], vc_ref[...]], axis=0)
        r = lax.broadcasted_iota(jnp.int32, (rows, 2 * BLOCK), 0) & (BLOCK - 1)
        c = lax.broadcasted_iota(jnp.int32, (rows, 2 * BLOCK), 1)
        cur_ok = (c >= BLOCK) & (c - BLOCK <= r)
        prev_ok = (c < BLOCK) & (c > r)
        head = lax.broadcasted_iota(jnp.int32, (rows, 1), 0) >> 7
        sink = jnp.zeros((rows, 1), F32)
        for g in range(A_GROUP):
            sink = jnp.where(head == g, sink_ref[kvh * A_GROUP + g], sink)
        lane = lax.broadcasted_iota(jnp.int32, (BLOCK, LANES), 1)
        for b in range(nb):
            blk = slice(b * BLOCK, (b + 1) * BLOCK)
            qs = jnp.concatenate([q_ref[blk, g * LANES:(g + 1) * LANES] for g in range(A_GROUP)], axis=0)
            k2 = kall[b * BLOCK:(b + 2) * BLOCK]
            v2 = vall[b * BLOCK:(b + 2) * BLOCK]
            ok = (cur_ok | prev_ok) if b > 0 else (cur_ok | (prev_ok & (n > 0)))
            sc = jnp.where(ok, _dot_nt(qs, k2) * scale, NEG)
            m = jnp.maximum(jnp.max(sc, axis=-1, keepdims=True), sink)
            p = jnp.exp(sc - m)
            l = jnp.sum(p, axis=-1, keepdims=True) + jnp.exp(sink - m)
            o = _dot((p * (1.0 / l)).astype(BF), v2)
            lse = m + jnp.log(l)
            lse_mat = jnp.zeros((BLOCK, LANES), F32)
            for g in range(A_GROUP):
                o_ref[blk, g * LANES:(g + 1) * LANES] = o[g * BLOCK:(g + 1) * BLOCK].astype(BF)
                lse_mat = jnp.where(lane == g, lse[g * BLOCK:(g + 1) * BLOCK], lse_mat)
            lse_ref[0, blk, :] = lse_mat

    cur = pl.BlockSpec((qt, LANES), lambda h, n: (n, h))
    prev = pl.BlockSpec((BLOCK, LANES), lambda h, n: (jnp.maximum(n * nb - 1, 0), h))
    return pl.pallas_call(
        body, grid=(A_KV_HEADS, s // qt),
        in_specs=[pl.BlockSpec(memory_space=pltpu.SMEM), pl.BlockSpec((qt, gw), lambda h, n: (n, h)),
                  cur, prev, cur, prev],
        out_specs=[pl.BlockSpec((qt, gw), lambda h, n: (n, h)),
                   pl.BlockSpec((1, qt, LANES), lambda h, n: (h, n, 0))],
        out_shape=[jax.ShapeDtypeStruct(q.shape, BF), jax.ShapeDtypeStruct((A_KV_HEADS, s, LANES), F32)],
        compiler_params=_cp("parallel", "parallel"), name=name)(sinks, q, k, k, v, v)


def _swa_bwd(q, k, v, o, do, lse, sinks, *, name):
    s = q.shape[0]
    nblk = s // BLOCK
    nb = min(SWA_BLOCKS, nblk)
    qt = nb * BLOCK
    nsteps = s // qt
    scale = A_HEAD_DIM ** -0.5
    gw = A_GROUP * LANES

    def body(sink_ref, qc_ref, qn_ref, k_ref, v_ref, oc_ref, on_ref, doc_ref, don_ref, lc_ref, ln_ref,
             dq_ref, dk_ref, dv_ref, dsink_ref, carry):
        kvh, n = pl.program_id(0), pl.program_id(1)

        @pl.when(n == 0)
        def _():
            carry[...] = jnp.zeros_like(carry)
            dsink_ref[...] = jnp.zeros_like(dsink_ref)

        half = A_GROUP * BLOCK
        rows = 2 * half
        qall = jnp.concatenate([qc_ref[...], qn_ref[...]], axis=0)
        oall = jnp.concatenate([oc_ref[...], on_ref[...]], axis=0)
        doall = jnp.concatenate([doc_ref[...], don_ref[...]], axis=0)
        lall = jnp.concatenate([lc_ref[0], ln_ref[0]], axis=0)
        row = lax.broadcasted_iota(jnp.int32, (rows, BLOCK), 0)
        r = row & (BLOCK - 1)
        c = lax.broadcasted_iota(jnp.int32, (rows, BLOCK), 1)
        diag_ok = (row < half) & (c <= r)
        next_ok = (row >= half) & (c > r)
        lane8 = lax.broadcasted_iota(jnp.int32, (8, LANES), 1)
        dsink = jnp.zeros((8, LANES), F32)
        off_prev = None
        for b in range(nb):
            blk = slice(b * BLOCK, (b + 1) * BLOCK)
            kv, vv = k_ref[blk, :], v_ref[blk, :]

            def stack(allv):
                return jnp.concatenate(
                    [allv[b * BLOCK:(b + 1) * BLOCK, g * LANES:(g + 1) * LANES] for g in range(A_GROUP)]
                    + [allv[(b + 1) * BLOCK:(b + 2) * BLOCK, g * LANES:(g + 1) * LANES] for g in range(A_GROUP)],
                    axis=0)

            qs, osk, dos = stack(qall), stack(oall), stack(doall)
            lse = jnp.concatenate([_lane_pick(lall[b * BLOCK:(b + 1) * BLOCK], g) for g in range(A_GROUP)]
                                  + [_lane_pick(lall[(b + 1) * BLOCK:(b + 2) * BLOCK], g) for g in range(A_GROUP)],
                                  axis=0)
            delta = jnp.sum(dos.astype(F32) * osk.astype(F32), axis=-1, keepdims=True)
            ok = (diag_ok | next_ok) if b < nb - 1 else (diag_ok | (next_ok & (n < nsteps - 1)))
            sc = jnp.where(ok, _dot_nt(qs, kv) * scale, NEG)
            p = jnp.exp(sc - lse)
            dv_ref[blk, :] = _dot_tn(p.astype(BF), dos)
            dp = _dot_nt(dos, vv)
            ds = (p * (dp - delta) * scale).astype(BF)
            dk_ref[blk, :] = _dot_tn(ds, qs)
            dqs = _dot(ds, kv)
            for g in range(A_GROUP):
                cur = slice(g * BLOCK, (g + 1) * BLOCK)
                before = carry[:, g * LANES:(g + 1) * LANES] if b == 0 else off_prev[cur]
                dq_ref[blk, g * LANES:(g + 1) * LANES] = before + dqs[cur]
                p_sink = jnp.exp(sink_ref[kvh * A_GROUP + g] - lse[cur])
                dsink = dsink + jnp.where(lane8 == g, -jnp.sum(p_sink * delta[cur]), 0.0)
            off_prev = dqs[half:]
        for g in range(A_GROUP):
            carry[:, g * LANES:(g + 1) * LANES] = off_prev[g * BLOCK:(g + 1) * BLOCK]
        dsink_ref[...] += dsink

    def nxt(n):
        return jnp.minimum((n + 1) * nb, nblk - 1)

    grp_c = pl.BlockSpec((qt, gw), lambda h, n: (n, h))
    grp_n = pl.BlockSpec((BLOCK, gw), lambda h, n: (nxt(n), h))
    kvb = pl.BlockSpec((qt, LANES), lambda h, n: (n, h))
    lse_c = pl.BlockSpec((1, qt, LANES), lambda h, n: (h, n, 0))
    lse_n = pl.BlockSpec((1, BLOCK, LANES), lambda h, n: (h, nxt(n), 0))
    dq, dk, dv, dsink = pl.pallas_call(
        body, grid=(A_KV_HEADS, nsteps),
        in_specs=[pl.BlockSpec(memory_space=pltpu.SMEM), grp_c, grp_n, kvb, kvb, grp_c, grp_n, grp_c, grp_n,
                  lse_c, lse_n],
        out_specs=[grp_c, kvb, kvb, pl.BlockSpec((8, LANES), lambda h, n: (h, 0))],
        out_shape=[jax.ShapeDtypeStruct(q.shape, F32), jax.ShapeDtypeStruct(k.shape, F32),
                   jax.ShapeDtypeStruct(v.shape, F32), jax.ShapeDtypeStruct((A_KV_HEADS * 8, LANES), F32)],
        scratch_shapes=[pltpu.VMEM((BLOCK, gw), F32)],
        compiler_params=_cp("parallel", "arbitrary"), name=name)(sinks, q, q, k, v, o, o, do, do, lse, lse)
    dsinks = dsink.reshape(A_KV_HEADS, 8, LANES)[:, 0, :A_GROUP].reshape(A_HEADS)
    return dq, dk, dv, dsinks


LOG2E = 1.4426950408889634


def _mla_fwd(q, k, v, *, name):
    s = q.shape[0]
    t = _tile(s, ATTN_TILE, LANES)
    nt = s // t
    scale = C_QK ** -0.5
    c2 = scale * LOG2E

    def body(q_ref, k_ref, v_ref, o_ref, lse_ref, m_sc, acc_sc):
        qi = pl.program_id(1)
        qv = q_ref[...]
        m_sc[...] = jnp.full_like(m_sc, NEG)
        acc_sc[...] = jnp.zeros_like(acc_sc)

        def step(start, width, diag):
            rows = pl.ds(pl.multiple_of(start, t), width)
            sc = _dot_nt(qv, k_ref[rows, :])
            if diag:
                r = lax.broadcasted_iota(jnp.int32, (t, t), 0)
                c = lax.broadcasted_iota(jnp.int32, (t, t), 1)
                own = jnp.where(c <= r, sc[:, width - t:], NEG)
                sc = own if width == t else jnp.concatenate([sc[:, :width - t], own], axis=1)
            m_prev = m_sc[...]
            m_new = jnp.maximum(m_prev, jnp.max(sc, axis=-1, keepdims=True))
            alpha = jnp.exp2((m_prev - m_new) * c2)
            p = jnp.exp2((sc - jnp.tile(m_new, (1, width // LANES))) * c2).astype(BF)
            lane = lax.broadcasted_iota(jnp.int32, (width, LANES), 1)
            vv = jnp.where(lane == C_V, jnp.ones((), BF), v_ref[rows, :])
            acc_sc[...] = alpha * acc_sc[...] + _dot(p, vv)
            m_sc[...] = m_new

        def single_body(j, carry):
            step(j * t, t, False)
            return carry

        with_diag = qi >= FWD_WIDE - 1

        @pl.when(with_diag)
        def _():
            step((qi - (FWD_WIDE - 1)) * t, FWD_WIDE * t, True)

        @pl.when(jnp.logical_not(with_diag))
        def _():
            step(qi * t, t, True)

        rest = jnp.where(with_diag, qi - (FWD_WIDE - 1), qi)
        n_wide = rest // FWD_WIDE
        done = n_wide * FWD_WIDE
        n_pair = (rest - done) // 2

        def wide_body(wi, carry):
            step(wi * (FWD_WIDE * t), FWD_WIDE * t, False)
            return carry

        def pair_body(wi, carry):
            step((done + 2 * wi) * t, 2 * t, False)
            return carry

        lax.fori_loop(0, n_wide, wide_body, 0)
        lax.fori_loop(0, n_pair, pair_body, 0)
        lax.fori_loop(done + 2 * n_pair, rest, single_body, 0)
        acc = acc_sc[...]
        l = _lane_pick(acc, C_V)
        lane = lax.broadcasted_iota(jnp.int32, (t, LANES), 1)
        o_ref[...] = jnp.where(lane == C_V, 0.0, acc * (1.0 / l)).astype(BF)
        lse2 = m_sc[...] * c2 + jnp.log(l) * LOG2E
        lse_ref[0, 0] = jnp.transpose(lse2)[0:8, :]

    qspec = pl.BlockSpec((t, LANES), lambda h, i: (i, h))
    kspec = pl.BlockSpec((s, LANES), lambda h, i: (0, h))
    return pl.pallas_call(
        body, grid=(C_HEADS, nt), in_specs=[qspec, kspec, kspec],
        out_specs=[qspec, pl.BlockSpec((1, 1, 8, t), lambda h, i: (h, i, 0, 0))],
        out_shape=[jax.ShapeDtypeStruct(q.shape, BF), jax.ShapeDtypeStruct((C_HEADS, nt, 8, t), F32)],
        scratch_shapes=[pltpu.VMEM((t, LANES), F32)] * 2,
        compiler_params=_cp("parallel", "parallel"), name=name)(q, k, v)


def _mla_delta(o, do, *, name):
    s = o.shape[0]
    t = _tile(s, ATTN_TILE, LANES)
    nt = s // t

    def body(o_ref, do_ref, d_ref):
        for h in range(C_HEADS):
            prod = jnp.transpose(_slot(o_ref, h).astype(F32) * _slot(do_ref, h).astype(F32))
            d_ref[h, 0] = jnp.broadcast_to(jnp.sum(prod, axis=0, keepdims=True), (8, t))

    blk = pl.BlockSpec((t, C_HEADS * LANES), lambda i: (i, 0))
    return pl.pallas_call(
        body, grid=(nt,), in_specs=[blk, blk],
        out_specs=pl.BlockSpec((C_HEADS, 1, 8, t), lambda i: (0, i, 0, 0)),
        out_shape=jax.ShapeDtypeStruct((C_HEADS, nt, 8, t), F32),
        compiler_params=_cp("parallel"), name=name)(o, do)


def _mla_bwd(q, k, v, do, lse2, delta, *, name):
    s = q.shape[0]
    t = _tile(s, ATTN_TILE, LANES)
    nt = s // t
    scale = C_QK ** -0.5
    c2 = scale * LOG2E

    def body(q_ref, do_ref, k_ref, v_ref, lse_ref, dl_ref, dq_ref, dk_ref, dv_ref):
        kj = pl.program_id(1)

        @pl.when(kj == 0)
        def _():
            dq_ref[...] = jnp.zeros_like(dq_ref)

        dk_ref[...] = jnp.zeros_like(dk_ref)
        dv_ref[...] = jnp.zeros_like(dv_ref)
        kv, vv = k_ref[...], v_ref[...]

        def step(i, n, diag):
            rows = pl.ds(pl.multiple_of(i * t, t), n * t)
            qv, dov = q_ref[rows, :], do_ref[rows, :]
            st = _dot_nt(kv, qv)
            if diag:
                r = lax.broadcasted_iota(jnp.int32, (t, t), 0)
                c = lax.broadcasted_iota(jnp.int32, (t, t), 1)
                own = jnp.where(r <= c, st[:, :t], NEG)
                st = own if n == 1 else jnp.concatenate([own, st[:, t:]], axis=1)
            lse = jnp.concatenate([lse_ref[0, i + u, 0:1, :] for u in range(n)], axis=1)
            dl = jnp.concatenate([dl_ref[0, i + u, 0:1, :] for u in range(n)], axis=1)
            pt = jnp.exp2(st * c2 - lse)
            dv_ref[...] += _dot(pt.astype(BF), dov)
            dpt = _dot_nt(vv, dov)
            dst = (pt * (dpt - dl) * scale).astype(BF)
            dk_ref[...] += _dot(dst, qv)
            dq_ref[rows, :] += _dot_tn(dst, kv)

        odd = (nt - kj) % 2
        first = kj + 2 - odd

        @pl.when(odd == 0)
        def _():
            step(kj, 2, True)

        @pl.when(odd == 1)
        def _():
            step(kj, 1, True)

        def pair_body(pi, carry):
            step(first + 2 * pi, 2, False)
            return carry

        lax.fori_loop(0, (nt - first) // 2, pair_body, 0)

    res = pl.BlockSpec((s, LANES), lambda h, j: (0, h))
    kspec = pl.BlockSpec((t, LANES), lambda h, j: (j, h))
    stat = pl.BlockSpec((1, nt, 8, t), lambda h, j: (h, 0, 0, 0))
    return pl.pallas_call(
        body, grid=(C_HEADS, nt), in_specs=[res, res, kspec, kspec, stat, stat],
        out_specs=[res, kspec, kspec],
        out_shape=[jax.ShapeDtypeStruct(q.shape, F32)] * 3,
        compiler_params=_cp("parallel", "arbitrary"), name=name)(q, do, k, v, lse2, delta)


def _conv_shifted(z, zprev, tm):
    row = lax.broadcasted_iota(jnp.int32, z.shape, 0)
    z1 = jnp.where(row == 0, zprev[7:8], pltpu.roll(z, 1, 0))
    z2 = jnp.where(row == 0, zprev[6:7], jnp.where(row == 1, zprev[7:8], pltpu.roll(z, 2, 0)))
    return z1, z2


def _conv_fwd(proj, w, *, name):
    s = proj.shape[0]
    d = D_MODEL
    tm = _tile(s, PREP_TILE, 8)

    def body(x_ref, xp_ref, w_ref, y_ref):
        i = pl.program_id(0)
        z = x_ref[:, d:2 * d] * x_ref[:, 2 * d:]
        zprev = jnp.where(i > 0, xp_ref[:, d:2 * d] * xp_ref[:, 2 * d:], 0.0)
        z1, z2 = _conv_shifted(z, zprev, tm)
        y = w_ref[0:1, :] * z2 + w_ref[1:2, :] * z1 + w_ref[2:3, :] * z
        y_ref[...] = (x_ref[:, :d] * y).astype(BF)

    per8 = tm // 8
    return pl.pallas_call(
        body, grid=(s // tm,),
        in_specs=[pl.BlockSpec((tm, 3 * d), lambda i: (i, 0)),
                  pl.BlockSpec((8, 3 * d), lambda i: (jnp.maximum(i * per8 - 1, 0), 0)),
                  pl.BlockSpec((8, d), lambda i: (0, 0))],
        out_specs=pl.BlockSpec((tm, d), lambda i: (i, 0)),
        out_shape=jax.ShapeDtypeStruct((s, d), BF),
        compiler_params=_cp("parallel"), name=name)(proj, proj, w)


def _conv_bwd(proj, dyb, w, *, name):
    s = proj.shape[0]
    d = D_MODEL
    tm = _tile(s, PREP_TILE, 8)
    nt = s // tm

    def body(x_ref, xp_ref, xn_ref, dy_ref, dyn_ref, w_ref, dx_ref, dw_ref):
        i = pl.program_id(0)

        @pl.when(i == 0)
        def _():
            dw_ref[...] = jnp.zeros_like(dw_ref)

        b, c, u = x_ref[:, :d], x_ref[:, d:2 * d], x_ref[:, 2 * d:]
        z = c * u
        zprev = jnp.where(i > 0, xp_ref[:, d:2 * d] * xp_ref[:, 2 * d:], 0.0)
        z1, z2 = _conv_shifted(z, zprev, tm)
        w0, w1, w2 = w_ref[0:1, :], w_ref[1:2, :], w_ref[2:3, :]
        y = w0 * z2 + w1 * z1 + w2 * z
        dyb_v = dy_ref[...]
        dyc = dyb_v * b
        dyn = jnp.where(i < nt - 1, dyn_ref[...] * xn_ref[:, :d], 0.0)
        row = lax.broadcasted_iota(jnp.int32, dyc.shape, 0)
        d1 = jnp.where(row == tm - 1, dyn[0:1], pltpu.roll(dyc, tm - 1, 0))
        d2 = jnp.where(row == tm - 1, dyn[1:2], jnp.where(row == tm - 2, dyn[0:1], pltpu.roll(dyc, tm - 2, 0)))
        dz = w2 * dyc + w1 * d1 + w0 * d2
        dx_ref[:, :d] = (dyb_v * y).astype(BF)
        dx_ref[:, d:2 * d] = (dz * u).astype(BF)
        dx_ref[:, 2 * d:] = (dz * c).astype(BF)
        dw_ref[0:1, :] += jnp.sum(dyc * z2, axis=0, keepdims=True)
        dw_ref[1:2, :] += jnp.sum(dyc * z1, axis=0, keepdims=True)
        dw_ref[2:3, :] += jnp.sum(dyc * z, axis=0, keepdims=True)

    per8 = tm // 8
    last8 = s // 8 - 1
    dx, dw = pl.pallas_call(
        body, grid=(nt,),
        in_specs=[pl.BlockSpec((tm, 3 * d), lambda i: (i, 0)),
                  pl.BlockSpec((8, 3 * d), lambda i: (jnp.maximum(i * per8 - 1, 0), 0)),
                  pl.BlockSpec((8, 3 * d), lambda i: (jnp.minimum((i + 1) * per8, last8), 0)),
                  pl.BlockSpec((tm, d), lambda i: (i, 0)),
                  pl.BlockSpec((8, d), lambda i: (jnp.minimum((i + 1) * per8, last8), 0)),
                  pl.BlockSpec((8, d), lambda i: (0, 0))],
        out_specs=[pl.BlockSpec((tm, 3 * d), lambda i: (i, 0)), pl.BlockSpec((8, d), lambda i: (0, 0))],
        out_shape=[jax.ShapeDtypeStruct((s, 3 * d), BF), jax.ShapeDtypeStruct((8, d), F32)],
        compiler_params=_cp("arbitrary"), name=name)(proj, proj, proj, dyb, dyb, w)
    return dx, dw[0:3]


def _ffn_up(h, wgu, *, name):
    s, d = h.shape
    f = wgu.shape[1] // 2
    tm = _tile(s, ROW_TILE, 8)
    tn = _tile(f, 1408, 128)
    nb = f // tn

    def body(h_ref, wg_ref, wu_ref, dg_ref, du_ref, a_ref):
        hv = h_ref[...]
        g = _dot(hv, wg_ref[...])
        u = _dot(hv, wu_ref[...])
        sig = jax.nn.sigmoid(g)
        silu = g * sig
        dg_ref[...] = (u * (sig + silu * (1.0 - sig))).astype(BF)
        du_ref[...] = silu.astype(BF)
        a_ref[...] = (silu * u).astype(BF)

    out = pl.BlockSpec((tm, tn), lambda j, i: (i, j))
    return pl.pallas_call(
        body, grid=(nb, s // tm),
        in_specs=[pl.BlockSpec((tm, d), lambda j, i: (i, 0)),
                  pl.BlockSpec((d, tn), lambda j, i: (0, j)),
                  pl.BlockSpec((d, tn), lambda j, i: (0, j + nb))],
        out_specs=[out] * 3, out_shape=[jax.ShapeDtypeStruct((s, f), BF)] * 3,
        compiler_params=_cp("parallel", "parallel"), name=name)(h, wgu, wgu)


def _ffn_bwd_mid(dy, wd, gate_factor, up_factor, *, name):
    s, d = dy.shape
    f = wd.shape[0]
    tm = _tile(s, ROW_TILE, 8)
    tn = _tile(f, 1408, 128)

    def body(dy_ref, wd_ref, g_ref, u_ref, dg_ref, du_ref):
        dyv = dy_ref[...].astype(BF)
        for c0 in range(0, tn, 2 * LANES):
            cols = slice(c0, min(c0 + 2 * LANES, tn))
            dact = _dot_nt(dyv, wd_ref[cols, :])
            dg_ref[:, cols] = (dact * g_ref[:, cols].astype(F32)).astype(BF)
            du_ref[:, cols] = (dact * u_ref[:, cols].astype(F32)).astype(BF)

    blk = pl.BlockSpec((tm, tn), lambda j, i: (i, j))
    return pl.pallas_call(
        body, grid=(f // tn, s // tm),
        in_specs=[pl.BlockSpec((tm, d), lambda j, i: (i, 0)), pl.BlockSpec((tn, d), lambda j, i: (j, 0)), blk, blk],
        out_specs=[blk, blk], out_shape=[jax.ShapeDtypeStruct((s, f), BF)] * 2,
        compiler_params=_cp("parallel", "parallel"), name=name)(dy, wd, gate_factor, up_factor)


def _pad_cols(w, heads, dim, layout):
    k = w.shape[0]
    w3 = w.reshape(k, heads, dim)
    pieces, lane = [], 0
    for lane0, dim0, cnt in sorted(layout):
        if lane0 > lane:
            pieces.append(jnp.zeros((k, heads, lane0 - lane), w.dtype))
        pieces.append(w3[:, :, dim0:dim0 + cnt])
        lane = lane0 + cnt
    if lane < LANES:
        pieces.append(jnp.zeros((k, heads, LANES - lane), w.dtype))
    return jnp.concatenate(pieces, axis=2).reshape(k, heads * LANES)


def _unpad_cols(w, heads, dim, layout):
    k = w.shape[0]
    w3 = w.reshape(k, heads, LANES)
    by_dim = sorted(layout, key=lambda seg: seg[1])
    assert sum(cnt for _, _, cnt in by_dim) == dim
    return jnp.concatenate([w3[:, :, lane0:lane0 + cnt] for lane0, _, cnt in by_dim], axis=2).reshape(k, heads * dim)


def _pad_rows(w, heads, dim):
    n = w.shape[1]
    return jnp.pad(w.reshape(heads, dim, n), ((0, 0), (0, LANES - dim), (0, 0))).reshape(heads * LANES, n)


def _unpad_rows(w, heads, dim):
    n = w.shape[1]
    return w.reshape(heads, LANES, n)[:, :dim, :].reshape(heads * dim, n)


def _pad_vec(g, layout):
    return _pad_cols(g.reshape(1, -1), 1, g.shape[0], layout)


def _unpad_vec(g, dim, layout):
    return _unpad_cols(g.reshape(1, LANES), 1, dim, layout)[0]


def _local_step(x, positions, w, target):
    s = x.shape[0]
    cos_a, sin_a, cos_c, sin_c = _rope_tables(positions.reshape(s, 1), name="rope_tables")
    grads = {n: [None] * len(w[n]) for n in WEIGHTS}
    saved = []

    for i in range(DEPTH):
        kind, j = i % N_MIXERS, i // N_MIXERS
        tag = f"l{i}"
        if i == 0:
            h = _rmsnorm_fwd(x, w['mix_norm'][i], name=f"{tag}_mix_norm")
        fnorm = w['ffn_norm'][i]
        if kind == 0:
            nqk = (A_HEADS + A_KV_HEADS) * A_HEAD_DIM
            wqkv = jnp.concatenate(
                [_pad_cols(w['a_w_qkv'][j][:, :nqk], A_HEADS + A_KV_HEADS, A_HEAD_DIM, LAYOUT_A),
                 _pad_cols(w['a_w_qkv'][j][:, nqk:], A_KV_HEADS, A_HEAD_DIM, LAYOUT_V)], axis=1)
            wo = _pad_rows(w['a_w_o'][j], A_HEADS, A_HEAD_DIM)
            gq, gk = _pad_vec(w['a_q_norm'][j], LAYOUT_A), _pad_vec(w['a_k_norm'][j], LAYOUT_A)
            qkv = _mm([(h, wqkv, 0)], out_dtype=F32, name=f"{tag}_qkv")
            qa, ka, va = _prep_a_fwd(qkv, cos_a, sin_a, gq, gk, name=f"{tag}_prep")
            o, lse = _swa_fwd(qa, ka, va, w['a_sinks'][j], name=f"{tag}_attn")
            x1, h2 = _mm([(o, wo, 0)], out_dtype=F32, res=x, norm_g=fnorm, name=f"{tag}_wo")
            mix = dict(wqkv=wqkv, wo=wo, gq=gq, gk=gk, qkv=qkv, qa=qa, ka=ka, va=va, o=o, lse=lse)
        elif kind == 1:
            cw = jnp.pad(w['b_conv_w'][j], ((0, 5), (0, 0)))
            proj = _mm([(h, w['b_w_in'][j], 0)], out_dtype=F32, name=f"{tag}_win")
            yb = _conv_fwd(proj, cw, name=f"{tag}_conv")
            x1, h2 = _mm([(yb, w['b_w_out'][j], 0)], out_dtype=F32, res=x, norm_g=fnorm, name=f"{tag}_wout")
            mix = dict(cw=cw, proj=proj, yb=yb)
        else:
            wdn = w['c_w_down'][j]
            nqk = C_Q_RANK + C_KV_RANK
            wdn = jnp.concatenate([wdn[:, :nqk], _pad_cols(wdn[:, nqk:], 1, C_ROPE, LAYOUT_KR)], axis=1)
            wq = _pad_cols(w['c_w_q_up'][j], C_HEADS, C_QK, LAYOUT_C)
            wkv = w['c_w_kv_up'][j].reshape(C_KV_RANK, C_HEADS, C_NOPE + C_V)
            wkn = _pad_cols(wkv[:, :, :C_NOPE].reshape(C_KV_RANK, -1), C_HEADS, C_NOPE, LAYOUT_KN)
            wv = _pad_cols(wkv[:, :, C_NOPE:].reshape(C_KV_RANK, -1), C_HEADS, C_V, LAYOUT_V)
            wo = _pad_rows(w['c_w_o'][j], C_HEADS, C_V)
            gq, gk = _pad_vec(w['c_q_norm'][j], LAYOUT_C), _pad_vec(w['c_k_norm'][j], LAYOUT_C)
            gqa, gkva = w['c_q_a_norm'][j].reshape(1, -1), w['c_kv_a_norm'][j].reshape(1, -1)
            dn = _mm([(h, wdn, 0)], out_dtype=F32, name=f"{tag}_wdown")
            cqn, ckvn = _prep_c1_fwd(dn, gqa, gkva, name=f"{tag}_prep1")
            qraw = _mm([(cqn, wq, 0)], out_dtype=F32, name=f"{tag}_wq")
            knope = _mm([(ckvn, wkn, 0)], out_dtype=F32, name=f"{tag}_wkn")
            vc = _mm([(ckvn, wv, 0)], out_dtype=BF, name=f"{tag}_wv")
            qc, kc = _prep_c2_fwd(qraw, knope, dn, cos_c, sin_c, gq, gk, name=f"{tag}_prep2")
            o, lse = _mla_fwd(qc, kc, vc, name=f"{tag}_attn")
            x1, h2 = _mm([(o, wo, 0)], out_dtype=F32, res=x, norm_g=fnorm, name=f"{tag}_wo")
            mix = dict(wdn=wdn, wq=wq, wkn=wkn, wv=wv, wo=wo, gq=gq, gk=gk, gqa=gqa, gkva=gkva, dn=dn, cqn=cqn,
                       ckvn=ckvn, qraw=qraw, knope=knope, vc=vc, qc=qc, kc=kc, o=o, lse=lse)
        gate_factor, up_factor, act = _ffn_up(h2, w['f_w_gate_up'][i], name=f"{tag}_ffn_up")
        saved.append(dict(x=x, h=h, x1=x1, h2=h2, gate_factor=gate_factor, up_factor=up_factor, act=act, mix=mix))
        if i + 1 < DEPTH:
            x, h = _mm([(act, w['f_w_down'][i], 0)], out_dtype=F32, res=x1, norm_g=w['mix_norm'][i + 1],
                       name=f"{tag}_ffn_down")
        else:
            x = _mm([(act, w['f_w_down'][i], 0)], out_dtype=F32, res=x1, name=f"{tag}_ffn_down")

    loss_blk, dx = _loss_fwd_bwd(x, target, name="loss")

    for i in reversed(range(DEPTH)):
        kind, j = i % N_MIXERS, i // N_MIXERS
        tag = f"l{i}b"
        sv = saved[i]
        mix = sv['mix']
        wgu, wd = w['f_w_gate_up'][i], w['f_w_down'][i]
        grads['f_w_down'][i] = _mm_tn(sv['act'], dx, name=f"{tag}_dwd")
        dgate, dup = _ffn_bwd_mid(dx, wd, sv['gate_factor'], sv['up_factor'], name=f"{tag}_ffn_mid")
        grads['f_w_gate_up'][i] = (_mm_tn(sv['h2'], dgate, name=f"{tag}_dwg"),
                                   _mm_tn(sv['h2'], dup, name=f"{tag}_dwu"))
        dx, dg = _mm_norm_bwd([(dgate, wgu, 0), (dup, wgu, 1)], sv['x1'], w['ffn_norm'][i], dx, name=f"{tag}_dh2")
        grads['ffn_norm'][i] = dg[0]
        if kind == 0:
            grads['a_w_o'][j] = _unpad_rows(_mm_tn(mix['o'], dx, name=f"{tag}_dwo"), A_HEADS, A_HEAD_DIM)
            do = _mm([(dx, mix['wo'], 0)], out_dtype=BF, trans_b=True, name=f"{tag}_do")
            dqa, dka, dva, dsinks = _swa_bwd(mix['qa'], mix['ka'], mix['va'], mix['o'], do, mix['lse'],
                                             w['a_sinks'][j], name=f"{tag}_attn")
            dqkv, dgq, dgk = _prep_a_bwd(mix['qkv'], dqa, dka, dva, cos_a, sin_a, mix['gq'], mix['gk'],
                                         name=f"{tag}_prep")
            grads['a_sinks'][j] = dsinks
            grads['a_q_norm'][j] = _unpad_vec(dgq, A_HEAD_DIM, LAYOUT_A)
            grads['a_k_norm'][j] = _unpad_vec(dgk, A_HEAD_DIM, LAYOUT_A)
            dwqkv = _mm_tn(sv['h'], dqkv, name=f"{tag}_dwqkv")
            nqk = (A_HEADS + A_KV_HEADS) * LANES
            grads['a_w_qkv'][j] = jnp.concatenate(
                [_unpad_cols(dwqkv[:, :nqk], A_HEADS + A_KV_HEADS, A_HEAD_DIM, LAYOUT_A),
                 _unpad_cols(dwqkv[:, nqk:], A_KV_HEADS, A_HEAD_DIM, LAYOUT_V)], axis=1)
            dh_pairs = [(dqkv, mix['wqkv'], 0)]
        elif kind == 1:
            grads['b_w_out'][j] = _mm_tn(mix['yb'], dx, name=f"{tag}_dwout")
            dyb = _mm([(dx, w['b_w_out'][j], 0)], out_dtype=F32, trans_b=True, name=f"{tag}_dyb")
            dproj, dcw = _conv_bwd(mix['proj'], dyb, mix['cw'], name=f"{tag}_conv")
            grads['b_conv_w'][j] = dcw
            grads['b_w_in'][j] = _mm_tn(sv['h'], dproj, name=f"{tag}_dwin")
            dh_pairs = [(dproj, w['b_w_in'][j], 0)]
        else:
            grads['c_w_o'][j] = _unpad_rows(_mm_tn(mix['o'], dx, name=f"{tag}_dwo"), C_HEADS, C_V)
            do = _mm([(dx, mix['wo'], 0)], out_dtype=BF, trans_b=True, name=f"{tag}_do")
            delta = _mla_delta(mix['o'], do, name=f"{tag}_delta")
            dqc, dkc, dvc = _mla_bwd(mix['qc'], mix['kc'], mix['vc'], do, mix['lse'], delta, name=f"{tag}_attn")
            dqraw, dknope, dkr, dgq, dgk = _prep_c2_bwd(mix['qraw'], mix['knope'], mix['dn'], dqc, dkc, cos_c, sin_c,
                                                        mix['gq'], mix['gk'], name=f"{tag}_prep2")
            grads['c_q_norm'][j] = _unpad_vec(dgq, C_QK, LAYOUT_C)
            grads['c_k_norm'][j] = _unpad_vec(dgk, C_QK, LAYOUT_C)
            grads['c_w_q_up'][j] = _unpad_cols(_mm_tn(mix['cqn'], dqraw, name=f"{tag}_dwq"), C_HEADS, C_QK,
                                               LAYOUT_C)
            dwkn = _unpad_cols(_mm_tn(mix['ckvn'], dknope, name=f"{tag}_dwkn"), C_HEADS, C_NOPE, LAYOUT_KN)
            dwv = _unpad_cols(_mm_tn(mix['ckvn'], dvc, name=f"{tag}_dwv"), C_HEADS, C_V, LAYOUT_V)
            grads['c_w_kv_up'][j] = jnp.concatenate(
                [dwkn.reshape(C_KV_RANK, C_HEADS, C_NOPE), dwv.reshape(C_KV_RANK, C_HEADS, C_V)], axis=2).reshape(
                C_KV_RANK, -1)
            dcq = _mm([(dqraw, mix['wq'], 0)], out_dtype=F32, trans_b=True, name=f"{tag}_dcq")
            dckv = _mm([(dknope, mix['wkn'], 0), (dvc, mix['wv'], 0)], out_dtype=F32, trans_b=True,
                       name=f"{tag}_dckv")
            ddn, dgqa, dgkva = _prep_c1_bwd(mix['dn'], dcq, dckv, dkr, mix['gqa'], mix['gkva'], name=f"{tag}_prep1")
            grads['c_q_a_norm'][j] = dgqa[0]
            grads['c_kv_a_norm'][j] = dgkva[0]
            dwdn = _mm_tn(sv['h'], ddn, name=f"{tag}_dwdown")
            nqk = C_Q_RANK + C_KV_RANK
            grads['c_w_down'][j] = jnp.concatenate(
                [dwdn[:, :nqk], _unpad_cols(dwdn[:, nqk:], 1, C_ROPE, LAYOUT_KR)], axis=1)
            dh_pairs = [(ddn, mix['wdn'], 0)]
        dx, dg = _mm_norm_bwd(dh_pairs, sv['x'], w['mix_norm'][i], dx, name=f"{tag}_dh")
        grads['mix_norm'][i] = dg[0]

    return loss_blk, dx, {n: (g if n in BIG else jnp.stack(g)) for n, g in grads.items()}


def _my_place():
    return lax.axis_index("x"), lax.axis_index("y"), lax.axis_index("c")


def _flips(x, y):
    return [(1 - x, y), (x, 1 - y), (1 - x, 1 - y)]


def _gather_weights(big, small):
    half = big.shape[0] // 2
    chunk = half // GATHER_CHUNKS

    def body(big_ref, small_ref, bout_ref, sout_ref, send_b, recv_b, send_f, recv_f, send_s, recv_s, loc):
        x, y, c = _my_place()
        me = 2 * x + y

        def rows(core, ch):
            return pl.ds(pl.multiple_of(core * half + ch * chunk, 16), chunk)

        def ici(k, ch, chip, to):
            return pltpu.make_async_remote_copy(
                src_ref=big_ref.at[rows(c, ch), :], dst_ref=bout_ref.at[chip, rows(c, ch), :],
                send_sem=send_b.at[k * GATHER_CHUNKS + ch], recv_sem=recv_b.at[k * GATHER_CHUNKS + ch],
                device_id=to, device_id_type=MESH)

        def forward(k, ch, chip, core):
            return pltpu.make_async_remote_copy(
                src_ref=bout_ref.at[chip, rows(core, ch), :], dst_ref=bout_ref.at[chip, rows(core, ch), :],
                send_sem=send_f.at[k * GATHER_CHUNKS + ch], recv_sem=recv_f.at[k * GATHER_CHUNKS + ch],
                device_id=(x, y, 1 - c), device_id_type=MESH)

        def small_copy(k, chip, to):
            return pltpu.make_async_remote_copy(src_ref=small_ref, dst_ref=sout_ref.at[chip],
                                                send_sem=send_s.at[k], recv_sem=recv_s.at[k],
                                                device_id=to, device_id_type=MESH)

        chips = _flips(x, y)
        every = [(ch, k, px, py) for ch in range(GATHER_CHUNKS) for k, (px, py) in enumerate(chips)]
        sends = [ici(k, ch, me, (px, py, c)) for ch, k, px, py in every]
        sends += [small_copy(k, me, (px, py, c)) for k, (px, py) in enumerate(chips)]
        for cp in sends:
            cp.start()
        own = (pltpu.make_async_copy(big_ref, bout_ref.at[me], loc.at[0]),
               pltpu.make_async_copy(small_ref, sout_ref.at[me], loc.at[1]))
        for cp in own:
            cp.start()
        passed = []
        for ch, k, px, py in every:
            ici(k, ch, 2 * px + py, (x, y, c)).wait_recv()
            passed.append(forward(k, ch, 2 * px + py, c))
            passed[-1].start()
        for ch, k, px, py in every:
            forward(k, ch, 2 * px + py, 1 - c).wait_recv()
        for k, (px, py) in enumerate(chips):
            small_copy(k, 2 * px + py, (x, y, c)).wait_recv()
        for cp in sends + passed:
            cp.wait_send()
        for cp in own:
            cp.wait()

    hbm = pl.BlockSpec(memory_space=pltpu.HBM)
    sem3 = pltpu.SemaphoreType.DMA((3,))
    semc = pltpu.SemaphoreType.DMA((3 * GATHER_CHUNKS,))
    return pl.pallas_call(
        body, in_specs=[hbm, hbm], out_specs=[hbm, hbm],
        out_shape=[jax.ShapeDtypeStruct((N_CHIPS,) + big.shape, big.dtype),
                   jax.ShapeDtypeStruct((N_CHIPS,) + small.shape, small.dtype)],
        scratch_shapes=[semc, semc, semc, semc, sem3, sem3, pltpu.SemaphoreType.DMA((2,))],
        name="gather_weights")(big, small)


def _exchange_grads(gbig, gsmall):
    def body(gbig_ref, gsmall_ref, got_ref, ssum_ref, sbuf, send_b, recv_b, send_s, recv_s):
        x, y, c = _my_place()
        me = 4 * x + 2 * y + c

        def big_copy(k, src_chip, to):
            return pltpu.make_async_remote_copy(src_ref=gbig_ref.at[src_chip], dst_ref=got_ref.at[k],
                                                send_sem=send_b.at[k], recv_sem=recv_b.at[k],
                                                device_id=to, device_id_type=MESH)

        def small_copy(k, slot, to):
            return pltpu.make_async_remote_copy(src_ref=gsmall_ref, dst_ref=sbuf.at[slot],
                                                send_sem=send_s.at[k], recv_sem=recv_s.at[k],
                                                device_id=to, device_id_type=MESH)

        peers = [(x ^ (k >> 2), y ^ ((k >> 1) & 1), c ^ (k & 1)) for k in range(1, N_DEV)]
        bigs = [big_copy(k, 2 * px + py, (px, py, c)) for k, (px, py) in enumerate(_flips(x, y))]
        smalls = [small_copy(k, me, p) for k, p in enumerate(peers)]
        for cp in bigs + smalls:
            cp.start()
        sbuf[me] = gsmall_ref[...]
        for k, (px, py, pc) in enumerate(peers):
            small_copy(k, 4 * px + 2 * py + pc, (x, y, c)).wait_recv()
        acc = sbuf[0]
        for d in range(1, N_DEV):
            acc = acc + sbuf[d]
        ssum_ref[...] = acc
        for k in range(3):
            big_copy(k, 0, (x, y, c)).wait_recv()
        for cp in bigs + smalls:
            cp.wait_send()

    hbm = pl.BlockSpec(memory_space=pltpu.HBM)
    vmem = pl.BlockSpec(memory_space=pltpu.VMEM)
    return pl.pallas_call(
        body, in_specs=[hbm, vmem], out_specs=[hbm, vmem],
        out_shape=[jax.ShapeDtypeStruct((3,) + gbig.shape[1:], gbig.dtype),
                   jax.ShapeDtypeStruct(gsmall.shape, gsmall.dtype)],
        scratch_shapes=[pltpu.VMEM((N_DEV,) + gsmall.shape, gsmall.dtype),
                        pltpu.SemaphoreType.DMA((3,)), pltpu.SemaphoreType.DMA((3,)),
                        pltpu.SemaphoreType.DMA((N_DEV - 1,)), pltpu.SemaphoreType.DMA((N_DEV - 1,))],
        name="exchange_grads")(gbig, gsmall)


def _sum_shards(chip, mine, got):
    r, cols = mine.shape[1:]
    tr = _tile(r, PACK_TILE_ROWS, 16)

    def body(chip_ref, mine_ref, got_ref, o_ref):
        o_ref[...] = ((mine_ref[0] + got_ref[0].astype(F32)) + got_ref[1].astype(F32)) + got_ref[2].astype(F32)

    grid_spec = pltpu.PrefetchScalarGridSpec(
        num_scalar_prefetch=1, grid=(r // tr,),
        in_specs=[pl.BlockSpec((1, tr, cols), lambda i, chip_ref: (chip_ref[0], i, 0)),
                  pl.BlockSpec((3, tr, cols), lambda i, chip_ref: (0, i, 0))],
        out_specs=pl.BlockSpec((tr, cols), lambda i, chip_ref: (i, 0)))
    return pl.pallas_call(
        body, grid_spec=grid_spec, out_shape=jax.ShapeDtypeStruct((r, cols), F32),
        compiler_params=_cp("parallel"), name="sum_shards")(chip, mine, got)


def _swap_sibling(t, *, name):
    def body(t_ref, got_ref, send_sem, recv_sem):
        x, y, c = _my_place()
        cp = pltpu.make_async_remote_copy(src_ref=t_ref, dst_ref=got_ref, send_sem=send_sem, recv_sem=recv_sem,
                                          device_id=(x, y, 1 - c), device_id_type=MESH)
        cp.start()
        cp.wait()

    hbm = pl.BlockSpec(memory_space=pltpu.HBM)
    return pl.pallas_call(
        body, in_specs=[hbm], out_specs=hbm, out_shape=jax.ShapeDtypeStruct(t.shape, t.dtype),
        scratch_shapes=[pltpu.SemaphoreType.DMA, pltpu.SemaphoreType.DMA], name=name)(t)


def _give_halves(parts, *, name):
    n = len(parts)
    r, cols = parts[0].shape
    half = r // 2

    def body(*refs):
        got_ref, send_sems, recv_sems = refs[n:]
        x, y, c = _my_place()
        theirs = pl.ds(pl.multiple_of((1 - c) * half, 16), half)
        copies = [pltpu.make_async_remote_copy(src_ref=refs[k].at[theirs, :], dst_ref=got_ref.at[k],
                                               send_sem=send_sems.at[k], recv_sem=recv_sems.at[k],
                                               device_id=(x, y, 1 - c), device_id_type=MESH) for k in range(n)]
        for cp in copies:
            cp.start()
        for cp in copies:
            cp.wait()

    hbm = pl.BlockSpec(memory_space=pltpu.HBM)
    return pl.pallas_call(
        body, in_specs=[hbm] * n, out_specs=hbm, out_shape=jax.ShapeDtypeStruct((n, half, cols), parts[0].dtype),
        scratch_shapes=[pltpu.SemaphoreType.DMA((n,)), pltpu.SemaphoreType.DMA((n,))], name=name)(*parts)


def _add_kept(core, parts, got, *, name):
    n, r, cols = got.shape
    tr = _tile(r, PACK_TILE_ROWS, 16)
    nblk = r // tr

    def body(core_ref, *refs):
        b_ref, o_ref, ob_ref = refs[n:]
        for k in range(n):
            sm = refs[k][...] + b_ref[k]
            o_ref[k] = sm
            ob_ref[k] = sm.astype(BF)

    blk = pl.BlockSpec((n, tr, cols), lambda i, core_ref: (0, i, 0))
    kept = pl.BlockSpec((tr, cols), lambda i, core_ref: (core_ref[0] * nblk + i, 0))
    grid_spec = pltpu.PrefetchScalarGridSpec(
        num_scalar_prefetch=1, grid=(nblk,), in_specs=[kept] * n + [blk], out_specs=[blk, blk])
    return pl.pallas_call(
        body, grid_spec=grid_spec,
        out_shape=[jax.ShapeDtypeStruct(got.shape, F32), jax.ShapeDtypeStruct(got.shape, BF)],
        compiler_params=_cp("parallel"), name=name)(core, *parts, got)


def _by_core(c, mine, sibling):
    return jnp.where(c == 0, mine, sibling), jnp.where(c == 0, sibling, mine)


def _adamw(wt, m, v, g, *, name):
    r, cols = wt.shape
    tr = _tile(r, 256, 8)

    def body(w_ref, m_ref, v_ref, g_ref, d_ref, nm_ref, nv_ref):
        gv = g_ref[...]
        nm = ADAM_B1 * m_ref[...] + (1.0 - ADAM_B1) * gv
        nv = ADAM_B2 * v_ref[...] + (1.0 - ADAM_B2) * (gv * gv)
        m_hat = nm / (1.0 - ADAM_B1 ** ADAM_STEP)
        v_hat = nv / (1.0 - ADAM_B2 ** ADAM_STEP)
        d_ref[...] = -ADAM_LR * (m_hat / (jnp.sqrt(v_hat) + ADAM_EPS) + ADAM_WD * w_ref[...])
        nm_ref[...] = nm
        nv_ref[...] = nv

    blk = pl.BlockSpec((tr, cols), lambda i: (i, 0))
    return pl.pallas_call(
        body, grid=(r // tr,), in_specs=[blk] * 4, out_specs=[blk] * 3,
        out_shape=[jax.ShapeDtypeStruct((r, cols), F32)] * 3,
        compiler_params=_cp("parallel"), name=name)(wt, m, v, g)


def _shard_shape(full, axis):
    return tuple(d // N_CHIPS if a == axis else d for a, d in enumerate(full))


def _pack_rows(n):
    rows = -(-n // PACK_COLS)
    return -(-rows // PACK_ROW_MULT) * PACK_ROW_MULT


def _pack(parts, rows, dtype):
    mat = jnp.concatenate([p.astype(dtype).reshape(-1, PACK_COLS) for p in parts], axis=0)
    return jnp.pad(mat, ((0, rows - mat.shape[0]), (0, 0)))


def _join_shards(stacked, axis):
    moved = jnp.moveaxis(stacked, 0, axis)
    shp = moved.shape
    return moved.reshape(shp[:axis] + (shp[axis] * shp[axis + 1],) + shp[axis + 2:])


def _split_shards(full, axis):
    shp = full.shape
    split = full.reshape(shp[:axis] + (N_CHIPS, shp[axis] // N_CHIPS) + shp[axis + 1:])
    return jnp.moveaxis(split, axis, 0)


SMALL_ROWS = 128
SMALL_LAYOUT = [('mix_norm', DEPTH * D_MODEL), ('ffn_norm', DEPTH * D_MODEL), ('b_conv_w', 3 * D_MODEL),
                ('c_q_a_norm', C_Q_RANK), ('c_kv_a_norm', C_KV_RANK), ('a_q_norm', 2 * A_HEAD_DIM),
                ('a_k_norm', 2 * A_HEAD_DIM), ('a_sinks', 2 * A_HEADS), ('c_q_norm', C_QK), ('c_k_norm', C_QK),
                ('loss', 1)]


def _small_offsets():
    offs, row = {}, 0
    for name, n in SMALL_LAYOUT:
        offs[name] = (row * LANES, n)
        row += -(-n // LANES)
    assert row <= SMALL_ROWS
    return offs


def kernel(x, positions, mix_norm, ffn_norm, a_w_qkv, a_q_norm, a_k_norm, a_sinks, a_w_o, b_w_in, b_conv_w, b_w_out, c_w_down, c_q_a_norm, c_kv_a_norm, c_w_q_up, c_w_kv_up, c_q_norm, c_k_norm, c_w_o, f_w_gate_up, f_w_down, loss_target, m_mix_norm, m_ffn_norm, m_a_w_qkv, m_a_q_norm, m_a_k_norm, m_a_sinks, m_a_w_o, m_b_w_in, m_b_conv_w, m_b_w_out, m_c_w_down, m_c_q_a_norm, m_c_kv_a_norm, m_c_w_q_up, m_c_w_kv_up, m_c_q_norm, m_c_k_norm, m_c_w_o, m_f_w_gate_up, m_f_w_down, v_mix_norm, v_ffn_norm, v_a_w_qkv, v_a_q_norm, v_a_k_norm, v_a_sinks, v_a_w_o, v_b_w_in, v_b_conv_w, v_b_w_out, v_c_w_down, v_c_q_a_norm, v_c_kv_a_norm, v_c_w_q_up, v_c_w_kv_up, v_c_q_norm, v_c_k_norm, v_c_w_o, v_f_w_gate_up, v_f_w_down):
    args = dict(locals())
    wshard = {n: args[n] for n in WEIGHTS}
    sharded = {**BIG, **SMALL_SHARDED}
    chip = 2 * lax.axis_index("x") + lax.axis_index("y")

    n_big = sum(wshard[n].size for n in BIG)
    rows = _pack_rows(n_big)
    big = _pack([wshard[n] for n in BIG], rows, BF)
    small = jnp.concatenate([wshard[n].reshape(-1) for n in SMALL_SHARDED])
    small = jnp.pad(small, (0, 8 * LANES - small.shape[0])).reshape(8, LANES)
    core = lax.axis_index("c")
    big_all, small_all = _gather_weights(big, small)
    big_all = big_all.reshape(N_CHIPS, -1)
    small_all = small_all.reshape(N_CHIPS, -1)
    w = {}
    off = 0
    for n, axis in BIG.items():
        shp = wshard[n].shape
        per = shp[1] * shp[2]
        w[n] = [_join_shards(big_all[:, off + l * per:off + (l + 1) * per].reshape((N_CHIPS,) + shp[1:]), axis - 1)
                for l in range(shp[0])]
        off += wshard[n].size
    off = 0
    for n, axis in SMALL_SHARDED.items():
        sz = wshard[n].size
        w[n] = _join_shards(small_all[:, off:off + sz].reshape((N_CHIPS,) + wshard[n].shape), axis)
        off += sz
    for n in WEIGHTS:
        if n not in sharded:
            w[n] = wshard[n]

    loss_blk, grad_x, g = _local_step(x[0], positions[0], w, loss_target[0])

    def shard_of(n, item, k):
        if isinstance(item, tuple):
            item = item[k // 2]
            k = k % 2
            width = item.shape[1] // 2
        else:
            width = item.shape[BIG[n] - 1] // N_CHIPS
        return item[:, k * width:(k + 1) * width] if BIG[n] == 2 else item[k * width:(k + 1) * width]

    gparts = [_pack([shard_of(n, item, k) for n in BIG for item in g[n]], rows, F32) for k in range(N_CHIPS)]
    offs = _small_offsets()
    smalls = {**{n: g[n] for n, _ in SMALL_LAYOUT if n != 'loss'}, 'loss': loss_blk[0:1, 0:1]}
    gsmall = jnp.concatenate(
        [jnp.pad(smalls[n].reshape(-1), (0, -cnt % LANES)) for n, cnt in SMALL_LAYOUT])
    gsmall = jnp.pad(gsmall, (0, SMALL_ROWS * LANES - gsmall.shape[0])).reshape(SMALL_ROWS, LANES)
    from_sibling = _give_halves(gparts, name="swap_grads")
    chip_sum, chip_sum_bf = _add_kept(core.reshape(1).astype(jnp.int32), gparts, from_sibling, name="add_sibling")
    got, ssum = _exchange_grads(chip_sum_bf, gsmall)
    part = _sum_shards(chip.reshape(1).astype(jnp.int32), chip_sum, got)
    part_sib = _swap_sibling(part, name="swap_sums")
    gflat = jnp.concatenate(_by_core(core, part, part_sib), axis=0).reshape(-1)

    ssum = ssum.reshape(-1)
    outs = {}
    off = 0
    for n in WEIGHTS:
        wt = wshard[n]
        if n in BIG:
            grad = gflat[off:off + wt.size]
            off += wt.size
        else:
            o0, cnt = offs[n]
            grad = ssum[o0:o0 + cnt].reshape(g[n].shape)
            if n in SMALL_SHARDED:
                grad = lax.dynamic_index_in_dim(_split_shards(grad, SMALL_SHARDED[n]), chip, 0, keepdims=False)
        shape2 = (-1, wt.shape[-1])
        grad = grad.reshape(shape2)
        res = _adamw(wt.reshape(shape2), args['m_' + n].reshape(shape2), args['v_' + n].reshape(shape2), grad,
                     name=f"adamw_{n}")
        outs[n] = [r.reshape(wt.shape) for r in [grad] + list(res)]
    loss = ssum[offs['loss'][0]]
    return (loss, grad_x[None], *[outs[n][0] for n in WEIGHTS], *[outs[n][1] for n in WEIGHTS],
            *[outs[n][2] for n in WEIGHTS], *[outs[n][3] for n in WEIGHTS])
```

```python
import functools

import jax
import jax.numpy as jnp
from jax import lax
from jax.experimental import pallas as pl
from jax.experimental.pallas import tpu as pltpu

F32 = jnp.float32
BF = jnp.bfloat16

D_MODEL = 1024
DEPTH = 4
N_MIXERS = 3
ROPE_THETA = 500000.0
EPS = 1e-6
BLOCK = 128
LANES = 128

A_HEADS, A_KV_HEADS, A_HEAD_DIM = 16, 4, 64
A_ROT_DIM = A_HEAD_DIM // 4
A_GROUP = A_HEADS // A_KV_HEADS
C_HEADS, C_NOPE, C_ROPE, C_V, C_Q_RANK, C_KV_RANK = 16, 64, 32, 64, 384, 256
C_QK = C_NOPE + C_ROPE
D_FF = 2816

ADAM_LR, ADAM_B1, ADAM_B2, ADAM_EPS, ADAM_WD, ADAM_STEP = 0.001, 0.9, 0.999, 1e-08, 0.01, 10

N_CHIPS = 4
N_DEV = 8
MESH = pl.DeviceIdType.MESH

VMEM_LIMIT = 56 * 1024 * 1024
ROW_TILE = 512
PREP_TILE = 256
ATTN_TILE = 512
FWD_WIDE = 4
BWD_WIDE = 4
SWA_BLOCKS = 16
NEG = -1e30

WEIGHTS = ['mix_norm', 'ffn_norm', 'a_w_qkv', 'a_q_norm', 'a_k_norm', 'a_sinks', 'a_w_o', 'b_w_in', 'b_conv_w',
           'b_w_out', 'c_w_down', 'c_q_a_norm', 'c_kv_a_norm', 'c_w_q_up', 'c_w_kv_up', 'c_q_norm', 'c_k_norm',
           'c_w_o', 'f_w_gate_up', 'f_w_down']
BIG = {'a_w_qkv': 2, 'a_w_o': 1, 'b_w_in': 2, 'b_w_out': 1, 'c_w_down': 1, 'c_w_q_up': 2, 'c_w_kv_up': 2,
       'c_w_o': 1, 'f_w_gate_up': 2, 'f_w_down': 1}
SMALL_SHARDED = {'b_conv_w': 2, 'c_q_a_norm': 1, 'c_kv_a_norm': 1}
PACK_COLS = 256
PACK_ROW_MULT = 32
PACK_TILE_ROWS = 1024
GATHER_CHUNKS = 3


def _cp(*sem):
    return pltpu.CompilerParams(dimension_semantics=sem, vmem_limit_bytes=VMEM_LIMIT)


def _tile(n, target, mult):
    if n <= target:
        return n
    t = (target // mult) * mult
    while t >= mult:
        if n % t == 0:
            return t
        t -= mult
    raise ValueError(f"no tile for {n}")


def _dot(a, b):
    return lax.dot_general(a, b, (((1,), (0,)), ((), ())), preferred_element_type=F32)


def _dot_nt(a, b):
    return lax.dot_general(a, b, (((1,), (1,)), ((), ())), preferred_element_type=F32)


def _dot_tn(a, b):
    return lax.dot_general(a, b, (((0,), (0,)), ((), ())), preferred_element_type=F32)


def _mm(pairs, *, out_dtype, name, res=None, trans_b=False, norm_g=None):
    a0, b0, _ = pairs[0]
    m = a0.shape[0]
    n = b0.shape[0] if trans_b else b0.shape[1]
    tm = _tile(m, 2 * ROW_TILE if sum(a.shape[1] for a, _, _ in pairs) <= 1024 else ROW_TILE, 8)
    tn = _tile(n, 1024, 128)
    n_pairs = len(pairs)
    has_res = res is not None
    has_norm = norm_g is not None
    assert not has_norm or tn == n

    def body(*refs):
        acc = None
        for p in range(n_pairs):
            a = refs[2 * p][...].astype(BF)
            b = refs[2 * p + 1][...].astype(BF)
            d = _dot_nt(a, b) if trans_b else _dot(a, b)
            acc = d if acc is None else acc + d
        nxt = 2 * n_pairs
        if has_res:
            acc = acc + refs[nxt][...]
            nxt += 1
        if has_norm:
            y = acc * lax.rsqrt(jnp.mean(acc * acc, axis=-1, keepdims=True) + EPS)
            refs[-1][...] = (y * refs[nxt][...]).astype(BF)
            refs[-2][...] = acc.astype(out_dtype)
        else:
            refs[-1][...] = acc.astype(out_dtype)

    in_specs, args = [], []
    for a, b, kblk in pairs:
        k = a.shape[1]
        in_specs.append(pl.BlockSpec((tm, k), lambda j, i: (i, 0)))
        if trans_b:
            in_specs.append(pl.BlockSpec((tn, k), functools.partial(lambda j, i, kb: (j, kb), kb=kblk)))
        else:
            in_specs.append(pl.BlockSpec((k, tn), lambda j, i: (0, j)))
        args += [a, b]
    if has_res:
        in_specs.append(pl.BlockSpec((tm, tn), lambda j, i: (i, j)))
        args.append(res)
    out_spec = pl.BlockSpec((tm, tn), lambda j, i: (i, j))
    out_specs, out_shape = out_spec, jax.ShapeDtypeStruct((m, n), out_dtype)
    if has_norm:
        in_specs.append(pl.BlockSpec((1, n), lambda j, i: (0, 0)))
        args.append(norm_g.reshape(1, n))
        out_specs, out_shape = [out_spec, out_spec], [out_shape, jax.ShapeDtypeStruct((m, n), BF)]
    return pl.pallas_call(
        body, grid=(n // tn, m // tm), in_specs=in_specs, out_specs=out_specs, out_shape=out_shape,
        compiler_params=_cp("parallel", "parallel"), name=name)(*args)


def _mm_norm_bwd(pairs, x, g, dres, *, name):
    m, d = x.shape
    tm = _tile(m, ROW_TILE, 8)
    n_pairs = len(pairs)

    def body(*refs):
        x_ref, g_ref, dres_ref, dx_ref, dg_ref = refs[2 * n_pairs:]

        @pl.when(pl.program_id(0) == 0)
        def _():
            dg_ref[...] = jnp.zeros_like(dg_ref)

        dh = None
        for p in range(n_pairs):
            dpart = _dot_nt(refs[2 * p][...].astype(BF), refs[2 * p + 1][...].astype(BF))
            dh = dpart if dh is None else dh + dpart
        dx, xhat = _rms_bwd_math(x_ref[...], g_ref[...], dh, d)
        dx_ref[...] = dres_ref[...] + dx
        dg_ref[0:1, :] += jnp.sum(dh * xhat, axis=0, keepdims=True)

    in_specs, args = [], []
    for a, b, kblk in pairs:
        k = a.shape[1]
        in_specs.append(pl.BlockSpec((tm, k), lambda i: (i, 0)))
        in_specs.append(pl.BlockSpec((d, k), functools.partial(lambda i, kb: (0, kb), kb=kblk)))
        args += [a, b]
    row = pl.BlockSpec((tm, d), lambda i: (i, 0))
    in_specs += [row, pl.BlockSpec((1, d), lambda i: (0, 0)), row]
    dx, dg = pl.pallas_call(
        body, grid=(m // tm,), in_specs=in_specs,
        out_specs=[row, pl.BlockSpec((8, d), lambda i: (0, 0))],
        out_shape=[jax.ShapeDtypeStruct((m, d), F32), jax.ShapeDtypeStruct((8, d), F32)],
        compiler_params=_cp("arbitrary"), name=name)(*args, x, g.reshape(1, d), dres)
    return dx, dg[0:1]


def _mm_tn(a, b, *, name):
    m, k = a.shape
    n = b.shape[1]
    tm = _tile(m, 2 * ROW_TILE, 8)
    tk = _tile(k, 1408, 128)
    tn = _tile(n, 1408, 128)

    def body(a_ref, b_ref, o_ref):
        @pl.when(pl.program_id(2) == 0)
        def _():
            o_ref[...] = jnp.zeros_like(o_ref)

        o_ref[...] += _dot_tn(a_ref[...].astype(BF), b_ref[...].astype(BF))

    return pl.pallas_call(
        body, grid=(k // tk, n // tn, m // tm),
        in_specs=[pl.BlockSpec((tm, tk), lambda kk, j, i: (i, kk)),
                  pl.BlockSpec((tm, tn), lambda kk, j, i: (i, j))],
        out_specs=pl.BlockSpec((tk, tn), lambda kk, j, i: (kk, j)),
        out_shape=jax.ShapeDtypeStruct((k, n), F32),
        compiler_params=_cp("parallel", "parallel", "arbitrary"), name=name)(a, b)


def _rmsnorm_fwd(x, g, *, name):
    s, d = x.shape
    tm = _tile(s, ROW_TILE, 8)

    def body(x_ref, g_ref, h_ref):
        xv = x_ref[...]
        y = xv * lax.rsqrt(jnp.mean(xv * xv, axis=-1, keepdims=True) + EPS)
        h_ref[...] = (y * g_ref[...]).astype(BF)

    return pl.pallas_call(
        body, grid=(s // tm,),
        in_specs=[pl.BlockSpec((tm, d), lambda i: (i, 0)), pl.BlockSpec((1, d), lambda i: (0, 0))],
        out_specs=pl.BlockSpec((tm, d), lambda i: (i, 0)),
        out_shape=jax.ShapeDtypeStruct((s, d), BF),
        compiler_params=_cp("parallel"), name=name)(x, g.reshape(1, d))


def _row_sum(v):
    return jnp.sum(v, axis=-1, keepdims=True)


def _slot_sum(v):
    ones = jnp.ones((LANES, LANES), BF)
    hi = v.astype(BF)
    lo = (v - hi.astype(F32)).astype(BF)
    return _dot(hi, ones) + _dot(lo, ones)


def _rms_bwd_math(xv, g, dh, n, row_sum=_row_sum):
    rstd = lax.rsqrt(row_sum(xv * xv) * (1.0 / n) + EPS)
    xhat = xv * rstd
    dxh = dh * g
    dx = rstd * (dxh - xhat * (row_sum(dxh * xhat) * (1.0 / n)))
    return dx, xhat


def _loss_fwd_bwd(y, target, *, name):
    s, d = y.shape
    tm = _tile(s, ROW_TILE, 8)

    def body(y_ref, t_ref, loss_ref, dy_ref):
        @pl.when(pl.program_id(0) == 0)
        def _():
            loss_ref[...] = jnp.zeros_like(loss_ref)

        err = y_ref[...] - t_ref[...]
        dy_ref[...] = err * (1.0 / d)
        loss_ref[...] += 0.5 * jnp.sum(jnp.sum(err * err, axis=-1, keepdims=True) * (1.0 / d))

    row = pl.BlockSpec((tm, d), lambda i: (i, 0))
    return pl.pallas_call(
        body, grid=(s // tm,), in_specs=[row, row],
        out_specs=[pl.BlockSpec((8, LANES), lambda i: (0, 0)), row],
        out_shape=[jax.ShapeDtypeStruct((8, LANES), F32), jax.ShapeDtypeStruct((s, d), F32)],
        compiler_params=_cp("arbitrary"), name=name)(y, target)


HALF = LANES // 2
A_HR, C_HR = A_ROT_DIM // 2, C_ROPE // 2
A_X1, C_X1 = 0, 32
LAYOUT_A = [(A_X1, 0, A_HR), (A_HR, A_ROT_DIM, A_HEAD_DIM - A_ROT_DIM), (HALF + A_X1, A_HR, A_HR)]
LAYOUT_C = [(0, 0, 32), (C_X1, C_NOPE, C_HR), (HALF, 32, 32), (HALF + C_X1, C_NOPE + C_HR, C_HR)]
LAYOUT_KN = [(0, 0, 32), (HALF, 32, 32)]
LAYOUT_KR = [(C_X1, 0, C_HR), (HALF + C_X1, C_HR, C_HR)]
LAYOUT_V = [(0, 0, C_V)]


def _rope_rows():
    lane = jnp.arange(LANES)
    fa = ROPE_THETA ** (-jnp.arange(0, A_ROT_DIM, 2, dtype=F32) / A_ROT_DIM)
    fc = ROPE_THETA ** (-jnp.arange(0, C_ROPE, 2, dtype=F32) / C_ROPE)

    def rows(f, x1, hr):
        first = (lane >= x1) & (lane < x1 + hr)
        second = (lane >= HALF + x1) & (lane < HALF + x1 + hr)
        freq = jnp.where(first | second, f[(lane - x1) % HALF % hr], 0.0)
        return freq, jnp.where(first, -1.0, jnp.where(second, 1.0, 0.0))

    freq_a, sign_a = rows(fa, A_X1, A_HR)
    freq_c, sign_c = rows(fc, C_X1, C_HR)
    return jnp.stack([freq_a, sign_a, freq_c, sign_c] + [jnp.zeros(LANES)] * 4).astype(F32)


def _rope_tables(pos_col, *, name):
    s = pos_col.shape[0]
    tm = _tile(s, ROW_TILE, 8)

    def body(pos_ref, rows_ref, ca_ref, sa_ref, cc_ref, sc_ref):
        p = pos_ref[...].astype(F32)
        ang_a = p * rows_ref[0:1, :]
        ang_c = p * rows_ref[2:3, :]
        ca_ref[...] = jnp.cos(ang_a)
        sa_ref[...] = jnp.sin(ang_a) * rows_ref[1:2, :]
        cc_ref[...] = jnp.cos(ang_c)
        sc_ref[...] = jnp.sin(ang_c) * rows_ref[3:4, :]

    tab = pl.BlockSpec((tm, LANES), lambda i: (i, 0))
    return pl.pallas_call(
        body, grid=(s // tm,),
        in_specs=[pl.BlockSpec((tm, 1), lambda i: (i, 0)), pl.BlockSpec((8, LANES), lambda i: (0, 0))],
        out_specs=[tab] * 4, out_shape=[jax.ShapeDtypeStruct((s, LANES), F32)] * 4,
        compiler_params=_cp("parallel"), name=name)(pos_col, _rope_rows())


def _rope(xv, cos, sin):
    return xv * cos + pltpu.roll(xv, HALF, 1) * sin


def _slot(ref, t):
    return ref[:, t * LANES:(t + 1) * LANES]


def _head_norm_rope(xv, g, cos, sin, n):
    y = xv * lax.rsqrt(_slot_sum(xv * xv) * (1.0 / n) + EPS) * g
    return _rope(y, cos, sin)


def _head_norm_rope_bwd(xv, g, dy, cos, sin, n):
    dyn = _rope(dy, cos, -sin)
    dx, xhat = _rms_bwd_math(xv, g, dyn, n, _slot_sum)
    return dx, jnp.sum(dyn * xhat, axis=0, keepdims=True)


def _prep_a_fwd(qkv, cos, sin, gq, gk, *, name):
    s = qkv.shape[0]
    tm = _tile(s, PREP_TILE, 8)
    nq, nkv = A_HEADS, A_KV_HEADS

    def body(x_ref, c_ref, s_ref, gq_ref, gk_ref, q_ref, k_ref, v_ref):
        cv, sv = c_ref[...], s_ref[...]
        for t in range(nq):
            q_ref[:, t * LANES:(t + 1) * LANES] = _head_norm_rope(
                _slot(x_ref, t), gq_ref[...], cv, sv, A_HEAD_DIM).astype(BF)
        for t in range(nkv):
            k_ref[:, t * LANES:(t + 1) * LANES] = _head_norm_rope(
                _slot(x_ref, nq + t), gk_ref[...], cv, sv, A_HEAD_DIM).astype(BF)
        v_ref[...] = x_ref[:, (nq + nkv) * LANES:].astype(BF)

    def rows(w):
        return pl.BlockSpec((tm, w), lambda i: (i, 0))

    vec = pl.BlockSpec((1, LANES), lambda i: (0, 0))
    return pl.pallas_call(
        body, grid=(s // tm,),
        in_specs=[rows(qkv.shape[1]), rows(LANES), rows(LANES), vec, vec],
        out_specs=[rows(nq * LANES), rows(nkv * LANES), rows(nkv * LANES)],
        out_shape=[jax.ShapeDtypeStruct((s, nq * LANES), BF), jax.ShapeDtypeStruct((s, nkv * LANES), BF),
                   jax.ShapeDtypeStruct((s, nkv * LANES), BF)],
        compiler_params=_cp("parallel"), name=name)(qkv, cos, sin, gq, gk)


def _prep_a_bwd(qkv, dq, dk, dv, cos, sin, gq, gk, *, name):
    s = qkv.shape[0]
    tm = _tile(s, PREP_TILE, 8)
    nq, nkv = A_HEADS, A_KV_HEADS

    def body(x_ref, dq_ref, dk_ref, dv_ref, c_ref, s_ref, gq_ref, gk_ref, dx_ref, dg_ref):
        @pl.when(pl.program_id(0) == 0)
        def _():
            dg_ref[...] = jnp.zeros_like(dg_ref)

        cv, sv = c_ref[...], s_ref[...]
        dgq = jnp.zeros((1, LANES), F32)
        dgk = jnp.zeros((1, LANES), F32)
        for t in range(nq):
            dx, dg = _head_norm_rope_bwd(_slot(x_ref, t), gq_ref[...], _slot(dq_ref, t), cv, sv,
                                         A_HEAD_DIM)
            dx_ref[:, t * LANES:(t + 1) * LANES] = dx.astype(BF)
            dgq = dgq + dg
        for t in range(nkv):
            dx, dg = _head_norm_rope_bwd(_slot(x_ref, nq + t), gk_ref[...], _slot(dk_ref, t), cv, sv,
                                         A_HEAD_DIM)
            dx_ref[:, (nq + t) * LANES:(nq + t + 1) * LANES] = dx.astype(BF)
            dgk = dgk + dg
        dx_ref[:, (nq + nkv) * LANES:] = dv_ref[...].astype(BF)
        dg_ref[0:1, :] += dgq
        dg_ref[1:2, :] += dgk

    def rows(w):
        return pl.BlockSpec((tm, w), lambda i: (i, 0))

    vec = pl.BlockSpec((1, LANES), lambda i: (0, 0))
    dx, dg = pl.pallas_call(
        body, grid=(s // tm,),
        in_specs=[rows(qkv.shape[1]), rows(nq * LANES), rows(nkv * LANES), rows(nkv * LANES), rows(LANES),
                  rows(LANES), vec, vec],
        out_specs=[rows(qkv.shape[1]), pl.BlockSpec((8, LANES), lambda i: (0, 0))],
        out_shape=[jax.ShapeDtypeStruct(qkv.shape, BF), jax.ShapeDtypeStruct((8, LANES), F32)],
        compiler_params=_cp("arbitrary"), name=name)(qkv, dq, dk, dv, cos, sin, gq, gk)
    return dx, dg[0:1], dg[1:2]


def _prep_c1_fwd(dn, gq, gkv, *, name):
    s = dn.shape[0]
    tm = _tile(s, ROW_TILE, 8)

    def body(x_ref, gq_ref, gkv_ref, cq_ref, ckv_ref):
        for lo, n, g_ref, o_ref in ((0, C_Q_RANK, gq_ref, cq_ref), (C_Q_RANK, C_KV_RANK, gkv_ref, ckv_ref)):
            xv = x_ref[:, lo:lo + n]
            y = xv * lax.rsqrt(jnp.mean(xv * xv, axis=-1, keepdims=True) + EPS)
            o_ref[...] = (y * g_ref[...]).astype(BF)

    def rows(w):
        return pl.BlockSpec((tm, w), lambda i: (i, 0))

    return pl.pallas_call(
        body, grid=(s // tm,),
        in_specs=[rows(dn.shape[1]), pl.BlockSpec((1, C_Q_RANK), lambda i: (0, 0)),
                  pl.BlockSpec((1, C_KV_RANK), lambda i: (0, 0))],
        out_specs=[rows(C_Q_RANK), rows(C_KV_RANK)],
        out_shape=[jax.ShapeDtypeStruct((s, C_Q_RANK), BF), jax.ShapeDtypeStruct((s, C_KV_RANK), BF)],
        compiler_params=_cp("parallel"), name=name)(dn, gq, gkv)


def _prep_c1_bwd(dn, dcq, dckv, dkr, gq, gkv, *, name):
    s = dn.shape[0]
    tm = _tile(s, ROW_TILE, 8)

    def body(x_ref, dcq_ref, dckv_ref, dkr_ref, gq_ref, gkv_ref, dx_ref, dgq_ref, dgkv_ref):
        @pl.when(pl.program_id(0) == 0)
        def _():
            dgq_ref[...] = jnp.zeros_like(dgq_ref)
            dgkv_ref[...] = jnp.zeros_like(dgkv_ref)

        for lo, n, g_ref, d_ref, dg_ref in ((0, C_Q_RANK, gq_ref, dcq_ref, dgq_ref),
                                            (C_Q_RANK, C_KV_RANK, gkv_ref, dckv_ref, dgkv_ref)):
            dv = d_ref[...]
            dx, xhat = _rms_bwd_math(x_ref[:, lo:lo + n], g_ref[...], dv, n)
            dx_ref[:, lo:lo + n] = dx.astype(BF)
            dg_ref[0:1, :] += jnp.sum(dv * xhat, axis=0, keepdims=True)
        dx_ref[:, C_Q_RANK + C_KV_RANK:] = dkr_ref[...].astype(BF)

    def rows(w):
        return pl.BlockSpec((tm, w), lambda i: (i, 0))

    def vec(w, r=1):
        return pl.BlockSpec((r, w), lambda i: (0, 0))

    dx, dgq, dgkv = pl.pallas_call(
        body, grid=(s // tm,),
        in_specs=[rows(dn.shape[1]), rows(C_Q_RANK), rows(C_KV_RANK), rows(LANES), vec(C_Q_RANK), vec(C_KV_RANK)],
        out_specs=[rows(dn.shape[1]), vec(C_Q_RANK, 8), vec(C_KV_RANK, 8)],
        out_shape=[jax.ShapeDtypeStruct(dn.shape, BF), jax.ShapeDtypeStruct((8, C_Q_RANK), F32),
                   jax.ShapeDtypeStruct((8, C_KV_RANK), F32)],
        compiler_params=_cp("arbitrary"), name=name)(dn, dcq, dckv, dkr, gq, gkv)
    return dx, dgq[0:1], dgkv[0:1]


def _prep_c2_fwd(qraw, knope, dn, cos, sin, gq, gk, *, name):
    s = qraw.shape[0]
    tm = _tile(s, PREP_TILE, 8)
    kr_blk = dn.shape[1] // LANES - 1

    def body(q_ref, kn_ref, kr_ref, c_ref, s_ref, gq_ref, gk_ref, qo_ref, ko_ref):
        cv, sv, kr = c_ref[...], s_ref[...], kr_ref[...]
        for t in range(C_HEADS):
            qo_ref[:, t * LANES:(t + 1) * LANES] = _head_norm_rope(
                _slot(q_ref, t), gq_ref[...], cv, sv, C_QK).astype(BF)
            ko_ref[:, t * LANES:(t + 1) * LANES] = _head_norm_rope(
                _slot(kn_ref, t) + kr, gk_ref[...], cv, sv, C_QK).astype(BF)

    def rows(w, blk=0):
        return pl.BlockSpec((tm, w), lambda i: (i, blk))

    vec = pl.BlockSpec((1, LANES), lambda i: (0, 0))
    w = C_HEADS * LANES
    return pl.pallas_call(
        body, grid=(s // tm,),
        in_specs=[rows(w), rows(w), rows(LANES, kr_blk), rows(LANES), rows(LANES), vec, vec],
        out_specs=[rows(w), rows(w)],
        out_shape=[jax.ShapeDtypeStruct((s, w), BF)] * 2,
        compiler_params=_cp("parallel"), name=name)(qraw, knope, dn, cos, sin, gq, gk)


def _prep_c2_bwd(qraw, knope, dn, dq, dk, cos, sin, gq, gk, *, name):
    s = qraw.shape[0]
    tm = _tile(s, PREP_TILE, 8)
    kr_blk = dn.shape[1] // LANES - 1

    def body(q_ref, kn_ref, kr_ref, dq_ref, dk_ref, c_ref, s_ref, gq_ref, gk_ref,
             dqo_ref, dkno_ref, dkr_ref, dg_ref):
        @pl.when(pl.program_id(0) == 0)
        def _():
            dg_ref[...] = jnp.zeros_like(dg_ref)

        cv, sv, kr = c_ref[...], s_ref[...], kr_ref[...]
        dgq = jnp.zeros((1, LANES), F32)
        dgk = jnp.zeros((1, LANES), F32)
        dkr = jnp.zeros((tm, LANES), F32)
        for t in range(C_HEADS):
            dx, dg = _head_norm_rope_bwd(_slot(q_ref, t), gq_ref[...], _slot(dq_ref, t), cv, sv,
                                         C_QK)
            dqo_ref[:, t * LANES:(t + 1) * LANES] = dx.astype(BF)
            dgq = dgq + dg
            dx, dg = _head_norm_rope_bwd(_slot(kn_ref, t) + kr, gk_ref[...], _slot(dk_ref, t), cv, sv,
                                         C_QK)
            dkno_ref[:, t * LANES:(t + 1) * LANES] = dx.astype(BF)
            dkr = dkr + dx
            dgk = dgk + dg
        dkr_ref[...] = dkr
        dg_ref[0:1, :] += dgq
        dg_ref[1:2, :] += dgk

    def rows(w, blk=0):
        return pl.BlockSpec((tm, w), lambda i: (i, blk))

    vec = pl.BlockSpec((1, LANES), lambda i: (0, 0))
    w = C_HEADS * LANES
    dqo, dkno, dkr, dg = pl.pallas_call(
        body, grid=(s // tm,),
        in_specs=[rows(w), rows(w), rows(LANES, kr_blk), rows(w), rows(w), rows(LANES), rows(LANES), vec, vec],
        out_specs=[rows(w), rows(w), rows(LANES), pl.BlockSpec((8, LANES), lambda i: (0, 0))],
        out_shape=[jax.ShapeDtypeStruct((s, w), BF), jax.ShapeDtypeStruct((s, w), BF),
                   jax.ShapeDtypeStruct((s, LANES), F32), jax.ShapeDtypeStruct((8, LANES), F32)],
        compiler_params=_cp("arbitrary"), name=name)(qraw, knope, dn, dq, dk, cos, sin, gq, gk)
    return dqo, dkno, dkr, dg[0:1], dg[1:2]


def _lane_pick(mat, g):
    lane = lax.broadcasted_iota(jnp.int32, mat.shape, 1)
    return jnp.sum(jnp.where(lane == g, mat, 0.0), axis=-1, keepdims=True)


def _swa_fwd(q, k, v, sinks, *, name):
    s = q.shape[0]
    nb = min(SWA_BLOCKS, s // BLOCK)
    qt = nb * BLOCK
    scale = A_HEAD_DIM ** -0.5
    gw = A_GROUP * LANES

    def body(sink_ref, q_ref, kc_ref, kp_ref, vc_ref, vp_ref, o_ref, lse_ref):
        kvh, n = pl.program_id(0), pl.program_id(1)
        rows = A_GROUP * BLOCK
        kall = jnp.concatenate([kp_ref[...], kc_ref[...]], axis=0)
        vall = jnp.concatenate([vp_ref[...], vc_ref[...]], axis=0)
        r = lax.broadcasted_iota(jnp.int32, (rows, 2 * BLOCK), 0) & (BLOCK - 1)
        c = lax.broadcasted_iota(jnp.int32, (rows, 2 * BLOCK), 1)
        cur_ok = (c >= BLOCK) & (c - BLOCK <= r)
        prev_ok = (c < BLOCK) & (c > r)
        head = lax.broadcasted_iota(jnp.int32, (rows, 1), 0) >> 7
        sink = jnp.zeros((rows, 1), F32)
        for g in range(A_GROUP):
            sink = jnp.where(head == g, sink_ref[kvh * A_GROUP + g], sink)
        lane = lax.broadcasted_iota(jnp.int32, (BLOCK, LANES), 1)
        for b in range(nb):
            blk = slice(b * BLOCK, (b + 1) * BLOCK)
            qs = jnp.concatenate([q_ref[blk, g * LANES:(g + 1) * LANES] for g in range(A_GROUP)], axis=0)
            k2 = kall[b * BLOCK:(b + 2) * BLOCK]
            v2 = vall[b * BLOCK:(b + 2) * BLOCK]
            ok = (cur_ok | prev_ok) if b > 0 else (cur_ok | (prev_ok & (n > 0)))
            sc = jnp.where(ok, _dot_nt(qs, k2) * scale, NEG)
            m = jnp.maximum(jnp.max(sc, axis=-1, keepdims=True), sink)
            p = jnp.exp(sc - m)
            l = jnp.sum(p, axis=-1, keepdims=True) + jnp.exp(sink - m)
            o = _dot((p * (1.0 / l)).astype(BF), v2)
            lse = m + jnp.log(l)
            lse_mat = jnp.zeros((BLOCK, LANES), F32)
            for g in range(A_GROUP):
                o_ref[blk, g * LANES:(g + 1) * LANES] = o[g * BLOCK:(g + 1) * BLOCK].astype(BF)
                lse_mat = jnp.where(lane == g, lse[g * BLOCK:(g + 1) * BLOCK], lse_mat)
            lse_ref[0, blk, :] = lse_mat

    cur = pl.BlockSpec((qt, LANES), lambda h, n: (n, h))
    prev = pl.BlockSpec((BLOCK, LANES), lambda h, n: (jnp.maximum(n * nb - 1, 0), h))
    return pl.pallas_call(
        body, grid=(A_KV_HEADS, s // qt),
        in_specs=[pl.BlockSpec(memory_space=pltpu.SMEM), pl.BlockSpec((qt, gw), lambda h, n: (n, h)),
                  cur, prev, cur, prev],
        out_specs=[pl.BlockSpec((qt, gw), lambda h, n: (n, h)),
                   pl.BlockSpec((1, qt, LANES), lambda h, n: (h, n, 0))],
        out_shape=[jax.ShapeDtypeStruct(q.shape, BF), jax.ShapeDtypeStruct((A_KV_HEADS, s, LANES), F32)],
        compiler_params=_cp("parallel", "parallel"), name=name)(sinks, q, k, k, v, v)


def _swa_bwd(q, k, v, o, do, lse, sinks, *, name):
    s = q.shape[0]
    nblk = s // BLOCK
    nb = min(SWA_BLOCKS, nblk)
    qt = nb * BLOCK
    nsteps = s // qt
    scale = A_HEAD_DIM ** -0.5
    gw = A_GROUP * LANES

    def body(sink_ref, qc_ref, qn_ref, k_ref, v_ref, oc_ref, on_ref, doc_ref, don_ref, lc_ref, ln_ref,
             dq_ref, dk_ref, dv_ref, dsink_ref, carry):
        kvh, n = pl.program_id(0), pl.program_id(1)

        @pl.when(n == 0)
        def _():
            carry[...] = jnp.zeros_like(carry)
            dsink_ref[...] = jnp.zeros_like(dsink_ref)

        half = A_GROUP * BLOCK
        rows = 2 * half
        qall = jnp.concatenate([qc_ref[...], qn_ref[...]], axis=0)
        oall = jnp.concatenate([oc_ref[...], on_ref[...]], axis=0)
        doall = jnp.concatenate([doc_ref[...], don_ref[...]], axis=0)
        lall = jnp.concatenate([lc_ref[0], ln_ref[0]], axis=0)
        row = lax.broadcasted_iota(jnp.int32, (rows, BLOCK), 0)
        r = row & (BLOCK - 1)
        c = lax.broadcasted_iota(jnp.int32, (rows, BLOCK), 1)
        diag_ok = (row < half) & (c <= r)
        next_ok = (row >= half) & (c > r)
        lane8 = lax.broadcasted_iota(jnp.int32, (8, LANES), 1)
        dsink = jnp.zeros((8, LANES), F32)
        off_prev = None
        for b in range(nb):
            blk = slice(b * BLOCK, (b + 1) * BLOCK)
            kv, vv = k_ref[blk, :], v_ref[blk, :]

            def stack(allv):
                return jnp.concatenate(
                    [allv[b * BLOCK:(b + 1) * BLOCK, g * LANES:(g + 1) * LANES] for g in range(A_GROUP)]
                    + [allv[(b + 1) * BLOCK:(b + 2) * BLOCK, g * LANES:(g + 1) * LANES] for g in range(A_GROUP)],
                    axis=0)

            qs, osk, dos = stack(qall), stack(oall), stack(doall)
            lse = jnp.concatenate([_lane_pick(lall[b * BLOCK:(b + 1) * BLOCK], g) for g in range(A_GROUP)]
                                  + [_lane_pick(lall[(b + 1) * BLOCK:(b + 2) * BLOCK], g) for g in range(A_GROUP)],
                                  axis=0)
            delta = jnp.sum(dos.astype(F32) * osk.astype(F32), axis=-1, keepdims=True)
            ok = (diag_ok | next_ok) if b < nb - 1 else (diag_ok | (next_ok & (n < nsteps - 1)))
            sc = jnp.where(ok, _dot_nt(qs, kv) * scale, NEG)
            p = jnp.exp(sc - lse)
            dv_ref[blk, :] = _dot_tn(p.astype(BF), dos)
            dp = _dot_nt(dos, vv)
            ds = (p * (dp - delta) * scale).astype(BF)
            dk_ref[blk, :] = _dot_tn(ds, qs)
            dqs = _dot(ds, kv)
            for g in range(A_GROUP):
                cur = slice(g * BLOCK, (g + 1) * BLOCK)
                before = carry[:, g * LANES:(g + 1) * LANES] if b == 0 else off_prev[cur]
                dq_ref[blk, g * LANES:(g + 1) * LANES] = before + dqs[cur]
                p_sink = jnp.exp(sink_ref[kvh * A_GROUP + g] - lse[cur])
                dsink = dsink + jnp.where(lane8 == g, -jnp.sum(p_sink * delta[cur]), 0.0)
            off_prev = dqs[half:]
        for g in range(A_GROUP):
            carry[:, g * LANES:(g + 1) * LANES] = off_prev[g * BLOCK:(g + 1) * BLOCK]
        dsink_ref[...] += dsink

    def nxt(n):
        return jnp.minimum((n + 1) * nb, nblk - 1)

    grp_c = pl.BlockSpec((qt, gw), lambda h, n: (n, h))
    grp_n = pl.BlockSpec((BLOCK, gw), lambda h, n: (nxt(n), h))
    kvb = pl.BlockSpec((qt, LANES), lambda h, n: (n, h))
    lse_c = pl.BlockSpec((1, qt, LANES), lambda h, n: (h, n, 0))
    lse_n = pl.BlockSpec((1, BLOCK, LANES), lambda h, n: (h, nxt(n), 0))
    dq, dk, dv, dsink = pl.pallas_call(
        body, grid=(A_KV_HEADS, nsteps),
        in_specs=[pl.BlockSpec(memory_space=pltpu.SMEM), grp_c, grp_n, kvb, kvb, grp_c, grp_n, grp_c, grp_n,
                  lse_c, lse_n],
        out_specs=[grp_c, kvb, kvb, pl.BlockSpec((8, LANES), lambda h, n: (h, 0))],
        out_shape=[jax.ShapeDtypeStruct(q.shape, F32), jax.ShapeDtypeStruct(k.shape, F32),
                   jax.ShapeDtypeStruct(v.shape, F32), jax.ShapeDtypeStruct((A_KV_HEADS * 8, LANES), F32)],
        scratch_shapes=[pltpu.VMEM((BLOCK, gw), F32)],
        compiler_params=_cp("parallel", "arbitrary"), name=name)(sinks, q, q, k, v, o, o, do, do, lse, lse)
    dsinks = dsink.reshape(A_KV_HEADS, 8, LANES)[:, 0, :A_GROUP].reshape(A_HEADS)
    return dq, dk, dv, dsinks


LOG2E = 1.4426950408889634


def _mla_fwd(q, k, v, *, name):
    s = q.shape[0]
    t = _tile(s, ATTN_TILE, LANES)
    nt = s // t
    scale = C_QK ** -0.5
    c2 = scale * LOG2E

    def body(q_ref, k_ref, v_ref, o_ref, lse_ref, m_sc, acc_sc):
        qi = pl.program_id(1)
        qv = q_ref[...]
        m_sc[...] = jnp.full_like(m_sc, NEG)
        acc_sc[...] = jnp.zeros_like(acc_sc)

        def step(start, width, diag):
            rows = pl.ds(pl.multiple_of(start, t), width)
            sc = _dot_nt(qv, k_ref[rows, :])
            if diag:
                r = lax.broadcasted_iota(jnp.int32, (t, t), 0)
                c = lax.broadcasted_iota(jnp.int32, (t, t), 1)
                own = jnp.where(c <= r, sc[:, width - t:], NEG)
                sc = own if width == t else jnp.concatenate([sc[:, :width - t], own], axis=1)
            m_prev = m_sc[...]
            m_new = jnp.maximum(m_prev, jnp.max(sc, axis=-1, keepdims=True))
            alpha = jnp.exp2((m_prev - m_new) * c2)
            p = jnp.exp2((sc - jnp.tile(m_new, (1, width // LANES))) * c2).astype(BF)
            lane = lax.broadcasted_iota(jnp.int32, (width, LANES), 1)
            vv = jnp.where(lane == C_V, jnp.ones((), BF), v_ref[rows, :])
            acc_sc[...] = alpha * acc_sc[...] + _dot(p, vv)
            m_sc[...] = m_new

        def single_body(j, carry):
            step(j * t, t, False)
            return carry

        with_diag = qi >= FWD_WIDE - 1

        @pl.when(with_diag)
        def _():
            step((qi - (FWD_WIDE - 1)) * t, FWD_WIDE * t, True)

        @pl.when(jnp.logical_not(with_diag))
        def _():
            step(qi * t, t, True)

        rest = jnp.where(with_diag, qi - (FWD_WIDE - 1), qi)
        n_wide = rest // FWD_WIDE
        done = n_wide * FWD_WIDE
        n_pair = (rest - done) // 2

        def wide_body(wi, carry):
            step(wi * (FWD_WIDE * t), FWD_WIDE * t, False)
            return carry

        def pair_body(wi, carry):
            step((done + 2 * wi) * t, 2 * t, False)
            return carry

        lax.fori_loop(0, n_wide, wide_body, 0)
        lax.fori_loop(0, n_pair, pair_body, 0)
        lax.fori_loop(done + 2 * n_pair, rest, single_body, 0)
        acc = acc_sc[...]
        l = _lane_pick(acc, C_V)
        lane = lax.broadcasted_iota(jnp.int32, (t, LANES), 1)
        o_ref[...] = jnp.where(lane == C_V, 0.0, acc * (1.0 / l)).astype(BF)
        lse2 = m_sc[...] * c2 + jnp.log(l) * LOG2E
        lse_ref[0, 0] = jnp.transpose(lse2)[0:8, :]

    qspec = pl.BlockSpec((t, LANES), lambda h, i: (i, h))
    kspec = pl.BlockSpec((s, LANES), lambda h, i: (0, h))
    return pl.pallas_call(
        body, grid=(C_HEADS, nt), in_specs=[qspec, kspec, kspec],
        out_specs=[qspec, pl.BlockSpec((1, 1, 8, t), lambda h, i: (h, i, 0, 0))],
        out_shape=[jax.ShapeDtypeStruct(q.shape, BF), jax.ShapeDtypeStruct((C_HEADS, nt, 8, t), F32)],
        scratch_shapes=[pltpu.VMEM((t, LANES), F32)] * 2,
        compiler_params=_cp("parallel", "parallel"), name=name)(q, k, v)


def _mla_delta(o, do, *, name):
    s = o.shape[0]
    t = _tile(s, ATTN_TILE, LANES)
    nt = s // t

    def body(o_ref, do_ref, d_ref):
        for h in range(C_HEADS):
            prod = jnp.transpose(_slot(o_ref, h).astype(F32) * _slot(do_ref, h).astype(F32))
            d_ref[h, 0] = jnp.broadcast_to(jnp.sum(prod, axis=0, keepdims=True), (8, t))

    blk = pl.BlockSpec((t, C_HEADS * LANES), lambda i: (i, 0))
    return pl.pallas_call(
        body, grid=(nt,), in_specs=[blk, blk],
        out_specs=pl.BlockSpec((C_HEADS, 1, 8, t), lambda i: (0, i, 0, 0)),
        out_shape=jax.ShapeDtypeStruct((C_HEADS, nt, 8, t), F32),
        compiler_params=_cp("parallel"), name=name)(o, do)


def _mla_bwd(q, k, v, do, lse2, delta, *, name):
    s = q.shape[0]
    t = _tile(s, ATTN_TILE, LANES)
    nt = s // t
    scale = C_QK ** -0.5
    c2 = scale * LOG2E

    def body(q_ref, do_ref, k_ref, v_ref, lse_ref, dl_ref, dq_ref, dk_ref, dv_ref):
        kj = pl.program_id(1)

        @pl.when(kj == 0)
        def _():
            dq_ref[...] = jnp.zeros_like(dq_ref)

        dk_ref[...] = jnp.zeros_like(dk_ref)
        dv_ref[...] = jnp.zeros_like(dv_ref)
        kv, vv = k_ref[...], v_ref[...]

        def step(i, n, diag):
            rows = pl.ds(pl.multiple_of(i * t, t), n * t)
            qv, dov = q_ref[rows, :], do_ref[rows, :]
            st = _dot_nt(kv, qv)
            if diag:
                r = lax.broadcasted_iota(jnp.int32, (t, t), 0)
                c = lax.broadcasted_iota(jnp.int32, (t, t), 1)
                own = jnp.where(r <= c, st[:, :t], NEG)
                st = own if n == 1 else jnp.concatenate([own, st[:, t:]], axis=1)
            lse = jnp.concatenate([lse_ref[0, i + u, 0:1, :] for u in range(n)], axis=1)
            dl = jnp.concatenate([dl_ref[0, i + u, 0:1, :] for u in range(n)], axis=1)
            pt = jnp.exp2(st * c2 - lse)
            dv_ref[...] += _dot(pt.astype(BF), dov)
            dpt = _dot_nt(vv, dov)
            dst = (pt * (dpt - dl) * scale).astype(BF)
            dk_ref[...] += _dot(dst, qv)
            dq_ref[rows, :] += _dot_tn(dst, kv)

        width = min(BWD_WIDE, nt)
        lead = (nt - kj - 1) % width + 1
        for v in range(1, width + 1):
            @pl.when(lead == v)
            def _(v=v):
                step(kj, v, True)

        first = kj + lead

        def run_body(pi, carry):
            step(first + width * pi, width, False)
            return carry

        lax.fori_loop(0, (nt - first) // width, run_body, 0)

    res = pl.BlockSpec((s, LANES), lambda h, j: (0, h))
    kspec = pl.BlockSpec((t, LANES), lambda h, j: (j, h))
    stat = pl.BlockSpec((1, nt, 8, t), lambda h, j: (h, 0, 0, 0))
    return pl.pallas_call(
        body, grid=(C_HEADS, nt), in_specs=[res, res, kspec, kspec, stat, stat],
        out_specs=[res, kspec, kspec],
        out_shape=[jax.ShapeDtypeStruct(q.shape, F32)] * 3,
        compiler_params=_cp("parallel", "arbitrary"), name=name)(q, do, k, v, lse2, delta)


def _conv_shifted(z, zprev, tm):
    row = lax.broadcasted_iota(jnp.int32, z.shape, 0)
    z1 = jnp.where(row == 0, zprev[7:8], pltpu.roll(z, 1, 0))
    z2 = jnp.where(row == 0, zprev[6:7], jnp.where(row == 1, zprev[7:8], pltpu.roll(z, 2, 0)))
    return z1, z2


def _conv_fwd(proj, w, *, name):
    s = proj.shape[0]
    d = D_MODEL
    tm = _tile(s, PREP_TILE, 8)

    def body(x_ref, xp_ref, w_ref, y_ref):
        i = pl.program_id(0)
        z = x_ref[:, d:2 * d] * x_ref[:, 2 * d:]
        zprev = jnp.where(i > 0, xp_ref[:, d:2 * d] * xp_ref[:, 2 * d:], 0.0)
        z1, z2 = _conv_shifted(z, zprev, tm)
        y = w_ref[0:1, :] * z2 + w_ref[1:2, :] * z1 + w_ref[2:3, :] * z
        y_ref[...] = (x_ref[:, :d] * y).astype(BF)

    per8 = tm // 8
    return pl.pallas_call(
        body, grid=(s // tm,),
        in_specs=[pl.BlockSpec((tm, 3 * d), lambda i: (i, 0)),
                  pl.BlockSpec((8, 3 * d), lambda i: (jnp.maximum(i * per8 - 1, 0), 0)),
                  pl.BlockSpec((8, d), lambda i: (0, 0))],
        out_specs=pl.BlockSpec((tm, d), lambda i: (i, 0)),
        out_shape=jax.ShapeDtypeStruct((s, d), BF),
        compiler_params=_cp("parallel"), name=name)(proj, proj, w)


def _conv_bwd(proj, dyb, w, *, name):
    s = proj.shape[0]
    d = D_MODEL
    tm = _tile(s, PREP_TILE, 8)
    nt = s // tm

    def body(x_ref, xp_ref, xn_ref, dy_ref, dyn_ref, w_ref, dx_ref, dw_ref):
        i = pl.program_id(0)

        @pl.when(i == 0)
        def _():
            dw_ref[...] = jnp.zeros_like(dw_ref)

        b, c, u = x_ref[:, :d], x_ref[:, d:2 * d], x_ref[:, 2 * d:]
        z = c * u
        zprev = jnp.where(i > 0, xp_ref[:, d:2 * d] * xp_ref[:, 2 * d:], 0.0)
        z1, z2 = _conv_shifted(z, zprev, tm)
        w0, w1, w2 = w_ref[0:1, :], w_ref[1:2, :], w_ref[2:3, :]
        y = w0 * z2 + w1 * z1 + w2 * z
        dyb_v = dy_ref[...]
        dyc = dyb_v * b
        dyn = jnp.where(i < nt - 1, dyn_ref[...] * xn_ref[:, :d], 0.0)
        row = lax.broadcasted_iota(jnp.int32, dyc.shape, 0)
        d1 = jnp.where(row == tm - 1, dyn[0:1], pltpu.roll(dyc, tm - 1, 0))
        d2 = jnp.where(row == tm - 1, dyn[1:2], jnp.where(row == tm - 2, dyn[0:1], pltpu.roll(dyc, tm - 2, 0)))
        dz = w2 * dyc + w1 * d1 + w0 * d2
        dx_ref[:, :d] = (dyb_v * y).astype(BF)
        dx_ref[:, d:2 * d] = (dz * u).astype(BF)
        dx_ref[:, 2 * d:] = (dz * c).astype(BF)
        dw_ref[0:1, :] += jnp.sum(dyc * z2, axis=0, keepdims=True)
        dw_ref[1:2, :] += jnp.sum(dyc * z1, axis=0, keepdims=True)
        dw_ref[2:3, :] += jnp.sum(dyc * z, axis=0, keepdims=True)

    per8 = tm // 8
    last8 = s // 8 - 1
    dx, dw = pl.pallas_call(
        body, grid=(nt,),
        in_specs=[pl.BlockSpec((tm, 3 * d), lambda i: (i, 0)),
                  pl.BlockSpec((8, 3 * d), lambda i: (jnp.maximum(i * per8 - 1, 0), 0)),
                  pl.BlockSpec((8, 3 * d), lambda i: (jnp.minimum((i + 1) * per8, last8), 0)),
                  pl.BlockSpec((tm, d), lambda i: (i, 0)),
                  pl.BlockSpec((8, d), lambda i: (jnp.minimum((i + 1) * per8, last8), 0)),
                  pl.BlockSpec((8, d), lambda i: (0, 0))],
        out_specs=[pl.BlockSpec((tm, 3 * d), lambda i: (i, 0)), pl.BlockSpec((8, d), lambda i: (0, 0))],
        out_shape=[jax.ShapeDtypeStruct((s, 3 * d), BF), jax.ShapeDtypeStruct((8, d), F32)],
        compiler_params=_cp("arbitrary"), name=name)(proj, proj, proj, dyb, dyb, w)
    return dx, dw[0:3]


def _ffn_up(h, wgu, *, name):
    s, d = h.shape
    f = wgu.shape[1] // 2
    tm = _tile(s, ROW_TILE, 8)
    tn = _tile(f, 1408, 128)
    nb = f // tn

    def body(h_ref, wg_ref, wu_ref, dg_ref, du_ref, a_ref):
        hv = h_ref[...]
        g = _dot(hv, wg_ref[...])
        u = _dot(hv, wu_ref[...])
        sig = jax.nn.sigmoid(g)
        silu = g * sig
        dg_ref[...] = (u * (sig + silu * (1.0 - sig))).astype(BF)
        du_ref[...] = silu.astype(BF)
        a_ref[...] = (silu * u).astype(BF)

    out = pl.BlockSpec((tm, tn), lambda j, i: (i, j))
    return pl.pallas_call(
        body, grid=(nb, s // tm),
        in_specs=[pl.BlockSpec((tm, d), lambda j, i: (i, 0)),
                  pl.BlockSpec((d, tn), lambda j, i: (0, j)),
                  pl.BlockSpec((d, tn), lambda j, i: (0, j + nb))],
        out_specs=[out] * 3, out_shape=[jax.ShapeDtypeStruct((s, f), BF)] * 3,
        compiler_params=_cp("parallel", "parallel"), name=name)(h, wgu, wgu)


def _ffn_bwd_mid(dy, wd, gate_factor, up_factor, *, name):
    s, d = dy.shape
    f = wd.shape[0]
    tm = _tile(s, ROW_TILE, 8)
    tn = _tile(f, 1408, 128)

    def body(dy_ref, wd_ref, g_ref, u_ref, dg_ref, du_ref):
        dyv = dy_ref[...].astype(BF)
        for c0 in range(0, tn, 2 * LANES):
            cols = slice(c0, min(c0 + 2 * LANES, tn))
            dact = _dot_nt(dyv, wd_ref[cols, :])
            dg_ref[:, cols] = (dact * g_ref[:, cols].astype(F32)).astype(BF)
            du_ref[:, cols] = (dact * u_ref[:, cols].astype(F32)).astype(BF)

    blk = pl.BlockSpec((tm, tn), lambda j, i: (i, j))
    return pl.pallas_call(
        body, grid=(f // tn, s // tm),
        in_specs=[pl.BlockSpec((tm, d), lambda j, i: (i, 0)), pl.BlockSpec((tn, d), lambda j, i: (j, 0)), blk, blk],
        out_specs=[blk, blk], out_shape=[jax.ShapeDtypeStruct((s, f), BF)] * 2,
        compiler_params=_cp("parallel", "parallel"), name=name)(dy, wd, gate_factor, up_factor)


def _pad_cols(w, heads, dim, layout):
    k = w.shape[0]
    w3 = w.reshape(k, heads, dim)
    pieces, lane = [], 0
    for lane0, dim0, cnt in sorted(layout):
        if lane0 > lane:
            pieces.append(jnp.zeros((k, heads, lane0 - lane), w.dtype))
        pieces.append(w3[:, :, dim0:dim0 + cnt])
        lane = lane0 + cnt
    if lane < LANES:
        pieces.append(jnp.zeros((k, heads, LANES - lane), w.dtype))
    return jnp.concatenate(pieces, axis=2).reshape(k, heads * LANES)


def _unpad_cols(w, heads, dim, layout):
    k = w.shape[0]
    w3 = w.reshape(k, heads, LANES)
    by_dim = sorted(layout, key=lambda seg: seg[1])
    assert sum(cnt for _, _, cnt in by_dim) == dim
    return jnp.concatenate([w3[:, :, lane0:lane0 + cnt] for lane0, _, cnt in by_dim], axis=2).reshape(k, heads * dim)


def _pad_rows(w, heads, dim):
    n = w.shape[1]
    return jnp.pad(w.reshape(heads, dim, n), ((0, 0), (0, LANES - dim), (0, 0))).reshape(heads * LANES, n)


def _unpad_rows(w, heads, dim):
    n = w.shape[1]
    return w.reshape(heads, LANES, n)[:, :dim, :].reshape(heads * dim, n)


def _pad_vec(g, layout):
    return _pad_cols(g.reshape(1, -1), 1, g.shape[0], layout)


def _unpad_vec(g, dim, layout):
    return _unpad_cols(g.reshape(1, LANES), 1, dim, layout)[0]


def _local_step(x, positions, w, target):
    s = x.shape[0]
    cos_a, sin_a, cos_c, sin_c = _rope_tables(positions.reshape(s, 1), name="rope_tables")
    grads = {n: [None] * len(w[n]) for n in WEIGHTS}
    saved = []

    for i in range(DEPTH):
        kind, j = i % N_MIXERS, i // N_MIXERS
        tag = f"l{i}"
        if i == 0:
            h = _rmsnorm_fwd(x, w['mix_norm'][i], name=f"{tag}_mix_norm")
        fnorm = w['ffn_norm'][i]
        if kind == 0:
            nqk = (A_HEADS + A_KV_HEADS) * A_HEAD_DIM
            wqkv = jnp.concatenate(
                [_pad_cols(w['a_w_qkv'][j][:, :nqk], A_HEADS + A_KV_HEADS, A_HEAD_DIM, LAYOUT_A),
                 _pad_cols(w['a_w_qkv'][j][:, nqk:], A_KV_HEADS, A_HEAD_DIM, LAYOUT_V)], axis=1)
            wo = _pad_rows(w['a_w_o'][j], A_HEADS, A_HEAD_DIM)
            gq, gk = _pad_vec(w['a_q_norm'][j], LAYOUT_A), _pad_vec(w['a_k_norm'][j], LAYOUT_A)
            qkv = _mm([(h, wqkv, 0)], out_dtype=F32, name=f"{tag}_qkv")
            qa, ka, va = _prep_a_fwd(qkv, cos_a, sin_a, gq, gk, name=f"{tag}_prep")
            o, lse = _swa_fwd(qa, ka, va, w['a_sinks'][j], name=f"{tag}_attn")
            x1, h2 = _mm([(o, wo, 0)], out_dtype=F32, res=x, norm_g=fnorm, name=f"{tag}_wo")
            mix = dict(wqkv=wqkv, wo=wo, gq=gq, gk=gk, qkv=qkv, qa=qa, ka=ka, va=va, o=o, lse=lse)
        elif kind == 1:
            cw = jnp.pad(w['b_conv_w'][j], ((0, 5), (0, 0)))
            proj = _mm([(h, w['b_w_in'][j], 0)], out_dtype=F32, name=f"{tag}_win")
            yb = _conv_fwd(proj, cw, name=f"{tag}_conv")
            x1, h2 = _mm([(yb, w['b_w_out'][j], 0)], out_dtype=F32, res=x, norm_g=fnorm, name=f"{tag}_wout")
            mix = dict(cw=cw, proj=proj, yb=yb)
        else:
            wdn = w['c_w_down'][j]
            nqk = C_Q_RANK + C_KV_RANK
            wdn = jnp.concatenate([wdn[:, :nqk], _pad_cols(wdn[:, nqk:], 1, C_ROPE, LAYOUT_KR)], axis=1)
            wq = _pad_cols(w['c_w_q_up'][j], C_HEADS, C_QK, LAYOUT_C)
            wkv = w['c_w_kv_up'][j].reshape(C_KV_RANK, C_HEADS, C_NOPE + C_V)
            wkn = _pad_cols(wkv[:, :, :C_NOPE].reshape(C_KV_RANK, -1), C_HEADS, C_NOPE, LAYOUT_KN)
            wv = _pad_cols(wkv[:, :, C_NOPE:].reshape(C_KV_RANK, -1), C_HEADS, C_V, LAYOUT_V)
            wo = _pad_rows(w['c_w_o'][j], C_HEADS, C_V)
            gq, gk = _pad_vec(w['c_q_norm'][j], LAYOUT_C), _pad_vec(w['c_k_norm'][j], LAYOUT_C)
            gqa, gkva = w['c_q_a_norm'][j].reshape(1, -1), w['c_kv_a_norm'][j].reshape(1, -1)
            dn = _mm([(h, wdn, 0)], out_dtype=F32, name=f"{tag}_wdown")
            cqn, ckvn = _prep_c1_fwd(dn, gqa, gkva, name=f"{tag}_prep1")
            qraw = _mm([(cqn, wq, 0)], out_dtype=F32, name=f"{tag}_wq")
            knope = _mm([(ckvn, wkn, 0)], out_dtype=F32, name=f"{tag}_wkn")
            vc = _mm([(ckvn, wv, 0)], out_dtype=BF, name=f"{tag}_wv")
            qc, kc = _prep_c2_fwd(qraw, knope, dn, cos_c, sin_c, gq, gk, name=f"{tag}_prep2")
            o, lse = _mla_fwd(qc, kc, vc, name=f"{tag}_attn")
            x1, h2 = _mm([(o, wo, 0)], out_dtype=F32, res=x, norm_g=fnorm, name=f"{tag}_wo")
            mix = dict(wdn=wdn, wq=wq, wkn=wkn, wv=wv, wo=wo, gq=gq, gk=gk, gqa=gqa, gkva=gkva, dn=dn, cqn=cqn,
                       ckvn=ckvn, qraw=qraw, knope=knope, vc=vc, qc=qc, kc=kc, o=o, lse=lse)
        gate_factor, up_factor, act = _ffn_up(h2, w['f_w_gate_up'][i], name=f"{tag}_ffn_up")
        saved.append(dict(x=x, h=h, x1=x1, h2=h2, gate_factor=gate_factor, up_factor=up_factor, act=act, mix=mix))
        if i + 1 < DEPTH:
            x, h = _mm([(act, w['f_w_down'][i], 0)], out_dtype=F32, res=x1, norm_g=w['mix_norm'][i + 1],
                       name=f"{tag}_ffn_down")
        else:
            x = _mm([(act, w['f_w_down'][i], 0)], out_dtype=F32, res=x1, name=f"{tag}_ffn_down")

    loss_blk, dx = _loss_fwd_bwd(x, target, name="loss")

    for i in reversed(range(DEPTH)):
        kind, j = i % N_MIXERS, i // N_MIXERS
        tag = f"l{i}b"
        sv = saved[i]
        mix = sv['mix']
        wgu, wd = w['f_w_gate_up'][i], w['f_w_down'][i]
        grads['f_w_down'][i] = _mm_tn(sv['act'], dx, name=f"{tag}_dwd")
        dgate, dup = _ffn_bwd_mid(dx, wd, sv['gate_factor'], sv['up_factor'], name=f"{tag}_ffn_mid")
        grads['f_w_gate_up'][i] = (_mm_tn(sv['h2'], dgate, name=f"{tag}_dwg"),
                                   _mm_tn(sv['h2'], dup, name=f"{tag}_dwu"))
        dx, dg = _mm_norm_bwd([(dgate, wgu, 0), (dup, wgu, 1)], sv['x1'], w['ffn_norm'][i], dx, name=f"{tag}_dh2")
        grads['ffn_norm'][i] = dg[0]
        if kind == 0:
            grads['a_w_o'][j] = _unpad_rows(_mm_tn(mix['o'], dx, name=f"{tag}_dwo"), A_HEADS, A_HEAD_DIM)
            do = _mm([(dx, mix['wo'], 0)], out_dtype=BF, trans_b=True, name=f"{tag}_do")
            dqa, dka, dva, dsinks = _swa_bwd(mix['qa'], mix['ka'], mix['va'], mix['o'], do, mix['lse'],
                                             w['a_sinks'][j], name=f"{tag}_attn")
            dqkv, dgq, dgk = _prep_a_bwd(mix['qkv'], dqa, dka, dva, cos_a, sin_a, mix['gq'], mix['gk'],
                                         name=f"{tag}_prep")
            grads['a_sinks'][j] = dsinks
            grads['a_q_norm'][j] = _unpad_vec(dgq, A_HEAD_DIM, LAYOUT_A)
            grads['a_k_norm'][j] = _unpad_vec(dgk, A_HEAD_DIM, LAYOUT_A)
            dwqkv = _mm_tn(sv['h'], dqkv, name=f"{tag}_dwqkv")
            nqk = (A_HEADS + A_KV_HEADS) * LANES
            grads['a_w_qkv'][j] = jnp.concatenate(
                [_unpad_cols(dwqkv[:, :nqk], A_HEADS + A_KV_HEADS, A_HEAD_DIM, LAYOUT_A),
                 _unpad_cols(dwqkv[:, nqk:], A_KV_HEADS, A_HEAD_DIM, LAYOUT_V)], axis=1)
            dh_pairs = [(dqkv, mix['wqkv'], 0)]
        elif kind == 1:
            grads['b_w_out'][j] = _mm_tn(mix['yb'], dx, name=f"{tag}_dwout")
            dyb = _mm([(dx, w['b_w_out'][j], 0)], out_dtype=F32, trans_b=True, name=f"{tag}_dyb")
            dproj, dcw = _conv_bwd(mix['proj'], dyb, mix['cw'], name=f"{tag}_conv")
            grads['b_conv_w'][j] = dcw
            grads['b_w_in'][j] = _mm_tn(sv['h'], dproj, name=f"{tag}_dwin")
            dh_pairs = [(dproj, w['b_w_in'][j], 0)]
        else:
            grads['c_w_o'][j] = _unpad_rows(_mm_tn(mix['o'], dx, name=f"{tag}_dwo"), C_HEADS, C_V)
            do = _mm([(dx, mix['wo'], 0)], out_dtype=BF, trans_b=True, name=f"{tag}_do")
            delta = _mla_delta(mix['o'], do, name=f"{tag}_delta")
            dqc, dkc, dvc = _mla_bwd(mix['qc'], mix['kc'], mix['vc'], do, mix['lse'], delta, name=f"{tag}_attn")
            dqraw, dknope, dkr, dgq, dgk = _prep_c2_bwd(mix['qraw'], mix['knope'], mix['dn'], dqc, dkc, cos_c, sin_c,
                                                        mix['gq'], mix['gk'], name=f"{tag}_prep2")
            grads['c_q_norm'][j] = _unpad_vec(dgq, C_QK, LAYOUT_C)
            grads['c_k_norm'][j] = _unpad_vec(dgk, C_QK, LAYOUT_C)
            grads['c_w_q_up'][j] = _unpad_cols(_mm_tn(mix['cqn'], dqraw, name=f"{tag}_dwq"), C_HEADS, C_QK,
                                               LAYOUT_C)
            dwkn = _unpad_cols(_mm_tn(mix['ckvn'], dknope, name=f"{tag}_dwkn"), C_HEADS, C_NOPE, LAYOUT_KN)
            dwv = _unpad_cols(_mm_tn(mix['ckvn'], dvc, name=f"{tag}_dwv"), C_HEADS, C_V, LAYOUT_V)
            grads['c_w_kv_up'][j] = jnp.concatenate(
                [dwkn.reshape(C_KV_RANK, C_HEADS, C_NOPE), dwv.reshape(C_KV_RANK, C_HEADS, C_V)], axis=2).reshape(
                C_KV_RANK, -1)
            dcq = _mm([(dqraw, mix['wq'], 0)], out_dtype=F32, trans_b=True, name=f"{tag}_dcq")
            dckv = _mm([(dknope, mix['wkn'], 0), (dvc, mix['wv'], 0)], out_dtype=F32, trans_b=True,
                       name=f"{tag}_dckv")
            ddn, dgqa, dgkva = _prep_c1_bwd(mix['dn'], dcq, dckv, dkr, mix['gqa'], mix['gkva'], name=f"{tag}_prep1")
            grads['c_q_a_norm'][j] = dgqa[0]
            grads['c_kv_a_norm'][j] = dgkva[0]
            dwdn = _mm_tn(sv['h'], ddn, name=f"{tag}_dwdown")
            nqk = C_Q_RANK + C_KV_RANK
            grads['c_w_down'][j] = jnp.concatenate(
                [dwdn[:, :nqk], _unpad_cols(dwdn[:, nqk:], 1, C_ROPE, LAYOUT_KR)], axis=1)
            dh_pairs = [(ddn, mix['wdn'], 0)]
        dx, dg = _mm_norm_bwd(dh_pairs, sv['x'], w['mix_norm'][i], dx, name=f"{tag}_dh")
        grads['mix_norm'][i] = dg[0]

    return loss_blk, dx, {n: (g if n in BIG else jnp.stack(g)) for n, g in grads.items()}


def _my_place():
    return lax.axis_index("x"), lax.axis_index("y"), lax.axis_index("c")


def _flips(x, y):
    return [(1 - x, y), (x, 1 - y), (1 - x, 1 - y)]


def _gather_weights(big, small):
    half = big.shape[0] // 2
    chunk = half // GATHER_CHUNKS

    def body(big_ref, small_ref, bout_ref, sout_ref, send_b, recv_b, send_f, recv_f, send_s, recv_s, loc):
        x, y, c = _my_place()
        me = 2 * x + y

        def rows(core, ch):
            return pl.ds(pl.multiple_of(core * half + ch * chunk, 16), chunk)

        def ici(k, ch, chip, to):
            return pltpu.make_async_remote_copy(
                src_ref=big_ref.at[rows(c, ch), :], dst_ref=bout_ref.at[chip, rows(c, ch), :],
                send_sem=send_b.at[k * GATHER_CHUNKS + ch], recv_sem=recv_b.at[k * GATHER_CHUNKS + ch],
                device_id=to, device_id_type=MESH)

        def forward(k, ch, chip, core):
            return pltpu.make_async_remote_copy(
                src_ref=bout_ref.at[chip, rows(core, ch), :], dst_ref=bout_ref.at[chip, rows(core, ch), :],
                send_sem=send_f.at[k * GATHER_CHUNKS + ch], recv_sem=recv_f.at[k * GATHER_CHUNKS + ch],
                device_id=(x, y, 1 - c), device_id_type=MESH)

        def small_copy(k, chip, to):
            return pltpu.make_async_remote_copy(src_ref=small_ref, dst_ref=sout_ref.at[chip],
                                                send_sem=send_s.at[k], recv_sem=recv_s.at[k],
                                                device_id=to, device_id_type=MESH)

        chips = _flips(x, y)
        every = [(ch, k, px, py) for ch in range(GATHER_CHUNKS) for k, (px, py) in enumerate(chips)]
        sends = [ici(k, ch, me, (px, py, c)) for ch, k, px, py in every]
        sends += [small_copy(k, me, (px, py, c)) for k, (px, py) in enumerate(chips)]
        for cp in sends:
            cp.start()
        own = (pltpu.make_async_copy(big_ref, bout_ref.at[me], loc.at[0]),
               pltpu.make_async_copy(small_ref, sout_ref.at[me], loc.at[1]))
        for cp in own:
            cp.start()
        passed = []
        for ch, k, px, py in every:
            ici(k, ch, 2 * px + py, (x, y, c)).wait_recv()
            passed.append(forward(k, ch, 2 * px + py, c))
            passed[-1].start()
        for ch, k, px, py in every:
            forward(k, ch, 2 * px + py, 1 - c).wait_recv()
        for k, (px, py) in enumerate(chips):
            small_copy(k, 2 * px + py, (x, y, c)).wait_recv()
        for cp in sends + passed:
            cp.wait_send()
        for cp in own:
            cp.wait()

    hbm = pl.BlockSpec(memory_space=pltpu.HBM)
    sem3 = pltpu.SemaphoreType.DMA((3,))
    semc = pltpu.SemaphoreType.DMA((3 * GATHER_CHUNKS,))
    return pl.pallas_call(
        body, in_specs=[hbm, hbm], out_specs=[hbm, hbm],
        out_shape=[jax.ShapeDtypeStruct((N_CHIPS,) + big.shape, big.dtype),
                   jax.ShapeDtypeStruct((N_CHIPS,) + small.shape, small.dtype)],
        scratch_shapes=[semc, semc, semc, semc, sem3, sem3, pltpu.SemaphoreType.DMA((2,))],
        name="gather_weights")(big, small)


def _exchange_grads(gbig, gsmall):
    def body(gbig_ref, gsmall_ref, got_ref, ssum_ref, sbuf, send_b, recv_b, send_s, recv_s):
        x, y, c = _my_place()
        me = 4 * x + 2 * y + c

        def big_copy(k, src_chip, to):
            return pltpu.make_async_remote_copy(src_ref=gbig_ref.at[src_chip], dst_ref=got_ref.at[k],
                                                send_sem=send_b.at[k], recv_sem=recv_b.at[k],
                                                device_id=to, device_id_type=MESH)

        def small_copy(k, slot, to):
            return pltpu.make_async_remote_copy(src_ref=gsmall_ref, dst_ref=sbuf.at[slot],
                                                send_sem=send_s.at[k], recv_sem=recv_s.at[k],
                                                device_id=to, device_id_type=MESH)

        peers = [(x ^ (k >> 2), y ^ ((k >> 1) & 1), c ^ (k & 1)) for k in range(1, N_DEV)]
        bigs = [big_copy(k, 2 * px + py, (px, py, c)) for k, (px, py) in enumerate(_flips(x, y))]
        smalls = [small_copy(k, me, p) for k, p in enumerate(peers)]
        for cp in bigs + smalls:
            cp.start()
        sbuf[me] = gsmall_ref[...]
        for k, (px, py, pc) in enumerate(peers):
            small_copy(k, 4 * px + 2 * py + pc, (x, y, c)).wait_recv()
        acc = sbuf[0]
        for d in range(1, N_DEV):
            acc = acc + sbuf[d]
        ssum_ref[...] = acc
        for k in range(3):
            big_copy(k, 0, (x, y, c)).wait_recv()
        for cp in bigs + smalls:
            cp.wait_send()

    hbm = pl.BlockSpec(memory_space=pltpu.HBM)
    vmem = pl.BlockSpec(memory_space=pltpu.VMEM)
    return pl.pallas_call(
        body, in_specs=[hbm, vmem], out_specs=[hbm, vmem],
        out_shape=[jax.ShapeDtypeStruct((3,) + gbig.shape[1:], gbig.dtype),
                   jax.ShapeDtypeStruct(gsmall.shape, gsmall.dtype)],
        scratch_shapes=[pltpu.VMEM((N_DEV,) + gsmall.shape, gsmall.dtype),
                        pltpu.SemaphoreType.DMA((3,)), pltpu.SemaphoreType.DMA((3,)),
                        pltpu.SemaphoreType.DMA((N_DEV - 1,)), pltpu.SemaphoreType.DMA((N_DEV - 1,))],
        name="exchange_grads")(gbig, gsmall)


def _sum_shards(chip, mine, got):
    r, cols = mine.shape[1:]
    tr = _tile(r, PACK_TILE_ROWS, 16)

    def body(chip_ref, mine_ref, got_ref, o_ref):
        o_ref[...] = ((mine_ref[0] + got_ref[0].astype(F32)) + got_ref[1].astype(F32)) + got_ref[2].astype(F32)

    grid_spec = pltpu.PrefetchScalarGridSpec(
        num_scalar_prefetch=1, grid=(r // tr,),
        in_specs=[pl.BlockSpec((1, tr, cols), lambda i, chip_ref: (chip_ref[0], i, 0)),
                  pl.BlockSpec((3, tr, cols), lambda i, chip_ref: (0, i, 0))],
        out_specs=pl.BlockSpec((tr, cols), lambda i, chip_ref: (i, 0)))
    return pl.pallas_call(
        body, grid_spec=grid_spec, out_shape=jax.ShapeDtypeStruct((r, cols), F32),
        compiler_params=_cp("parallel"), name="sum_shards")(chip, mine, got)


def _swap_sibling(t, *, name):
    def body(t_ref, got_ref, send_sem, recv_sem):
        x, y, c = _my_place()
        cp = pltpu.make_async_remote_copy(src_ref=t_ref, dst_ref=got_ref, send_sem=send_sem, recv_sem=recv_sem,
                                          device_id=(x, y, 1 - c), device_id_type=MESH)
        cp.start()
        cp.wait()

    hbm = pl.BlockSpec(memory_space=pltpu.HBM)
    return pl.pallas_call(
        body, in_specs=[hbm], out_specs=hbm, out_shape=jax.ShapeDtypeStruct(t.shape, t.dtype),
        scratch_shapes=[pltpu.SemaphoreType.DMA, pltpu.SemaphoreType.DMA], name=name)(t)


def _give_halves(parts, *, name):
    n = len(parts)
    r, cols = parts[0].shape
    half = r // 2

    def body(*refs):
        got_ref, send_sems, recv_sems = refs[n:]
        x, y, c = _my_place()
        theirs = pl.ds(pl.multiple_of((1 - c) * half, 16), half)
        copies = [pltpu.make_async_remote_copy(src_ref=refs[k].at[theirs, :], dst_ref=got_ref.at[k],
                                               send_sem=send_sems.at[k], recv_sem=recv_sems.at[k],
                                               device_id=(x, y, 1 - c), device_id_type=MESH) for k in range(n)]
        for cp in copies:
            cp.start()
        for cp in copies:
            cp.wait()

    hbm = pl.BlockSpec(memory_space=pltpu.HBM)
    return pl.pallas_call(
        body, in_specs=[hbm] * n, out_specs=hbm, out_shape=jax.ShapeDtypeStruct((n, half, cols), parts[0].dtype),
        scratch_shapes=[pltpu.SemaphoreType.DMA((n,)), pltpu.SemaphoreType.DMA((n,))], name=name)(*parts)


def _add_kept(core, parts, got, *, name):
    n, r, cols = got.shape
    tr = _tile(r, PACK_TILE_ROWS, 16)
    nblk = r // tr

    def body(core_ref, *refs):
        b_ref, o_ref, ob_ref = refs[n:]
        for k in range(n):
            sm = refs[k][...] + b_ref[k]
            o_ref[k] = sm
            ob_ref[k] = sm.astype(BF)

    blk = pl.BlockSpec((n, tr, cols), lambda i, core_ref: (0, i, 0))
    kept = pl.BlockSpec((tr, cols), lambda i, core_ref: (core_ref[0] * nblk + i, 0))
    grid_spec = pltpu.PrefetchScalarGridSpec(
        num_scalar_prefetch=1, grid=(nblk,), in_specs=[kept] * n + [blk], out_specs=[blk, blk])
    return pl.pallas_call(
        body, grid_spec=grid_spec,
        out_shape=[jax.ShapeDtypeStruct(got.shape, F32), jax.ShapeDtypeStruct(got.shape, BF)],
        compiler_params=_cp("parallel"), name=name)(core, *parts, got)


def _by_core(c, mine, sibling):
    return jnp.where(c == 0, mine, sibling), jnp.where(c == 0, sibling, mine)


def _adamw(wt, m, v, g, *, name):
    r, cols = wt.shape
    tr = _tile(r, 256, 8)

    def body(w_ref, m_ref, v_ref, g_ref, d_ref, nm_ref, nv_ref):
        gv = g_ref[...]
        nm = ADAM_B1 * m_ref[...] + (1.0 - ADAM_B1) * gv
        nv = ADAM_B2 * v_ref[...] + (1.0 - ADAM_B2) * (gv * gv)
        m_hat = nm / (1.0 - ADAM_B1 ** ADAM_STEP)
        v_hat = nv / (1.0 - ADAM_B2 ** ADAM_STEP)
        d_ref[...] = -ADAM_LR * (m_hat / (jnp.sqrt(v_hat) + ADAM_EPS) + ADAM_WD * w_ref[...])
        nm_ref[...] = nm
        nv_ref[...] = nv

    blk = pl.BlockSpec((tr, cols), lambda i: (i, 0))
    return pl.pallas_call(
        body, grid=(r // tr,), in_specs=[blk] * 4, out_specs=[blk] * 3,
        out_shape=[jax.ShapeDtypeStruct((r, cols), F32)] * 3,
        compiler_params=_cp("parallel"), name=name)(wt, m, v, g)


def _shard_shape(full, axis):
    return tuple(d // N_CHIPS if a == axis else d for a, d in enumerate(full))


def _pack_rows(n):
    rows = -(-n // PACK_COLS)
    return -(-rows // PACK_ROW_MULT) * PACK_ROW_MULT


def _pack(parts, rows, dtype):
    mat = jnp.concatenate([p.astype(dtype).reshape(-1, PACK_COLS) for p in parts], axis=0)
    return jnp.pad(mat, ((0, rows - mat.shape[0]), (0, 0)))


def _join_shards(stacked, axis):
    moved = jnp.moveaxis(stacked, 0, axis)
    shp = moved.shape
    return moved.reshape(shp[:axis] + (shp[axis] * shp[axis + 1],) + shp[axis + 2:])


def _split_shards(full, axis):
    shp = full.shape
    split = full.reshape(shp[:axis] + (N_CHIPS, shp[axis] // N_CHIPS) + shp[axis + 1:])
    return jnp.moveaxis(split, axis, 0)


SMALL_ROWS = 128
SMALL_LAYOUT = [('mix_norm', DEPTH * D_MODEL), ('ffn_norm', DEPTH * D_MODEL), ('b_conv_w', 3 * D_MODEL),
                ('c_q_a_norm', C_Q_RANK), ('c_kv_a_norm', C_KV_RANK), ('a_q_norm', 2 * A_HEAD_DIM),
                ('a_k_norm', 2 * A_HEAD_DIM), ('a_sinks', 2 * A_HEADS), ('c_q_norm', C_QK), ('c_k_norm', C_QK),
                ('loss', 1)]


def _small_offsets():
    offs, row = {}, 0
    for name, n in SMALL_LAYOUT:
        offs[name] = (row * LANES, n)
        row += -(-n // LANES)
    assert row <= SMALL_ROWS
    return offs


def kernel(x, positions, mix_norm, ffn_norm, a_w_qkv, a_q_norm, a_k_norm, a_sinks, a_w_o, b_w_in, b_conv_w, b_w_out, c_w_down, c_q_a_norm, c_kv_a_norm, c_w_q_up, c_w_kv_up, c_q_norm, c_k_norm, c_w_o, f_w_gate_up, f_w_down, loss_target, m_mix_norm, m_ffn_norm, m_a_w_qkv, m_a_q_norm, m_a_k_norm, m_a_sinks, m_a_w_o, m_b_w_in, m_b_conv_w, m_b_w_out, m_c_w_down, m_c_q_a_norm, m_c_kv_a_norm, m_c_w_q_up, m_c_w_kv_up, m_c_q_norm, m_c_k_norm, m_c_w_o, m_f_w_gate_up, m_f_w_down, v_mix_norm, v_ffn_norm, v_a_w_qkv, v_a_q_norm, v_a_k_norm, v_a_sinks, v_a_w_o, v_b_w_in, v_b_conv_w, v_b_w_out, v_c_w_down, v_c_q_a_norm, v_c_kv_a_norm, v_c_w_q_up, v_c_w_kv_up, v_c_q_norm, v_c_k_norm, v_c_w_o, v_f_w_gate_up, v_f_w_down):
    args = dict(locals())
    wshard = {n: args[n] for n in WEIGHTS}
    sharded = {**BIG, **SMALL_SHARDED}
    chip = 2 * lax.axis_index("x") + lax.axis_index("y")

    n_big = sum(wshard[n].size for n in BIG)
    rows = _pack_rows(n_big)
    big = _pack([wshard[n] for n in BIG], rows, BF)
    small = jnp.concatenate([wshard[n].reshape(-1) for n in SMALL_SHARDED])
    small = jnp.pad(small, (0, 8 * LANES - small.shape[0])).reshape(8, LANES)
    core = lax.axis_index("c")
    big_all, small_all = _gather_weights(big, small)
    big_all = big_all.reshape(N_CHIPS, -1)
    small_all = small_all.reshape(N_CHIPS, -1)
    w = {}
    off = 0
    for n, axis in BIG.items():
        shp = wshard[n].shape
        per = shp[1] * shp[2]
        w[n] = [_join_shards(big_all[:, off + l * per:off + (l + 1) * per].reshape((N_CHIPS,) + shp[1:]), axis - 1)
                for l in range(shp[0])]
        off += wshard[n].size
    off = 0
    for n, axis in SMALL_SHARDED.items():
        sz = wshard[n].size
        w[n] = _join_shards(small_all[:, off:off + sz].reshape((N_CHIPS,) + wshard[n].shape), axis)
        off += sz
    for n in WEIGHTS:
        if n not in sharded:
            w[n] = wshard[n]

    loss_blk, grad_x, g = _local_step(x[0], positions[0], w, loss_target[0])

    def shard_of(n, item, k):
        if isinstance(item, tuple):
            item = item[k // 2]
            k = k % 2
            width = item.shape[1] // 2
        else:
            width = item.shape[BIG[n] - 1] // N_CHIPS
        return item[:, k * width:(k + 1) * width] if BIG[n] == 2 else item[k * width:(k + 1) * width]

    gparts = [_pack([shard_of(n, item, k) for n in BIG for item in g[n]], rows, F32) for k in range(N_CHIPS)]
    offs = _small_offsets()
    smalls = {**{n: g[n] for n, _ in SMALL_LAYOUT if n != 'loss'}, 'loss': loss_blk[0:1, 0:1]}
    gsmall = jnp.concatenate(
        [jnp.pad(smalls[n].reshape(-1), (0, -cnt % LANES)) for n, cnt in SMALL_LAYOUT])
    gsmall = jnp.pad(gsmall, (0, SMALL_ROWS * LANES - gsmall.shape[0])).reshape(SMALL_ROWS, LANES)
    from_sibling = _give_halves(gparts, name="swap_grads")
    chip_sum, chip_sum_bf = _add_kept(core.reshape(1).astype(jnp.int32), gparts, from_sibling, name="add_sibling")
    got, ssum = _exchange_grads(chip_sum_bf, gsmall)
    part = _sum_shards(chip.reshape(1).astype(jnp.int32), chip_sum, got)
    part_sib = _swap_sibling(part, name="swap_sums")
    gflat = jnp.concatenate(_by_core(core, part, part_sib), axis=0).reshape(-1)

    ssum = ssum.reshape(-1)
    outs = {}
    off = 0
    for n in WEIGHTS:
        wt = wshard[n]
        if n in BIG:
            grad = gflat[off:off + wt.size]
            off += wt.size
        else:
            o0, cnt = offs[n]
            grad = ssum[o0:o0 + cnt].reshape(g[n].shape)
            if n in SMALL_SHARDED:
                grad = lax.dynamic_index_in_dim(_split_shards(grad, SMALL_SHARDED[n]), chip, 0, keepdims=False)
        shape2 = (-1, wt.shape[-1])
        grad = grad.reshape(shape2)
        res = _adamw(wt.reshape(shape2), args['m_' + n].reshape(shape2), args['v_' + n].reshape(shape2), grad,
                     name=f"adamw_{n}")
        outs[n] = [r.reshape(wt.shape) for r in [grad] + list(res)]
    loss = ssum[offs['loss'][0]]
    return (loss, grad_x[None], *[outs[n][0] for n in WEIGHTS], *[outs[n][1] for n in WEIGHTS],
            *[outs[n][2] for n in WEIGHTS], *[outs[n][3] for n in WEIGHTS])
```

```python
import functools

import jax
import jax.numpy as jnp
from jax import lax
from jax.experimental import pallas as pl
from jax.experimental.pallas import tpu as pltpu

F32 = jnp.float32
BF = jnp.bfloat16

D_MODEL = 1024
DEPTH = 4
N_MIXERS = 3
ROPE_THETA = 500000.0
EPS = 1e-6
BLOCK = 128
LANES = 128

A_HEADS, A_KV_HEADS, A_HEAD_DIM = 16, 4, 64
A_ROT_DIM = A_HEAD_DIM // 4
A_GROUP = A_HEADS // A_KV_HEADS
C_HEADS, C_NOPE, C_ROPE, C_V, C_Q_RANK, C_KV_RANK = 16, 64, 32, 64, 384, 256
C_QK = C_NOPE + C_ROPE
D_FF = 2816

ADAM_LR, ADAM_B1, ADAM_B2, ADAM_EPS, ADAM_WD, ADAM_STEP = 0.001, 0.9, 0.999, 1e-08, 0.01, 10

N_CHIPS = 4
N_DEV = 8
MESH = pl.DeviceIdType.MESH

VMEM_LIMIT = 56 * 1024 * 1024
ROW_TILE = 512
PREP_TILE = 256
ATTN_TILE = 512
FWD_WIDE = 4
BWD_WIDE = 5
SWA_BLOCKS = 16
NEG = -1e30

WEIGHTS = ['mix_norm', 'ffn_norm', 'a_w_qkv', 'a_q_norm', 'a_k_norm', 'a_sinks', 'a_w_o', 'b_w_in', 'b_conv_w',
           'b_w_out', 'c_w_down', 'c_q_a_norm', 'c_kv_a_norm', 'c_w_q_up', 'c_w_kv_up', 'c_q_norm', 'c_k_norm',
           'c_w_o', 'f_w_gate_up', 'f_w_down']
BIG = {'a_w_qkv': 2, 'a_w_o': 1, 'b_w_in': 2, 'b_w_out': 1, 'c_w_down': 1, 'c_w_q_up': 2, 'c_w_kv_up': 2,
       'c_w_o': 1, 'f_w_gate_up': 2, 'f_w_down': 1}
SMALL_SHARDED = {'b_conv_w': 2, 'c_q_a_norm': 1, 'c_kv_a_norm': 1}
PACK_COLS = 256
PACK_ROW_MULT = 32
PACK_TILE_ROWS = 1024
GATHER_CHUNKS = 3


def _cp(*sem):
    return pltpu.CompilerParams(dimension_semantics=sem, vmem_limit_bytes=VMEM_LIMIT)


def _tile(n, target, mult):
    if n <= target:
        return n
    t = (target // mult) * mult
    while t >= mult:
        if n % t == 0:
            return t
        t -= mult
    raise ValueError(f"no tile for {n}")


def _dot(a, b):
    return lax.dot_general(a, b, (((1,), (0,)), ((), ())), preferred_element_type=F32)


def _dot_nt(a, b):
    return lax.dot_general(a, b, (((1,), (1,)), ((), ())), preferred_element_type=F32)


def _dot_tn(a, b):
    return lax.dot_general(a, b, (((0,), (0,)), ((), ())), preferred_element_type=F32)


def _mm(pairs, *, out_dtype, name, res=None, trans_b=False, norm_g=None):
    a0, b0, _ = pairs[0]
    m = a0.shape[0]
    n = b0.shape[0] if trans_b else b0.shape[1]
    tm = _tile(m, 2 * ROW_TILE if sum(a.shape[1] for a, _, _ in pairs) <= 1024 else ROW_TILE, 8)
    tn = _tile(n, 1024, 128)
    n_pairs = len(pairs)
    has_res = res is not None
    has_norm = norm_g is not None
    assert not has_norm or tn == n

    def body(*refs):
        acc = None
        for p in range(n_pairs):
            a = refs[2 * p][...].astype(BF)
            b = refs[2 * p + 1][...].astype(BF)
            d = _dot_nt(a, b) if trans_b else _dot(a, b)
            acc = d if acc is None else acc + d
        nxt = 2 * n_pairs
        if has_res:
            acc = acc + refs[nxt][...]
            nxt += 1
        if has_norm:
            y = acc * lax.rsqrt(jnp.mean(acc * acc, axis=-1, keepdims=True) + EPS)
            refs[-1][...] = (y * refs[nxt][...]).astype(BF)
            refs[-2][...] = acc.astype(out_dtype)
        else:
            refs[-1][...] = acc.astype(out_dtype)

    in_specs, args = [], []
    for a, b, kblk in pairs:
        k = a.shape[1]
        in_specs.append(pl.BlockSpec((tm, k), lambda j, i: (i, 0)))
        if trans_b:
            in_specs.append(pl.BlockSpec((tn, k), functools.partial(lambda j, i, kb: (j, kb), kb=kblk)))
        else:
            in_specs.append(pl.BlockSpec((k, tn), lambda j, i: (0, j)))
        args += [a, b]
    if has_res:
        in_specs.append(pl.BlockSpec((tm, tn), lambda j, i: (i, j)))
        args.append(res)
    out_spec = pl.BlockSpec((tm, tn), lambda j, i: (i, j))
    out_specs, out_shape = out_spec, jax.ShapeDtypeStruct((m, n), out_dtype)
    if has_norm:
        in_specs.append(pl.BlockSpec((1, n), lambda j, i: (0, 0)))
        args.append(norm_g.reshape(1, n))
        out_specs, out_shape = [out_spec, out_spec], [out_shape, jax.ShapeDtypeStruct((m, n), BF)]
    return pl.pallas_call(
        body, grid=(n // tn, m // tm), in_specs=in_specs, out_specs=out_specs, out_shape=out_shape,
        compiler_params=_cp("parallel", "parallel"), name=name)(*args)


def _mm_norm_bwd(pairs, x, g, dres, *, name):
    m, d = x.shape
    tm = _tile(m, ROW_TILE, 8)
    n_pairs = len(pairs)

    def body(*refs):
        x_ref, g_ref, dres_ref, dx_ref, dg_ref = refs[2 * n_pairs:]

        @pl.when(pl.program_id(0) == 0)
        def _():
            dg_ref[...] = jnp.zeros_like(dg_ref)

        dh = None
        for p in range(n_pairs):
            dpart = _dot_nt(refs[2 * p][...].astype(BF), refs[2 * p + 1][...].astype(BF))
            dh = dpart if dh is None else dh + dpart
        dx, xhat = _rms_bwd_math(x_ref[...], g_ref[...], dh, d)
        dx_ref[...] = dres_ref[...] + dx
        dg_ref[0:1, :] += jnp.sum(dh * xhat, axis=0, keepdims=True)

    in_specs, args = [], []
    for a, b, kblk in pairs:
        k = a.shape[1]
        in_specs.append(pl.BlockSpec((tm, k), lambda i: (i, 0)))
        in_specs.append(pl.BlockSpec((d, k), functools.partial(lambda i, kb: (0, kb), kb=kblk)))
        args += [a, b]
    row = pl.BlockSpec((tm, d), lambda i: (i, 0))
    in_specs += [row, pl.BlockSpec((1, d), lambda i: (0, 0)), row]
    dx, dg = pl.pallas_call(
        body, grid=(m // tm,), in_specs=in_specs,
        out_specs=[row, pl.BlockSpec((8, d), lambda i: (0, 0))],
        out_shape=[jax.ShapeDtypeStruct((m, d), F32), jax.ShapeDtypeStruct((8, d), F32)],
        compiler_params=_cp("arbitrary"), name=name)(*args, x, g.reshape(1, d), dres)
    return dx, dg[0:1]


def _mm_tn(a, b, *, name):
    m, k = a.shape
    n = b.shape[1]
    tm = _tile(m, 2 * ROW_TILE, 8)
    tk = _tile(k, 1408, 128)
    tn = _tile(n, 1408, 128)

    def body(a_ref, b_ref, o_ref):
        @pl.when(pl.program_id(2) == 0)
        def _():
            o_ref[...] = jnp.zeros_like(o_ref)

        o_ref[...] += _dot_tn(a_ref[...].astype(BF), b_ref[...].astype(BF))

    return pl.pallas_call(
        body, grid=(k // tk, n // tn, m // tm),
        in_specs=[pl.BlockSpec((tm, tk), lambda kk, j, i: (i, kk)),
                  pl.BlockSpec((tm, tn), lambda kk, j, i: (i, j))],
        out_specs=pl.BlockSpec((tk, tn), lambda kk, j, i: (kk, j)),
        out_shape=jax.ShapeDtypeStruct((k, n), F32),
        compiler_params=_cp("parallel", "parallel", "arbitrary"), name=name)(a, b)


def _rmsnorm_fwd(x, g, *, name):
    s, d = x.shape
    tm = _tile(s, ROW_TILE, 8)

    def body(x_ref, g_ref, h_ref):
        xv = x_ref[...]
        y = xv * lax.rsqrt(jnp.mean(xv * xv, axis=-1, keepdims=True) + EPS)
        h_ref[...] = (y * g_ref[...]).astype(BF)

    return pl.pallas_call(
        body, grid=(s // tm,),
        in_specs=[pl.BlockSpec((tm, d), lambda i: (i, 0)), pl.BlockSpec((1, d), lambda i: (0, 0))],
        out_specs=pl.BlockSpec((tm, d), lambda i: (i, 0)),
        out_shape=jax.ShapeDtypeStruct((s, d), BF),
        compiler_params=_cp("parallel"), name=name)(x, g.reshape(1, d))


def _row_sum(v):
    return jnp.sum(v, axis=-1, keepdims=True)


def _slot_sum(v):
    ones = jnp.ones((LANES, LANES), BF)
    hi = v.astype(BF)
    lo = (v - hi.astype(F32)).astype(BF)
    return _dot(hi, ones) + _dot(lo, ones)


def _rms_bwd_math(xv, g, dh, n, row_sum=_row_sum):
    rstd = lax.rsqrt(row_sum(xv * xv) * (1.0 / n) + EPS)
    xhat = xv * rstd
    dxh = dh * g
    dx = rstd * (dxh - xhat * (row_sum(dxh * xhat) * (1.0 / n)))
    return dx, xhat


def _loss_fwd_bwd(y, target, *, name):
    s, d = y.shape
    tm = _tile(s, ROW_TILE, 8)

    def body(y_ref, t_ref, loss_ref, dy_ref):
        @pl.when(pl.program_id(0) == 0)
        def _():
            loss_ref[...] = jnp.zeros_like(loss_ref)

        err = y_ref[...] - t_ref[...]
        dy_ref[...] = err * (1.0 / d)
        loss_ref[...] += 0.5 * jnp.sum(jnp.sum(err * err, axis=-1, keepdims=True) * (1.0 / d))

    row = pl.BlockSpec((tm, d), lambda i: (i, 0))
    return pl.pallas_call(
        body, grid=(s // tm,), in_specs=[row, row],
        out_specs=[pl.BlockSpec((8, LANES), lambda i: (0, 0)), row],
        out_shape=[jax.ShapeDtypeStruct((8, LANES), F32), jax.ShapeDtypeStruct((s, d), F32)],
        compiler_params=_cp("arbitrary"), name=name)(y, target)


HALF = LANES // 2
A_HR, C_HR = A_ROT_DIM // 2, C_ROPE // 2
A_X1, C_X1 = 0, 32
LAYOUT_A = [(A_X1, 0, A_HR), (A_HR, A_ROT_DIM, A_HEAD_DIM - A_ROT_DIM), (HALF + A_X1, A_HR, A_HR)]
LAYOUT_C = [(0, 0, 32), (C_X1, C_NOPE, C_HR), (HALF, 32, 32), (HALF + C_X1, C_NOPE + C_HR, C_HR)]
LAYOUT_KN = [(0, 0, 32), (HALF, 32, 32)]
LAYOUT_KR = [(C_X1, 0, C_HR), (HALF + C_X1, C_HR, C_HR)]
LAYOUT_V = [(0, 0, C_V)]


def _rope_rows():
    lane = jnp.arange(LANES)
    fa = ROPE_THETA ** (-jnp.arange(0, A_ROT_DIM, 2, dtype=F32) / A_ROT_DIM)
    fc = ROPE_THETA ** (-jnp.arange(0, C_ROPE, 2, dtype=F32) / C_ROPE)

    def rows(f, x1, hr):
        first = (lane >= x1) & (lane < x1 + hr)
        second = (lane >= HALF + x1) & (lane < HALF + x1 + hr)
        freq = jnp.where(first | second, f[(lane - x1) % HALF % hr], 0.0)
        return freq, jnp.where(first, -1.0, jnp.where(second, 1.0, 0.0))

    freq_a, sign_a = rows(fa, A_X1, A_HR)
    freq_c, sign_c = rows(fc, C_X1, C_HR)
    return jnp.stack([freq_a, sign_a, freq_c, sign_c] + [jnp.zeros(LANES)] * 4).astype(F32)


def _rope_tables(pos_col, *, name):
    s = pos_col.shape[0]
    tm = _tile(s, ROW_TILE, 8)

    def body(pos_ref, rows_ref, ca_ref, sa_ref, cc_ref, sc_ref):
        p = pos_ref[...].astype(F32)
        ang_a = p * rows_ref[0:1, :]
        ang_c = p * rows_ref[2:3, :]
        ca_ref[...] = jnp.cos(ang_a)
        sa_ref[...] = jnp.sin(ang_a) * rows_ref[1:2, :]
        cc_ref[...] = jnp.cos(ang_c)
        sc_ref[...] = jnp.sin(ang_c) * rows_ref[3:4, :]

    tab = pl.BlockSpec((tm, LANES), lambda i: (i, 0))
    return pl.pallas_call(
        body, grid=(s // tm,),
        in_specs=[pl.BlockSpec((tm, 1), lambda i: (i, 0)), pl.BlockSpec((8, LANES), lambda i: (0, 0))],
        out_specs=[tab] * 4, out_shape=[jax.ShapeDtypeStruct((s, LANES), F32)] * 4,
        compiler_params=_cp("parallel"), name=name)(pos_col, _rope_rows())


def _rope(xv, cos, sin):
    return xv * cos + pltpu.roll(xv, HALF, 1) * sin


def _slot(ref, t):
    return ref[:, t * LANES:(t + 1) * LANES]


def _head_norm_rope(xv, g, cos, sin, n):
    y = xv * lax.rsqrt(_slot_sum(xv * xv) * (1.0 / n) + EPS) * g
    return _rope(y, cos, sin)


def _head_norm_rope_bwd(xv, g, dy, cos, sin, n):
    dyn = _rope(dy, cos, -sin)
    dx, xhat = _rms_bwd_math(xv, g, dyn, n, _slot_sum)
    return dx, jnp.sum(dyn * xhat, axis=0, keepdims=True)


def _prep_a_fwd(qkv, cos, sin, gq, gk, *, name):
    s = qkv.shape[0]
    tm = _tile(s, PREP_TILE, 8)
    nq, nkv = A_HEADS, A_KV_HEADS

    def body(x_ref, c_ref, s_ref, gq_ref, gk_ref, q_ref, k_ref, v_ref):
        cv, sv = c_ref[...], s_ref[...]
        for t in range(nq):
            q_ref[:, t * LANES:(t + 1) * LANES] = _head_norm_rope(
                _slot(x_ref, t), gq_ref[...], cv, sv, A_HEAD_DIM).astype(BF)
        for t in range(nkv):
            k_ref[:, t * LANES:(t + 1) * LANES] = _head_norm_rope(
                _slot(x_ref, nq + t), gk_ref[...], cv, sv, A_HEAD_DIM).astype(BF)
        v_ref[...] = x_ref[:, (nq + nkv) * LANES:].astype(BF)

    def rows(w):
        return pl.BlockSpec((tm, w), lambda i: (i, 0))

    vec = pl.BlockSpec((1, LANES), lambda i: (0, 0))
    return pl.pallas_call(
        body, grid=(s // tm,),
        in_specs=[rows(qkv.shape[1]), rows(LANES), rows(LANES), vec, vec],
        out_specs=[rows(nq * LANES), rows(nkv * LANES), rows(nkv * LANES)],
        out_shape=[jax.ShapeDtypeStruct((s, nq * LANES), BF), jax.ShapeDtypeStruct((s, nkv * LANES), BF),
                   jax.ShapeDtypeStruct((s, nkv * LANES), BF)],
        compiler_params=_cp("parallel"), name=name)(qkv, cos, sin, gq, gk)


def _prep_a_bwd(qkv, dq, dk, dv, cos, sin, gq, gk, *, name):
    s = qkv.shape[0]
    tm = _tile(s, PREP_TILE, 8)
    nq, nkv = A_HEADS, A_KV_HEADS

    def body(x_ref, dq_ref, dk_ref, dv_ref, c_ref, s_ref, gq_ref, gk_ref, dx_ref, dg_ref):
        @pl.when(pl.program_id(0) == 0)
        def _():
            dg_ref[...] = jnp.zeros_like(dg_ref)

        cv, sv = c_ref[...], s_ref[...]
        dgq = jnp.zeros((1, LANES), F32)
        dgk = jnp.zeros((1, LANES), F32)
        for t in range(nq):
            dx, dg = _head_norm_rope_bwd(_slot(x_ref, t), gq_ref[...], _slot(dq_ref, t), cv, sv,
                                         A_HEAD_DIM)
            dx_ref[:, t * LANES:(t + 1) * LANES] = dx.astype(BF)
            dgq = dgq + dg
        for t in range(nkv):
            dx, dg = _head_norm_rope_bwd(_slot(x_ref, nq + t), gk_ref[...], _slot(dk_ref, t), cv, sv,
                                         A_HEAD_DIM)
            dx_ref[:, (nq + t) * LANES:(nq + t + 1) * LANES] = dx.astype(BF)
            dgk = dgk + dg
        dx_ref[:, (nq + nkv) * LANES:] = dv_ref[...].astype(BF)
        dg_ref[0:1, :] += dgq
        dg_ref[1:2, :] += dgk

    def rows(w):
        return pl.BlockSpec((tm, w), lambda i: (i, 0))

    vec = pl.BlockSpec((1, LANES), lambda i: (0, 0))
    dx, dg = pl.pallas_call(
        body, grid=(s // tm,),
        in_specs=[rows(qkv.shape[1]), rows(nq * LANES), rows(nkv * LANES), rows(nkv * LANES), rows(LANES),
                  rows(LANES), vec, vec],
        out_specs=[rows(qkv.shape[1]), pl.BlockSpec((8, LANES), lambda i: (0, 0))],
        out_shape=[jax.ShapeDtypeStruct(qkv.shape, BF), jax.ShapeDtypeStruct((8, LANES), F32)],
        compiler_params=_cp("arbitrary"), name=name)(qkv, dq, dk, dv, cos, sin, gq, gk)
    return dx, dg[0:1], dg[1:2]


def _prep_c1_fwd(dn, gq, gkv, *, name):
    s = dn.shape[0]
    tm = _tile(s, ROW_TILE, 8)

    def body(x_ref, gq_ref, gkv_ref, cq_ref, ckv_ref):
        for lo, n, g_ref, o_ref in ((0, C_Q_RANK, gq_ref, cq_ref), (C_Q_RANK, C_KV_RANK, gkv_ref, ckv_ref)):
            xv = x_ref[:, lo:lo + n]
            y = xv * lax.rsqrt(jnp.mean(xv * xv, axis=-1, keepdims=True) + EPS)
            o_ref[...] = (y * g_ref[...]).astype(BF)

    def rows(w):
        return pl.BlockSpec((tm, w), lambda i: (i, 0))

    return pl.pallas_call(
        body, grid=(s // tm,),
        in_specs=[rows(dn.shape[1]), pl.BlockSpec((1, C_Q_RANK), lambda i: (0, 0)),
                  pl.BlockSpec((1, C_KV_RANK), lambda i: (0, 0))],
        out_specs=[rows(C_Q_RANK), rows(C_KV_RANK)],
        out_shape=[jax.ShapeDtypeStruct((s, C_Q_RANK), BF), jax.ShapeDtypeStruct((s, C_KV_RANK), BF)],
        compiler_params=_cp("parallel"), name=name)(dn, gq, gkv)


def _prep_c1_bwd(dn, dcq, dckv, dkr, gq, gkv, *, name):
    s = dn.shape[0]
    tm = _tile(s, ROW_TILE, 8)

    def body(x_ref, dcq_ref, dckv_ref, dkr_ref, gq_ref, gkv_ref, dx_ref, dgq_ref, dgkv_ref):
        @pl.when(pl.program_id(0) == 0)
        def _():
            dgq_ref[...] = jnp.zeros_like(dgq_ref)
            dgkv_ref[...] = jnp.zeros_like(dgkv_ref)

        for lo, n, g_ref, d_ref, dg_ref in ((0, C_Q_RANK, gq_ref, dcq_ref, dgq_ref),
                                            (C_Q_RANK, C_KV_RANK, gkv_ref, dckv_ref, dgkv_ref)):
            dv = d_ref[...]
            dx, xhat = _rms_bwd_math(x_ref[:, lo:lo + n], g_ref[...], dv, n)
            dx_ref[:, lo:lo + n] = dx.astype(BF)
            dg_ref[0:1, :] += jnp.sum(dv * xhat, axis=0, keepdims=True)
        dx_ref[:, C_Q_RANK + C_KV_RANK:] = dkr_ref[...].astype(BF)

    def rows(w):
        return pl.BlockSpec((tm, w), lambda i: (i, 0))

    def vec(w, r=1):
        return pl.BlockSpec((r, w), lambda i: (0, 0))

    dx, dgq, dgkv = pl.pallas_call(
        body, grid=(s // tm,),
        in_specs=[rows(dn.shape[1]), rows(C_Q_RANK), rows(C_KV_RANK), rows(LANES), vec(C_Q_RANK), vec(C_KV_RANK)],
        out_specs=[rows(dn.shape[1]), vec(C_Q_RANK, 8), vec(C_KV_RANK, 8)],
        out_shape=[jax.ShapeDtypeStruct(dn.shape, BF), jax.ShapeDtypeStruct((8, C_Q_RANK), F32),
                   jax.ShapeDtypeStruct((8, C_KV_RANK), F32)],
        compiler_params=_cp("arbitrary"), name=name)(dn, dcq, dckv, dkr, gq, gkv)
    return dx, dgq[0:1], dgkv[0:1]


def _prep_c2_fwd(qraw, knope, dn, cos, sin, gq, gk, *, name):
    s = qraw.shape[0]
    tm = _tile(s, PREP_TILE, 8)
    kr_blk = dn.shape[1] // LANES - 1

    def body(q_ref, kn_ref, kr_ref, c_ref, s_ref, gq_ref, gk_ref, qo_ref, ko_ref):
        cv, sv, kr = c_ref[...], s_ref[...], kr_ref[...]
        for t in range(C_HEADS):
            qo_ref[:, t * LANES:(t + 1) * LANES] = _head_norm_rope(
                _slot(q_ref, t), gq_ref[...], cv, sv, C_QK).astype(BF)
            ko_ref[:, t * LANES:(t + 1) * LANES] = _head_norm_rope(
                _slot(kn_ref, t) + kr, gk_ref[...], cv, sv, C_QK).astype(BF)

    def rows(w, blk=0):
        return pl.BlockSpec((tm, w), lambda i: (i, blk))

    vec = pl.BlockSpec((1, LANES), lambda i: (0, 0))
    w = C_HEADS * LANES
    return pl.pallas_call(
        body, grid=(s // tm,),
        in_specs=[rows(w), rows(w), rows(LANES, kr_blk), rows(LANES), rows(LANES), vec, vec],
        out_specs=[rows(w), rows(w)],
        out_shape=[jax.ShapeDtypeStruct((s, w), BF)] * 2,
        compiler_params=_cp("parallel"), name=name)(qraw, knope, dn, cos, sin, gq, gk)


def _prep_c2_bwd(qraw, knope, dn, dq, dk, cos, sin, gq, gk, *, name):
    s = qraw.shape[0]
    tm = _tile(s, PREP_TILE, 8)
    kr_blk = dn.shape[1] // LANES - 1

    def body(q_ref, kn_ref, kr_ref, dq_ref, dk_ref, c_ref, s_ref, gq_ref, gk_ref,
             dqo_ref, dkno_ref, dkr_ref, dg_ref):
        @pl.when(pl.program_id(0) == 0)
        def _():
            dg_ref[...] = jnp.zeros_like(dg_ref)

        cv, sv, kr = c_ref[...], s_ref[...], kr_ref[...]
        dgq = jnp.zeros((1, LANES), F32)
        dgk = jnp.zeros((1, LANES), F32)
        dkr = jnp.zeros((tm, LANES), F32)
        for t in range(C_HEADS):
            dx, dg = _head_norm_rope_bwd(_slot(q_ref, t), gq_ref[...], _slot(dq_ref, t), cv, sv,
                                         C_QK)
            dqo_ref[:, t * LANES:(t + 1) * LANES] = dx.astype(BF)
            dgq = dgq + dg
            dx, dg = _head_norm_rope_bwd(_slot(kn_ref, t) + kr, gk_ref[...], _slot(dk_ref, t), cv, sv,
                                         C_QK)
            dkno_ref[:, t * LANES:(t + 1) * LANES] = dx.astype(BF)
            dkr = dkr + dx
            dgk = dgk + dg
        dkr_ref[...] = dkr
        dg_ref[0:1, :] += dgq
        dg_ref[1:2, :] += dgk

    def rows(w, blk=0):
        return pl.BlockSpec((tm, w), lambda i: (i, blk))

    vec = pl.BlockSpec((1, LANES), lambda i: (0, 0))
    w = C_HEADS * LANES
    dqo, dkno, dkr, dg = pl.pallas_call(
        body, grid=(s // tm,),
        in_specs=[rows(w), rows(w), rows(LANES, kr_blk), rows(w), rows(w), rows(LANES), rows(LANES), vec, vec],
        out_specs=[rows(w), rows(w), rows(LANES), pl.BlockSpec((8, LANES), lambda i: (0, 0))],
        out_shape=[jax.ShapeDtypeStruct((s, w), BF), jax.ShapeDtypeStruct((s, w), BF),
                   jax.ShapeDtypeStruct((s, LANES), F32), jax.ShapeDtypeStruct((8, LANES), F32)],
        compiler_params=_cp("arbitrary"), name=name)(qraw, knope, dn, dq, dk, cos, sin, gq, gk)
    return dqo, dkno, dkr, dg[0:1], dg[1:2]


def _lane_pick(mat, g):
    lane = lax.broadcasted_iota(jnp.int32, mat.shape, 1)
    return jnp.sum(jnp.where(lane == g, mat, 0.0), axis=-1, keepdims=True)


def _swa_fwd(q, k, v, sinks, *, name):
    s = q.shape[0]
    nb = min(SWA_BLOCKS, s // BLOCK)
    qt = nb * BLOCK
    scale = A_HEAD_DIM ** -0.5
    gw = A_GROUP * LANES

    def body(sink_ref, q_ref, kc_ref, kp_ref, vc_ref, vp_ref, o_ref, lse_ref):
        kvh, n = pl.program_id(0), pl.program_id(1)
        rows = A_GROUP * BLOCK
        kall = jnp.concatenate([kp_ref[...], kc_ref[...]], axis=0)
        vall = jnp.concatenate([vp_ref[...], vc_ref[...]], axis=0)
        r = lax.broadcasted_iota(jnp.int32, (rows, 2 * BLOCK), 0) & (BLOCK - 1)
        c = lax.broadcasted_iota(jnp.int32, (rows, 2 * BLOCK), 1)
        cur_ok = (c >= BLOCK) & (c - BLOCK <= r)
        prev_ok = (c < BLOCK) & (c > r)
        head = lax.broadcasted_iota(jnp.int32, (rows, 1), 0) >> 7
        sink = jnp.zeros((rows, 1), F32)
        for g in range(A_GROUP):
            sink = jnp.where(head == g, sink_ref[kvh * A_GROUP + g], sink)
        lane = lax.broadcasted_iota(jnp.int32, (BLOCK, LANES), 1)
        for b in range(nb):
            blk = slice(b * BLOCK, (b + 1) * BLOCK)
            qs = jnp.concatenate([q_ref[blk, g * LANES:(g + 1) * LANES] for g in range(A_GROUP)], axis=0)
            k2 = kall[b * BLOCK:(b + 2) * BLOCK]
            v2 = vall[b * BLOCK:(b + 2) * BLOCK]
            ok = (cur_ok | prev_ok) if b > 0 else (cur_ok | (prev_ok & (n > 0)))
            sc = jnp.where(ok, _dot_nt(qs, k2) * scale, NEG)
            m = jnp.maximum(jnp.max(sc, axis=-1, keepdims=True), sink)
            p = jnp.exp(sc - m)
            l = jnp.sum(p, axis=-1, keepdims=True) + jnp.exp(sink - m)
            o = _dot((p * (1.0 / l)).astype(BF), v2)
            lse = m + jnp.log(l)
            lse_mat = jnp.zeros((BLOCK, LANES), F32)
            for g in range(A_GROUP):
                o_ref[blk, g * LANES:(g + 1) * LANES] = o[g * BLOCK:(g + 1) * BLOCK].astype(BF)
                lse_mat = jnp.where(lane == g, lse[g * BLOCK:(g + 1) * BLOCK], lse_mat)
            lse_ref[0, blk, :] = lse_mat

    cur = pl.BlockSpec((qt, LANES), lambda h, n: (n, h))
    prev = pl.BlockSpec((BLOCK, LANES), lambda h, n: (jnp.maximum(n * nb - 1, 0), h))
    return pl.pallas_call(
        body, grid=(A_KV_HEADS, s // qt),
        in_specs=[pl.BlockSpec(memory_space=pltpu.SMEM), pl.BlockSpec((qt, gw), lambda h, n: (n, h)),
                  cur, prev, cur, prev],
        out_specs=[pl.BlockSpec((qt, gw), lambda h, n: (n, h)),
                   pl.BlockSpec((1, qt, LANES), lambda h, n: (h, n, 0))],
        out_shape=[jax.ShapeDtypeStruct(q.shape, BF), jax.ShapeDtypeStruct((A_KV_HEADS, s, LANES), F32)],
        compiler_params=_cp("parallel", "parallel"), name=name)(sinks, q, k, k, v, v)


def _swa_bwd(q, k, v, o, do, lse, sinks, *, name):
    s = q.shape[0]
    nblk = s // BLOCK
    nb = min(SWA_BLOCKS, nblk)
    qt = nb * BLOCK
    nsteps = s // qt
    scale = A_HEAD_DIM ** -0.5
    gw = A_GROUP * LANES

    def body(sink_ref, qc_ref, qn_ref, k_ref, v_ref, oc_ref, on_ref, doc_ref, don_ref, lc_ref, ln_ref,
             dq_ref, dk_ref, dv_ref, dsink_ref, carry):
        kvh, n = pl.program_id(0), pl.program_id(1)

        @pl.when(n == 0)
        def _():
            carry[...] = jnp.zeros_like(carry)
            dsink_ref[...] = jnp.zeros_like(dsink_ref)

        half = A_GROUP * BLOCK
        rows = 2 * half
        qall = jnp.concatenate([qc_ref[...], qn_ref[...]], axis=0)
        oall = jnp.concatenate([oc_ref[...], on_ref[...]], axis=0)
        doall = jnp.concatenate([doc_ref[...], don_ref[...]], axis=0)
        lall = jnp.concatenate([lc_ref[0], ln_ref[0]], axis=0)
        row = lax.broadcasted_iota(jnp.int32, (rows, BLOCK), 0)
        r = row & (BLOCK - 1)
        c = lax.broadcasted_iota(jnp.int32, (rows, BLOCK), 1)
        diag_ok = (row < half) & (c <= r)
        next_ok = (row >= half) & (c > r)
        lane8 = lax.broadcasted_iota(jnp.int32, (8, LANES), 1)
        dsink = jnp.zeros((8, LANES), F32)
        off_prev = None
        for b in range(nb):
            blk = slice(b * BLOCK, (b + 1) * BLOCK)
            kv, vv = k_ref[blk, :], v_ref[blk, :]

            def stack(allv):
                return jnp.concatenate(
                    [allv[b * BLOCK:(b + 1) * BLOCK, g * LANES:(g + 1) * LANES] for g in range(A_GROUP)]
                    + [allv[(b + 1) * BLOCK:(b + 2) * BLOCK, g * LANES:(g + 1) * LANES] for g in range(A_GROUP)],
                    axis=0)

            qs, osk, dos = stack(qall), stack(oall), stack(doall)
            lse = jnp.concatenate([_lane_pick(lall[b * BLOCK:(b + 1) * BLOCK], g) for g in range(A_GROUP)]
                                  + [_lane_pick(lall[(b + 1) * BLOCK:(b + 2) * BLOCK], g) for g in range(A_GROUP)],
                                  axis=0)
            delta = jnp.sum(dos.astype(F32) * osk.astype(F32), axis=-1, keepdims=True)
            ok = (diag_ok | next_ok) if b < nb - 1 else (diag_ok | (next_ok & (n < nsteps - 1)))
            sc = jnp.where(ok, _dot_nt(qs, kv) * scale, NEG)
            p = jnp.exp(sc - lse)
            dv_ref[blk, :] = _dot_tn(p.astype(BF), dos)
            dp = _dot_nt(dos, vv)
            ds = (p * (dp - delta) * scale).astype(BF)
            dk_ref[blk, :] = _dot_tn(ds, qs)
            dqs = _dot(ds, kv)
            for g in range(A_GROUP):
                cur = slice(g * BLOCK, (g + 1) * BLOCK)
                before = carry[:, g * LANES:(g + 1) * LANES] if b == 0 else off_prev[cur]
                dq_ref[blk, g * LANES:(g + 1) * LANES] = before + dqs[cur]
                p_sink = jnp.exp(sink_ref[kvh * A_GROUP + g] - lse[cur])
                dsink = dsink + jnp.where(lane8 == g, -jnp.sum(p_sink * delta[cur]), 0.0)
            off_prev = dqs[half:]
        for g in range(A_GROUP):
            carry[:, g * LANES:(g + 1) * LANES] = off_prev[g * BLOCK:(g + 1) * BLOCK]
        dsink_ref[...] += dsink

    def nxt(n):
        return jnp.minimum((n + 1) * nb, nblk - 1)

    grp_c = pl.BlockSpec((qt, gw), lambda h, n: (n, h))
    grp_n = pl.BlockSpec((BLOCK, gw), lambda h, n: (nxt(n), h))
    kvb = pl.BlockSpec((qt, LANES), lambda h, n: (n, h))
    lse_c = pl.BlockSpec((1, qt, LANES), lambda h, n: (h, n, 0))
    lse_n = pl.BlockSpec((1, BLOCK, LANES), lambda h, n: (h, nxt(n), 0))
    dq, dk, dv, dsink = pl.pallas_call(
        body, grid=(A_KV_HEADS, nsteps),
        in_specs=[pl.BlockSpec(memory_space=pltpu.SMEM), grp_c, grp_n, kvb, kvb, grp_c, grp_n, grp_c, grp_n,
                  lse_c, lse_n],
        out_specs=[grp_c, kvb, kvb, pl.BlockSpec((8, LANES), lambda h, n: (h, 0))],
        out_shape=[jax.ShapeDtypeStruct(q.shape, F32), jax.ShapeDtypeStruct(k.shape, F32),
                   jax.ShapeDtypeStruct(v.shape, F32), jax.ShapeDtypeStruct((A_KV_HEADS * 8, LANES), F32)],
        scratch_shapes=[pltpu.VMEM((BLOCK, gw), F32)],
        compiler_params=_cp("parallel", "arbitrary"), name=name)(sinks, q, q, k, v, o, o, do, do, lse, lse)
    dsinks = dsink.reshape(A_KV_HEADS, 8, LANES)[:, 0, :A_GROUP].reshape(A_HEADS)
    return dq, dk, dv, dsinks


LOG2E = 1.4426950408889634


def _mla_fwd(q, k, v, *, name):
    s = q.shape[0]
    t = _tile(s, ATTN_TILE, LANES)
    nt = s // t
    scale = C_QK ** -0.5
    c2 = scale * LOG2E

    def body(q_ref, k_ref, v_ref, o_ref, lse_ref, m_sc, acc_sc):
        qi = pl.program_id(1)
        qv = q_ref[...]
        m_sc[...] = jnp.full_like(m_sc, NEG)
        acc_sc[...] = jnp.zeros_like(acc_sc)

        def step(start, width, diag):
            rows = pl.ds(pl.multiple_of(start, t), width)
            sc = _dot_nt(qv, k_ref[rows, :])
            if diag:
                r = lax.broadcasted_iota(jnp.int32, (t, t), 0)
                c = lax.broadcasted_iota(jnp.int32, (t, t), 1)
                own = jnp.where(c <= r, sc[:, width - t:], NEG)
                sc = own if width == t else jnp.concatenate([sc[:, :width - t], own], axis=1)
            m_prev = m_sc[...]
            m_new = jnp.maximum(m_prev, jnp.max(sc, axis=-1, keepdims=True))
            alpha = jnp.exp2((m_prev - m_new) * c2)
            p = jnp.exp2((sc - jnp.tile(m_new, (1, width // LANES))) * c2).astype(BF)
            lane = lax.broadcasted_iota(jnp.int32, (width, LANES), 1)
            vv = jnp.where(lane == C_V, jnp.ones((), BF), v_ref[rows, :])
            acc_sc[...] = alpha * acc_sc[...] + _dot(p, vv)
            m_sc[...] = m_new

        def single_body(j, carry):
            step(j * t, t, False)
            return carry

        with_diag = qi >= FWD_WIDE - 1

        @pl.when(with_diag)
        def _():
            step((qi - (FWD_WIDE - 1)) * t, FWD_WIDE * t, True)

        @pl.when(jnp.logical_not(with_diag))
        def _():
            step(qi * t, t, True)

        rest = jnp.where(with_diag, qi - (FWD_WIDE - 1), qi)
        n_wide = rest // FWD_WIDE
        done = n_wide * FWD_WIDE
        n_pair = (rest - done) // 2

        def wide_body(wi, carry):
            step(wi * (FWD_WIDE * t), FWD_WIDE * t, False)
            return carry

        def pair_body(wi, carry):
            step((done + 2 * wi) * t, 2 * t, False)
            return carry

        lax.fori_loop(0, n_wide, wide_body, 0)
        lax.fori_loop(0, n_pair, pair_body, 0)
        lax.fori_loop(done + 2 * n_pair, rest, single_body, 0)
        acc = acc_sc[...]
        l = _lane_pick(acc, C_V)
        lane = lax.broadcasted_iota(jnp.int32, (t, LANES), 1)
        o_ref[...] = jnp.where(lane == C_V, 0.0, acc * (1.0 / l)).astype(BF)
        lse2 = m_sc[...] * c2 + jnp.log(l) * LOG2E
        lse_ref[0, 0] = jnp.transpose(lse2)[0:8, :]

    qspec = pl.BlockSpec((t, LANES), lambda h, i: (i, h))
    kspec = pl.BlockSpec((s, LANES), lambda h, i: (0, h))
    return pl.pallas_call(
        body, grid=(C_HEADS, nt), in_specs=[qspec, kspec, kspec],
        out_specs=[qspec, pl.BlockSpec((1, 1, 8, t), lambda h, i: (h, i, 0, 0))],
        out_shape=[jax.ShapeDtypeStruct(q.shape, BF), jax.ShapeDtypeStruct((C_HEADS, nt, 8, t), F32)],
        scratch_shapes=[pltpu.VMEM((t, LANES), F32)] * 2,
        compiler_params=_cp("parallel", "parallel"), name=name)(q, k, v)


def _mla_delta(o, do, *, name):
    s = o.shape[0]
    t = _tile(s, ATTN_TILE, LANES)
    nt = s // t

    def body(o_ref, do_ref, d_ref):
        for h in range(C_HEADS):
            prod = jnp.transpose(_slot(o_ref, h).astype(F32) * _slot(do_ref, h).astype(F32))
            d_ref[h, 0] = jnp.broadcast_to(jnp.sum(prod, axis=0, keepdims=True), (8, t))

    blk = pl.BlockSpec((t, C_HEADS * LANES), lambda i: (i, 0))
    return pl.pallas_call(
        body, grid=(nt,), in_specs=[blk, blk],
        out_specs=pl.BlockSpec((C_HEADS, 1, 8, t), lambda i: (0, i, 0, 0)),
        out_shape=jax.ShapeDtypeStruct((C_HEADS, nt, 8, t), F32),
        compiler_params=_cp("parallel"), name=name)(o, do)


def _mla_bwd(q, k, v, do, lse2, delta, *, name):
    s = q.shape[0]
    t = _tile(s, ATTN_TILE, LANES)
    nt = s // t
    scale = C_QK ** -0.5
    c2 = scale * LOG2E

    def body(q_ref, do_ref, k_ref, v_ref, lse_ref, dl_ref, dq_ref, dk_ref, dv_ref):
        kj = pl.program_id(1)

        @pl.when(kj == 0)
        def _():
            dq_ref[...] = jnp.zeros_like(dq_ref)

        dk_ref[...] = jnp.zeros_like(dk_ref)
        dv_ref[...] = jnp.zeros_like(dv_ref)
        kv, vv = k_ref[...], v_ref[...]

        def step(i, n, diag):
            rows = pl.ds(pl.multiple_of(i * t, t), n * t)
            qv, dov = q_ref[rows, :], do_ref[rows, :]
            st = _dot_nt(kv, qv)
            if diag:
                r = lax.broadcasted_iota(jnp.int32, (t, t), 0)
                c = lax.broadcasted_iota(jnp.int32, (t, t), 1)
                own = jnp.where(r <= c, st[:, :t], NEG)
                st = own if n == 1 else jnp.concatenate([own, st[:, t:]], axis=1)
            lse = jnp.concatenate([lse_ref[0, i + u, 0:1, :] for u in range(n)], axis=1)
            dl = jnp.concatenate([dl_ref[0, i + u, 0:1, :] for u in range(n)], axis=1)
            pt = jnp.exp2(st * c2 - lse)
            dv_ref[...] += _dot(pt.astype(BF), dov)
            dpt = _dot_nt(vv, dov)
            dst = (pt * (dpt - dl) * scale).astype(BF)
            dk_ref[...] += _dot(dst, qv)
            dq_ref[rows, :] += _dot_tn(dst, kv)

        width = min(BWD_WIDE, nt)
        lead = (nt - kj - 1) % width + 1
        for v in range(1, width + 1):
            @pl.when(lead == v)
            def _(v=v):
                step(kj, v, True)

        first = kj + lead

        def run_body(pi, carry):
            step(first + width * pi, width, False)
            return carry

        lax.fori_loop(0, (nt - first) // width, run_body, 0)

    res = pl.BlockSpec((s, LANES), lambda h, j: (0, h))
    kspec = pl.BlockSpec((t, LANES), lambda h, j: (j, h))
    stat = pl.BlockSpec((1, nt, 8, t), lambda h, j: (h, 0, 0, 0))
    return pl.pallas_call(
        body, grid=(C_HEADS, nt), in_specs=[res, res, kspec, kspec, stat, stat],
        out_specs=[res, kspec, kspec],
        out_shape=[jax.ShapeDtypeStruct(q.shape, F32)] * 3,
        compiler_params=_cp("parallel", "arbitrary"), name=name)(q, do, k, v, lse2, delta)


def _conv_shifted(z, zprev, tm):
    row = lax.broadcasted_iota(jnp.int32, z.shape, 0)
    z1 = jnp.where(row == 0, zprev[7:8], pltpu.roll(z, 1, 0))
    z2 = jnp.where(row == 0, zprev[6:7], jnp.where(row == 1, zprev[7:8], pltpu.roll(z, 2, 0)))
    return z1, z2


def _conv_fwd(proj, w, *, name):
    s = proj.shape[0]
    d = D_MODEL
    tm = _tile(s, PREP_TILE, 8)

    def body(x_ref, xp_ref, w_ref, y_ref):
        i = pl.program_id(0)
        z = x_ref[:, d:2 * d] * x_ref[:, 2 * d:]
        zprev = jnp.where(i > 0, xp_ref[:, d:2 * d] * xp_ref[:, 2 * d:], 0.0)
        z1, z2 = _conv_shifted(z, zprev, tm)
        y = w_ref[0:1, :] * z2 + w_ref[1:2, :] * z1 + w_ref[2:3, :] * z
        y_ref[...] = (x_ref[:, :d] * y).astype(BF)

    per8 = tm // 8
    return pl.pallas_call(
        body, grid=(s // tm,),
        in_specs=[pl.BlockSpec((tm, 3 * d), lambda i: (i, 0)),
                  pl.BlockSpec((8, 3 * d), lambda i: (jnp.maximum(i * per8 - 1, 0), 0)),
                  pl.BlockSpec((8, d), lambda i: (0, 0))],
        out_specs=pl.BlockSpec((tm, d), lambda i: (i, 0)),
        out_shape=jax.ShapeDtypeStruct((s, d), BF),
        compiler_params=_cp("parallel"), name=name)(proj, proj, w)


def _conv_bwd(proj, dyb, w, *, name):
    s = proj.shape[0]
    d = D_MODEL
    tm = _tile(s, PREP_TILE, 8)
    nt = s // tm

    def body(x_ref, xp_ref, xn_ref, dy_ref, dyn_ref, w_ref, dx_ref, dw_ref):
        i = pl.program_id(0)

        @pl.when(i == 0)
        def _():
            dw_ref[...] = jnp.zeros_like(dw_ref)

        b, c, u = x_ref[:, :d], x_ref[:, d:2 * d], x_ref[:, 2 * d:]
        z = c * u
        zprev = jnp.where(i > 0, xp_ref[:, d:2 * d] * xp_ref[:, 2 * d:], 0.0)
        z1, z2 = _conv_shifted(z, zprev, tm)
        w0, w1, w2 = w_ref[0:1, :], w_ref[1:2, :], w_ref[2:3, :]
        y = w0 * z2 + w1 * z1 + w2 * z
        dyb_v = dy_ref[...]
        dyc = dyb_v * b
        dyn = jnp.where(i < nt - 1, dyn_ref[...] * xn_ref[:, :d], 0.0)
        row = lax.broadcasted_iota(jnp.int32, dyc.shape, 0)
        d1 = jnp.where(row == tm - 1, dyn[0:1], pltpu.roll(dyc, tm - 1, 0))
        d2 = jnp.where(row == tm - 1, dyn[1:2], jnp.where(row == tm - 2, dyn[0:1], pltpu.roll(dyc, tm - 2, 0)))
        dz = w2 * dyc + w1 * d1 + w0 * d2
        dx_ref[:, :d] = (dyb_v * y).astype(BF)
        dx_ref[:, d:2 * d] = (dz * u).astype(BF)
        dx_ref[:, 2 * d:] = (dz * c).astype(BF)
        dw_ref[0:1, :] += jnp.sum(dyc * z2, axis=0, keepdims=True)
        dw_ref[1:2, :] += jnp.sum(dyc * z1, axis=0, keepdims=True)
        dw_ref[2:3, :] += jnp.sum(dyc * z, axis=0, keepdims=True)

    per8 = tm // 8
    last8 = s // 8 - 1
    dx, dw = pl.pallas_call(
        body, grid=(nt,),
        in_specs=[pl.BlockSpec((tm, 3 * d), lambda i: (i, 0)),
                  pl.BlockSpec((8, 3 * d), lambda i: (jnp.maximum(i * per8 - 1, 0), 0)),
                  pl.BlockSpec((8, 3 * d), lambda i: (jnp.minimum((i + 1) * per8, last8), 0)),
                  pl.BlockSpec((tm, d), lambda i: (i, 0)),
                  pl.BlockSpec((8, d), lambda i: (jnp.minimum((i + 1) * per8, last8), 0)),
                  pl.BlockSpec((8, d), lambda i: (0, 0))],
        out_specs=[pl.BlockSpec((tm, 3 * d), lambda i: (i, 0)), pl.BlockSpec((8, d), lambda i: (0, 0))],
        out_shape=[jax.ShapeDtypeStruct((s, 3 * d), BF), jax.ShapeDtypeStruct((8, d), F32)],
        compiler_params=_cp("arbitrary"), name=name)(proj, proj, proj, dyb, dyb, w)
    return dx, dw[0:3]


def _ffn_up(h, wgu, *, name):
    s, d = h.shape
    f = wgu.shape[1] // 2
    tm = _tile(s, 2 * ROW_TILE, 8)
    tn = _tile(f, 1408, 128)
    nb = f // tn

    def body(h_ref, wg_ref, wu_ref, dg_ref, du_ref, a_ref):
        hv = h_ref[...]
        g = _dot(hv, wg_ref[...])
        u = _dot(hv, wu_ref[...])
        sig = jax.nn.sigmoid(g)
        silu = g * sig
        dg_ref[...] = (u * (sig + silu * (1.0 - sig))).astype(BF)
        du_ref[...] = silu.astype(BF)
        a_ref[...] = (silu * u).astype(BF)

    out = pl.BlockSpec((tm, tn), lambda j, i: (i, j))
    return pl.pallas_call(
        body, grid=(nb, s // tm),
        in_specs=[pl.BlockSpec((tm, d), lambda j, i: (i, 0)),
                  pl.BlockSpec((d, tn), lambda j, i: (0, j)),
                  pl.BlockSpec((d, tn), lambda j, i: (0, j + nb))],
        out_specs=[out] * 3, out_shape=[jax.ShapeDtypeStruct((s, f), BF)] * 3,
        compiler_params=_cp("parallel", "parallel"), name=name)(h, wgu, wgu)


def _ffn_bwd_mid(dy, wd, gate_factor, up_factor, *, name):
    s, d = dy.shape
    f = wd.shape[0]
    tm = _tile(s, ROW_TILE, 8)
    tn = _tile(f, 1408, 128)

    def body(dy_ref, wd_ref, g_ref, u_ref, dg_ref, du_ref):
        dyv = dy_ref[...].astype(BF)
        for c0 in range(0, tn, 2 * LANES):
            cols = slice(c0, min(c0 + 2 * LANES, tn))
            dact = _dot_nt(dyv, wd_ref[cols, :])
            dg_ref[:, cols] = (dact * g_ref[:, cols].astype(F32)).astype(BF)
            du_ref[:, cols] = (dact * u_ref[:, cols].astype(F32)).astype(BF)

    blk = pl.BlockSpec((tm, tn), lambda j, i: (i, j))
    return pl.pallas_call(
        body, grid=(f // tn, s // tm),
        in_specs=[pl.BlockSpec((tm, d), lambda j, i: (i, 0)), pl.BlockSpec((tn, d), lambda j, i: (j, 0)), blk, blk],
        out_specs=[blk, blk], out_shape=[jax.ShapeDtypeStruct((s, f), BF)] * 2,
        compiler_params=_cp("parallel", "parallel"), name=name)(dy, wd, gate_factor, up_factor)


def _pad_cols(w, heads, dim, layout):
    k = w.shape[0]
    w3 = w.reshape(k, heads, dim)
    pieces, lane = [], 0
    for lane0, dim0, cnt in sorted(layout):
        if lane0 > lane:
            pieces.append(jnp.zeros((k, heads, lane0 - lane), w.dtype))
        pieces.append(w3[:, :, dim0:dim0 + cnt])
        lane = lane0 + cnt
    if lane < LANES:
        pieces.append(jnp.zeros((k, heads, LANES - lane), w.dtype))
    return jnp.concatenate(pieces, axis=2).reshape(k, heads * LANES)


def _unpad_cols(w, heads, dim, layout):
    k = w.shape[0]
    w3 = w.reshape(k, heads, LANES)
    by_dim = sorted(layout, key=lambda seg: seg[1])
    assert sum(cnt for _, _, cnt in by_dim) == dim
    return jnp.concatenate([w3[:, :, lane0:lane0 + cnt] for lane0, _, cnt in by_dim], axis=2).reshape(k, heads * dim)


def _pad_rows(w, heads, dim):
    n = w.shape[1]
    return jnp.pad(w.reshape(heads, dim, n), ((0, 0), (0, LANES - dim), (0, 0))).reshape(heads * LANES, n)


def _unpad_rows(w, heads, dim):
    n = w.shape[1]
    return w.reshape(heads, LANES, n)[:, :dim, :].reshape(heads * dim, n)


def _pad_vec(g, layout):
    return _pad_cols(g.reshape(1, -1), 1, g.shape[0], layout)


def _unpad_vec(g, dim, layout):
    return _unpad_cols(g.reshape(1, LANES), 1, dim, layout)[0]


def _local_step(x, positions, w, target):
    s = x.shape[0]
    cos_a, sin_a, cos_c, sin_c = _rope_tables(positions.reshape(s, 1), name="rope_tables")
    grads = {n: [None] * len(w[n]) for n in WEIGHTS}
    saved = []

    for i in range(DEPTH):
        kind, j = i % N_MIXERS, i // N_MIXERS
        tag = f"l{i}"
        if i == 0:
            h = _rmsnorm_fwd(x, w['mix_norm'][i], name=f"{tag}_mix_norm")
        fnorm = w['ffn_norm'][i]
        if kind == 0:
            nqk = (A_HEADS + A_KV_HEADS) * A_HEAD_DIM
            wqkv = jnp.concatenate(
                [_pad_cols(w['a_w_qkv'][j][:, :nqk], A_HEADS + A_KV_HEADS, A_HEAD_DIM, LAYOUT_A),
                 _pad_cols(w['a_w_qkv'][j][:, nqk:], A_KV_HEADS, A_HEAD_DIM, LAYOUT_V)], axis=1)
            wo = _pad_rows(w['a_w_o'][j], A_HEADS, A_HEAD_DIM)
            gq, gk = _pad_vec(w['a_q_norm'][j], LAYOUT_A), _pad_vec(w['a_k_norm'][j], LAYOUT_A)
            qkv = _mm([(h, wqkv, 0)], out_dtype=F32, name=f"{tag}_qkv")
            qa, ka, va = _prep_a_fwd(qkv, cos_a, sin_a, gq, gk, name=f"{tag}_prep")
            o, lse = _swa_fwd(qa, ka, va, w['a_sinks'][j], name=f"{tag}_attn")
            x1, h2 = _mm([(o, wo, 0)], out_dtype=F32, res=x, norm_g=fnorm, name=f"{tag}_wo")
            mix = dict(wqkv=wqkv, wo=wo, gq=gq, gk=gk, qkv=qkv, qa=qa, ka=ka, va=va, o=o, lse=lse)
        elif kind == 1:
            cw = jnp.pad(w['b_conv_w'][j], ((0, 5), (0, 0)))
            proj = _mm([(h, w['b_w_in'][j], 0)], out_dtype=F32, name=f"{tag}_win")
            yb = _conv_fwd(proj, cw, name=f"{tag}_conv")
            x1, h2 = _mm([(yb, w['b_w_out'][j], 0)], out_dtype=F32, res=x, norm_g=fnorm, name=f"{tag}_wout")
            mix = dict(cw=cw, proj=proj, yb=yb)
        else:
            wdn = w['c_w_down'][j]
            nqk = C_Q_RANK + C_KV_RANK
            wdn = jnp.concatenate([wdn[:, :nqk], _pad_cols(wdn[:, nqk:], 1, C_ROPE, LAYOUT_KR)], axis=1)
            wq = _pad_cols(w['c_w_q_up'][j], C_HEADS, C_QK, LAYOUT_C)
            wkv = w['c_w_kv_up'][j].reshape(C_KV_RANK, C_HEADS, C_NOPE + C_V)
            wkn = _pad_cols(wkv[:, :, :C_NOPE].reshape(C_KV_RANK, -1), C_HEADS, C_NOPE, LAYOUT_KN)
            wv = _pad_cols(wkv[:, :, C_NOPE:].reshape(C_KV_RANK, -1), C_HEADS, C_V, LAYOUT_V)
            wo = _pad_rows(w['c_w_o'][j], C_HEADS, C_V)
            gq, gk = _pad_vec(w['c_q_norm'][j], LAYOUT_C), _pad_vec(w['c_k_norm'][j], LAYOUT_C)
            gqa, gkva = w['c_q_a_norm'][j].reshape(1, -1), w['c_kv_a_norm'][j].reshape(1, -1)
            dn = _mm([(h, wdn, 0)], out_dtype=F32, name=f"{tag}_wdown")
            cqn, ckvn = _prep_c1_fwd(dn, gqa, gkva, name=f"{tag}_prep1")
            qraw = _mm([(cqn, wq, 0)], out_dtype=F32, name=f"{tag}_wq")
            knope = _mm([(ckvn, wkn, 0)], out_dtype=F32, name=f"{tag}_wkn")
            vc = _mm([(ckvn, wv, 0)], out_dtype=BF, name=f"{tag}_wv")
            qc, kc = _prep_c2_fwd(qraw, knope, dn, cos_c, sin_c, gq, gk, name=f"{tag}_prep2")
            o, lse = _mla_fwd(qc, kc, vc, name=f"{tag}_attn")
            x1, h2 = _mm([(o, wo, 0)], out_dtype=F32, res=x, norm_g=fnorm, name=f"{tag}_wo")
            mix = dict(wdn=wdn, wq=wq, wkn=wkn, wv=wv, wo=wo, gq=gq, gk=gk, gqa=gqa, gkva=gkva, dn=dn, cqn=cqn,
                       ckvn=ckvn, qraw=qraw, knope=knope, vc=vc, qc=qc, kc=kc, o=o, lse=lse)
        gate_factor, up_factor, act = _ffn_up(h2, w['f_w_gate_up'][i], name=f"{tag}_ffn_up")
        saved.append(dict(x=x, h=h, x1=x1, h2=h2, gate_factor=gate_factor, up_factor=up_factor, act=act, mix=mix))
        if i + 1 < DEPTH:
            x, h = _mm([(act, w['f_w_down'][i], 0)], out_dtype=F32, res=x1, norm_g=w['mix_norm'][i + 1],
                       name=f"{tag}_ffn_down")
        else:
            x = _mm([(act, w['f_w_down'][i], 0)], out_dtype=F32, res=x1, name=f"{tag}_ffn_down")

    loss_blk, dx = _loss_fwd_bwd(x, target, name="loss")

    for i in reversed(range(DEPTH)):
        kind, j = i % N_MIXERS, i // N_MIXERS
        tag = f"l{i}b"
        sv = saved[i]
        mix = sv['mix']
        wgu, wd = w['f_w_gate_up'][i], w['f_w_down'][i]
        grads['f_w_down'][i] = _mm_tn(sv['act'], dx, name=f"{tag}_dwd")
        dgate, dup = _ffn_bwd_mid(dx, wd, sv['gate_factor'], sv['up_factor'], name=f"{tag}_ffn_mid")
        grads['f_w_gate_up'][i] = (_mm_tn(sv['h2'], dgate, name=f"{tag}_dwg"),
                                   _mm_tn(sv['h2'], dup, name=f"{tag}_dwu"))
        dx, dg = _mm_norm_bwd([(dgate, wgu, 0), (dup, wgu, 1)], sv['x1'], w['ffn_norm'][i], dx, name=f"{tag}_dh2")
        grads['ffn_norm'][i] = dg[0]
        if kind == 0:
            grads['a_w_o'][j] = _unpad_rows(_mm_tn(mix['o'], dx, name=f"{tag}_dwo"), A_HEADS, A_HEAD_DIM)
            do = _mm([(dx, mix['wo'], 0)], out_dtype=BF, trans_b=True, name=f"{tag}_do")
            dqa, dka, dva, dsinks = _swa_bwd(mix['qa'], mix['ka'], mix['va'], mix['o'], do, mix['lse'],
                                             w['a_sinks'][j], name=f"{tag}_attn")
            dqkv, dgq, dgk = _prep_a_bwd(mix['qkv'], dqa, dka, dva, cos_a, sin_a, mix['gq'], mix['gk'],
                                         name=f"{tag}_prep")
            grads['a_sinks'][j] = dsinks
            grads['a_q_norm'][j] = _unpad_vec(dgq, A_HEAD_DIM, LAYOUT_A)
            grads['a_k_norm'][j] = _unpad_vec(dgk, A_HEAD_DIM, LAYOUT_A)
            dwqkv = _mm_tn(sv['h'], dqkv, name=f"{tag}_dwqkv")
            nqk = (A_HEADS + A_KV_HEADS) * LANES
            grads['a_w_qkv'][j] = jnp.concatenate(
                [_unpad_cols(dwqkv[:, :nqk], A_HEADS + A_KV_HEADS, A_HEAD_DIM, LAYOUT_A),
                 _unpad_cols(dwqkv[:, nqk:], A_KV_HEADS, A_HEAD_DIM, LAYOUT_V)], axis=1)
            dh_pairs = [(dqkv, mix['wqkv'], 0)]
        elif kind == 1:
            grads['b_w_out'][j] = _mm_tn(mix['yb'], dx, name=f"{tag}_dwout")
            dyb = _mm([(dx, w['b_w_out'][j], 0)], out_dtype=F32, trans_b=True, name=f"{tag}_dyb")
            dproj, dcw = _conv_bwd(mix['proj'], dyb, mix['cw'], name=f"{tag}_conv")
            grads['b_conv_w'][j] = dcw
            grads['b_w_in'][j] = _mm_tn(sv['h'], dproj, name=f"{tag}_dwin")
            dh_pairs = [(dproj, w['b_w_in'][j], 0)]
        else:
            grads['c_w_o'][j] = _unpad_rows(_mm_tn(mix['o'], dx, name=f"{tag}_dwo"), C_HEADS, C_V)
            do = _mm([(dx, mix['wo'], 0)], out_dtype=BF, trans_b=True, name=f"{tag}_do")
            delta = _mla_delta(mix['o'], do, name=f"{tag}_delta")
            dqc, dkc, dvc = _mla_bwd(mix['qc'], mix['kc'], mix['vc'], do, mix['lse'], delta, name=f"{tag}_attn")
            dqraw, dknope, dkr, dgq, dgk = _prep_c2_bwd(mix['qraw'], mix['knope'], mix['dn'], dqc, dkc, cos_c, sin_c,
                                                        mix['gq'], mix['gk'], name=f"{tag}_prep2")
            grads['c_q_norm'][j] = _unpad_vec(dgq, C_QK, LAYOUT_C)
            grads['c_k_norm'][j] = _unpad_vec(dgk, C_QK, LAYOUT_C)
            grads['c_w_q_up'][j] = _unpad_cols(_mm_tn(mix['cqn'], dqraw, name=f"{tag}_dwq"), C_HEADS, C_QK,
                                               LAYOUT_C)
            dwkn = _unpad_cols(_mm_tn(mix['ckvn'], dknope, name=f"{tag}_dwkn"), C_HEADS, C_NOPE, LAYOUT_KN)
            dwv = _unpad_cols(_mm_tn(mix['ckvn'], dvc, name=f"{tag}_dwv"), C_HEADS, C_V, LAYOUT_V)
            grads['c_w_kv_up'][j] = jnp.concatenate(
                [dwkn.reshape(C_KV_RANK, C_HEADS, C_NOPE), dwv.reshape(C_KV_RANK, C_HEADS, C_V)], axis=2).reshape(
                C_KV_RANK, -1)
            dcq = _mm([(dqraw, mix['wq'], 0)], out_dtype=F32, trans_b=True, name=f"{tag}_dcq")
            dckv = _mm([(dknope, mix['wkn'], 0), (dvc, mix['wv'], 0)], out_dtype=F32, trans_b=True,
                       name=f"{tag}_dckv")
            ddn, dgqa, dgkva = _prep_c1_bwd(mix['dn'], dcq, dckv, dkr, mix['gqa'], mix['gkva'], name=f"{tag}_prep1")
            grads['c_q_a_norm'][j] = dgqa[0]
            grads['c_kv_a_norm'][j] = dgkva[0]
            dwdn = _mm_tn(sv['h'], ddn, name=f"{tag}_dwdown")
            nqk = C_Q_RANK + C_KV_RANK
            grads['c_w_down'][j] = jnp.concatenate(
                [dwdn[:, :nqk], _unpad_cols(dwdn[:, nqk:], 1, C_ROPE, LAYOUT_KR)], axis=1)
            dh_pairs = [(ddn, mix['wdn'], 0)]
        dx, dg = _mm_norm_bwd(dh_pairs, sv['x'], w['mix_norm'][i], dx, name=f"{tag}_dh")
        grads['mix_norm'][i] = dg[0]

    return loss_blk, dx, {n: (g if n in BIG else jnp.stack(g)) for n, g in grads.items()}


def _my_place():
    return lax.axis_index("x"), lax.axis_index("y"), lax.axis_index("c")


def _flips(x, y):
    return [(1 - x, y), (x, 1 - y), (1 - x, 1 - y)]


def _gather_weights(big, small):
    half = big.shape[0] // 2
    chunk = half // GATHER_CHUNKS

    def body(big_ref, small_ref, bout_ref, sout_ref, send_b, recv_b, send_f, recv_f, send_s, recv_s, loc):
        x, y, c = _my_place()
        me = 2 * x + y

        def rows(core, ch):
            return pl.ds(pl.multiple_of(core * half + ch * chunk, 16), chunk)

        def ici(k, ch, chip, to):
            return pltpu.make_async_remote_copy(
                src_ref=big_ref.at[rows(c, ch), :], dst_ref=bout_ref.at[chip, rows(c, ch), :],
                send_sem=send_b.at[k * GATHER_CHUNKS + ch], recv_sem=recv_b.at[k * GATHER_CHUNKS + ch],
                device_id=to, device_id_type=MESH)

        def forward(k, ch, chip, core):
            return pltpu.make_async_remote_copy(
                src_ref=bout_ref.at[chip, rows(core, ch), :], dst_ref=bout_ref.at[chip, rows(core, ch), :],
                send_sem=send_f.at[k * GATHER_CHUNKS + ch], recv_sem=recv_f.at[k * GATHER_CHUNKS + ch],
                device_id=(x, y, 1 - c), device_id_type=MESH)

        def small_copy(k, chip, to):
            return pltpu.make_async_remote_copy(src_ref=small_ref, dst_ref=sout_ref.at[chip],
                                                send_sem=send_s.at[k], recv_sem=recv_s.at[k],
                                                device_id=to, device_id_type=MESH)

        chips = _flips(x, y)
        every = [(ch, k, px, py) for ch in range(GATHER_CHUNKS) for k, (px, py) in enumerate(chips)]
        sends = [ici(k, ch, me, (px, py, c)) for ch, k, px, py in every]
        sends += [small_copy(k, me, (px, py, c)) for k, (px, py) in enumerate(chips)]
        for cp in sends:
            cp.start()
        own = (pltpu.make_async_copy(big_ref, bout_ref.at[me], loc.at[0]),
               pltpu.make_async_copy(small_ref, sout_ref.at[me], loc.at[1]))
        for cp in own:
            cp.start()
        passed = []
        for ch, k, px, py in every:
            ici(k, ch, 2 * px + py, (x, y, c)).wait_recv()
            passed.append(forward(k, ch, 2 * px + py, c))
            passed[-1].start()
        for ch, k, px, py in every:
            forward(k, ch, 2 * px + py, 1 - c).wait_recv()
        for k, (px, py) in enumerate(chips):
            small_copy(k, 2 * px + py, (x, y, c)).wait_recv()
        for cp in sends + passed:
            cp.wait_send()
        for cp in own:
            cp.wait()

    hbm = pl.BlockSpec(memory_space=pltpu.HBM)
    sem3 = pltpu.SemaphoreType.DMA((3,))
    semc = pltpu.SemaphoreType.DMA((3 * GATHER_CHUNKS,))
    return pl.pallas_call(
        body, in_specs=[hbm, hbm], out_specs=[hbm, hbm],
        out_shape=[jax.ShapeDtypeStruct((N_CHIPS,) + big.shape, big.dtype),
                   jax.ShapeDtypeStruct((N_CHIPS,) + small.shape, small.dtype)],
        scratch_shapes=[semc, semc, semc, semc, sem3, sem3, pltpu.SemaphoreType.DMA((2,))],
        name="gather_weights")(big, small)


def _exchange_grads(gbig, gsmall):
    def body(gbig_ref, gsmall_ref, got_ref, ssum_ref, sbuf, send_b, recv_b, send_s, recv_s):
        x, y, c = _my_place()
        me = 4 * x + 2 * y + c

        def big_copy(k, src_chip, to):
            return pltpu.make_async_remote_copy(src_ref=gbig_ref.at[src_chip], dst_ref=got_ref.at[k],
                                                send_sem=send_b.at[k], recv_sem=recv_b.at[k],
                                                device_id=to, device_id_type=MESH)

        def small_copy(k, slot, to):
            return pltpu.make_async_remote_copy(src_ref=gsmall_ref, dst_ref=sbuf.at[slot],
                                                send_sem=send_s.at[k], recv_sem=recv_s.at[k],
                                                device_id=to, device_id_type=MESH)

        peers = [(x ^ (k >> 2), y ^ ((k >> 1) & 1), c ^ (k & 1)) for k in range(1, N_DEV)]
        bigs = [big_copy(k, 2 * px + py, (px, py, c)) for k, (px, py) in enumerate(_flips(x, y))]
        smalls = [small_copy(k, me, p) for k, p in enumerate(peers)]
        for cp in bigs + smalls:
            cp.start()
        sbuf[me] = gsmall_ref[...]
        for k, (px, py, pc) in enumerate(peers):
            small_copy(k, 4 * px + 2 * py + pc, (x, y, c)).wait_recv()
        acc = sbuf[0]
        for d in range(1, N_DEV):
            acc = acc + sbuf[d]
        ssum_ref[...] = acc
        for k in range(3):
            big_copy(k, 0, (x, y, c)).wait_recv()
        for cp in bigs + smalls:
            cp.wait_send()

    hbm = pl.BlockSpec(memory_space=pltpu.HBM)
    vmem = pl.BlockSpec(memory_space=pltpu.VMEM)
    return pl.pallas_call(
        body, in_specs=[hbm, vmem], out_specs=[hbm, vmem],
        out_shape=[jax.ShapeDtypeStruct((3,) + gbig.shape[1:], gbig.dtype),
                   jax.ShapeDtypeStruct(gsmall.shape, gsmall.dtype)],
        scratch_shapes=[pltpu.VMEM((N_DEV,) + gsmall.shape, gsmall.dtype),
                        pltpu.SemaphoreType.DMA((3,)), pltpu.SemaphoreType.DMA((3,)),
                        pltpu.SemaphoreType.DMA((N_DEV - 1,)), pltpu.SemaphoreType.DMA((N_DEV - 1,))],
        name="exchange_grads")(gbig, gsmall)


def _sum_shards(chip, mine, got):
    r, cols = mine.shape[1:]
    tr = _tile(r, PACK_TILE_ROWS, 16)

    def body(chip_ref, mine_ref, got_ref, o_ref):
        o_ref[...] = ((mine_ref[0] + got_ref[0].astype(F32)) + got_ref[1].astype(F32)) + got_ref[2].astype(F32)

    grid_spec = pltpu.PrefetchScalarGridSpec(
        num_scalar_prefetch=1, grid=(r // tr,),
        in_specs=[pl.BlockSpec((1, tr, cols), lambda i, chip_ref: (chip_ref[0], i, 0)),
                  pl.BlockSpec((3, tr, cols), lambda i, chip_ref: (0, i, 0))],
        out_specs=pl.BlockSpec((tr, cols), lambda i, chip_ref: (i, 0)))
    return pl.pallas_call(
        body, grid_spec=grid_spec, out_shape=jax.ShapeDtypeStruct((r, cols), F32),
        compiler_params=_cp("parallel"), name="sum_shards")(chip, mine, got)


def _swap_sibling(t, *, name):
    def body(t_ref, got_ref, send_sem, recv_sem):
        x, y, c = _my_place()
        cp = pltpu.make_async_remote_copy(src_ref=t_ref, dst_ref=got_ref, send_sem=send_sem, recv_sem=recv_sem,
                                          device_id=(x, y, 1 - c), device_id_type=MESH)
        cp.start()
        cp.wait()

    hbm = pl.BlockSpec(memory_space=pltpu.HBM)
    return pl.pallas_call(
        body, in_specs=[hbm], out_specs=hbm, out_shape=jax.ShapeDtypeStruct(t.shape, t.dtype),
        scratch_shapes=[pltpu.SemaphoreType.DMA, pltpu.SemaphoreType.DMA], name=name)(t)


def _give_halves(parts, *, name):
    n = len(parts)
    r, cols = parts[0].shape
    half = r // 2

    def body(*refs):
        got_ref, send_sems, recv_sems = refs[n:]
        x, y, c = _my_place()
        theirs = pl.ds(pl.multiple_of((1 - c) * half, 16), half)
        copies = [pltpu.make_async_remote_copy(src_ref=refs[k].at[theirs, :], dst_ref=got_ref.at[k],
                                               send_sem=send_sems.at[k], recv_sem=recv_sems.at[k],
                                               device_id=(x, y, 1 - c), device_id_type=MESH) for k in range(n)]
        for cp in copies:
            cp.start()
        for cp in copies:
            cp.wait()

    hbm = pl.BlockSpec(memory_space=pltpu.HBM)
    return pl.pallas_call(
        body, in_specs=[hbm] * n, out_specs=hbm, out_shape=jax.ShapeDtypeStruct((n, half, cols), parts[0].dtype),
        scratch_shapes=[pltpu.SemaphoreType.DMA((n,)), pltpu.SemaphoreType.DMA((n,))], name=name)(*parts)


def _add_kept(core, parts, got, *, name):
    n, r, cols = got.shape
    tr = _tile(r, PACK_TILE_ROWS, 16)
    nblk = r // tr

    def body(core_ref, *refs):
        b_ref, o_ref, ob_ref = refs[n:]
        for k in range(n):
            sm = refs[k][...] + b_ref[k]
            o_ref[k] = sm
            ob_ref[k] = sm.astype(BF)

    blk = pl.BlockSpec((n, tr, cols), lambda i, core_ref: (0, i, 0))
    kept = pl.BlockSpec((tr, cols), lambda i, core_ref: (core_ref[0] * nblk + i, 0))
    grid_spec = pltpu.PrefetchScalarGridSpec(
        num_scalar_prefetch=1, grid=(nblk,), in_specs=[kept] * n + [blk], out_specs=[blk, blk])
    return pl.pallas_call(
        body, grid_spec=grid_spec,
        out_shape=[jax.ShapeDtypeStruct(got.shape, F32), jax.ShapeDtypeStruct(got.shape, BF)],
        compiler_params=_cp("parallel"), name=name)(core, *parts, got)


def _by_core(c, mine, sibling):
    return jnp.where(c == 0, mine, sibling), jnp.where(c == 0, sibling, mine)


def _adamw(wt, m, v, g, *, name):
    r, cols = wt.shape
    tr = _tile(r, 256, 8)

    def body(w_ref, m_ref, v_ref, g_ref, d_ref, nm_ref, nv_ref):
        gv = g_ref[...]
        nm = ADAM_B1 * m_ref[...] + (1.0 - ADAM_B1) * gv
        nv = ADAM_B2 * v_ref[...] + (1.0 - ADAM_B2) * (gv * gv)
        m_hat = nm / (1.0 - ADAM_B1 ** ADAM_STEP)
        v_hat = nv / (1.0 - ADAM_B2 ** ADAM_STEP)
        d_ref[...] = -ADAM_LR * (m_hat / (jnp.sqrt(v_hat) + ADAM_EPS) + ADAM_WD * w_ref[...])
        nm_ref[...] = nm
        nv_ref[...] = nv

    blk = pl.BlockSpec((tr, cols), lambda i: (i, 0))
    return pl.pallas_call(
        body, grid=(r // tr,), in_specs=[blk] * 4, out_specs=[blk] * 3,
        out_shape=[jax.ShapeDtypeStruct((r, cols), F32)] * 3,
        compiler_params=_cp("parallel"), name=name)(wt, m, v, g)


def _shard_shape(full, axis):
    return tuple(d // N_CHIPS if a == axis else d for a, d in enumerate(full))


def _pack_rows(n):
    rows = -(-n // PACK_COLS)
    return -(-rows // PACK_ROW_MULT) * PACK_ROW_MULT


def _pack(parts, rows, dtype):
    mat = jnp.concatenate([p.astype(dtype).reshape(-1, PACK_COLS) for p in parts], axis=0)
    return jnp.pad(mat, ((0, rows - mat.shape[0]), (0, 0)))


def _join_shards(stacked, axis):
    moved = jnp.moveaxis(stacked, 0, axis)
    shp = moved.shape
    return moved.reshape(shp[:axis] + (shp[axis] * shp[axis + 1],) + shp[axis + 2:])


def _split_shards(full, axis):
    shp = full.shape
    split = full.reshape(shp[:axis] + (N_CHIPS, shp[axis] // N_CHIPS) + shp[axis + 1:])
    return jnp.moveaxis(split, axis, 0)


SMALL_ROWS = 128
SMALL_LAYOUT = [('mix_norm', DEPTH * D_MODEL), ('ffn_norm', DEPTH * D_MODEL), ('b_conv_w', 3 * D_MODEL),
                ('c_q_a_norm', C_Q_RANK), ('c_kv_a_norm', C_KV_RANK), ('a_q_norm', 2 * A_HEAD_DIM),
                ('a_k_norm', 2 * A_HEAD_DIM), ('a_sinks', 2 * A_HEADS), ('c_q_norm', C_QK), ('c_k_norm', C_QK),
                ('loss', 1)]


def _small_offsets():
    offs, row = {}, 0
    for name, n in SMALL_LAYOUT:
        offs[name] = (row * LANES, n)
        row += -(-n // LANES)
    assert row <= SMALL_ROWS
    return offs


def kernel(x, positions, mix_norm, ffn_norm, a_w_qkv, a_q_norm, a_k_norm, a_sinks, a_w_o, b_w_in, b_conv_w, b_w_out, c_w_down, c_q_a_norm, c_kv_a_norm, c_w_q_up, c_w_kv_up, c_q_norm, c_k_norm, c_w_o, f_w_gate_up, f_w_down, loss_target, m_mix_norm, m_ffn_norm, m_a_w_qkv, m_a_q_norm, m_a_k_norm, m_a_sinks, m_a_w_o, m_b_w_in, m_b_conv_w, m_b_w_out, m_c_w_down, m_c_q_a_norm, m_c_kv_a_norm, m_c_w_q_up, m_c_w_kv_up, m_c_q_norm, m_c_k_norm, m_c_w_o, m_f_w_gate_up, m_f_w_down, v_mix_norm, v_ffn_norm, v_a_w_qkv, v_a_q_norm, v_a_k_norm, v_a_sinks, v_a_w_o, v_b_w_in, v_b_conv_w, v_b_w_out, v_c_w_down, v_c_q_a_norm, v_c_kv_a_norm, v_c_w_q_up, v_c_w_kv_up, v_c_q_norm, v_c_k_norm, v_c_w_o, v_f_w_gate_up, v_f_w_down):
    args = dict(locals())
    wshard = {n: args[n] for n in WEIGHTS}
    sharded = {**BIG, **SMALL_SHARDED}
    chip = 2 * lax.axis_index("x") + lax.axis_index("y")

    n_big = sum(wshard[n].size for n in BIG)
    rows = _pack_rows(n_big)
    big = _pack([wshard[n] for n in BIG], rows, BF)
    small = jnp.concatenate([wshard[n].reshape(-1) for n in SMALL_SHARDED])
    small = jnp.pad(small, (0, 8 * LANES - small.shape[0])).reshape(8, LANES)
    core = lax.axis_index("c")
    big_all, small_all = _gather_weights(big, small)
    big_all = big_all.reshape(N_CHIPS, -1)
    small_all = small_all.reshape(N_CHIPS, -1)
    w = {}
    off = 0
    for n, axis in BIG.items():
        shp = wshard[n].shape
        per = shp[1] * shp[2]
        w[n] = [_join_shards(big_all[:, off + l * per:off + (l + 1) * per].reshape((N_CHIPS,) + shp[1:]), axis - 1)
                for l in range(shp[0])]
        off += wshard[n].size
    off = 0
    for n, axis in SMALL_SHARDED.items():
        sz = wshard[n].size
        w[n] = _join_shards(small_all[:, off:off + sz].reshape((N_CHIPS,) + wshard[n].shape), axis)
        off += sz
    for n in WEIGHTS:
        if n not in sharded:
            w[n] = wshard[n]

    loss_blk, grad_x, g = _local_step(x[0], positions[0], w, loss_target[0])

    def shard_of(n, item, k):
        if isinstance(item, tuple):
            item = item[k // 2]
            k = k % 2
            width = item.shape[1] // 2
        else:
            width = item.shape[BIG[n] - 1] // N_CHIPS
        return item[:, k * width:(k + 1) * width] if BIG[n] == 2 else item[k * width:(k + 1) * width]

    gparts = [_pack([shard_of(n, item, k) for n in BIG for item in g[n]], rows, F32) for k in range(N_CHIPS)]
    offs = _small_offsets()
    smalls = {**{n: g[n] for n, _ in SMALL_LAYOUT if n != 'loss'}, 'loss': loss_blk[0:1, 0:1]}
    gsmall = jnp.concatenate(
        [jnp.pad(smalls[n].reshape(-1), (0, -cnt % LANES)) for n, cnt in SMALL_LAYOUT])
    gsmall = jnp.pad(gsmall, (0, SMALL_ROWS * LANES - gsmall.shape[0])).reshape(SMALL_ROWS, LANES)
    from_sibling = _give_halves(gparts, name="swap_grads")
    chip_sum, chip_sum_bf = _add_kept(core.reshape(1).astype(jnp.int32), gparts, from_sibling, name="add_sibling")
    got, ssum = _exchange_grads(chip_sum_bf, gsmall)
    part = _sum_shards(chip.reshape(1).astype(jnp.int32), chip_sum, got)
    part_sib = _swap_sibling(part, name="swap_sums")
    gflat = jnp.concatenate(_by_core(core, part, part_sib), axis=0).reshape(-1)

    ssum = ssum.reshape(-1)
    outs = {}
    off = 0
    for n in WEIGHTS:
        wt = wshard[n]
        if n in BIG:
            grad = gflat[off:off + wt.size]
            off += wt.size
        else:
            o0, cnt = offs[n]
            grad = ssum[o0:o0 + cnt].reshape(g[n].shape)
            if n in SMALL_SHARDED:
                grad = lax.dynamic_index_in_dim(_split_shards(grad, SMALL_SHARDED[n]), chip, 0, keepdims=False)
        shape2 = (-1, wt.shape[-1])
        grad = grad.reshape(shape2)
        res = _adamw(wt.reshape(shape2), args['m_' + n].reshape(shape2), args['v_' + n].reshape(shape2), grad,
                     name=f"adamw_{n}")
        outs[n] = [r.reshape(wt.shape) for r in [grad] + list(res)]
    loss = ssum[offs['loss'][0]]
    return (loss, grad_x[None], *[outs[n][0] for n in WEIGHTS], *[outs[n][1] for n in WEIGHTS],
            *[outs[n][2] for n in WEIGHTS], *[outs[n][3] for n in WEIGHTS])
```

```python
import functools

import jax
import jax.numpy as jnp
from jax import lax
from jax.experimental import pallas as pl
from jax.experimental.pallas import tpu as pltpu

F32 = jnp.float32
BF = jnp.bfloat16

D_MODEL = 1024
DEPTH = 4
N_MIXERS = 3
ROPE_THETA = 500000.0
EPS = 1e-6
BLOCK = 128
LANES = 128

A_HEADS, A_KV_HEADS, A_HEAD_DIM = 16, 4, 64
A_ROT_DIM = A_HEAD_DIM // 4
A_GROUP = A_HEADS // A_KV_HEADS
C_HEADS, C_NOPE, C_ROPE, C_V, C_Q_RANK, C_KV_RANK = 16, 64, 32, 64, 384, 256
C_QK = C_NOPE + C_ROPE
D_FF = 2816

ADAM_LR, ADAM_B1, ADAM_B2, ADAM_EPS, ADAM_WD, ADAM_STEP = 0.001, 0.9, 0.999, 1e-08, 0.01, 10

N_CHIPS = 4
N_DEV = 8
MESH = pl.DeviceIdType.MESH

VMEM_LIMIT = 56 * 1024 * 1024
ROW_TILE = 512
PREP_TILE = 256
ATTN_TILE = 512
FWD_WIDE = 4
BWD_WIDE = 5
SWA_BLOCKS = 16
NEG = -1e30

WEIGHTS = ['mix_norm', 'ffn_norm', 'a_w_qkv', 'a_q_norm', 'a_k_norm', 'a_sinks', 'a_w_o', 'b_w_in', 'b_conv_w',
           'b_w_out', 'c_w_down', 'c_q_a_norm', 'c_kv_a_norm', 'c_w_q_up', 'c_w_kv_up', 'c_q_norm', 'c_k_norm',
           'c_w_o', 'f_w_gate_up', 'f_w_down']
BIG = {'a_w_qkv': 2, 'a_w_o': 1, 'b_w_in': 2, 'b_w_out': 1, 'c_w_down': 1, 'c_w_q_up': 2, 'c_w_kv_up': 2,
       'c_w_o': 1, 'f_w_gate_up': 2, 'f_w_down': 1}
SMALL_SHARDED = {'b_conv_w': 2, 'c_q_a_norm': 1, 'c_kv_a_norm': 1}
PACK_COLS = 256
PACK_ROW_MULT = 32
PACK_TILE_ROWS = 1024
GATHER_CHUNKS = 3


def _cp(*sem):
    return pltpu.CompilerParams(dimension_semantics=sem, vmem_limit_bytes=VMEM_LIMIT)


def _tile(n, target, mult):
    if n <= target:
        return n
    t = (target // mult) * mult
    while t >= mult:
        if n % t == 0:
            return t
        t -= mult
    raise ValueError(f"no tile for {n}")


def _dot(a, b):
    return lax.dot_general(a, b, (((1,), (0,)), ((), ())), preferred_element_type=F32)


def _dot_nt(a, b):
    return lax.dot_general(a, b, (((1,), (1,)), ((), ())), preferred_element_type=F32)


def _dot_tn(a, b):
    return lax.dot_general(a, b, (((0,), (0,)), ((), ())), preferred_element_type=F32)


def _mm(pairs, *, out_dtype, name, res=None, trans_b=False, norm_g=None):
    a0, b0, _ = pairs[0]
    m = a0.shape[0]
    n = b0.shape[0] if trans_b else b0.shape[1]
    tm = _tile(m, 2 * ROW_TILE if sum(a.shape[1] for a, _, _ in pairs) <= 1024 else ROW_TILE, 8)
    tn = _tile(n, 1024, 128)
    n_pairs = len(pairs)
    has_res = res is not None
    has_norm = norm_g is not None
    assert not has_norm or tn == n

    def body(*refs):
        acc = None
        for p in range(n_pairs):
            a = refs[2 * p][...].astype(BF)
            b = refs[2 * p + 1][...].astype(BF)
            d = _dot_nt(a, b) if trans_b else _dot(a, b)
            acc = d if acc is None else acc + d
        nxt = 2 * n_pairs
        if has_res:
            acc = acc + refs[nxt][...]
            nxt += 1
        if has_norm:
            y = acc * lax.rsqrt(jnp.mean(acc * acc, axis=-1, keepdims=True) + EPS)
            refs[-1][...] = (y * refs[nxt][...]).astype(BF)
            refs[-2][...] = acc.astype(out_dtype)
        else:
            refs[-1][...] = acc.astype(out_dtype)

    in_specs, args = [], []
    for a, b, kblk in pairs:
        k = a.shape[1]
        in_specs.append(pl.BlockSpec((tm, k), lambda j, i: (i, 0)))
        if trans_b:
            in_specs.append(pl.BlockSpec((tn, k), functools.partial(lambda j, i, kb: (j, kb), kb=kblk)))
        else:
            in_specs.append(pl.BlockSpec((k, tn), lambda j, i: (0, j)))
        args += [a, b]
    if has_res:
        in_specs.append(pl.BlockSpec((tm, tn), lambda j, i: (i, j)))
        args.append(res)
    out_spec = pl.BlockSpec((tm, tn), lambda j, i: (i, j))
    out_specs, out_shape = out_spec, jax.ShapeDtypeStruct((m, n), out_dtype)
    if has_norm:
        in_specs.append(pl.BlockSpec((1, n), lambda j, i: (0, 0)))
        args.append(norm_g.reshape(1, n))
        out_specs, out_shape = [out_spec, out_spec], [out_shape, jax.ShapeDtypeStruct((m, n), BF)]
    return pl.pallas_call(
        body, grid=(n // tn, m // tm), in_specs=in_specs, out_specs=out_specs, out_shape=out_shape,
        compiler_params=_cp("parallel", "parallel"), name=name)(*args)


def _mm_loss(a, b, res, target, *, name):
    m, k = a.shape
    d = b.shape[1]
    tm = _tile(m, ROW_TILE, 8)

    def body(a_ref, b_ref, res_ref, t_ref, loss_ref, dy_ref):
        @pl.when(pl.program_id(0) == 0)
        def _():
            loss_ref[...] = jnp.zeros_like(loss_ref)

        err = _dot(a_ref[...].astype(BF), b_ref[...].astype(BF)) + res_ref[...] - t_ref[...]
        dy_ref[...] = err * (1.0 / d)
        loss_ref[...] += 0.5 * jnp.sum(jnp.sum(err * err, axis=-1, keepdims=True) * (1.0 / d))

    row = pl.BlockSpec((tm, d), lambda i: (i, 0))
    return pl.pallas_call(
        body, grid=(m // tm,),
        in_specs=[pl.BlockSpec((tm, k), lambda i: (i, 0)), pl.BlockSpec((k, d), lambda i: (0, 0)), row, row],
        out_specs=[pl.BlockSpec((8, LANES), lambda i: (0, 0)), row],
        out_shape=[jax.ShapeDtypeStruct((8, LANES), F32), jax.ShapeDtypeStruct((m, d), F32)],
        compiler_params=_cp("arbitrary"), name=name)(a, b, res, target)


def _mm_norm_bwd(pairs, x, g, dres, *, name):
    m, d = x.shape
    tm = _tile(m, ROW_TILE, 8)
    n_pairs = len(pairs)

    def body(*refs):
        x_ref, g_ref, dres_ref, dx_ref, dg_ref = refs[2 * n_pairs:]

        @pl.when(pl.program_id(0) == 0)
        def _():
            dg_ref[...] = jnp.zeros_like(dg_ref)

        dh = None
        for p in range(n_pairs):
            dpart = _dot_nt(refs[2 * p][...].astype(BF), refs[2 * p + 1][...].astype(BF))
            dh = dpart if dh is None else dh + dpart
        dx, xhat = _rms_bwd_math(x_ref[...], g_ref[...], dh, d)
        dx_ref[...] = dres_ref[...] + dx
        dg_ref[0:1, :] += jnp.sum(dh * xhat, axis=0, keepdims=True)

    in_specs, args = [], []
    for a, b, kblk in pairs:
        k = a.shape[1]
        in_specs.append(pl.BlockSpec((tm, k), lambda i: (i, 0)))
        in_specs.append(pl.BlockSpec((d, k), functools.partial(lambda i, kb: (0, kb), kb=kblk)))
        args += [a, b]
    row = pl.BlockSpec((tm, d), lambda i: (i, 0))
    in_specs += [row, pl.BlockSpec((1, d), lambda i: (0, 0)), row]
    dx, dg = pl.pallas_call(
        body, grid=(m // tm,), in_specs=in_specs,
        out_specs=[row, pl.BlockSpec((8, d), lambda i: (0, 0))],
        out_shape=[jax.ShapeDtypeStruct((m, d), F32), jax.ShapeDtypeStruct((8, d), F32)],
        compiler_params=_cp("arbitrary"), name=name)(*args, x, g.reshape(1, d), dres)
    return dx, dg[0:1]


def _mm_tn(a, b, *, name):
    m, k = a.shape
    n = b.shape[1]
    tm = _tile(m, 2 * ROW_TILE, 8)
    tk = _tile(k, 1408, 128)
    tn = _tile(n, 1408, 128)

    def body(a_ref, b_ref, o_ref):
        @pl.when(pl.program_id(2) == 0)
        def _():
            o_ref[...] = jnp.zeros_like(o_ref)

        o_ref[...] += _dot_tn(a_ref[...].astype(BF), b_ref[...].astype(BF))

    return pl.pallas_call(
        body, grid=(k // tk, n // tn, m // tm),
        in_specs=[pl.BlockSpec((tm, tk), lambda kk, j, i: (i, kk)),
                  pl.BlockSpec((tm, tn), lambda kk, j, i: (i, j))],
        out_specs=pl.BlockSpec((tk, tn), lambda kk, j, i: (kk, j)),
        out_shape=jax.ShapeDtypeStruct((k, n), F32),
        compiler_params=_cp("parallel", "parallel", "arbitrary"), name=name)(a, b)


def _rmsnorm_fwd(x, g, *, name):
    s, d = x.shape
    tm = _tile(s, ROW_TILE, 8)

    def body(x_ref, g_ref, h_ref):
        xv = x_ref[...]
        y = xv * lax.rsqrt(jnp.mean(xv * xv, axis=-1, keepdims=True) + EPS)
        h_ref[...] = (y * g_ref[...]).astype(BF)

    return pl.pallas_call(
        body, grid=(s // tm,),
        in_specs=[pl.BlockSpec((tm, d), lambda i: (i, 0)), pl.BlockSpec((1, d), lambda i: (0, 0))],
        out_specs=pl.BlockSpec((tm, d), lambda i: (i, 0)),
        out_shape=jax.ShapeDtypeStruct((s, d), BF),
        compiler_params=_cp("parallel"), name=name)(x, g.reshape(1, d))


def _row_sum(v):
    return jnp.sum(v, axis=-1, keepdims=True)


def _slot_sum(v):
    ones = jnp.ones((LANES, LANES), BF)
    hi = v.astype(BF)
    lo = (v - hi.astype(F32)).astype(BF)
    return _dot(hi, ones) + _dot(lo, ones)


def _rms_bwd_math(xv, g, dh, n, row_sum=_row_sum):
    rstd = lax.rsqrt(row_sum(xv * xv) * (1.0 / n) + EPS)
    xhat = xv * rstd
    dxh = dh * g
    dx = rstd * (dxh - xhat * (row_sum(dxh * xhat) * (1.0 / n)))
    return dx, xhat


HALF = LANES // 2
A_HR, C_HR = A_ROT_DIM // 2, C_ROPE // 2
A_X1, C_X1 = 0, 32
LAYOUT_A = [(A_X1, 0, A_HR), (A_HR, A_ROT_DIM, A_HEAD_DIM - A_ROT_DIM), (HALF + A_X1, A_HR, A_HR)]
LAYOUT_C = [(0, 0, 32), (C_X1, C_NOPE, C_HR), (HALF, 32, 32), (HALF + C_X1, C_NOPE + C_HR, C_HR)]
LAYOUT_KN = [(0, 0, 32), (HALF, 32, 32)]
LAYOUT_KR = [(C_X1, 0, C_HR), (HALF + C_X1, C_HR, C_HR)]
LAYOUT_V = [(0, 0, C_V)]


def _rope_rows():
    lane = jnp.arange(LANES)
    fa = ROPE_THETA ** (-jnp.arange(0, A_ROT_DIM, 2, dtype=F32) / A_ROT_DIM)
    fc = ROPE_THETA ** (-jnp.arange(0, C_ROPE, 2, dtype=F32) / C_ROPE)

    def rows(f, x1, hr):
        first = (lane >= x1) & (lane < x1 + hr)
        second = (lane >= HALF + x1) & (lane < HALF + x1 + hr)
        freq = jnp.where(first | second, f[(lane - x1) % HALF % hr], 0.0)
        return freq, jnp.where(first, -1.0, jnp.where(second, 1.0, 0.0))

    freq_a, sign_a = rows(fa, A_X1, A_HR)
    freq_c, sign_c = rows(fc, C_X1, C_HR)
    return jnp.stack([freq_a, sign_a, freq_c, sign_c] + [jnp.zeros(LANES)] * 4).astype(F32)


def _rope_tables(pos_col, *, name):
    s = pos_col.shape[0]
    tm = _tile(s, ROW_TILE, 8)

    def body(pos_ref, rows_ref, ca_ref, sa_ref, cc_ref, sc_ref):
        p = pos_ref[...].astype(F32)
        ang_a = p * rows_ref[0:1, :]
        ang_c = p * rows_ref[2:3, :]
        ca_ref[...] = jnp.cos(ang_a)
        sa_ref[...] = jnp.sin(ang_a) * rows_ref[1:2, :]
        cc_ref[...] = jnp.cos(ang_c)
        sc_ref[...] = jnp.sin(ang_c) * rows_ref[3:4, :]

    tab = pl.BlockSpec((tm, LANES), lambda i: (i, 0))
    return pl.pallas_call(
        body, grid=(s // tm,),
        in_specs=[pl.BlockSpec((tm, 1), lambda i: (i, 0)), pl.BlockSpec((8, LANES), lambda i: (0, 0))],
        out_specs=[tab] * 4, out_shape=[jax.ShapeDtypeStruct((s, LANES), F32)] * 4,
        compiler_params=_cp("parallel"), name=name)(pos_col, _rope_rows())


def _rope(xv, cos, sin):
    return xv * cos + pltpu.roll(xv, HALF, 1) * sin


def _slot(ref, t):
    return ref[:, t * LANES:(t + 1) * LANES]


def _head_norm_rope(xv, g, cos, sin, n):
    y = xv * lax.rsqrt(_slot_sum(xv * xv) * (1.0 / n) + EPS) * g
    return _rope(y, cos, sin)


def _head_norm_rope_bwd(xv, g, dy, cos, sin, n):
    dyn = _rope(dy, cos, -sin)
    dx, xhat = _rms_bwd_math(xv, g, dyn, n, _slot_sum)
    return dx, jnp.sum(dyn * xhat, axis=0, keepdims=True)


def _prep_a_fwd(qkv, cos, sin, gq, gk, *, name):
    s = qkv.shape[0]
    tm = _tile(s, PREP_TILE, 8)
    nq, nkv = A_HEADS, A_KV_HEADS

    def body(x_ref, c_ref, s_ref, gq_ref, gk_ref, q_ref, k_ref, v_ref):
        cv, sv = c_ref[...], s_ref[...]
        for t in range(nq):
            q_ref[:, t * LANES:(t + 1) * LANES] = _head_norm_rope(
                _slot(x_ref, t), gq_ref[...], cv, sv, A_HEAD_DIM).astype(BF)
        for t in range(nkv):
            k_ref[:, t * LANES:(t + 1) * LANES] = _head_norm_rope(
                _slot(x_ref, nq + t), gk_ref[...], cv, sv, A_HEAD_DIM).astype(BF)
        v_ref[...] = x_ref[:, (nq + nkv) * LANES:].astype(BF)

    def rows(w):
        return pl.BlockSpec((tm, w), lambda i: (i, 0))

    vec = pl.BlockSpec((1, LANES), lambda i: (0, 0))
    return pl.pallas_call(
        body, grid=(s // tm,),
        in_specs=[rows(qkv.shape[1]), rows(LANES), rows(LANES), vec, vec],
        out_specs=[rows(nq * LANES), rows(nkv * LANES), rows(nkv * LANES)],
        out_shape=[jax.ShapeDtypeStruct((s, nq * LANES), BF), jax.ShapeDtypeStruct((s, nkv * LANES), BF),
                   jax.ShapeDtypeStruct((s, nkv * LANES), BF)],
        compiler_params=_cp("parallel"), name=name)(qkv, cos, sin, gq, gk)


def _prep_a_bwd(qkv, dq, dk, dv, cos, sin, gq, gk, *, name):
    s = qkv.shape[0]
    tm = _tile(s, PREP_TILE, 8)
    nq, nkv = A_HEADS, A_KV_HEADS

    def body(x_ref, dq_ref, dk_ref, dv_ref, c_ref, s_ref, gq_ref, gk_ref, dx_ref, dg_ref):
        @pl.when(pl.program_id(0) == 0)
        def _():
            dg_ref[...] = jnp.zeros_like(dg_ref)

        cv, sv = c_ref[...], s_ref[...]
        dgq = jnp.zeros((1, LANES), F32)
        dgk = jnp.zeros((1, LANES), F32)
        for t in range(nq):
            dx, dg = _head_norm_rope_bwd(_slot(x_ref, t), gq_ref[...], _slot(dq_ref, t), cv, sv,
                                         A_HEAD_DIM)
            dx_ref[:, t * LANES:(t + 1) * LANES] = dx.astype(BF)
            dgq = dgq + dg
        for t in range(nkv):
            dx, dg = _head_norm_rope_bwd(_slot(x_ref, nq + t), gk_ref[...], _slot(dk_ref, t), cv, sv,
                                         A_HEAD_DIM)
            dx_ref[:, (nq + t) * LANES:(nq + t + 1) * LANES] = dx.astype(BF)
            dgk = dgk + dg
        dx_ref[:, (nq + nkv) * LANES:] = dv_ref[...].astype(BF)
        dg_ref[0:1, :] += dgq
        dg_ref[1:2, :] += dgk

    def rows(w):
        return pl.BlockSpec((tm, w), lambda i: (i, 0))

    vec = pl.BlockSpec((1, LANES), lambda i: (0, 0))
    dx, dg = pl.pallas_call(
        body, grid=(s // tm,),
        in_specs=[rows(qkv.shape[1]), rows(nq * LANES), rows(nkv * LANES), rows(nkv * LANES), rows(LANES),
                  rows(LANES), vec, vec],
        out_specs=[rows(qkv.shape[1]), pl.BlockSpec((8, LANES), lambda i: (0, 0))],
        out_shape=[jax.ShapeDtypeStruct(qkv.shape, BF), jax.ShapeDtypeStruct((8, LANES), F32)],
        compiler_params=_cp("arbitrary"), name=name)(qkv, dq, dk, dv, cos, sin, gq, gk)
    return dx, dg[0:1], dg[1:2]


def _prep_c1_fwd(dn, gq, gkv, *, name):
    s = dn.shape[0]
    tm = _tile(s, ROW_TILE, 8)

    def body(x_ref, gq_ref, gkv_ref, cq_ref, ckv_ref):
        for lo, n, g_ref, o_ref in ((0, C_Q_RANK, gq_ref, cq_ref), (C_Q_RANK, C_KV_RANK, gkv_ref, ckv_ref)):
            xv = x_ref[:, lo:lo + n]
            y = xv * lax.rsqrt(jnp.mean(xv * xv, axis=-1, keepdims=True) + EPS)
            o_ref[...] = (y * g_ref[...]).astype(BF)

    def rows(w):
        return pl.BlockSpec((tm, w), lambda i: (i, 0))

    return pl.pallas_call(
        body, grid=(s // tm,),
        in_specs=[rows(dn.shape[1]), pl.BlockSpec((1, C_Q_RANK), lambda i: (0, 0)),
                  pl.BlockSpec((1, C_KV_RANK), lambda i: (0, 0))],
        out_specs=[rows(C_Q_RANK), rows(C_KV_RANK)],
        out_shape=[jax.ShapeDtypeStruct((s, C_Q_RANK), BF), jax.ShapeDtypeStruct((s, C_KV_RANK), BF)],
        compiler_params=_cp("parallel"), name=name)(dn, gq, gkv)


def _prep_c1_bwd(dn, dcq, dckv, dkr, gq, gkv, *, name):
    s = dn.shape[0]
    tm = _tile(s, ROW_TILE, 8)

    def body(x_ref, dcq_ref, dckv_ref, dkr_ref, gq_ref, gkv_ref, dx_ref, dgq_ref, dgkv_ref):
        @pl.when(pl.program_id(0) == 0)
        def _():
            dgq_ref[...] = jnp.zeros_like(dgq_ref)
            dgkv_ref[...] = jnp.zeros_like(dgkv_ref)

        for lo, n, g_ref, d_ref, dg_ref in ((0, C_Q_RANK, gq_ref, dcq_ref, dgq_ref),
                                            (C_Q_RANK, C_KV_RANK, gkv_ref, dckv_ref, dgkv_ref)):
            dv = d_ref[...]
            dx, xhat = _rms_bwd_math(x_ref[:, lo:lo + n], g_ref[...], dv, n)
            dx_ref[:, lo:lo + n] = dx.astype(BF)
            dg_ref[0:1, :] += jnp.sum(dv * xhat, axis=0, keepdims=True)
        dx_ref[:, C_Q_RANK + C_KV_RANK:] = dkr_ref[...].astype(BF)

    def rows(w):
        return pl.BlockSpec((tm, w), lambda i: (i, 0))

    def vec(w, r=1):
        return pl.BlockSpec((r, w), lambda i: (0, 0))

    dx, dgq, dgkv = pl.pallas_call(
        body, grid=(s // tm,),
        in_specs=[rows(dn.shape[1]), rows(C_Q_RANK), rows(C_KV_RANK), rows(LANES), vec(C_Q_RANK), vec(C_KV_RANK)],
        out_specs=[rows(dn.shape[1]), vec(C_Q_RANK, 8), vec(C_KV_RANK, 8)],
        out_shape=[jax.ShapeDtypeStruct(dn.shape, BF), jax.ShapeDtypeStruct((8, C_Q_RANK), F32),
                   jax.ShapeDtypeStruct((8, C_KV_RANK), F32)],
        compiler_params=_cp("arbitrary"), name=name)(dn, dcq, dckv, dkr, gq, gkv)
    return dx, dgq[0:1], dgkv[0:1]


def _prep_c2_fwd(qraw, knope, dn, cos, sin, gq, gk, *, name):
    s = qraw.shape[0]
    tm = _tile(s, PREP_TILE, 8)
    kr_blk = dn.shape[1] // LANES - 1

    def body(q_ref, kn_ref, kr_ref, c_ref, s_ref, gq_ref, gk_ref, qo_ref, ko_ref):
        cv, sv, kr = c_ref[...], s_ref[...], kr_ref[...]
        for t in range(C_HEADS):
            qo_ref[:, t * LANES:(t + 1) * LANES] = _head_norm_rope(
                _slot(q_ref, t), gq_ref[...], cv, sv, C_QK).astype(BF)
            ko_ref[:, t * LANES:(t + 1) * LANES] = _head_norm_rope(
                _slot(kn_ref, t) + kr, gk_ref[...], cv, sv, C_QK).astype(BF)

    def rows(w, blk=0):
        return pl.BlockSpec((tm, w), lambda i: (i, blk))

    vec = pl.BlockSpec((1, LANES), lambda i: (0, 0))
    w = C_HEADS * LANES
    return pl.pallas_call(
        body, grid=(s // tm,),
        in_specs=[rows(w), rows(w), rows(LANES, kr_blk), rows(LANES), rows(LANES), vec, vec],
        out_specs=[rows(w), rows(w)],
        out_shape=[jax.ShapeDtypeStruct((s, w), BF)] * 2,
        compiler_params=_cp("parallel"), name=name)(qraw, knope, dn, cos, sin, gq, gk)


def _prep_c2_bwd(qraw, knope, dn, dq, dk, cos, sin, gq, gk, *, name):
    s = qraw.shape[0]
    tm = _tile(s, PREP_TILE, 8)
    kr_blk = dn.shape[1] // LANES - 1

    def body(q_ref, kn_ref, kr_ref, dq_ref, dk_ref, c_ref, s_ref, gq_ref, gk_ref,
             dqo_ref, dkno_ref, dkr_ref, dg_ref):
        @pl.when(pl.program_id(0) == 0)
        def _():
            dg_ref[...] = jnp.zeros_like(dg_ref)

        cv, sv, kr = c_ref[...], s_ref[...], kr_ref[...]
        dgq = jnp.zeros((1, LANES), F32)
        dgk = jnp.zeros((1, LANES), F32)
        dkr = jnp.zeros((tm, LANES), F32)
        for t in range(C_HEADS):
            dx, dg = _head_norm_rope_bwd(_slot(q_ref, t), gq_ref[...], _slot(dq_ref, t), cv, sv,
                                         C_QK)
            dqo_ref[:, t * LANES:(t + 1) * LANES] = dx.astype(BF)
            dgq = dgq + dg
            dx, dg = _head_norm_rope_bwd(_slot(kn_ref, t) + kr, gk_ref[...], _slot(dk_ref, t), cv, sv,
                                         C_QK)
            dkno_ref[:, t * LANES:(t + 1) * LANES] = dx.astype(BF)
            dkr = dkr + dx
            dgk = dgk + dg
        dkr_ref[...] = dkr
        dg_ref[0:1, :] += dgq
        dg_ref[1:2, :] += dgk

    def rows(w, blk=0):
        return pl.BlockSpec((tm, w), lambda i: (i, blk))

    vec = pl.BlockSpec((1, LANES), lambda i: (0, 0))
    w = C_HEADS * LANES
    dqo, dkno, dkr, dg = pl.pallas_call(
        body, grid=(s // tm,),
        in_specs=[rows(w), rows(w), rows(LANES, kr_blk), rows(w), rows(w), rows(LANES), rows(LANES), vec, vec],
        out_specs=[rows(w), rows(w), rows(LANES), pl.BlockSpec((8, LANES), lambda i: (0, 0))],
        out_shape=[jax.ShapeDtypeStruct((s, w), BF), jax.ShapeDtypeStruct((s, w), BF),
                   jax.ShapeDtypeStruct((s, LANES), F32), jax.ShapeDtypeStruct((8, LANES), F32)],
        compiler_params=_cp("arbitrary"), name=name)(qraw, knope, dn, dq, dk, cos, sin, gq, gk)
    return dqo, dkno, dkr, dg[0:1], dg[1:2]


def _lane_pick(mat, g):
    lane = lax.broadcasted_iota(jnp.int32, mat.shape, 1)
    return jnp.sum(jnp.where(lane == g, mat, 0.0), axis=-1, keepdims=True)


def _swa_fwd(q, k, v, sinks, *, name):
    s = q.shape[0]
    nb = min(SWA_BLOCKS, s // BLOCK)
    qt = nb * BLOCK
    scale = A_HEAD_DIM ** -0.5
    gw = A_GROUP * LANES

    def body(sink_ref, q_ref, kc_ref, kp_ref, vc_ref, vp_ref, o_ref, lse_ref):
        kvh, n = pl.program_id(0), pl.program_id(1)
        rows = A_GROUP * BLOCK
        kall = jnp.concatenate([kp_ref[...], kc_ref[...]], axis=0)
        vall = jnp.concatenate([vp_ref[...], vc_ref[...]], axis=0)
        r = lax.broadcasted_iota(jnp.int32, (rows, 2 * BLOCK), 0) & (BLOCK - 1)
        c = lax.broadcasted_iota(jnp.int32, (rows, 2 * BLOCK), 1)
        cur_ok = (c >= BLOCK) & (c - BLOCK <= r)
        prev_ok = (c < BLOCK) & (c > r)
        head = lax.broadcasted_iota(jnp.int32, (rows, 1), 0) >> 7
        sink = jnp.zeros((rows, 1), F32)
        for g in range(A_GROUP):
            sink = jnp.where(head == g, sink_ref[kvh * A_GROUP + g], sink)
        lane = lax.broadcasted_iota(jnp.int32, (BLOCK, LANES), 1)
        for b in range(nb):
            blk = slice(b * BLOCK, (b + 1) * BLOCK)
            qs = jnp.concatenate([q_ref[blk, g * LANES:(g + 1) * LANES] for g in range(A_GROUP)], axis=0)
            k2 = kall[b * BLOCK:(b + 2) * BLOCK]
            v2 = vall[b * BLOCK:(b + 2) * BLOCK]
            ok = (cur_ok | prev_ok) if b > 0 else (cur_ok | (prev_ok & (n > 0)))
            sc = jnp.where(ok, _dot_nt(qs, k2) * scale, NEG)
            m = jnp.maximum(jnp.max(sc, axis=-1, keepdims=True), sink)
            p = jnp.exp(sc - m)
            l = jnp.sum(p, axis=-1, keepdims=True) + jnp.exp(sink - m)
            o = _dot((p * (1.0 / l)).astype(BF), v2)
            lse = m + jnp.log(l)
            lse_mat = jnp.zeros((BLOCK, LANES), F32)
            for g in range(A_GROUP):
                o_ref[blk, g * LANES:(g + 1) * LANES] = o[g * BLOCK:(g + 1) * BLOCK].astype(BF)
                lse_mat = jnp.where(lane == g, lse[g * BLOCK:(g + 1) * BLOCK], lse_mat)
            lse_ref[0, blk, :] = lse_mat

    cur = pl.BlockSpec((qt, LANES), lambda h, n: (n, h))
    prev = pl.BlockSpec((BLOCK, LANES), lambda h, n: (jnp.maximum(n * nb - 1, 0), h))
    return pl.pallas_call(
        body, grid=(A_KV_HEADS, s // qt),
        in_specs=[pl.BlockSpec(memory_space=pltpu.SMEM), pl.BlockSpec((qt, gw), lambda h, n: (n, h)),
                  cur, prev, cur, prev],
        out_specs=[pl.BlockSpec((qt, gw), lambda h, n: (n, h)),
                   pl.BlockSpec((1, qt, LANES), lambda h, n: (h, n, 0))],
        out_shape=[jax.ShapeDtypeStruct(q.shape, BF), jax.ShapeDtypeStruct((A_KV_HEADS, s, LANES), F32)],
        compiler_params=_cp("parallel", "parallel"), name=name)(sinks, q, k, k, v, v)


def _swa_bwd(q, k, v, o, do, lse, sinks, *, name):
    s = q.shape[0]
    nblk = s // BLOCK
    nb = min(SWA_BLOCKS, nblk)
    qt = nb * BLOCK
    nsteps = s // qt
    scale = A_HEAD_DIM ** -0.5
    gw = A_GROUP * LANES

    def body(sink_ref, qc_ref, qn_ref, k_ref, v_ref, oc_ref, on_ref, doc_ref, don_ref, lc_ref, ln_ref,
             dq_ref, dk_ref, dv_ref, dsink_ref, carry):
        kvh, n = pl.program_id(0), pl.program_id(1)

        @pl.when(n == 0)
        def _():
            carry[...] = jnp.zeros_like(carry)
            dsink_ref[...] = jnp.zeros_like(dsink_ref)

        half = A_GROUP * BLOCK
        rows = 2 * half
        qall = jnp.concatenate([qc_ref[...], qn_ref[...]], axis=0)
        oall = jnp.concatenate([oc_ref[...], on_ref[...]], axis=0)
        doall = jnp.concatenate([doc_ref[...], don_ref[...]], axis=0)
        lall = jnp.concatenate([lc_ref[0], ln_ref[0]], axis=0)
        row = lax.broadcasted_iota(jnp.int32, (rows, BLOCK), 0)
        r = row & (BLOCK - 1)
        c = lax.broadcasted_iota(jnp.int32, (rows, BLOCK), 1)
        diag_ok = (row < half) & (c <= r)
        next_ok = (row >= half) & (c > r)
        lane8 = lax.broadcasted_iota(jnp.int32, (8, LANES), 1)
        dsink = jnp.zeros((8, LANES), F32)
        off_prev = None
        for b in range(nb):
            blk = slice(b * BLOCK, (b + 1) * BLOCK)
            kv, vv = k_ref[blk, :], v_ref[blk, :]

            def stack(allv):
                return jnp.concatenate(
                    [allv[b * BLOCK:(b + 1) * BLOCK, g * LANES:(g + 1) * LANES] for g in range(A_GROUP)]
                    + [allv[(b + 1) * BLOCK:(b + 2) * BLOCK, g * LANES:(g + 1) * LANES] for g in range(A_GROUP)],
                    axis=0)

            qs, osk, dos = stack(qall), stack(oall), stack(doall)
            lse = jnp.concatenate([_lane_pick(lall[b * BLOCK:(b + 1) * BLOCK], g) for g in range(A_GROUP)]
                                  + [_lane_pick(lall[(b + 1) * BLOCK:(b + 2) * BLOCK], g) for g in range(A_GROUP)],
                                  axis=0)
            delta = jnp.sum(dos.astype(F32) * osk.astype(F32), axis=-1, keepdims=True)
            ok = (diag_ok | next_ok) if b < nb - 1 else (diag_ok | (next_ok & (n < nsteps - 1)))
            sc = jnp.where(ok, _dot_nt(qs, kv) * scale, NEG)
            p = jnp.exp(sc - lse)
            dv_ref[blk, :] = _dot_tn(p.astype(BF), dos)
            dp = _dot_nt(dos, vv)
            ds = (p * (dp - delta) * scale).astype(BF)
            dk_ref[blk, :] = _dot_tn(ds, qs)
            dqs = _dot(ds, kv)
            for g in range(A_GROUP):
                cur = slice(g * BLOCK, (g + 1) * BLOCK)
                before = carry[:, g * LANES:(g + 1) * LANES] if b == 0 else off_prev[cur]
                dq_ref[blk, g * LANES:(g + 1) * LANES] = before + dqs[cur]
                p_sink = jnp.exp(sink_ref[kvh * A_GROUP + g] - lse[cur])
                dsink = dsink + jnp.where(lane8 == g, -jnp.sum(p_sink * delta[cur]), 0.0)
            off_prev = dqs[half:]
        for g in range(A_GROUP):
            carry[:, g * LANES:(g + 1) * LANES] = off_prev[g * BLOCK:(g + 1) * BLOCK]
        dsink_ref[...] += dsink

    def nxt(n):
        return jnp.minimum((n + 1) * nb, nblk - 1)

    grp_c = pl.BlockSpec((qt, gw), lambda h, n: (n, h))
    grp_n = pl.BlockSpec((BLOCK, gw), lambda h, n: (nxt(n), h))
    kvb = pl.BlockSpec((qt, LANES), lambda h, n: (n, h))
    lse_c = pl.BlockSpec((1, qt, LANES), lambda h, n: (h, n, 0))
    lse_n = pl.BlockSpec((1, BLOCK, LANES), lambda h, n: (h, nxt(n), 0))
    dq, dk, dv, dsink = pl.pallas_call(
        body, grid=(A_KV_HEADS, nsteps),
        in_specs=[pl.BlockSpec(memory_space=pltpu.SMEM), grp_c, grp_n, kvb, kvb, grp_c, grp_n, grp_c, grp_n,
                  lse_c, lse_n],
        out_specs=[grp_c, kvb, kvb, pl.BlockSpec((8, LANES), lambda h, n: (h, 0))],
        out_shape=[jax.ShapeDtypeStruct(q.shape, F32), jax.ShapeDtypeStruct(k.shape, F32),
                   jax.ShapeDtypeStruct(v.shape, F32), jax.ShapeDtypeStruct((A_KV_HEADS * 8, LANES), F32)],
        scratch_shapes=[pltpu.VMEM((BLOCK, gw), F32)],
        compiler_params=_cp("parallel", "arbitrary"), name=name)(sinks, q, q, k, v, o, o, do, do, lse, lse)
    dsinks = dsink.reshape(A_KV_HEADS, 8, LANES)[:, 0, :A_GROUP].reshape(A_HEADS)
    return dq, dk, dv, dsinks


LOG2E = 1.4426950408889634


def _mla_fwd(q, k, v, *, name):
    s = q.shape[0]
    t = _tile(s, ATTN_TILE, LANES)
    nt = s // t
    scale = C_QK ** -0.5
    c2 = scale * LOG2E

    def body(q_ref, k_ref, v_ref, o_ref, lse_ref, m_sc, acc_sc):
        qi = pl.program_id(1)
        qv = q_ref[...]
        m_sc[...] = jnp.full_like(m_sc, NEG)
        acc_sc[...] = jnp.zeros_like(acc_sc)

        def step(start, width, diag):
            rows = pl.ds(pl.multiple_of(start, t), width)
            sc = _dot_nt(qv, k_ref[rows, :])
            if diag:
                r = lax.broadcasted_iota(jnp.int32, (t, t), 0)
                c = lax.broadcasted_iota(jnp.int32, (t, t), 1)
                own = jnp.where(c <= r, sc[:, width - t:], NEG)
                sc = own if width == t else jnp.concatenate([sc[:, :width - t], own], axis=1)
            m_prev = m_sc[...]
            m_new = jnp.maximum(m_prev, jnp.max(sc, axis=-1, keepdims=True))
            alpha = jnp.exp2((m_prev - m_new) * c2)
            p = jnp.exp2((sc - jnp.tile(m_new, (1, width // LANES))) * c2).astype(BF)
            lane = lax.broadcasted_iota(jnp.int32, (width, LANES), 1)
            vv = jnp.where(lane == C_V, jnp.ones((), BF), v_ref[rows, :])
            acc_sc[...] = alpha * acc_sc[...] + _dot(p, vv)
            m_sc[...] = m_new

        def single_body(j, carry):
            step(j * t, t, False)
            return carry

        with_diag = qi >= FWD_WIDE - 1

        @pl.when(with_diag)
        def _():
            step((qi - (FWD_WIDE - 1)) * t, FWD_WIDE * t, True)

        @pl.when(jnp.logical_not(with_diag))
        def _():
            step(qi * t, t, True)

        rest = jnp.where(with_diag, qi - (FWD_WIDE - 1), qi)
        n_wide = rest // FWD_WIDE
        done = n_wide * FWD_WIDE
        n_pair = (rest - done) // 2

        def wide_body(wi, carry):
            step(wi * (FWD_WIDE * t), FWD_WIDE * t, False)
            return carry

        def pair_body(wi, carry):
            step((done + 2 * wi) * t, 2 * t, False)
            return carry

        lax.fori_loop(0, n_wide, wide_body, 0)
        lax.fori_loop(0, n_pair, pair_body, 0)
        lax.fori_loop(done + 2 * n_pair, rest, single_body, 0)
        acc = acc_sc[...]
        l = _lane_pick(acc, C_V)
        lane = lax.broadcasted_iota(jnp.int32, (t, LANES), 1)
        o_ref[...] = jnp.where(lane == C_V, 0.0, acc * (1.0 / l)).astype(BF)
        lse2 = m_sc[...] * c2 + jnp.log(l) * LOG2E
        lse_ref[0, 0] = jnp.transpose(lse2)[0:8, :]

    qspec = pl.BlockSpec((t, LANES), lambda h, i: (i, h))
    kspec = pl.BlockSpec((s, LANES), lambda h, i: (0, h))
    return pl.pallas_call(
        body, grid=(C_HEADS, nt), in_specs=[qspec, kspec, kspec],
        out_specs=[qspec, pl.BlockSpec((1, 1, 8, t), lambda h, i: (h, i, 0, 0))],
        out_shape=[jax.ShapeDtypeStruct(q.shape, BF), jax.ShapeDtypeStruct((C_HEADS, nt, 8, t), F32)],
        scratch_shapes=[pltpu.VMEM((t, LANES), F32)] * 2,
        compiler_params=_cp("parallel", "parallel"), name=name)(q, k, v)


def _mla_delta(o, do, *, name):
    s = o.shape[0]
    t = _tile(s, ATTN_TILE, LANES)
    nt = s // t

    def body(o_ref, do_ref, d_ref):
        for h in range(C_HEADS):
            prod = jnp.transpose(_slot(o_ref, h).astype(F32) * _slot(do_ref, h).astype(F32))
            d_ref[h, 0] = jnp.broadcast_to(jnp.sum(prod, axis=0, keepdims=True), (8, t))

    blk = pl.BlockSpec((t, C_HEADS * LANES), lambda i: (i, 0))
    return pl.pallas_call(
        body, grid=(nt,), in_specs=[blk, blk],
        out_specs=pl.BlockSpec((C_HEADS, 1, 8, t), lambda i: (0, i, 0, 0)),
        out_shape=jax.ShapeDtypeStruct((C_HEADS, nt, 8, t), F32),
        compiler_params=_cp("parallel"), name=name)(o, do)


def _mla_bwd(q, k, v, do, lse2, delta, *, name):
    s = q.shape[0]
    t = _tile(s, ATTN_TILE, LANES)
    nt = s // t
    scale = C_QK ** -0.5
    c2 = scale * LOG2E

    def body(q_ref, do_ref, k_ref, v_ref, lse_ref, dl_ref, dq_ref, dk_ref, dv_ref):
        kj = pl.program_id(1)

        @pl.when(kj == 0)
        def _():
            dq_ref[...] = jnp.zeros_like(dq_ref)

        dk_ref[...] = jnp.zeros_like(dk_ref)
        dv_ref[...] = jnp.zeros_like(dv_ref)
        kv, vv = k_ref[...], v_ref[...]

        def step(i, n, diag):
            rows = pl.ds(pl.multiple_of(i * t, t), n * t)
            qv, dov = q_ref[rows, :], do_ref[rows, :]
            st = _dot_nt(kv, qv)
            if diag:
                r = lax.broadcasted_iota(jnp.int32, (t, t), 0)
                c = lax.broadcasted_iota(jnp.int32, (t, t), 1)
                own = jnp.where(r <= c, st[:, :t], NEG)
                st = own if n == 1 else jnp.concatenate([own, st[:, t:]], axis=1)
            lse = jnp.concatenate([lse_ref[0, i + u, 0:1, :] for u in range(n)], axis=1)
            dl = jnp.concatenate([dl_ref[0, i + u, 0:1, :] for u in range(n)], axis=1)
            pt = jnp.exp2(st * c2 - lse)
            dv_ref[...] += _dot(pt.astype(BF), dov)
            dpt = _dot_nt(vv, dov)
            dst = (pt * (dpt - dl) * scale).astype(BF)
            dk_ref[...] += _dot(dst, qv)
            dq_ref[rows, :] += _dot_tn(dst, kv)

        width = min(BWD_WIDE, nt)
        lead = (nt - kj - 1) % width + 1
        for v in range(1, width + 1):
            @pl.when(lead == v)
            def _(v=v):
                step(kj, v, True)

        first = kj + lead

        def run_body(pi, carry):
            step(first + width * pi, width, False)
            return carry

        lax.fori_loop(0, (nt - first) // width, run_body, 0)

    res = pl.BlockSpec((s, LANES), lambda h, j: (0, h))
    kspec = pl.BlockSpec((t, LANES), lambda h, j: (j, h))
    stat = pl.BlockSpec((1, nt, 8, t), lambda h, j: (h, 0, 0, 0))
    return pl.pallas_call(
        body, grid=(C_HEADS, nt), in_specs=[res, res, kspec, kspec, stat, stat],
        out_specs=[res, kspec, kspec],
        out_shape=[jax.ShapeDtypeStruct(q.shape, F32)] * 3,
        compiler_params=_cp("parallel", "arbitrary"), name=name)(q, do, k, v, lse2, delta)


def _conv_shifted(z, zprev, tm):
    row = lax.broadcasted_iota(jnp.int32, z.shape, 0)
    z1 = jnp.where(row == 0, zprev[7:8], pltpu.roll(z, 1, 0))
    z2 = jnp.where(row == 0, zprev[6:7], jnp.where(row == 1, zprev[7:8], pltpu.roll(z, 2, 0)))
    return z1, z2


def _conv_fwd(proj, w, *, name):
    s = proj.shape[0]
    d = D_MODEL
    tm = _tile(s, PREP_TILE, 8)

    def body(x_ref, xp_ref, w_ref, y_ref):
        i = pl.program_id(0)
        z = x_ref[:, d:2 * d] * x_ref[:, 2 * d:]
        zprev = jnp.where(i > 0, xp_ref[:, d:2 * d] * xp_ref[:, 2 * d:], 0.0)
        z1, z2 = _conv_shifted(z, zprev, tm)
        y = w_ref[0:1, :] * z2 + w_ref[1:2, :] * z1 + w_ref[2:3, :] * z
        y_ref[...] = (x_ref[:, :d] * y).astype(BF)

    per8 = tm // 8
    return pl.pallas_call(
        body, grid=(s // tm,),
        in_specs=[pl.BlockSpec((tm, 3 * d), lambda i: (i, 0)),
                  pl.BlockSpec((8, 3 * d), lambda i: (jnp.maximum(i * per8 - 1, 0), 0)),
                  pl.BlockSpec((8, d), lambda i: (0, 0))],
        out_specs=pl.BlockSpec((tm, d), lambda i: (i, 0)),
        out_shape=jax.ShapeDtypeStruct((s, d), BF),
        compiler_params=_cp("parallel"), name=name)(proj, proj, w)


def _conv_bwd(proj, dyb, w, *, name):
    s = proj.shape[0]
    d = D_MODEL
    tm = _tile(s, PREP_TILE, 8)
    nt = s // tm

    def body(x_ref, xp_ref, xn_ref, dy_ref, dyn_ref, w_ref, dx_ref, dw_ref):
        i = pl.program_id(0)

        @pl.when(i == 0)
        def _():
            dw_ref[...] = jnp.zeros_like(dw_ref)

        b, c, u = x_ref[:, :d], x_ref[:, d:2 * d], x_ref[:, 2 * d:]
        z = c * u
        zprev = jnp.where(i > 0, xp_ref[:, d:2 * d] * xp_ref[:, 2 * d:], 0.0)
        z1, z2 = _conv_shifted(z, zprev, tm)
        w0, w1, w2 = w_ref[0:1, :], w_ref[1:2, :], w_ref[2:3, :]
        y = w0 * z2 + w1 * z1 + w2 * z
        dyb_v = dy_ref[...]
        dyc = dyb_v * b
        dyn = jnp.where(i < nt - 1, dyn_ref[...] * xn_ref[:, :d], 0.0)
        row = lax.broadcasted_iota(jnp.int32, dyc.shape, 0)
        d1 = jnp.where(row == tm - 1, dyn[0:1], pltpu.roll(dyc, tm - 1, 0))
        d2 = jnp.where(row == tm - 1, dyn[1:2], jnp.where(row == tm - 2, dyn[0:1], pltpu.roll(dyc, tm - 2, 0)))
        dz = w2 * dyc + w1 * d1 + w0 * d2
        dx_ref[:, :d] = (dyb_v * y).astype(BF)
        dx_ref[:, d:2 * d] = (dz * u).astype(BF)
        dx_ref[:, 2 * d:] = (dz * c).astype(BF)
        dw_ref[0:1, :] += jnp.sum(dyc * z2, axis=0, keepdims=True)
        dw_ref[1:2, :] += jnp.sum(dyc * z1, axis=0, keepdims=True)
        dw_ref[2:3, :] += jnp.sum(dyc * z, axis=0, keepdims=True)

    per8 = tm // 8
    last8 = s // 8 - 1
    dx, dw = pl.pallas_call(
        body, grid=(nt,),
        in_specs=[pl.BlockSpec((tm, 3 * d), lambda i: (i, 0)),
                  pl.BlockSpec((8, 3 * d), lambda i: (jnp.maximum(i * per8 - 1, 0), 0)),
                  pl.BlockSpec((8, 3 * d), lambda i: (jnp.minimum((i + 1) * per8, last8), 0)),
                  pl.BlockSpec((tm, d), lambda i: (i, 0)),
                  pl.BlockSpec((8, d), lambda i: (jnp.minimum((i + 1) * per8, last8), 0)),
                  pl.BlockSpec((8, d), lambda i: (0, 0))],
        out_specs=[pl.BlockSpec((tm, 3 * d), lambda i: (i, 0)), pl.BlockSpec((8, d), lambda i: (0, 0))],
        out_shape=[jax.ShapeDtypeStruct((s, 3 * d), BF), jax.ShapeDtypeStruct((8, d), F32)],
        compiler_params=_cp("arbitrary"), name=name)(proj, proj, proj, dyb, dyb, w)
    return dx, dw[0:3]


def _ffn_up(h, wgu, *, name):
    s, d = h.shape
    f = wgu.shape[1] // 2
    tm = _tile(s, 2 * ROW_TILE, 8)
    tn = _tile(f, 1408, 128)
    nb = f // tn

    def body(h_ref, wg_ref, wu_ref, dg_ref, du_ref, a_ref):
        hv = h_ref[...]
        g = _dot(hv, wg_ref[...])
        u = _dot(hv, wu_ref[...])
        sig = jax.nn.sigmoid(g)
        silu = g * sig
        dg_ref[...] = (u * (sig + silu * (1.0 - sig))).astype(BF)
        du_ref[...] = silu.astype(BF)
        a_ref[...] = (silu * u).astype(BF)

    out = pl.BlockSpec((tm, tn), lambda j, i: (i, j))
    return pl.pallas_call(
        body, grid=(nb, s // tm),
        in_specs=[pl.BlockSpec((tm, d), lambda j, i: (i, 0)),
                  pl.BlockSpec((d, tn), lambda j, i: (0, j)),
                  pl.BlockSpec((d, tn), lambda j, i: (0, j + nb))],
        out_specs=[out] * 3, out_shape=[jax.ShapeDtypeStruct((s, f), BF)] * 3,
        compiler_params=_cp("parallel", "parallel"), name=name)(h, wgu, wgu)


def _ffn_bwd_mid(dy, wd, gate_factor, up_factor, *, name):
    s, d = dy.shape
    f = wd.shape[0]
    tm = _tile(s, ROW_TILE, 8)
    tn = _tile(f, 1408, 128)

    def body(dy_ref, wd_ref, g_ref, u_ref, dg_ref, du_ref):
        dyv = dy_ref[...].astype(BF)
        for c0 in range(0, tn, 2 * LANES):
            cols = slice(c0, min(c0 + 2 * LANES, tn))
            dact = _dot_nt(dyv, wd_ref[cols, :])
            dg_ref[:, cols] = (dact * g_ref[:, cols].astype(F32)).astype(BF)
            du_ref[:, cols] = (dact * u_ref[:, cols].astype(F32)).astype(BF)

    blk = pl.BlockSpec((tm, tn), lambda j, i: (i, j))
    return pl.pallas_call(
        body, grid=(f // tn, s // tm),
        in_specs=[pl.BlockSpec((tm, d), lambda j, i: (i, 0)), pl.BlockSpec((tn, d), lambda j, i: (j, 0)), blk, blk],
        out_specs=[blk, blk], out_shape=[jax.ShapeDtypeStruct((s, f), BF)] * 2,
        compiler_params=_cp("parallel", "parallel"), name=name)(dy, wd, gate_factor, up_factor)


def _pad_cols(w, heads, dim, layout):
    k = w.shape[0]
    w3 = w.reshape(k, heads, dim)
    pieces, lane = [], 0
    for lane0, dim0, cnt in sorted(layout):
        if lane0 > lane:
            pieces.append(jnp.zeros((k, heads, lane0 - lane), w.dtype))
        pieces.append(w3[:, :, dim0:dim0 + cnt])
        lane = lane0 + cnt
    if lane < LANES:
        pieces.append(jnp.zeros((k, heads, LANES - lane), w.dtype))
    return jnp.concatenate(pieces, axis=2).reshape(k, heads * LANES)


def _unpad_cols(w, heads, dim, layout):
    k = w.shape[0]
    w3 = w.reshape(k, heads, LANES)
    by_dim = sorted(layout, key=lambda seg: seg[1])
    assert sum(cnt for _, _, cnt in by_dim) == dim
    return jnp.concatenate([w3[:, :, lane0:lane0 + cnt] for lane0, _, cnt in by_dim], axis=2).reshape(k, heads * dim)


def _pad_rows(w, heads, dim):
    n = w.shape[1]
    return jnp.pad(w.reshape(heads, dim, n), ((0, 0), (0, LANES - dim), (0, 0))).reshape(heads * LANES, n)


def _unpad_rows(w, heads, dim):
    n = w.shape[1]
    return w.reshape(heads, LANES, n)[:, :dim, :].reshape(heads * dim, n)


def _pad_vec(g, layout):
    return _pad_cols(g.reshape(1, -1), 1, g.shape[0], layout)


def _unpad_vec(g, dim, layout):
    return _unpad_cols(g.reshape(1, LANES), 1, dim, layout)[0]


def _local_step(x, positions, w, target):
    s = x.shape[0]
    cos_a, sin_a, cos_c, sin_c = _rope_tables(positions.reshape(s, 1), name="rope_tables")
    grads = {n: [None] * len(w[n]) for n in WEIGHTS}
    saved = []

    for i in range(DEPTH):
        kind, j = i % N_MIXERS, i // N_MIXERS
        tag = f"l{i}"
        if i == 0:
            h = _rmsnorm_fwd(x, w['mix_norm'][i], name=f"{tag}_mix_norm")
        fnorm = w['ffn_norm'][i]
        if kind == 0:
            nqk = (A_HEADS + A_KV_HEADS) * A_HEAD_DIM
            wqkv = jnp.concatenate(
                [_pad_cols(w['a_w_qkv'][j][:, :nqk], A_HEADS + A_KV_HEADS, A_HEAD_DIM, LAYOUT_A),
                 _pad_cols(w['a_w_qkv'][j][:, nqk:], A_KV_HEADS, A_HEAD_DIM, LAYOUT_V)], axis=1)
            wo = _pad_rows(w['a_w_o'][j], A_HEADS, A_HEAD_DIM)
            gq, gk = _pad_vec(w['a_q_norm'][j], LAYOUT_A), _pad_vec(w['a_k_norm'][j], LAYOUT_A)
            qkv = _mm([(h, wqkv, 0)], out_dtype=F32, name=f"{tag}_qkv")
            qa, ka, va = _prep_a_fwd(qkv, cos_a, sin_a, gq, gk, name=f"{tag}_prep")
            o, lse = _swa_fwd(qa, ka, va, w['a_sinks'][j], name=f"{tag}_attn")
            x1, h2 = _mm([(o, wo, 0)], out_dtype=F32, res=x, norm_g=fnorm, name=f"{tag}_wo")
            mix = dict(wqkv=wqkv, wo=wo, gq=gq, gk=gk, qkv=qkv, qa=qa, ka=ka, va=va, o=o, lse=lse)
        elif kind == 1:
            cw = jnp.pad(w['b_conv_w'][j], ((0, 5), (0, 0)))
            proj = _mm([(h, w['b_w_in'][j], 0)], out_dtype=F32, name=f"{tag}_win")
            yb = _conv_fwd(proj, cw, name=f"{tag}_conv")
            x1, h2 = _mm([(yb, w['b_w_out'][j], 0)], out_dtype=F32, res=x, norm_g=fnorm, name=f"{tag}_wout")
            mix = dict(cw=cw, proj=proj, yb=yb)
        else:
            wdn = w['c_w_down'][j]
            nqk = C_Q_RANK + C_KV_RANK
            wdn = jnp.concatenate([wdn[:, :nqk], _pad_cols(wdn[:, nqk:], 1, C_ROPE, LAYOUT_KR)], axis=1)
            wq = _pad_cols(w['c_w_q_up'][j], C_HEADS, C_QK, LAYOUT_C)
            wkv = w['c_w_kv_up'][j].reshape(C_KV_RANK, C_HEADS, C_NOPE + C_V)
            wkn = _pad_cols(wkv[:, :, :C_NOPE].reshape(C_KV_RANK, -1), C_HEADS, C_NOPE, LAYOUT_KN)
            wv = _pad_cols(wkv[:, :, C_NOPE:].reshape(C_KV_RANK, -1), C_HEADS, C_V, LAYOUT_V)
            wo = _pad_rows(w['c_w_o'][j], C_HEADS, C_V)
            gq, gk = _pad_vec(w['c_q_norm'][j], LAYOUT_C), _pad_vec(w['c_k_norm'][j], LAYOUT_C)
            gqa, gkva = w['c_q_a_norm'][j].reshape(1, -1), w['c_kv_a_norm'][j].reshape(1, -1)
            dn = _mm([(h, wdn, 0)], out_dtype=F32, name=f"{tag}_wdown")
            cqn, ckvn = _prep_c1_fwd(dn, gqa, gkva, name=f"{tag}_prep1")
            qraw = _mm([(cqn, wq, 0)], out_dtype=F32, name=f"{tag}_wq")
            knope = _mm([(ckvn, wkn, 0)], out_dtype=F32, name=f"{tag}_wkn")
            vc = _mm([(ckvn, wv, 0)], out_dtype=BF, name=f"{tag}_wv")
            qc, kc = _prep_c2_fwd(qraw, knope, dn, cos_c, sin_c, gq, gk, name=f"{tag}_prep2")
            o, lse = _mla_fwd(qc, kc, vc, name=f"{tag}_attn")
            x1, h2 = _mm([(o, wo, 0)], out_dtype=F32, res=x, norm_g=fnorm, name=f"{tag}_wo")
            mix = dict(wdn=wdn, wq=wq, wkn=wkn, wv=wv, wo=wo, gq=gq, gk=gk, gqa=gqa, gkva=gkva, dn=dn, cqn=cqn,
                       ckvn=ckvn, qraw=qraw, knope=knope, vc=vc, qc=qc, kc=kc, o=o, lse=lse)
        gate_factor, up_factor, act = _ffn_up(h2, w['f_w_gate_up'][i], name=f"{tag}_ffn_up")
        saved.append(dict(x=x, h=h, x1=x1, h2=h2, gate_factor=gate_factor, up_factor=up_factor, act=act, mix=mix))
        if i + 1 < DEPTH:
            x, h = _mm([(act, w['f_w_down'][i], 0)], out_dtype=F32, res=x1, norm_g=w['mix_norm'][i + 1],
                       name=f"{tag}_ffn_down")
        else:
            loss_blk, dx = _mm_loss(act, w['f_w_down'][i], x1, target, name=f"{tag}_ffn_down_loss")

    for i in reversed(range(DEPTH)):
        kind, j = i % N_MIXERS, i // N_MIXERS
        tag = f"l{i}b"
        sv = saved[i]
        mix = sv['mix']
        wgu, wd = w['f_w_gate_up'][i], w['f_w_down'][i]
        grads['f_w_down'][i] = _mm_tn(sv['act'], dx, name=f"{tag}_dwd")
        dgate, dup = _ffn_bwd_mid(dx, wd, sv['gate_factor'], sv['up_factor'], name=f"{tag}_ffn_mid")
        grads['f_w_gate_up'][i] = (_mm_tn(sv['h2'], dgate, name=f"{tag}_dwg"),
                                   _mm_tn(sv['h2'], dup, name=f"{tag}_dwu"))
        dx, dg = _mm_norm_bwd([(dgate, wgu, 0), (dup, wgu, 1)], sv['x1'], w['ffn_norm'][i], dx, name=f"{tag}_dh2")
        grads['ffn_norm'][i] = dg[0]
        if kind == 0:
            grads['a_w_o'][j] = _unpad_rows(_mm_tn(mix['o'], dx, name=f"{tag}_dwo"), A_HEADS, A_HEAD_DIM)
            do = _mm([(dx, mix['wo'], 0)], out_dtype=BF, trans_b=True, name=f"{tag}_do")
            dqa, dka, dva, dsinks = _swa_bwd(mix['qa'], mix['ka'], mix['va'], mix['o'], do, mix['lse'],
                                             w['a_sinks'][j], name=f"{tag}_attn")
            dqkv, dgq, dgk = _prep_a_bwd(mix['qkv'], dqa, dka, dva, cos_a, sin_a, mix['gq'], mix['gk'],
                                         name=f"{tag}_prep")
            grads['a_sinks'][j] = dsinks
            grads['a_q_norm'][j] = _unpad_vec(dgq, A_HEAD_DIM, LAYOUT_A)
            grads['a_k_norm'][j] = _unpad_vec(dgk, A_HEAD_DIM, LAYOUT_A)
            dwqkv = _mm_tn(sv['h'], dqkv, name=f"{tag}_dwqkv")
            nqk = (A_HEADS + A_KV_HEADS) * LANES
            grads['a_w_qkv'][j] = jnp.concatenate(
                [_unpad_cols(dwqkv[:, :nqk], A_HEADS + A_KV_HEADS, A_HEAD_DIM, LAYOUT_A),
                 _unpad_cols(dwqkv[:, nqk:], A_KV_HEADS, A_HEAD_DIM, LAYOUT_V)], axis=1)
            dh_pairs = [(dqkv, mix['wqkv'], 0)]
        elif kind == 1:
            grads['b_w_out'][j] = _mm_tn(mix['yb'], dx, name=f"{tag}_dwout")
            dyb = _mm([(dx, w['b_w_out'][j], 0)], out_dtype=F32, trans_b=True, name=f"{tag}_dyb")
            dproj, dcw = _conv_bwd(mix['proj'], dyb, mix['cw'], name=f"{tag}_conv")
            grads['b_conv_w'][j] = dcw
            grads['b_w_in'][j] = _mm_tn(sv['h'], dproj, name=f"{tag}_dwin")
            dh_pairs = [(dproj, w['b_w_in'][j], 0)]
        else:
            grads['c_w_o'][j] = _unpad_rows(_mm_tn(mix['o'], dx, name=f"{tag}_dwo"), C_HEADS, C_V)
            do = _mm([(dx, mix['wo'], 0)], out_dtype=BF, trans_b=True, name=f"{tag}_do")
            delta = _mla_delta(mix['o'], do, name=f"{tag}_delta")
            dqc, dkc, dvc = _mla_bwd(mix['qc'], mix['kc'], mix['vc'], do, mix['lse'], delta, name=f"{tag}_attn")
            dqraw, dknope, dkr, dgq, dgk = _prep_c2_bwd(mix['qraw'], mix['knope'], mix['dn'], dqc, dkc, cos_c, sin_c,
                                                        mix['gq'], mix['gk'], name=f"{tag}_prep2")
            grads['c_q_norm'][j] = _unpad_vec(dgq, C_QK, LAYOUT_C)
            grads['c_k_norm'][j] = _unpad_vec(dgk, C_QK, LAYOUT_C)
            grads['c_w_q_up'][j] = _unpad_cols(_mm_tn(mix['cqn'], dqraw, name=f"{tag}_dwq"), C_HEADS, C_QK,
                                               LAYOUT_C)
            dwkn = _unpad_cols(_mm_tn(mix['ckvn'], dknope, name=f"{tag}_dwkn"), C_HEADS, C_NOPE, LAYOUT_KN)
            dwv = _unpad_cols(_mm_tn(mix['ckvn'], dvc, name=f"{tag}_dwv"), C_HEADS, C_V, LAYOUT_V)
            grads['c_w_kv_up'][j] = jnp.concatenate(
                [dwkn.reshape(C_KV_RANK, C_HEADS, C_NOPE), dwv.reshape(C_KV_RANK, C_HEADS, C_V)], axis=2).reshape(
                C_KV_RANK, -1)
            dcq = _mm([(dqraw, mix['wq'], 0)], out_dtype=F32, trans_b=True, name=f"{tag}_dcq")
            dckv = _mm([(dknope, mix['wkn'], 0), (dvc, mix['wv'], 0)], out_dtype=F32, trans_b=True,
                       name=f"{tag}_dckv")
            ddn, dgqa, dgkva = _prep_c1_bwd(mix['dn'], dcq, dckv, dkr, mix['gqa'], mix['gkva'], name=f"{tag}_prep1")
            grads['c_q_a_norm'][j] = dgqa[0]
            grads['c_kv_a_norm'][j] = dgkva[0]
            dwdn = _mm_tn(sv['h'], ddn, name=f"{tag}_dwdown")
            nqk = C_Q_RANK + C_KV_RANK
            grads['c_w_down'][j] = jnp.concatenate(
                [dwdn[:, :nqk], _unpad_cols(dwdn[:, nqk:], 1, C_ROPE, LAYOUT_KR)], axis=1)
            dh_pairs = [(ddn, mix['wdn'], 0)]
        dx, dg = _mm_norm_bwd(dh_pairs, sv['x'], w['mix_norm'][i], dx, name=f"{tag}_dh")
        grads['mix_norm'][i] = dg[0]

    return loss_blk, dx, {n: (g if n in BIG else jnp.stack(g)) for n, g in grads.items()}


def _my_place():
    return lax.axis_index("x"), lax.axis_index("y"), lax.axis_index("c")


def _flips(x, y):
    return [(1 - x, y), (x, 1 - y), (1 - x, 1 - y)]


def _gather_weights(big, small):
    half = big.shape[0] // 2
    chunk = half // GATHER_CHUNKS

    def body(big_ref, small_ref, bout_ref, sout_ref, send_b, recv_b, send_f, recv_f, send_s, recv_s, loc):
        x, y, c = _my_place()
        me = 2 * x + y

        def rows(core, ch):
            return pl.ds(pl.multiple_of(core * half + ch * chunk, 16), chunk)

        def ici(k, ch, chip, to):
            return pltpu.make_async_remote_copy(
                src_ref=big_ref.at[rows(c, ch), :], dst_ref=bout_ref.at[chip, rows(c, ch), :],
                send_sem=send_b.at[k * GATHER_CHUNKS + ch], recv_sem=recv_b.at[k * GATHER_CHUNKS + ch],
                device_id=to, device_id_type=MESH)

        def forward(k, ch, chip, core):
            return pltpu.make_async_remote_copy(
                src_ref=bout_ref.at[chip, rows(core, ch), :], dst_ref=bout_ref.at[chip, rows(core, ch), :],
                send_sem=send_f.at[k * GATHER_CHUNKS + ch], recv_sem=recv_f.at[k * GATHER_CHUNKS + ch],
                device_id=(x, y, 1 - c), device_id_type=MESH)

        def small_copy(k, chip, to):
            return pltpu.make_async_remote_copy(src_ref=small_ref, dst_ref=sout_ref.at[chip],
                                                send_sem=send_s.at[k], recv_sem=recv_s.at[k],
                                                device_id=to, device_id_type=MESH)

        chips = _flips(x, y)
        every = [(ch, k, px, py) for ch in range(GATHER_CHUNKS) for k, (px, py) in enumerate(chips)]
        sends = [ici(k, ch, me, (px, py, c)) for ch, k, px, py in every]
        sends += [small_copy(k, me, (px, py, c)) for k, (px, py) in enumerate(chips)]
        for cp in sends:
            cp.start()
        own = (pltpu.make_async_copy(big_ref, bout_ref.at[me], loc.at[0]),
               pltpu.make_async_copy(small_ref, sout_ref.at[me], loc.at[1]))
        for cp in own:
            cp.start()
        passed = []
        for ch, k, px, py in every:
            ici(k, ch, 2 * px + py, (x, y, c)).wait_recv()
            passed.append(forward(k, ch, 2 * px + py, c))
            passed[-1].start()
        for ch, k, px, py in every:
            forward(k, ch, 2 * px + py, 1 - c).wait_recv()
        for k, (px, py) in enumerate(chips):
            small_copy(k, 2 * px + py, (x, y, c)).wait_recv()
        for cp in sends + passed:
            cp.wait_send()
        for cp in own:
            cp.wait()

    hbm = pl.BlockSpec(memory_space=pltpu.HBM)
    sem3 = pltpu.SemaphoreType.DMA((3,))
    semc = pltpu.SemaphoreType.DMA((3 * GATHER_CHUNKS,))
    return pl.pallas_call(
        body, in_specs=[hbm, hbm], out_specs=[hbm, hbm],
        out_shape=[jax.ShapeDtypeStruct((N_CHIPS,) + big.shape, big.dtype),
                   jax.ShapeDtypeStruct((N_CHIPS,) + small.shape, small.dtype)],
        scratch_shapes=[semc, semc, semc, semc, sem3, sem3, pltpu.SemaphoreType.DMA((2,))],
        name="gather_weights")(big, small)


def _exchange_grads(gbig, gsmall):
    def body(gbig_ref, gsmall_ref, got_ref, ssum_ref, sbuf, send_b, recv_b, send_s, recv_s):
        x, y, c = _my_place()
        me = 4 * x + 2 * y + c

        def big_copy(k, src_chip, to):
            return pltpu.make_async_remote_copy(src_ref=gbig_ref.at[src_chip], dst_ref=got_ref.at[k],
                                                send_sem=send_b.at[k], recv_sem=recv_b.at[k],
                                                device_id=to, device_id_type=MESH)

        def small_copy(k, slot, to):
            return pltpu.make_async_remote_copy(src_ref=gsmall_ref, dst_ref=sbuf.at[slot],
                                                send_sem=send_s.at[k], recv_sem=recv_s.at[k],
                                                device_id=to, device_id_type=MESH)

        peers = [(x ^ (k >> 2), y ^ ((k >> 1) & 1), c ^ (k & 1)) for k in range(1, N_DEV)]
        bigs = [big_copy(k, 2 * px + py, (px, py, c)) for k, (px, py) in enumerate(_flips(x, y))]
        smalls = [small_copy(k, me, p) for k, p in enumerate(peers)]
        for cp in bigs + smalls:
            cp.start()
        sbuf[me] = gsmall_ref[...]
        for k, (px, py, pc) in enumerate(peers):
            small_copy(k, 4 * px + 2 * py + pc, (x, y, c)).wait_recv()
        acc = sbuf[0]
        for d in range(1, N_DEV):
            acc = acc + sbuf[d]
        ssum_ref[...] = acc
        for k in range(3):
            big_copy(k, 0, (x, y, c)).wait_recv()
        for cp in bigs + smalls:
            cp.wait_send()

    hbm = pl.BlockSpec(memory_space=pltpu.HBM)
    vmem = pl.BlockSpec(memory_space=pltpu.VMEM)
    return pl.pallas_call(
        body, in_specs=[hbm, vmem], out_specs=[hbm, vmem],
        out_shape=[jax.ShapeDtypeStruct((3,) + gbig.shape[1:], gbig.dtype),
                   jax.ShapeDtypeStruct(gsmall.shape, gsmall.dtype)],
        scratch_shapes=[pltpu.VMEM((N_DEV,) + gsmall.shape, gsmall.dtype),
                        pltpu.SemaphoreType.DMA((3,)), pltpu.SemaphoreType.DMA((3,)),
                        pltpu.SemaphoreType.DMA((N_DEV - 1,)), pltpu.SemaphoreType.DMA((N_DEV - 1,))],
        name="exchange_grads")(gbig, gsmall)


def _sum_shards(chip, mine, got):
    r, cols = mine.shape[1:]
    tr = _tile(r, PACK_TILE_ROWS, 16)

    def body(chip_ref, mine_ref, got_ref, o_ref):
        o_ref[...] = ((mine_ref[0] + got_ref[0].astype(F32)) + got_ref[1].astype(F32)) + got_ref[2].astype(F32)

    grid_spec = pltpu.PrefetchScalarGridSpec(
        num_scalar_prefetch=1, grid=(r // tr,),
        in_specs=[pl.BlockSpec((1, tr, cols), lambda i, chip_ref: (chip_ref[0], i, 0)),
                  pl.BlockSpec((3, tr, cols), lambda i, chip_ref: (0, i, 0))],
        out_specs=pl.BlockSpec((tr, cols), lambda i, chip_ref: (i, 0)))
    return pl.pallas_call(
        body, grid_spec=grid_spec, out_shape=jax.ShapeDtypeStruct((r, cols), F32),
        compiler_params=_cp("parallel"), name="sum_shards")(chip, mine, got)


def _swap_sibling(t, *, name):
    def body(t_ref, got_ref, send_sem, recv_sem):
        x, y, c = _my_place()
        cp = pltpu.make_async_remote_copy(src_ref=t_ref, dst_ref=got_ref, send_sem=send_sem, recv_sem=recv_sem,
                                          device_id=(x, y, 1 - c), device_id_type=MESH)
        cp.start()
        cp.wait()

    hbm = pl.BlockSpec(memory_space=pltpu.HBM)
    return pl.pallas_call(
        body, in_specs=[hbm], out_specs=hbm, out_shape=jax.ShapeDtypeStruct(t.shape, t.dtype),
        scratch_shapes=[pltpu.SemaphoreType.DMA, pltpu.SemaphoreType.DMA], name=name)(t)


def _give_halves(parts, *, name):
    n = len(parts)
    r, cols = parts[0].shape
    half = r // 2

    def body(*refs):
        got_ref, send_sems, recv_sems = refs[n:]
        x, y, c = _my_place()
        theirs = pl.ds(pl.multiple_of((1 - c) * half, 16), half)
        copies = [pltpu.make_async_remote_copy(src_ref=refs[k].at[theirs, :], dst_ref=got_ref.at[k],
                                               send_sem=send_sems.at[k], recv_sem=recv_sems.at[k],
                                               device_id=(x, y, 1 - c), device_id_type=MESH) for k in range(n)]
        for cp in copies:
            cp.start()
        for cp in copies:
            cp.wait()

    hbm = pl.BlockSpec(memory_space=pltpu.HBM)
    return pl.pallas_call(
        body, in_specs=[hbm] * n, out_specs=hbm, out_shape=jax.ShapeDtypeStruct((n, half, cols), parts[0].dtype),
        scratch_shapes=[pltpu.SemaphoreType.DMA((n,)), pltpu.SemaphoreType.DMA((n,))], name=name)(*parts)


def _add_kept(core, parts, got, *, name):
    n, r, cols = got.shape
    tr = _tile(r, PACK_TILE_ROWS, 16)
    nblk = r // tr

    def body(core_ref, *refs):
        b_ref, o_ref, ob_ref = refs[n:]
        for k in range(n):
            sm = refs[k][...] + b_ref[k]
            o_ref[k] = sm
            ob_ref[k] = sm.astype(BF)

    blk = pl.BlockSpec((n, tr, cols), lambda i, core_ref: (0, i, 0))
    kept = pl.BlockSpec((tr, cols), lambda i, core_ref: (core_ref[0] * nblk + i, 0))
    grid_spec = pltpu.PrefetchScalarGridSpec(
        num_scalar_prefetch=1, grid=(nblk,), in_specs=[kept] * n + [blk], out_specs=[blk, blk])
    return pl.pallas_call(
        body, grid_spec=grid_spec,
        out_shape=[jax.ShapeDtypeStruct(got.shape, F32), jax.ShapeDtypeStruct(got.shape, BF)],
        compiler_params=_cp("parallel"), name=name)(core, *parts, got)


def _by_core(c, mine, sibling):
    return jnp.where(c == 0, mine, sibling), jnp.where(c == 0, sibling, mine)


def _adamw(wt, m, v, g, *, name):
    r, cols = wt.shape
    tr = _tile(r, 256, 8)

    def body(w_ref, m_ref, v_ref, g_ref, d_ref, nm_ref, nv_ref):
        gv = g_ref[...]
        nm = ADAM_B1 * m_ref[...] + (1.0 - ADAM_B1) * gv
        nv = ADAM_B2 * v_ref[...] + (1.0 - ADAM_B2) * (gv * gv)
        m_hat = nm / (1.0 - ADAM_B1 ** ADAM_STEP)
        v_hat = nv / (1.0 - ADAM_B2 ** ADAM_STEP)
        d_ref[...] = -ADAM_LR * (m_hat / (jnp.sqrt(v_hat) + ADAM_EPS) + ADAM_WD * w_ref[...])
        nm_ref[...] = nm
        nv_ref[...] = nv

    blk = pl.BlockSpec((tr, cols), lambda i: (i, 0))
    return pl.pallas_call(
        body, grid=(r // tr,), in_specs=[blk] * 4, out_specs=[blk] * 3,
        out_shape=[jax.ShapeDtypeStruct((r, cols), F32)] * 3,
        compiler_params=_cp("parallel"), name=name)(wt, m, v, g)


def _shard_shape(full, axis):
    return tuple(d // N_CHIPS if a == axis else d for a, d in enumerate(full))


def _pack_rows(n):
    rows = -(-n // PACK_COLS)
    return -(-rows // PACK_ROW_MULT) * PACK_ROW_MULT


def _pack(parts, rows, dtype):
    mat = jnp.concatenate([p.astype(dtype).reshape(-1, PACK_COLS) for p in parts], axis=0)
    return jnp.pad(mat, ((0, rows - mat.shape[0]), (0, 0)))


def _join_shards(stacked, axis):
    moved = jnp.moveaxis(stacked, 0, axis)
    shp = moved.shape
    return moved.reshape(shp[:axis] + (shp[axis] * shp[axis + 1],) + shp[axis + 2:])


def _split_shards(full, axis):
    shp = full.shape
    split = full.reshape(shp[:axis] + (N_CHIPS, shp[axis] // N_CHIPS) + shp[axis + 1:])
    return jnp.moveaxis(split, axis, 0)


SMALL_ROWS = 128
SMALL_LAYOUT = [('mix_norm', DEPTH * D_MODEL), ('ffn_norm', DEPTH * D_MODEL), ('b_conv_w', 3 * D_MODEL),
                ('c_q_a_norm', C_Q_RANK), ('c_kv_a_norm', C_KV_RANK), ('a_q_norm', 2 * A_HEAD_DIM),
                ('a_k_norm', 2 * A_HEAD_DIM), ('a_sinks', 2 * A_HEADS), ('c_q_norm', C_QK), ('c_k_norm', C_QK),
                ('loss', 1)]


def _small_offsets():
    offs, row = {}, 0
    for name, n in SMALL_LAYOUT:
        offs[name] = (row * LANES, n)
        row += -(-n // LANES)
    assert row <= SMALL_ROWS
    return offs


def kernel(x, positions, mix_norm, ffn_norm, a_w_qkv, a_q_norm, a_k_norm, a_sinks, a_w_o, b_w_in, b_conv_w, b_w_out, c_w_down, c_q_a_norm, c_kv_a_norm, c_w_q_up, c_w_kv_up, c_q_norm, c_k_norm, c_w_o, f_w_gate_up, f_w_down, loss_target, m_mix_norm, m_ffn_norm, m_a_w_qkv, m_a_q_norm, m_a_k_norm, m_a_sinks, m_a_w_o, m_b_w_in, m_b_conv_w, m_b_w_out, m_c_w_down, m_c_q_a_norm, m_c_kv_a_norm, m_c_w_q_up, m_c_w_kv_up, m_c_q_norm, m_c_k_norm, m_c_w_o, m_f_w_gate_up, m_f_w_down, v_mix_norm, v_ffn_norm, v_a_w_qkv, v_a_q_norm, v_a_k_norm, v_a_sinks, v_a_w_o, v_b_w_in, v_b_conv_w, v_b_w_out, v_c_w_down, v_c_q_a_norm, v_c_kv_a_norm, v_c_w_q_up, v_c_w_kv_up, v_c_q_norm, v_c_k_norm, v_c_w_o, v_f_w_gate_up, v_f_w_down):
    args = dict(locals())
    wshard = {n: args[n] for n in WEIGHTS}
    sharded = {**BIG, **SMALL_SHARDED}
    chip = 2 * lax.axis_index("x") + lax.axis_index("y")

    n_big = sum(wshard[n].size for n in BIG)
    rows = _pack_rows(n_big)
    big = _pack([wshard[n] for n in BIG], rows, BF)
    small = jnp.concatenate([wshard[n].reshape(-1) for n in SMALL_SHARDED])
    small = jnp.pad(small, (0, 8 * LANES - small.shape[0])).reshape(8, LANES)
    core = lax.axis_index("c")
    big_all, small_all = _gather_weights(big, small)
    big_all = big_all.reshape(N_CHIPS, -1)
    small_all = small_all.reshape(N_CHIPS, -1)
    w = {}
    off = 0
    for n, axis in BIG.items():
        shp = wshard[n].shape
        per = shp[1] * shp[2]
        w[n] = [_join_shards(big_all[:, off + l * per:off + (l + 1) * per].reshape((N_CHIPS,) + shp[1:]), axis - 1)
                for l in range(shp[0])]
        off += wshard[n].size
    off = 0
    for n, axis in SMALL_SHARDED.items():
        sz = wshard[n].size
        w[n] = _join_shards(small_all[:, off:off + sz].reshape((N_CHIPS,) + wshard[n].shape), axis)
        off += sz
    for n in WEIGHTS:
        if n not in sharded:
            w[n] = wshard[n]

    loss_blk, grad_x, g = _local_step(x[0], positions[0], w, loss_target[0])

    def shard_of(n, item, k):
        if isinstance(item, tuple):
            item = item[k // 2]
            k = k % 2
            width = item.shape[1] // 2
        else:
            width = item.shape[BIG[n] - 1] // N_CHIPS
        return item[:, k * width:(k + 1) * width] if BIG[n] == 2 else item[k * width:(k + 1) * width]

    gparts = [_pack([shard_of(n, item, k) for n in BIG for item in g[n]], rows, F32) for k in range(N_CHIPS)]
    offs = _small_offsets()
    smalls = {**{n: g[n] for n, _ in SMALL_LAYOUT if n != 'loss'}, 'loss': loss_blk[0:1, 0:1]}
    gsmall = jnp.concatenate(
        [jnp.pad(smalls[n].reshape(-1), (0, -cnt % LANES)) for n, cnt in SMALL_LAYOUT])
    gsmall = jnp.pad(gsmall, (0, SMALL_ROWS * LANES - gsmall.shape[0])).reshape(SMALL_ROWS, LANES)
    from_sibling = _give_halves(gparts, name="swap_grads")
    chip_sum, chip_sum_bf = _add_kept(core.reshape(1).astype(jnp.int32), gparts, from_sibling, name="add_sibling")
    got, ssum = _exchange_grads(chip_sum_bf, gsmall)
    part = _sum_shards(chip.reshape(1).astype(jnp.int32), chip_sum, got)
    part_sib = _swap_sibling(part, name="swap_sums")
    gflat = jnp.concatenate(_by_core(core, part, part_sib), axis=0).reshape(-1)

    ssum = ssum.reshape(-1)
    outs = {}
    off = 0
    for n in WEIGHTS:
        wt = wshard[n]
        if n in BIG:
            grad = gflat[off:off + wt.size]
            off += wt.size
        else:
            o0, cnt = offs[n]
            grad = ssum[o0:o0 + cnt].reshape(g[n].shape)
            if n in SMALL_SHARDED:
                grad = lax.dynamic_index_in_dim(_split_shards(grad, SMALL_SHARDED[n]), chip, 0, keepdims=False)
        shape2 = (-1, wt.shape[-1])
        grad = grad.reshape(shape2)
        res = _adamw(wt.reshape(shape2), args['m_' + n].reshape(shape2), args['v_' + n].reshape(shape2), grad,
                     name=f"adamw_{n}")
        outs[n] = [r.reshape(wt.shape) for r in [grad] + list(res)]
    loss = ssum[offs['loss'][0]]
    return (loss, grad_x[None], *[outs[n][0] for n in WEIGHTS], *[outs[n][1] for n in WEIGHTS],
            *[outs[n][2] for n in WEIGHTS], *[outs[n][3] for n in WEIGHTS])
```
